```python
import jax, jax.numpy as jnp
from jax import lax
import numpy as np

D_MODEL = 2048
BATCH = 8
SEQ = 2048
DEPTH = 4

NORM_EPS = 1e-5
ROPE_THETA = 500000.0
ROPE_FRACTION = 4
BAND_BLOCK = 128

A_DIM = 128
A_HEADS = D_MODEL // 2 // A_DIM
A_BRANCHES = ((128, 1), (512, 4), (2048, 16))
A_WIDTH = A_HEADS * A_DIM
B_KDIM = 128
B_VDIM = 128
B_HEADS = D_MODEL // 2 // B_VDIM
B_KWIDTH = B_HEADS * B_KDIM
B_WIDTH = B_HEADS * B_VDIM
B_CHUNK = 64
EVEN_SPLITS = (A_WIDTH, A_WIDTH, A_WIDTH, B_KWIDTH, B_KWIDTH, B_WIDTH, B_WIDTH)
EVEN_IN = sum(EVEN_SPLITS)
EVEN_MIX = A_WIDTH + B_WIDTH

C_DIM = 64
C_Q_HEADS = D_MODEL // C_DIM
C_KV_HEADS = C_Q_HEADS // 8
C_GROUP = C_Q_HEADS // C_KV_HEADS
C_WINDOW = 128
C_QKV = (C_Q_HEADS + 2 * C_KV_HEADS) * C_DIM

D_FF = 4 * D_MODEL
N_EVEN = (DEPTH + 1) // 2
N_ODD = DEPTH // 2

kernel_name = "hybrid_dilated_hgrn2_swa_sink_trunk"


def rmsnorm(x, g):
    xf = x.astype(jnp.float32)
    y = xf * lax.rsqrt(jnp.mean(jnp.square(xf), axis=-1, keepdims=True) + NORM_EPS)
    return (y * g.astype(jnp.float32)).astype(x.dtype)


def rope_tables(seq, head_dim):
    rot = head_dim // ROPE_FRACTION
    inv_freq = 1.0 / (ROPE_THETA ** (jnp.arange(0, rot, 2, dtype=jnp.float32) / rot))
    ang = jnp.arange(seq, dtype=jnp.float32)[:, None] * inv_freq[None, :]
    return jnp.cos(ang), jnp.sin(ang)


def apply_partial_rope(x, cos, sin):
    half = cos.shape[-1]
    xf = x.astype(jnp.float32)
    x1, x2, rest = xf[..., :half], xf[..., half:2 * half], xf[..., 2 * half:]
    out = jnp.concatenate([x1 * cos - x2 * sin, x2 * cos + x1 * sin, rest], axis=-1)
    return out.astype(x.dtype)


def banded_attention(q, k, v, max_dist, sink=None):
    L, D = q.shape[-2], q.shape[-1]
    nb = -(-L // BAND_BLOCK)
    pad_r = nb * BAND_BLOCK - L

    def pad_seq(t, left):
        return jnp.pad(t, [(0, 0)] * (t.ndim - 2) + [(left, pad_r), (0, 0)])

    qb = pad_seq(q, 0).reshape(q.shape[:-2] + (nb, BAND_BLOCK, D))
    kp = pad_seq(k, BAND_BLOCK).reshape(k.shape[:-2] + (nb + 1, BAND_BLOCK, D))
    vp = pad_seq(v, BAND_BLOCK).reshape(v.shape[:-2] + (nb + 1, BAND_BLOCK, D))
    kb = jnp.concatenate([kp[..., :-1, :, :], kp[..., 1:, :, :]], axis=-2)
    vb = jnp.concatenate([vp[..., :-1, :, :], vp[..., 1:, :, :]], axis=-2)
    s = jnp.einsum('...gnid,...njd->...gnij', qb, kb,
                   preferred_element_type=jnp.float32) * (D ** -0.5)
    qi = jnp.arange(BAND_BLOCK)[:, None]
    kj = jnp.arange(2 * BAND_BLOCK)[None, :]
    dist = qi + BAND_BLOCK - kj
    kpos = jnp.arange(nb)[:, None, None] * BAND_BLOCK - BAND_BLOCK + kj
    mask = (dist >= 0) & (dist <= max_dist) & (kpos >= 0)
    s = jnp.where(mask, s, -jnp.inf)
    m = jnp.max(s, axis=-1)
    if sink is not None:
        sink = sink.astype(jnp.float32)[..., None, None]
        m = jnp.maximum(m, sink)
    p = jnp.exp(s - m[..., None])
    l = jnp.sum(p, axis=-1)
    if sink is not None:
        l = l + jnp.exp(sink - m)
    num = jnp.einsum('...gnij,...njd->...gnid', p.astype(v.dtype), vb,
                     preferred_element_type=jnp.float32)
    num = num.reshape(num.shape[:-3] + (nb * BAND_BLOCK, D))[..., :L, :]
    m = m.reshape(m.shape[:-2] + (nb * BAND_BLOCK,))[..., :L]
    l = l.reshape(l.shape[:-2] + (nb * BAND_BLOCK,))[..., :L]
    return num, m, l


def dilated_attention(q, k, v):
    B, H, S, Dh = q.shape
    nums, ms, ls = [], [], []
    for window, dil in A_BRANCHES:
        L = S // dil

        def to_res(t):
            return jnp.swapaxes(t.reshape(B, H, L, dil, Dh), 2, 3)

        num, m, l = banded_attention(to_res(q)[..., None, :, :], to_res(k), to_res(v),
                                     window // dil)
        nums.append(jnp.swapaxes(num[..., 0, :, :], 2, 3).reshape(B, H, S, Dh))
        ms.append(jnp.swapaxes(m[..., 0, :], 2, 3).reshape(B, H, S))
        ls.append(jnp.swapaxes(l[..., 0, :], 2, 3).reshape(B, H, S))
    m_all = jnp.stack(ms)
    w = jnp.exp(m_all - jnp.max(m_all, axis=0, keepdims=True))
    num = jnp.sum(w[..., None] * jnp.stack(nums), axis=0)
    den = jnp.sum(w * jnp.stack(ls), axis=0)
    return num / den[..., None]


def hgrn2_chunkwise(q, k, v, log_f):
    B, H, S, K = q.shape
    V = v.shape[-1]
    n = S // B_CHUNK

    def chunks(t):
        return jnp.moveaxis(t.reshape(B, H, n, B_CHUNK, t.shape[-1]), 2, 0)

    causal = jnp.tril(jnp.ones((B_CHUNK, B_CHUNK), bool))

    def step(state, xs):
        qc, kc, vc, gc = xs
        b = jnp.cumsum(gc, axis=-2)
        diff = b[..., :, None, :] - b[..., None, :, :]
        decay = jnp.exp(jnp.where(causal[..., None], diff, -jnp.inf))
        attn = jnp.einsum('bhtk,bhsk,bhtsk->bhts', qc, kc, decay)
        o = (jnp.einsum('bhts,bhsv->bhtv', attn, vc)
             + jnp.einsum('bhtk,bhkv->bhtv', qc * jnp.exp(b), state))
        b_last = b[..., -1:, :]
        new_state = (jnp.exp(b_last[..., 0, :])[..., None] * state
                     + jnp.einsum('bhsk,bhsv->bhkv', kc * jnp.exp(b_last - b), vc))
        return new_state, o

    state0 = jnp.zeros((B, H, K, V), jnp.float32)
    _, o = lax.scan(step, state0, (chunks(q), chunks(k), chunks(v), chunks(log_f)))
    return jnp.moveaxis(o, 0, 2).reshape(B, H, S, V)


def split_heads(t, n_heads):
    B, S, _ = t.shape
    return t.reshape(B, S, n_heads, -1).transpose(0, 2, 1, 3)


def merge_heads(t):
    B, H, S, Dh = t.shape
    return t.transpose(0, 2, 1, 3).reshape(B, S, H * Dh)


def even_mixer(h, w_in, w_out, lower_bound, out_norm_g, cos_a, sin_a):
    proj = h @ w_in
    qa, ka, va, qb, fb, ib, gb = jnp.split(proj, np.cumsum(EVEN_SPLITS)[:-1].tolist(), axis=-1)
    qa = apply_partial_rope(split_heads(qa, A_HEADS), cos_a, sin_a)
    ka = apply_partial_rope(split_heads(ka, A_HEADS), cos_a, sin_a)
    oa = dilated_attention(qa, ka, split_heads(va, A_HEADS))
    lb = lower_bound.astype(jnp.float32).reshape(B_HEADS, 1, B_KDIM)
    gate = lb + (1.0 - lb) * jax.nn.sigmoid(split_heads(fb, B_HEADS).astype(jnp.float32))
    q_b = jax.nn.silu(split_heads(qb, B_HEADS).astype(jnp.float32)) * (B_KDIM ** -0.5)
    ob = hgrn2_chunkwise(q_b, 1.0 - gate, split_heads(ib, B_HEADS).astype(jnp.float32),
                         jnp.log(gate))
    ob = rmsnorm(ob, out_norm_g) * jax.nn.silu(split_heads(gb, B_HEADS).astype(jnp.float32))
    mixed = jnp.concatenate([merge_heads(oa), merge_heads(ob)], axis=-1)
    return mixed.astype(h.dtype) @ w_out


def odd_mixer(h, w_qkv, b_qkv, sinks, w_o, b_o, cos_c, sin_c):
    B, S, _ = h.shape
    proj = h @ w_qkv + b_qkv
    q, k, v = jnp.split(proj, [C_Q_HEADS * C_DIM, (C_Q_HEADS + C_KV_HEADS) * C_DIM], axis=-1)
    q = apply_partial_rope(split_heads(q, C_Q_HEADS), cos_c, sin_c)
    q = q.reshape(B, C_KV_HEADS, C_GROUP, S, C_DIM)
    k = apply_partial_rope(split_heads(k, C_KV_HEADS), cos_c, sin_c)
    v = split_heads(v, C_KV_HEADS)
    num, _, l = banded_attention(q, k, v, C_WINDOW - 1,
                                 sink=sinks.reshape(C_KV_HEADS, C_GROUP))
    o = (num / l[..., None]).reshape(B, C_Q_HEADS, S, C_DIM)
    return merge_heads(o).astype(h.dtype) @ w_o + b_o


def squared_relu_mlp(h, w1, w2):
    return jnp.square(jax.nn.relu(h @ w1)) @ w2


def _fwd_setup_inputs(seed: int = 0) -> dict:
    key = jax.random.key(seed)
    ks = jax.random.split(key, 15)

    def nrm(k, shape, scale):
        return scale * jax.random.normal(k, shape, jnp.float32)

    return {
        "x": nrm(ks[0], (BATCH, SEQ, D_MODEL), 1.0),
        "norm_mix_g": 1.0 + nrm(ks[1], (DEPTH, D_MODEL), 0.02),
        "norm_mlp_g": 1.0 + nrm(ks[2], (DEPTH, D_MODEL), 0.02),
        "final_norm_g": 1.0 + nrm(ks[3], (D_MODEL,), 0.02),
        "even_w_in": nrm(ks[4], (N_EVEN, D_MODEL, EVEN_IN), D_MODEL ** -0.5),
        "even_w_out": nrm(ks[5], (N_EVEN, EVEN_MIX, D_MODEL), EVEN_MIX ** -0.5),
        "hgrn_lb_raw": 1.0 + nrm(ks[6], (N_EVEN, B_KWIDTH), 0.1),
        "hgrn_norm_g": 1.0 + nrm(ks[7], (N_EVEN, B_VDIM), 0.02),
        "odd_w_qkv": nrm(ks[8], (N_ODD, D_MODEL, C_QKV), D_MODEL ** -0.5),
        "odd_b_qkv": nrm(ks[9], (N_ODD, C_QKV), 0.02),
        "odd_sinks": nrm(ks[10], (N_ODD, C_Q_HEADS), 1.0),
        "odd_w_o": nrm(ks[11], (N_ODD, C_Q_HEADS * C_DIM, D_MODEL), (C_Q_HEADS * C_DIM) ** -0.5),
        "odd_b_o": nrm(ks[12], (N_ODD, D_MODEL), 0.02),
        "mlp_w1": nrm(ks[13], (DEPTH, D_MODEL, D_FF), D_MODEL ** -0.5),
        "mlp_w2": nrm(ks[14], (DEPTH, D_FF, D_MODEL), D_FF ** -0.5),
    }


def _fwd_reference(x, norm_mix_g, norm_mlp_g, final_norm_g, even_w_in, even_w_out, hgrn_lb_raw,
              hgrn_norm_g, odd_w_qkv, odd_b_qkv, odd_sinks, odd_w_o, odd_b_o, mlp_w1, mlp_w2):
    S = x.shape[1]
    cos_a, sin_a = rope_tables(S, A_DIM)
    cos_c, sin_c = rope_tables(S, C_DIM)
    lb_soft = jax.nn.softmax(hgrn_lb_raw.astype(jnp.float32), axis=0)
    lower_bounds = jnp.cumsum(lb_soft, axis=0) - lb_soft[0:1]
    for layer in range(DEPTH):
        h = rmsnorm(x, norm_mix_g[layer])
        if layer % 2 == 0:
            e = layer // 2
            mix = even_mixer(h, even_w_in[e], even_w_out[e], lower_bounds[e], hgrn_norm_g[e],
                             cos_a, sin_a)
        else:
            o = layer // 2
            mix = odd_mixer(h, odd_w_qkv[o], odd_b_qkv[o], odd_sinks[o], odd_w_o[o], odd_b_o[o],
                            cos_c, sin_c)
        x = x + mix.astype(x.dtype)
        h = rmsnorm(x, norm_mlp_g[layer])
        x = x + squared_relu_mlp(h, mlp_w1[layer], mlp_w2[layer]).astype(x.dtype)
    return rmsnorm(x, final_norm_g)


import jax as _jax
import jax.numpy as _jnp

TWIN_FORMAT = 'train_step'
FWD_PARAMS = ['x', 'norm_mix_g', 'norm_mlp_g', 'final_norm_g', 'even_w_in', 'even_w_out', 'hgrn_lb_raw', 'hgrn_norm_g', 'odd_w_qkv', 'odd_b_qkv', 'odd_sinks', 'odd_w_o', 'odd_b_o', 'mlp_w1', 'mlp_w2']
TWIN_WEIGHTS = ['norm_mix_g', 'norm_mlp_g', 'final_norm_g', 'even_w_in', 'even_w_out', 'hgrn_lb_raw', 'hgrn_norm_g', 'odd_w_qkv', 'odd_b_qkv', 'odd_sinks', 'odd_w_o', 'odd_b_o', 'mlp_w1', 'mlp_w2']
TWIN_DIFF_INPUT = 'x'
TWIN_INPUTS = ['x', 'norm_mix_g', 'norm_mlp_g', 'final_norm_g', 'even_w_in', 'even_w_out', 'hgrn_lb_raw', 'hgrn_norm_g', 'odd_w_qkv', 'odd_b_qkv', 'odd_sinks', 'odd_w_o', 'odd_b_o', 'mlp_w1', 'mlp_w2', 'loss_target', 'm_norm_mix_g', 'm_norm_mlp_g', 'm_final_norm_g', 'm_even_w_in', 'm_even_w_out', 'm_hgrn_lb_raw', 'm_hgrn_norm_g', 'm_odd_w_qkv', 'm_odd_b_qkv', 'm_odd_sinks', 'm_odd_w_o', 'm_odd_b_o', 'm_mlp_w1', 'm_mlp_w2', 'v_norm_mix_g', 'v_norm_mlp_g', 'v_final_norm_g', 'v_even_w_in', 'v_even_w_out', 'v_hgrn_lb_raw', 'v_hgrn_norm_g', 'v_odd_w_qkv', 'v_odd_b_qkv', 'v_odd_sinks', 'v_odd_w_o', 'v_odd_b_o', 'v_mlp_w1', 'v_mlp_w2']
TWIN_OUTPUTS = ['loss', 'grad_x', 'grad_norm_mix_g', 'grad_norm_mlp_g', 'grad_final_norm_g', 'grad_even_w_in', 'grad_even_w_out', 'grad_hgrn_lb_raw', 'grad_hgrn_norm_g', 'grad_odd_w_qkv', 'grad_odd_b_qkv', 'grad_odd_sinks', 'grad_odd_w_o', 'grad_odd_b_o', 'grad_mlp_w1', 'grad_mlp_w2', 'delta_norm_mix_g', 'delta_norm_mlp_g', 'delta_final_norm_g', 'delta_even_w_in', 'delta_even_w_out', 'delta_hgrn_lb_raw', 'delta_hgrn_norm_g', 'delta_odd_w_qkv', 'delta_odd_b_qkv', 'delta_odd_sinks', 'delta_odd_w_o', 'delta_odd_b_o', 'delta_mlp_w1', 'delta_mlp_w2', 'new_m_norm_mix_g', 'new_m_norm_mlp_g', 'new_m_final_norm_g', 'new_m_even_w_in', 'new_m_even_w_out', 'new_m_hgrn_lb_raw', 'new_m_hgrn_norm_g', 'new_m_odd_w_qkv', 'new_m_odd_b_qkv', 'new_m_odd_sinks', 'new_m_odd_w_o', 'new_m_odd_b_o', 'new_m_mlp_w1', 'new_m_mlp_w2', 'new_v_norm_mix_g', 'new_v_norm_mlp_g', 'new_v_final_norm_g', 'new_v_even_w_in', 'new_v_even_w_out', 'new_v_hgrn_lb_raw', 'new_v_hgrn_norm_g', 'new_v_odd_w_qkv', 'new_v_odd_b_qkv', 'new_v_odd_sinks', 'new_v_odd_w_o', 'new_v_odd_b_o', 'new_v_mlp_w1', 'new_v_mlp_w2']
TWIN_LEAF_KINDS = {'loss': 'loss', 'grad_x': 'grad_x', 'grad_norm_mix_g': 'grad_w', 'grad_norm_mlp_g': 'grad_w', 'grad_final_norm_g': 'grad_w', 'grad_even_w_in': 'grad_w', 'grad_even_w_out': 'grad_w', 'grad_hgrn_lb_raw': 'grad_w', 'grad_hgrn_norm_g': 'grad_w', 'grad_odd_w_qkv': 'grad_w', 'grad_odd_b_qkv': 'grad_w', 'grad_odd_sinks': 'grad_w', 'grad_odd_w_o': 'grad_w', 'grad_odd_b_o': 'grad_w', 'grad_mlp_w1': 'grad_w', 'grad_mlp_w2': 'grad_w', 'delta_norm_mix_g': 'delta_w', 'delta_norm_mlp_g': 'delta_w', 'delta_final_norm_g': 'delta_w', 'delta_even_w_in': 'delta_w', 'delta_even_w_out': 'delta_w', 'delta_hgrn_lb_raw': 'delta_w', 'delta_hgrn_norm_g': 'delta_w', 'delta_odd_w_qkv': 'delta_w', 'delta_odd_b_qkv': 'delta_w', 'delta_odd_sinks': 'delta_w', 'delta_odd_w_o': 'delta_w', 'delta_odd_b_o': 'delta_w', 'delta_mlp_w1': 'delta_w', 'delta_mlp_w2': 'delta_w', 'new_m_norm_mix_g': 'new_m', 'new_m_norm_mlp_g': 'new_m', 'new_m_final_norm_g': 'new_m', 'new_m_even_w_in': 'new_m', 'new_m_even_w_out': 'new_m', 'new_m_hgrn_lb_raw': 'new_m', 'new_m_hgrn_norm_g': 'new_m', 'new_m_odd_w_qkv': 'new_m', 'new_m_odd_b_qkv': 'new_m', 'new_m_odd_sinks': 'new_m', 'new_m_odd_w_o': 'new_m', 'new_m_odd_b_o': 'new_m', 'new_m_mlp_w1': 'new_m', 'new_m_mlp_w2': 'new_m', 'new_v_norm_mix_g': 'new_v', 'new_v_norm_mlp_g': 'new_v', 'new_v_final_norm_g': 'new_v', 'new_v_even_w_in': 'new_v', 'new_v_even_w_out': 'new_v', 'new_v_hgrn_lb_raw': 'new_v', 'new_v_hgrn_norm_g': 'new_v', 'new_v_odd_w_qkv': 'new_v', 'new_v_odd_b_qkv': 'new_v', 'new_v_odd_sinks': 'new_v', 'new_v_odd_w_o': 'new_v', 'new_v_odd_b_o': 'new_v', 'new_v_mlp_w1': 'new_v', 'new_v_mlp_w2': 'new_v'}


def _forward(args):
    return _fwd_reference(*[args[k] for k in FWD_PARAMS])


def _output_shape():
    out = _jax.eval_shape(lambda: _forward(_fwd_setup_inputs(0)))
    return out.shape, out.dtype

N_MICROBATCH = 1
ADAM_LR = 0.001
ADAM_B1 = 0.9
ADAM_B2 = 0.999
ADAM_EPS = 1e-08
ADAM_WD = 0.01
ADAM_STEP = 10
PER_EXAMPLE_BATCH_AXIS = {'x': 0, 'loss_target': 0}
SHARED_INPUTS = []
_WEIGHT_DTYPES = {'norm_mix_g': _jnp.float32, 'norm_mlp_g': _jnp.float32, 'final_norm_g': _jnp.float32, 'even_w_in': _jnp.float32, 'even_w_out': _jnp.float32, 'hgrn_lb_raw': _jnp.float32, 'hgrn_norm_g': _jnp.float32, 'odd_w_qkv': _jnp.float32, 'odd_b_qkv': _jnp.float32, 'odd_sinks': _jnp.float32, 'odd_w_o': _jnp.float32, 'odd_b_o': _jnp.float32, 'mlp_w1': _jnp.float32, 'mlp_w2': _jnp.float32}
MOMENT_SCALE = {'norm_mix_g': 2.919232e-02, 'norm_mlp_g': 4.689417e-02, 'final_norm_g': 8.437283e+00, 'even_w_in': 1.923455e-02, 'even_w_out': 2.550737e-02, 'hgrn_lb_raw': 1.262548e-03, 'hgrn_norm_g': 1.009622e-01, 'odd_w_qkv': 2.053890e-02, 'odd_b_qkv': 6.095450e-02, 'odd_sinks': 6.942603e-03, 'odd_w_o': 2.337661e-02, 'odd_b_o': 6.530527e-02, 'mlp_w1': 2.331770e-02, 'mlp_w2': 4.914072e-02}


def _to_microbatches(a, axis):
    t = _jnp.moveaxis(a, axis, 0)
    t = t.reshape((N_MICROBATCH, t.shape[0] // N_MICROBATCH) + t.shape[1:])
    return _jnp.moveaxis(t, 1, axis + 1)


def setup_inputs(seed: int = 0) -> dict:
    inp = _fwd_setup_inputs(seed)
    key = _jax.random.fold_in(_jax.random.key(seed), 7919)
    shape, _ = _output_shape()
    out = dict(inp)
    out["loss_target"] = _jax.random.normal(_jax.random.fold_in(key, 0), shape, _jnp.float32)
    for i, name in enumerate(TWIN_WEIGHTS):
        w = inp[name].astype(_jnp.float32)
        if MOMENT_SCALE is None:
            s = _jnp.sqrt(_jnp.mean(_jnp.square(w)) + 1e-30)
        else:
            s = MOMENT_SCALE[name]
        km, kv = _jax.random.split(_jax.random.fold_in(key, i + 1))
        out[name] = w
        out["m_" + name] = s * _jax.random.normal(km, w.shape, _jnp.float32)
        out["v_" + name] = (s * s) * _jax.random.uniform(kv, w.shape, _jnp.float32, 0.5, 1.5)
    if N_MICROBATCH > 1:
        for name, axis in PER_EXAMPLE_BATCH_AXIS.items():
            out[name] = _to_microbatches(out[name], axis)
    return {'x': out['x'], 'norm_mix_g': out['norm_mix_g'], 'norm_mlp_g': out['norm_mlp_g'], 'final_norm_g': out['final_norm_g'], 'even_w_in': out['even_w_in'], 'even_w_out': out['even_w_out'], 'hgrn_lb_raw': out['hgrn_lb_raw'], 'hgrn_norm_g': out['hgrn_norm_g'], 'odd_w_qkv': out['odd_w_qkv'], 'odd_b_qkv': out['odd_b_qkv'], 'odd_sinks': out['odd_sinks'], 'odd_w_o': out['odd_w_o'], 'odd_b_o': out['odd_b_o'], 'mlp_w1': out['mlp_w1'], 'mlp_w2': out['mlp_w2'], 'loss_target': out['loss_target'], 'm_norm_mix_g': out['m_norm_mix_g'], 'm_norm_mlp_g': out['m_norm_mlp_g'], 'm_final_norm_g': out['m_final_norm_g'], 'm_even_w_in': out['m_even_w_in'], 'm_even_w_out': out['m_even_w_out'], 'm_hgrn_lb_raw': out['m_hgrn_lb_raw'], 'm_hgrn_norm_g': out['m_hgrn_norm_g'], 'm_odd_w_qkv': out['m_odd_w_qkv'], 'm_odd_b_qkv': out['m_odd_b_qkv'], 'm_odd_sinks': out['m_odd_sinks'], 'm_odd_w_o': out['m_odd_w_o'], 'm_odd_b_o': out['m_odd_b_o'], 'm_mlp_w1': out['m_mlp_w1'], 'm_mlp_w2': out['m_mlp_w2'], 'v_norm_mix_g': out['v_norm_mix_g'], 'v_norm_mlp_g': out['v_norm_mlp_g'], 'v_final_norm_g': out['v_final_norm_g'], 'v_even_w_in': out['v_even_w_in'], 'v_even_w_out': out['v_even_w_out'], 'v_hgrn_lb_raw': out['v_hgrn_lb_raw'], 'v_hgrn_norm_g': out['v_hgrn_norm_g'], 'v_odd_w_qkv': out['v_odd_w_qkv'], 'v_odd_b_qkv': out['v_odd_b_qkv'], 'v_odd_sinks': out['v_odd_sinks'], 'v_odd_w_o': out['v_odd_w_o'], 'v_odd_b_o': out['v_odd_b_o'], 'v_mlp_w1': out['v_mlp_w1'], 'v_mlp_w2': out['v_mlp_w2']}


def _loss(weights, diff, rest, loss_target):
    with _jax.named_scope("forward"):
        args = {**rest, TWIN_DIFF_INPUT: diff, **{k: w.astype(_WEIGHT_DTYPES[k]) for k, w in weights.items()}}
        y = _forward(args)
    with _jax.named_scope("loss_head"):
        err = _jnp.square(y.astype(_jnp.float32) - loss_target)
        return 0.5 * _jnp.sum(_jnp.mean(err, axis=-1)) if err.ndim else 0.5 * err


def _adamw(w, g, m, v):
    m = ADAM_B1 * m + (1.0 - ADAM_B1) * g
    v = ADAM_B2 * v + (1.0 - ADAM_B2) * _jnp.square(g)
    m_hat = m / (1.0 - ADAM_B1 ** ADAM_STEP)
    v_hat = v / (1.0 - ADAM_B2 ** ADAM_STEP)
    delta = -ADAM_LR * (m_hat / (_jnp.sqrt(v_hat) + ADAM_EPS) + ADAM_WD * w)
    return delta, m, v


def reference(x, norm_mix_g, norm_mlp_g, final_norm_g, even_w_in, even_w_out, hgrn_lb_raw, hgrn_norm_g, odd_w_qkv, odd_b_qkv, odd_sinks, odd_w_o, odd_b_o, mlp_w1, mlp_w2, loss_target, m_norm_mix_g, m_norm_mlp_g, m_final_norm_g, m_even_w_in, m_even_w_out, m_hgrn_lb_raw, m_hgrn_norm_g, m_odd_w_qkv, m_odd_b_qkv, m_odd_sinks, m_odd_w_o, m_odd_b_o, m_mlp_w1, m_mlp_w2, v_norm_mix_g, v_norm_mlp_g, v_final_norm_g, v_even_w_in, v_even_w_out, v_hgrn_lb_raw, v_hgrn_norm_g, v_odd_w_qkv, v_odd_b_qkv, v_odd_sinks, v_odd_w_o, v_odd_b_o, v_mlp_w1, v_mlp_w2):
    given = dict(x=x, norm_mix_g=norm_mix_g, norm_mlp_g=norm_mlp_g, final_norm_g=final_norm_g, even_w_in=even_w_in, even_w_out=even_w_out, hgrn_lb_raw=hgrn_lb_raw, hgrn_norm_g=hgrn_norm_g, odd_w_qkv=odd_w_qkv, odd_b_qkv=odd_b_qkv, odd_sinks=odd_sinks, odd_w_o=odd_w_o, odd_b_o=odd_b_o, mlp_w1=mlp_w1, mlp_w2=mlp_w2, loss_target=loss_target, m_norm_mix_g=m_norm_mix_g, m_norm_mlp_g=m_norm_mlp_g, m_final_norm_g=m_final_norm_g, m_even_w_in=m_even_w_in, m_even_w_out=m_even_w_out, m_hgrn_lb_raw=m_hgrn_lb_raw, m_hgrn_norm_g=m_hgrn_norm_g, m_odd_w_qkv=m_odd_w_qkv, m_odd_b_qkv=m_odd_b_qkv, m_odd_sinks=m_odd_sinks, m_odd_w_o=m_odd_w_o, m_odd_b_o=m_odd_b_o, m_mlp_w1=m_mlp_w1, m_mlp_w2=m_mlp_w2, v_norm_mix_g=v_norm_mix_g, v_norm_mlp_g=v_norm_mlp_g, v_final_norm_g=v_final_norm_g, v_even_w_in=v_even_w_in, v_even_w_out=v_even_w_out, v_hgrn_lb_raw=v_hgrn_lb_raw, v_hgrn_norm_g=v_hgrn_norm_g, v_odd_w_qkv=v_odd_w_qkv, v_odd_b_qkv=v_odd_b_qkv, v_odd_sinks=v_odd_sinks, v_odd_w_o=v_odd_w_o, v_odd_b_o=v_odd_b_o, v_mlp_w1=v_mlp_w1, v_mlp_w2=v_mlp_w2)
    weights = {n: given[n] for n in TWIN_WEIGHTS}
    shared = {n: given[n] for n in SHARED_INPUTS}
    per_example = {n: given[n] for n in ['x']}
    grad_fn = _jax.value_and_grad(_loss, argnums=(0, 1))

    def one_microbatch(ex, loss_target):
        ex = dict(ex)
        diff = ex.pop(TWIN_DIFF_INPUT)
        return grad_fn(weights, diff, {**shared, **ex}, loss_target)

    if N_MICROBATCH == 1:
        loss, (grad_w, grad_x) = one_microbatch(per_example, given["loss_target"])
    else:
        def body(carry, xs):
            loss_sum, grad_sum = carry
            l_k, (gw_k, gx_k) = one_microbatch(xs[0], xs[1])
            with _jax.named_scope("update"):
                return (loss_sum + l_k, _jax.tree.map(_jnp.add, grad_sum, gw_k)), gx_k

        init = (_jnp.zeros((), _jnp.float32), _jax.tree.map(_jnp.zeros_like, weights))
        (loss, grad_w), grad_x = _jax.lax.scan(body, init, (per_example, given["loss_target"]))
    with _jax.named_scope("update"):
        delta_w, new_m, new_v = {}, {}, {}
        for n in TWIN_WEIGHTS:
            delta_w[n], new_m[n], new_v[n] = _adamw(weights[n], grad_w[n], given["m_" + n], given["v_" + n])
    return (loss, grad_x, *[grad_w[n] for n in TWIN_WEIGHTS], *[delta_w[n] for n in TWIN_WEIGHTS],
            *[new_m[n] for n in TWIN_WEIGHTS], *[new_v[n] for n in TWIN_WEIGHTS])
```

```python
import functools
import math

import jax
import jax.numpy as jnp
import numpy as np
from jax import lax
from jax.experimental import pallas as pl
from jax.experimental.pallas import tpu as pltpu

F32 = jnp.float32
BF16 = jnp.bfloat16

NORM_EPS = 1e-5
ROPE_THETA = 500000.0
ROPE_FRACTION = 4
LANES = 128
BAND = 128
A_DIM = 128
A_BRANCHES = ((128, 1), (512, 4), (2048, 16))
B_DIM = 128
B_CHUNK = 64
B_SUB = 16
C_DIM = 64
C_GROUP = 8
C_WINDOW = 128

ADAM_LR = 0.001
ADAM_B1 = 0.9
ADAM_B2 = 0.999
ADAM_EPS = 1e-08
ADAM_WD = 0.01
ADAM_STEP = 10

VMEM_LIMIT = 56 * 1024 * 1024


def _cparams(*sem):
    return pltpu.CompilerParams(dimension_semantics=tuple(sem), vmem_limit_bytes=VMEM_LIMIT)


def _tile(n, want):
    if n <= want:
        return n
    t = want - want % LANES
    while n % t:
        t -= LANES
    assert t > 0, (n, want)
    return t


def matmul(a, b, *, trans_a=False, trans_b=False, out_dtypes, a_fn=None, epilogue=None, extras=(), name,
           b_layer=None, into=None, tm=1024, tn=1024, tk=512):
    if trans_a:
        K, M = a.shape
    else:
        M, K = a.shape
    b_mat = b.shape[1:] if b_layer is not None else b.shape
    if trans_b:
        N, K2 = b_mat
    else:
        K2, N = b_mat
    assert K == K2, (a.shape, b.shape)
    if into is not None:
        rh, cs = into[0].shape[3:]
        tm, tn = _tile(rh, tm), _tile(cs, tn)
    tm, tn, tk = _tile(M, tm), _tile(N, tn), _tile(K, tk)
    nk = K // tk
    n_extra = len(extras)
    n_out = len(out_dtypes)

    def body(*refs):
        a_ref, b_ref = refs[0], refs[1]
        extra_refs = refs[2:2 + n_extra]
        out_refs = refs[-1 - n_out:-1]
        acc_ref = refs[-1]
        k = pl.program_id(2)

        @pl.when(k == 0)
        def _():
            acc_ref[...] = jnp.zeros_like(acc_ref)

        at = a_ref[...]
        if a_fn is not None:
            at = a_fn(at.astype(F32))
        at = at.astype(BF16)
        bt = b_ref[...].astype(BF16)
        dims = (((0,) if trans_a else (1,), (1,) if trans_b else (0,)), ((), ()))
        acc_ref[...] += lax.dot_general(at, bt, dims, preferred_element_type=F32)

        @pl.when(k == nk - 1)
        def _():
            acc = acc_ref[...]
            ex = [r[...] for r in extra_refs]
            outs = epilogue(acc, *ex) if epilogue is not None else (acc,)
            for o_ref, o in zip(out_refs, outs):
                o_ref[...] = o.astype(o_ref.dtype)

    a_spec = pl.BlockSpec((tk, tm), lambda i, j, k: (k, i)) if trans_a else pl.BlockSpec((tm, tk), lambda i, j, k: (i, k))
    if b_layer is None:
        b_spec = (pl.BlockSpec((tn, tk), lambda i, j, k: (j, k)) if trans_b
                  else pl.BlockSpec((tk, tn), lambda i, j, k: (k, j)))
    else:
        b_spec = (pl.BlockSpec((None, tn, tk), lambda i, j, k: (b_layer, j, k)) if trans_b
                  else pl.BlockSpec((None, tk, tn), lambda i, j, k: (b_layer, k, j)))
    e_specs = []
    for e in extras:
        if e.shape == (1, N):
            e_specs.append(pl.BlockSpec((1, tn), lambda i, j, k: (0, j)))
        else:
            assert e.shape == (M, N), (e.shape, M, N)
            e_specs.append(pl.BlockSpec((tm, tn), lambda i, j, k: (i, j)))
    args = [a, b, *extras]
    in_specs = [a_spec, b_spec] + e_specs
    aliases = {}
    if into is None:
        out_specs = [pl.BlockSpec((tm, tn), lambda i, j, k: (i, j)) for _ in out_dtypes]
        out_shape = [jax.ShapeDtypeStruct((M, N), dt) for dt in out_dtypes]
    else:
        buf, layer, sharding = into
        assert n_out == 1 and buf.dtype == out_dtypes[0]
        index = grad_buffer_index(buf.shape, (M, N), sharding, layer, tm, tn)
        out_specs = [pl.BlockSpec((None, None, None, tm, tn), lambda i, j, k: index(i, j))]
        out_shape = [jax.ShapeDtypeStruct(buf.shape, buf.dtype)]
        in_specs.append(pl.BlockSpec(memory_space=pl.ANY))
        args.append(buf)
        aliases = {len(args) - 1: 0}
    outs = pl.pallas_call(
        body, name=name,
        grid=(M // tm, N // tn, nk),
        in_specs=in_specs, out_specs=out_specs, out_shape=out_shape,
        input_output_aliases=aliases,
        scratch_shapes=[pltpu.VMEM((tm, tn), F32)],
        compiler_params=_cparams("parallel", "parallel", "arbitrary"),
    )(*args)
    return outs[0] if n_out == 1 else tuple(outs)


def grad_buffer_index(buf_shape, mat_shape, sharding, layer, tm, tn):
    _, _, _, rh, cs = buf_shape
    M, N = mat_shape
    ib, jb = rh // tm, cs // tn
    if sharding == "cols":
        assert (M, N) == (2 * rh, 4 * cs), (buf_shape, mat_shape)
        return lambda i, j: (j // jb, i // ib, layer, i % ib, j % jb)
    assert sharding == "rows" and (M, N) == (8 * rh, cs), (buf_shape, mat_shape)
    return lambda i, j: (i // (2 * ib), (i // ib) % 2, layer, i % ib, j)


def rmsnorm_fwd(x, g, *, name, tr=256):
    S, D = x.shape
    tr = _tile(S, tr)

    def body(x_ref, g_ref, h_ref):
        xv = x_ref[...]
        rstd = lax.rsqrt(jnp.mean(xv * xv, axis=-1, keepdims=True) + NORM_EPS)
        h_ref[...] = (xv * rstd * g_ref[...]).astype(h_ref.dtype)

    return pl.pallas_call(
        body, name=name, grid=(S // tr,),
        in_specs=[pl.BlockSpec((tr, D), lambda i: (i, 0)), pl.BlockSpec((1, D), lambda i: (0, 0))],
        out_specs=pl.BlockSpec((tr, D), lambda i: (i, 0)),
        out_shape=jax.ShapeDtypeStruct((S, D), BF16),
        compiler_params=_cparams("parallel"),
    )(x, g.reshape(1, D))


def _rms_bwd_rows(xv, gv, dh):
    rstd = lax.rsqrt(jnp.mean(xv * xv, axis=-1, keepdims=True) + NORM_EPS)
    xhat = xv * rstd
    dxhat = dh * gv
    dx = rstd * (dxhat - xhat * jnp.mean(dxhat * xhat, axis=-1, keepdims=True))
    return dx, dh * xhat


def rmsnorm_bwd(x, g, dh, dres, *, name, tr=256):
    S, D = x.shape
    tr = _tile(S, tr)

    def body(x_ref, g_ref, dh_ref, dres_ref, dx_ref, dg_ref):
        @pl.when(pl.program_id(0) == 0)
        def _():
            dg_ref[...] = jnp.zeros_like(dg_ref)

        dx, dgr = _rms_bwd_rows(x_ref[...], g_ref[...], dh_ref[...].astype(F32))
        dx_ref[...] = dres_ref[...] + dx
        dg_ref[...] += jnp.sum(dgr, axis=0, keepdims=True)

    row = pl.BlockSpec((tr, D), lambda i: (i, 0))
    vec = pl.BlockSpec((1, D), lambda i: (0, 0))
    return pl.pallas_call(
        body, name=name, grid=(S // tr,),
        in_specs=[row, vec, row, row],
        out_specs=[row, vec],
        out_shape=[jax.ShapeDtypeStruct((S, D), F32), jax.ShapeDtypeStruct((1, D), F32)],
        compiler_params=_cparams("arbitrary"),
    )(x, g.reshape(1, D), dh, dres)


def final_norm_loss(x, g, target, *, name, tr=256):
    S, D = x.shape
    tr = _tile(S, tr)

    def body(x_ref, g_ref, t_ref, loss_ref, dx_ref, dg_ref):
        @pl.when(pl.program_id(0) == 0)
        def _():
            dg_ref[...] = jnp.zeros_like(dg_ref)
            loss_ref[...] = jnp.zeros_like(loss_ref)

        xv, gv = x_ref[...], g_ref[...]
        rstd = lax.rsqrt(jnp.mean(xv * xv, axis=-1, keepdims=True) + NORM_EPS)
        err = xv * rstd * gv - t_ref[...]
        part = 0.5 * jnp.sum(jnp.mean(err * err, axis=-1, keepdims=True), axis=0, keepdims=True)
        loss_ref[...] += jnp.broadcast_to(part, loss_ref.shape)
        dx, dgr = _rms_bwd_rows(xv, gv, err * (1.0 / D))
        dx_ref[...] = dx
        dg_ref[...] += jnp.sum(dgr, axis=0, keepdims=True)

    row = pl.BlockSpec((tr, D), lambda i: (i, 0))
    vec = pl.BlockSpec((1, D), lambda i: (0, 0))
    return pl.pallas_call(
        body, name=name, grid=(S // tr,),
        in_specs=[row, vec, row],
        out_specs=[pl.BlockSpec((8, LANES), lambda i: (0, 0)), row, vec],
        out_shape=[jax.ShapeDtypeStruct((8, LANES), F32), jax.ShapeDtypeStruct((S, D), F32),
                   jax.ShapeDtypeStruct((1, D), F32)],
        compiler_params=_cparams("arbitrary"),
    )(x, g.reshape(1, D), target)


def colsum(a, *, name, tr=256):
    S, N = a.shape
    tr = _tile(S, tr)

    def body(a_ref, o_ref):
        @pl.when(pl.program_id(0) == 0)
        def _():
            o_ref[...] = jnp.zeros_like(o_ref)

        o_ref[...] += jnp.sum(a_ref[...].astype(F32), axis=0, keepdims=True)

    return pl.pallas_call(
        body, name=name, grid=(S // tr,),
        in_specs=[pl.BlockSpec((tr, N), lambda i: (i, 0))],
        out_specs=pl.BlockSpec((1, N), lambda i: (0, 0)),
        out_shape=jax.ShapeDtypeStruct((1, N), F32),
        compiler_params=_cparams("arbitrary"),
    )(a)


def rope_tables(seq, head_dim):
    rot = head_dim // ROPE_FRACTION
    half = rot // 2
    inv_freq = 1.0 / (ROPE_THETA ** (jnp.arange(0, rot, 2, dtype=F32) / rot))
    ang = jnp.arange(seq, dtype=F32)[:, None] * inv_freq[None, :]
    cos, sin = jnp.cos(ang), jnp.sin(ang)
    ones = jnp.ones((seq, LANES - 2 * half), F32)
    zeros = jnp.zeros((seq, LANES - 2 * half), F32)
    zh = jnp.zeros((seq, half), F32)
    c = jnp.concatenate([cos, cos, ones], axis=1)
    sa = jnp.concatenate([-sin, zh, zeros], axis=1)
    sb = jnp.concatenate([zh, sin, zeros], axis=1)
    return c, sa, sb, half


def _rope(x, c, sa, sb, half):
    return x * c + pltpu.roll(x, LANES - half, 1) * sa + pltpu.roll(x, half, 1) * sb


def _rope_bwd(d, c, sa, sb, half):
    return d * c + pltpu.roll(d * sa, half, 1) + pltpu.roll(d * sb, LANES - half, 1)


def _band_masks(n, max_dist):
    qi = lax.broadcasted_iota(jnp.int32, (BAND, BAND), 0)
    kj = lax.broadcasted_iota(jnp.int32, (BAND, BAND), 1)
    cur = kj <= qi
    prev = ((kj >= qi) if max_dist == BAND else (kj > qi)) & (n > 0)
    return prev, cur


def _dot_nt(a, b):
    return lax.dot_general(a, b, (((1,), (1,)), ((), ())), preferred_element_type=F32)


def _dot_tn(a, b):
    return lax.dot_general(a, b, (((0,), (0,)), ((), ())), preferred_element_type=F32)


def _dot(a, b):
    return jnp.dot(a, b, preferred_element_type=F32)


def band_attention_fwd(src, tabs, *, L, dil, n_heads, group, q_blk, k_blk, v_blk, blk_per_row, max_dist, scale,
                       half, sink=None, normalize, name):
    c_t, sa_t, sb_t = tabs
    nb = L // BAND
    W = dil * n_heads * LANES
    use_sink = sink is not None

    def body(*refs):
        q_ref, k_ref, v_ref, c_ref, sa_ref, sb_ref = refs[:6]
        pos = 6
        if use_sink:
            sink_ref = refs[pos]
            pos += 1
        o_ref, m_ref, l_ref = refs[pos:pos + 3] if not normalize else (refs[pos], refs[pos + 1], None)
        kr_ref, vb_ref = refs[-2], refs[-1]

        def prep(n, carry):
            rows = pl.ds(pl.multiple_of(n * BAND, BAND), BAND)
            kr_ref[rows, :] = _rope(k_ref[rows, :], c_ref[rows, :], sa_ref[rows, :], sb_ref[rows, :], half).astype(BF16)
            vb_ref[rows, :] = v_ref[rows, :].astype(BF16)
            return carry

        lax.fori_loop(0, nb, prep, 0)

        def step(n, carry):
            rows = pl.ds(pl.multiple_of(n * BAND, BAND), BAND)
            prow = pl.ds(pl.multiple_of(jnp.maximum(n - 1, 0) * BAND, BAND), BAND)
            q = _rope(q_ref[rows, :], c_ref[rows, :], sa_ref[rows, :], sb_ref[rows, :], half).astype(BF16)
            mp, mc = _band_masks(n, max_dist)
            sp = jnp.where(mp, _dot_nt(q, kr_ref[prow, :]) * scale, -jnp.inf)
            sc = jnp.where(mc, _dot_nt(q, kr_ref[rows, :]) * scale, -jnp.inf)
            m = jnp.maximum(jnp.max(sp, axis=-1, keepdims=True), jnp.max(sc, axis=-1, keepdims=True))
            if use_sink:
                sk = sink_ref[:, 0:1]
                m = jnp.maximum(m, sk)
            pp = jnp.exp(sp - m)
            pc = jnp.exp(sc - m)
            l = jnp.sum(pp, axis=-1, keepdims=True) + jnp.sum(pc, axis=-1, keepdims=True)
            if use_sink:
                l = l + jnp.exp(sk - m)
            num = _dot(pp.astype(BF16), vb_ref[prow, :]) + _dot(pc.astype(BF16), vb_ref[rows, :])
            if normalize:
                o_ref[rows, :] = num / l
                m_ref[rows, :] = jnp.broadcast_to(m + jnp.log(l), (BAND, LANES))
            else:
                o_ref[rows, :] = num
                m_ref[rows, :] = jnp.broadcast_to(m, (BAND, LANES))
                l_ref[rows, :] = jnp.broadcast_to(l, (BAND, LANES))
            return carry

        lax.fori_loop(0, nb, step, 0)

    def col(off, div):
        return pl.BlockSpec((L, LANES), lambda r, h: (0, r * blk_per_row + off + h // div))

    tab = pl.BlockSpec((L, LANES), lambda r, h: (0, r))
    out = pl.BlockSpec((L, LANES), lambda r, h: (0, r * n_heads + h))
    in_specs = [col(q_blk, 1), col(k_blk, group), col(v_blk, group), tab, tab, tab]
    args = [src, src, src, c_t, sa_t, sb_t]
    if use_sink:
        in_specs.append(pl.BlockSpec((None, 1, LANES), lambda r, h: (h, 0, 0)))
        args.append(sink)
    n_out = 2 if normalize else 3
    return pl.pallas_call(
        body, name=name, grid=(dil, n_heads),
        in_specs=in_specs,
        out_specs=[out] * n_out,
        out_shape=[jax.ShapeDtypeStruct((L, W), F32)] * n_out,
        scratch_shapes=[pltpu.VMEM((L, LANES), BF16), pltpu.VMEM((L, LANES), BF16)],
        compiler_params=_cparams("parallel", "arbitrary"),
    )(*args)


def band_attention_bwd(src, tabs, o, lse, do, *, L, dil, n_heads, group, q_blk, k_blk, v_blk, blk_per_row, max_dist,
                       scale, half, do_bpr, sink=None, acc=None, name):
    c_t, sa_t, sb_t = tabs
    nb = L // BAND
    n_kv = n_heads // group
    use_sink = sink is not None
    use_acc = acc is not None
    assert not (use_acc and group != 1)

    def body(*refs):
        q_ref, k_ref, v_ref, c_ref, sa_ref, sb_ref, o_ref, lse_ref, do_ref = refs[:9]
        pos = 9
        if use_sink:
            sink_ref = refs[pos]
            pos += 1
        if use_acc:
            aq_ref, ak_ref, av_ref = refs[pos:pos + 3]
            pos += 3
        dq_ref, dk_ref, dv_ref = refs[pos:pos + 3]
        pos += 3
        if use_sink:
            dsink_ref = refs[pos]
        kr_ref, vb_ref, dka_ref, dva_ref = refs[-4:]
        h = pl.program_id(1)

        def prep(n, carry):
            rows = pl.ds(pl.multiple_of(n * BAND, BAND), BAND)
            kr_ref[rows, :] = _rope(k_ref[rows, :], c_ref[rows, :], sa_ref[rows, :], sb_ref[rows, :], half).astype(BF16)
            vb_ref[rows, :] = v_ref[rows, :].astype(BF16)
            return carry

        lax.fori_loop(0, nb, prep, 0)

        @pl.when(h % group == 0)
        def _():
            dka_ref[...] = jnp.zeros_like(dka_ref)
            dva_ref[...] = jnp.zeros_like(dva_ref)

        def step(n, dsk):
            rows = pl.ds(pl.multiple_of(n * BAND, BAND), BAND)
            prow = pl.ds(pl.multiple_of(jnp.maximum(n - 1, 0) * BAND, BAND), BAND)
            cv, sav, sbv = c_ref[rows, :], sa_ref[rows, :], sb_ref[rows, :]
            q = _rope(q_ref[rows, :], cv, sav, sbv, half).astype(BF16)
            dov = do_ref[rows, :]
            lsev = lse_ref[rows, 0:1]
            delta = jnp.sum(dov * o_ref[rows, :], axis=-1, keepdims=True)
            dob = dov.astype(BF16)
            mp, mc = _band_masks(n, max_dist)
            kp, kc, vp, vc = kr_ref[prow, :], kr_ref[rows, :], vb_ref[prow, :], vb_ref[rows, :]
            pp = jnp.exp(jnp.where(mp, _dot_nt(q, kp) * scale, -jnp.inf) - lsev)
            pc = jnp.exp(jnp.where(mc, _dot_nt(q, kc) * scale, -jnp.inf) - lsev)
            dsp = (pp * (_dot_nt(dob, vp) - delta) * scale).astype(BF16)
            dsc = (pc * (_dot_nt(dob, vc) - delta) * scale).astype(BF16)
            dq = _rope_bwd(_dot(dsp, kp) + _dot(dsc, kc), cv, sav, sbv, half)
            if use_acc:
                dq = dq + aq_ref[rows, :]
            dq_ref[rows, :] = dq
            dka_ref[prow, :] += _dot_tn(dsp, q)
            dka_ref[rows, :] += _dot_tn(dsc, q)
            dva_ref[prow, :] += _dot_tn(pp.astype(BF16), dob)
            dva_ref[rows, :] += _dot_tn(pc.astype(BF16), dob)
            if use_sink:
                dsk = dsk - jnp.sum(jnp.exp(sink_ref[:, 0:1] - lsev) * delta, axis=0, keepdims=True)
            return dsk

        dsk = lax.fori_loop(0, nb, step, jnp.zeros((1, 1), F32))
        if use_sink:
            dsink_ref[...] = jnp.broadcast_to(dsk, dsink_ref.shape)

        @pl.when(h % group == group - 1)
        def _():
            def fin(n, carry):
                rows = pl.ds(pl.multiple_of(n * BAND, BAND), BAND)
                dk = _rope_bwd(dka_ref[rows, :], c_ref[rows, :], sa_ref[rows, :], sb_ref[rows, :], half)
                dv = dva_ref[rows, :]
                if use_acc:
                    dk = dk + ak_ref[rows, :]
                    dv = dv + av_ref[rows, :]
                dk_ref[rows, :] = dk
                dv_ref[rows, :] = dv
                return carry

            lax.fori_loop(0, nb, fin, 0)

    def col(off, div):
        return pl.BlockSpec((L, LANES), lambda r, h: (0, r * blk_per_row + off + h // div))

    tab = pl.BlockSpec((L, LANES), lambda r, h: (0, r))
    qo = pl.BlockSpec((L, LANES), lambda r, h: (0, r * n_heads + h))
    kvo = pl.BlockSpec((L, LANES), lambda r, h: (0, r * n_kv + h // group))
    dospec = pl.BlockSpec((L, LANES), lambda r, h: (0, r * do_bpr + h))
    in_specs = [col(q_blk, 1), col(k_blk, group), col(v_blk, group), tab, tab, tab, qo, qo, dospec]
    args = [src, src, src, c_t, sa_t, sb_t, o, lse, do]
    if use_sink:
        in_specs.append(pl.BlockSpec((None, 1, LANES), lambda r, h: (h, 0, 0)))
        args.append(sink)
    if use_acc:
        in_specs += [qo, kvo, kvo]
        args += list(acc)
    out_specs = [qo, kvo, kvo]
    out_shape = [jax.ShapeDtypeStruct((L, dil * n_heads * LANES), F32),
                 jax.ShapeDtypeStruct((L, dil * n_kv * LANES), F32),
                 jax.ShapeDtypeStruct((L, dil * n_kv * LANES), F32)]
    if use_sink:
        out_specs.append(pl.BlockSpec((None, 1, LANES), lambda r, h: (h, 0, 0)))
        out_shape.append(jax.ShapeDtypeStruct((n_heads, 1, LANES), F32))
    return pl.pallas_call(
        body, name=name, grid=(dil, n_heads),
        in_specs=in_specs, out_specs=out_specs, out_shape=out_shape,
        scratch_shapes=[pltpu.VMEM((L, LANES), BF16), pltpu.VMEM((L, LANES), BF16),
                        pltpu.VMEM((L, LANES), F32), pltpu.VMEM((L, LANES), F32)],
        compiler_params=_cparams("parallel", "arbitrary"),
    )(*args)


def merge_branches(parts, *, name, tr=256):
    S, W = parts[0][0].shape
    tr = _tile(S, tr)
    nbr = len(parts)

    def body(*refs):
        ins, (o_ref, lse_ref) = refs[:3 * nbr], refs[3 * nbr:]
        nums = [ins[3 * i][...] for i in range(nbr)]
        ms = [ins[3 * i + 1][...] for i in range(nbr)]
        ls = [ins[3 * i + 2][...] for i in range(nbr)]
        mx = functools.reduce(jnp.maximum, ms)
        ws = [jnp.exp(m - mx) for m in ms]
        num = sum(w * n for w, n in zip(ws, nums))
        den = sum(w * l for w, l in zip(ws, ls))
        o_ref[...] = num / den
        lse_ref[...] = mx + jnp.log(den)

    row = pl.BlockSpec((tr, W), lambda i: (i, 0))
    flat = [a for p in parts for a in p]
    return pl.pallas_call(
        body, name=name, grid=(S // tr,),
        in_specs=[row] * len(flat), out_specs=[row, row],
        out_shape=[jax.ShapeDtypeStruct((S, W), F32)] * 2,
        compiler_params=_cparams("parallel"),
    )(*flat)


def _sigmoid(x):
    return 1.0 / (1.0 + jnp.exp(-x))


def _tri(n, lower):
    r = lax.broadcasted_iota(jnp.int32, (n, n), 0)
    c = lax.broadcasted_iota(jnp.int32, (n, n), 1)
    return ((c <= r) if lower else (c >= r)).astype(F32)


def _dot_exact(a, b, trans_a=False, trans_b=False):
    dims = (((0,) if trans_a else (1,), (1,) if trans_b else (0,)), ((), ()))
    return lax.dot_general(a, b, dims, preferred_element_type=F32, precision=lax.Precision.HIGHEST)


def _hgrn_gates(qb, fb, lb):
    sq = _sigmoid(qb)
    q = qb * sq * (B_DIM ** -0.5)
    sf = _sigmoid(fb)
    gate = lb + (1.0 - lb) * sf
    return q, 1.0 - gate, gate, sq, sf


def _hgrn_intra_fwd(q_ref, k_ref, b_ref):
    C, n_sub = B_CHUNK, B_CHUNK // B_SUB
    b_all, k_all = b_ref[...], k_ref[...]
    srow = lax.broadcasted_iota(jnp.int32, (C, LANES), 0)
    lane = lax.broadcasted_iota(jnp.int32, (B_SUB, C), 1)
    trow = lax.broadcasted_iota(jnp.int32, (B_SUB, LANES), 0)
    blocks = []
    for i in range(n_sub):
        r0 = i * B_SUB
        qi, bi = q_ref[pl.ds(r0, B_SUB), :], b_ref[pl.ds(r0, B_SUB), :]
        if i == 0:
            a_i = jnp.zeros((B_SUB, C), F32)
        else:
            ref_b = b_ref[pl.ds(r0 - 1, 1), :]
            qt = (qi * jnp.exp(bi - ref_b)).astype(BF16)
            kt = jnp.where(srow < r0, k_all * jnp.exp(jnp.minimum(ref_b - b_all, 0.0)), 0.0).astype(BF16)
            a_i = _dot_nt(qt, kt)
        for sl in range(B_SUB):
            s = r0 + sl
            e = jnp.where(trow >= sl, jnp.exp(jnp.minimum(bi - b_ref[pl.ds(s, 1), :], 0.0)), 0.0)
            colv = jnp.sum(qi * k_ref[pl.ds(s, 1), :] * e, axis=-1, keepdims=True)
            a_i = a_i + jnp.where(lane == s, colv, 0.0)
        blocks.append(a_i)
    return jnp.concatenate(blocks, axis=0)


def hgrn_fwd(src, lb, norm_g, *, S, n_heads, q_blk, f_blk, i_blk, g_blk, name):
    nc = S // B_CHUNK

    def body(qb_ref, fb_ref, ib_ref, gb_ref, lb_ref, ng_ref, out_ref, o_ref, st_ref, state_ref, q_s, k_s, b_s):
        state_ref[...] = jnp.zeros_like(state_ref)
        tril = _tri(B_CHUNK, True)

        def step(c, carry):
            rows = pl.ds(pl.multiple_of(c * B_CHUNK, B_CHUNK), B_CHUNK)
            q, k, gate, _, _ = _hgrn_gates(qb_ref[rows, :], fb_ref[rows, :], lb_ref[...])
            b = _dot_exact(tril, jnp.log(gate))
            vf = ib_ref[rows, :]
            v = vf.astype(BF16)
            q_s[...], k_s[...], b_s[...] = q, k, b
            st = state_ref[...]
            st_ref[c] = st
            a = _hgrn_intra_fwd(q_s, k_s, b_s)
            o = _dot(a.astype(BF16), v) + _dot_nt((q * jnp.exp(b)).astype(BF16), st.astype(BF16))
            b_last = b_s[pl.ds(B_CHUNK - 1, 1), :]
            state_ref[...] = st * jnp.exp(b_last) + _dot_exact(vf, k * jnp.exp(b_last - b), trans_a=True)
            o_ref[rows, :] = o
            rstd = lax.rsqrt(jnp.mean(o * o, axis=-1, keepdims=True) + NORM_EPS)
            gb = gb_ref[rows, :]
            out_ref[rows, :] = o * rstd * ng_ref[...] * (gb * _sigmoid(gb))
            return carry

        lax.fori_loop(0, nc, step, 0)

    def col(off):
        return pl.BlockSpec((S, LANES), lambda h: (0, off + h))

    hv = pl.BlockSpec((None, 1, LANES), lambda h: (h, 0, 0))
    return pl.pallas_call(
        body, name=name, grid=(n_heads,),
        in_specs=[col(q_blk), col(f_blk), col(i_blk), col(g_blk), hv, pl.BlockSpec((1, LANES), lambda h: (0, 0))],
        out_specs=[col(0), col(0), pl.BlockSpec((None, nc, LANES, LANES), lambda h: (h, 0, 0, 0))],
        out_shape=[jax.ShapeDtypeStruct((S, n_heads * LANES), F32), jax.ShapeDtypeStruct((S, n_heads * LANES), F32),
                   jax.ShapeDtypeStruct((n_heads, nc, LANES, LANES), F32)],
        scratch_shapes=[pltpu.VMEM((LANES, LANES), F32)] + [pltpu.VMEM((B_CHUNK, LANES), F32)] * 3,
        compiler_params=_cparams("parallel"),
    )(src, src, src, src, lb, norm_g.reshape(1, LANES))


def hgrn_bwd(src, lb, norm_g, o, states, dout, *, S, n_heads, q_blk, f_blk, i_blk, g_blk, dout_blk, name):
    nc = S // B_CHUNK
    C, n_sub = B_CHUNK, B_CHUNK // B_SUB

    def body(qb_ref, fb_ref, ib_ref, gb_ref, lb_ref, ng_ref, o_ref, st_ref, dout_ref,
             dqb_ref, dfb_ref, dib_ref, dgb_ref, dlb_ref, dng_ref, dstate_ref, q_s, k_s, b_s, dq_s, dk_s):
        @pl.when(pl.program_id(0) == 0)
        def _():
            dng_ref[...] = jnp.zeros_like(dng_ref)

        dstate_ref[...] = jnp.zeros_like(dstate_ref)
        tril = _tri(C, True)
        triu = _tri(C, False)
        lbv, ngv = lb_ref[...], ng_ref[...]
        srow = lax.broadcasted_iota(jnp.int32, (C, LANES), 0)
        lane = lax.broadcasted_iota(jnp.int32, (B_SUB, C), 1)
        trow = lax.broadcasted_iota(jnp.int32, (B_SUB, LANES), 0)
        causal = lax.broadcasted_iota(jnp.int32, (C, C), 1) <= lax.broadcasted_iota(jnp.int32, (C, C), 0)

        def step(ci, carry):
            dlog_carry, dlb_acc, dng_acc = carry
            c = nc - 1 - ci
            rows = pl.ds(pl.multiple_of(c * C, C), C)
            qb, fb, gb = qb_ref[rows, :], fb_ref[rows, :], gb_ref[rows, :]
            q, k, gate, sq, sf = _hgrn_gates(qb, fb, lbv)
            b = _dot_exact(tril, jnp.log(gate))
            vf = ib_ref[rows, :]
            v = vf.astype(BF16)
            q_s[...], k_s[...], b_s[...] = q, k, b
            st = st_ref[c]
            dst = dstate_ref[...]

            ov = o_ref[rows, :]
            dout = dout_ref[rows, :]
            rstd = lax.rsqrt(jnp.mean(ov * ov, axis=-1, keepdims=True) + NORM_EPS)
            xhat = ov * rstd
            sg = _sigmoid(gb)
            dy = dout * (gb * sg)
            dgb_ref[rows, :] = dout * (xhat * ngv) * (sg * (1.0 + gb * (1.0 - sg)))
            dng_acc = dng_acc + jnp.sum(dy * xhat, axis=0, keepdims=True)
            dxhat = dy * ngv
            do = rstd * (dxhat - xhat * jnp.mean(dxhat * xhat, axis=-1, keepdims=True))
            dob = do.astype(BF16)

            eb = jnp.exp(b)
            b_last = b_s[pl.ds(C - 1, 1), :]
            ebl = jnp.exp(b_last - b)
            a = _hgrn_intra_fwd(q_s, k_s, b_s)
            da = jnp.where(causal, _dot_exact(do, vf, trans_b=True), 0.0)
            dv = _dot_tn(a.astype(BF16), dob) + _dot_nt((k * ebl).astype(BF16), dst.astype(BF16))
            dq_s[...] = _dot_exact(do, st) * eb
            dk_s[...] = _dot_exact(vf, dst) * ebl
            dstate_ref[...] = dst * jnp.exp(b_last) + _dot_exact(do, q * eb, trans_a=True)
            b_all, k_all = b, k
            for i in range(n_sub):
                r0 = i * B_SUB
                blk = pl.ds(r0, B_SUB)
                qi, bi = q_s[blk, :], b_s[blk, :]
                da_i = da[r0:r0 + B_SUB, :]
                dq_i = jnp.zeros((B_SUB, LANES), F32)
                if i > 0:
                    ref_b = b_s[pl.ds(r0 - 1, 1), :]
                    eq = jnp.exp(bi - ref_b)
                    ek = jnp.where(srow < r0, jnp.exp(jnp.minimum(ref_b - b_all, 0.0)), 0.0)
                    da_off = jnp.where(lane < r0, da_i, 0.0)
                    dq_i = _dot_exact(da_off, k_all * ek) * eq
                    dk_s[...] += _dot_exact(da_off, qi * eq, trans_a=True) * ek
                for sl in range(B_SUB):
                    s = r0 + sl
                    e = jnp.where(trow >= sl, jnp.exp(jnp.minimum(bi - b_s[pl.ds(s, 1), :], 0.0)), 0.0)
                    dac = jnp.sum(jnp.where(lane == s, da_i, 0.0), axis=-1, keepdims=True)
                    dq_i = dq_i + dac * k_s[pl.ds(s, 1), :] * e
                    dk_s[pl.ds(s, 1), :] += jnp.sum(dac * qi * e, axis=0, keepdims=True)
                dq_s[blk, :] += dq_i
            dq, dk = dq_s[...], dk_s[...]
            db = q * dq - k * dk
            dlog = _dot_exact(triu, db) + dlog_carry
            dlog_carry = dlog_carry + jnp.sum(db, axis=0, keepdims=True)
            dgate = dlog / gate - dk
            dqb_ref[rows, :] = dq * (B_DIM ** -0.5) * (sq * (1.0 + qb * (1.0 - sq)))
            dfb_ref[rows, :] = dgate * (1.0 - lbv) * sf * (1.0 - sf)
            dib_ref[rows, :] = dv
            dlb_acc = dlb_acc + jnp.sum(dgate * (1.0 - sf), axis=0, keepdims=True)
            return dlog_carry, dlb_acc, dng_acc

        z = jnp.zeros((1, LANES), F32)
        _, dlb_acc, dng_acc = lax.fori_loop(0, nc, step, (z, z, z))
        dlb_ref[...] = dlb_acc
        dng_ref[...] += dng_acc

    def col(off):
        return pl.BlockSpec((S, LANES), lambda h: (0, off + h))

    hv = pl.BlockSpec((None, 1, LANES), lambda h: (h, 0, 0))
    vec = pl.BlockSpec((1, LANES), lambda h: (0, 0))
    full = jax.ShapeDtypeStruct((S, n_heads * LANES), F32)
    return pl.pallas_call(
        body, name=name, grid=(n_heads,),
        in_specs=[col(q_blk), col(f_blk), col(i_blk), col(g_blk), hv, vec, col(0),
                  pl.BlockSpec((None, nc, LANES, LANES), lambda h: (h, 0, 0, 0)), col(dout_blk)],
        out_specs=[col(0), col(0), col(0), col(0), hv, vec],
        out_shape=[full, full, full, full, jax.ShapeDtypeStruct((n_heads, 1, LANES), F32),
                   jax.ShapeDtypeStruct((1, LANES), F32)],
        scratch_shapes=[pltpu.VMEM((LANES, LANES), F32)] + [pltpu.VMEM((B_CHUNK, LANES), F32)] * 5,
        compiler_params=_cparams("arbitrary"),
    )(src, src, src, src, lb, norm_g.reshape(1, LANES), o, states, dout)


def _pad_heads_cols(w, n_heads):
    lead = w.shape[:-1]
    w = w.reshape(lead + (n_heads, C_DIM))
    w = jnp.pad(w, [(0, 0)] * len(lead) + [(0, 0), (0, LANES - C_DIM)])
    return w.reshape(lead + (n_heads * LANES,))


def _unpad_heads_cols(w, n_heads):
    lead = w.shape[:-1]
    return w.reshape(lead + (n_heads, LANES))[..., :C_DIM].reshape(lead + (n_heads * C_DIM,))


def _lower_bounds(lb_raw):
    lb_soft = jax.nn.softmax(lb_raw.astype(F32), axis=0)
    return jnp.cumsum(lb_soft, axis=0) - lb_soft[0:1]


def _relu2(u):
    r = jnp.maximum(u, 0.0)
    return r * r


def local_step(x, target, P, gbufs):
    S, D = x.shape
    depth = P["norm_mix_g"].shape[0]
    HA = D // 2 // A_DIM
    HB = HA
    HQ = D // C_DIM
    HKV = HQ // C_GROUP
    even_in = 7 * HA * LANES
    tabs_a = rope_tables(S, A_DIM)
    tabs_c = rope_tables(S, C_DIM)
    half_a, half_c = tabs_a[3], tabs_c[3]
    lbounds = _lower_bounds(P["hgrn_lb_raw"])
    a_kw = dict(n_heads=HA, group=1, q_blk=0, k_blk=HA, v_blk=2 * HA, blk_per_row=7 * HA, scale=A_DIM ** -0.5,
                half=half_a)
    b_kw = dict(S=S, n_heads=HB, q_blk=3 * HA, f_blk=3 * HA + HB, i_blk=3 * HA + 2 * HB, g_blk=3 * HA + 3 * HB)
    c_kw = dict(L=S, dil=1, n_heads=HQ, group=C_GROUP, q_blk=0, k_blk=HQ, v_blk=HQ + HKV, blk_per_row=HQ + 2 * HKV,
                max_dist=C_WINDOW - 1, scale=C_DIM ** -0.5, half=half_c)

    def a_tabs(dil):
        return tuple(t.reshape(S // dil, dil * LANES) for t in tabs_a[:3])

    saved = []
    for layer in range(depth):
        sv = {"x0": x}
        idx = layer // 2
        h = rmsnorm_fwd(x, P["norm_mix_g"][layer], name="norm_mix_fwd")
        sv["h"] = h
        if layer % 2 == 0:
            proj = matmul(h, P["even_w_in"], b_layer=idx, out_dtypes=(F32,), name="even_in_fwd")
            parts = []
            for window, dil in A_BRANCHES:
                L = S // dil
                res = band_attention_fwd(proj.reshape(L, dil * even_in), a_tabs(dil), L=L, dil=dil,
                                         max_dist=window // dil, normalize=False, name=f"dilated_fwd_d{dil}", **a_kw)
                parts.append(tuple(r.reshape(S, HA * LANES) for r in res))
            oa, lse = merge_branches(parts, name="dilated_merge")
            lb_e = lbounds[idx].reshape(HB, 1, LANES)
            ob, o_raw, states = hgrn_fwd(proj, lb_e, P["hgrn_norm_g"][idx], name="hgrn_fwd", **b_kw)
            mixed = jnp.concatenate([oa, ob], axis=1).astype(BF16)
            sv.update(proj=proj, oa=oa, lse=lse, o_raw=o_raw, states=states, mixed=mixed, lb=lb_e)
            x = matmul(mixed, P["even_w_out"], b_layer=idx, extras=(x,), epilogue=lambda acc, r: (acc + r,),
                       out_dtypes=(F32,), name="even_out_fwd")
        else:
            proj = matmul(h, P["odd_w_qkv"], b_layer=idx, extras=(P["odd_b_qkv"][idx].reshape(1, -1),),
                          epilogue=lambda acc, b: (acc + b,), out_dtypes=(F32,), name="odd_qkv_fwd")
            sink = jnp.broadcast_to(P["odd_sinks"][idx].reshape(HQ, 1, 1), (HQ, 1, LANES))
            o, lse = band_attention_fwd(proj, tabs_c[:3], sink=sink, normalize=True, name="swa_fwd", **c_kw)
            sv.update(proj=proj, o=o, lse=lse, sink=sink)
            x = matmul(o, P["odd_w_o"], b_layer=idx, extras=(P["odd_b_o"][idx].reshape(1, D), x),
                       epilogue=lambda acc, b, r: (acc + b + r,), out_dtypes=(F32,), name="odd_out_fwd")
        sv["x1"] = x
        h2 = rmsnorm_fwd(x, P["norm_mlp_g"][layer], name="norm_mlp_fwd")
        u = matmul(h2, P["mlp_w1"], b_layer=layer, out_dtypes=(BF16,), name="mlp_up_fwd")
        x = matmul(u, P["mlp_w2"], b_layer=layer, a_fn=_relu2, extras=(x,), epilogue=lambda acc, r: (acc + r,),
                   out_dtypes=(F32,), name="mlp_down_fwd")
        sv.update(h2=h2, u=u)
        saved.append(sv)

    loss, dx, dgf = final_norm_loss(x, P["final_norm_g"], target, name="final_norm_loss")

    G = {k: [None] * depth for k in ("norm_mix_g", "norm_mlp_g")}
    for k in ("hgrn_lb", "hgrn_norm_g"):
        G[k] = [None] * ((depth + 1) // 2)
    for k in ("odd_w_qkv", "odd_b_qkv", "odd_sinks", "odd_w_o", "odd_b_o"):
        G[k] = [None] * (depth // 2)
    G["final_norm_g"] = dgf.reshape(D)
    B = dict(gbufs)

    def wgrad(a, d, key, layer, sharding, name, a_fn=None):
        B[key] = matmul(a, d, trans_a=True, a_fn=a_fn, out_dtypes=(BF16,), into=(B[key], layer, sharding), name=name)

    for layer in reversed(range(depth)):
        sv = saved[layer]
        idx = layer // 2
        du = matmul(dx, P["mlp_w2"], b_layer=layer, trans_b=True, extras=(sv["u"],),
                    epilogue=lambda acc, u: (acc * (2.0 * jnp.maximum(u.astype(F32), 0.0)),),
                    out_dtypes=(BF16,), name="mlp_down_bwd_x")
        wgrad(sv["u"], dx, "mlp_w2", layer, "rows", "mlp_down_bwd_w", a_fn=_relu2)
        wgrad(sv["h2"], du, "mlp_w1", layer, "cols", "mlp_up_bwd_w")
        dh2 = matmul(du, P["mlp_w1"], b_layer=layer, trans_b=True, out_dtypes=(F32,), name="mlp_up_bwd_x")
        dx, dg = rmsnorm_bwd(sv["x1"], P["norm_mlp_g"][layer], dh2, dx, name="norm_mlp_bwd")
        G["norm_mlp_g"][layer] = dg.reshape(D)
        if layer % 2 == 0:
            dmixed = matmul(dx, P["even_w_out"], b_layer=idx, trans_b=True, out_dtypes=(F32,), name="even_out_bwd_x")
            wgrad(sv["mixed"], dx, "even_w_out", idx, "rows", "even_out_bwd_w")
            acc = None
            for window, dil in A_BRANCHES:
                L = S // dil
                rs = lambda a: a.reshape(L, -1)
                acc = band_attention_bwd(rs(sv["proj"]), a_tabs(dil), rs(sv["oa"]), rs(sv["lse"]), rs(dmixed),
                                         L=L, dil=dil, max_dist=window // dil, do_bpr=2 * HA,
                                         acc=None if acc is None else tuple(rs(a) for a in acc),
                                         name=f"dilated_bwd_d{dil}", **a_kw)
                acc = tuple(a.reshape(S, HA * LANES) for a in acc)
            dqb, dfb, dib, dgb, dlb, dng = hgrn_bwd(sv["proj"], sv["lb"], P["hgrn_norm_g"][idx], sv["o_raw"],
                                                    sv["states"], dmixed, dout_blk=HA, name="hgrn_bwd", **b_kw)
            G["hgrn_lb"][idx] = dlb.reshape(HB * LANES)
            G["hgrn_norm_g"][idx] = dng.reshape(LANES)
            dproj = jnp.concatenate(list(acc) + [dqb, dfb, dib, dgb], axis=1).astype(BF16)
            wgrad(sv["h"], dproj, "even_w_in", idx, "cols", "even_in_bwd_w")
            dh = matmul(dproj, P["even_w_in"], b_layer=idx, trans_b=True, out_dtypes=(F32,), name="even_in_bwd_x")
        else:
            do = matmul(dx, P["odd_w_o"], b_layer=idx, trans_b=True, out_dtypes=(F32,), name="odd_out_bwd_x")
            G["odd_b_o"][idx] = colsum(dx, name="odd_out_bwd_b").reshape(D)
            G["odd_w_o"][idx] = matmul(sv["o"], dx, trans_a=True, out_dtypes=(BF16,), name="odd_out_bwd_w")
            dq, dk, dv, dsink = band_attention_bwd(sv["proj"], tabs_c[:3], sv["o"], sv["lse"], do, sink=sv["sink"],
                                                   do_bpr=HQ, name="swa_bwd", **c_kw)
            G["odd_sinks"][idx] = dsink[:, 0, 0]
            dproj = jnp.concatenate([dq, dk, dv], axis=1)
            G["odd_b_qkv"][idx] = colsum(dproj, name="odd_qkv_bwd_b").reshape(-1)
            dproj = dproj.astype(BF16)
            G["odd_w_qkv"][idx] = matmul(sv["h"], dproj, trans_a=True, out_dtypes=(BF16,), name="odd_qkv_bwd_w")
            dh = matmul(dproj, P["odd_w_qkv"], b_layer=idx, trans_b=True, out_dtypes=(F32,), name="odd_qkv_bwd_x")
        dx, dg = rmsnorm_bwd(sv["x0"], P["norm_mix_g"][layer], dh, dx, name="norm_mix_bwd")
        G["norm_mix_g"][layer] = dg.reshape(D)

    grads = {k: jnp.stack(v) if isinstance(v, list) else v for k, v in G.items()}
    _, lb_vjp = jax.vjp(_lower_bounds, P["hgrn_lb_raw"])
    grads["hgrn_lb_raw"] = lb_vjp(grads.pop("hgrn_lb"))[0]
    grads.update(B)
    return loss, dx, grads


def grad_buffers(depth, D):
    HA = D // 2 // A_DIM
    n_even = (depth + 1) // 2
    shapes = {"even_w_in": (n_even, D // 2, 7 * HA * LANES // 4), "even_w_out": (n_even, D // 8, D),
              "mlp_w1": (depth, D // 2, D), "mlp_w2": (depth, D // 2, D)}
    return {k: lax.empty((4, 2) + s, BF16) for k, s in shapes.items()}


def buffer_to_full(buf, sharding):
    _, _, n, rh, cs = buf.shape
    if sharding == "cols":
        return jnp.transpose(buf, (2, 1, 3, 0, 4)).reshape(n, 2 * rh, 4 * cs)
    return jnp.transpose(buf, (2, 0, 1, 3, 4)).reshape(n, 8 * rh, cs)


def full_to_buffer(full, sharding):
    n, M, N = full.shape
    if sharding == "cols":
        return jnp.transpose(full.reshape(n, 2, M // 2, 4, N // 4), (3, 1, 0, 2, 4))
    return jnp.transpose(full.reshape(n, 4, 2, M // 8, N), (1, 2, 0, 3, 4))


MESH = pl.DeviceIdType.MESH
ANY = pl.BlockSpec(memory_space=pl.ANY)
SHARDING = {"even_w_in": "cols", "even_w_out": "rows", "odd_w_qkv": "cols", "odd_w_o": "rows",
            "mlp_w1": "cols", "mlp_w2": "rows"}
BIG = tuple(SHARDING)


def _mesh_pos():
    return lax.axis_index("x"), lax.axis_index("y"), lax.axis_index("c")


def _other_chips(x, y):
    return [(1 - x, y), (x, 1 - y), (1 - x, 1 - y)]


def cast_bf16(w, *, name, tr=256):
    n, R, C = w.shape
    tr = _tile(R, tr)

    def body(w_ref, o_ref):
        o_ref[...] = w_ref[...].astype(BF16)

    spec = pl.BlockSpec((None, tr, C), lambda l, i: (l, i, 0))
    return pl.pallas_call(
        body, name=name, grid=(n, R // tr), in_specs=[spec], out_specs=spec,
        out_shape=jax.ShapeDtypeStruct(w.shape, BF16), compiler_params=_cparams("parallel", "parallel"),
    )(w)


def gather_weights(shards, shardings, *, name):
    T = len(shards)

    def full_shape(t):
        n, R, C = shards[t].shape
        return (n, R, 4 * C) if shardings[t] == "cols" else (n, 4 * R, C)

    def body(*refs):
        in_refs, out_refs = refs[:T], refs[T:2 * T]
        send_sems, recv_sems, local_sems = refs[2 * T:]
        x, y, c = _mesh_pos()
        chips = _other_chips(x, y)
        sibling = (x, y, 1 - c)

        def win(t, s, hf):
            n, R, C = shards[t].shape
            rh = R // 2
            if shardings[t] == "cols":
                rows = pl.ds(0, R) if hf is None else pl.ds(hf * rh, rh)
                return out_refs[t].at[:, rows, pl.ds(s * C, C)]
            rows = pl.ds(s * R, R) if hf is None else pl.ds(s * R + hf * rh, rh)
            return out_refs[t].at[:, rows, :]

        def half_of_shard(t):
            rh = shards[t].shape[1] // 2
            return in_refs[t].at[:, pl.ds(c * rh, rh), :]

        def remote(t, k, src, dst, to):
            return pltpu.make_async_remote_copy(src_ref=src, dst_ref=dst, send_sem=send_sems.at[t, k],
                                                recv_sem=recv_sems.at[t, k], device_id=to, device_id_type=MESH)

        mine = [pltpu.make_async_copy(in_refs[t], win(t, 2 * x + y, None), local_sems.at[t]) for t in range(T)]
        for cp in mine:
            cp.start()
        first = [remote(t, j, half_of_shard(t), win(t, 2 * x + y, c), (*chip, c))
                 for t in range(T) for j, chip in enumerate(chips)]
        for cp in first:
            cp.start()
        passed = []
        for t in range(T):
            for j, (cx, cy) in enumerate(chips):
                landed = win(t, 2 * cx + cy, c)
                remote(t, j, landed, landed, (cx, cy, c)).wait_recv()
                fwd = remote(t, 3 + j, landed, landed, sibling)
                fwd.start()
                passed.append(fwd)
        for t in range(T):
            for j, (cx, cy) in enumerate(chips):
                other = win(t, 2 * cx + cy, 1 - c)
                remote(t, 3 + j, other, other, sibling).wait_recv()
        for cp in first + passed:
            cp.wait_send()
        for cp in mine:
            cp.wait()

    return pl.pallas_call(
        body, name=name,
        in_specs=[ANY] * T, out_specs=[ANY] * T,
        out_shape=[jax.ShapeDtypeStruct(full_shape(t), shards[t].dtype) for t in range(T)],
        scratch_shapes=[pltpu.SemaphoreType.DMA((T, 6)), pltpu.SemaphoreType.DMA((T, 6)),
                        pltpu.SemaphoreType.DMA((T,))],
    )(*shards)


def exchange_halves(bufs, *, name):
    T = len(bufs)

    def body(*refs):
        in_refs, out_refs = refs[:T], refs[T:2 * T]
        send_sems, recv_sems = refs[2 * T:]
        x, y, c = _mesh_pos()
        cps = [pltpu.make_async_remote_copy(src_ref=in_refs[t].at[:, pl.ds(1 - c, 1)], dst_ref=out_refs[t],
                                            send_sem=send_sems.at[t], recv_sem=recv_sems.at[t],
                                            device_id=(x, y, 1 - c), device_id_type=MESH) for t in range(T)]
        for cp in cps:
            cp.start()
        for cp in cps:
            cp.wait()

    return pl.pallas_call(
        body, name=name, in_specs=[ANY] * T, out_specs=[ANY] * T,
        out_shape=[jax.ShapeDtypeStruct((4, 1) + b.shape[2:], b.dtype) for b in bufs],
        scratch_shapes=[pltpu.SemaphoreType.DMA((T,)), pltpu.SemaphoreType.DMA((T,))],
    )(*bufs)


def pair_sum(buf, got, c, *, name, tr=512):
    _, _, n, rh, C = buf.shape
    tr = _tile(rh, tr)

    def body(c_ref, a_ref, b_ref, o_ref):
        o_ref[...] = (a_ref[...].astype(F32) + b_ref[...].astype(F32)).astype(o_ref.dtype)

    grid_spec = pltpu.PrefetchScalarGridSpec(
        num_scalar_prefetch=1, grid=(4, n, rh // tr),
        in_specs=[pl.BlockSpec((None, None, None, tr, C), lambda s, l, i, c_ref: (s, c_ref[0], l, i, 0)),
                  pl.BlockSpec((None, None, None, tr, C), lambda s, l, i, c_ref: (s, 0, l, i, 0))],
        out_specs=pl.BlockSpec((None, None, tr, C), lambda s, l, i, c_ref: (s, l, i, 0)))
    return pl.pallas_call(
        body, name=name, grid_spec=grid_spec, out_shape=jax.ShapeDtypeStruct((4, n, rh, C), BF16),
        compiler_params=_cparams("parallel", "parallel", "parallel"),
    )(c.reshape(1).astype(jnp.int32), buf, got)


def scatter_to_owners(sums, *, name):
    T = len(sums)

    def body(*refs):
        in_refs, out_refs = refs[:T], refs[T:2 * T]
        send_sems, recv_sems = refs[2 * T:]
        x, y, c = _mesh_pos()
        chips = _other_chips(x, y)
        cps = [pltpu.make_async_remote_copy(src_ref=in_refs[t].at[pl.ds(2 * cx + cy, 1)],
                                            dst_ref=out_refs[t].at[pl.ds(j, 1)],
                                            send_sem=send_sems.at[t, j], recv_sem=recv_sems.at[t, j],
                                            device_id=(cx, cy, c), device_id_type=MESH)
               for t in range(T) for j, (cx, cy) in enumerate(chips)]
        for cp in cps:
            cp.start()
        for cp in cps:
            cp.wait()

    return pl.pallas_call(
        body, name=name, in_specs=[ANY] * T, out_specs=[ANY] * T,
        out_shape=[jax.ShapeDtypeStruct((3,) + s.shape[1:], s.dtype) for s in sums],
        scratch_shapes=[pltpu.SemaphoreType.DMA((T, 3)), pltpu.SemaphoreType.DMA((T, 3))],
    )(*sums)


def owner_sum(sums, got, s, *, name, tr=512):
    _, n, rh, C = sums.shape
    tr = _tile(rh, tr)

    def body(s_ref, a_ref, b0_ref, b1_ref, b2_ref, o_ref):
        o_ref[...] = ((a_ref[...].astype(F32) + b0_ref[...].astype(F32)) + b1_ref[...].astype(F32)) \
            + b2_ref[...].astype(F32)

    def got_spec(j):
        return pl.BlockSpec((None, None, tr, C), lambda l, i, s_ref: (j, l, i, 0))

    grid_spec = pltpu.PrefetchScalarGridSpec(
        num_scalar_prefetch=1, grid=(n, rh // tr),
        in_specs=[pl.BlockSpec((None, None, tr, C), lambda l, i, s_ref: (s_ref[0], l, i, 0)),
                  got_spec(0), got_spec(1), got_spec(2)],
        out_specs=pl.BlockSpec((None, tr, C), lambda l, i, s_ref: (l, i, 0)))
    return pl.pallas_call(
        body, name=name, grid_spec=grid_spec, out_shape=jax.ShapeDtypeStruct((n, rh, C), F32),
        compiler_params=_cparams("parallel", "parallel"),
    )(s.reshape(1).astype(jnp.int32), sums, got, got, got)


def join_halves(halves, *, name):
    T = len(halves)

    def body(*refs):
        in_refs, out_refs = refs[:T], refs[T:2 * T]
        send_sems, recv_sems, local_sems = refs[2 * T:]
        x, y, c = _mesh_pos()

        def win(t, hf):
            rh = halves[t].shape[1]
            return out_refs[t].at[:, pl.ds(hf * rh, rh), :]

        mine = [pltpu.make_async_copy(in_refs[t], win(t, c), local_sems.at[t]) for t in range(T)]
        cps = [pltpu.make_async_remote_copy(src_ref=in_refs[t], dst_ref=win(t, c), send_sem=send_sems.at[t],
                                            recv_sem=recv_sems.at[t], device_id=(x, y, 1 - c), device_id_type=MESH)
               for t in range(T)]
        for cp in mine + cps:
            cp.start()
        for t in range(T):
            other = win(t, 1 - c)
            pltpu.make_async_remote_copy(src_ref=other, dst_ref=other, send_sem=send_sems.at[t],
                                         recv_sem=recv_sems.at[t], device_id=(x, y, 1 - c),
                                         device_id_type=MESH).wait_recv()
        for cp in cps:
            cp.wait_send()
        for cp in mine:
            cp.wait()

    return pl.pallas_call(
        body, name=name, in_specs=[ANY] * T, out_specs=[ANY] * T,
        out_shape=[jax.ShapeDtypeStruct((h.shape[0], 2 * h.shape[1], h.shape[2]), h.dtype) for h in halves],
        scratch_shapes=[pltpu.SemaphoreType.DMA((T,)), pltpu.SemaphoreType.DMA((T,)), pltpu.SemaphoreType.DMA((T,))],
    )(*halves)


def allreduce_small(v, *, name):
    rows = v.shape[0]

    def body(v_ref, out_ref, buf, send_sems, recv_sems):
        x, y, c = _mesh_pos()
        me = 4 * x + 2 * y + c
        flips = [(dx, dy, dc) for dx in (0, 1) for dy in (0, 1) for dc in (0, 1)][1:]

        def peer(f):
            return tuple(1 - p if d else p for d, p in zip(f, (x, y, c)))

        cps = []
        for k, f in enumerate(flips):
            px, py, pc = peer(f)
            cps.append(pltpu.make_async_remote_copy(src_ref=v_ref, dst_ref=buf.at[me], send_sem=send_sems.at[k],
                                                    recv_sem=recv_sems.at[k], device_id=(px, py, pc),
                                                    device_id_type=MESH))
        for cp in cps:
            cp.start()
        buf[me] = v_ref[...]
        for k, f in enumerate(flips):
            px, py, pc = peer(f)
            slot = buf.at[4 * px + 2 * py + pc]
            pltpu.make_async_remote_copy(src_ref=slot, dst_ref=slot, send_sem=send_sems.at[k],
                                         recv_sem=recv_sems.at[k], device_id=(px, py, pc),
                                         device_id_type=MESH).wait_recv()
        for cp in cps:
            cp.wait_send()
        acc = buf[0]
        for i in range(1, 8):
            acc = acc + buf[i]
        out_ref[...] = acc

    vm = pl.BlockSpec(memory_space=pltpu.VMEM)
    return pl.pallas_call(
        body, name=name, in_specs=[vm], out_specs=vm,
        out_shape=jax.ShapeDtypeStruct(v.shape, F32),
        scratch_shapes=[pltpu.VMEM((8, rows, LANES), F32), pltpu.SemaphoreType.DMA((7,)),
                        pltpu.SemaphoreType.DMA((7,))],
    )(v)


def _adam_math(w, m, v, g):
    m = ADAM_B1 * m + (1.0 - ADAM_B1) * g
    v = ADAM_B2 * v + (1.0 - ADAM_B2) * (g * g)
    m_hat = m / (1.0 - ADAM_B1 ** ADAM_STEP)
    v_hat = v / (1.0 - ADAM_B2 ** ADAM_STEP)
    delta = -ADAM_LR * (m_hat / (jnp.sqrt(v_hat) + ADAM_EPS) + ADAM_WD * w)
    return delta, m, v


def adamw(w, m, v, g, *, name, tr=256):
    def body(w_ref, m_ref, v_ref, g_ref, d_ref, nm_ref, nv_ref):
        d, nm, nv = _adam_math(w_ref[...], m_ref[...], v_ref[...], g_ref[...])
        d_ref[...] = d
        nm_ref[...] = nm
        nv_ref[...] = nv

    shape = jax.ShapeDtypeStruct(w.shape, F32)
    if w.ndim == 2:
        return pl.pallas_call(body, name=name, out_shape=[shape] * 3)(w, m, v, g)
    n, R, C = w.shape
    tr = _tile(R, tr)
    spec = pl.BlockSpec((None, tr, C), lambda l, i: (l, i, 0))
    return pl.pallas_call(
        body, name=name, grid=(n, R // tr), in_specs=[spec] * 4, out_specs=[spec] * 3, out_shape=[shape] * 3,
        compiler_params=_cparams("parallel", "parallel"),
    )(w, m, v, g)


def _pack(parts):
    flat = jnp.concatenate([p.reshape(-1).astype(F32) for p in parts])
    size = -(-flat.shape[0] // (8 * LANES)) * (8 * LANES)
    return jnp.pad(flat, (0, size - flat.shape[0])).reshape(size // LANES, LANES)


def _unpack(block, shapes):
    flat = block.reshape(-1)
    out, pos = [], 0
    for shp in shapes:
        size = int(np.prod(shp))
        out.append(flat[pos:pos + size].reshape(shp))
        pos += size
    return out


SMALL = ("norm_mix_g", "norm_mlp_g", "final_norm_g", "hgrn_lb_raw", "hgrn_norm_g", "odd_sinks")
WEIGHTS = ("norm_mix_g", "norm_mlp_g", "final_norm_g", "even_w_in", "even_w_out", "hgrn_lb_raw", "hgrn_norm_g",
           "odd_w_qkv", "odd_b_qkv", "odd_sinks", "odd_w_o", "odd_b_o", "mlp_w1", "mlp_w2")


def kernel(x, norm_mix_g, norm_mlp_g, final_norm_g, even_w_in, even_w_out, hgrn_lb_raw, hgrn_norm_g, odd_w_qkv, odd_b_qkv, odd_sinks, odd_w_o, odd_b_o, mlp_w1, mlp_w2, loss_target, m_norm_mix_g, m_norm_mlp_g, m_final_norm_g, m_even_w_in, m_even_w_out, m_hgrn_lb_raw, m_hgrn_norm_g, m_odd_w_qkv, m_odd_b_qkv, m_odd_sinks, m_odd_w_o, m_odd_b_o, m_mlp_w1, m_mlp_w2, v_norm_mix_g, v_norm_mlp_g, v_final_norm_g, v_even_w_in, v_even_w_out, v_hgrn_lb_raw, v_hgrn_norm_g, v_odd_w_qkv, v_odd_b_qkv, v_odd_sinks, v_odd_w_o, v_odd_b_o, v_mlp_w1, v_mlp_w2):
    args = locals()
    W = {k: args[k] for k in WEIGHTS}
    M = {k: args["m_" + k] for k in WEIGHTS}
    V = {k: args["v_" + k] for k in WEIGHTS}
    _, S, D = x.shape
    depth = norm_mix_g.shape[0]
    HQ = D // C_DIM
    HKV = HQ // C_GROUP
    xi, yi, ci = _mesh_pos()
    shard = 2 * xi + yi

    full = dict(zip(BIG, gather_weights([cast_bf16(W[k], name="cast_" + k) for k in BIG],
                                        [SHARDING[k] for k in BIG], name="gather_weights")))
    n_odd, bq = odd_b_qkv.shape
    bo = odd_b_o.shape[1]
    keep = (ci == 0).astype(F32)
    pieces = [lax.dynamic_update_slice(jnp.zeros((n_odd, 4 * bq), F32), odd_b_qkv * keep, (0, shard * bq)),
              lax.dynamic_update_slice(jnp.zeros((n_odd, 4 * bo), F32), odd_b_o * keep, (0, shard * bo))]
    b_qkv_full, b_o_full = _unpack(allreduce_small(_pack(pieces), name="gather_biases"),
                                   [(n_odd, 4 * bq), (n_odd, 4 * bo)])
    P = {k: W[k] for k in SMALL}
    P.update(even_w_in=full["even_w_in"], even_w_out=full["even_w_out"], mlp_w1=full["mlp_w1"],
             mlp_w2=full["mlp_w2"],
             odd_w_qkv=_pad_heads_cols(full["odd_w_qkv"], HQ + 2 * HKV),
             odd_b_qkv=_pad_heads_cols(b_qkv_full, HQ + 2 * HKV),
             odd_w_o=jnp.swapaxes(_pad_heads_cols(jnp.swapaxes(full["odd_w_o"], 1, 2), HQ), 1, 2),
             odd_b_o=b_o_full)

    loss, dx, G = local_step(x[0], loss_target[0], P, grad_buffers(depth, D))

    G["odd_w_qkv"] = full_to_buffer(_unpad_heads_cols(G["odd_w_qkv"], HQ + 2 * HKV), "cols")
    G["odd_w_o"] = full_to_buffer(
        jnp.swapaxes(_unpad_heads_cols(jnp.swapaxes(G["odd_w_o"], 1, 2), HQ), 1, 2), "rows")
    bufs = [G[k] for k in BIG]
    got = exchange_halves(bufs, name="exchange_halves")
    sums = [pair_sum(b, g, ci, name="pair_sum_" + k) for k, b, g in zip(BIG, bufs, got)]
    got = scatter_to_owners(sums, name="scatter_to_owners")
    halves = [owner_sum(s, g, shard, name="owner_sum_" + k) for k, s, g in zip(BIG, sums, got)]
    grads = dict(zip(BIG, join_halves(halves, name="join_halves")))

    small_keys = SMALL + ("odd_b_qkv", "odd_b_o")
    G["odd_b_qkv"] = _unpad_heads_cols(G["odd_b_qkv"], HQ + 2 * HKV)
    small_shapes = [G[k].shape for k in small_keys]
    small = _unpack(allreduce_small(_pack([G[k] for k in small_keys]), name="reduce_small"), small_shapes)
    grads.update(zip(small_keys, small))
    grads["odd_b_qkv"] = lax.dynamic_slice(grads["odd_b_qkv"], (0, shard * bq), (n_odd, bq))
    grads["odd_b_o"] = lax.dynamic_slice(grads["odd_b_o"], (0, shard * bo), (n_odd, bo))

    deltas, new_m, new_v = {}, {}, {}
    for k in WEIGHTS:
        w2 = (lambda a: a.reshape(1, -1)) if W[k].ndim == 1 else (lambda a: a)
        d, nm, nv = adamw(w2(W[k]), w2(M[k]), w2(V[k]), w2(grads[k]), name="adamw_" + k)
        deltas[k], new_m[k], new_v[k] = (a.reshape(W[k].shape) for a in (d, nm, nv))

    loss = lax.psum(loss[0, 0], ("x", "y", "c"))
    return (loss, dx[None], *[grads[k] for k in WEIGHTS], *[deltas[k] for k in WEIGHTS],
            *[new_m[k] for k in WEIGHTS], *[new_v[k] for k in WEIGHTS])
```

```python
import functools
import math

import jax
import jax.numpy as jnp
import numpy as np
from jax import lax
from jax.experimental import pallas as pl
from jax.experimental.pallas import tpu as pltpu

F32 = jnp.float32
BF16 = jnp.bfloat16

NORM_EPS = 1e-5
ROPE_THETA = 500000.0
ROPE_FRACTION = 4
LANES = 128
BAND = 128
A_DIM = 128
A_BRANCHES = ((128, 1), (512, 4), (2048, 16))
B_DIM = 128
B_CHUNK = 64
B_SUB = 16
C_DIM = 64
C_GROUP = 8
C_WINDOW = 128

ADAM_LR = 0.001
ADAM_B1 = 0.9
ADAM_B2 = 0.999
ADAM_EPS = 1e-08
ADAM_WD = 0.01
ADAM_STEP = 10

VMEM_LIMIT = 56 * 1024 * 1024


def _cparams(*sem):
    return pltpu.CompilerParams(dimension_semantics=tuple(sem), vmem_limit_bytes=VMEM_LIMIT)


def _tile(n, want):
    if n <= want:
        return n
    t = want - want % LANES
    while n % t:
        t -= LANES
    assert t > 0, (n, want)
    return t


def matmul(a, b, *, trans_a=False, trans_b=False, out_dtypes, a_fn=None, epilogue=None, extras=(), name,
           b_layer=None, into=None, tm=1024, tn=1024, tk=2048):
    if trans_a:
        K, M = a.shape
    else:
        M, K = a.shape
    b_mat = b.shape[1:] if b_layer is not None else b.shape
    if trans_b:
        N, K2 = b_mat
    else:
        K2, N = b_mat
    assert K == K2, (a.shape, b.shape)
    if into is not None:
        rh, cs = into[0].shape[3:]
        tm, tn = _tile(rh, tm), _tile(cs, tn)
    tm, tn, tk = _tile(M, tm), _tile(N, tn), _tile(K, tk)
    nk = K // tk
    n_extra = len(extras)
    n_out = len(out_dtypes)

    def body(*refs):
        a_ref, b_ref = refs[0], refs[1]
        extra_refs = refs[2:2 + n_extra]
        out_refs = refs[-1 - n_out:-1]
        acc_ref = refs[-1]
        k = pl.program_id(2)
        at = a_ref[...]
        if a_fn is not None:
            at = a_fn(at.astype(F32))
        at = at.astype(BF16)
        bt = b_ref[...].astype(BF16)
        dims = (((0,) if trans_a else (1,), (1,) if trans_b else (0,)), ((), ()))
        prod = lax.dot_general(at, bt, dims, preferred_element_type=F32)

        def finish(acc):
            ex = [r[...] for r in extra_refs]
            outs = epilogue(acc, *ex) if epilogue is not None else (acc,)
            for o_ref, o in zip(out_refs, outs):
                o_ref[...] = o.astype(o_ref.dtype)

        if nk == 1:
            finish(prod)
            return

        @pl.when(k == 0)
        def _():
            acc_ref[...] = prod

        @pl.when((k > 0) & (k < nk - 1))
        def _():
            acc_ref[...] += prod

        @pl.when(k == nk - 1)
        def _():
            finish(acc_ref[...] + prod)

    a_spec = pl.BlockSpec((tk, tm), lambda i, j, k: (k, i)) if trans_a else pl.BlockSpec((tm, tk), lambda i, j, k: (i, k))
    if b_layer is None:
        b_spec = (pl.BlockSpec((tn, tk), lambda i, j, k: (j, k)) if trans_b
                  else pl.BlockSpec((tk, tn), lambda i, j, k: (k, j)))
    else:
        b_spec = (pl.BlockSpec((None, tn, tk), lambda i, j, k: (b_layer, j, k)) if trans_b
                  else pl.BlockSpec((None, tk, tn), lambda i, j, k: (b_layer, k, j)))
    e_specs = []
    for e in extras:
        if e.shape == (1, N):
            e_specs.append(pl.BlockSpec((1, tn), lambda i, j, k: (0, j)))
        else:
            assert e.shape == (M, N), (e.shape, M, N)
            e_specs.append(pl.BlockSpec((tm, tn), lambda i, j, k: (i, j)))
    args = [a, b, *extras]
    in_specs = [a_spec, b_spec] + e_specs
    aliases = {}
    if into is None:
        out_specs = [pl.BlockSpec((tm, tn), lambda i, j, k: (i, j)) for _ in out_dtypes]
        out_shape = [jax.ShapeDtypeStruct((M, N), dt) for dt in out_dtypes]
    else:
        buf, layer, sharding = into
        assert n_out == 1 and buf.dtype == out_dtypes[0]
        index = grad_buffer_index(buf.shape, (M, N), sharding, layer, tm, tn)
        out_specs = [pl.BlockSpec((None, None, None, tm, tn), lambda i, j, k: index(i, j))]
        out_shape = [jax.ShapeDtypeStruct(buf.shape, buf.dtype)]
        in_specs.append(pl.BlockSpec(memory_space=pl.ANY))
        args.append(buf)
        aliases = {len(args) - 1: 0}
    outs = pl.pallas_call(
        body, name=name,
        grid=(M // tm, N // tn, nk),
        in_specs=in_specs, out_specs=out_specs, out_shape=out_shape,
        input_output_aliases=aliases,
        scratch_shapes=[pltpu.VMEM((tm, tn), F32)],
        compiler_params=_cparams("parallel", "parallel", "arbitrary"),
    )(*args)
    return outs[0] if n_out == 1 else tuple(outs)


def grad_buffer_index(buf_shape, mat_shape, sharding, layer, tm, tn):
    _, _, _, rh, cs = buf_shape
    M, N = mat_shape
    ib, jb = rh // tm, cs // tn
    if sharding == "cols":
        assert (M, N) == (2 * rh, 4 * cs), (buf_shape, mat_shape)
        return lambda i, j: (j // jb, i // ib, layer, i % ib, j % jb)
    assert sharding == "rows" and (M, N) == (8 * rh, cs), (buf_shape, mat_shape)
    return lambda i, j: (i // (2 * ib), (i // ib) % 2, layer, i % ib, j)


def rmsnorm_fwd(x, g, *, name, tr=256):
    S, D = x.shape
    tr = _tile(S, tr)

    def body(x_ref, g_ref, h_ref):
        xv = x_ref[...]
        rstd = lax.rsqrt(jnp.mean(xv * xv, axis=-1, keepdims=True) + NORM_EPS)
        h_ref[...] = (xv * rstd * g_ref[...]).astype(h_ref.dtype)

    return pl.pallas_call(
        body, name=name, grid=(S // tr,),
        in_specs=[pl.BlockSpec((tr, D), lambda i: (i, 0)), pl.BlockSpec((1, D), lambda i: (0, 0))],
        out_specs=pl.BlockSpec((tr, D), lambda i: (i, 0)),
        out_shape=jax.ShapeDtypeStruct((S, D), BF16),
        compiler_params=_cparams("parallel"),
    )(x, g.reshape(1, D))


def _rms_bwd_rows(xv, gv, dh):
    rstd = lax.rsqrt(jnp.mean(xv * xv, axis=-1, keepdims=True) + NORM_EPS)
    xhat = xv * rstd
    dxhat = dh * gv
    dx = rstd * (dxhat - xhat * jnp.mean(dxhat * xhat, axis=-1, keepdims=True))
    return dx, dh * xhat


def rmsnorm_bwd(x, g, dh, dres, *, name, tr=256):
    S, D = x.shape
    tr = _tile(S, tr)

    def body(x_ref, g_ref, dh_ref, dres_ref, dx_ref, dg_ref):
        @pl.when(pl.program_id(0) == 0)
        def _():
            dg_ref[...] = jnp.zeros_like(dg_ref)

        dx, dgr = _rms_bwd_rows(x_ref[...], g_ref[...], dh_ref[...].astype(F32))
        dx_ref[...] = dres_ref[...] + dx
        dg_ref[...] += jnp.sum(dgr, axis=0, keepdims=True)

    row = pl.BlockSpec((tr, D), lambda i: (i, 0))
    vec = pl.BlockSpec((1, D), lambda i: (0, 0))
    return pl.pallas_call(
        body, name=name, grid=(S // tr,),
        in_specs=[row, vec, row, row],
        out_specs=[row, vec],
        out_shape=[jax.ShapeDtypeStruct((S, D), F32), jax.ShapeDtypeStruct((1, D), F32)],
        compiler_params=_cparams("arbitrary"),
    )(x, g.reshape(1, D), dh, dres)


def final_norm_loss(x, g, target, *, name, tr=256):
    S, D = x.shape
    tr = _tile(S, tr)

    def body(x_ref, g_ref, t_ref, loss_ref, dx_ref, dg_ref):
        @pl.when(pl.program_id(0) == 0)
        def _():
            dg_ref[...] = jnp.zeros_like(dg_ref)
            loss_ref[...] = jnp.zeros_like(loss_ref)

        xv, gv = x_ref[...], g_ref[...]
        rstd = lax.rsqrt(jnp.mean(xv * xv, axis=-1, keepdims=True) + NORM_EPS)
        err = xv * rstd * gv - t_ref[...]
        part = 0.5 * jnp.sum(jnp.mean(err * err, axis=-1, keepdims=True), axis=0, keepdims=True)
        loss_ref[...] += jnp.broadcast_to(part, loss_ref.shape)
        dx, dgr = _rms_bwd_rows(xv, gv, err * (1.0 / D))
        dx_ref[...] = dx
        dg_ref[...] += jnp.sum(dgr, axis=0, keepdims=True)

    row = pl.BlockSpec((tr, D), lambda i: (i, 0))
    vec = pl.BlockSpec((1, D), lambda i: (0, 0))
    return pl.pallas_call(
        body, name=name, grid=(S // tr,),
        in_specs=[row, vec, row],
        out_specs=[pl.BlockSpec((8, LANES), lambda i: (0, 0)), row, vec],
        out_shape=[jax.ShapeDtypeStruct((8, LANES), F32), jax.ShapeDtypeStruct((S, D), F32),
                   jax.ShapeDtypeStruct((1, D), F32)],
        compiler_params=_cparams("arbitrary"),
    )(x, g.reshape(1, D), target)


def colsum(a, *, name, tr=256):
    S, N = a.shape
    tr = _tile(S, tr)

    def body(a_ref, o_ref):
        @pl.when(pl.program_id(0) == 0)
        def _():
            o_ref[...] = jnp.zeros_like(o_ref)

        o_ref[...] += jnp.sum(a_ref[...].astype(F32), axis=0, keepdims=True)

    return pl.pallas_call(
        body, name=name, grid=(S // tr,),
        in_specs=[pl.BlockSpec((tr, N), lambda i: (i, 0))],
        out_specs=pl.BlockSpec((1, N), lambda i: (0, 0)),
        out_shape=jax.ShapeDtypeStruct((1, N), F32),
        compiler_params=_cparams("arbitrary"),
    )(a)


def rope_tables(seq, head_dim):
    rot = head_dim // ROPE_FRACTION
    half = rot // 2
    inv_freq = 1.0 / (ROPE_THETA ** (jnp.arange(0, rot, 2, dtype=F32) / rot))
    ang = jnp.arange(seq, dtype=F32)[:, None] * inv_freq[None, :]
    cos, sin = jnp.cos(ang), jnp.sin(ang)
    ones = jnp.ones((seq, LANES - 2 * half), F32)
    zeros = jnp.zeros((seq, LANES - 2 * half), F32)
    zh = jnp.zeros((seq, half), F32)
    c = jnp.concatenate([cos, cos, ones], axis=1)
    sa = jnp.concatenate([-sin, zh, zeros], axis=1)
    sb = jnp.concatenate([zh, sin, zeros], axis=1)
    return c, sa, sb, half


def _rope(x, c, sa, sb, half):
    return x * c + pltpu.roll(x, LANES - half, 1) * sa + pltpu.roll(x, half, 1) * sb


def _rope_bwd(d, c, sa, sb, half):
    return d * c + pltpu.roll(d * sa, half, 1) + pltpu.roll(d * sb, LANES - half, 1)


def _band_masks(n, max_dist):
    qi = lax.broadcasted_iota(jnp.int32, (BAND, BAND), 0)
    kj = lax.broadcasted_iota(jnp.int32, (BAND, BAND), 1)
    cur = kj <= qi
    prev = ((kj >= qi) if max_dist == BAND else (kj > qi)) & (n > 0)
    return prev, cur


def _dot_nt(a, b):
    return lax.dot_general(a, b, (((1,), (1,)), ((), ())), preferred_element_type=F32)


def _dot_tn(a, b):
    return lax.dot_general(a, b, (((0,), (0,)), ((), ())), preferred_element_type=F32)


def _dot(a, b):
    return jnp.dot(a, b, preferred_element_type=F32)


def band_attention_fwd(src, tabs, *, L, dil, n_heads, group, q_blk, k_blk, v_blk, blk_per_row, max_dist, scale,
                       half, sink=None, normalize, name):
    c_t, sa_t, sb_t = tabs
    nb = L // BAND
    W = dil * n_heads * LANES
    use_sink = sink is not None

    def body(*refs):
        q_ref, k_ref, v_ref, c_ref, sa_ref, sb_ref = refs[:6]
        pos = 6
        if use_sink:
            sink_ref = refs[pos]
            pos += 1
        o_ref, m_ref, l_ref = refs[pos:pos + 3] if not normalize else (refs[pos], refs[pos + 1], None)
        kr_ref, vb_ref = refs[-2], refs[-1]

        def prep(n, carry):
            rows = pl.ds(pl.multiple_of(n * BAND, BAND), BAND)
            kr_ref[rows, :] = _rope(k_ref[rows, :], c_ref[rows, :], sa_ref[rows, :], sb_ref[rows, :], half).astype(BF16)
            vb_ref[rows, :] = v_ref[rows, :].astype(BF16)
            return carry

        lax.fori_loop(0, nb, prep, 0)

        def step(n, carry):
            rows = pl.ds(pl.multiple_of(n * BAND, BAND), BAND)
            prow = pl.ds(pl.multiple_of(jnp.maximum(n - 1, 0) * BAND, BAND), BAND)
            q = _rope(q_ref[rows, :], c_ref[rows, :], sa_ref[rows, :], sb_ref[rows, :], half).astype(BF16)
            mp, mc = _band_masks(n, max_dist)
            sp = jnp.where(mp, _dot_nt(q, kr_ref[prow, :]) * scale, -jnp.inf)
            sc = jnp.where(mc, _dot_nt(q, kr_ref[rows, :]) * scale, -jnp.inf)
            m = jnp.maximum(jnp.max(sp, axis=-1, keepdims=True), jnp.max(sc, axis=-1, keepdims=True))
            if use_sink:
                sk = sink_ref[:, 0:1]
                m = jnp.maximum(m, sk)
            pp = jnp.exp(sp - m)
            pc = jnp.exp(sc - m)
            l = jnp.sum(pp, axis=-1, keepdims=True) + jnp.sum(pc, axis=-1, keepdims=True)
            if use_sink:
                l = l + jnp.exp(sk - m)
            num = _dot(pp.astype(BF16), vb_ref[prow, :]) + _dot(pc.astype(BF16), vb_ref[rows, :])
            if normalize:
                o_ref[rows, :] = num / l
                m_ref[rows, :] = jnp.broadcast_to(m + jnp.log(l), (BAND, LANES))
            else:
                o_ref[rows, :] = num
                m_ref[rows, :] = jnp.broadcast_to(m, (BAND, LANES))
                l_ref[rows, :] = jnp.broadcast_to(l, (BAND, LANES))
            return carry

        lax.fori_loop(0, nb, step, 0)

    def col(off, div):
        return pl.BlockSpec((L, LANES), lambda r, h: (0, r * blk_per_row + off + h // div))

    tab = pl.BlockSpec((L, LANES), lambda r, h: (0, r))
    out = pl.BlockSpec((L, LANES), lambda r, h: (0, r * n_heads + h))
    in_specs = [col(q_blk, 1), col(k_blk, group), col(v_blk, group), tab, tab, tab]
    args = [src, src, src, c_t, sa_t, sb_t]
    if use_sink:
        in_specs.append(pl.BlockSpec((None, 1, LANES), lambda r, h: (h, 0, 0)))
        args.append(sink)
    n_out = 2 if normalize else 3
    return pl.pallas_call(
        body, name=name, grid=(dil, n_heads),
        in_specs=in_specs,
        out_specs=[out] * n_out,
        out_shape=[jax.ShapeDtypeStruct((L, W), F32)] * n_out,
        scratch_shapes=[pltpu.VMEM((L, LANES), BF16), pltpu.VMEM((L, LANES), BF16)],
        compiler_params=_cparams("parallel", "arbitrary"),
    )(*args)


def band_attention_bwd(src, tabs, o, lse, do, *, L, dil, n_heads, group, q_blk, k_blk, v_blk, blk_per_row, max_dist,
                       scale, half, do_bpr, sink=None, acc=None, name):
    c_t, sa_t, sb_t = tabs
    nb = L // BAND
    n_kv = n_heads // group
    use_sink = sink is not None
    use_acc = acc is not None
    assert not (use_acc and group != 1)

    def body(*refs):
        q_ref, k_ref, v_ref, c_ref, sa_ref, sb_ref, o_ref, lse_ref, do_ref = refs[:9]
        pos = 9
        if use_sink:
            sink_ref = refs[pos]
            pos += 1
        if use_acc:
            aq_ref, ak_ref, av_ref = refs[pos:pos + 3]
            pos += 3
        dq_ref, dk_ref, dv_ref = refs[pos:pos + 3]
        pos += 3
        if use_sink:
            dsink_ref = refs[pos]
        kr_ref, vb_ref, dka_ref, dva_ref = refs[-4:]
        h = pl.program_id(1)

        def prep(n, carry):
            rows = pl.ds(pl.multiple_of(n * BAND, BAND), BAND)
            kr_ref[rows, :] = _rope(k_ref[rows, :], c_ref[rows, :], sa_ref[rows, :], sb_ref[rows, :], half).astype(BF16)
            vb_ref[rows, :] = v_ref[rows, :].astype(BF16)
            return carry

        lax.fori_loop(0, nb, prep, 0)

        @pl.when(h % group == 0)
        def _():
            dka_ref[...] = jnp.zeros_like(dka_ref)
            dva_ref[...] = jnp.zeros_like(dva_ref)

        def step(n, dsk):
            rows = pl.ds(pl.multiple_of(n * BAND, BAND), BAND)
            prow = pl.ds(pl.multiple_of(jnp.maximum(n - 1, 0) * BAND, BAND), BAND)
            cv, sav, sbv = c_ref[rows, :], sa_ref[rows, :], sb_ref[rows, :]
            q = _rope(q_ref[rows, :], cv, sav, sbv, half).astype(BF16)
            dov = do_ref[rows, :]
            lsev = lse_ref[rows, 0:1]
            delta = jnp.sum(dov * o_ref[rows, :], axis=-1, keepdims=True)
            dob = dov.astype(BF16)
            mp, mc = _band_masks(n, max_dist)
            kp, kc, vp, vc = kr_ref[prow, :], kr_ref[rows, :], vb_ref[prow, :], vb_ref[rows, :]
            pp = jnp.exp(jnp.where(mp, _dot_nt(q, kp) * scale, -jnp.inf) - lsev)
            pc = jnp.exp(jnp.where(mc, _dot_nt(q, kc) * scale, -jnp.inf) - lsev)
            dsp = (pp * (_dot_nt(dob, vp) - delta) * scale).astype(BF16)
            dsc = (pc * (_dot_nt(dob, vc) - delta) * scale).astype(BF16)
            dq = _rope_bwd(_dot(dsp, kp) + _dot(dsc, kc), cv, sav, sbv, half)
            if use_acc:
                dq = dq + aq_ref[rows, :]
            dq_ref[rows, :] = dq
            dka_ref[prow, :] += _dot_tn(dsp, q)
            dka_ref[rows, :] += _dot_tn(dsc, q)
            dva_ref[prow, :] += _dot_tn(pp.astype(BF16), dob)
            dva_ref[rows, :] += _dot_tn(pc.astype(BF16), dob)
            if use_sink:
                dsk = dsk - jnp.sum(jnp.exp(sink_ref[:, 0:1] - lsev) * delta, axis=0, keepdims=True)
            return dsk

        dsk = lax.fori_loop(0, nb, step, jnp.zeros((1, 1), F32))
        if use_sink:
            dsink_ref[...] = jnp.broadcast_to(dsk, dsink_ref.shape)

        @pl.when(h % group == group - 1)
        def _():
            def fin(n, carry):
                rows = pl.ds(pl.multiple_of(n * BAND, BAND), BAND)
                dk = _rope_bwd(dka_ref[rows, :], c_ref[rows, :], sa_ref[rows, :], sb_ref[rows, :], half)
                dv = dva_ref[rows, :]
                if use_acc:
                    dk = dk + ak_ref[rows, :]
                    dv = dv + av_ref[rows, :]
                dk_ref[rows, :] = dk
                dv_ref[rows, :] = dv
                return carry

            lax.fori_loop(0, nb, fin, 0)

    def col(off, div):
        return pl.BlockSpec((L, LANES), lambda r, h: (0, r * blk_per_row + off + h // div))

    tab = pl.BlockSpec((L, LANES), lambda r, h: (0, r))
    qo = pl.BlockSpec((L, LANES), lambda r, h: (0, r * n_heads + h))
    kvo = pl.BlockSpec((L, LANES), lambda r, h: (0, r * n_kv + h // group))
    dospec = pl.BlockSpec((L, LANES), lambda r, h: (0, r * do_bpr + h))
    in_specs = [col(q_blk, 1), col(k_blk, group), col(v_blk, group), tab, tab, tab, qo, qo, dospec]
    args = [src, src, src, c_t, sa_t, sb_t, o, lse, do]
    if use_sink:
        in_specs.append(pl.BlockSpec((None, 1, LANES), lambda r, h: (h, 0, 0)))
        args.append(sink)
    if use_acc:
        in_specs += [qo, kvo, kvo]
        args += list(acc)
    out_specs = [qo, kvo, kvo]
    out_shape = [jax.ShapeDtypeStruct((L, dil * n_heads * LANES), F32),
                 jax.ShapeDtypeStruct((L, dil * n_kv * LANES), F32),
                 jax.ShapeDtypeStruct((L, dil * n_kv * LANES), F32)]
    if use_sink:
        out_specs.append(pl.BlockSpec((None, 1, LANES), lambda r, h: (h, 0, 0)))
        out_shape.append(jax.ShapeDtypeStruct((n_heads, 1, LANES), F32))
    return pl.pallas_call(
        body, name=name, grid=(dil, n_heads),
        in_specs=in_specs, out_specs=out_specs, out_shape=out_shape,
        scratch_shapes=[pltpu.VMEM((L, LANES), BF16), pltpu.VMEM((L, LANES), BF16),
                        pltpu.VMEM((L, LANES), F32), pltpu.VMEM((L, LANES), F32)],
        compiler_params=_cparams("parallel", "arbitrary"),
    )(*args)


def merge_branches(parts, *, name, tr=256):
    S, W = parts[0][0].shape
    tr = _tile(S, tr)
    nbr = len(parts)

    def body(*refs):
        ins, (o_ref, lse_ref) = refs[:3 * nbr], refs[3 * nbr:]
        nums = [ins[3 * i][...] for i in range(nbr)]
        ms = [ins[3 * i + 1][...] for i in range(nbr)]
        ls = [ins[3 * i + 2][...] for i in range(nbr)]
        mx = functools.reduce(jnp.maximum, ms)
        ws = [jnp.exp(m - mx) for m in ms]
        num = sum(w * n for w, n in zip(ws, nums))
        den = sum(w * l for w, l in zip(ws, ls))
        o_ref[...] = num / den
        lse_ref[...] = mx + jnp.log(den)

    row = pl.BlockSpec((tr, W), lambda i: (i, 0))
    flat = [a for p in parts for a in p]
    return pl.pallas_call(
        body, name=name, grid=(S // tr,),
        in_specs=[row] * len(flat), out_specs=[row, row],
        out_shape=[jax.ShapeDtypeStruct((S, W), F32)] * 2,
        compiler_params=_cparams("parallel"),
    )(*flat)


def _sigmoid(x):
    return 1.0 / (1.0 + jnp.exp(-x))


def _tri(n, lower):
    r = lax.broadcasted_iota(jnp.int32, (n, n), 0)
    c = lax.broadcasted_iota(jnp.int32, (n, n), 1)
    return ((c <= r) if lower else (c >= r)).astype(F32)


def _dot_exact(a, b, trans_a=False, trans_b=False):
    dims = (((0,) if trans_a else (1,), (1,) if trans_b else (0,)), ((), ()))
    return lax.dot_general(a, b, dims, preferred_element_type=F32, precision=lax.Precision.HIGHEST)


def _hgrn_gates(qb, fb, lb):
    sq = _sigmoid(qb)
    q = qb * sq * (B_DIM ** -0.5)
    sf = _sigmoid(fb)
    gate = lb + (1.0 - lb) * sf
    return q, 1.0 - gate, gate, sq, sf


def _hgrn_intra_fwd(q_ref, k_ref, b_ref):
    C, n_sub = B_CHUNK, B_CHUNK // B_SUB
    b_all, k_all = b_ref[...], k_ref[...]
    srow = lax.broadcasted_iota(jnp.int32, (C, LANES), 0)
    lane = lax.broadcasted_iota(jnp.int32, (B_SUB, C), 1)
    trow = lax.broadcasted_iota(jnp.int32, (B_SUB, LANES), 0)
    blocks = []
    for i in range(n_sub):
        r0 = i * B_SUB
        qi, bi = q_ref[pl.ds(r0, B_SUB), :], b_ref[pl.ds(r0, B_SUB), :]
        if i == 0:
            a_i = jnp.zeros((B_SUB, C), F32)
        else:
            ref_b = b_ref[pl.ds(r0 - 1, 1), :]
            qt = (qi * jnp.exp(bi - ref_b)).astype(BF16)
            kt = jnp.where(srow < r0, k_all * jnp.exp(jnp.minimum(ref_b - b_all, 0.0)), 0.0).astype(BF16)
            a_i = _dot_nt(qt, kt)
        for sl in range(B_SUB):
            s = r0 + sl
            e = jnp.where(trow >= sl, jnp.exp(jnp.minimum(bi - b_ref[pl.ds(s, 1), :], 0.0)), 0.0)
            colv = jnp.sum(qi * k_ref[pl.ds(s, 1), :] * e, axis=-1, keepdims=True)
            a_i = a_i + jnp.where(lane == s, colv, 0.0)
        blocks.append(a_i)
    return jnp.concatenate(blocks, axis=0)


def hgrn_fwd(src, lb, norm_g, *, S, n_heads, q_blk, f_blk, i_blk, g_blk, name):
    nc = S // B_CHUNK

    def body(qb_ref, fb_ref, ib_ref, gb_ref, lb_ref, ng_ref, out_ref, o_ref, st_ref, state_ref, q_s, k_s, b_s):
        state_ref[...] = jnp.zeros_like(state_ref)
        tril = _tri(B_CHUNK, True)

        def step(c, carry):
            rows = pl.ds(pl.multiple_of(c * B_CHUNK, B_CHUNK), B_CHUNK)
            q, k, gate, _, _ = _hgrn_gates(qb_ref[rows, :], fb_ref[rows, :], lb_ref[...])
            b = _dot_exact(tril, jnp.log(gate))
            vf = ib_ref[rows, :]
            v = vf.astype(BF16)
            q_s[...], k_s[...], b_s[...] = q, k, b
            st = state_ref[...]
            st_ref[c] = st
            a = _hgrn_intra_fwd(q_s, k_s, b_s)
            o = _dot(a.astype(BF16), v) + _dot_nt((q * jnp.exp(b)).astype(BF16), st.astype(BF16))
            b_last = b_s[pl.ds(B_CHUNK - 1, 1), :]
            state_ref[...] = st * jnp.exp(b_last) + _dot_exact(vf, k * jnp.exp(b_last - b), trans_a=True)
            o_ref[rows, :] = o
            rstd = lax.rsqrt(jnp.mean(o * o, axis=-1, keepdims=True) + NORM_EPS)
            gb = gb_ref[rows, :]
            out_ref[rows, :] = o * rstd * ng_ref[...] * (gb * _sigmoid(gb))
            return carry

        lax.fori_loop(0, nc, step, 0)

    def col(off):
        return pl.BlockSpec((S, LANES), lambda h: (0, off + h))

    hv = pl.BlockSpec((None, 1, LANES), lambda h: (h, 0, 0))
    return pl.pallas_call(
        body, name=name, grid=(n_heads,),
        in_specs=[col(q_blk), col(f_blk), col(i_blk), col(g_blk), hv, pl.BlockSpec((1, LANES), lambda h: (0, 0))],
        out_specs=[col(0), col(0), pl.BlockSpec((None, nc, LANES, LANES), lambda h: (h, 0, 0, 0))],
        out_shape=[jax.ShapeDtypeStruct((S, n_heads * LANES), F32), jax.ShapeDtypeStruct((S, n_heads * LANES), F32),
                   jax.ShapeDtypeStruct((n_heads, nc, LANES, LANES), F32)],
        scratch_shapes=[pltpu.VMEM((LANES, LANES), F32)] + [pltpu.VMEM((B_CHUNK, LANES), F32)] * 3,
        compiler_params=_cparams("parallel"),
    )(src, src, src, src, lb, norm_g.reshape(1, LANES))


def hgrn_bwd(src, lb, norm_g, o, states, dout, *, S, n_heads, q_blk, f_blk, i_blk, g_blk, dout_blk, name):
    nc = S // B_CHUNK
    C, n_sub = B_CHUNK, B_CHUNK // B_SUB

    def body(qb_ref, fb_ref, ib_ref, gb_ref, lb_ref, ng_ref, o_ref, st_ref, dout_ref,
             dqb_ref, dfb_ref, dib_ref, dgb_ref, dlb_ref, dng_ref, dstate_ref, q_s, k_s, b_s, dq_s, dk_s):
        @pl.when(pl.program_id(0) == 0)
        def _():
            dng_ref[...] = jnp.zeros_like(dng_ref)

        dstate_ref[...] = jnp.zeros_like(dstate_ref)
        tril = _tri(C, True)
        triu = _tri(C, False)
        lbv, ngv = lb_ref[...], ng_ref[...]
        srow = lax.broadcasted_iota(jnp.int32, (C, LANES), 0)
        lane = lax.broadcasted_iota(jnp.int32, (B_SUB, C), 1)
        trow = lax.broadcasted_iota(jnp.int32, (B_SUB, LANES), 0)
        causal = lax.broadcasted_iota(jnp.int32, (C, C), 1) <= lax.broadcasted_iota(jnp.int32, (C, C), 0)

        def step(ci, carry):
            dlog_carry, dlb_acc, dng_acc = carry
            c = nc - 1 - ci
            rows = pl.ds(pl.multiple_of(c * C, C), C)
            qb, fb, gb = qb_ref[rows, :], fb_ref[rows, :], gb_ref[rows, :]
            q, k, gate, sq, sf = _hgrn_gates(qb, fb, lbv)
            b = _dot_exact(tril, jnp.log(gate))
            vf = ib_ref[rows, :]
            v = vf.astype(BF16)
            q_s[...], k_s[...], b_s[...] = q, k, b
            st = st_ref[c]
            dst = dstate_ref[...]

            ov = o_ref[rows, :]
            dout = dout_ref[rows, :]
            rstd = lax.rsqrt(jnp.mean(ov * ov, axis=-1, keepdims=True) + NORM_EPS)
            xhat = ov * rstd
            sg = _sigmoid(gb)
            dy = dout * (gb * sg)
            dgb_ref[rows, :] = dout * (xhat * ngv) * (sg * (1.0 + gb * (1.0 - sg)))
            dng_acc = dng_acc + jnp.sum(dy * xhat, axis=0, keepdims=True)
            dxhat = dy * ngv
            do = rstd * (dxhat - xhat * jnp.mean(dxhat * xhat, axis=-1, keepdims=True))
            dob = do.astype(BF16)

            eb = jnp.exp(b)
            b_last = b_s[pl.ds(C - 1, 1), :]
            ebl = jnp.exp(b_last - b)
            a = _hgrn_intra_fwd(q_s, k_s, b_s)
            da = jnp.where(causal, _dot_exact(do, vf, trans_b=True), 0.0)
            dv = _dot_tn(a.astype(BF16), dob) + _dot_nt((k * ebl).astype(BF16), dst.astype(BF16))
            dq_s[...] = _dot_exact(do, st) * eb
            dk_s[...] = _dot_exact(vf, dst) * ebl
            dstate_ref[...] = dst * jnp.exp(b_last) + _dot_exact(do, q * eb, trans_a=True)
            b_all, k_all = b, k
            for i in range(n_sub):
                r0 = i * B_SUB
                blk = pl.ds(r0, B_SUB)
                qi, bi = q_s[blk, :], b_s[blk, :]
                da_i = da[r0:r0 + B_SUB, :]
                dq_i = jnp.zeros((B_SUB, LANES), F32)
                if i > 0:
                    ref_b = b_s[pl.ds(r0 - 1, 1), :]
                    eq = jnp.exp(bi - ref_b)
                    ek = jnp.where(srow < r0, jnp.exp(jnp.minimum(ref_b - b_all, 0.0)), 0.0)
                    da_off = jnp.where(lane < r0, da_i, 0.0)
                    dq_i = _dot_exact(da_off, k_all * ek) * eq
                    dk_s[...] += _dot_exact(da_off, qi * eq, trans_a=True) * ek
                for sl in range(B_SUB):
                    s = r0 + sl
                    e = jnp.where(trow >= sl, jnp.exp(jnp.minimum(bi - b_s[pl.ds(s, 1), :], 0.0)), 0.0)
                    dac = jnp.sum(jnp.where(lane == s, da_i, 0.0), axis=-1, keepdims=True)
                    dq_i = dq_i + dac * k_s[pl.ds(s, 1), :] * e
                    dk_s[pl.ds(s, 1), :] += jnp.sum(dac * qi * e, axis=0, keepdims=True)
                dq_s[blk, :] += dq_i
            dq, dk = dq_s[...], dk_s[...]
            db = q * dq - k * dk
            dlog = _dot_exact(triu, db) + dlog_carry
            dlog_carry = dlog_carry + jnp.sum(db, axis=0, keepdims=True)
            dgate = dlog / gate - dk
            dqb_ref[rows, :] = dq * (B_DIM ** -0.5) * (sq * (1.0 + qb * (1.0 - sq)))
            dfb_ref[rows, :] = dgate * (1.0 - lbv) * sf * (1.0 - sf)
            dib_ref[rows, :] = dv
            dlb_acc = dlb_acc + jnp.sum(dgate * (1.0 - sf), axis=0, keepdims=True)
            return dlog_carry, dlb_acc, dng_acc

        z = jnp.zeros((1, LANES), F32)
        _, dlb_acc, dng_acc = lax.fori_loop(0, nc, step, (z, z, z))
        dlb_ref[...] = dlb_acc
        dng_ref[...] += dng_acc

    def col(off):
        return pl.BlockSpec((S, LANES), lambda h: (0, off + h))

    hv = pl.BlockSpec((None, 1, LANES), lambda h: (h, 0, 0))
    vec = pl.BlockSpec((1, LANES), lambda h: (0, 0))
    full = jax.ShapeDtypeStruct((S, n_heads * LANES), F32)
    return pl.pallas_call(
        body, name=name, grid=(n_heads,),
        in_specs=[col(q_blk), col(f_blk), col(i_blk), col(g_blk), hv, vec, col(0),
                  pl.BlockSpec((None, nc, LANES, LANES), lambda h: (h, 0, 0, 0)), col(dout_blk)],
        out_specs=[col(0), col(0), col(0), col(0), hv, vec],
        out_shape=[full, full, full, full, jax.ShapeDtypeStruct((n_heads, 1, LANES), F32),
                   jax.ShapeDtypeStruct((1, LANES), F32)],
        scratch_shapes=[pltpu.VMEM((LANES, LANES), F32)] + [pltpu.VMEM((B_CHUNK, LANES), F32)] * 5,
        compiler_params=_cparams("arbitrary"),
    )(src, src, src, src, lb, norm_g.reshape(1, LANES), o, states, dout)


def _pad_heads_cols(w, n_heads):
    lead = w.shape[:-1]
    w = w.reshape(lead + (n_heads, C_DIM))
    w = jnp.pad(w, [(0, 0)] * len(lead) + [(0, 0), (0, LANES - C_DIM)])
    return w.reshape(lead + (n_heads * LANES,))


def _unpad_heads_cols(w, n_heads):
    lead = w.shape[:-1]
    return w.reshape(lead + (n_heads, LANES))[..., :C_DIM].reshape(lead + (n_heads * C_DIM,))


def _lower_bounds(lb_raw):
    lb_soft = jax.nn.softmax(lb_raw.astype(F32), axis=0)
    return jnp.cumsum(lb_soft, axis=0) - lb_soft[0:1]


def _relu2(u):
    r = jnp.maximum(u, 0.0)
    return r * r


def local_step(x, target, P, gbufs):
    S, D = x.shape
    depth = P["norm_mix_g"].shape[0]
    HA = D // 2 // A_DIM
    HB = HA
    HQ = D // C_DIM
    HKV = HQ // C_GROUP
    even_in = 7 * HA * LANES
    tabs_a = rope_tables(S, A_DIM)
    tabs_c = rope_tables(S, C_DIM)
    half_a, half_c = tabs_a[3], tabs_c[3]
    lbounds = _lower_bounds(P["hgrn_lb_raw"])
    a_kw = dict(n_heads=HA, group=1, q_blk=0, k_blk=HA, v_blk=2 * HA, blk_per_row=7 * HA, scale=A_DIM ** -0.5,
                half=half_a)
    b_kw = dict(S=S, n_heads=HB, q_blk=3 * HA, f_blk=3 * HA + HB, i_blk=3 * HA + 2 * HB, g_blk=3 * HA + 3 * HB)
    c_kw = dict(L=S, dil=1, n_heads=HQ, group=C_GROUP, q_blk=0, k_blk=HQ, v_blk=HQ + HKV, blk_per_row=HQ + 2 * HKV,
                max_dist=C_WINDOW - 1, scale=C_DIM ** -0.5, half=half_c)

    def a_tabs(dil):
        return tuple(t.reshape(S // dil, dil * LANES) for t in tabs_a[:3])

    saved = []
    for layer in range(depth):
        sv = {"x0": x}
        idx = layer // 2
        h = rmsnorm_fwd(x, P["norm_mix_g"][layer], name="norm_mix_fwd")
        sv["h"] = h
        if layer % 2 == 0:
            proj = matmul(h, P["even_w_in"], b_layer=idx, out_dtypes=(F32,), name="even_in_fwd")
            parts = []
            for window, dil in A_BRANCHES:
                L = S // dil
                res = band_attention_fwd(proj.reshape(L, dil * even_in), a_tabs(dil), L=L, dil=dil,
                                         max_dist=window // dil, normalize=False, name=f"dilated_fwd_d{dil}", **a_kw)
                parts.append(tuple(r.reshape(S, HA * LANES) for r in res))
            oa, lse = merge_branches(parts, name="dilated_merge")
            lb_e = lbounds[idx].reshape(HB, 1, LANES)
            ob, o_raw, states = hgrn_fwd(proj, lb_e, P["hgrn_norm_g"][idx], name="hgrn_fwd", **b_kw)
            mixed = jnp.concatenate([oa, ob], axis=1).astype(BF16)
            sv.update(proj=proj, oa=oa, lse=lse, o_raw=o_raw, states=states, mixed=mixed, lb=lb_e)
            x = matmul(mixed, P["even_w_out"], b_layer=idx, extras=(x,), epilogue=lambda acc, r: (acc + r,),
                       out_dtypes=(F32,), name="even_out_fwd")
        else:
            proj = matmul(h, P["odd_w_qkv"], b_layer=idx, extras=(P["odd_b_qkv"][idx].reshape(1, -1),),
                          epilogue=lambda acc, b: (acc + b,), out_dtypes=(F32,), name="odd_qkv_fwd")
            sink = jnp.broadcast_to(P["odd_sinks"][idx].reshape(HQ, 1, 1), (HQ, 1, LANES))
            o, lse = band_attention_fwd(proj, tabs_c[:3], sink=sink, normalize=True, name="swa_fwd", **c_kw)
            sv.update(proj=proj, o=o, lse=lse, sink=sink)
            x = matmul(o, P["odd_w_o"], b_layer=idx, extras=(P["odd_b_o"][idx].reshape(1, D), x),
                       epilogue=lambda acc, b, r: (acc + b + r,), out_dtypes=(F32,), name="odd_out_fwd")
        sv["x1"] = x
        h2 = rmsnorm_fwd(x, P["norm_mlp_g"][layer], name="norm_mlp_fwd")
        u = matmul(h2, P["mlp_w1"], b_layer=layer, out_dtypes=(BF16,), name="mlp_up_fwd")
        x = matmul(u, P["mlp_w2"], b_layer=layer, a_fn=_relu2, extras=(x,), epilogue=lambda acc, r: (acc + r,),
                   out_dtypes=(F32,), name="mlp_down_fwd")
        sv.update(h2=h2, u=u)
        saved.append(sv)

    loss, dx, dgf = final_norm_loss(x, P["final_norm_g"], target, name="final_norm_loss")

    G = {k: [None] * depth for k in ("norm_mix_g", "norm_mlp_g")}
    for k in ("hgrn_lb", "hgrn_norm_g"):
        G[k] = [None] * ((depth + 1) // 2)
    for k in ("odd_w_qkv", "odd_b_qkv", "odd_sinks", "odd_w_o", "odd_b_o"):
        G[k] = [None] * (depth // 2)
    G["final_norm_g"] = dgf.reshape(D)
    B = dict(gbufs)

    def wgrad(a, d, key, layer, sharding, name, a_fn=None):
        B[key] = matmul(a, d, trans_a=True, a_fn=a_fn, out_dtypes=(BF16,), into=(B[key], layer, sharding), name=name)

    for layer in reversed(range(depth)):
        sv = saved[layer]
        idx = layer // 2
        du = matmul(dx, P["mlp_w2"], b_layer=layer, trans_b=True, extras=(sv["u"],),
                    epilogue=lambda acc, u: (acc * (2.0 * jnp.maximum(u.astype(F32), 0.0)),),
                    out_dtypes=(BF16,), name="mlp_down_bwd_x")
        wgrad(sv["u"], dx, "mlp_w2", layer, "rows", "mlp_down_bwd_w", a_fn=_relu2)
        wgrad(sv["h2"], du, "mlp_w1", layer, "cols", "mlp_up_bwd_w")
        dh2 = matmul(du, P["mlp_w1"], b_layer=layer, trans_b=True, out_dtypes=(F32,), name="mlp_up_bwd_x")
        dx, dg = rmsnorm_bwd(sv["x1"], P["norm_mlp_g"][layer], dh2, dx, name="norm_mlp_bwd")
        G["norm_mlp_g"][layer] = dg.reshape(D)
        if layer % 2 == 0:
            dmixed = matmul(dx, P["even_w_out"], b_layer=idx, trans_b=True, out_dtypes=(F32,), name="even_out_bwd_x")
            wgrad(sv["mixed"], dx, "even_w_out", idx, "rows", "even_out_bwd_w")
            acc = None
            for window, dil in A_BRANCHES:
                L = S // dil
                rs = lambda a: a.reshape(L, -1)
                acc = band_attention_bwd(rs(sv["proj"]), a_tabs(dil), rs(sv["oa"]), rs(sv["lse"]), rs(dmixed),
                                         L=L, dil=dil, max_dist=window // dil, do_bpr=2 * HA,
                                         acc=None if acc is None else tuple(rs(a) for a in acc),
                                         name=f"dilated_bwd_d{dil}", **a_kw)
                acc = tuple(a.reshape(S, HA * LANES) for a in acc)
            dqb, dfb, dib, dgb, dlb, dng = hgrn_bwd(sv["proj"], sv["lb"], P["hgrn_norm_g"][idx], sv["o_raw"],
                                                    sv["states"], dmixed, dout_blk=HA, name="hgrn_bwd", **b_kw)
            G["hgrn_lb"][idx] = dlb.reshape(HB * LANES)
            G["hgrn_norm_g"][idx] = dng.reshape(LANES)
            dproj = jnp.concatenate(list(acc) + [dqb, dfb, dib, dgb], axis=1).astype(BF16)
            wgrad(sv["h"], dproj, "even_w_in", idx, "cols", "even_in_bwd_w")
            dh = matmul(dproj, P["even_w_in"], b_layer=idx, trans_b=True, out_dtypes=(F32,), name="even_in_bwd_x")
        else:
            do = matmul(dx, P["odd_w_o"], b_layer=idx, trans_b=True, out_dtypes=(F32,), name="odd_out_bwd_x")
            G["odd_b_o"][idx] = colsum(dx, name="odd_out_bwd_b").reshape(D)
            G["odd_w_o"][idx] = matmul(sv["o"], dx, trans_a=True, out_dtypes=(BF16,), name="odd_out_bwd_w")
            dq, dk, dv, dsink = band_attention_bwd(sv["proj"], tabs_c[:3], sv["o"], sv["lse"], do, sink=sv["sink"],
                                                   do_bpr=HQ, name="swa_bwd", **c_kw)
            G["odd_sinks"][idx] = dsink[:, 0, 0]
            dproj = jnp.concatenate([dq, dk, dv], axis=1)
            G["odd_b_qkv"][idx] = colsum(dproj, name="odd_qkv_bwd_b").reshape(-1)
            dproj = dproj.astype(BF16)
            G["odd_w_qkv"][idx] = matmul(sv["h"], dproj, trans_a=True, out_dtypes=(BF16,), name="odd_qkv_bwd_w")
            dh = matmul(dproj, P["odd_w_qkv"], b_layer=idx, trans_b=True, out_dtypes=(F32,), name="odd_qkv_bwd_x")
        dx, dg = rmsnorm_bwd(sv["x0"], P["norm_mix_g"][layer], dh, dx, name="norm_mix_bwd")
        G["norm_mix_g"][layer] = dg.reshape(D)

    grads = {k: jnp.stack(v) if isinstance(v, list) else v for k, v in G.items()}
    _, lb_vjp = jax.vjp(_lower_bounds, P["hgrn_lb_raw"])
    grads["hgrn_lb_raw"] = lb_vjp(grads.pop("hgrn_lb"))[0]
    grads.update(B)
    return loss, dx, grads


def grad_buffers(depth, D):
    HA = D // 2 // A_DIM
    n_even = (depth + 1) // 2
    shapes = {"even_w_in": (n_even, D // 2, 7 * HA * LANES // 4), "even_w_out": (n_even, D // 8, D),
              "mlp_w1": (depth, D // 2, D), "mlp_w2": (depth, D // 2, D)}
    return {k: lax.empty((4, 2) + s, BF16) for k, s in shapes.items()}


def buffer_to_full(buf, sharding):
    _, _, n, rh, cs = buf.shape
    if sharding == "cols":
        return jnp.transpose(buf, (2, 1, 3, 0, 4)).reshape(n, 2 * rh, 4 * cs)
    return jnp.transpose(buf, (2, 0, 1, 3, 4)).reshape(n, 8 * rh, cs)


def full_to_buffer(full, sharding):
    n, M, N = full.shape
    if sharding == "cols":
        return jnp.transpose(full.reshape(n, 2, M // 2, 4, N // 4), (3, 1, 0, 2, 4))
    return jnp.transpose(full.reshape(n, 4, 2, M // 8, N), (1, 2, 0, 3, 4))


MESH = pl.DeviceIdType.MESH
ANY = pl.BlockSpec(memory_space=pl.ANY)
SHARDING = {"even_w_in": "cols", "even_w_out": "rows", "odd_w_qkv": "cols", "odd_w_o": "rows",
            "mlp_w1": "cols", "mlp_w2": "rows"}
BIG = tuple(SHARDING)


def _mesh_pos():
    return lax.axis_index("x"), lax.axis_index("y"), lax.axis_index("c")


def _other_chips(x, y):
    return [(1 - x, y), (x, 1 - y), (1 - x, 1 - y)]


def cast_into_full(w, sharding, s, *, name, tr=256):
    n, R, C = w.shape
    tr = _tile(R, tr)
    nr = R // tr

    def body(s_ref, w_ref, o_ref):
        o_ref[...] = w_ref[...].astype(BF16)

    if sharding == "cols":
        full, out_map = (n, R, 4 * C), (lambda l, i, s_ref: (l, i, s_ref[0]))
    else:
        full, out_map = (n, 4 * R, C), (lambda l, i, s_ref: (l, s_ref[0] * nr + i, 0))
    grid_spec = pltpu.PrefetchScalarGridSpec(
        num_scalar_prefetch=1, grid=(n, nr),
        in_specs=[pl.BlockSpec((None, tr, C), lambda l, i, s_ref: (l, i, 0))],
        out_specs=pl.BlockSpec((None, tr, C), out_map))
    return pl.pallas_call(
        body, name=name, grid_spec=grid_spec, out_shape=jax.ShapeDtypeStruct(full, BF16),
        compiler_params=_cparams("parallel", "parallel"),
    )(s.reshape(1).astype(jnp.int32), w)


def gather_weights(fulls, shardings, *, name):
    T = len(fulls)

    def body(*refs):
        out_refs = refs[T:2 * T]
        send_sems, recv_sems = refs[2 * T:]
        x, y, c = _mesh_pos()
        chips = _other_chips(x, y)
        sibling = (x, y, 1 - c)

        def win(t, s, hf):
            n, M, N = fulls[t].shape
            if shardings[t] == "cols":
                rh, C = M // 2, N // 4
                return out_refs[t].at[:, pl.ds(hf * rh, rh), pl.ds(s * C, C)]
            rh = M // 8
            return out_refs[t].at[:, pl.ds((2 * s + hf) * rh, rh), :]

        def remote(t, k, src, dst, to):
            return pltpu.make_async_remote_copy(src_ref=src, dst_ref=dst, send_sem=send_sems.at[t, k],
                                                recv_sem=recv_sems.at[t, k], device_id=to, device_id_type=MESH)

        first = [remote(t, j, win(t, 2 * x + y, c), win(t, 2 * x + y, c), (*chip, c))
                 for t in range(T) for j, chip in enumerate(chips)]
        for cp in first:
            cp.start()
        passed = []
        for t in range(T):
            for j, (cx, cy) in enumerate(chips):
                landed = win(t, 2 * cx + cy, c)
                remote(t, j, landed, landed, (cx, cy, c)).wait_recv()
                fwd = remote(t, 3 + j, landed, landed, sibling)
                fwd.start()
                passed.append(fwd)
        for t in range(T):
            for j, (cx, cy) in enumerate(chips):
                other = win(t, 2 * cx + cy, 1 - c)
                remote(t, 3 + j, other, other, sibling).wait_recv()
        for cp in first + passed:
            cp.wait_send()

    return pl.pallas_call(
        body, name=name,
        in_specs=[ANY] * T, out_specs=[ANY] * T,
        out_shape=[jax.ShapeDtypeStruct(f.shape, f.dtype) for f in fulls],
        input_output_aliases={t: t for t in range(T)},
        scratch_shapes=[pltpu.SemaphoreType.DMA((T, 6)), pltpu.SemaphoreType.DMA((T, 6))],
    )(*fulls)


def exchange_halves(bufs, *, name):
    T = len(bufs)

    def body(*refs):
        in_refs, out_refs = refs[:T], refs[T:2 * T]
        send_sems, recv_sems = refs[2 * T:]
        x, y, c = _mesh_pos()
        cps = [pltpu.make_async_remote_copy(src_ref=in_refs[t].at[:, pl.ds(1 - c, 1)], dst_ref=out_refs[t],
                                            send_sem=send_sems.at[t], recv_sem=recv_sems.at[t],
                                            device_id=(x, y, 1 - c), device_id_type=MESH) for t in range(T)]
        for cp in cps:
            cp.start()
        for cp in cps:
            cp.wait()

    return pl.pallas_call(
        body, name=name, in_specs=[ANY] * T, out_specs=[ANY] * T,
        out_shape=[jax.ShapeDtypeStruct((4, 1) + b.shape[2:], b.dtype) for b in bufs],
        scratch_shapes=[pltpu.SemaphoreType.DMA((T,)), pltpu.SemaphoreType.DMA((T,))],
    )(*bufs)


def pair_sum(buf, got, c, *, name, tr=512):
    _, _, n, rh, C = buf.shape
    tr = _tile(rh, tr)

    def body(c_ref, a_ref, b_ref, o_ref):
        o_ref[...] = (a_ref[...].astype(F32) + b_ref[...].astype(F32)).astype(o_ref.dtype)

    grid_spec = pltpu.PrefetchScalarGridSpec(
        num_scalar_prefetch=1, grid=(4, n, rh // tr),
        in_specs=[pl.BlockSpec((None, None, None, tr, C), lambda s, l, i, c_ref: (s, c_ref[0], l, i, 0)),
                  pl.BlockSpec((None, None, None, tr, C), lambda s, l, i, c_ref: (s, 0, l, i, 0))],
        out_specs=pl.BlockSpec((None, None, tr, C), lambda s, l, i, c_ref: (s, l, i, 0)))
    return pl.pallas_call(
        body, name=name, grid_spec=grid_spec, out_shape=jax.ShapeDtypeStruct((4, n, rh, C), BF16),
        compiler_params=_cparams("parallel", "parallel", "parallel"),
    )(c.reshape(1).astype(jnp.int32), buf, got)


def scatter_to_owners(sums, *, name):
    T = len(sums)

    def body(*refs):
        in_refs, out_refs = refs[:T], refs[T:2 * T]
        send_sems, recv_sems = refs[2 * T:]
        x, y, c = _mesh_pos()
        chips = _other_chips(x, y)
        cps = [pltpu.make_async_remote_copy(src_ref=in_refs[t].at[pl.ds(2 * cx + cy, 1)],
                                            dst_ref=out_refs[t].at[pl.ds(j, 1)],
                                            send_sem=send_sems.at[t, j], recv_sem=recv_sems.at[t, j],
                                            device_id=(cx, cy, c), device_id_type=MESH)
               for t in range(T) for j, (cx, cy) in enumerate(chips)]
        for cp in cps:
            cp.start()
        for cp in cps:
            cp.wait()

    return pl.pallas_call(
        body, name=name, in_specs=[ANY] * T, out_specs=[ANY] * T,
        out_shape=[jax.ShapeDtypeStruct((3,) + s.shape[1:], s.dtype) for s in sums],
        scratch_shapes=[pltpu.SemaphoreType.DMA((T, 3)), pltpu.SemaphoreType.DMA((T, 3))],
    )(*sums)


def owner_sum(sums, got, s, c, *, name, tr=512):
    _, n, rh, C = sums.shape
    tr = _tile(rh, tr)
    nr = rh // tr

    def body(sc_ref, a_ref, b0_ref, b1_ref, b2_ref, o_ref):
        o_ref[...] = ((a_ref[...].astype(F32) + b0_ref[...].astype(F32)) + b1_ref[...].astype(F32)) \
            + b2_ref[...].astype(F32)

    def got_spec(j):
        return pl.BlockSpec((None, None, tr, C), lambda l, i, sc_ref: (j, l, i, 0))

    grid_spec = pltpu.PrefetchScalarGridSpec(
        num_scalar_prefetch=1, grid=(n, nr),
        in_specs=[pl.BlockSpec((None, None, tr, C), lambda l, i, sc_ref: (sc_ref[0], l, i, 0)),
                  got_spec(0), got_spec(1), got_spec(2)],
        out_specs=pl.BlockSpec((None, tr, C), lambda l, i, sc_ref: (l, sc_ref[1] * nr + i, 0)))
    return pl.pallas_call(
        body, name=name, grid_spec=grid_spec, out_shape=jax.ShapeDtypeStruct((n, 2 * rh, C), F32),
        compiler_params=_cparams("parallel", "parallel"),
    )(jnp.stack([s, c]).astype(jnp.int32), sums, got, got, got)


def join_halves(grads, *, name):
    T = len(grads)

    def body(*refs):
        out_refs = refs[T:2 * T]
        send_sems, recv_sems = refs[2 * T:]
        x, y, c = _mesh_pos()

        def win(t, hf):
            rh = grads[t].shape[1] // 2
            return out_refs[t].at[:, pl.ds(hf * rh, rh), :]

        def remote(t, w):
            return pltpu.make_async_remote_copy(src_ref=w, dst_ref=w, send_sem=send_sems.at[t],
                                                recv_sem=recv_sems.at[t], device_id=(x, y, 1 - c),
                                                device_id_type=MESH)

        cps = [remote(t, win(t, c)) for t in range(T)]
        for cp in cps:
            cp.start()
        for t in range(T):
            remote(t, win(t, 1 - c)).wait_recv()
        for cp in cps:
            cp.wait_send()

    return pl.pallas_call(
        body, name=name, in_specs=[ANY] * T, out_specs=[ANY] * T,
        out_shape=[jax.ShapeDtypeStruct(g.shape, g.dtype) for g in grads],
        input_output_aliases={t: t for t in range(T)},
        scratch_shapes=[pltpu.SemaphoreType.DMA((T,)), pltpu.SemaphoreType.DMA((T,))],
    )(*grads)


def allreduce_small(v, *, name):
    rows = v.shape[0]

    def body(v_ref, out_ref, buf, send_sems, recv_sems):
        x, y, c = _mesh_pos()
        me = 4 * x + 2 * y + c
        flips = [(dx, dy, dc) for dx in (0, 1) for dy in (0, 1) for dc in (0, 1)][1:]

        def peer(f):
            return tuple(1 - p if d else p for d, p in zip(f, (x, y, c)))

        cps = []
        for k, f in enumerate(flips):
            px, py, pc = peer(f)
            cps.append(pltpu.make_async_remote_copy(src_ref=v_ref, dst_ref=buf.at[me], send_sem=send_sems.at[k],
                                                    recv_sem=recv_sems.at[k], device_id=(px, py, pc),
                                                    device_id_type=MESH))
        for cp in cps:
            cp.start()
        buf[me] = v_ref[...]
        for k, f in enumerate(flips):
            px, py, pc = peer(f)
            slot = buf.at[4 * px + 2 * py + pc]
            pltpu.make_async_remote_copy(src_ref=slot, dst_ref=slot, send_sem=send_sems.at[k],
                                         recv_sem=recv_sems.at[k], device_id=(px, py, pc),
                                         device_id_type=MESH).wait_recv()
        for cp in cps:
            cp.wait_send()
        acc = buf[0]
        for i in range(1, 8):
            acc = acc + buf[i]
        out_ref[...] = acc

    vm = pl.BlockSpec(memory_space=pltpu.VMEM)
    return pl.pallas_call(
        body, name=name, in_specs=[vm], out_specs=vm,
        out_shape=jax.ShapeDtypeStruct(v.shape, F32),
        scratch_shapes=[pltpu.VMEM((8, rows, LANES), F32), pltpu.SemaphoreType.DMA((7,)),
                        pltpu.SemaphoreType.DMA((7,))],
    )(v)


def _adam_math(w, m, v, g):
    m = ADAM_B1 * m + (1.0 - ADAM_B1) * g
    v = ADAM_B2 * v + (1.0 - ADAM_B2) * (g * g)
    m_hat = m / (1.0 - ADAM_B1 ** ADAM_STEP)
    v_hat = v / (1.0 - ADAM_B2 ** ADAM_STEP)
    delta = -ADAM_LR * (m_hat / (jnp.sqrt(v_hat) + ADAM_EPS) + ADAM_WD * w)
    return delta, m, v


def adamw(w, m, v, g, *, name, tr=256):
    def body(w_ref, m_ref, v_ref, g_ref, d_ref, nm_ref, nv_ref):
        d, nm, nv = _adam_math(w_ref[...], m_ref[...], v_ref[...], g_ref[...])
        d_ref[...] = d
        nm_ref[...] = nm
        nv_ref[...] = nv

    shape = jax.ShapeDtypeStruct(w.shape, F32)
    if w.ndim == 2:
        return pl.pallas_call(body, name=name, out_shape=[shape] * 3)(w, m, v, g)
    n, R, C = w.shape
    tr = _tile(R, tr)
    spec = pl.BlockSpec((None, tr, C), lambda l, i: (l, i, 0))
    return pl.pallas_call(
        body, name=name, grid=(n, R // tr), in_specs=[spec] * 4, out_specs=[spec] * 3, out_shape=[shape] * 3,
        compiler_params=_cparams("parallel", "parallel"),
    )(w, m, v, g)


def _pack(parts):
    flat = jnp.concatenate([p.reshape(-1).astype(F32) for p in parts])
    size = -(-flat.shape[0] // (8 * LANES)) * (8 * LANES)
    return jnp.pad(flat, (0, size - flat.shape[0])).reshape(size // LANES, LANES)


def _unpack(block, shapes):
    flat = block.reshape(-1)
    out, pos = [], 0
    for shp in shapes:
        size = int(np.prod(shp))
        out.append(flat[pos:pos + size].reshape(shp))
        pos += size
    return out


SMALL = ("norm_mix_g", "norm_mlp_g", "final_norm_g", "hgrn_lb_raw", "hgrn_norm_g", "odd_sinks")
WEIGHTS = ("norm_mix_g", "norm_mlp_g", "final_norm_g", "even_w_in", "even_w_out", "hgrn_lb_raw", "hgrn_norm_g",
           "odd_w_qkv", "odd_b_qkv", "odd_sinks", "odd_w_o", "odd_b_o", "mlp_w1", "mlp_w2")


def kernel(x, norm_mix_g, norm_mlp_g, final_norm_g, even_w_in, even_w_out, hgrn_lb_raw, hgrn_norm_g, odd_w_qkv, odd_b_qkv, odd_sinks, odd_w_o, odd_b_o, mlp_w1, mlp_w2, loss_target, m_norm_mix_g, m_norm_mlp_g, m_final_norm_g, m_even_w_in, m_even_w_out, m_hgrn_lb_raw, m_hgrn_norm_g, m_odd_w_qkv, m_odd_b_qkv, m_odd_sinks, m_odd_w_o, m_odd_b_o, m_mlp_w1, m_mlp_w2, v_norm_mix_g, v_norm_mlp_g, v_final_norm_g, v_even_w_in, v_even_w_out, v_hgrn_lb_raw, v_hgrn_norm_g, v_odd_w_qkv, v_odd_b_qkv, v_odd_sinks, v_odd_w_o, v_odd_b_o, v_mlp_w1, v_mlp_w2):
    args = locals()
    W = {k: args[k] for k in WEIGHTS}
    M = {k: args["m_" + k] for k in WEIGHTS}
    V = {k: args["v_" + k] for k in WEIGHTS}
    _, S, D = x.shape
    depth = norm_mix_g.shape[0]
    HQ = D // C_DIM
    HKV = HQ // C_GROUP
    xi, yi, ci = _mesh_pos()
    shard = 2 * xi + yi

    full = dict(zip(BIG, gather_weights([cast_into_full(W[k], SHARDING[k], shard, name="cast_" + k) for k in BIG],
                                        [SHARDING[k] for k in BIG], name="gather_weights")))
    n_odd, bq = odd_b_qkv.shape
    bo = odd_b_o.shape[1]
    keep = (ci == 0).astype(F32)
    pieces = [lax.dynamic_update_slice(jnp.zeros((n_odd, 4 * bq), F32), odd_b_qkv * keep, (0, shard * bq)),
              lax.dynamic_update_slice(jnp.zeros((n_odd, 4 * bo), F32), odd_b_o * keep, (0, shard * bo))]
    b_qkv_full, b_o_full = _unpack(allreduce_small(_pack(pieces), name="gather_biases"),
                                   [(n_odd, 4 * bq), (n_odd, 4 * bo)])
    P = {k: W[k] for k in SMALL}
    P.update(even_w_in=full["even_w_in"], even_w_out=full["even_w_out"], mlp_w1=full["mlp_w1"],
             mlp_w2=full["mlp_w2"],
             odd_w_qkv=_pad_heads_cols(full["odd_w_qkv"], HQ + 2 * HKV),
             odd_b_qkv=_pad_heads_cols(b_qkv_full, HQ + 2 * HKV),
             odd_w_o=jnp.swapaxes(_pad_heads_cols(jnp.swapaxes(full["odd_w_o"], 1, 2), HQ), 1, 2),
             odd_b_o=b_o_full)

    loss, dx, G = local_step(x[0], loss_target[0], P, grad_buffers(depth, D))

    G["odd_w_qkv"] = full_to_buffer(_unpad_heads_cols(G["odd_w_qkv"], HQ + 2 * HKV), "cols")
    G["odd_w_o"] = full_to_buffer(
        jnp.swapaxes(_unpad_heads_cols(jnp.swapaxes(G["odd_w_o"], 1, 2), HQ), 1, 2), "rows")
    bufs = [G[k] for k in BIG]
    got = exchange_halves(bufs, name="exchange_halves")
    sums = [pair_sum(b, g, ci, name="pair_sum_" + k) for k, b, g in zip(BIG, bufs, got)]
    got = scatter_to_owners(sums, name="scatter_to_owners")
    halves = [owner_sum(s, g, shard, ci, name="owner_sum_" + k) for k, s, g in zip(BIG, sums, got)]
    grads = dict(zip(BIG, join_halves(halves, name="join_halves")))

    small_keys = SMALL + ("odd_b_qkv", "odd_b_o")
    G["odd_b_qkv"] = _unpad_heads_cols(G["odd_b_qkv"], HQ + 2 * HKV)
    small_shapes = [G[k].shape for k in small_keys]
    small = _unpack(allreduce_small(_pack([G[k] for k in small_keys]), name="reduce_small"), small_shapes)
    grads.update(zip(small_keys, small))
    grads["odd_b_qkv"] = lax.dynamic_slice(grads["odd_b_qkv"], (0, shard * bq), (n_odd, bq))
    grads["odd_b_o"] = lax.dynamic_slice(grads["odd_b_o"], (0, shard * bo), (n_odd, bo))

    deltas, new_m, new_v = {}, {}, {}
    for k in WEIGHTS:
        w2 = (lambda a: a.reshape(1, -1)) if W[k].ndim == 1 else (lambda a: a)
        d, nm, nv = adamw(w2(W[k]), w2(M[k]), w2(V[k]), w2(grads[k]), name="adamw_" + k)
        deltas[k], new_m[k], new_v[k] = (a.reshape(W[k].shape) for a in (d, nm, nv))

    loss = lax.psum(loss[0, 0], ("x", "y", "c"))
    return (loss, dx[None], *[grads[k] for k in WEIGHTS], *[deltas[k] for k in WEIGHTS],
            *[new_m[k] for k in WEIGHTS], *[new_v[k] for k in WEIGHTS])
```

```python
import functools
import math

import jax
import jax.numpy as jnp
import numpy as np
from jax import lax
from jax.experimental import pallas as pl
from jax.experimental.pallas import tpu as pltpu

F32 = jnp.float32
BF16 = jnp.bfloat16

NORM_EPS = 1e-5
ROPE_THETA = 500000.0
ROPE_FRACTION = 4
LANES = 128
BAND = 128
A_DIM = 128
A_BRANCHES = ((128, 1), (512, 4), (2048, 16))
B_DIM = 128
B_CHUNK = 64
B_SUB = 16
C_DIM = 64
C_GROUP = 8
C_WINDOW = 128

ADAM_LR = 0.001
ADAM_B1 = 0.9
ADAM_B2 = 0.999
ADAM_EPS = 1e-08
ADAM_WD = 0.01
ADAM_STEP = 10

VMEM_LIMIT = 56 * 1024 * 1024


def _cparams(*sem):
    return pltpu.CompilerParams(dimension_semantics=tuple(sem), vmem_limit_bytes=VMEM_LIMIT)


def _tile(n, want):
    if n <= want:
        return n
    t = want - want % LANES
    while n % t:
        t -= LANES
    assert t > 0, (n, want)
    return t


def matmul(a, b, *, trans_a=False, trans_b=False, out_dtypes, a_fn=None, epilogue=None, extras=(), name,
           b_layer=None, into=None, dep=None, tm=1024, tn=1024, tk=2048):
    if trans_a:
        K, M = a.shape
    else:
        M, K = a.shape
    b_mat = b.shape[1:] if b_layer is not None else b.shape
    if trans_b:
        N, K2 = b_mat
    else:
        K2, N = b_mat
    assert K == K2, (a.shape, b.shape)
    if into is not None:
        rh, cs = into[0][2:]
        tm, tn = _tile(rh, tm), _tile(cs, tn)
    tm, tn, tk = _tile(M, tm), _tile(N, tn), _tile(K, tk)
    nk = K // tk
    n_extra = len(extras)
    n_out = len(out_dtypes)

    def body(*refs):
        a_ref, b_ref = refs[0], refs[1]
        extra_refs = refs[2:2 + n_extra]
        out_refs = refs[-1 - n_out:-1]
        acc_ref = refs[-1]
        k = pl.program_id(2)
        at = a_ref[...]
        if a_fn is not None:
            at = a_fn(at.astype(F32))
        at = at.astype(BF16)
        bt = b_ref[...].astype(BF16)
        dims = (((0,) if trans_a else (1,), (1,) if trans_b else (0,)), ((), ()))
        prod = lax.dot_general(at, bt, dims, preferred_element_type=F32)

        def finish(acc):
            ex = [r[...] for r in extra_refs]
            outs = epilogue(acc, *ex) if epilogue is not None else (acc,)
            for o_ref, o in zip(out_refs, outs):
                o_ref[...] = o.astype(o_ref.dtype)

        if nk == 1:
            finish(prod)
            return

        @pl.when(k == 0)
        def _():
            acc_ref[...] = prod

        @pl.when((k > 0) & (k < nk - 1))
        def _():
            acc_ref[...] += prod

        @pl.when(k == nk - 1)
        def _():
            finish(acc_ref[...] + prod)

    a_spec = pl.BlockSpec((tk, tm), lambda i, j, k: (k, i)) if trans_a else pl.BlockSpec((tm, tk), lambda i, j, k: (i, k))
    if b_layer is None:
        b_spec = (pl.BlockSpec((tn, tk), lambda i, j, k: (j, k)) if trans_b
                  else pl.BlockSpec((tk, tn), lambda i, j, k: (k, j)))
    else:
        b_spec = (pl.BlockSpec((None, tn, tk), lambda i, j, k: (b_layer, j, k)) if trans_b
                  else pl.BlockSpec((None, tk, tn), lambda i, j, k: (b_layer, k, j)))
    e_specs = []
    for e in extras:
        if e.shape == (1, N):
            e_specs.append(pl.BlockSpec((1, tn), lambda i, j, k: (0, j)))
        else:
            assert e.shape == (M, N), (e.shape, M, N)
            e_specs.append(pl.BlockSpec((tm, tn), lambda i, j, k: (i, j)))
    args = [a, b, *extras]
    in_specs = [a_spec, b_spec] + e_specs
    if dep is not None:
        args.append(dep)
        in_specs.append(pl.BlockSpec(dep.shape, lambda i, j, k: (0, 0)))
    if into is None:
        out_specs = [pl.BlockSpec((tm, tn), lambda i, j, k: (i, j)) for _ in out_dtypes]
        out_shape = [jax.ShapeDtypeStruct((M, N), dt) for dt in out_dtypes]
    else:
        buf_shape, sharding = into
        assert n_out == 1
        index = grad_buffer_index(buf_shape, (M, N), sharding, tm, tn)
        out_specs = [pl.BlockSpec((None, None, tm, tn), lambda i, j, k: index(i, j))]
        out_shape = [jax.ShapeDtypeStruct(buf_shape, out_dtypes[0])]
    outs = pl.pallas_call(
        body, name=name,
        grid=(M // tm, N // tn, nk),
        in_specs=in_specs, out_specs=out_specs, out_shape=out_shape,
        scratch_shapes=[pltpu.VMEM((tm, tn), F32)],
        compiler_params=_cparams("parallel", "parallel", "arbitrary"),
    )(*args)
    return outs[0] if n_out == 1 else tuple(outs)


def grad_buffer_shape(mat_shape, sharding):
    M, N = mat_shape
    return (4, 2, M // 2, N // 4) if sharding == "cols" else (4, 2, M // 8, N)


def grad_buffer_index(buf_shape, mat_shape, sharding, tm, tn):
    _, _, rh, cs = buf_shape
    M, N = mat_shape
    ib, jb = rh // tm, cs // tn
    if sharding == "cols":
        assert (M, N) == (2 * rh, 4 * cs), (buf_shape, mat_shape)
        return lambda i, j: (j // jb, i // ib, i % ib, j % jb)
    assert sharding == "rows" and (M, N) == (8 * rh, cs), (buf_shape, mat_shape)
    return lambda i, j: (i // (2 * ib), (i // ib) % 2, i % ib, j)


def rmsnorm_fwd(x, g, *, name, dep=None, tr=256):
    S, D = x.shape
    tr = _tile(S, tr)

    def body(x_ref, g_ref, *rest):
        h_ref = rest[-1]
        xv = x_ref[...]
        rstd = lax.rsqrt(jnp.mean(xv * xv, axis=-1, keepdims=True) + NORM_EPS)
        h_ref[...] = (xv * rstd * g_ref[...]).astype(h_ref.dtype)

    args = [x, g.reshape(1, D)]
    in_specs = [pl.BlockSpec((tr, D), lambda i: (i, 0)), pl.BlockSpec((1, D), lambda i: (0, 0))]
    if dep is not None:
        args.append(dep)
        in_specs.append(pl.BlockSpec(dep.shape, lambda i: (0, 0)))
    return pl.pallas_call(
        body, name=name, grid=(S // tr,),
        in_specs=in_specs,
        out_specs=pl.BlockSpec((tr, D), lambda i: (i, 0)),
        out_shape=jax.ShapeDtypeStruct((S, D), BF16),
        compiler_params=_cparams("parallel"),
    )(*args)


def _rms_bwd_rows(xv, gv, dh):
    rstd = lax.rsqrt(jnp.mean(xv * xv, axis=-1, keepdims=True) + NORM_EPS)
    xhat = xv * rstd
    dxhat = dh * gv
    dx = rstd * (dxhat - xhat * jnp.mean(dxhat * xhat, axis=-1, keepdims=True))
    return dx, dh * xhat


def rmsnorm_bwd(x, g, dh, dres, *, name, tr=256):
    S, D = x.shape
    tr = _tile(S, tr)

    def body(x_ref, g_ref, dh_ref, dres_ref, dx_ref, dg_ref):
        @pl.when(pl.program_id(0) == 0)
        def _():
            dg_ref[...] = jnp.zeros_like(dg_ref)

        dx, dgr = _rms_bwd_rows(x_ref[...], g_ref[...], dh_ref[...].astype(F32))
        dx_ref[...] = dres_ref[...] + dx
        dg_ref[...] += jnp.sum(dgr, axis=0, keepdims=True)

    row = pl.BlockSpec((tr, D), lambda i: (i, 0))
    vec = pl.BlockSpec((1, D), lambda i: (0, 0))
    return pl.pallas_call(
        body, name=name, grid=(S // tr,),
        in_specs=[row, vec, row, row],
        out_specs=[row, vec],
        out_shape=[jax.ShapeDtypeStruct((S, D), F32), jax.ShapeDtypeStruct((1, D), F32)],
        compiler_params=_cparams("arbitrary"),
    )(x, g.reshape(1, D), dh, dres)


def final_norm_loss(x, g, target, *, name, tr=256):
    S, D = x.shape
    tr = _tile(S, tr)

    def body(x_ref, g_ref, t_ref, loss_ref, dx_ref, dg_ref):
        @pl.when(pl.program_id(0) == 0)
        def _():
            dg_ref[...] = jnp.zeros_like(dg_ref)
            loss_ref[...] = jnp.zeros_like(loss_ref)

        xv, gv = x_ref[...], g_ref[...]
        rstd = lax.rsqrt(jnp.mean(xv * xv, axis=-1, keepdims=True) + NORM_EPS)
        err = xv * rstd * gv - t_ref[...]
        part = 0.5 * jnp.sum(jnp.mean(err * err, axis=-1, keepdims=True), axis=0, keepdims=True)
        loss_ref[...] += jnp.broadcast_to(part, loss_ref.shape)
        dx, dgr = _rms_bwd_rows(xv, gv, err * (1.0 / D))
        dx_ref[...] = dx
        dg_ref[...] += jnp.sum(dgr, axis=0, keepdims=True)

    row = pl.BlockSpec((tr, D), lambda i: (i, 0))
    vec = pl.BlockSpec((1, D), lambda i: (0, 0))
    return pl.pallas_call(
        body, name=name, grid=(S // tr,),
        in_specs=[row, vec, row],
        out_specs=[pl.BlockSpec((8, LANES), lambda i: (0, 0)), row, vec],
        out_shape=[jax.ShapeDtypeStruct((8, LANES), F32), jax.ShapeDtypeStruct((S, D), F32),
                   jax.ShapeDtypeStruct((1, D), F32)],
        compiler_params=_cparams("arbitrary"),
    )(x, g.reshape(1, D), target)


def colsum(a, *, name, tr=256):
    S, N = a.shape
    tr = _tile(S, tr)

    def body(a_ref, o_ref):
        @pl.when(pl.program_id(0) == 0)
        def _():
            o_ref[...] = jnp.zeros_like(o_ref)

        o_ref[...] += jnp.sum(a_ref[...].astype(F32), axis=0, keepdims=True)

    return pl.pallas_call(
        body, name=name, grid=(S // tr,),
        in_specs=[pl.BlockSpec((tr, N), lambda i: (i, 0))],
        out_specs=pl.BlockSpec((1, N), lambda i: (0, 0)),
        out_shape=jax.ShapeDtypeStruct((1, N), F32),
        compiler_params=_cparams("arbitrary"),
    )(a)


def rope_tables(seq, head_dim):
    rot = head_dim // ROPE_FRACTION
    half = rot // 2
    inv_freq = 1.0 / (ROPE_THETA ** (jnp.arange(0, rot, 2, dtype=F32) / rot))
    ang = jnp.arange(seq, dtype=F32)[:, None] * inv_freq[None, :]
    cos, sin = jnp.cos(ang), jnp.sin(ang)
    ones = jnp.ones((seq, LANES - 2 * half), F32)
    zeros = jnp.zeros((seq, LANES - 2 * half), F32)
    zh = jnp.zeros((seq, half), F32)
    c = jnp.concatenate([cos, cos, ones], axis=1)
    sa = jnp.concatenate([-sin, zh, zeros], axis=1)
    sb = jnp.concatenate([zh, sin, zeros], axis=1)
    return c, sa, sb, half


def _rope(x, c, sa, sb, half):
    return x * c + pltpu.roll(x, LANES - half, 1) * sa + pltpu.roll(x, half, 1) * sb


def _rope_bwd(d, c, sa, sb, half):
    return d * c + pltpu.roll(d * sa, half, 1) + pltpu.roll(d * sb, LANES - half, 1)


def _band_masks(n, max_dist):
    qi = lax.broadcasted_iota(jnp.int32, (BAND, BAND), 0)
    kj = lax.broadcasted_iota(jnp.int32, (BAND, BAND), 1)
    cur = kj <= qi
    prev = ((kj >= qi) if max_dist == BAND else (kj > qi)) & (n > 0)
    return prev, cur


def _dot_nt(a, b):
    return lax.dot_general(a, b, (((1,), (1,)), ((), ())), preferred_element_type=F32)


def _dot_tn(a, b):
    return lax.dot_general(a, b, (((0,), (0,)), ((), ())), preferred_element_type=F32)


def _dot(a, b):
    return jnp.dot(a, b, preferred_element_type=F32)


def band_attention_fwd(src, tabs, *, L, dil, n_heads, group, q_blk, k_blk, v_blk, blk_per_row, max_dist, scale,
                       half, sink=None, normalize, name):
    c_t, sa_t, sb_t = tabs
    nb = L // BAND
    W = dil * n_heads * LANES
    use_sink = sink is not None

    def body(*refs):
        q_ref, k_ref, v_ref, c_ref, sa_ref, sb_ref = refs[:6]
        pos = 6
        if use_sink:
            sink_ref = refs[pos]
            pos += 1
        o_ref, m_ref, l_ref = refs[pos:pos + 3] if not normalize else (refs[pos], refs[pos + 1], None)
        kr_ref, vb_ref = refs[-2], refs[-1]

        def prep(n, carry):
            rows = pl.ds(pl.multiple_of(n * BAND, BAND), BAND)
            kr_ref[rows, :] = _rope(k_ref[rows, :], c_ref[rows, :], sa_ref[rows, :], sb_ref[rows, :], half).astype(BF16)
            vb_ref[rows, :] = v_ref[rows, :].astype(BF16)
            return carry

        lax.fori_loop(0, nb, prep, 0)

        def step(n, carry):
            rows = pl.ds(pl.multiple_of(n * BAND, BAND), BAND)
            prow = pl.ds(pl.multiple_of(jnp.maximum(n - 1, 0) * BAND, BAND), BAND)
            q = _rope(q_ref[rows, :], c_ref[rows, :], sa_ref[rows, :], sb_ref[rows, :], half).astype(BF16)
            mp, mc = _band_masks(n, max_dist)
            sp = jnp.where(mp, _dot_nt(q, kr_ref[prow, :]) * scale, -jnp.inf)
            sc = jnp.where(mc, _dot_nt(q, kr_ref[rows, :]) * scale, -jnp.inf)
            m = jnp.maximum(jnp.max(sp, axis=-1, keepdims=True), jnp.max(sc, axis=-1, keepdims=True))
            if use_sink:
                sk = sink_ref[:, 0:1]
                m = jnp.maximum(m, sk)
            pp = jnp.exp(sp - m)
            pc = jnp.exp(sc - m)
            l = jnp.sum(pp, axis=-1, keepdims=True) + jnp.sum(pc, axis=-1, keepdims=True)
            if use_sink:
                l = l + jnp.exp(sk - m)
            num = _dot(pp.astype(BF16), vb_ref[prow, :]) + _dot(pc.astype(BF16), vb_ref[rows, :])
            if normalize:
                o_ref[rows, :] = num / l
                m_ref[rows, :] = jnp.broadcast_to(m + jnp.log(l), (BAND, LANES))
            else:
                o_ref[rows, :] = num
                m_ref[rows, :] = jnp.broadcast_to(m, (BAND, LANES))
                l_ref[rows, :] = jnp.broadcast_to(l, (BAND, LANES))
            return carry

        lax.fori_loop(0, nb, step, 0)

    def col(off, div):
        return pl.BlockSpec((L, LANES), lambda r, h: (0, r * blk_per_row + off + h // div))

    tab = pl.BlockSpec((L, LANES), lambda r, h: (0, r))
    out = pl.BlockSpec((L, LANES), lambda r, h: (0, r * n_heads + h))
    in_specs = [col(q_blk, 1), col(k_blk, group), col(v_blk, group), tab, tab, tab]
    args = [src, src, src, c_t, sa_t, sb_t]
    if use_sink:
        in_specs.append(pl.BlockSpec((None, 1, LANES), lambda r, h: (h, 0, 0)))
        args.append(sink)
    n_out = 2 if normalize else 3
    return pl.pallas_call(
        body, name=name, grid=(dil, n_heads),
        in_specs=in_specs,
        out_specs=[out] * n_out,
        out_shape=[jax.ShapeDtypeStruct((L, W), F32)] * n_out,
        scratch_shapes=[pltpu.VMEM((L, LANES), BF16), pltpu.VMEM((L, LANES), BF16)],
        compiler_params=_cparams("parallel", "arbitrary"),
    )(*args)


def band_attention_bwd(src, tabs, o, lse, do, *, L, dil, n_heads, group, q_blk, k_blk, v_blk, blk_per_row, max_dist,
                       scale, half, do_bpr, sink=None, acc=None, name):
    c_t, sa_t, sb_t = tabs
    nb = L // BAND
    n_kv = n_heads // group
    use_sink = sink is not None
    use_acc = acc is not None
    assert not (use_acc and group != 1)

    def body(*refs):
        q_ref, k_ref, v_ref, c_ref, sa_ref, sb_ref, o_ref, lse_ref, do_ref = refs[:9]
        pos = 9
        if use_sink:
            sink_ref = refs[pos]
            pos += 1
        if use_acc:
            aq_ref, ak_ref, av_ref = refs[pos:pos + 3]
            pos += 3
        dq_ref, dk_ref, dv_ref = refs[pos:pos + 3]
        pos += 3
        if use_sink:
            dsink_ref = refs[pos]
        kr_ref, vb_ref, dka_ref, dva_ref = refs[-4:]
        h = pl.program_id(1)

        def prep(n, carry):
            rows = pl.ds(pl.multiple_of(n * BAND, BAND), BAND)
            kr_ref[rows, :] = _rope(k_ref[rows, :], c_ref[rows, :], sa_ref[rows, :], sb_ref[rows, :], half).astype(BF16)
            vb_ref[rows, :] = v_ref[rows, :].astype(BF16)
            return carry

        lax.fori_loop(0, nb, prep, 0)

        @pl.when(h % group == 0)
        def _():
            dka_ref[...] = jnp.zeros_like(dka_ref)
            dva_ref[...] = jnp.zeros_like(dva_ref)

        def step(n, dsk):
            rows = pl.ds(pl.multiple_of(n * BAND, BAND), BAND)
            prow = pl.ds(pl.multiple_of(jnp.maximum(n - 1, 0) * BAND, BAND), BAND)
            cv, sav, sbv = c_ref[rows, :], sa_ref[rows, :], sb_ref[rows, :]
            q = _rope(q_ref[rows, :], cv, sav, sbv, half).astype(BF16)
            dov = do_ref[rows, :]
            lsev = lse_ref[rows, 0:1]
            delta = jnp.sum(dov * o_ref[rows, :], axis=-1, keepdims=True)
            dob = dov.astype(BF16)
            mp, mc = _band_masks(n, max_dist)
            kp, kc, vp, vc = kr_ref[prow, :], kr_ref[rows, :], vb_ref[prow, :], vb_ref[rows, :]
            pp = jnp.exp(jnp.where(mp, _dot_nt(q, kp) * scale, -jnp.inf) - lsev)
            pc = jnp.exp(jnp.where(mc, _dot_nt(q, kc) * scale, -jnp.inf) - lsev)
            dsp = (pp * (_dot_nt(dob, vp) - delta) * scale).astype(BF16)
            dsc = (pc * (_dot_nt(dob, vc) - delta) * scale).astype(BF16)
            dq = _rope_bwd(_dot(dsp, kp) + _dot(dsc, kc), cv, sav, sbv, half)
            if use_acc:
                dq = dq + aq_ref[rows, :]
            dq_ref[rows, :] = dq
            dka_ref[prow, :] += _dot_tn(dsp, q)
            dka_ref[rows, :] += _dot_tn(dsc, q)
            dva_ref[prow, :] += _dot_tn(pp.astype(BF16), dob)
            dva_ref[rows, :] += _dot_tn(pc.astype(BF16), dob)
            if use_sink:
                dsk = dsk - jnp.sum(jnp.exp(sink_ref[:, 0:1] - lsev) * delta, axis=0, keepdims=True)
            return dsk

        dsk = lax.fori_loop(0, nb, step, jnp.zeros((1, 1), F32))
        if use_sink:
            dsink_ref[...] = jnp.broadcast_to(dsk, dsink_ref.shape)

        @pl.when(h % group == group - 1)
        def _():
            def fin(n, carry):
                rows = pl.ds(pl.multiple_of(n * BAND, BAND), BAND)
                dk = _rope_bwd(dka_ref[rows, :], c_ref[rows, :], sa_ref[rows, :], sb_ref[rows, :], half)
                dv = dva_ref[rows, :]
                if use_acc:
                    dk = dk + ak_ref[rows, :]
                    dv = dv + av_ref[rows, :]
                dk_ref[rows, :] = dk
                dv_ref[rows, :] = dv
                return carry

            lax.fori_loop(0, nb, fin, 0)

    def col(off, div):
        return pl.BlockSpec((L, LANES), lambda r, h: (0, r * blk_per_row + off + h // div))

    tab = pl.BlockSpec((L, LANES), lambda r, h: (0, r))
    qo = pl.BlockSpec((L, LANES), lambda r, h: (0, r * n_heads + h))
    kvo = pl.BlockSpec((L, LANES), lambda r, h: (0, r * n_kv + h // group))
    dospec = pl.BlockSpec((L, LANES), lambda r, h: (0, r * do_bpr + h))
    in_specs = [col(q_blk, 1), col(k_blk, group), col(v_blk, group), tab, tab, tab, qo, qo, dospec]
    args = [src, src, src, c_t, sa_t, sb_t, o, lse, do]
    if use_sink:
        in_specs.append(pl.BlockSpec((None, 1, LANES), lambda r, h: (h, 0, 0)))
        args.append(sink)
    if use_acc:
        in_specs += [qo, kvo, kvo]
        args += list(acc)
    out_specs = [qo, kvo, kvo]
    out_shape = [jax.ShapeDtypeStruct((L, dil * n_heads * LANES), F32),
                 jax.ShapeDtypeStruct((L, dil * n_kv * LANES), F32),
                 jax.ShapeDtypeStruct((L, dil * n_kv * LANES), F32)]
    if use_sink:
        out_specs.append(pl.BlockSpec((None, 1, LANES), lambda r, h: (h, 0, 0)))
        out_shape.append(jax.ShapeDtypeStruct((n_heads, 1, LANES), F32))
    return pl.pallas_call(
        body, name=name, grid=(dil, n_heads),
        in_specs=in_specs, out_specs=out_specs, out_shape=out_shape,
        scratch_shapes=[pltpu.VMEM((L, LANES), BF16), pltpu.VMEM((L, LANES), BF16),
                        pltpu.VMEM((L, LANES), F32), pltpu.VMEM((L, LANES), F32)],
        compiler_params=_cparams("parallel", "arbitrary"),
    )(*args)


def merge_branches(parts, *, name, tr=256):
    S, W = parts[0][0].shape
    tr = _tile(S, tr)
    nbr = len(parts)

    def body(*refs):
        ins, (o_ref, lse_ref) = refs[:3 * nbr], refs[3 * nbr:]
        nums = [ins[3 * i][...] for i in range(nbr)]
        ms = [ins[3 * i + 1][...] for i in range(nbr)]
        ls = [ins[3 * i + 2][...] for i in range(nbr)]
        mx = functools.reduce(jnp.maximum, ms)
        ws = [jnp.exp(m - mx) for m in ms]
        num = sum(w * n for w, n in zip(ws, nums))
        den = sum(w * l for w, l in zip(ws, ls))
        o_ref[...] = num / den
        lse_ref[...] = mx + jnp.log(den)

    row = pl.BlockSpec((tr, W), lambda i: (i, 0))
    flat = [a for p in parts for a in p]
    return pl.pallas_call(
        body, name=name, grid=(S // tr,),
        in_specs=[row] * len(flat), out_specs=[row, row],
        out_shape=[jax.ShapeDtypeStruct((S, W), F32)] * 2,
        compiler_params=_cparams("parallel"),
    )(*flat)


def _sigmoid(x):
    return 1.0 / (1.0 + jnp.exp(-x))


def _tri(n, lower):
    r = lax.broadcasted_iota(jnp.int32, (n, n), 0)
    c = lax.broadcasted_iota(jnp.int32, (n, n), 1)
    return ((c <= r) if lower else (c >= r)).astype(F32)


def _dot_exact(a, b, trans_a=False, trans_b=False):
    dims = (((0,) if trans_a else (1,), (1,) if trans_b else (0,)), ((), ()))
    return lax.dot_general(a, b, dims, preferred_element_type=F32, precision=lax.Precision.HIGHEST)


def _hgrn_gates(qb, fb, lb):
    sq = _sigmoid(qb)
    q = qb * sq * (B_DIM ** -0.5)
    sf = _sigmoid(fb)
    gate = lb + (1.0 - lb) * sf
    return q, 1.0 - gate, gate, sq, sf


def _hgrn_intra_fwd(q_ref, k_ref, b_ref):
    C, n_sub = B_CHUNK, B_CHUNK // B_SUB
    b_all, k_all = b_ref[...], k_ref[...]
    srow = lax.broadcasted_iota(jnp.int32, (C, LANES), 0)
    lane = lax.broadcasted_iota(jnp.int32, (B_SUB, C), 1)
    trow = lax.broadcasted_iota(jnp.int32, (B_SUB, LANES), 0)
    blocks = []
    for i in range(n_sub):
        r0 = i * B_SUB
        qi, bi = q_ref[pl.ds(r0, B_SUB), :], b_ref[pl.ds(r0, B_SUB), :]
        if i == 0:
            a_i = jnp.zeros((B_SUB, C), F32)
        else:
            ref_b = b_ref[pl.ds(r0 - 1, 1), :]
            qt = (qi * jnp.exp(bi - ref_b)).astype(BF16)
            kt = jnp.where(srow < r0, k_all * jnp.exp(jnp.minimum(ref_b - b_all, 0.0)), 0.0).astype(BF16)
            a_i = _dot_nt(qt, kt)
        for sl in range(B_SUB):
            s = r0 + sl
            e = jnp.where(trow >= sl, jnp.exp(jnp.minimum(bi - b_ref[pl.ds(s, 1), :], 0.0)), 0.0)
            colv = jnp.sum(qi * k_ref[pl.ds(s, 1), :] * e, axis=-1, keepdims=True)
            a_i = a_i + jnp.where(lane == s, colv, 0.0)
        blocks.append(a_i)
    return jnp.concatenate(blocks, axis=0)


def hgrn_fwd(src, lb, norm_g, *, S, n_heads, q_blk, f_blk, i_blk, g_blk, name):
    nc = S // B_CHUNK

    def body(qb_ref, fb_ref, ib_ref, gb_ref, lb_ref, ng_ref, out_ref, o_ref, st_ref, state_ref, q_s, k_s, b_s):
        state_ref[...] = jnp.zeros_like(state_ref)
        tril = _tri(B_CHUNK, True)

        def step(c, carry):
            rows = pl.ds(pl.multiple_of(c * B_CHUNK, B_CHUNK), B_CHUNK)
            q, k, gate, _, _ = _hgrn_gates(qb_ref[rows, :], fb_ref[rows, :], lb_ref[...])
            b = _dot_exact(tril, jnp.log(gate))
            vf = ib_ref[rows, :]
            v = vf.astype(BF16)
            q_s[...], k_s[...], b_s[...] = q, k, b
            st = state_ref[...]
            st_ref[c] = st
            a = _hgrn_intra_fwd(q_s, k_s, b_s)
            o = _dot(a.astype(BF16), v) + _dot_nt((q * jnp.exp(b)).astype(BF16), st.astype(BF16))
            b_last = b_s[pl.ds(B_CHUNK - 1, 1), :]
            state_ref[...] = st * jnp.exp(b_last) + _dot_exact(vf, k * jnp.exp(b_last - b), trans_a=True)
            o_ref[rows, :] = o
            rstd = lax.rsqrt(jnp.mean(o * o, axis=-1, keepdims=True) + NORM_EPS)
            gb = gb_ref[rows, :]
            out_ref[rows, :] = o * rstd * ng_ref[...] * (gb * _sigmoid(gb))
            return carry

        lax.fori_loop(0, nc, step, 0)

    def col(off):
        return pl.BlockSpec((S, LANES), lambda h: (0, off + h))

    hv = pl.BlockSpec((None, 1, LANES), lambda h: (h, 0, 0))
    return pl.pallas_call(
        body, name=name, grid=(n_heads,),
        in_specs=[col(q_blk), col(f_blk), col(i_blk), col(g_blk), hv, pl.BlockSpec((1, LANES), lambda h: (0, 0))],
        out_specs=[col(0), col(0), pl.BlockSpec((None, nc, LANES, LANES), lambda h: (h, 0, 0, 0))],
        out_shape=[jax.ShapeDtypeStruct((S, n_heads * LANES), F32), jax.ShapeDtypeStruct((S, n_heads * LANES), F32),
                   jax.ShapeDtypeStruct((n_heads, nc, LANES, LANES), F32)],
        scratch_shapes=[pltpu.VMEM((LANES, LANES), F32)] + [pltpu.VMEM((B_CHUNK, LANES), F32)] * 3,
        compiler_params=_cparams("parallel"),
    )(src, src, src, src, lb, norm_g.reshape(1, LANES))


def hgrn_bwd(src, lb, norm_g, o, states, dout, *, S, n_heads, q_blk, f_blk, i_blk, g_blk, dout_blk, name):
    nc = S // B_CHUNK
    C, n_sub = B_CHUNK, B_CHUNK // B_SUB

    def body(qb_ref, fb_ref, ib_ref, gb_ref, lb_ref, ng_ref, o_ref, st_ref, dout_ref,
             dqb_ref, dfb_ref, dib_ref, dgb_ref, dlb_ref, dng_ref, dstate_ref, q_s, k_s, b_s, dq_s, dk_s):
        @pl.when(pl.program_id(0) == 0)
        def _():
            dng_ref[...] = jnp.zeros_like(dng_ref)

        dstate_ref[...] = jnp.zeros_like(dstate_ref)
        tril = _tri(C, True)
        triu = _tri(C, False)
        lbv, ngv = lb_ref[...], ng_ref[...]
        srow = lax.broadcasted_iota(jnp.int32, (C, LANES), 0)
        lane = lax.broadcasted_iota(jnp.int32, (B_SUB, C), 1)
        trow = lax.broadcasted_iota(jnp.int32, (B_SUB, LANES), 0)
        causal = lax.broadcasted_iota(jnp.int32, (C, C), 1) <= lax.broadcasted_iota(jnp.int32, (C, C), 0)

        def step(ci, carry):
            dlog_carry, dlb_acc, dng_acc = carry
            c = nc - 1 - ci
            rows = pl.ds(pl.multiple_of(c * C, C), C)
            qb, fb, gb = qb_ref[rows, :], fb_ref[rows, :], gb_ref[rows, :]
            q, k, gate, sq, sf = _hgrn_gates(qb, fb, lbv)
            b = _dot_exact(tril, jnp.log(gate))
            vf = ib_ref[rows, :]
            v = vf.astype(BF16)
            q_s[...], k_s[...], b_s[...] = q, k, b
            st = st_ref[c]
            dst = dstate_ref[...]

            ov = o_ref[rows, :]
            dout = dout_ref[rows, :]
            rstd = lax.rsqrt(jnp.mean(ov * ov, axis=-1, keepdims=True) + NORM_EPS)
            xhat = ov * rstd
            sg = _sigmoid(gb)
            dy = dout * (gb * sg)
            dgb_ref[rows, :] = dout * (xhat * ngv) * (sg * (1.0 + gb * (1.0 - sg)))
            dng_acc = dng_acc + jnp.sum(dy * xhat, axis=0, keepdims=True)
            dxhat = dy * ngv
            do = rstd * (dxhat - xhat * jnp.mean(dxhat * xhat, axis=-1, keepdims=True))
            dob = do.astype(BF16)

            eb = jnp.exp(b)
            b_last = b_s[pl.ds(C - 1, 1), :]
            ebl = jnp.exp(b_last - b)
            a = _hgrn_intra_fwd(q_s, k_s, b_s)
            da = jnp.where(causal, _dot_exact(do, vf, trans_b=True), 0.0)
            dv = _dot_tn(a.astype(BF16), dob) + _dot_nt((k * ebl).astype(BF16), dst.astype(BF16))
            dq_s[...] = _dot_exact(do, st) * eb
            dk_s[...] = _dot_exact(vf, dst) * ebl
            dstate_ref[...] = dst * jnp.exp(b_last) + _dot_exact(do, q * eb, trans_a=True)
            b_all, k_all = b, k
            for i in range(n_sub):
                r0 = i * B_SUB
                blk = pl.ds(r0, B_SUB)
                qi, bi = q_s[blk, :], b_s[blk, :]
                da_i = da[r0:r0 + B_SUB, :]
                dq_i = jnp.zeros((B_SUB, LANES), F32)
                if i > 0:
                    ref_b = b_s[pl.ds(r0 - 1, 1), :]
                    eq = jnp.exp(bi - ref_b)
                    ek = jnp.where(srow < r0, jnp.exp(jnp.minimum(ref_b - b_all, 0.0)), 0.0)
                    da_off = jnp.where(lane < r0, da_i, 0.0)
                    dq_i = _dot_exact(da_off, k_all * ek) * eq
                    dk_s[...] += _dot_exact(da_off, qi * eq, trans_a=True) * ek
                for sl in range(B_SUB):
                    s = r0 + sl
                    e = jnp.where(trow >= sl, jnp.exp(jnp.minimum(bi - b_s[pl.ds(s, 1), :], 0.0)), 0.0)
                    dac = jnp.sum(jnp.where(lane == s, da_i, 0.0), axis=-1, keepdims=True)
                    dq_i = dq_i + dac * k_s[pl.ds(s, 1), :] * e
                    dk_s[pl.ds(s, 1), :] += jnp.sum(dac * qi * e, axis=0, keepdims=True)
                dq_s[blk, :] += dq_i
            dq, dk = dq_s[...], dk_s[...]
            db = q * dq - k * dk
            dlog = _dot_exact(triu, db) + dlog_carry
            dlog_carry = dlog_carry + jnp.sum(db, axis=0, keepdims=True)
            dgate = dlog / gate - dk
            dqb_ref[rows, :] = dq * (B_DIM ** -0.5) * (sq * (1.0 + qb * (1.0 - sq)))
            dfb_ref[rows, :] = dgate * (1.0 - lbv) * sf * (1.0 - sf)
            dib_ref[rows, :] = dv
            dlb_acc = dlb_acc + jnp.sum(dgate * (1.0 - sf), axis=0, keepdims=True)
            return dlog_carry, dlb_acc, dng_acc

        z = jnp.zeros((1, LANES), F32)
        _, dlb_acc, dng_acc = lax.fori_loop(0, nc, step, (z, z, z))
        dlb_ref[...] = dlb_acc
        dng_ref[...] += dng_acc

    def col(off):
        return pl.BlockSpec((S, LANES), lambda h: (0, off + h))

    hv = pl.BlockSpec((None, 1, LANES), lambda h: (h, 0, 0))
    vec = pl.BlockSpec((1, LANES), lambda h: (0, 0))
    full = jax.ShapeDtypeStruct((S, n_heads * LANES), F32)
    return pl.pallas_call(
        body, name=name, grid=(n_heads,),
        in_specs=[col(q_blk), col(f_blk), col(i_blk), col(g_blk), hv, vec, col(0),
                  pl.BlockSpec((None, nc, LANES, LANES), lambda h: (h, 0, 0, 0)), col(dout_blk)],
        out_specs=[col(0), col(0), col(0), col(0), hv, vec],
        out_shape=[full, full, full, full, jax.ShapeDtypeStruct((n_heads, 1, LANES), F32),
                   jax.ShapeDtypeStruct((1, LANES), F32)],
        scratch_shapes=[pltpu.VMEM((LANES, LANES), F32)] + [pltpu.VMEM((B_CHUNK, LANES), F32)] * 5,
        compiler_params=_cparams("arbitrary"),
    )(src, src, src, src, lb, norm_g.reshape(1, LANES), o, states, dout)


def _pad_heads_cols(w, n_heads):
    lead = w.shape[:-1]
    w = w.reshape(lead + (n_heads, C_DIM))
    w = jnp.pad(w, [(0, 0)] * len(lead) + [(0, 0), (0, LANES - C_DIM)])
    return w.reshape(lead + (n_heads * LANES,))


def _unpad_heads_cols(w, n_heads):
    lead = w.shape[:-1]
    return w.reshape(lead + (n_heads, LANES))[..., :C_DIM].reshape(lead + (n_heads * C_DIM,))


def _pad_heads_rows(w, n_heads):
    w = jnp.pad(w.reshape(n_heads, C_DIM, w.shape[1]), [(0, 0), (0, LANES - C_DIM), (0, 0)])
    return w.reshape(n_heads * LANES, w.shape[2])


def _unpad_heads_rows(w, n_heads):
    return w.reshape(n_heads, LANES, w.shape[1])[:, :C_DIM].reshape(n_heads * C_DIM, w.shape[1])


def _lower_bounds(lb_raw):
    lb_soft = jax.nn.softmax(lb_raw.astype(F32), axis=0)
    return jnp.cumsum(lb_soft, axis=0) - lb_soft[0:1]


def _relu2(u):
    r = jnp.maximum(u, 0.0)
    return r * r


def local_step(x, target, P, layer_weights, after_mixer):
    S, D = x.shape
    depth = P["norm_mix_g"].shape[0]
    HA = D // 2 // A_DIM
    HB = HA
    HQ = D // C_DIM
    HKV = HQ // C_GROUP
    even_in = 7 * HA * LANES
    tabs_a = rope_tables(S, A_DIM)
    tabs_c = rope_tables(S, C_DIM)
    half_a, half_c = tabs_a[3], tabs_c[3]
    lbounds = _lower_bounds(P["hgrn_lb_raw"])
    a_kw = dict(n_heads=HA, group=1, q_blk=0, k_blk=HA, v_blk=2 * HA, blk_per_row=7 * HA, scale=A_DIM ** -0.5,
                half=half_a)
    b_kw = dict(S=S, n_heads=HB, q_blk=3 * HA, f_blk=3 * HA + HB, i_blk=3 * HA + 2 * HB, g_blk=3 * HA + 3 * HB)
    c_kw = dict(L=S, dil=1, n_heads=HQ, group=C_GROUP, q_blk=0, k_blk=HQ, v_blk=HQ + HKV, blk_per_row=HQ + 2 * HKV,
                max_dist=C_WINDOW - 1, scale=C_DIM ** -0.5, half=half_c)

    def a_tabs(dil):
        return tuple(t.reshape(S // dil, dil * LANES) for t in tabs_a[:3])

    saved = []
    for layer in range(depth):
        Wl = layer_weights(layer, x)
        sv = {"x0": x, "W": Wl}
        idx = layer // 2
        h = rmsnorm_fwd(x, P["norm_mix_g"][layer], name="norm_mix_fwd")
        sv["h"] = h
        if layer % 2 == 0:
            proj = matmul(h, Wl["even_w_in"], out_dtypes=(F32,), name="even_in_fwd")
            parts = []
            for window, dil in A_BRANCHES:
                L = S // dil
                res = band_attention_fwd(proj.reshape(L, dil * even_in), a_tabs(dil), L=L, dil=dil,
                                         max_dist=window // dil, normalize=False, name=f"dilated_fwd_d{dil}", **a_kw)
                parts.append(tuple(r.reshape(S, HA * LANES) for r in res))
            oa, lse = merge_branches(parts, name="dilated_merge")
            lb_e = lbounds[idx].reshape(HB, 1, LANES)
            ob, o_raw, states = hgrn_fwd(proj, lb_e, P["hgrn_norm_g"][idx], name="hgrn_fwd", **b_kw)
            mixed = jnp.concatenate([oa, ob], axis=1).astype(BF16)
            sv.update(proj=proj, oa=oa, lse=lse, o_raw=o_raw, states=states, mixed=mixed, lb=lb_e)
            x = matmul(mixed, Wl["even_w_out"], extras=(x,), epilogue=lambda acc, r: (acc + r,),
                       out_dtypes=(F32,), name="even_out_fwd")
        else:
            proj = matmul(h, Wl["odd_w_qkv"], extras=(P["odd_b_qkv"][idx].reshape(1, -1),),
                          epilogue=lambda acc, b: (acc + b,), out_dtypes=(F32,), name="odd_qkv_fwd")
            sink = jnp.broadcast_to(P["odd_sinks"][idx].reshape(HQ, 1, 1), (HQ, 1, LANES))
            o, lse = band_attention_fwd(proj, tabs_c[:3], sink=sink, normalize=True, name="swa_fwd", **c_kw)
            sv.update(proj=proj, o=o, lse=lse, sink=sink)
            x = matmul(o, Wl["odd_w_o"], extras=(P["odd_b_o"][idx].reshape(1, D), x),
                       epilogue=lambda acc, b, r: (acc + b + r,), out_dtypes=(F32,), name="odd_out_fwd")
        sv["x1"] = x
        h2 = rmsnorm_fwd(x, P["norm_mlp_g"][layer], dep=after_mixer(layer, x), name="norm_mlp_fwd")
        u = matmul(h2, Wl["mlp_w1"], out_dtypes=(BF16,), name="mlp_up_fwd")
        x = matmul(u, Wl["mlp_w2"], a_fn=_relu2, extras=(x,), epilogue=lambda acc, r: (acc + r,),
                   out_dtypes=(F32,), name="mlp_down_fwd")
        sv.update(h2=h2, u=u)
        saved.append(sv)

    loss, dx, dgf = final_norm_loss(x, P["final_norm_g"], target, name="final_norm_loss")

    G = {k: [None] * depth for k in ("norm_mix_g", "norm_mlp_g")}
    for k in ("hgrn_lb", "hgrn_norm_g"):
        G[k] = [None] * ((depth + 1) // 2)
    for k in ("odd_w_qkv", "odd_b_qkv", "odd_sinks", "odd_w_o", "odd_b_o"):
        G[k] = [None] * (depth // 2)
    G["final_norm_g"] = dgf.reshape(D)
    B = {}

    def wgrad(a, d, key, idx, name, a_fn=None):
        M, N = a.shape[1], d.shape[1]
        into = (grad_buffer_shape((M, N), SHARDING[key]), SHARDING[key])
        B[key, idx] = matmul(a, d, trans_a=True, a_fn=a_fn, out_dtypes=(BF16,), into=into, name=name)

    for layer in reversed(range(depth)):
        sv = saved[layer]
        Wl = sv["W"]
        idx = layer // 2
        du = matmul(dx, Wl["mlp_w2"], trans_b=True, extras=(sv["u"],),
                    epilogue=lambda acc, u: (acc * (2.0 * jnp.maximum(u.astype(F32), 0.0)),),
                    out_dtypes=(BF16,), name="mlp_down_bwd_x")
        wgrad(sv["u"], dx, "mlp_w2", layer, "mlp_down_bwd_w", a_fn=_relu2)
        wgrad(sv["h2"], du, "mlp_w1", layer, "mlp_up_bwd_w")
        dh2 = matmul(du, Wl["mlp_w1"], trans_b=True, out_dtypes=(F32,), name="mlp_up_bwd_x")
        dx, dg = rmsnorm_bwd(sv["x1"], P["norm_mlp_g"][layer], dh2, dx, name="norm_mlp_bwd")
        G["norm_mlp_g"][layer] = dg.reshape(D)
        if layer % 2 == 0:
            dmixed = matmul(dx, Wl["even_w_out"], trans_b=True, out_dtypes=(F32,), name="even_out_bwd_x")
            wgrad(sv["mixed"], dx, "even_w_out", idx, "even_out_bwd_w")
            acc = None
            for window, dil in A_BRANCHES:
                L = S // dil
                rs = lambda a: a.reshape(L, -1)
                acc = band_attention_bwd(rs(sv["proj"]), a_tabs(dil), rs(sv["oa"]), rs(sv["lse"]), rs(dmixed),
                                         L=L, dil=dil, max_dist=window // dil, do_bpr=2 * HA,
                                         acc=None if acc is None else tuple(rs(a) for a in acc),
                                         name=f"dilated_bwd_d{dil}", **a_kw)
                acc = tuple(a.reshape(S, HA * LANES) for a in acc)
            dqb, dfb, dib, dgb, dlb, dng = hgrn_bwd(sv["proj"], sv["lb"], P["hgrn_norm_g"][idx], sv["o_raw"],
                                                    sv["states"], dmixed, dout_blk=HA, name="hgrn_bwd", **b_kw)
            G["hgrn_lb"][idx] = dlb.reshape(HB * LANES)
            G["hgrn_norm_g"][idx] = dng.reshape(LANES)
            dproj = jnp.concatenate(list(acc) + [dqb, dfb, dib, dgb], axis=1).astype(BF16)
            wgrad(sv["h"], dproj, "even_w_in", idx, "even_in_bwd_w")
            dh = matmul(dproj, Wl["even_w_in"], trans_b=True, out_dtypes=(F32,), name="even_in_bwd_x")
        else:
            do = matmul(dx, Wl["odd_w_o"], trans_b=True, out_dtypes=(F32,), name="odd_out_bwd_x")
            G["odd_b_o"][idx] = colsum(dx, name="odd_out_bwd_b").reshape(D)
            G["odd_w_o"][idx] = matmul(sv["o"], dx, trans_a=True, out_dtypes=(BF16,), name="odd_out_bwd_w")
            dq, dk, dv, dsink = band_attention_bwd(sv["proj"], tabs_c[:3], sv["o"], sv["lse"], do, sink=sv["sink"],
                                                   do_bpr=HQ, name="swa_bwd", **c_kw)
            G["odd_sinks"][idx] = dsink[:, 0, 0]
            dproj = jnp.concatenate([dq, dk, dv], axis=1)
            G["odd_b_qkv"][idx] = colsum(dproj, name="odd_qkv_bwd_b").reshape(-1)
            dproj = dproj.astype(BF16)
            G["odd_w_qkv"][idx] = matmul(sv["h"], dproj, trans_a=True, out_dtypes=(BF16,), name="odd_qkv_bwd_w")
            dh = matmul(dproj, Wl["odd_w_qkv"], trans_b=True, out_dtypes=(F32,), name="odd_qkv_bwd_x")
            HQ2 = HQ + 2 * HKV
            B["odd_w_qkv", idx] = full_to_buffer(_unpad_heads_cols(G["odd_w_qkv"][idx], HQ2), "cols")
            B["odd_w_o", idx] = full_to_buffer(_unpad_heads_rows(G["odd_w_o"][idx], HQ), "rows")
        dx, dg = rmsnorm_bwd(sv["x0"], P["norm_mix_g"][layer], dh, dx, name="norm_mix_bwd")
        G["norm_mix_g"][layer] = dg.reshape(D)

    del G["odd_w_qkv"], G["odd_w_o"]
    grads = {k: jnp.stack(v) if isinstance(v, list) else v for k, v in G.items()}
    _, lb_vjp = jax.vjp(_lower_bounds, P["hgrn_lb_raw"])
    grads["hgrn_lb_raw"] = lb_vjp(grads.pop("hgrn_lb"))[0]
    grads["odd_b_qkv"] = _unpad_heads_cols(grads["odd_b_qkv"], HQ + 2 * HKV)
    return loss, dx, grads, B


def full_to_buffer(full, sharding):
    M, N = full.shape
    if sharding == "cols":
        return jnp.transpose(full.reshape(2, M // 2, 4, N // 4), (2, 0, 1, 3))
    return full.reshape(4, 2, M // 8, N)


MESH = pl.DeviceIdType.MESH
ANY = pl.BlockSpec(memory_space=pl.ANY)
SHARDING = {"even_w_in": "cols", "even_w_out": "rows", "odd_w_qkv": "cols", "odd_w_o": "rows",
            "mlp_w1": "cols", "mlp_w2": "rows"}
BIG = tuple(SHARDING)


def _mesh_pos():
    return lax.axis_index("x"), lax.axis_index("y"), lax.axis_index("c")


def _other_chips(x, y):
    return [(1 - x, y), (x, 1 - y), (1 - x, 1 - y)]


def cast_into_full(w, layer, sharding, s, *, name, tr=256):
    n, R, C = w.shape
    tr = _tile(R, tr)
    nr = R // tr

    def body(s_ref, w_ref, o_ref):
        o_ref[...] = w_ref[...].astype(BF16)

    if sharding == "cols":
        full, out_map = (R, 4 * C), (lambda i, s_ref: (i, s_ref[0]))
    else:
        full, out_map = (4 * R, C), (lambda i, s_ref: (s_ref[0] * nr + i, 0))
    grid_spec = pltpu.PrefetchScalarGridSpec(
        num_scalar_prefetch=1, grid=(nr,),
        in_specs=[pl.BlockSpec((None, tr, C), lambda i, s_ref: (layer, i, 0))],
        out_specs=pl.BlockSpec((tr, C), out_map))
    return pl.pallas_call(
        body, name=name, grid_spec=grid_spec, out_shape=jax.ShapeDtypeStruct(full, BF16),
        compiler_params=_cparams("parallel"),
    )(s.reshape(1).astype(jnp.int32), w)


HBM = pl.BlockSpec(memory_space=pltpu.HBM)
SEM = pl.BlockSpec(memory_space=pltpu.SEMAPHORE)
EFFECT = pltpu.SideEffectType.DATAFLOW_SIDE_EFFECTING


def _weight_window(ref, sharding, s, hf):
    M, N = ref.shape
    if sharding == "cols":
        rh, C = M // 2, N // 4
        return ref.at[pl.ds(hf * rh, rh), pl.ds(s * C, C)]
    rh = M // 8
    return ref.at[pl.ds((2 * s + hf) * rh, rh), :]


def _in_hbm(a):
    return pltpu.with_memory_space_constraint(a, pltpu.HBM)


def gather_start(groups, shardings, *, name):
    sizes = [len(g) for g in groups]
    flat = [m for g in groups for m in g]
    flat_sh = [sh for g in shardings for sh in g]
    T, ng = len(flat), len(groups)

    def body(*refs):
        sems = refs[T:T + 6 * ng]
        thru = refs[T + 6 * ng:2 * T + 6 * ng]
        token = refs[2 * T + 6 * ng]
        x, y, c = _mesh_pos()
        pos = 0
        for g in range(ng):
            for t in range(sizes[g]):
                w = _weight_window(thru[pos], flat_sh[pos], 2 * x + y, c)
                for j, (cx, cy) in enumerate(_other_chips(x, y)):
                    pltpu.make_async_remote_copy(src_ref=w, dst_ref=w, send_sem=sems[6 * g + j],
                                                 recv_sem=sems[6 * g + 3 + j], device_id=(cx, cy, c),
                                                 device_id_type=MESH).start()
                pos += 1
        token[...] = jnp.zeros_like(token)

    outs = pl.pallas_call(
        body, name=name,
        in_specs=[HBM] * T,
        out_specs=[SEM] * (6 * ng) + [HBM] * T + [pl.BlockSpec(memory_space=pltpu.VMEM)],
        out_shape=[pltpu.SemaphoreType.DMA(())] * (6 * ng) + [pltpu.HBM(m.shape, m.dtype) for m in flat]
        + [jax.ShapeDtypeStruct((8, LANES), F32)],
        input_output_aliases={t: 6 * ng + t for t in range(T)},
        compiler_params=pltpu.CompilerParams(has_side_effects=EFFECT),
    )(*[_in_hbm(m) for m in flat])
    res, pos = [], 6 * ng
    for g in range(ng):
        res.append((list(outs[6 * g:6 * g + 6]), list(outs[pos:pos + sizes[g]])))
        pos += sizes[g]
    return res, outs[-1]


def gather_forward(mats, shardings, sems, after, *, name):
    T = len(mats)

    def body(*refs):
        sems1 = refs[T:T + 6]
        sems2 = refs[T + 7:T + 13]
        thru = refs[T + 13:2 * T + 13]
        token = refs[2 * T + 13]
        x, y, c = _mesh_pos()
        chips = _other_chips(x, y)
        for t in range(T):
            own = _weight_window(thru[t], shardings[t], 2 * x + y, c)
            for j, (cx, cy) in enumerate(chips):
                landed = _weight_window(thru[t], shardings[t], 2 * cx + cy, c)
                first = pltpu.make_async_remote_copy(src_ref=own, dst_ref=landed, send_sem=sems1[j],
                                                     recv_sem=sems1[3 + j], device_id=(cx, cy, c),
                                                     device_id_type=MESH)
                first.wait_recv()
                first.wait_send()
        for t in range(T):
            for j, (cx, cy) in enumerate(chips):
                landed = _weight_window(thru[t], shardings[t], 2 * cx + cy, c)
                pltpu.make_async_remote_copy(src_ref=landed, dst_ref=landed, send_sem=sems2[j],
                                             recv_sem=sems2[3 + j], device_id=(x, y, 1 - c),
                                             device_id_type=MESH).start()
        token[...] = jnp.zeros_like(token)

    outs = pl.pallas_call(
        body, name=name,
        in_specs=[HBM] * T + [SEM] * 6 + [ANY],
        out_specs=[SEM] * 6 + [HBM] * T + [pl.BlockSpec(memory_space=pltpu.VMEM)],
        out_shape=[pltpu.SemaphoreType.DMA(())] * 6 + [pltpu.HBM(m.shape, m.dtype) for m in mats]
        + [jax.ShapeDtypeStruct((8, LANES), F32)],
        input_output_aliases={t: 6 + t for t in range(T)},
        compiler_params=pltpu.CompilerParams(has_side_effects=EFFECT),
    )(*mats, *sems, after)
    return list(outs[:6]), list(outs[6:6 + T]), outs[-1]


def gather_finish(mats, shardings, sems, after, *, name):
    T = len(mats)

    def body(*refs):
        sems2 = refs[T:T + 6]
        thru = refs[T + 7:2 * T + 7]
        x, y, c = _mesh_pos()
        for t in range(T):
            for j, (cx, cy) in enumerate(_other_chips(x, y)):
                sent = _weight_window(thru[t], shardings[t], 2 * cx + cy, c)
                other = _weight_window(thru[t], shardings[t], 2 * cx + cy, 1 - c)
                cp = pltpu.make_async_remote_copy(src_ref=sent, dst_ref=other, send_sem=sems2[j],
                                                  recv_sem=sems2[3 + j], device_id=(x, y, 1 - c),
                                                  device_id_type=MESH)
                cp.wait_recv()
                cp.wait_send()

    outs = pl.pallas_call(
        body, name=name,
        in_specs=[HBM] * T + [SEM] * 6 + [ANY],
        out_specs=[HBM] * T,
        out_shape=[pltpu.HBM(m.shape, m.dtype) for m in mats],
        input_output_aliases={t: t for t in range(T)},
        compiler_params=pltpu.CompilerParams(has_side_effects=EFFECT),
    )(*mats, *sems, after)
    return list(outs)


def exchange_halves(bufs, *, name):
    T = len(bufs)

    def body(*refs):
        in_refs, out_refs = refs[:T], refs[T:2 * T]
        send_sems, recv_sems = refs[2 * T:]
        x, y, c = _mesh_pos()
        cps = [pltpu.make_async_remote_copy(src_ref=in_refs[t].at[:, pl.ds(1 - c, 1)], dst_ref=out_refs[t],
                                            send_sem=send_sems.at[t], recv_sem=recv_sems.at[t],
                                            device_id=(x, y, 1 - c), device_id_type=MESH) for t in range(T)]
        for cp in cps:
            cp.start()
        for cp in cps:
            cp.wait()

    return pl.pallas_call(
        body, name=name, in_specs=[ANY] * T, out_specs=[ANY] * T,
        out_shape=[jax.ShapeDtypeStruct((4, 1) + b.shape[2:], b.dtype) for b in bufs],
        scratch_shapes=[pltpu.SemaphoreType.DMA((T,)), pltpu.SemaphoreType.DMA((T,))],
    )(*bufs)


def pair_sum(buf, got, c, *, name, tr=512):
    _, _, n, rh, C = buf.shape
    tr = _tile(rh, tr)

    def body(c_ref, a_ref, b_ref, o_ref):
        o_ref[...] = (a_ref[...].astype(F32) + b_ref[...].astype(F32)).astype(o_ref.dtype)

    grid_spec = pltpu.PrefetchScalarGridSpec(
        num_scalar_prefetch=1, grid=(4, n, rh // tr),
        in_specs=[pl.BlockSpec((None, None, None, tr, C), lambda s, l, i, c_ref: (s, c_ref[0], l, i, 0)),
                  pl.BlockSpec((None, None, None, tr, C), lambda s, l, i, c_ref: (s, 0, l, i, 0))],
        out_specs=pl.BlockSpec((None, None, tr, C), lambda s, l, i, c_ref: (s, l, i, 0)))
    return pl.pallas_call(
        body, name=name, grid_spec=grid_spec, out_shape=jax.ShapeDtypeStruct((4, n, rh, C), BF16),
        compiler_params=_cparams("parallel", "parallel", "parallel"),
    )(c.reshape(1).astype(jnp.int32), buf, got)


def scatter_to_owners(sums, *, name):
    T = len(sums)

    def body(*refs):
        in_refs, out_refs = refs[:T], refs[T:2 * T]
        send_sems, recv_sems = refs[2 * T:]
        x, y, c = _mesh_pos()
        chips = _other_chips(x, y)
        cps = [pltpu.make_async_remote_copy(src_ref=in_refs[t].at[pl.ds(2 * cx + cy, 1)],
                                            dst_ref=out_refs[t].at[pl.ds(j, 1)],
                                            send_sem=send_sems.at[t, j], recv_sem=recv_sems.at[t, j],
                                            device_id=(cx, cy, c), device_id_type=MESH)
               for t in range(T) for j, (cx, cy) in enumerate(chips)]
        for cp in cps:
            cp.start()
        for cp in cps:
            cp.wait()

    return pl.pallas_call(
        body, name=name, in_specs=[ANY] * T, out_specs=[ANY] * T,
        out_shape=[jax.ShapeDtypeStruct((3,) + s.shape[1:], s.dtype) for s in sums],
        scratch_shapes=[pltpu.SemaphoreType.DMA((T, 3)), pltpu.SemaphoreType.DMA((T, 3))],
    )(*sums)


def owner_sum(sums, got, s, c, grad, layer, *, name, tr=512):
    rh, C = sums.shape[2:]
    tr = _tile(rh, tr)
    nr = rh // tr

    def body(sc_ref, a_ref, b0_ref, b1_ref, b2_ref, g_ref, o_ref):
        o_ref[...] = ((a_ref[...].astype(F32) + b0_ref[...].astype(F32)) + b1_ref[...].astype(F32)) \
            + b2_ref[...].astype(F32)

    def got_spec(j):
        return pl.BlockSpec((None, None, tr, C), lambda i, sc_ref: (j, 0, i, 0))

    grid_spec = pltpu.PrefetchScalarGridSpec(
        num_scalar_prefetch=1, grid=(nr,),
        in_specs=[pl.BlockSpec((None, None, tr, C), lambda i, sc_ref: (sc_ref[0], 0, i, 0)),
                  got_spec(0), got_spec(1), got_spec(2), ANY],
        out_specs=pl.BlockSpec((None, tr, C), lambda i, sc_ref: (layer, sc_ref[1] * nr + i, 0)))
    return pl.pallas_call(
        body, name=name, grid_spec=grid_spec, out_shape=jax.ShapeDtypeStruct(grad.shape, F32),
        input_output_aliases={5: 0},
        compiler_params=_cparams("parallel"),
    )(jnp.stack([s, c]).astype(jnp.int32), sums, got, got, got, grad)


def join_halves(grads, *, name):
    T = len(grads)

    def body(*refs):
        out_refs = refs[T:2 * T]
        send_sems, recv_sems = refs[2 * T:]
        x, y, c = _mesh_pos()

        def win(t, hf):
            rh = grads[t].shape[1] // 2
            return out_refs[t].at[:, pl.ds(hf * rh, rh), :]

        def remote(t, w):
            return pltpu.make_async_remote_copy(src_ref=w, dst_ref=w, send_sem=send_sems.at[t],
                                                recv_sem=recv_sems.at[t], device_id=(x, y, 1 - c),
                                                device_id_type=MESH)

        cps = [remote(t, win(t, c)) for t in range(T)]
        for cp in cps:
            cp.start()
        for t in range(T):
            remote(t, win(t, 1 - c)).wait_recv()
        for cp in cps:
            cp.wait_send()

    return pl.pallas_call(
        body, name=name, in_specs=[ANY] * T, out_specs=[ANY] * T,
        out_shape=[jax.ShapeDtypeStruct(g.shape, g.dtype) for g in grads],
        input_output_aliases={t: t for t in range(T)},
        scratch_shapes=[pltpu.SemaphoreType.DMA((T,)), pltpu.SemaphoreType.DMA((T,))],
    )(*grads)


def allreduce_small(v, *, name):
    rows = v.shape[0]

    def body(v_ref, out_ref, buf, send_sems, recv_sems):
        x, y, c = _mesh_pos()
        me = 4 * x + 2 * y + c
        flips = [(dx, dy, dc) for dx in (0, 1) for dy in (0, 1) for dc in (0, 1)][1:]

        def peer(f):
            return tuple(1 - p if d else p for d, p in zip(f, (x, y, c)))

        cps = []
        for k, f in enumerate(flips):
            px, py, pc = peer(f)
            cps.append(pltpu.make_async_remote_copy(src_ref=v_ref, dst_ref=buf.at[me], send_sem=send_sems.at[k],
                                                    recv_sem=recv_sems.at[k], device_id=(px, py, pc),
                                                    device_id_type=MESH))
        for cp in cps:
            cp.start()
        buf[me] = v_ref[...]
        for k, f in enumerate(flips):
            px, py, pc = peer(f)
            slot = buf.at[4 * px + 2 * py + pc]
            pltpu.make_async_remote_copy(src_ref=slot, dst_ref=slot, send_sem=send_sems.at[k],
                                         recv_sem=recv_sems.at[k], device_id=(px, py, pc),
                                         device_id_type=MESH).wait_recv()
        for cp in cps:
            cp.wait_send()
        acc = buf[0]
        for i in range(1, 8):
            acc = acc + buf[i]
        out_ref[...] = acc

    vm = pl.BlockSpec(memory_space=pltpu.VMEM)
    return pl.pallas_call(
        body, name=name, in_specs=[vm], out_specs=vm,
        out_shape=jax.ShapeDtypeStruct(v.shape, F32),
        scratch_shapes=[pltpu.VMEM((8, rows, LANES), F32), pltpu.SemaphoreType.DMA((7,)),
                        pltpu.SemaphoreType.DMA((7,))],
    )(v)


def _adam_math(w, m, v, g):
    m = ADAM_B1 * m + (1.0 - ADAM_B1) * g
    v = ADAM_B2 * v + (1.0 - ADAM_B2) * (g * g)
    m_hat = m / (1.0 - ADAM_B1 ** ADAM_STEP)
    v_hat = v / (1.0 - ADAM_B2 ** ADAM_STEP)
    delta = -ADAM_LR * (m_hat / (jnp.sqrt(v_hat) + ADAM_EPS) + ADAM_WD * w)
    return delta, m, v


def adamw(w, m, v, g, *, name, tr=256):
    def body(w_ref, m_ref, v_ref, g_ref, d_ref, nm_ref, nv_ref):
        d, nm, nv = _adam_math(w_ref[...], m_ref[...], v_ref[...], g_ref[...])
        d_ref[...] = d
        nm_ref[...] = nm
        nv_ref[...] = nv

    shape = jax.ShapeDtypeStruct(w.shape, F32)
    if w.ndim == 2:
        return pl.pallas_call(body, name=name, out_shape=[shape] * 3)(w, m, v, g)
    n, R, C = w.shape
    tr = _tile(R, tr)
    spec = pl.BlockSpec((None, tr, C), lambda l, i: (l, i, 0))
    return pl.pallas_call(
        body, name=name, grid=(n, R // tr), in_specs=[spec] * 4, out_specs=[spec] * 3, out_shape=[shape] * 3,
        compiler_params=_cparams("parallel", "parallel"),
    )(w, m, v, g)


def _pack(parts):
    flat = jnp.concatenate([p.reshape(-1).astype(F32) for p in parts])
    size = -(-flat.shape[0] // (8 * LANES)) * (8 * LANES)
    return jnp.pad(flat, (0, size - flat.shape[0])).reshape(size // LANES, LANES)


def _unpack(block, shapes):
    flat = block.reshape(-1)
    out, pos = [], 0
    for shp in shapes:
        size = int(np.prod(shp))
        out.append(flat[pos:pos + size].reshape(shp))
        pos += size
    return out


SMALL = ("norm_mix_g", "norm_mlp_g", "final_norm_g", "hgrn_lb_raw", "hgrn_norm_g", "odd_sinks")
WEIGHTS = ("norm_mix_g", "norm_mlp_g", "final_norm_g", "even_w_in", "even_w_out", "hgrn_lb_raw", "hgrn_norm_g",
           "odd_w_qkv", "odd_b_qkv", "odd_sinks", "odd_w_o", "odd_b_o", "mlp_w1", "mlp_w2")


def kernel(x, norm_mix_g, norm_mlp_g, final_norm_g, even_w_in, even_w_out, hgrn_lb_raw, hgrn_norm_g, odd_w_qkv, odd_b_qkv, odd_sinks, odd_w_o, odd_b_o, mlp_w1, mlp_w2, loss_target, m_norm_mix_g, m_norm_mlp_g, m_final_norm_g, m_even_w_in, m_even_w_out, m_hgrn_lb_raw, m_hgrn_norm_g, m_odd_w_qkv, m_odd_b_qkv, m_odd_sinks, m_odd_w_o, m_odd_b_o, m_mlp_w1, m_mlp_w2, v_norm_mix_g, v_norm_mlp_g, v_final_norm_g, v_even_w_in, v_even_w_out, v_hgrn_lb_raw, v_hgrn_norm_g, v_odd_w_qkv, v_odd_b_qkv, v_odd_sinks, v_odd_w_o, v_odd_b_o, v_mlp_w1, v_mlp_w2):
    args = locals()
    W = {k: args[k] for k in WEIGHTS}
    M = {k: args["m_" + k] for k in WEIGHTS}
    V = {k: args["v_" + k] for k in WEIGHTS}
    _, S, D = x.shape
    depth = norm_mix_g.shape[0]
    HQ = D // C_DIM
    HKV = HQ // C_GROUP
    xi, yi, ci = _mesh_pos()
    shard = 2 * xi + yi

    def layer_keys(layer):
        mixer = ("even_w_in", "even_w_out") if layer % 2 == 0 else ("odd_w_qkv", "odd_w_o")
        return [(k, layer // 2) for k in mixer] + [("mlp_w1", layer), ("mlp_w2", layer)]

    keys = [layer_keys(layer) for layer in range(depth)]
    shardings = [[SHARDING[k] for k, _ in ks] for ks in keys]
    started, dep0 = gather_start(
        [[cast_into_full(W[k], i, SHARDING[k], shard, name="cast_" + k) for k, i in ks] for ks in keys],
        shardings, name="gather_start")
    forwarded = {}

    def forward_group(layer, after):
        sems, mats = started[layer]
        forwarded[layer] = gather_forward(mats, shardings[layer], sems, after, name="gather_forward")
        return forwarded[layer][2]

    def layer_weights(layer, x_in):
        if layer == 0:
            forward_group(0, dep0)
        sems, mats, _ = forwarded[layer]
        mats = gather_finish(mats, shardings[layer], sems, x_in, name="gather_finish")
        Wl = {k: m for (k, _), m in zip(keys[layer], mats)}
        if layer % 2 == 1:
            Wl["odd_w_qkv"] = _pad_heads_cols(Wl["odd_w_qkv"], HQ + 2 * HKV)
            Wl["odd_w_o"] = _pad_heads_rows(Wl["odd_w_o"], HQ)
        return Wl

    def after_mixer(layer, x1):
        return forward_group(layer + 1, x1) if layer + 1 < depth else None

    n_odd, bq = odd_b_qkv.shape
    bo = odd_b_o.shape[1]
    keep = (ci == 0).astype(F32)
    pieces = [lax.dynamic_update_slice(jnp.zeros((n_odd, 4 * bq), F32), odd_b_qkv * keep, (0, shard * bq)),
              lax.dynamic_update_slice(jnp.zeros((n_odd, 4 * bo), F32), odd_b_o * keep, (0, shard * bo))]
    b_qkv_full, b_o_full = _unpack(allreduce_small(_pack(pieces), name="gather_biases"),
                                   [(n_odd, 4 * bq), (n_odd, 4 * bo)])
    P = {k: W[k] for k in SMALL}
    P.update(odd_b_qkv=_pad_heads_cols(b_qkv_full, HQ + 2 * HKV), odd_b_o=b_o_full)

    loss, dx, G, B = local_step(x[0], loss_target[0], P, layer_weights, after_mixer)

    order = [ki for ks in keys for ki in ks]
    bufs = [B[ki].reshape((4, 2, 1) + B[ki].shape[2:]) for ki in order]
    got = exchange_halves(bufs, name="exchange_halves")
    sums = [pair_sum(b, g, ci, name="pair_sum_" + k) for (k, _), b, g in zip(order, bufs, got)]
    got = scatter_to_owners(sums, name="scatter_to_owners")
    reduced = {k: lax.empty(W[k].shape, F32) for k in BIG}
    for (k, i), s, g in zip(order, sums, got):
        reduced[k] = owner_sum(s, g, shard, ci, reduced[k], i, name="owner_sum_" + k)
    grads = dict(zip(BIG, join_halves([reduced[k] for k in BIG], name="join_halves")))

    small_keys = SMALL + ("odd_b_qkv", "odd_b_o")
    small_shapes = [G[k].shape for k in small_keys]
    small = _unpack(allreduce_small(_pack([G[k] for k in small_keys]), name="reduce_small"), small_shapes)
    grads.update(zip(small_keys, small))
    grads["odd_b_qkv"] = lax.dynamic_slice(grads["odd_b_qkv"], (0, shard * bq), (n_odd, bq))
    grads["odd_b_o"] = lax.dynamic_slice(grads["odd_b_o"], (0, shard * bo), (n_odd, bo))

    deltas, new_m, new_v = {}, {}, {}
    for k in WEIGHTS:
        w2 = (lambda a: a.reshape(1, -1)) if W[k].ndim == 1 else (lambda a: a)
        d, nm, nv = adamw(w2(W[k]), w2(M[k]), w2(V[k]), w2(grads[k]), name="adamw_" + k)
        deltas[k], new_m[k], new_v[k] = (a.reshape(W[k].shape) for a in (d, nm, nv))

    loss = lax.psum(loss[0, 0], ("x", "y", "c"))
    return (loss, dx[None], *[grads[k] for k in WEIGHTS], *[deltas[k] for k in WEIGHTS],
            *[new_m[k] for k in WEIGHTS], *[new_v[k] for k in WEIGHTS])
```

```python
import functools
import math

import jax
import jax.numpy as jnp
import numpy as np
from jax import lax
from jax.experimental import pallas as pl
from jax.experimental.pallas import tpu as pltpu

F32 = jnp.float32
BF16 = jnp.bfloat16

NORM_EPS = 1e-5
ROPE_THETA = 500000.0
ROPE_FRACTION = 4
LANES = 128
BAND = 128
A_DIM = 128
A_BRANCHES = ((128, 1), (512, 4), (2048, 16))
B_DIM = 128
B_CHUNK = 64
B_SUB = 16
C_DIM = 64
C_GROUP = 8
C_WINDOW = 128

ADAM_LR = 0.001
ADAM_B1 = 0.9
ADAM_B2 = 0.999
ADAM_EPS = 1e-08
ADAM_WD = 0.01
ADAM_STEP = 10

VMEM_LIMIT = 56 * 1024 * 1024


def _cparams(*sem):
    return pltpu.CompilerParams(dimension_semantics=tuple(sem), vmem_limit_bytes=VMEM_LIMIT)


def _tile(n, want):
    if n <= want:
        return n
    t = want - want % LANES
    while n % t:
        t -= LANES
    assert t > 0, (n, want)
    return t


def matmul(a, b, *, trans_a=False, trans_b=False, out_dtypes, a_fn=None, epilogue=None, extras=(), name,
           b_layer=None, into=None, dep=None, tm=1024, tn=1024, tk=2048):
    if trans_a:
        K, M = a.shape
    else:
        M, K = a.shape
    b_mat = b.shape[1:] if b_layer is not None else b.shape
    if trans_b:
        N, K2 = b_mat
    else:
        K2, N = b_mat
    assert K == K2, (a.shape, b.shape)
    if into is not None:
        rh, cs = into[0][2:]
        tm, tn = _tile(rh, tm), _tile(cs, tn)
    tm, tn, tk = _tile(M, tm), _tile(N, tn), _tile(K, tk)
    nk = K // tk
    n_extra = len(extras)
    n_out = len(out_dtypes)

    def body(*refs):
        a_ref, b_ref = refs[0], refs[1]
        extra_refs = refs[2:2 + n_extra]
        out_refs = refs[-1 - n_out:-1]
        acc_ref = refs[-1]
        k = pl.program_id(2)
        at = a_ref[...]
        if a_fn is not None:
            at = a_fn(at.astype(F32))
        at = at.astype(BF16)
        bt = b_ref[...].astype(BF16)
        dims = (((0,) if trans_a else (1,), (1,) if trans_b else (0,)), ((), ()))
        prod = lax.dot_general(at, bt, dims, preferred_element_type=F32)

        def finish(acc):
            ex = [r[...] for r in extra_refs]
            outs = epilogue(acc, *ex) if epilogue is not None else (acc,)
            for o_ref, o in zip(out_refs, outs):
                o_ref[...] = o.astype(o_ref.dtype)

        if nk == 1:
            finish(prod)
            return

        @pl.when(k == 0)
        def _():
            acc_ref[...] = prod

        @pl.when((k > 0) & (k < nk - 1))
        def _():
            acc_ref[...] += prod

        @pl.when(k == nk - 1)
        def _():
            finish(acc_ref[...] + prod)

    a_spec = pl.BlockSpec((tk, tm), lambda i, j, k: (k, i)) if trans_a else pl.BlockSpec((tm, tk), lambda i, j, k: (i, k))
    if b_layer is None:
        b_spec = (pl.BlockSpec((tn, tk), lambda i, j, k: (j, k)) if trans_b
                  else pl.BlockSpec((tk, tn), lambda i, j, k: (k, j)))
    else:
        b_spec = (pl.BlockSpec((None, tn, tk), lambda i, j, k: (b_layer, j, k)) if trans_b
                  else pl.BlockSpec((None, tk, tn), lambda i, j, k: (b_layer, k, j)))
    e_specs = []
    for e in extras:
        if e.shape == (1, N):
            e_specs.append(pl.BlockSpec((1, tn), lambda i, j, k: (0, j)))
        else:
            assert e.shape == (M, N), (e.shape, M, N)
            e_specs.append(pl.BlockSpec((tm, tn), lambda i, j, k: (i, j)))
    args = [a, b, *extras]
    in_specs = [a_spec, b_spec] + e_specs
    if dep is not None:
        args.append(dep)
        in_specs.append(pl.BlockSpec(dep.shape, lambda i, j, k: (0, 0)))
    if into is None:
        out_specs = [pl.BlockSpec((tm, tn), lambda i, j, k: (i, j)) for _ in out_dtypes]
        out_shape = [jax.ShapeDtypeStruct((M, N), dt) for dt in out_dtypes]
    else:
        buf_shape, sharding = into
        assert n_out == 1
        index = grad_buffer_index(buf_shape, (M, N), sharding, tm, tn)
        out_specs = [pl.BlockSpec((None, None, tm, tn), lambda i, j, k: index(i, j))]
        out_shape = [jax.ShapeDtypeStruct(buf_shape, out_dtypes[0])]
    outs = pl.pallas_call(
        body, name=name,
        grid=(M // tm, N // tn, nk),
        in_specs=in_specs, out_specs=out_specs, out_shape=out_shape,
        scratch_shapes=[pltpu.VMEM((tm, tn), F32)],
        compiler_params=_cparams("parallel", "parallel", "arbitrary"),
    )(*args)
    return outs[0] if n_out == 1 else tuple(outs)


def grad_buffer_shape(mat_shape, sharding):
    M, N = mat_shape
    return (4, 2, M // 2, N // 4) if sharding == "cols" else (4, 2, M // 8, N)


def grad_buffer_index(buf_shape, mat_shape, sharding, tm, tn):
    _, _, rh, cs = buf_shape
    M, N = mat_shape
    ib, jb = rh // tm, cs // tn
    if sharding == "cols":
        assert (M, N) == (2 * rh, 4 * cs), (buf_shape, mat_shape)
        return lambda i, j: (j // jb, i // ib, i % ib, j % jb)
    assert sharding == "rows" and (M, N) == (8 * rh, cs), (buf_shape, mat_shape)
    return lambda i, j: (i // (2 * ib), (i // ib) % 2, i % ib, j)


def rmsnorm_fwd(x, g, *, name, dep=None, tr=256):
    S, D = x.shape
    tr = _tile(S, tr)

    def body(x_ref, g_ref, *rest):
        h_ref = rest[-1]
        xv = x_ref[...]
        rstd = lax.rsqrt(jnp.mean(xv * xv, axis=-1, keepdims=True) + NORM_EPS)
        h_ref[...] = (xv * rstd * g_ref[...]).astype(h_ref.dtype)

    args = [x, g.reshape(1, D)]
    in_specs = [pl.BlockSpec((tr, D), lambda i: (i, 0)), pl.BlockSpec((1, D), lambda i: (0, 0))]
    if dep is not None:
        args.append(dep)
        in_specs.append(pl.BlockSpec(dep.shape, lambda i: (0, 0)))
    return pl.pallas_call(
        body, name=name, grid=(S // tr,),
        in_specs=in_specs,
        out_specs=pl.BlockSpec((tr, D), lambda i: (i, 0)),
        out_shape=jax.ShapeDtypeStruct((S, D), BF16),
        compiler_params=_cparams("parallel"),
    )(*args)


def _rms_bwd_rows(xv, gv, dh):
    rstd = lax.rsqrt(jnp.mean(xv * xv, axis=-1, keepdims=True) + NORM_EPS)
    xhat = xv * rstd
    dxhat = dh * gv
    dx = rstd * (dxhat - xhat * jnp.mean(dxhat * xhat, axis=-1, keepdims=True))
    return dx, dh * xhat


def rmsnorm_bwd(x, g, dh, dres, *, name, tr=256):
    S, D = x.shape
    tr = _tile(S, tr)

    def body(x_ref, g_ref, dh_ref, dres_ref, dx_ref, dg_ref):
        @pl.when(pl.program_id(0) == 0)
        def _():
            dg_ref[...] = jnp.zeros_like(dg_ref)

        dx, dgr = _rms_bwd_rows(x_ref[...], g_ref[...], dh_ref[...].astype(F32))
        dx_ref[...] = dres_ref[...] + dx
        dg_ref[...] += jnp.sum(dgr, axis=0, keepdims=True)

    row = pl.BlockSpec((tr, D), lambda i: (i, 0))
    vec = pl.BlockSpec((1, D), lambda i: (0, 0))
    return pl.pallas_call(
        body, name=name, grid=(S // tr,),
        in_specs=[row, vec, row, row],
        out_specs=[row, vec],
        out_shape=[jax.ShapeDtypeStruct((S, D), F32), jax.ShapeDtypeStruct((1, D), F32)],
        compiler_params=_cparams("arbitrary"),
    )(x, g.reshape(1, D), dh, dres)


def final_norm_loss(x, g, target, *, name, tr=256):
    S, D = x.shape
    tr = _tile(S, tr)

    def body(x_ref, g_ref, t_ref, loss_ref, dx_ref, dg_ref):
        @pl.when(pl.program_id(0) == 0)
        def _():
            dg_ref[...] = jnp.zeros_like(dg_ref)
            loss_ref[...] = jnp.zeros_like(loss_ref)

        xv, gv = x_ref[...], g_ref[...]
        rstd = lax.rsqrt(jnp.mean(xv * xv, axis=-1, keepdims=True) + NORM_EPS)
        err = xv * rstd * gv - t_ref[...]
        part = 0.5 * jnp.sum(jnp.mean(err * err, axis=-1, keepdims=True), axis=0, keepdims=True)
        loss_ref[...] += jnp.broadcast_to(part, loss_ref.shape)
        dx, dgr = _rms_bwd_rows(xv, gv, err * (1.0 / D))
        dx_ref[...] = dx
        dg_ref[...] += jnp.sum(dgr, axis=0, keepdims=True)

    row = pl.BlockSpec((tr, D), lambda i: (i, 0))
    vec = pl.BlockSpec((1, D), lambda i: (0, 0))
    return pl.pallas_call(
        body, name=name, grid=(S // tr,),
        in_specs=[row, vec, row],
        out_specs=[pl.BlockSpec((8, LANES), lambda i: (0, 0)), row, vec],
        out_shape=[jax.ShapeDtypeStruct((8, LANES), F32), jax.ShapeDtypeStruct((S, D), F32),
                   jax.ShapeDtypeStruct((1, D), F32)],
        compiler_params=_cparams("arbitrary"),
    )(x, g.reshape(1, D), target)


def colsum(a, *, name, tr=256):
    S, N = a.shape
    tr = _tile(S, tr)

    def body(a_ref, o_ref):
        @pl.when(pl.program_id(0) == 0)
        def _():
            o_ref[...] = jnp.zeros_like(o_ref)

        o_ref[...] += jnp.sum(a_ref[...].astype(F32), axis=0, keepdims=True)

    return pl.pallas_call(
        body, name=name, grid=(S // tr,),
        in_specs=[pl.BlockSpec((tr, N), lambda i: (i, 0))],
        out_specs=pl.BlockSpec((1, N), lambda i: (0, 0)),
        out_shape=jax.ShapeDtypeStruct((1, N), F32),
        compiler_params=_cparams("arbitrary"),
    )(a)


def rope_tables(seq, head_dim):
    rot = head_dim // ROPE_FRACTION
    half = rot // 2
    inv_freq = 1.0 / (ROPE_THETA ** (jnp.arange(0, rot, 2, dtype=F32) / rot))
    ang = jnp.arange(seq, dtype=F32)[:, None] * inv_freq[None, :]
    cos, sin = jnp.cos(ang), jnp.sin(ang)
    ones = jnp.ones((seq, LANES - 2 * half), F32)
    zeros = jnp.zeros((seq, LANES - 2 * half), F32)
    zh = jnp.zeros((seq, half), F32)
    c = jnp.concatenate([cos, cos, ones], axis=1)
    sa = jnp.concatenate([-sin, zh, zeros], axis=1)
    sb = jnp.concatenate([zh, sin, zeros], axis=1)
    return c, sa, sb, half


def _rope(x, c, sa, sb, half):
    return x * c + pltpu.roll(x, LANES - half, 1) * sa + pltpu.roll(x, half, 1) * sb


def _rope_bwd(d, c, sa, sb, half):
    return d * c + pltpu.roll(d * sa, half, 1) + pltpu.roll(d * sb, LANES - half, 1)


def _band_masks(n, max_dist):
    qi = lax.broadcasted_iota(jnp.int32, (BAND, BAND), 0)
    kj = lax.broadcasted_iota(jnp.int32, (BAND, BAND), 1)
    cur = kj <= qi
    prev = ((kj >= qi) if max_dist == BAND else (kj > qi)) & (n > 0)
    return prev, cur


def _dot_nt(a, b):
    return lax.dot_general(a, b, (((1,), (1,)), ((), ())), preferred_element_type=F32)


def _dot_tn(a, b):
    return lax.dot_general(a, b, (((0,), (0,)), ((), ())), preferred_element_type=F32)


def _dot(a, b):
    return jnp.dot(a, b, preferred_element_type=F32)


def band_attention_fwd(src, tabs, *, L, dil, n_heads, group, q_blk, k_blk, v_blk, blk_per_row, max_dist, scale,
                       half, sink=None, normalize, name):
    c_t, sa_t, sb_t = tabs
    nb = L // BAND
    W = dil * n_heads * LANES
    use_sink = sink is not None

    def body(*refs):
        q_ref, k_ref, v_ref, c_ref, sa_ref, sb_ref = refs[:6]
        pos = 6
        if use_sink:
            sink_ref = refs[pos]
            pos += 1
        o_ref, m_ref, l_ref = refs[pos:pos + 3] if not normalize else (refs[pos], refs[pos + 1], None)
        kr_ref, vb_ref = refs[-2], refs[-1]

        def prep(n, carry):
            rows = pl.ds(pl.multiple_of(n * BAND, BAND), BAND)
            kr_ref[rows, :] = _rope(k_ref[rows, :], c_ref[rows, :], sa_ref[rows, :], sb_ref[rows, :], half).astype(BF16)
            vb_ref[rows, :] = v_ref[rows, :].astype(BF16)
            return carry

        lax.fori_loop(0, nb, prep, 0)

        def step(n, carry):
            rows = pl.ds(pl.multiple_of(n * BAND, BAND), BAND)
            prow = pl.ds(pl.multiple_of(jnp.maximum(n - 1, 0) * BAND, BAND), BAND)
            q = _rope(q_ref[rows, :], c_ref[rows, :], sa_ref[rows, :], sb_ref[rows, :], half).astype(BF16)
            mp, mc = _band_masks(n, max_dist)
            sp = jnp.where(mp, _dot_nt(q, kr_ref[prow, :]) * scale, -jnp.inf)
            sc = jnp.where(mc, _dot_nt(q, kr_ref[rows, :]) * scale, -jnp.inf)
            m = jnp.maximum(jnp.max(sp, axis=-1, keepdims=True), jnp.max(sc, axis=-1, keepdims=True))
            if use_sink:
                sk = sink_ref[:, 0:1]
                m = jnp.maximum(m, sk)
            pp = jnp.exp(sp - m)
            pc = jnp.exp(sc - m)
            l = jnp.sum(pp, axis=-1, keepdims=True) + jnp.sum(pc, axis=-1, keepdims=True)
            if use_sink:
                l = l + jnp.exp(sk - m)
            num = _dot(pp.astype(BF16), vb_ref[prow, :]) + _dot(pc.astype(BF16), vb_ref[rows, :])
            if normalize:
                o_ref[rows, :] = num / l
                m_ref[rows, :] = jnp.broadcast_to(m + jnp.log(l), (BAND, LANES))
            else:
                o_ref[rows, :] = num
                m_ref[rows, :] = jnp.broadcast_to(m, (BAND, LANES))
                l_ref[rows, :] = jnp.broadcast_to(l, (BAND, LANES))
            return carry

        lax.fori_loop(0, nb, step, 0)

    def col(off, div):
        return pl.BlockSpec((L, LANES), lambda r, h: (0, r * blk_per_row + off + h // div))

    tab = pl.BlockSpec((L, LANES), lambda r, h: (0, r))
    out = pl.BlockSpec((L, LANES), lambda r, h: (0, r * n_heads + h))
    in_specs = [col(q_blk, 1), col(k_blk, group), col(v_blk, group), tab, tab, tab]
    args = [src, src, src, c_t, sa_t, sb_t]
    if use_sink:
        in_specs.append(pl.BlockSpec((None, 1, LANES), lambda r, h: (h, 0, 0)))
        args.append(sink)
    n_out = 2 if normalize else 3
    return pl.pallas_call(
        body, name=name, grid=(dil, n_heads),
        in_specs=in_specs,
        out_specs=[out] * n_out,
        out_shape=[jax.ShapeDtypeStruct((L, W), F32)] * n_out,
        scratch_shapes=[pltpu.VMEM((L, LANES), BF16), pltpu.VMEM((L, LANES), BF16)],
        compiler_params=_cparams("parallel", "arbitrary"),
    )(*args)


def band_attention_bwd(src, tabs, o, lse, do, *, L, dil, n_heads, group, q_blk, k_blk, v_blk, blk_per_row, max_dist,
                       scale, half, do_bpr, sink=None, acc=None, name):
    c_t, sa_t, sb_t = tabs
    nb = L // BAND
    n_kv = n_heads // group
    use_sink = sink is not None
    use_acc = acc is not None
    assert not (use_acc and group != 1)

    def body(*refs):
        q_ref, k_ref, v_ref, c_ref, sa_ref, sb_ref, o_ref, lse_ref, do_ref = refs[:9]
        pos = 9
        if use_sink:
            sink_ref = refs[pos]
            pos += 1
        if use_acc:
            aq_ref, ak_ref, av_ref = refs[pos:pos + 3]
            pos += 3
        dq_ref, dk_ref, dv_ref = refs[pos:pos + 3]
        pos += 3
        if use_sink:
            dsink_ref = refs[pos]
        kr_ref, vb_ref, dka_ref, dva_ref = refs[-4:]
        h = pl.program_id(1)

        def prep(n, carry):
            rows = pl.ds(pl.multiple_of(n * BAND, BAND), BAND)
            kr_ref[rows, :] = _rope(k_ref[rows, :], c_ref[rows, :], sa_ref[rows, :], sb_ref[rows, :], half).astype(BF16)
            vb_ref[rows, :] = v_ref[rows, :].astype(BF16)
            return carry

        lax.fori_loop(0, nb, prep, 0)

        @pl.when(h % group == 0)
        def _():
            dka_ref[...] = jnp.zeros_like(dka_ref)
            dva_ref[...] = jnp.zeros_like(dva_ref)

        def step(n, dsk):
            rows = pl.ds(pl.multiple_of(n * BAND, BAND), BAND)
            prow = pl.ds(pl.multiple_of(jnp.maximum(n - 1, 0) * BAND, BAND), BAND)
            cv, sav, sbv = c_ref[rows, :], sa_ref[rows, :], sb_ref[rows, :]
            q = _rope(q_ref[rows, :], cv, sav, sbv, half).astype(BF16)
            dov = do_ref[rows, :]
            lsev = lse_ref[rows, 0:1]
            delta = jnp.sum(dov * o_ref[rows, :], axis=-1, keepdims=True)
            dob = dov.astype(BF16)
            mp, mc = _band_masks(n, max_dist)
            kp, kc, vp, vc = kr_ref[prow, :], kr_ref[rows, :], vb_ref[prow, :], vb_ref[rows, :]
            pp = jnp.exp(jnp.where(mp, _dot_nt(q, kp) * scale, -jnp.inf) - lsev)
            pc = jnp.exp(jnp.where(mc, _dot_nt(q, kc) * scale, -jnp.inf) - lsev)
            dsp = (pp * (_dot_nt(dob, vp) - delta) * scale).astype(BF16)
            dsc = (pc * (_dot_nt(dob, vc) - delta) * scale).astype(BF16)
            dq = _rope_bwd(_dot(dsp, kp) + _dot(dsc, kc), cv, sav, sbv, half)
            if use_acc:
                dq = dq + aq_ref[rows, :]
            dq_ref[rows, :] = dq
            dka_ref[prow, :] += _dot_tn(dsp, q)
            dka_ref[rows, :] += _dot_tn(dsc, q)
            dva_ref[prow, :] += _dot_tn(pp.astype(BF16), dob)
            dva_ref[rows, :] += _dot_tn(pc.astype(BF16), dob)
            if use_sink:
                dsk = dsk - jnp.sum(jnp.exp(sink_ref[:, 0:1] - lsev) * delta, axis=0, keepdims=True)
            return dsk

        dsk = lax.fori_loop(0, nb, step, jnp.zeros((1, 1), F32))
        if use_sink:
            dsink_ref[...] = jnp.broadcast_to(dsk, dsink_ref.shape)

        @pl.when(h % group == group - 1)
        def _():
            def fin(n, carry):
                rows = pl.ds(pl.multiple_of(n * BAND, BAND), BAND)
                dk = _rope_bwd(dka_ref[rows, :], c_ref[rows, :], sa_ref[rows, :], sb_ref[rows, :], half)
                dv = dva_ref[rows, :]
                if use_acc:
                    dk = dk + ak_ref[rows, :]
                    dv = dv + av_ref[rows, :]
                dk_ref[rows, :] = dk
                dv_ref[rows, :] = dv
                return carry

            lax.fori_loop(0, nb, fin, 0)

    def col(off, div):
        return pl.BlockSpec((L, LANES), lambda r, h: (0, r * blk_per_row + off + h // div))

    tab = pl.BlockSpec((L, LANES), lambda r, h: (0, r))
    qo = pl.BlockSpec((L, LANES), lambda r, h: (0, r * n_heads + h))
    kvo = pl.BlockSpec((L, LANES), lambda r, h: (0, r * n_kv + h // group))
    dospec = pl.BlockSpec((L, LANES), lambda r, h: (0, r * do_bpr + h))
    in_specs = [col(q_blk, 1), col(k_blk, group), col(v_blk, group), tab, tab, tab, qo, qo, dospec]
    args = [src, src, src, c_t, sa_t, sb_t, o, lse, do]
    if use_sink:
        in_specs.append(pl.BlockSpec((None, 1, LANES), lambda r, h: (h, 0, 0)))
        args.append(sink)
    if use_acc:
        in_specs += [qo, kvo, kvo]
        args += list(acc)
    out_specs = [qo, kvo, kvo]
    out_shape = [jax.ShapeDtypeStruct((L, dil * n_heads * LANES), F32),
                 jax.ShapeDtypeStruct((L, dil * n_kv * LANES), F32),
                 jax.ShapeDtypeStruct((L, dil * n_kv * LANES), F32)]
    if use_sink:
        out_specs.append(pl.BlockSpec((None, 1, LANES), lambda r, h: (h, 0, 0)))
        out_shape.append(jax.ShapeDtypeStruct((n_heads, 1, LANES), F32))
    return pl.pallas_call(
        body, name=name, grid=(dil, n_heads),
        in_specs=in_specs, out_specs=out_specs, out_shape=out_shape,
        scratch_shapes=[pltpu.VMEM((L, LANES), BF16), pltpu.VMEM((L, LANES), BF16),
                        pltpu.VMEM((L, LANES), F32), pltpu.VMEM((L, LANES), F32)],
        compiler_params=_cparams("parallel", "arbitrary"),
    )(*args)


def merge_branches(parts, *, name, tr=256):
    S, W = parts[0][0].shape
    tr = _tile(S, tr)
    nbr = len(parts)

    def body(*refs):
        ins, (o_ref, lse_ref) = refs[:3 * nbr], refs[3 * nbr:]
        nums = [ins[3 * i][...] for i in range(nbr)]
        ms = [ins[3 * i + 1][...] for i in range(nbr)]
        ls = [ins[3 * i + 2][...] for i in range(nbr)]
        mx = functools.reduce(jnp.maximum, ms)
        ws = [jnp.exp(m - mx) for m in ms]
        num = sum(w * n for w, n in zip(ws, nums))
        den = sum(w * l for w, l in zip(ws, ls))
        o_ref[...] = num / den
        lse_ref[...] = mx + jnp.log(den)

    row = pl.BlockSpec((tr, W), lambda i: (i, 0))
    flat = [a for p in parts for a in p]
    return pl.pallas_call(
        body, name=name, grid=(S // tr,),
        in_specs=[row] * len(flat), out_specs=[row, row],
        out_shape=[jax.ShapeDtypeStruct((S, W), F32)] * 2,
        compiler_params=_cparams("parallel"),
    )(*flat)


def _sigmoid(x):
    return 1.0 / (1.0 + jnp.exp(-x))


def _tri(n, lower):
    r = lax.broadcasted_iota(jnp.int32, (n, n), 0)
    c = lax.broadcasted_iota(jnp.int32, (n, n), 1)
    return ((c <= r) if lower else (c >= r)).astype(F32)


def _dot_exact(a, b, trans_a=False, trans_b=False):
    dims = (((0,) if trans_a else (1,), (1,) if trans_b else (0,)), ((), ()))
    return lax.dot_general(a, b, dims, preferred_element_type=F32, precision=lax.Precision.HIGHEST)


def _hgrn_gates(qb, fb, lb):
    sq = _sigmoid(qb)
    q = qb * sq * (B_DIM ** -0.5)
    sf = _sigmoid(fb)
    gate = lb + (1.0 - lb) * sf
    return q, 1.0 - gate, gate, sq, sf


def _hgrn_intra_fwd(q_ref, k_ref, b_ref):
    C, n_sub = B_CHUNK, B_CHUNK // B_SUB
    b_all, k_all = b_ref[...], k_ref[...]
    srow = lax.broadcasted_iota(jnp.int32, (C, LANES), 0)
    lane = lax.broadcasted_iota(jnp.int32, (B_SUB, C), 1)
    trow = lax.broadcasted_iota(jnp.int32, (B_SUB, LANES), 0)
    blocks = []
    for i in range(n_sub):
        r0 = i * B_SUB
        qi, bi = q_ref[pl.ds(r0, B_SUB), :], b_ref[pl.ds(r0, B_SUB), :]
        if i == 0:
            a_i = jnp.zeros((B_SUB, C), F32)
        else:
            ref_b = b_ref[pl.ds(r0 - 1, 1), :]
            qt = (qi * jnp.exp(bi - ref_b)).astype(BF16)
            kt = jnp.where(srow < r0, k_all * jnp.exp(jnp.minimum(ref_b - b_all, 0.0)), 0.0).astype(BF16)
            a_i = _dot_nt(qt, kt)
        for sl in range(B_SUB):
            s = r0 + sl
            e = jnp.where(trow >= sl, jnp.exp(jnp.minimum(bi - b_ref[pl.ds(s, 1), :], 0.0)), 0.0)
            colv = jnp.sum(qi * k_ref[pl.ds(s, 1), :] * e, axis=-1, keepdims=True)
            a_i = a_i + jnp.where(lane == s, colv, 0.0)
        blocks.append(a_i)
    return jnp.concatenate(blocks, axis=0)


def hgrn_fwd(src, lb, norm_g, *, S, n_heads, q_blk, f_blk, i_blk, g_blk, name):
    nc = S // B_CHUNK

    def body(qb_ref, fb_ref, ib_ref, gb_ref, lb_ref, ng_ref, out_ref, o_ref, st_ref, state_ref, q_s, k_s, b_s):
        state_ref[...] = jnp.zeros_like(state_ref)
        tril = _tri(B_CHUNK, True)

        def step(c, carry):
            rows = pl.ds(pl.multiple_of(c * B_CHUNK, B_CHUNK), B_CHUNK)
            q, k, gate, _, _ = _hgrn_gates(qb_ref[rows, :], fb_ref[rows, :], lb_ref[...])
            b = _dot_exact(tril, jnp.log(gate))
            vf = ib_ref[rows, :]
            v = vf.astype(BF16)
            q_s[...], k_s[...], b_s[...] = q, k, b
            st = state_ref[...]
            st_ref[c] = st
            a = _hgrn_intra_fwd(q_s, k_s, b_s)
            o = _dot(a.astype(BF16), v) + _dot_nt((q * jnp.exp(b)).astype(BF16), st.astype(BF16))
            b_last = b_s[pl.ds(B_CHUNK - 1, 1), :]
            state_ref[...] = st * jnp.exp(b_last) + _dot_exact(vf, k * jnp.exp(b_last - b), trans_a=True)
            o_ref[rows, :] = o
            rstd = lax.rsqrt(jnp.mean(o * o, axis=-1, keepdims=True) + NORM_EPS)
            gb = gb_ref[rows, :]
            out_ref[rows, :] = o * rstd * ng_ref[...] * (gb * _sigmoid(gb))
            return carry

        lax.fori_loop(0, nc, step, 0)

    def col(off):
        return pl.BlockSpec((S, LANES), lambda h: (0, off + h))

    hv = pl.BlockSpec((None, 1, LANES), lambda h: (h, 0, 0))
    return pl.pallas_call(
        body, name=name, grid=(n_heads,),
        in_specs=[col(q_blk), col(f_blk), col(i_blk), col(g_blk), hv, pl.BlockSpec((1, LANES), lambda h: (0, 0))],
        out_specs=[col(0), col(0), pl.BlockSpec((None, nc, LANES, LANES), lambda h: (h, 0, 0, 0))],
        out_shape=[jax.ShapeDtypeStruct((S, n_heads * LANES), F32), jax.ShapeDtypeStruct((S, n_heads * LANES), F32),
                   jax.ShapeDtypeStruct((n_heads, nc, LANES, LANES), F32)],
        scratch_shapes=[pltpu.VMEM((LANES, LANES), F32)] + [pltpu.VMEM((B_CHUNK, LANES), F32)] * 3,
        compiler_params=_cparams("parallel"),
    )(src, src, src, src, lb, norm_g.reshape(1, LANES))


def hgrn_bwd(src, lb, norm_g, o, states, dout, *, S, n_heads, q_blk, f_blk, i_blk, g_blk, dout_blk, name):
    nc = S // B_CHUNK
    C, n_sub = B_CHUNK, B_CHUNK // B_SUB

    def body(qb_ref, fb_ref, ib_ref, gb_ref, lb_ref, ng_ref, o_ref, st_ref, dout_ref,
             dqb_ref, dfb_ref, dib_ref, dgb_ref, dlb_ref, dng_ref, dstate_ref, q_s, k_s, b_s, dq_s, dk_s):
        @pl.when(pl.program_id(0) == 0)
        def _():
            dng_ref[...] = jnp.zeros_like(dng_ref)

        dstate_ref[...] = jnp.zeros_like(dstate_ref)
        tril = _tri(C, True)
        triu = _tri(C, False)
        lbv, ngv = lb_ref[...], ng_ref[...]
        srow = lax.broadcasted_iota(jnp.int32, (C, LANES), 0)
        lane = lax.broadcasted_iota(jnp.int32, (B_SUB, C), 1)
        trow = lax.broadcasted_iota(jnp.int32, (B_SUB, LANES), 0)
        causal = lax.broadcasted_iota(jnp.int32, (C, C), 1) <= lax.broadcasted_iota(jnp.int32, (C, C), 0)

        def step(ci, carry):
            dlog_carry, dlb_acc, dng_acc = carry
            c = nc - 1 - ci
            rows = pl.ds(pl.multiple_of(c * C, C), C)
            qb, fb, gb = qb_ref[rows, :], fb_ref[rows, :], gb_ref[rows, :]
            q, k, gate, sq, sf = _hgrn_gates(qb, fb, lbv)
            b = _dot_exact(tril, jnp.log(gate))
            vf = ib_ref[rows, :]
            v = vf.astype(BF16)
            q_s[...], k_s[...], b_s[...] = q, k, b
            st = st_ref[c]
            dst = dstate_ref[...]

            ov = o_ref[rows, :]
            dout = dout_ref[rows, :]
            rstd = lax.rsqrt(jnp.mean(ov * ov, axis=-1, keepdims=True) + NORM_EPS)
            xhat = ov * rstd
            sg = _sigmoid(gb)
            dy = dout * (gb * sg)
            dgb_ref[rows, :] = dout * (xhat * ngv) * (sg * (1.0 + gb * (1.0 - sg)))
            dng_acc = dng_acc + jnp.sum(dy * xhat, axis=0, keepdims=True)
            dxhat = dy * ngv
            do = rstd * (dxhat - xhat * jnp.mean(dxhat * xhat, axis=-1, keepdims=True))
            dob = do.astype(BF16)

            eb = jnp.exp(b)
            b_last = b_s[pl.ds(C - 1, 1), :]
            ebl = jnp.exp(b_last - b)
            a = _hgrn_intra_fwd(q_s, k_s, b_s)
            da = jnp.where(causal, _dot_exact(do, vf, trans_b=True), 0.0)
            dv = _dot_tn(a.astype(BF16), dob) + _dot_nt((k * ebl).astype(BF16), dst.astype(BF16))
            dq_s[...] = _dot_exact(do, st) * eb
            dk_s[...] = _dot_exact(vf, dst) * ebl
            dstate_ref[...] = dst * jnp.exp(b_last) + _dot_exact(do, q * eb, trans_a=True)
            b_all, k_all = b, k
            for i in range(n_sub):
                r0 = i * B_SUB
                blk = pl.ds(r0, B_SUB)
                qi, bi = q_s[blk, :], b_s[blk, :]
                da_i = da[r0:r0 + B_SUB, :]
                dq_i = jnp.zeros((B_SUB, LANES), F32)
                if i > 0:
                    ref_b = b_s[pl.ds(r0 - 1, 1), :]
                    eq = jnp.exp(bi - ref_b)
                    ek = jnp.where(srow < r0, jnp.exp(jnp.minimum(ref_b - b_all, 0.0)), 0.0)
                    da_off = jnp.where(lane < r0, da_i, 0.0)
                    dq_i = _dot_exact(da_off, k_all * ek) * eq
                    dk_s[...] += _dot_exact(da_off, qi * eq, trans_a=True) * ek
                for sl in range(B_SUB):
                    s = r0 + sl
                    e = jnp.where(trow >= sl, jnp.exp(jnp.minimum(bi - b_s[pl.ds(s, 1), :], 0.0)), 0.0)
                    dac = jnp.sum(jnp.where(lane == s, da_i, 0.0), axis=-1, keepdims=True)
                    dq_i = dq_i + dac * k_s[pl.ds(s, 1), :] * e
                    dk_s[pl.ds(s, 1), :] += jnp.sum(dac * qi * e, axis=0, keepdims=True)
                dq_s[blk, :] += dq_i
            dq, dk = dq_s[...], dk_s[...]
            db = q * dq - k * dk
            dlog = _dot_exact(triu, db) + dlog_carry
            dlog_carry = dlog_carry + jnp.sum(db, axis=0, keepdims=True)
            dgate = dlog / gate - dk
            dqb_ref[rows, :] = dq * (B_DIM ** -0.5) * (sq * (1.0 + qb * (1.0 - sq)))
            dfb_ref[rows, :] = dgate * (1.0 - lbv) * sf * (1.0 - sf)
            dib_ref[rows, :] = dv
            dlb_acc = dlb_acc + jnp.sum(dgate * (1.0 - sf), axis=0, keepdims=True)
            return dlog_carry, dlb_acc, dng_acc

        z = jnp.zeros((1, LANES), F32)
        _, dlb_acc, dng_acc = lax.fori_loop(0, nc, step, (z, z, z))
        dlb_ref[...] = dlb_acc
        dng_ref[...] += dng_acc

    def col(off):
        return pl.BlockSpec((S, LANES), lambda h: (0, off + h))

    hv = pl.BlockSpec((None, 1, LANES), lambda h: (h, 0, 0))
    vec = pl.BlockSpec((1, LANES), lambda h: (0, 0))
    full = jax.ShapeDtypeStruct((S, n_heads * LANES), F32)
    return pl.pallas_call(
        body, name=name, grid=(n_heads,),
        in_specs=[col(q_blk), col(f_blk), col(i_blk), col(g_blk), hv, vec, col(0),
                  pl.BlockSpec((None, nc, LANES, LANES), lambda h: (h, 0, 0, 0)), col(dout_blk)],
        out_specs=[col(0), col(0), col(0), col(0), hv, vec],
        out_shape=[full, full, full, full, jax.ShapeDtypeStruct((n_heads, 1, LANES), F32),
                   jax.ShapeDtypeStruct((1, LANES), F32)],
        scratch_shapes=[pltpu.VMEM((LANES, LANES), F32)] + [pltpu.VMEM((B_CHUNK, LANES), F32)] * 5,
        compiler_params=_cparams("arbitrary"),
    )(src, src, src, src, lb, norm_g.reshape(1, LANES), o, states, dout)


def _pad_heads_cols(w, n_heads):
    lead = w.shape[:-1]
    w = w.reshape(lead + (n_heads, C_DIM))
    w = jnp.pad(w, [(0, 0)] * len(lead) + [(0, 0), (0, LANES - C_DIM)])
    return w.reshape(lead + (n_heads * LANES,))


def _unpad_heads_cols(w, n_heads):
    lead = w.shape[:-1]
    return w.reshape(lead + (n_heads, LANES))[..., :C_DIM].reshape(lead + (n_heads * C_DIM,))


def _pad_heads_rows(w, n_heads):
    w = jnp.pad(w.reshape(n_heads, C_DIM, w.shape[1]), [(0, 0), (0, LANES - C_DIM), (0, 0)])
    return w.reshape(n_heads * LANES, w.shape[2])


def _unpad_heads_rows(w, n_heads):
    return w.reshape(n_heads, LANES, w.shape[1])[:, :C_DIM].reshape(n_heads * C_DIM, w.shape[1])


def _lower_bounds(lb_raw):
    lb_soft = jax.nn.softmax(lb_raw.astype(F32), axis=0)
    return jnp.cumsum(lb_soft, axis=0) - lb_soft[0:1]


def _relu2(u):
    r = jnp.maximum(u, 0.0)
    return r * r


def local_step(x, target, P, hooks):
    S, D = x.shape
    depth = P["norm_mix_g"].shape[0]
    HA = D // 2 // A_DIM
    HB = HA
    HQ = D // C_DIM
    HKV = HQ // C_GROUP
    even_in = 7 * HA * LANES
    tabs_a = rope_tables(S, A_DIM)
    tabs_c = rope_tables(S, C_DIM)
    half_a, half_c = tabs_a[3], tabs_c[3]
    lbounds = _lower_bounds(P["hgrn_lb_raw"])
    a_kw = dict(n_heads=HA, group=1, q_blk=0, k_blk=HA, v_blk=2 * HA, blk_per_row=7 * HA, scale=A_DIM ** -0.5,
                half=half_a)
    b_kw = dict(S=S, n_heads=HB, q_blk=3 * HA, f_blk=3 * HA + HB, i_blk=3 * HA + 2 * HB, g_blk=3 * HA + 3 * HB)
    c_kw = dict(L=S, dil=1, n_heads=HQ, group=C_GROUP, q_blk=0, k_blk=HQ, v_blk=HQ + HKV, blk_per_row=HQ + 2 * HKV,
                max_dist=C_WINDOW - 1, scale=C_DIM ** -0.5, half=half_c)

    def a_tabs(dil):
        return tuple(t.reshape(S // dil, dil * LANES) for t in tabs_a[:3])

    saved = []
    for layer in range(depth):
        Wl = dict(hooks["weights_in"](layer, x))
        sv = {"x0": x, "W": Wl}
        idx = layer // 2
        h = rmsnorm_fwd(x, P["norm_mix_g"][layer], name="norm_mix_fwd")
        sv["h"] = h
        if layer % 2 == 0:
            proj = matmul(h, Wl["even_w_in"], out_dtypes=(F32,), name="even_in_fwd")
            parts = []
            for window, dil in A_BRANCHES:
                L = S // dil
                res = band_attention_fwd(proj.reshape(L, dil * even_in), a_tabs(dil), L=L, dil=dil,
                                         max_dist=window // dil, normalize=False, name=f"dilated_fwd_d{dil}", **a_kw)
                parts.append(tuple(r.reshape(S, HA * LANES) for r in res))
            oa, lse = merge_branches(parts, name="dilated_merge")
            hooks["mid"](layer, oa)
            lb_e = lbounds[idx].reshape(HB, 1, LANES)
            ob, o_raw, states = hgrn_fwd(proj, lb_e, P["hgrn_norm_g"][idx], name="hgrn_fwd", **b_kw)
            Wl.update(hooks["weights_rest"](layer, ob))
            mixed = jnp.concatenate([oa, ob], axis=1).astype(BF16)
            sv.update(proj=proj, oa=oa, lse=lse, o_raw=o_raw, states=states, mixed=mixed, lb=lb_e)
            x = matmul(mixed, Wl["even_w_out"], extras=(x,), epilogue=lambda acc, r: (acc + r,),
                       out_dtypes=(F32,), name="even_out_fwd")
        else:
            proj = matmul(h, Wl["odd_w_qkv"], extras=(P["odd_b_qkv"][idx].reshape(1, -1),),
                          epilogue=lambda acc, b: (acc + b,), out_dtypes=(F32,), name="odd_qkv_fwd")
            sink = jnp.broadcast_to(P["odd_sinks"][idx].reshape(HQ, 1, 1), (HQ, 1, LANES))
            o, lse = band_attention_fwd(proj, tabs_c[:3], sink=sink, normalize=True, name="swa_fwd", **c_kw)
            hooks["mid"](layer, o)
            Wl.update(hooks["weights_rest"](layer, o))
            sv.update(proj=proj, o=o, lse=lse, sink=sink)
            x = matmul(o, Wl["odd_w_o"], extras=(P["odd_b_o"][idx].reshape(1, D), x),
                       epilogue=lambda acc, b, r: (acc + b + r,), out_dtypes=(F32,), name="odd_out_fwd")
        sv["x1"] = x
        h2 = rmsnorm_fwd(x, P["norm_mlp_g"][layer], dep=hooks["after_mixer"](layer, x), name="norm_mlp_fwd")
        u = matmul(h2, Wl["mlp_w1"], out_dtypes=(BF16,), name="mlp_up_fwd")
        x = matmul(u, Wl["mlp_w2"], a_fn=_relu2, extras=(x,), epilogue=lambda acc, r: (acc + r,),
                   out_dtypes=(F32,), name="mlp_down_fwd")
        sv.update(h2=h2, u=u)
        saved.append(sv)

    loss, dx, dgf = final_norm_loss(x, P["final_norm_g"], target, name="final_norm_loss")

    G = {k: [None] * depth for k in ("norm_mix_g", "norm_mlp_g")}
    for k in ("hgrn_lb", "hgrn_norm_g"):
        G[k] = [None] * ((depth + 1) // 2)
    for k in ("odd_w_qkv", "odd_b_qkv", "odd_sinks", "odd_w_o", "odd_b_o"):
        G[k] = [None] * (depth // 2)
    G["final_norm_g"] = dgf.reshape(D)
    B = {}

    def wgrad(a, d, key, idx, name, a_fn=None):
        M, N = a.shape[1], d.shape[1]
        into = (grad_buffer_shape((M, N), SHARDING[key]), SHARDING[key])
        B[key, idx] = matmul(a, d, trans_a=True, a_fn=a_fn, out_dtypes=(BF16,), into=into, name=name)

    dep = None
    for layer in reversed(range(depth)):
        sv = saved[layer]
        Wl = sv["W"]
        idx = layer // 2
        du = matmul(dx, Wl["mlp_w2"], trans_b=True, extras=(sv["u"],), dep=dep,
                    epilogue=lambda acc, u: (acc * (2.0 * jnp.maximum(u.astype(F32), 0.0)),),
                    out_dtypes=(BF16,), name="mlp_down_bwd_x")
        wgrad(sv["u"], dx, "mlp_w2", layer, "mlp_down_bwd_w", a_fn=_relu2)
        wgrad(sv["h2"], du, "mlp_w1", layer, "mlp_up_bwd_w")
        dh2 = matmul(du, Wl["mlp_w1"], trans_b=True, out_dtypes=(F32,), name="mlp_up_bwd_x")
        dx, dg = rmsnorm_bwd(sv["x1"], P["norm_mlp_g"][layer], dh2, dx, name="norm_mlp_bwd")
        G["norm_mlp_g"][layer] = dg.reshape(D)
        dep = hooks["grads_ready"]([("mlp_w2", layer), ("mlp_w1", layer)], B, dx)
        if layer % 2 == 0:
            dmixed = matmul(dx, Wl["even_w_out"], trans_b=True, dep=dep, out_dtypes=(F32,), name="even_out_bwd_x")
            wgrad(sv["mixed"], dx, "even_w_out", idx, "even_out_bwd_w")
            acc = None
            for window, dil in A_BRANCHES:
                L = S // dil
                rs = lambda a: a.reshape(L, -1)
                acc = band_attention_bwd(rs(sv["proj"]), a_tabs(dil), rs(sv["oa"]), rs(sv["lse"]), rs(dmixed),
                                         L=L, dil=dil, max_dist=window // dil, do_bpr=2 * HA,
                                         acc=None if acc is None else tuple(rs(a) for a in acc),
                                         name=f"dilated_bwd_d{dil}", **a_kw)
                acc = tuple(a.reshape(S, HA * LANES) for a in acc)
            dqb, dfb, dib, dgb, dlb, dng = hgrn_bwd(sv["proj"], sv["lb"], P["hgrn_norm_g"][idx], sv["o_raw"],
                                                    sv["states"], dmixed, dout_blk=HA, name="hgrn_bwd", **b_kw)
            G["hgrn_lb"][idx] = dlb.reshape(HB * LANES)
            G["hgrn_norm_g"][idx] = dng.reshape(LANES)
            dproj = jnp.concatenate(list(acc) + [dqb, dfb, dib, dgb], axis=1).astype(BF16)
            wgrad(sv["h"], dproj, "even_w_in", idx, "even_in_bwd_w")
            dh = matmul(dproj, Wl["even_w_in"], trans_b=True, out_dtypes=(F32,), name="even_in_bwd_x")
        else:
            do = matmul(dx, Wl["odd_w_o"], trans_b=True, dep=dep, out_dtypes=(F32,), name="odd_out_bwd_x")
            G["odd_b_o"][idx] = colsum(dx, name="odd_out_bwd_b").reshape(D)
            G["odd_w_o"][idx] = matmul(sv["o"], dx, trans_a=True, out_dtypes=(BF16,), name="odd_out_bwd_w")
            dq, dk, dv, dsink = band_attention_bwd(sv["proj"], tabs_c[:3], sv["o"], sv["lse"], do, sink=sv["sink"],
                                                   do_bpr=HQ, name="swa_bwd", **c_kw)
            G["odd_sinks"][idx] = dsink[:, 0, 0]
            dproj = jnp.concatenate([dq, dk, dv], axis=1)
            G["odd_b_qkv"][idx] = colsum(dproj, name="odd_qkv_bwd_b").reshape(-1)
            dproj = dproj.astype(BF16)
            G["odd_w_qkv"][idx] = matmul(sv["h"], dproj, trans_a=True, out_dtypes=(BF16,), name="odd_qkv_bwd_w")
            dh = matmul(dproj, Wl["odd_w_qkv"], trans_b=True, out_dtypes=(F32,), name="odd_qkv_bwd_x")
            HQ2 = HQ + 2 * HKV
            B["odd_w_qkv", idx] = full_to_buffer(_unpad_heads_cols(G["odd_w_qkv"][idx], HQ2), "cols")
            B["odd_w_o", idx] = full_to_buffer(_unpad_heads_rows(G["odd_w_o"][idx], HQ), "rows")
        dx, dg = rmsnorm_bwd(sv["x0"], P["norm_mix_g"][layer], dh, dx, name="norm_mix_bwd")
        G["norm_mix_g"][layer] = dg.reshape(D)
        mixer = ("even_w_out", "even_w_in") if layer % 2 == 0 else ("odd_w_o", "odd_w_qkv")
        dep = hooks["grads_ready"]([(k, idx) for k in mixer], B, dx)

    del G["odd_w_qkv"], G["odd_w_o"]
    grads = {k: jnp.stack(v) if isinstance(v, list) else v for k, v in G.items()}
    _, lb_vjp = jax.vjp(_lower_bounds, P["hgrn_lb_raw"])
    grads["hgrn_lb_raw"] = lb_vjp(grads.pop("hgrn_lb"))[0]
    grads["odd_b_qkv"] = _unpad_heads_cols(grads["odd_b_qkv"], HQ + 2 * HKV)
    return loss, dx, grads, B


def full_to_buffer(full, sharding):
    M, N = full.shape
    if sharding == "cols":
        return jnp.transpose(full.reshape(2, M // 2, 4, N // 4), (2, 0, 1, 3))
    return full.reshape(4, 2, M // 8, N)


MESH = pl.DeviceIdType.MESH
ANY = pl.BlockSpec(memory_space=pl.ANY)
SHARDING = {"even_w_in": "cols", "even_w_out": "rows", "odd_w_qkv": "cols", "odd_w_o": "rows",
            "mlp_w1": "cols", "mlp_w2": "rows"}
BIG = tuple(SHARDING)


def _mesh_pos():
    return lax.axis_index("x"), lax.axis_index("y"), lax.axis_index("c")


def _other_chips(x, y):
    return [(1 - x, y), (x, 1 - y), (1 - x, 1 - y)]


def cast_into_full(w, layer, sharding, s, *, name, tr=256):
    n, R, C = w.shape
    tr = _tile(R, tr)
    nr = R // tr

    def body(s_ref, w_ref, o_ref):
        o_ref[...] = w_ref[...].astype(BF16)

    if sharding == "cols":
        full, out_map = (R, 4 * C), (lambda i, s_ref: (i, s_ref[0]))
    else:
        full, out_map = (4 * R, C), (lambda i, s_ref: (s_ref[0] * nr + i, 0))
    grid_spec = pltpu.PrefetchScalarGridSpec(
        num_scalar_prefetch=1, grid=(nr,),
        in_specs=[pl.BlockSpec((None, tr, C), lambda i, s_ref: (layer, i, 0))],
        out_specs=pl.BlockSpec((tr, C), out_map))
    return pl.pallas_call(
        body, name=name, grid_spec=grid_spec, out_shape=jax.ShapeDtypeStruct(full, BF16),
        compiler_params=_cparams("parallel"),
    )(s.reshape(1).astype(jnp.int32), w)


HBM = pl.BlockSpec(memory_space=pltpu.HBM)
SEM = pl.BlockSpec(memory_space=pltpu.SEMAPHORE)
EFFECT = pltpu.SideEffectType.DATAFLOW_SIDE_EFFECTING


def _weight_window(ref, sharding, s, hf):
    M, N = ref.shape
    if sharding == "cols":
        rh, C = M // 2, N // 4
        return ref.at[pl.ds(hf * rh, rh), pl.ds(s * C, C)]
    rh = M // 8
    return ref.at[pl.ds((2 * s + hf) * rh, rh), :]


def _in_hbm(a):
    return pltpu.with_memory_space_constraint(a, pltpu.HBM)


def gather_start(groups, shardings, after, *, name):
    sizes = [len(g) for g in groups]
    flat = [m for g in groups for m in g]
    flat_sh = [sh for g in shardings for sh in g]
    T, ng = len(flat), len(groups)

    def body(*refs):
        sems = refs[T + 1:T + 1 + 6 * ng]
        thru = refs[T + 1 + 6 * ng:2 * T + 1 + 6 * ng]
        token = refs[2 * T + 1 + 6 * ng]
        x, y, c = _mesh_pos()
        pos = 0
        for g in range(ng):
            for t in range(sizes[g]):
                w = _weight_window(thru[pos], flat_sh[pos], 2 * x + y, c)
                for j, (cx, cy) in enumerate(_other_chips(x, y)):
                    pltpu.make_async_remote_copy(src_ref=w, dst_ref=w, send_sem=sems[6 * g + j],
                                                 recv_sem=sems[6 * g + 3 + j], device_id=(cx, cy, c),
                                                 device_id_type=MESH).start()
                pos += 1
        token[...] = jnp.zeros_like(token)

    outs = pl.pallas_call(
        body, name=name,
        in_specs=[HBM] * T + [ANY],
        out_specs=[SEM] * (6 * ng) + [HBM] * T + [pl.BlockSpec(memory_space=pltpu.VMEM)],
        out_shape=[pltpu.SemaphoreType.DMA(())] * (6 * ng) + [pltpu.HBM(m.shape, m.dtype) for m in flat]
        + [jax.ShapeDtypeStruct((8, LANES), F32)],
        input_output_aliases={t: 6 * ng + t for t in range(T)},
        compiler_params=pltpu.CompilerParams(has_side_effects=EFFECT),
    )(*[_in_hbm(m) for m in flat], after)
    res, pos = [], 6 * ng
    for g in range(ng):
        res.append((list(outs[6 * g:6 * g + 6]), list(outs[pos:pos + sizes[g]])))
        pos += sizes[g]
    return res, outs[-1]


def gather_forward(mats, shardings, sems, after, *, name):
    T = len(mats)

    def body(*refs):
        sems1 = refs[T:T + 6]
        sems2 = refs[T + 7:T + 13]
        thru = refs[T + 13:2 * T + 13]
        token = refs[2 * T + 13]
        x, y, c = _mesh_pos()
        chips = _other_chips(x, y)
        for t in range(T):
            own = _weight_window(thru[t], shardings[t], 2 * x + y, c)
            for j, (cx, cy) in enumerate(chips):
                landed = _weight_window(thru[t], shardings[t], 2 * cx + cy, c)
                first = pltpu.make_async_remote_copy(src_ref=own, dst_ref=landed, send_sem=sems1[j],
                                                     recv_sem=sems1[3 + j], device_id=(cx, cy, c),
                                                     device_id_type=MESH)
                first.wait_recv()
                first.wait_send()
        for t in range(T):
            for j, (cx, cy) in enumerate(chips):
                landed = _weight_window(thru[t], shardings[t], 2 * cx + cy, c)
                pltpu.make_async_remote_copy(src_ref=landed, dst_ref=landed, send_sem=sems2[j],
                                             recv_sem=sems2[3 + j], device_id=(x, y, 1 - c),
                                             device_id_type=MESH).start()
        token[...] = jnp.zeros_like(token)

    outs = pl.pallas_call(
        body, name=name,
        in_specs=[HBM] * T + [SEM] * 6 + [ANY],
        out_specs=[SEM] * 6 + [HBM] * T + [pl.BlockSpec(memory_space=pltpu.VMEM)],
        out_shape=[pltpu.SemaphoreType.DMA(())] * 6 + [pltpu.HBM(m.shape, m.dtype) for m in mats]
        + [jax.ShapeDtypeStruct((8, LANES), F32)],
        input_output_aliases={t: 6 + t for t in range(T)},
        compiler_params=pltpu.CompilerParams(has_side_effects=EFFECT),
    )(*mats, *sems, after)
    return list(outs[:6]), list(outs[6:6 + T]), outs[-1]


def gather_finish(mats, shardings, sems, after, *, name):
    T = len(mats)

    def body(*refs):
        sems2 = refs[T:T + 6]
        thru = refs[T + 7:2 * T + 7]
        x, y, c = _mesh_pos()
        for t in range(T):
            for j, (cx, cy) in enumerate(_other_chips(x, y)):
                sent = _weight_window(thru[t], shardings[t], 2 * cx + cy, c)
                other = _weight_window(thru[t], shardings[t], 2 * cx + cy, 1 - c)
                cp = pltpu.make_async_remote_copy(src_ref=sent, dst_ref=other, send_sem=sems2[j],
                                                  recv_sem=sems2[3 + j], device_id=(x, y, 1 - c),
                                                  device_id_type=MESH)
                cp.wait_recv()
                cp.wait_send()

    outs = pl.pallas_call(
        body, name=name,
        in_specs=[HBM] * T + [SEM] * 6 + [ANY],
        out_specs=[HBM] * T,
        out_shape=[pltpu.HBM(m.shape, m.dtype) for m in mats],
        input_output_aliases={t: t for t in range(T)},
        compiler_params=pltpu.CompilerParams(has_side_effects=EFFECT),
    )(*mats, *sems, after)
    return list(outs)


def exchange_halves(bufs, *, name):
    T = len(bufs)

    def body(*refs):
        in_refs, out_refs = refs[:T], refs[T:2 * T]
        send_sems, recv_sems = refs[2 * T:]
        x, y, c = _mesh_pos()
        cps = [pltpu.make_async_remote_copy(src_ref=in_refs[t].at[:, pl.ds(1 - c, 1)], dst_ref=out_refs[t],
                                            send_sem=send_sems.at[t], recv_sem=recv_sems.at[t],
                                            device_id=(x, y, 1 - c), device_id_type=MESH) for t in range(T)]
        for cp in cps:
            cp.start()
        for cp in cps:
            cp.wait()

    return pl.pallas_call(
        body, name=name, in_specs=[ANY] * T, out_specs=[ANY] * T,
        out_shape=[jax.ShapeDtypeStruct((4, 1) + b.shape[2:], b.dtype) for b in bufs],
        scratch_shapes=[pltpu.SemaphoreType.DMA((T,)), pltpu.SemaphoreType.DMA((T,))],
    )(*bufs)


def pair_sum(buf, got, c, *, name, tr=512):
    _, _, n, rh, C = buf.shape
    tr = _tile(rh, tr)

    def body(c_ref, a_ref, b_ref, o_ref):
        o_ref[...] = (a_ref[...].astype(F32) + b_ref[...].astype(F32)).astype(o_ref.dtype)

    grid_spec = pltpu.PrefetchScalarGridSpec(
        num_scalar_prefetch=1, grid=(4, n, rh // tr),
        in_specs=[pl.BlockSpec((None, None, None, tr, C), lambda s, l, i, c_ref: (s, c_ref[0], l, i, 0)),
                  pl.BlockSpec((None, None, None, tr, C), lambda s, l, i, c_ref: (s, 0, l, i, 0))],
        out_specs=pl.BlockSpec((None, None, tr, C), lambda s, l, i, c_ref: (s, l, i, 0)))
    return pl.pallas_call(
        body, name=name, grid_spec=grid_spec, out_shape=jax.ShapeDtypeStruct((4, n, rh, C), BF16),
        compiler_params=_cparams("parallel", "parallel", "parallel"),
    )(c.reshape(1).astype(jnp.int32), buf, got)


def scatter_to_owners(sums, *, name):
    T = len(sums)

    def body(*refs):
        in_refs, out_refs = refs[:T], refs[T:2 * T]
        send_sems, recv_sems = refs[2 * T:]
        x, y, c = _mesh_pos()
        chips = _other_chips(x, y)
        cps = [pltpu.make_async_remote_copy(src_ref=in_refs[t].at[pl.ds(2 * cx + cy, 1)],
                                            dst_ref=out_refs[t].at[pl.ds(j, 1)],
                                            send_sem=send_sems.at[t, j], recv_sem=recv_sems.at[t, j],
                                            device_id=(cx, cy, c), device_id_type=MESH)
               for t in range(T) for j, (cx, cy) in enumerate(chips)]
        for cp in cps:
            cp.start()
        for cp in cps:
            cp.wait()

    return pl.pallas_call(
        body, name=name, in_specs=[ANY] * T, out_specs=[ANY] * T,
        out_shape=[jax.ShapeDtypeStruct((3,) + s.shape[1:], s.dtype) for s in sums],
        scratch_shapes=[pltpu.SemaphoreType.DMA((T, 3)), pltpu.SemaphoreType.DMA((T, 3))],
    )(*sums)


def scatter_start(sums, *, name):
    T = len(sums)
    lands = [_in_hbm(lax.empty((3,) + s.shape[1:], s.dtype)) for s in sums]

    def body(*refs):
        sems = refs[2 * T:2 * T + 6]
        src, dst = refs[2 * T + 6:3 * T + 6], refs[3 * T + 6:4 * T + 6]
        token = refs[4 * T + 6]
        x, y, c = _mesh_pos()
        for t in range(T):
            for j, (cx, cy) in enumerate(_other_chips(x, y)):
                pltpu.make_async_remote_copy(src_ref=src[t].at[pl.ds(2 * cx + cy, 1)], dst_ref=dst[t].at[pl.ds(j, 1)],
                                             send_sem=sems[j], recv_sem=sems[3 + j], device_id=(cx, cy, c),
                                             device_id_type=MESH).start()
        token[...] = jnp.zeros_like(token)

    outs = pl.pallas_call(
        body, name=name,
        in_specs=[HBM] * (2 * T),
        out_specs=[SEM] * 6 + [HBM] * (2 * T) + [pl.BlockSpec(memory_space=pltpu.VMEM)],
        out_shape=[pltpu.SemaphoreType.DMA(())] * 6 + [pltpu.HBM(a.shape, a.dtype) for a in list(sums) + lands]
        + [jax.ShapeDtypeStruct((8, LANES), F32)],
        input_output_aliases={t: 6 + t for t in range(2 * T)},
        compiler_params=pltpu.CompilerParams(has_side_effects=EFFECT),
    )(*[_in_hbm(s) for s in sums], *lands)
    return list(outs[:6]), list(outs[6:6 + T]), list(outs[6 + T:6 + 2 * T]), outs[-1]


def scatter_wait(sums, lands, sems, after, *, name):
    T = len(sums)

    def body(*refs):
        sem_refs = refs[2 * T:2 * T + 6]
        src, dst = refs[2 * T + 7:3 * T + 7], refs[3 * T + 7:4 * T + 7]
        x, y, c = _mesh_pos()
        for t in range(T):
            for j, (cx, cy) in enumerate(_other_chips(x, y)):
                cp = pltpu.make_async_remote_copy(src_ref=src[t].at[pl.ds(2 * cx + cy, 1)],
                                                  dst_ref=dst[t].at[pl.ds(j, 1)], send_sem=sem_refs[j],
                                                  recv_sem=sem_refs[3 + j], device_id=(cx, cy, c),
                                                  device_id_type=MESH)
                cp.wait_recv()
                cp.wait_send()

    outs = pl.pallas_call(
        body, name=name,
        in_specs=[HBM] * (2 * T) + [SEM] * 6 + [ANY],
        out_specs=[HBM] * (2 * T),
        out_shape=[pltpu.HBM(a.shape, a.dtype) for a in list(sums) + list(lands)],
        input_output_aliases={t: t for t in range(2 * T)},
        compiler_params=pltpu.CompilerParams(has_side_effects=EFFECT),
    )(*sums, *lands, *sems, after)
    return list(outs[:T]), list(outs[T:])


def owner_sum(sums, got, s, c, grad, layer, *, name, tr=512):
    rh, C = sums.shape[2:]
    tr = _tile(rh, tr)
    nr = rh // tr

    def body(sc_ref, a_ref, b0_ref, b1_ref, b2_ref, g_ref, o_ref):
        o_ref[...] = ((a_ref[...].astype(F32) + b0_ref[...].astype(F32)) + b1_ref[...].astype(F32)) \
            + b2_ref[...].astype(F32)

    def got_spec(j):
        return pl.BlockSpec((None, None, tr, C), lambda i, sc_ref: (j, 0, i, 0))

    grid_spec = pltpu.PrefetchScalarGridSpec(
        num_scalar_prefetch=1, grid=(nr,),
        in_specs=[pl.BlockSpec((None, None, tr, C), lambda i, sc_ref: (sc_ref[0], 0, i, 0)),
                  got_spec(0), got_spec(1), got_spec(2), ANY],
        out_specs=pl.BlockSpec((None, tr, C), lambda i, sc_ref: (layer, sc_ref[1] * nr + i, 0)))
    return pl.pallas_call(
        body, name=name, grid_spec=grid_spec, out_shape=jax.ShapeDtypeStruct(grad.shape, F32),
        input_output_aliases={5: 0},
        compiler_params=_cparams("parallel"),
    )(jnp.stack([s, c]).astype(jnp.int32), sums, got, got, got, grad)


def join_halves(grads, *, name):
    T = len(grads)

    def body(*refs):
        out_refs = refs[T:2 * T]
        send_sems, recv_sems = refs[2 * T:]
        x, y, c = _mesh_pos()

        def win(t, hf):
            rh = grads[t].shape[1] // 2
            return out_refs[t].at[:, pl.ds(hf * rh, rh), :]

        def remote(t, w):
            return pltpu.make_async_remote_copy(src_ref=w, dst_ref=w, send_sem=send_sems.at[t],
                                                recv_sem=recv_sems.at[t], device_id=(x, y, 1 - c),
                                                device_id_type=MESH)

        cps = [remote(t, win(t, c)) for t in range(T)]
        for cp in cps:
            cp.start()
        for t in range(T):
            remote(t, win(t, 1 - c)).wait_recv()
        for cp in cps:
            cp.wait_send()

    return pl.pallas_call(
        body, name=name, in_specs=[ANY] * T, out_specs=[ANY] * T,
        out_shape=[jax.ShapeDtypeStruct(g.shape, g.dtype) for g in grads],
        input_output_aliases={t: t for t in range(T)},
        scratch_shapes=[pltpu.SemaphoreType.DMA((T,)), pltpu.SemaphoreType.DMA((T,))],
    )(*grads)


def allreduce_small(v, *, name):
    rows = v.shape[0]

    def body(v_ref, out_ref, buf, send_sems, recv_sems):
        x, y, c = _mesh_pos()
        me = 4 * x + 2 * y + c
        flips = [(dx, dy, dc) for dx in (0, 1) for dy in (0, 1) for dc in (0, 1)][1:]

        def peer(f):
            return tuple(1 - p if d else p for d, p in zip(f, (x, y, c)))

        cps = []
        for k, f in enumerate(flips):
            px, py, pc = peer(f)
            cps.append(pltpu.make_async_remote_copy(src_ref=v_ref, dst_ref=buf.at[me], send_sem=send_sems.at[k],
                                                    recv_sem=recv_sems.at[k], device_id=(px, py, pc),
                                                    device_id_type=MESH))
        for cp in cps:
            cp.start()
        buf[me] = v_ref[...]
        for k, f in enumerate(flips):
            px, py, pc = peer(f)
            slot = buf.at[4 * px + 2 * py + pc]
            pltpu.make_async_remote_copy(src_ref=slot, dst_ref=slot, send_sem=send_sems.at[k],
                                         recv_sem=recv_sems.at[k], device_id=(px, py, pc),
                                         device_id_type=MESH).wait_recv()
        for cp in cps:
            cp.wait_send()
        acc = buf[0]
        for i in range(1, 8):
            acc = acc + buf[i]
        out_ref[...] = acc

    vm = pl.BlockSpec(memory_space=pltpu.VMEM)
    return pl.pallas_call(
        body, name=name, in_specs=[vm], out_specs=vm,
        out_shape=jax.ShapeDtypeStruct(v.shape, F32),
        scratch_shapes=[pltpu.VMEM((8, rows, LANES), F32), pltpu.SemaphoreType.DMA((7,)),
                        pltpu.SemaphoreType.DMA((7,))],
    )(v)


def _adam_math(w, m, v, g):
    m = ADAM_B1 * m + (1.0 - ADAM_B1) * g
    v = ADAM_B2 * v + (1.0 - ADAM_B2) * (g * g)
    m_hat = m / (1.0 - ADAM_B1 ** ADAM_STEP)
    v_hat = v / (1.0 - ADAM_B2 ** ADAM_STEP)
    delta = -ADAM_LR * (m_hat / (jnp.sqrt(v_hat) + ADAM_EPS) + ADAM_WD * w)
    return delta, m, v


def adamw(w, m, v, g, *, name, tr=256):
    def body(w_ref, m_ref, v_ref, g_ref, d_ref, nm_ref, nv_ref):
        d, nm, nv = _adam_math(w_ref[...], m_ref[...], v_ref[...], g_ref[...])
        d_ref[...] = d
        nm_ref[...] = nm
        nv_ref[...] = nv

    shape = jax.ShapeDtypeStruct(w.shape, F32)
    if w.ndim == 2:
        return pl.pallas_call(body, name=name, out_shape=[shape] * 3)(w, m, v, g)
    n, R, C = w.shape
    tr = _tile(R, tr)
    spec = pl.BlockSpec((None, tr, C), lambda l, i: (l, i, 0))
    return pl.pallas_call(
        body, name=name, grid=(n, R // tr), in_specs=[spec] * 4, out_specs=[spec] * 3, out_shape=[shape] * 3,
        compiler_params=_cparams("parallel", "parallel"),
    )(w, m, v, g)


def _pack(parts):
    flat = jnp.concatenate([p.reshape(-1).astype(F32) for p in parts])
    size = -(-flat.shape[0] // (8 * LANES)) * (8 * LANES)
    return jnp.pad(flat, (0, size - flat.shape[0])).reshape(size // LANES, LANES)


def _unpack(block, shapes):
    flat = block.reshape(-1)
    out, pos = [], 0
    for shp in shapes:
        size = int(np.prod(shp))
        out.append(flat[pos:pos + size].reshape(shp))
        pos += size
    return out


SMALL = ("norm_mix_g", "norm_mlp_g", "final_norm_g", "hgrn_lb_raw", "hgrn_norm_g", "odd_sinks")
WEIGHTS = ("norm_mix_g", "norm_mlp_g", "final_norm_g", "even_w_in", "even_w_out", "hgrn_lb_raw", "hgrn_norm_g",
           "odd_w_qkv", "odd_b_qkv", "odd_sinks", "odd_w_o", "odd_b_o", "mlp_w1", "mlp_w2")


def kernel(x, norm_mix_g, norm_mlp_g, final_norm_g, even_w_in, even_w_out, hgrn_lb_raw, hgrn_norm_g, odd_w_qkv, odd_b_qkv, odd_sinks, odd_w_o, odd_b_o, mlp_w1, mlp_w2, loss_target, m_norm_mix_g, m_norm_mlp_g, m_final_norm_g, m_even_w_in, m_even_w_out, m_hgrn_lb_raw, m_hgrn_norm_g, m_odd_w_qkv, m_odd_b_qkv, m_odd_sinks, m_odd_w_o, m_odd_b_o, m_mlp_w1, m_mlp_w2, v_norm_mix_g, v_norm_mlp_g, v_final_norm_g, v_even_w_in, v_even_w_out, v_hgrn_lb_raw, v_hgrn_norm_g, v_odd_w_qkv, v_odd_b_qkv, v_odd_sinks, v_odd_w_o, v_odd_b_o, v_mlp_w1, v_mlp_w2):
    args = locals()
    W = {k: args[k] for k in WEIGHTS}
    M = {k: args["m_" + k] for k in WEIGHTS}
    V = {k: args["v_" + k] for k in WEIGHTS}
    _, S, D = x.shape
    depth = norm_mix_g.shape[0]
    HQ = D // C_DIM
    HKV = HQ // C_GROUP
    xi, yi, ci = _mesh_pos()
    shard = 2 * xi + yi

    n_odd, bq = odd_b_qkv.shape
    bo = odd_b_o.shape[1]
    keep = (ci == 0).astype(F32)
    pieces = [lax.dynamic_update_slice(jnp.zeros((n_odd, 4 * bq), F32), odd_b_qkv * keep, (0, shard * bq)),
              lax.dynamic_update_slice(jnp.zeros((n_odd, 4 * bo), F32), odd_b_o * keep, (0, shard * bo))]
    biases = allreduce_small(_pack(pieces), name="gather_biases")
    b_qkv_full, b_o_full = _unpack(biases, [(n_odd, 4 * bq), (n_odd, 4 * bo)])
    P = {k: W[k] for k in SMALL}
    P.update(odd_b_qkv=_pad_heads_cols(b_qkv_full, HQ + 2 * HKV), odd_b_o=b_o_full)

    def layer_keys(layer):
        mixer = ("even_w_in", "even_w_out") if layer % 2 == 0 else ("odd_w_qkv", "odd_w_o")
        return [(k, layer // 2) for k in mixer] + [("mlp_w1", layer), ("mlp_w2", layer)]

    keys = [layer_keys(0)[:1], layer_keys(0)[1:]] + [layer_keys(layer) for layer in range(1, depth)]
    shardings = [[SHARDING[k] for k, _ in ks] for ks in keys]
    started, dep0 = gather_start(
        [[cast_into_full(W[k], i, SHARDING[k], shard, name="cast_" + k) for k, i in ks] for ks in keys],
        shardings, biases, name="gather_start")
    forwarded = {}

    def forward_group(g, after):
        sems, mats = started[g]
        forwarded[g] = gather_forward(mats, shardings[g], sems, after, name="gather_forward")
        return forwarded[g][2]

    def finish_group(g, after):
        sems, mats, _ = forwarded[g]
        mats = gather_finish(mats, shardings[g], sems, after, name="gather_finish")
        Wl = {k: m for (k, _), m in zip(keys[g], mats)}
        if "odd_w_qkv" in Wl:
            Wl["odd_w_qkv"] = _pad_heads_cols(Wl["odd_w_qkv"], HQ + 2 * HKV)
            Wl["odd_w_o"] = _pad_heads_rows(Wl["odd_w_o"], HQ)
        return Wl

    def weights_in(layer, x_in):
        if layer == 0:
            forward_group(0, dep0)
        return finish_group(layer + 1 if layer else 0, x_in)

    def mid(layer, a):
        if layer == 0:
            forward_group(1, a)

    def weights_rest(layer, a):
        return finish_group(1, a) if layer == 0 else {}

    def after_mixer(layer, x1):
        return forward_group(layer + 2, x1) if layer + 1 < depth else None

    reduced = {k: lax.empty(W[k].shape, F32) for k in BIG}
    in_flight = []

    def land(after):
        for kis, (sems, sums, lands, _) in in_flight:
            sums, lands = scatter_wait(sums, lands, sems, after, name="scatter_wait")
            for (k, i), s, g in zip(kis, sums, lands):
                reduced[k] = owner_sum(s, g, shard, ci, reduced[k], i, name="owner_sum_" + k)
        in_flight.clear()

    def grads_ready(kis, B, after):
        land(after)
        bufs = [B[ki].reshape((4, 2, 1) + B[ki].shape[2:]) for ki in kis]
        got = exchange_halves(bufs, name="exchange_halves")
        sums = [pair_sum(b, g, ci, name="pair_sum_" + k) for (k, _), b, g in zip(kis, bufs, got)]
        in_flight.append((kis, scatter_start(sums, name="scatter_start")))
        return in_flight[-1][1][3]

    hooks = dict(weights_in=weights_in, mid=mid, weights_rest=weights_rest, after_mixer=after_mixer,
                 grads_ready=grads_ready)
    loss, dx, G, B = local_step(x[0], loss_target[0], P, hooks)
    land(dx)
    grads = dict(zip(BIG, join_halves([reduced[k] for k in BIG], name="join_halves")))

    small_keys = SMALL + ("odd_b_qkv", "odd_b_o")
    small_shapes = [G[k].shape for k in small_keys]
    small = _unpack(allreduce_small(_pack([G[k] for k in small_keys]), name="reduce_small"), small_shapes)
    grads.update(zip(small_keys, small))
    grads["odd_b_qkv"] = lax.dynamic_slice(grads["odd_b_qkv"], (0, shard * bq), (n_odd, bq))
    grads["odd_b_o"] = lax.dynamic_slice(grads["odd_b_o"], (0, shard * bo), (n_odd, bo))

    deltas, new_m, new_v = {}, {}, {}
    for k in WEIGHTS:
        w2 = (lambda a: a.reshape(1, -1)) if W[k].ndim == 1 else (lambda a: a)
        d, nm, nv = adamw(w2(W[k]), w2(M[k]), w2(V[k]), w2(grads[k]), name="adamw_" + k)
        deltas[k], new_m[k], new_v[k] = (a.reshape(W[k].shape) for a in (d, nm, nv))

    loss = lax.psum(loss[0, 0], ("x", "y", "c"))
    return (loss, dx[None], *[grads[k] for k in WEIGHTS], *[deltas[k] for k in WEIGHTS],
            *[new_m[k] for k in WEIGHTS], *[new_v[k] for k in WEIGHTS])
```

```python
import functools
import math

import jax
import jax.numpy as jnp
import numpy as np
from jax import lax
from jax.experimental import pallas as pl
from jax.experimental.pallas import tpu as pltpu

F32 = jnp.float32
BF16 = jnp.bfloat16

NORM_EPS = 1e-5
ROPE_THETA = 500000.0
ROPE_FRACTION = 4
LANES = 128
BAND = 128
A_DIM = 128
A_BRANCHES = ((128, 1), (512, 4), (2048, 16))
B_DIM = 128
B_CHUNK = 64
B_SUB = 16
A_HEADS_PER_STEP = 2
B_HEADS_PER_STEP = 2
C_DIM = 64
C_GROUP = 8
C_WINDOW = 128

ADAM_LR = 0.001
ADAM_B1 = 0.9
ADAM_B2 = 0.999
ADAM_EPS = 1e-08
ADAM_WD = 0.01
ADAM_STEP = 10

VMEM_LIMIT = 56 * 1024 * 1024


def _cparams(*sem):
    return pltpu.CompilerParams(dimension_semantics=tuple(sem), vmem_limit_bytes=VMEM_LIMIT)


def _tile(n, want):
    if n <= want:
        return n
    t = want - want % LANES
    while n % t:
        t -= LANES
    assert t > 0, (n, want)
    return t


def matmul(a, b, *, trans_a=False, trans_b=False, out_dtypes, a_fn=None, epilogue=None, extras=(), name,
           b_layer=None, into=None, dep=None, tm=1024, tn=1024, tk=2048):
    if trans_a:
        K, M = a.shape
    else:
        M, K = a.shape
    b_mat = b.shape[1:] if b_layer is not None else b.shape
    if trans_b:
        N, K2 = b_mat
    else:
        K2, N = b_mat
    assert K == K2, (a.shape, b.shape)
    if into is not None:
        rh, cs = into[0][2:]
        tm, tn = _tile(rh, tm), _tile(cs, tn)
    tm, tn, tk = _tile(M, tm), _tile(N, tn), _tile(K, tk)
    nk = K // tk
    n_extra = len(extras)
    n_out = len(out_dtypes)

    def body(*refs):
        a_ref, b_ref = refs[0], refs[1]
        extra_refs = refs[2:2 + n_extra]
        out_refs = refs[-1 - n_out:-1]
        acc_ref = refs[-1]
        k = pl.program_id(2)
        at = a_ref[...]
        if a_fn is not None:
            at = a_fn(at.astype(F32))
        at = at.astype(BF16)
        bt = b_ref[...].astype(BF16)
        dims = (((0,) if trans_a else (1,), (1,) if trans_b else (0,)), ((), ()))
        prod = lax.dot_general(at, bt, dims, preferred_element_type=F32)

        def finish(acc):
            ex = [r[...] for r in extra_refs]
            outs = epilogue(acc, *ex) if epilogue is not None else (acc,)
            for o_ref, o in zip(out_refs, outs):
                o_ref[...] = o.astype(o_ref.dtype)

        if nk == 1:
            finish(prod)
            return

        @pl.when(k == 0)
        def _():
            acc_ref[...] = prod

        @pl.when((k > 0) & (k < nk - 1))
        def _():
            acc_ref[...] += prod

        @pl.when(k == nk - 1)
        def _():
            finish(acc_ref[...] + prod)

    a_spec = pl.BlockSpec((tk, tm), lambda i, j, k: (k, i)) if trans_a else pl.BlockSpec((tm, tk), lambda i, j, k: (i, k))
    if b_layer is None:
        b_spec = (pl.BlockSpec((tn, tk), lambda i, j, k: (j, k)) if trans_b
                  else pl.BlockSpec((tk, tn), lambda i, j, k: (k, j)))
    else:
        b_spec = (pl.BlockSpec((None, tn, tk), lambda i, j, k: (b_layer, j, k)) if trans_b
                  else pl.BlockSpec((None, tk, tn), lambda i, j, k: (b_layer, k, j)))
    e_specs = []
    for e in extras:
        if e.shape == (1, N):
            e_specs.append(pl.BlockSpec((1, tn), lambda i, j, k: (0, j)))
        else:
            assert e.shape == (M, N), (e.shape, M, N)
            e_specs.append(pl.BlockSpec((tm, tn), lambda i, j, k: (i, j)))
    args = [a, b, *extras]
    in_specs = [a_spec, b_spec] + e_specs
    if dep is not None:
        args.append(dep)
        in_specs.append(pl.BlockSpec(dep.shape, lambda i, j, k: (0, 0)))
    if into is None:
        out_specs = [pl.BlockSpec((tm, tn), lambda i, j, k: (i, j)) for _ in out_dtypes]
        out_shape = [jax.ShapeDtypeStruct((M, N), dt) for dt in out_dtypes]
    else:
        buf_shape, sharding = into
        assert n_out == 1
        index = grad_buffer_index(buf_shape, (M, N), sharding, tm, tn)
        out_specs = [pl.BlockSpec((None, None, tm, tn), lambda i, j, k: index(i, j))]
        out_shape = [jax.ShapeDtypeStruct(buf_shape, out_dtypes[0])]
    outs = pl.pallas_call(
        body, name=name,
        grid=(M // tm, N // tn, nk),
        in_specs=in_specs, out_specs=out_specs, out_shape=out_shape,
        scratch_shapes=[pltpu.VMEM((tm, tn), F32)],
        compiler_params=_cparams("parallel", "parallel", "arbitrary"),
    )(*args)
    return outs[0] if n_out == 1 else tuple(outs)


def grad_buffer_shape(mat_shape, sharding):
    M, N = mat_shape
    return (4, 2, M // 2, N // 4) if sharding == "cols" else (4, 2, M // 8, N)


def grad_buffer_index(buf_shape, mat_shape, sharding, tm, tn):
    _, _, rh, cs = buf_shape
    M, N = mat_shape
    ib, jb = rh // tm, cs // tn
    if sharding == "cols":
        assert (M, N) == (2 * rh, 4 * cs), (buf_shape, mat_shape)
        return lambda i, j: (j // jb, i // ib, i % ib, j % jb)
    assert sharding == "rows" and (M, N) == (8 * rh, cs), (buf_shape, mat_shape)
    return lambda i, j: (i // (2 * ib), (i // ib) % 2, i % ib, j)


def rmsnorm_fwd(x, g, *, name, dep=None, tr=256):
    S, D = x.shape
    tr = _tile(S, tr)

    def body(x_ref, g_ref, *rest):
        h_ref = rest[-1]
        xv = x_ref[...]
        rstd = lax.rsqrt(jnp.mean(xv * xv, axis=-1, keepdims=True) + NORM_EPS)
        h_ref[...] = (xv * rstd * g_ref[...]).astype(h_ref.dtype)

    args = [x, g.reshape(1, D)]
    in_specs = [pl.BlockSpec((tr, D), lambda i: (i, 0)), pl.BlockSpec((1, D), lambda i: (0, 0))]
    if dep is not None:
        args.append(dep)
        in_specs.append(pl.BlockSpec(dep.shape, lambda i: (0, 0)))
    return pl.pallas_call(
        body, name=name, grid=(S // tr,),
        in_specs=in_specs,
        out_specs=pl.BlockSpec((tr, D), lambda i: (i, 0)),
        out_shape=jax.ShapeDtypeStruct((S, D), BF16),
        compiler_params=_cparams("parallel"),
    )(*args)


def _rms_bwd_rows(xv, gv, dh):
    rstd = lax.rsqrt(jnp.mean(xv * xv, axis=-1, keepdims=True) + NORM_EPS)
    xhat = xv * rstd
    dxhat = dh * gv
    dx = rstd * (dxhat - xhat * jnp.mean(dxhat * xhat, axis=-1, keepdims=True))
    return dx, dh * xhat


def rmsnorm_bwd(x, g, dh, dres, *, name, tr=256):
    S, D = x.shape
    tr = _tile(S, tr)

    def body(x_ref, g_ref, dh_ref, dres_ref, dx_ref, dg_ref):
        @pl.when(pl.program_id(0) == 0)
        def _():
            dg_ref[...] = jnp.zeros_like(dg_ref)

        dx, dgr = _rms_bwd_rows(x_ref[...], g_ref[...], dh_ref[...].astype(F32))
        dx_ref[...] = dres_ref[...] + dx
        dg_ref[...] += jnp.sum(dgr, axis=0, keepdims=True)

    row = pl.BlockSpec((tr, D), lambda i: (i, 0))
    vec = pl.BlockSpec((1, D), lambda i: (0, 0))
    return pl.pallas_call(
        body, name=name, grid=(S // tr,),
        in_specs=[row, vec, row, row],
        out_specs=[row, vec],
        out_shape=[jax.ShapeDtypeStruct((S, D), F32), jax.ShapeDtypeStruct((1, D), F32)],
        compiler_params=_cparams("arbitrary"),
    )(x, g.reshape(1, D), dh, dres)


def final_norm_loss(x, g, target, *, name, tr=256):
    S, D = x.shape
    tr = _tile(S, tr)

    def body(x_ref, g_ref, t_ref, loss_ref, dx_ref, dg_ref):
        @pl.when(pl.program_id(0) == 0)
        def _():
            dg_ref[...] = jnp.zeros_like(dg_ref)
            loss_ref[...] = jnp.zeros_like(loss_ref)

        xv, gv = x_ref[...], g_ref[...]
        rstd = lax.rsqrt(jnp.mean(xv * xv, axis=-1, keepdims=True) + NORM_EPS)
        err = xv * rstd * gv - t_ref[...]
        part = 0.5 * jnp.sum(jnp.mean(err * err, axis=-1, keepdims=True), axis=0, keepdims=True)
        loss_ref[...] += jnp.broadcast_to(part, loss_ref.shape)
        dx, dgr = _rms_bwd_rows(xv, gv, err * (1.0 / D))
        dx_ref[...] = dx
        dg_ref[...] += jnp.sum(dgr, axis=0, keepdims=True)

    row = pl.BlockSpec((tr, D), lambda i: (i, 0))
    vec = pl.BlockSpec((1, D), lambda i: (0, 0))
    return pl.pallas_call(
        body, name=name, grid=(S // tr,),
        in_specs=[row, vec, row],
        out_specs=[pl.BlockSpec((8, LANES), lambda i: (0, 0)), row, vec],
        out_shape=[jax.ShapeDtypeStruct((8, LANES), F32), jax.ShapeDtypeStruct((S, D), F32),
                   jax.ShapeDtypeStruct((1, D), F32)],
        compiler_params=_cparams("arbitrary"),
    )(x, g.reshape(1, D), target)


def colsum(a, *, name, tr=256):
    S, N = a.shape
    tr = _tile(S, tr)

    def body(a_ref, o_ref):
        @pl.when(pl.program_id(0) == 0)
        def _():
            o_ref[...] = jnp.zeros_like(o_ref)

        o_ref[...] += jnp.sum(a_ref[...].astype(F32), axis=0, keepdims=True)

    return pl.pallas_call(
        body, name=name, grid=(S // tr,),
        in_specs=[pl.BlockSpec((tr, N), lambda i: (i, 0))],
        out_specs=pl.BlockSpec((1, N), lambda i: (0, 0)),
        out_shape=jax.ShapeDtypeStruct((1, N), F32),
        compiler_params=_cparams("arbitrary"),
    )(a)


def rope_tables(seq, head_dim):
    rot = head_dim // ROPE_FRACTION
    half = rot // 2
    inv_freq = 1.0 / (ROPE_THETA ** (jnp.arange(0, rot, 2, dtype=F32) / rot))
    ang = jnp.arange(seq, dtype=F32)[:, None] * inv_freq[None, :]
    cos, sin = jnp.cos(ang), jnp.sin(ang)
    ones = jnp.ones((seq, LANES - 2 * half), F32)
    zeros = jnp.zeros((seq, LANES - 2 * half), F32)
    zh = jnp.zeros((seq, half), F32)
    c = jnp.concatenate([cos, cos, ones], axis=1)
    sa = jnp.concatenate([-sin, zh, zeros], axis=1)
    sb = jnp.concatenate([zh, sin, zeros], axis=1)
    return c, sa, sb, half


def _rope(x, c, sa, sb, half):
    return x * c + pltpu.roll(x, LANES - half, 1) * sa + pltpu.roll(x, half, 1) * sb


def _rope_bwd(d, c, sa, sb, half):
    return d * c + pltpu.roll(d * sa, half, 1) + pltpu.roll(d * sb, LANES - half, 1)


def _band_masks(n, max_dist):
    qi = lax.broadcasted_iota(jnp.int32, (BAND, BAND), 0)
    kj = lax.broadcasted_iota(jnp.int32, (BAND, BAND), 1)
    cur = kj <= qi
    prev = ((kj >= qi) if max_dist == BAND else (kj > qi)) & (n > 0)
    return prev, cur


def _dot_nt(a, b):
    return lax.dot_general(a, b, (((1,), (1,)), ((), ())), preferred_element_type=F32)


def _dot_tn(a, b):
    return lax.dot_general(a, b, (((0,), (0,)), ((), ())), preferred_element_type=F32)


def _dot(a, b):
    return jnp.dot(a, b, preferred_element_type=F32)


def band_attention_fwd(src, tabs, *, L, dil, n_heads, group, q_blk, k_blk, v_blk, blk_per_row, max_dist, scale,
                       half, sink=None, normalize, name):
    c_t, sa_t, sb_t = tabs
    nb = L // BAND
    W = dil * n_heads * LANES
    use_sink = sink is not None

    def body(*refs):
        q_ref, k_ref, v_ref, c_ref, sa_ref, sb_ref = refs[:6]
        pos = 6
        if use_sink:
            sink_ref = refs[pos]
            pos += 1
        o_ref, m_ref, l_ref = refs[pos:pos + 3] if not normalize else (refs[pos], refs[pos + 1], None)
        kr_ref, vb_ref = refs[-2], refs[-1]

        def prep(n, carry):
            rows = pl.ds(pl.multiple_of(n * BAND, BAND), BAND)
            kr_ref[rows, :] = _rope(k_ref[rows, :], c_ref[rows, :], sa_ref[rows, :], sb_ref[rows, :], half).astype(BF16)
            vb_ref[rows, :] = v_ref[rows, :].astype(BF16)
            return carry

        lax.fori_loop(0, nb, prep, 0)

        def step(n, carry):
            rows = pl.ds(pl.multiple_of(n * BAND, BAND), BAND)
            prow = pl.ds(pl.multiple_of(jnp.maximum(n - 1, 0) * BAND, BAND), BAND)
            q = _rope(q_ref[rows, :], c_ref[rows, :], sa_ref[rows, :], sb_ref[rows, :], half).astype(BF16)
            mp, mc = _band_masks(n, max_dist)
            sp = jnp.where(mp, _dot_nt(q, kr_ref[prow, :]) * scale, -jnp.inf)
            sc = jnp.where(mc, _dot_nt(q, kr_ref[rows, :]) * scale, -jnp.inf)
            m = jnp.maximum(jnp.max(sp, axis=-1, keepdims=True), jnp.max(sc, axis=-1, keepdims=True))
            if use_sink:
                sk = sink_ref[:, 0:1]
                m = jnp.maximum(m, sk)
            pp = jnp.exp(sp - m)
            pc = jnp.exp(sc - m)
            l = jnp.sum(pp, axis=-1, keepdims=True) + jnp.sum(pc, axis=-1, keepdims=True)
            if use_sink:
                l = l + jnp.exp(sk - m)
            num = _dot(pp.astype(BF16), vb_ref[prow, :]) + _dot(pc.astype(BF16), vb_ref[rows, :])
            if normalize:
                o_ref[rows, :] = num / l
                m_ref[rows, :] = jnp.broadcast_to(m + jnp.log(l), (BAND, LANES))
            else:
                o_ref[rows, :] = num
                m_ref[rows, :] = jnp.broadcast_to(m, (BAND, LANES))
                l_ref[rows, :] = jnp.broadcast_to(l, (BAND, LANES))
            return carry

        lax.fori_loop(0, nb, step, 0)

    def col(off, div):
        return pl.BlockSpec((L, LANES), lambda r, h: (0, r * blk_per_row + off + h // div))

    tab = pl.BlockSpec((L, LANES), lambda r, h: (0, r))
    out = pl.BlockSpec((L, LANES), lambda r, h: (0, r * n_heads + h))
    in_specs = [col(q_blk, 1), col(k_blk, group), col(v_blk, group), tab, tab, tab]
    args = [src, src, src, c_t, sa_t, sb_t]
    if use_sink:
        in_specs.append(pl.BlockSpec((None, 1, LANES), lambda r, h: (h, 0, 0)))
        args.append(sink)
    n_out = 2 if normalize else 3
    return pl.pallas_call(
        body, name=name, grid=(dil, n_heads),
        in_specs=in_specs,
        out_specs=[out] * n_out,
        out_shape=[jax.ShapeDtypeStruct((L, W), F32)] * n_out,
        scratch_shapes=[pltpu.VMEM((L, LANES), BF16), pltpu.VMEM((L, LANES), BF16)],
        compiler_params=_cparams("parallel", "arbitrary"),
    )(*args)


def band_attention_bwd(src, tabs, o, lse, do, *, L, dil, n_heads, group, q_blk, k_blk, v_blk, blk_per_row, max_dist,
                       scale, half, do_bpr, sink=None, acc=None, name):
    c_t, sa_t, sb_t = tabs
    nb = L // BAND
    n_kv = n_heads // group
    use_sink = sink is not None
    use_acc = acc is not None
    assert not (use_acc and group != 1)

    def body(*refs):
        q_ref, k_ref, v_ref, c_ref, sa_ref, sb_ref, o_ref, lse_ref, do_ref = refs[:9]
        pos = 9
        if use_sink:
            sink_ref = refs[pos]
            pos += 1
        if use_acc:
            aq_ref, ak_ref, av_ref = refs[pos:pos + 3]
            pos += 3
        dq_ref, dk_ref, dv_ref = refs[pos:pos + 3]
        pos += 3
        if use_sink:
            dsink_ref = refs[pos]
        kr_ref, vb_ref, dka_ref, dva_ref = refs[-4:]
        h = pl.program_id(1)

        def prep(n, carry):
            rows = pl.ds(pl.multiple_of(n * BAND, BAND), BAND)
            kr_ref[rows, :] = _rope(k_ref[rows, :], c_ref[rows, :], sa_ref[rows, :], sb_ref[rows, :], half).astype(BF16)
            vb_ref[rows, :] = v_ref[rows, :].astype(BF16)
            return carry

        lax.fori_loop(0, nb, prep, 0)

        @pl.when(h % group == 0)
        def _():
            dka_ref[...] = jnp.zeros_like(dka_ref)
            dva_ref[...] = jnp.zeros_like(dva_ref)

        def step(n, dsk):
            rows = pl.ds(pl.multiple_of(n * BAND, BAND), BAND)
            prow = pl.ds(pl.multiple_of(jnp.maximum(n - 1, 0) * BAND, BAND), BAND)
            cv, sav, sbv = c_ref[rows, :], sa_ref[rows, :], sb_ref[rows, :]
            q = _rope(q_ref[rows, :], cv, sav, sbv, half).astype(BF16)
            dov = do_ref[rows, :]
            lsev = lse_ref[rows, 0:1]
            delta = jnp.sum(dov * o_ref[rows, :], axis=-1, keepdims=True)
            dob = dov.astype(BF16)
            mp, mc = _band_masks(n, max_dist)
            kp, kc, vp, vc = kr_ref[prow, :], kr_ref[rows, :], vb_ref[prow, :], vb_ref[rows, :]
            pp = jnp.exp(jnp.where(mp, _dot_nt(q, kp) * scale, -jnp.inf) - lsev)
            pc = jnp.exp(jnp.where(mc, _dot_nt(q, kc) * scale, -jnp.inf) - lsev)
            dsp = (pp * (_dot_nt(dob, vp) - delta) * scale).astype(BF16)
            dsc = (pc * (_dot_nt(dob, vc) - delta) * scale).astype(BF16)
            dq = _rope_bwd(_dot(dsp, kp) + _dot(dsc, kc), cv, sav, sbv, half)
            if use_acc:
                dq = dq + aq_ref[rows, :]
            dq_ref[rows, :] = dq
            dka_ref[prow, :] += _dot_tn(dsp, q)
            dka_ref[rows, :] += _dot_tn(dsc, q)
            dva_ref[prow, :] += _dot_tn(pp.astype(BF16), dob)
            dva_ref[rows, :] += _dot_tn(pc.astype(BF16), dob)
            if use_sink:
                dsk = dsk - jnp.sum(jnp.exp(sink_ref[:, 0:1] - lsev) * delta, axis=0, keepdims=True)
            return dsk

        dsk = lax.fori_loop(0, nb, step, jnp.zeros((1, 1), F32))
        if use_sink:
            dsink_ref[...] = jnp.broadcast_to(dsk, dsink_ref.shape)

        @pl.when(h % group == group - 1)
        def _():
            def fin(n, carry):
                rows = pl.ds(pl.multiple_of(n * BAND, BAND), BAND)
                dk = _rope_bwd(dka_ref[rows, :], c_ref[rows, :], sa_ref[rows, :], sb_ref[rows, :], half)
                dv = dva_ref[rows, :]
                if use_acc:
                    dk = dk + ak_ref[rows, :]
                    dv = dv + av_ref[rows, :]
                dk_ref[rows, :] = dk
                dv_ref[rows, :] = dv
                return carry

            lax.fori_loop(0, nb, fin, 0)

    def col(off, div):
        return pl.BlockSpec((L, LANES), lambda r, h: (0, r * blk_per_row + off + h // div))

    tab = pl.BlockSpec((L, LANES), lambda r, h: (0, r))
    qo = pl.BlockSpec((L, LANES), lambda r, h: (0, r * n_heads + h))
    kvo = pl.BlockSpec((L, LANES), lambda r, h: (0, r * n_kv + h // group))
    dospec = pl.BlockSpec((L, LANES), lambda r, h: (0, r * do_bpr + h))
    in_specs = [col(q_blk, 1), col(k_blk, group), col(v_blk, group), tab, tab, tab, qo, qo, dospec]
    args = [src, src, src, c_t, sa_t, sb_t, o, lse, do]
    if use_sink:
        in_specs.append(pl.BlockSpec((None, 1, LANES), lambda r, h: (h, 0, 0)))
        args.append(sink)
    if use_acc:
        in_specs += [qo, kvo, kvo]
        args += list(acc)
    out_specs = [qo, kvo, kvo]
    out_shape = [jax.ShapeDtypeStruct((L, dil * n_heads * LANES), F32),
                 jax.ShapeDtypeStruct((L, dil * n_kv * LANES), F32),
                 jax.ShapeDtypeStruct((L, dil * n_kv * LANES), F32)]
    if use_sink:
        out_specs.append(pl.BlockSpec((None, 1, LANES), lambda r, h: (h, 0, 0)))
        out_shape.append(jax.ShapeDtypeStruct((n_heads, 1, LANES), F32))
    return pl.pallas_call(
        body, name=name, grid=(dil, n_heads),
        in_specs=in_specs, out_specs=out_specs, out_shape=out_shape,
        scratch_shapes=[pltpu.VMEM((L, LANES), BF16), pltpu.VMEM((L, LANES), BF16),
                        pltpu.VMEM((L, LANES), F32), pltpu.VMEM((L, LANES), F32)],
        compiler_params=_cparams("parallel", "arbitrary"),
    )(*args)


def banded_fwd(src, tabs, *, L, dil, n_heads, group, hb, q_blk, k_blk, v_blk, blk_per_row, max_dist, scale, half,
               sink=None, normalize, name):
    c_t, sa_t, sb_t = tabs
    nb = L // BAND
    kvb = hb // group
    n_hg = n_heads // hb
    W = dil * n_heads * LANES
    use_sink = sink is not None
    assert n_heads % hb == 0 and hb % group == 0
    assert (q_blk % hb, k_blk % kvb, v_blk % kvb) == (0, 0, 0) and (dil == 1 or blk_per_row % hb == 0)

    def body(*refs):
        q_ref, k_ref, v_ref, c_ref, sa_ref, sb_ref = refs[:6]
        pos = 6
        if use_sink:
            sink_ref = refs[pos]
            pos += 1
        o_ref, m_ref = refs[pos], refs[pos + 1]
        l_ref = None if normalize else refs[pos + 2]
        kr_ref, vb_ref = refs[-2], refs[-1]
        n = pl.program_id(2)

        @pl.when(n == 0)
        def _():
            def prep(b, carry):
                rows = pl.ds(pl.multiple_of(b * BAND, BAND), BAND)
                cv, sav, sbv = c_ref[rows, :], sa_ref[rows, :], sb_ref[rows, :]
                for j in range(kvb):
                    cols = slice(j * LANES, (j + 1) * LANES)
                    kr_ref[rows, cols] = _rope(k_ref[rows, cols], cv, sav, sbv, half).astype(BF16)
                    vb_ref[rows, cols] = v_ref[rows, cols].astype(BF16)
                return carry

            lax.fori_loop(0, nb, prep, 0)

        rows = pl.ds(pl.multiple_of(n * BAND, BAND), BAND)
        prow = pl.ds(pl.multiple_of(jnp.maximum(n - 1, 0) * BAND, BAND), BAND)
        cv, sav, sbv = c_ref[rows, :], sa_ref[rows, :], sb_ref[rows, :]
        mp, mc = _band_masks(n, max_dist)
        for i in range(hb):
            qc = slice(i * LANES, (i + 1) * LANES)
            kc = slice((i // group) * LANES, (i // group + 1) * LANES)
            q = _rope(q_ref[:, qc], cv, sav, sbv, half).astype(BF16)
            sp = jnp.where(mp, _dot_nt(q, kr_ref[prow, kc]) * scale, -jnp.inf)
            sc = jnp.where(mc, _dot_nt(q, kr_ref[rows, kc]) * scale, -jnp.inf)
            m = jnp.maximum(jnp.max(sp, axis=-1, keepdims=True), jnp.max(sc, axis=-1, keepdims=True))
            if use_sink:
                sk = sink_ref[i, :, 0:1]
                m = jnp.maximum(m, sk)
            pp = jnp.exp(sp - m)
            pc = jnp.exp(sc - m)
            l = jnp.sum(pp, axis=-1, keepdims=True) + jnp.sum(pc, axis=-1, keepdims=True)
            if use_sink:
                l = l + jnp.exp(sk - m)
            num = _dot(pp.astype(BF16), vb_ref[prow, kc]) + _dot(pc.astype(BF16), vb_ref[rows, kc])
            if normalize:
                o_ref[:, qc] = num / l
                m_ref[:, qc] = jnp.broadcast_to(m + jnp.log(l), (BAND, LANES))
            else:
                o_ref[:, qc] = num
                m_ref[:, qc] = jnp.broadcast_to(m, (BAND, LANES))
                l_ref[:, qc] = jnp.broadcast_to(l, (BAND, LANES))

    qspec = pl.BlockSpec((BAND, hb * LANES), lambda r, g, n: (n, (r * blk_per_row + q_blk) // hb + g))

    def kv(off):
        return pl.BlockSpec((L, kvb * LANES), lambda r, g, n: (0, (r * blk_per_row + off) // kvb + g))

    tab = pl.BlockSpec((L, LANES), lambda r, g, n: (0, r))
    out = pl.BlockSpec((BAND, hb * LANES), lambda r, g, n: (n, r * n_hg + g))
    in_specs = [qspec, kv(k_blk), kv(v_blk), tab, tab, tab]
    args = [src, src, src, c_t, sa_t, sb_t]
    if use_sink:
        in_specs.append(pl.BlockSpec((hb, 1, LANES), lambda r, g, n: (g, 0, 0)))
        args.append(sink)
    n_out = 2 if normalize else 3
    return pl.pallas_call(
        body, name=name, grid=(dil, n_hg, nb),
        in_specs=in_specs,
        out_specs=[out] * n_out,
        out_shape=[jax.ShapeDtypeStruct((L, W), F32)] * n_out,
        scratch_shapes=[pltpu.VMEM((L, kvb * LANES), BF16), pltpu.VMEM((L, kvb * LANES), BF16)],
        compiler_params=_cparams("parallel", "parallel", "arbitrary"),
    )(*args)


def banded_bwd(src, tabs, o, lse, do, *, L, dil, n_heads, group, hb, q_blk, k_blk, v_blk, blk_per_row, max_dist,
               scale, half, do_bpr, sink=None, acc=None, name):
    c_t, sa_t, sb_t = tabs
    nb = L // BAND
    kvb = hb // group
    n_hg = n_heads // hb
    n_kv = n_heads // group
    use_sink = sink is not None
    use_acc = acc is not None
    assert n_heads % hb == 0 and hb % group == 0 and do_bpr % hb == 0
    assert (q_blk % hb, k_blk % kvb, v_blk % kvb) == (0, 0, 0) and (dil == 1 or blk_per_row % hb == 0)

    def body(*refs):
        q_ref, k_ref, v_ref, c_ref, sa_ref, sb_ref, o_ref, lse_ref, do_ref = refs[:9]
        pos = 9
        if use_sink:
            sink_ref = refs[pos]
            pos += 1
        if use_acc:
            aq_ref, ak_ref, av_ref = refs[pos:pos + 3]
            pos += 3
        dq_ref, dk_ref, dv_ref = refs[pos:pos + 3]
        pos += 3
        if use_sink:
            dsink_ref = refs[pos]
        kr_ref, vb_ref, dka_ref, dva_ref = refs[-4:]
        n = pl.program_id(2)

        @pl.when(n == 0)
        def _():
            def prep(b, carry):
                rows = pl.ds(pl.multiple_of(b * BAND, BAND), BAND)
                cv, sav, sbv = c_ref[rows, :], sa_ref[rows, :], sb_ref[rows, :]
                for j in range(kvb):
                    cols = slice(j * LANES, (j + 1) * LANES)
                    kr_ref[rows, cols] = _rope(k_ref[rows, cols], cv, sav, sbv, half).astype(BF16)
                    vb_ref[rows, cols] = v_ref[rows, cols].astype(BF16)
                return carry

            lax.fori_loop(0, nb, prep, 0)
            dka_ref[...] = jnp.zeros_like(dka_ref)
            dva_ref[...] = jnp.zeros_like(dva_ref)
            if use_sink:
                dsink_ref[...] = jnp.zeros_like(dsink_ref)

        rows = pl.ds(pl.multiple_of(n * BAND, BAND), BAND)
        prow = pl.ds(pl.multiple_of(jnp.maximum(n - 1, 0) * BAND, BAND), BAND)
        cv, sav, sbv = c_ref[rows, :], sa_ref[rows, :], sb_ref[rows, :]
        mp, mc = _band_masks(n, max_dist)
        for i in range(hb):
            qc = slice(i * LANES, (i + 1) * LANES)
            kc = slice((i // group) * LANES, (i // group + 1) * LANES)
            q = _rope(q_ref[:, qc], cv, sav, sbv, half).astype(BF16)
            dov = do_ref[:, qc]
            lsev = lse_ref[:, i * LANES:i * LANES + 1]
            delta = jnp.sum(dov * o_ref[:, qc], axis=-1, keepdims=True)
            dob = dov.astype(BF16)
            kp, kcur, vp, vcur = kr_ref[prow, kc], kr_ref[rows, kc], vb_ref[prow, kc], vb_ref[rows, kc]
            pp = jnp.exp(jnp.where(mp, _dot_nt(q, kp) * scale, -jnp.inf) - lsev)
            pc = jnp.exp(jnp.where(mc, _dot_nt(q, kcur) * scale, -jnp.inf) - lsev)
            dsp = (pp * (_dot_nt(dob, vp) - delta) * scale).astype(BF16)
            dsc = (pc * (_dot_nt(dob, vcur) - delta) * scale).astype(BF16)
            dq = _rope_bwd(_dot(dsp, kp) + _dot(dsc, kcur), cv, sav, sbv, half)
            if use_acc:
                dq = dq + aq_ref[:, qc]
            dq_ref[:, qc] = dq
            dka_ref[prow, kc] += _dot_tn(dsp, q)
            dka_ref[rows, kc] += _dot_tn(dsc, q)
            dva_ref[prow, kc] += _dot_tn(pp.astype(BF16), dob)
            dva_ref[rows, kc] += _dot_tn(pc.astype(BF16), dob)
            if use_sink:
                dsk = jnp.sum(jnp.exp(sink_ref[i, :, 0:1] - lsev) * delta, axis=0, keepdims=True)
                dsink_ref[i] -= jnp.broadcast_to(dsk, (1, LANES))

        @pl.when(n == nb - 1)
        def _():
            def fin(b, carry):
                brow = pl.ds(pl.multiple_of(b * BAND, BAND), BAND)
                bc, bsa, bsb = c_ref[brow, :], sa_ref[brow, :], sb_ref[brow, :]
                for j in range(kvb):
                    cols = slice(j * LANES, (j + 1) * LANES)
                    dk = _rope_bwd(dka_ref[brow, cols], bc, bsa, bsb, half)
                    dv = dva_ref[brow, cols]
                    if use_acc:
                        dk = dk + ak_ref[brow, cols]
                        dv = dv + av_ref[brow, cols]
                    dk_ref[brow, cols] = dk
                    dv_ref[brow, cols] = dv
                return carry

            lax.fori_loop(0, nb, fin, 0)

    qspec = pl.BlockSpec((BAND, hb * LANES), lambda r, g, n: (n, (r * blk_per_row + q_blk) // hb + g))

    def kv(off):
        return pl.BlockSpec((L, kvb * LANES), lambda r, g, n: (0, (r * blk_per_row + off) // kvb + g))

    tab = pl.BlockSpec((L, LANES), lambda r, g, n: (0, r))
    qo = pl.BlockSpec((BAND, hb * LANES), lambda r, g, n: (n, r * n_hg + g))
    kvo = pl.BlockSpec((L, kvb * LANES), lambda r, g, n: (0, r * (n_kv // kvb) + g))
    dospec = pl.BlockSpec((BAND, hb * LANES), lambda r, g, n: (n, r * (do_bpr // hb) + g))
    in_specs = [qspec, kv(k_blk), kv(v_blk), tab, tab, tab, qo, qo, dospec]
    args = [src, src, src, c_t, sa_t, sb_t, o, lse, do]
    sink_spec = pl.BlockSpec((hb, 1, LANES), lambda r, g, n: (g, 0, 0))
    if use_sink:
        in_specs.append(sink_spec)
        args.append(sink)
    if use_acc:
        in_specs += [qo, kvo, kvo]
        args += list(acc)
    out_specs = [qo, kvo, kvo]
    out_shape = [jax.ShapeDtypeStruct((L, dil * n_heads * LANES), F32),
                 jax.ShapeDtypeStruct((L, dil * n_kv * LANES), F32),
                 jax.ShapeDtypeStruct((L, dil * n_kv * LANES), F32)]
    if use_sink:
        out_specs.append(sink_spec)
        out_shape.append(jax.ShapeDtypeStruct((n_heads, 1, LANES), F32))
    return pl.pallas_call(
        body, name=name, grid=(dil, n_hg, nb),
        in_specs=in_specs, out_specs=out_specs, out_shape=out_shape,
        scratch_shapes=[pltpu.VMEM((L, kvb * LANES), BF16), pltpu.VMEM((L, kvb * LANES), BF16),
                        pltpu.VMEM((L, kvb * LANES), F32), pltpu.VMEM((L, kvb * LANES), F32)],
        compiler_params=_cparams("parallel", "parallel", "arbitrary"),
    )(*args)


def merge_branches(parts, *, name, tr=256):
    S, W = parts[0][0].shape
    tr = _tile(S, tr)
    nbr = len(parts)

    def body(*refs):
        ins, (o_ref, lse_ref) = refs[:3 * nbr], refs[3 * nbr:]
        nums = [ins[3 * i][...] for i in range(nbr)]
        ms = [ins[3 * i + 1][...] for i in range(nbr)]
        ls = [ins[3 * i + 2][...] for i in range(nbr)]
        mx = functools.reduce(jnp.maximum, ms)
        ws = [jnp.exp(m - mx) for m in ms]
        num = sum(w * n for w, n in zip(ws, nums))
        den = sum(w * l for w, l in zip(ws, ls))
        o_ref[...] = num / den
        lse_ref[...] = mx + jnp.log(den)

    row = pl.BlockSpec((tr, W), lambda i: (i, 0))
    flat = [a for p in parts for a in p]
    return pl.pallas_call(
        body, name=name, grid=(S // tr,),
        in_specs=[row] * len(flat), out_specs=[row, row],
        out_shape=[jax.ShapeDtypeStruct((S, W), F32)] * 2,
        compiler_params=_cparams("parallel"),
    )(*flat)


def _sigmoid(x):
    return 1.0 / (1.0 + jnp.exp(-x))


def _tri(n, lower):
    r = lax.broadcasted_iota(jnp.int32, (n, n), 0)
    c = lax.broadcasted_iota(jnp.int32, (n, n), 1)
    return ((c <= r) if lower else (c >= r)).astype(F32)


def _dot_exact(a, b, trans_a=False, trans_b=False):
    dims = (((0,) if trans_a else (1,), (1,) if trans_b else (0,)), ((), ()))
    return lax.dot_general(a, b, dims, preferred_element_type=F32, precision=lax.Precision.HIGHEST)


def _hgrn_gates(qb, fb, lb):
    sq = _sigmoid(qb)
    q = qb * sq * (B_DIM ** -0.5)
    sf = _sigmoid(fb)
    gate = lb + (1.0 - lb) * sf
    return q, 1.0 - gate, gate, sq, sf


def _hgrn_intra_fwd(q_ref, k_ref, b_ref):
    C, n_sub = B_CHUNK, B_CHUNK // B_SUB
    b_all, k_all = b_ref[...], k_ref[...]
    srow = lax.broadcasted_iota(jnp.int32, (C, LANES), 0)
    lane = lax.broadcasted_iota(jnp.int32, (B_SUB, C), 1)
    trow = lax.broadcasted_iota(jnp.int32, (B_SUB, LANES), 0)
    blocks = []
    for i in range(n_sub):
        r0 = i * B_SUB
        qi, bi = q_ref[pl.ds(r0, B_SUB), :], b_ref[pl.ds(r0, B_SUB), :]
        if i == 0:
            a_i = jnp.zeros((B_SUB, C), F32)
        else:
            ref_b = b_ref[pl.ds(r0 - 1, 1), :]
            qt = (qi * jnp.exp(bi - ref_b)).astype(BF16)
            kt = jnp.where(srow < r0, k_all * jnp.exp(jnp.minimum(ref_b - b_all, 0.0)), 0.0).astype(BF16)
            a_i = _dot_nt(qt, kt)
        for sl in range(B_SUB):
            s = r0 + sl
            e = jnp.where(trow >= sl, jnp.exp(jnp.minimum(bi - b_ref[pl.ds(s, 1), :], 0.0)), 0.0)
            colv = jnp.sum(qi * k_ref[pl.ds(s, 1), :] * e, axis=-1, keepdims=True)
            a_i = a_i + jnp.where(lane == s, colv, 0.0)
        blocks.append(a_i)
    return jnp.concatenate(blocks, axis=0)


def hgrn_fwd(src, lb, norm_g, *, S, n_heads, hb, q_blk, f_blk, i_blk, g_blk, name):
    nc = S // B_CHUNK
    assert n_heads % hb == 0 and (q_blk % hb, f_blk % hb, i_blk % hb, g_blk % hb) == (0, 0, 0, 0)

    def body(qb_ref, fb_ref, ib_ref, gb_ref, lb_ref, ng_ref, out_ref, o_ref, st_ref, state_ref, q_s, k_s, b_s):
        state_ref[...] = jnp.zeros_like(state_ref)
        tril = _tri(B_CHUNK, True)

        def step(c, carry):
            rows = pl.ds(pl.multiple_of(c * B_CHUNK, B_CHUNK), B_CHUNK)
            for j in range(hb):
                cols = slice(j * LANES, (j + 1) * LANES)
                q, k, gate, _, _ = _hgrn_gates(qb_ref[rows, cols], fb_ref[rows, cols], lb_ref[j])
                b = _dot_exact(tril, jnp.log(gate))
                vf = ib_ref[rows, cols]
                v = vf.astype(BF16)
                q_s[j], k_s[j], b_s[j] = q, k, b
                st = state_ref[j]
                st_ref[j, c] = st
                a = _hgrn_intra_fwd(q_s.at[j], k_s.at[j], b_s.at[j])
                o = _dot(a.astype(BF16), v) + _dot_nt((q * jnp.exp(b)).astype(BF16), st.astype(BF16))
                b_last = b_s[j, pl.ds(B_CHUNK - 1, 1), :]
                state_ref[j] = st * jnp.exp(b_last) + _dot_exact(vf, k * jnp.exp(b_last - b), trans_a=True)
                o_ref[rows, cols] = o
                rstd = lax.rsqrt(jnp.mean(o * o, axis=-1, keepdims=True) + NORM_EPS)
                gb = gb_ref[rows, cols]
                out_ref[rows, cols] = o * rstd * ng_ref[...] * (gb * _sigmoid(gb))
            return carry

        lax.fori_loop(0, nc, step, 0)

    def col(off):
        return pl.BlockSpec((S, hb * LANES), lambda g: (0, off // hb + g))

    hv = pl.BlockSpec((hb, 1, LANES), lambda g: (g, 0, 0))
    return pl.pallas_call(
        body, name=name, grid=(n_heads // hb,),
        in_specs=[col(q_blk), col(f_blk), col(i_blk), col(g_blk), hv, pl.BlockSpec((1, LANES), lambda g: (0, 0))],
        out_specs=[col(0), col(0), pl.BlockSpec((hb, nc, LANES, LANES), lambda g: (g, 0, 0, 0))],
        out_shape=[jax.ShapeDtypeStruct((S, n_heads * LANES), F32), jax.ShapeDtypeStruct((S, n_heads * LANES), F32),
                   jax.ShapeDtypeStruct((n_heads, nc, LANES, LANES), F32)],
        scratch_shapes=[pltpu.VMEM((hb, LANES, LANES), F32)] + [pltpu.VMEM((hb, B_CHUNK, LANES), F32)] * 3,
        compiler_params=_cparams("parallel"),
    )(src, src, src, src, lb, norm_g.reshape(1, LANES))


def hgrn_bwd(src, lb, norm_g, o, states, dout, *, S, n_heads, hb, q_blk, f_blk, i_blk, g_blk, dout_blk, name):
    nc = S // B_CHUNK
    C, n_sub = B_CHUNK, B_CHUNK // B_SUB
    assert n_heads % hb == 0 and (q_blk % hb, f_blk % hb, i_blk % hb, g_blk % hb, dout_blk % hb) == (0, 0, 0, 0, 0)

    def body(qb_ref, fb_ref, ib_ref, gb_ref, lb_ref, ng_ref, o_ref, st_ref, dout_ref,
             dqb_ref, dfb_ref, dib_ref, dgb_ref, dlb_ref, dng_ref, dstate_ref, q_sh, k_sh, b_sh, dq_sh, dk_sh):
        @pl.when(pl.program_id(0) == 0)
        def _():
            dng_ref[...] = jnp.zeros_like(dng_ref)

        dstate_ref[...] = jnp.zeros_like(dstate_ref)
        tril = _tri(C, True)
        triu = _tri(C, False)
        ngv = ng_ref[...]
        srow = lax.broadcasted_iota(jnp.int32, (C, LANES), 0)
        lane = lax.broadcasted_iota(jnp.int32, (B_SUB, C), 1)
        trow = lax.broadcasted_iota(jnp.int32, (B_SUB, LANES), 0)
        causal = lax.broadcasted_iota(jnp.int32, (C, C), 1) <= lax.broadcasted_iota(jnp.int32, (C, C), 0)

        def one_head(j, c, rows, carry):
            dlog_carry, dlb_acc, dng_acc = carry
            cols = slice(j * LANES, (j + 1) * LANES)
            q_s, k_s, b_s, dq_s, dk_s = q_sh.at[j], k_sh.at[j], b_sh.at[j], dq_sh.at[j], dk_sh.at[j]
            lbv = lb_ref[j]
            qb, fb, gb = qb_ref[rows, cols], fb_ref[rows, cols], gb_ref[rows, cols]
            q, k, gate, sq, sf = _hgrn_gates(qb, fb, lbv)
            b = _dot_exact(tril, jnp.log(gate))
            vf = ib_ref[rows, cols]
            v = vf.astype(BF16)
            q_s[...], k_s[...], b_s[...] = q, k, b
            st = st_ref[j, c]
            dst = dstate_ref[j]

            ov = o_ref[rows, cols]
            dout = dout_ref[rows, cols]
            rstd = lax.rsqrt(jnp.mean(ov * ov, axis=-1, keepdims=True) + NORM_EPS)
            xhat = ov * rstd
            sg = _sigmoid(gb)
            dy = dout * (gb * sg)
            dgb_ref[rows, cols] = dout * (xhat * ngv) * (sg * (1.0 + gb * (1.0 - sg)))
            dng_acc = dng_acc + jnp.sum(dy * xhat, axis=0, keepdims=True)
            dxhat = dy * ngv
            do = rstd * (dxhat - xhat * jnp.mean(dxhat * xhat, axis=-1, keepdims=True))
            dob = do.astype(BF16)

            eb = jnp.exp(b)
            b_last = b_s[pl.ds(C - 1, 1), :]
            ebl = jnp.exp(b_last - b)
            a = _hgrn_intra_fwd(q_s, k_s, b_s)
            da = jnp.where(causal, _dot_exact(do, vf, trans_b=True), 0.0)
            dv = _dot_tn(a.astype(BF16), dob) + _dot_nt((k * ebl).astype(BF16), dst.astype(BF16))
            dq_s[...] = _dot_exact(do, st) * eb
            dk_s[...] = _dot_exact(vf, dst) * ebl
            dstate_ref[j] = dst * jnp.exp(b_last) + _dot_exact(do, q * eb, trans_a=True)
            b_all, k_all = b, k
            for i in range(n_sub):
                r0 = i * B_SUB
                blk = pl.ds(r0, B_SUB)
                qi, bi = q_s[blk, :], b_s[blk, :]
                da_i = da[r0:r0 + B_SUB, :]
                dq_i = jnp.zeros((B_SUB, LANES), F32)
                if i > 0:
                    ref_b = b_s[pl.ds(r0 - 1, 1), :]
                    eq = jnp.exp(bi - ref_b)
                    ek = jnp.where(srow < r0, jnp.exp(jnp.minimum(ref_b - b_all, 0.0)), 0.0)
                    da_off = jnp.where(lane < r0, da_i, 0.0)
                    dq_i = _dot_exact(da_off, k_all * ek) * eq
                    dk_s[...] += _dot_exact(da_off, qi * eq, trans_a=True) * ek
                for sl in range(B_SUB):
                    s = r0 + sl
                    e = jnp.where(trow >= sl, jnp.exp(jnp.minimum(bi - b_s[pl.ds(s, 1), :], 0.0)), 0.0)
                    dac = jnp.sum(jnp.where(lane == s, da_i, 0.0), axis=-1, keepdims=True)
                    dq_i = dq_i + dac * k_s[pl.ds(s, 1), :] * e
                    dk_s[pl.ds(s, 1), :] += jnp.sum(dac * qi * e, axis=0, keepdims=True)
                dq_s[blk, :] += dq_i
            dq, dk = dq_s[...], dk_s[...]
            db = q * dq - k * dk
            dlog = _dot_exact(triu, db) + dlog_carry
            dlog_carry = dlog_carry + jnp.sum(db, axis=0, keepdims=True)
            dgate = dlog / gate - dk
            dqb_ref[rows, cols] = dq * (B_DIM ** -0.5) * (sq * (1.0 + qb * (1.0 - sq)))
            dfb_ref[rows, cols] = dgate * (1.0 - lbv) * sf * (1.0 - sf)
            dib_ref[rows, cols] = dv
            dlb_acc = dlb_acc + jnp.sum(dgate * (1.0 - sf), axis=0, keepdims=True)
            return dlog_carry, dlb_acc, dng_acc

        def step(ci, carry):
            c = nc - 1 - ci
            rows = pl.ds(pl.multiple_of(c * C, C), C)
            return tuple(one_head(j, c, rows, carry[j]) for j in range(hb))

        z = jnp.zeros((1, LANES), F32)
        final = lax.fori_loop(0, nc, step, ((z, z, z),) * hb)
        for j in range(hb):
            dlb_ref[j] = final[j][1]
            dng_ref[...] += final[j][2]

    def col(off):
        return pl.BlockSpec((S, hb * LANES), lambda g: (0, off // hb + g), pipeline_mode=pl.Buffered(1))

    hv = pl.BlockSpec((hb, 1, LANES), lambda g: (g, 0, 0))
    vec = pl.BlockSpec((1, LANES), lambda g: (0, 0))
    full = jax.ShapeDtypeStruct((S, n_heads * LANES), F32)
    return pl.pallas_call(
        body, name=name, grid=(n_heads // hb,),
        in_specs=[col(q_blk), col(f_blk), col(i_blk), col(g_blk), hv, vec, col(0),
                  pl.BlockSpec((hb, nc, LANES, LANES), lambda g: (g, 0, 0, 0), pipeline_mode=pl.Buffered(1)),
                  col(dout_blk)],
        out_specs=[pl.BlockSpec((S, hb * LANES), lambda g: (0, g))] * 4 + [hv, vec],
        out_shape=[full, full, full, full, jax.ShapeDtypeStruct((n_heads, 1, LANES), F32),
                   jax.ShapeDtypeStruct((1, LANES), F32)],
        scratch_shapes=[pltpu.VMEM((hb, LANES, LANES), F32)] + [pltpu.VMEM((hb, B_CHUNK, LANES), F32)] * 5,
        compiler_params=_cparams("arbitrary"),
    )(src, src, src, src, lb, norm_g.reshape(1, LANES), o, states, dout)


def _pad_heads_cols(w, n_heads):
    lead = w.shape[:-1]
    w = w.reshape(lead + (n_heads, C_DIM))
    w = jnp.pad(w, [(0, 0)] * len(lead) + [(0, 0), (0, LANES - C_DIM)])
    return w.reshape(lead + (n_heads * LANES,))


def _unpad_heads_cols(w, n_heads):
    lead = w.shape[:-1]
    return w.reshape(lead + (n_heads, LANES))[..., :C_DIM].reshape(lead + (n_heads * C_DIM,))


def _pad_heads_rows(w, n_heads):
    w = jnp.pad(w.reshape(n_heads, C_DIM, w.shape[1]), [(0, 0), (0, LANES - C_DIM), (0, 0)])
    return w.reshape(n_heads * LANES, w.shape[2])


def _unpad_heads_rows(w, n_heads):
    return w.reshape(n_heads, LANES, w.shape[1])[:, :C_DIM].reshape(n_heads * C_DIM, w.shape[1])


def _lower_bounds(lb_raw):
    lb_soft = jax.nn.softmax(lb_raw.astype(F32), axis=0)
    return jnp.cumsum(lb_soft, axis=0) - lb_soft[0:1]


def _relu2(u):
    r = jnp.maximum(u, 0.0)
    return r * r


def local_step(x, target, P, hooks):
    S, D = x.shape
    depth = P["norm_mix_g"].shape[0]
    HA = D // 2 // A_DIM
    HB = HA
    HQ = D // C_DIM
    HKV = HQ // C_GROUP
    even_in = 7 * HA * LANES
    tabs_a = rope_tables(S, A_DIM)
    tabs_c = rope_tables(S, C_DIM)
    half_a, half_c = tabs_a[3], tabs_c[3]
    lbounds = _lower_bounds(P["hgrn_lb_raw"])
    a_kw = dict(n_heads=HA, group=1, hb=A_HEADS_PER_STEP, q_blk=0, k_blk=HA, v_blk=2 * HA, blk_per_row=7 * HA,
                scale=A_DIM ** -0.5, half=half_a)
    b_kw = dict(S=S, n_heads=HB, hb=B_HEADS_PER_STEP, q_blk=3 * HA, f_blk=3 * HA + HB, i_blk=3 * HA + 2 * HB,
                g_blk=3 * HA + 3 * HB)
    c_kw = dict(L=S, dil=1, n_heads=HQ, group=C_GROUP, hb=C_GROUP, q_blk=0, k_blk=HQ, v_blk=HQ + HKV,
                blk_per_row=HQ + 2 * HKV, max_dist=C_WINDOW - 1, scale=C_DIM ** -0.5, half=half_c)

    def a_tabs(dil):
        return tuple(t.reshape(S // dil, dil * LANES) for t in tabs_a[:3])

    saved = []
    for layer in range(depth):
        Wl = dict(hooks["weights_in"](layer, x))
        sv = {"x0": x, "W": Wl}
        idx = layer // 2
        h = rmsnorm_fwd(x, P["norm_mix_g"][layer], name="norm_mix_fwd")
        sv["h"] = h
        if layer % 2 == 0:
            proj = matmul(h, Wl["even_w_in"], out_dtypes=(F32,), name="even_in_fwd")
            parts = []
            for window, dil in A_BRANCHES:
                L = S // dil
                res = banded_fwd(proj.reshape(L, dil * even_in), a_tabs(dil), L=L, dil=dil,
                                         max_dist=window // dil, normalize=False, name=f"dilated_fwd_d{dil}", **a_kw)
                parts.append(tuple(r.reshape(S, HA * LANES) for r in res))
            oa, lse = merge_branches(parts, name="dilated_merge")
            hooks["mid"](layer, oa)
            lb_e = lbounds[idx].reshape(HB, 1, LANES)
            ob, o_raw, states = hgrn_fwd(proj, lb_e, P["hgrn_norm_g"][idx], name="hgrn_fwd", **b_kw)
            Wl.update(hooks["weights_rest"](layer, ob))
            mixed = jnp.concatenate([oa, ob], axis=1).astype(BF16)
            sv.update(proj=proj, oa=oa, lse=lse, o_raw=o_raw, states=states, mixed=mixed, lb=lb_e)
            x = matmul(mixed, Wl["even_w_out"], extras=(x,), epilogue=lambda acc, r: (acc + r,),
                       out_dtypes=(F32,), name="even_out_fwd")
        else:
            proj = matmul(h, Wl["odd_w_qkv"], extras=(P["odd_b_qkv"][idx].reshape(1, -1),),
                          epilogue=lambda acc, b: (acc + b,), out_dtypes=(F32,), name="odd_qkv_fwd")
            sink = jnp.broadcast_to(P["odd_sinks"][idx].reshape(HQ, 1, 1), (HQ, 1, LANES))
            o, lse = banded_fwd(proj, tabs_c[:3], sink=sink, normalize=True, name="swa_fwd", **c_kw)
            hooks["mid"](layer, o)
            Wl.update(hooks["weights_rest"](layer, o))
            sv.update(proj=proj, o=o, lse=lse, sink=sink)
            x = matmul(o, Wl["odd_w_o"], extras=(P["odd_b_o"][idx].reshape(1, D), x),
                       epilogue=lambda acc, b, r: (acc + b + r,), out_dtypes=(F32,), name="odd_out_fwd")
        sv["x1"] = x
        h2 = rmsnorm_fwd(x, P["norm_mlp_g"][layer], dep=hooks["after_mixer"](layer, x), name="norm_mlp_fwd")
        u = matmul(h2, Wl["mlp_w1"], out_dtypes=(BF16,), name="mlp_up_fwd")
        x = matmul(u, Wl["mlp_w2"], a_fn=_relu2, extras=(x,), epilogue=lambda acc, r: (acc + r,),
                   out_dtypes=(F32,), name="mlp_down_fwd")
        sv.update(h2=h2, u=u)
        saved.append(sv)

    loss, dx, dgf = final_norm_loss(x, P["final_norm_g"], target, name="final_norm_loss")

    G = {k: [None] * depth for k in ("norm_mix_g", "norm_mlp_g")}
    for k in ("hgrn_lb", "hgrn_norm_g"):
        G[k] = [None] * ((depth + 1) // 2)
    for k in ("odd_w_qkv", "odd_b_qkv", "odd_sinks", "odd_w_o", "odd_b_o"):
        G[k] = [None] * (depth // 2)
    G["final_norm_g"] = dgf.reshape(D)
    B = {}

    def wgrad(a, d, key, idx, name, a_fn=None):
        M, N = a.shape[1], d.shape[1]
        into = (grad_buffer_shape((M, N), SHARDING[key]), SHARDING[key])
        B[key, idx] = matmul(a, d, trans_a=True, a_fn=a_fn, out_dtypes=(BF16,), into=into, name=name)

    dep = None
    for layer in reversed(range(depth)):
        sv = saved[layer]
        Wl = sv["W"]
        idx = layer // 2
        du = matmul(dx, Wl["mlp_w2"], trans_b=True, extras=(sv["u"],), dep=dep,
                    epilogue=lambda acc, u: (acc * (2.0 * jnp.maximum(u.astype(F32), 0.0)),),
                    out_dtypes=(BF16,), name="mlp_down_bwd_x")
        wgrad(sv["u"], dx, "mlp_w2", layer, "mlp_down_bwd_w", a_fn=_relu2)
        wgrad(sv["h2"], du, "mlp_w1", layer, "mlp_up_bwd_w")
        dh2 = matmul(du, Wl["mlp_w1"], trans_b=True, out_dtypes=(F32,), name="mlp_up_bwd_x")
        dx, dg = rmsnorm_bwd(sv["x1"], P["norm_mlp_g"][layer], dh2, dx, name="norm_mlp_bwd")
        G["norm_mlp_g"][layer] = dg.reshape(D)
        dep = hooks["grads_ready"]([("mlp_w2", layer), ("mlp_w1", layer)], B, dx)
        if layer % 2 == 0:
            dmixed = matmul(dx, Wl["even_w_out"], trans_b=True, dep=dep, out_dtypes=(F32,), name="even_out_bwd_x")
            wgrad(sv["mixed"], dx, "even_w_out", idx, "even_out_bwd_w")
            acc = None
            for window, dil in A_BRANCHES:
                L = S // dil
                rs = lambda a: a.reshape(L, -1)
                acc = banded_bwd(rs(sv["proj"]), a_tabs(dil), rs(sv["oa"]), rs(sv["lse"]), rs(dmixed),
                                         L=L, dil=dil, max_dist=window // dil, do_bpr=2 * HA,
                                         acc=None if acc is None else tuple(rs(a) for a in acc),
                                         name=f"dilated_bwd_d{dil}", **a_kw)
                acc = tuple(a.reshape(S, HA * LANES) for a in acc)
            dqb, dfb, dib, dgb, dlb, dng = hgrn_bwd(sv["proj"], sv["lb"], P["hgrn_norm_g"][idx], sv["o_raw"],
                                                    sv["states"], dmixed, dout_blk=HA, name="hgrn_bwd", **b_kw)
            G["hgrn_lb"][idx] = dlb.reshape(HB * LANES)
            G["hgrn_norm_g"][idx] = dng.reshape(LANES)
            dproj = jnp.concatenate(list(acc) + [dqb, dfb, dib, dgb], axis=1).astype(BF16)
            wgrad(sv["h"], dproj, "even_w_in", idx, "even_in_bwd_w")
            dh = matmul(dproj, Wl["even_w_in"], trans_b=True, out_dtypes=(F32,), name="even_in_bwd_x")
        else:
            do = matmul(dx, Wl["odd_w_o"], trans_b=True, dep=dep, out_dtypes=(F32,), name="odd_out_bwd_x")
            G["odd_b_o"][idx] = colsum(dx, name="odd_out_bwd_b").reshape(D)
            G["odd_w_o"][idx] = matmul(sv["o"], dx, trans_a=True, out_dtypes=(BF16,), name="odd_out_bwd_w")
            dq, dk, dv, dsink = banded_bwd(sv["proj"], tabs_c[:3], sv["o"], sv["lse"], do, sink=sv["sink"],
                                                   do_bpr=HQ, name="swa_bwd", **c_kw)
            G["odd_sinks"][idx] = dsink[:, 0, 0]
            dproj = jnp.concatenate([dq, dk, dv], axis=1)
            G["odd_b_qkv"][idx] = colsum(dproj, name="odd_qkv_bwd_b").reshape(-1)
            dproj = dproj.astype(BF16)
            G["odd_w_qkv"][idx] = matmul(sv["h"], dproj, trans_a=True, out_dtypes=(BF16,), name="odd_qkv_bwd_w")
            dh = matmul(dproj, Wl["odd_w_qkv"], trans_b=True, out_dtypes=(F32,), name="odd_qkv_bwd_x")
            HQ2 = HQ + 2 * HKV
            B["odd_w_qkv", idx] = full_to_buffer(_unpad_heads_cols(G["odd_w_qkv"][idx], HQ2), "cols")
            B["odd_w_o", idx] = full_to_buffer(_unpad_heads_rows(G["odd_w_o"][idx], HQ), "rows")
        dx, dg = rmsnorm_bwd(sv["x0"], P["norm_mix_g"][layer], dh, dx, name="norm_mix_bwd")
        G["norm_mix_g"][layer] = dg.reshape(D)
        mixer = ("even_w_out", "even_w_in") if layer % 2 == 0 else ("odd_w_o", "odd_w_qkv")
        dep = hooks["grads_ready"]([(k, idx) for k in mixer], B, dx)

    del G["odd_w_qkv"], G["odd_w_o"]
    grads = {k: jnp.stack(v) if isinstance(v, list) else v for k, v in G.items()}
    _, lb_vjp = jax.vjp(_lower_bounds, P["hgrn_lb_raw"])
    grads["hgrn_lb_raw"] = lb_vjp(grads.pop("hgrn_lb"))[0]
    grads["odd_b_qkv"] = _unpad_heads_cols(grads["odd_b_qkv"], HQ + 2 * HKV)
    return loss, dx, grads, B


def full_to_buffer(full, sharding):
    M, N = full.shape
    if sharding == "cols":
        return jnp.transpose(full.reshape(2, M // 2, 4, N // 4), (2, 0, 1, 3))
    return full.reshape(4, 2, M // 8, N)


MESH = pl.DeviceIdType.MESH
ANY = pl.BlockSpec(memory_space=pl.ANY)
SHARDING = {"even_w_in": "cols", "even_w_out": "rows", "odd_w_qkv": "cols", "odd_w_o": "rows",
            "mlp_w1": "cols", "mlp_w2": "rows"}
BIG = tuple(SHARDING)


def _mesh_pos():
    return lax.axis_index("x"), lax.axis_index("y"), lax.axis_index("c")


def _other_chips(x, y):
    return [(1 - x, y), (x, 1 - y), (1 - x, 1 - y)]


def cast_into_full(w, layer, sharding, s, *, name, tr=256):
    n, R, C = w.shape
    tr = _tile(R, tr)
    nr = R // tr

    def body(s_ref, w_ref, o_ref):
        o_ref[...] = w_ref[...].astype(BF16)

    if sharding == "cols":
        full, out_map = (R, 4 * C), (lambda i, s_ref: (i, s_ref[0]))
    else:
        full, out_map = (4 * R, C), (lambda i, s_ref: (s_ref[0] * nr + i, 0))
    grid_spec = pltpu.PrefetchScalarGridSpec(
        num_scalar_prefetch=1, grid=(nr,),
        in_specs=[pl.BlockSpec((None, tr, C), lambda i, s_ref: (layer, i, 0))],
        out_specs=pl.BlockSpec((tr, C), out_map))
    return pl.pallas_call(
        body, name=name, grid_spec=grid_spec, out_shape=jax.ShapeDtypeStruct(full, BF16),
        compiler_params=_cparams("parallel"),
    )(s.reshape(1).astype(jnp.int32), w)


HBM = pl.BlockSpec(memory_space=pltpu.HBM)
SEM = pl.BlockSpec(memory_space=pltpu.SEMAPHORE)
EFFECT = pltpu.SideEffectType.DATAFLOW_SIDE_EFFECTING


def _weight_window(ref, sharding, s, hf):
    M, N = ref.shape
    if sharding == "cols":
        rh, C = M // 2, N // 4
        return ref.at[pl.ds(hf * rh, rh), pl.ds(s * C, C)]
    rh = M // 8
    return ref.at[pl.ds((2 * s + hf) * rh, rh), :]


def _in_hbm(a):
    return pltpu.with_memory_space_constraint(a, pltpu.HBM)


def gather_start(groups, shardings, after, *, name):
    sizes = [len(g) for g in groups]
    flat = [m for g in groups for m in g]
    flat_sh = [sh for g in shardings for sh in g]
    T, ng = len(flat), len(groups)

    def body(*refs):
        sems = refs[T + 1:T + 1 + 6 * ng]
        thru = refs[T + 1 + 6 * ng:2 * T + 1 + 6 * ng]
        token = refs[2 * T + 1 + 6 * ng]
        x, y, c = _mesh_pos()
        pos = 0
        for g in range(ng):
            for t in range(sizes[g]):
                w = _weight_window(thru[pos], flat_sh[pos], 2 * x + y, c)
                for j, (cx, cy) in enumerate(_other_chips(x, y)):
                    pltpu.make_async_remote_copy(src_ref=w, dst_ref=w, send_sem=sems[6 * g + j],
                                                 recv_sem=sems[6 * g + 3 + j], device_id=(cx, cy, c),
                                                 device_id_type=MESH).start()
                pos += 1
        token[...] = jnp.zeros_like(token)

    outs = pl.pallas_call(
        body, name=name,
        in_specs=[HBM] * T + [ANY],
        out_specs=[SEM] * (6 * ng) + [HBM] * T + [pl.BlockSpec(memory_space=pltpu.VMEM)],
        out_shape=[pltpu.SemaphoreType.DMA(())] * (6 * ng) + [pltpu.HBM(m.shape, m.dtype) for m in flat]
        + [jax.ShapeDtypeStruct((8, LANES), F32)],
        input_output_aliases={t: 6 * ng + t for t in range(T)},
        compiler_params=pltpu.CompilerParams(has_side_effects=EFFECT),
    )(*[_in_hbm(m) for m in flat], after)
    res, pos = [], 6 * ng
    for g in range(ng):
        res.append((list(outs[6 * g:6 * g + 6]), list(outs[pos:pos + sizes[g]])))
        pos += sizes[g]
    return res, outs[-1]


def gather_forward(mats, shardings, sems, after, *, name):
    T = len(mats)

    def body(*refs):
        sems1 = refs[T:T + 6]
        sems2 = refs[T + 7:T + 13]
        thru = refs[T + 13:2 * T + 13]
        token = refs[2 * T + 13]
        x, y, c = _mesh_pos()
        chips = _other_chips(x, y)
        for t in range(T):
            own = _weight_window(thru[t], shardings[t], 2 * x + y, c)
            for j, (cx, cy) in enumerate(chips):
                landed = _weight_window(thru[t], shardings[t], 2 * cx + cy, c)
                first = pltpu.make_async_remote_copy(src_ref=own, dst_ref=landed, send_sem=sems1[j],
                                                     recv_sem=sems1[3 + j], device_id=(cx, cy, c),
                                                     device_id_type=MESH)
                first.wait_recv()
                first.wait_send()
        for t in range(T):
            for j, (cx, cy) in enumerate(chips):
                landed = _weight_window(thru[t], shardings[t], 2 * cx + cy, c)
                pltpu.make_async_remote_copy(src_ref=landed, dst_ref=landed, send_sem=sems2[j],
                                             recv_sem=sems2[3 + j], device_id=(x, y, 1 - c),
                                             device_id_type=MESH).start()
        token[...] = jnp.zeros_like(token)

    outs = pl.pallas_call(
        body, name=name,
        in_specs=[HBM] * T + [SEM] * 6 + [ANY],
        out_specs=[SEM] * 6 + [HBM] * T + [pl.BlockSpec(memory_space=pltpu.VMEM)],
        out_shape=[pltpu.SemaphoreType.DMA(())] * 6 + [pltpu.HBM(m.shape, m.dtype) for m in mats]
        + [jax.ShapeDtypeStruct((8, LANES), F32)],
        input_output_aliases={t: 6 + t for t in range(T)},
        compiler_params=pltpu.CompilerParams(has_side_effects=EFFECT),
    )(*mats, *sems, after)
    return list(outs[:6]), list(outs[6:6 + T]), outs[-1]


def gather_finish(mats, shardings, sems, after, *, name):
    T = len(mats)

    def body(*refs):
        sems2 = refs[T:T + 6]
        thru = refs[T + 7:2 * T + 7]
        x, y, c = _mesh_pos()
        for t in range(T):
            for j, (cx, cy) in enumerate(_other_chips(x, y)):
                sent = _weight_window(thru[t], shardings[t], 2 * cx + cy, c)
                other = _weight_window(thru[t], shardings[t], 2 * cx + cy, 1 - c)
                cp = pltpu.make_async_remote_copy(src_ref=sent, dst_ref=other, send_sem=sems2[j],
                                                  recv_sem=sems2[3 + j], device_id=(x, y, 1 - c),
                                                  device_id_type=MESH)
                cp.wait_recv()
                cp.wait_send()

    outs = pl.pallas_call(
        body, name=name,
        in_specs=[HBM] * T + [SEM] * 6 + [ANY],
        out_specs=[HBM] * T,
        out_shape=[pltpu.HBM(m.shape, m.dtype) for m in mats],
        input_output_aliases={t: t for t in range(T)},
        compiler_params=pltpu.CompilerParams(has_side_effects=EFFECT),
    )(*mats, *sems, after)
    return list(outs)


def exchange_halves(bufs, *, name):
    T = len(bufs)

    def body(*refs):
        in_refs, out_refs = refs[:T], refs[T:2 * T]
        send_sems, recv_sems = refs[2 * T:]
        x, y, c = _mesh_pos()
        cps = [pltpu.make_async_remote_copy(src_ref=in_refs[t].at[:, pl.ds(1 - c, 1)], dst_ref=out_refs[t],
                                            send_sem=send_sems.at[t], recv_sem=recv_sems.at[t],
                                            device_id=(x, y, 1 - c), device_id_type=MESH) for t in range(T)]
        for cp in cps:
            cp.start()
        for cp in cps:
            cp.wait()

    return pl.pallas_call(
        body, name=name, in_specs=[ANY] * T, out_specs=[ANY] * T,
        out_shape=[jax.ShapeDtypeStruct((4, 1) + b.shape[2:], b.dtype) for b in bufs],
        scratch_shapes=[pltpu.SemaphoreType.DMA((T,)), pltpu.SemaphoreType.DMA((T,))],
    )(*bufs)


def pair_sum(buf, got, c, *, name, tr=512):
    _, _, n, rh, C = buf.shape
    tr = _tile(rh, tr)

    def body(c_ref, a_ref, b_ref, o_ref):
        o_ref[...] = (a_ref[...].astype(F32) + b_ref[...].astype(F32)).astype(o_ref.dtype)

    grid_spec = pltpu.PrefetchScalarGridSpec(
        num_scalar_prefetch=1, grid=(4, n, rh // tr),
        in_specs=[pl.BlockSpec((None, None, None, tr, C), lambda s, l, i, c_ref: (s, c_ref[0], l, i, 0)),
                  pl.BlockSpec((None, None, None, tr, C), lambda s, l, i, c_ref: (s, 0, l, i, 0))],
        out_specs=pl.BlockSpec((None, None, tr, C), lambda s, l, i, c_ref: (s, l, i, 0)))
    return pl.pallas_call(
        body, name=name, grid_spec=grid_spec, out_shape=jax.ShapeDtypeStruct((4, n, rh, C), BF16),
        compiler_params=_cparams("parallel", "parallel", "parallel"),
    )(c.reshape(1).astype(jnp.int32), buf, got)


def scatter_to_owners(sums, *, name):
    T = len(sums)

    def body(*refs):
        in_refs, out_refs = refs[:T], refs[T:2 * T]
        send_sems, recv_sems = refs[2 * T:]
        x, y, c = _mesh_pos()
        chips = _other_chips(x, y)
        cps = [pltpu.make_async_remote_copy(src_ref=in_refs[t].at[pl.ds(2 * cx + cy, 1)],
                                            dst_ref=out_refs[t].at[pl.ds(j, 1)],
                                            send_sem=send_sems.at[t, j], recv_sem=recv_sems.at[t, j],
                                            device_id=(cx, cy, c), device_id_type=MESH)
               for t in range(T) for j, (cx, cy) in enumerate(chips)]
        for cp in cps:
            cp.start()
        for cp in cps:
            cp.wait()

    return pl.pallas_call(
        body, name=name, in_specs=[ANY] * T, out_specs=[ANY] * T,
        out_shape=[jax.ShapeDtypeStruct((3,) + s.shape[1:], s.dtype) for s in sums],
        scratch_shapes=[pltpu.SemaphoreType.DMA((T, 3)), pltpu.SemaphoreType.DMA((T, 3))],
    )(*sums)


def scatter_start(sums, *, name):
    T = len(sums)
    lands = [_in_hbm(lax.empty((3,) + s.shape[1:], s.dtype)) for s in sums]

    def body(*refs):
        sems = refs[2 * T:2 * T + 6]
        src, dst = refs[2 * T + 6:3 * T + 6], refs[3 * T + 6:4 * T + 6]
        token = refs[4 * T + 6]
        x, y, c = _mesh_pos()
        for t in range(T):
            for j, (cx, cy) in enumerate(_other_chips(x, y)):
                pltpu.make_async_remote_copy(src_ref=src[t].at[pl.ds(2 * cx + cy, 1)], dst_ref=dst[t].at[pl.ds(j, 1)],
                                             send_sem=sems[j], recv_sem=sems[3 + j], device_id=(cx, cy, c),
                                             device_id_type=MESH).start()
        token[...] = jnp.zeros_like(token)

    outs = pl.pallas_call(
        body, name=name,
        in_specs=[HBM] * (2 * T),
        out_specs=[SEM] * 6 + [HBM] * (2 * T) + [pl.BlockSpec(memory_space=pltpu.VMEM)],
        out_shape=[pltpu.SemaphoreType.DMA(())] * 6 + [pltpu.HBM(a.shape, a.dtype) for a in list(sums) + lands]
        + [jax.ShapeDtypeStruct((8, LANES), F32)],
        input_output_aliases={t: 6 + t for t in range(2 * T)},
        compiler_params=pltpu.CompilerParams(has_side_effects=EFFECT),
    )(*[_in_hbm(s) for s in sums], *lands)
    return list(outs[:6]), list(outs[6:6 + T]), list(outs[6 + T:6 + 2 * T]), outs[-1]


def scatter_wait(sums, lands, sems, after, *, name):
    T = len(sums)

    def body(*refs):
        sem_refs = refs[2 * T:2 * T + 6]
        src, dst = refs[2 * T + 7:3 * T + 7], refs[3 * T + 7:4 * T + 7]
        x, y, c = _mesh_pos()
        for t in range(T):
            for j, (cx, cy) in enumerate(_other_chips(x, y)):
                cp = pltpu.make_async_remote_copy(src_ref=src[t].at[pl.ds(2 * cx + cy, 1)],
                                                  dst_ref=dst[t].at[pl.ds(j, 1)], send_sem=sem_refs[j],
                                                  recv_sem=sem_refs[3 + j], device_id=(cx, cy, c),
                                                  device_id_type=MESH)
                cp.wait_recv()
                cp.wait_send()

    outs = pl.pallas_call(
        body, name=name,
        in_specs=[HBM] * (2 * T) + [SEM] * 6 + [ANY],
        out_specs=[HBM] * (2 * T),
        out_shape=[pltpu.HBM(a.shape, a.dtype) for a in list(sums) + list(lands)],
        input_output_aliases={t: t for t in range(2 * T)},
        compiler_params=pltpu.CompilerParams(has_side_effects=EFFECT),
    )(*sums, *lands, *sems, after)
    return list(outs[:T]), list(outs[T:])


def owner_sum(sums, got, s, c, grad, layer, *, name, tr=512):
    rh, C = sums.shape[2:]
    tr = _tile(rh, tr)
    nr = rh // tr

    def body(sc_ref, a_ref, b0_ref, b1_ref, b2_ref, g_ref, o_ref):
        o_ref[...] = ((a_ref[...].astype(F32) + b0_ref[...].astype(F32)) + b1_ref[...].astype(F32)) \
            + b2_ref[...].astype(F32)

    def got_spec(j):
        return pl.BlockSpec((None, None, tr, C), lambda i, sc_ref: (j, 0, i, 0))

    grid_spec = pltpu.PrefetchScalarGridSpec(
        num_scalar_prefetch=1, grid=(nr,),
        in_specs=[pl.BlockSpec((None, None, tr, C), lambda i, sc_ref: (sc_ref[0], 0, i, 0)),
                  got_spec(0), got_spec(1), got_spec(2), ANY],
        out_specs=pl.BlockSpec((None, tr, C), lambda i, sc_ref: (layer, sc_ref[1] * nr + i, 0)))
    return pl.pallas_call(
        body, name=name, grid_spec=grid_spec, out_shape=jax.ShapeDtypeStruct(grad.shape, F32),
        input_output_aliases={5: 0},
        compiler_params=_cparams("parallel"),
    )(jnp.stack([s, c]).astype(jnp.int32), sums, got, got, got, grad)


def join_halves(grads, *, name):
    T = len(grads)

    def body(*refs):
        out_refs = refs[T:2 * T]
        send_sems, recv_sems = refs[2 * T:]
        x, y, c = _mesh_pos()

        def win(t, hf):
            rh = grads[t].shape[1] // 2
            return out_refs[t].at[:, pl.ds(hf * rh, rh), :]

        def remote(t, w):
            return pltpu.make_async_remote_copy(src_ref=w, dst_ref=w, send_sem=send_sems.at[t],
                                                recv_sem=recv_sems.at[t], device_id=(x, y, 1 - c),
                                                device_id_type=MESH)

        cps = [remote(t, win(t, c)) for t in range(T)]
        for cp in cps:
            cp.start()
        for t in range(T):
            remote(t, win(t, 1 - c)).wait_recv()
        for cp in cps:
            cp.wait_send()

    return pl.pallas_call(
        body, name=name, in_specs=[ANY] * T, out_specs=[ANY] * T,
        out_shape=[jax.ShapeDtypeStruct(g.shape, g.dtype) for g in grads],
        input_output_aliases={t: t for t in range(T)},
        scratch_shapes=[pltpu.SemaphoreType.DMA((T,)), pltpu.SemaphoreType.DMA((T,))],
    )(*grads)


def allreduce_small(v, *, name):
    rows = v.shape[0]

    def body(v_ref, out_ref, buf, send_sems, recv_sems):
        x, y, c = _mesh_pos()
        me = 4 * x + 2 * y + c
        flips = [(dx, dy, dc) for dx in (0, 1) for dy in (0, 1) for dc in (0, 1)][1:]

        def peer(f):
            return tuple(1 - p if d else p for d, p in zip(f, (x, y, c)))

        cps = []
        for k, f in enumerate(flips):
            px, py, pc = peer(f)
            cps.append(pltpu.make_async_remote_copy(src_ref=v_ref, dst_ref=buf.at[me], send_sem=send_sems.at[k],
                                                    recv_sem=recv_sems.at[k], device_id=(px, py, pc),
                                                    device_id_type=MESH))
        for cp in cps:
            cp.start()
        buf[me] = v_ref[...]
        for k, f in enumerate(flips):
            px, py, pc = peer(f)
            slot = buf.at[4 * px + 2 * py + pc]
            pltpu.make_async_remote_copy(src_ref=slot, dst_ref=slot, send_sem=send_sems.at[k],
                                         recv_sem=recv_sems.at[k], device_id=(px, py, pc),
                                         device_id_type=MESH).wait_recv()
        for cp in cps:
            cp.wait_send()
        acc = buf[0]
        for i in range(1, 8):
            acc = acc + buf[i]
        out_ref[...] = acc

    vm = pl.BlockSpec(memory_space=pltpu.VMEM)
    return pl.pallas_call(
        body, name=name, in_specs=[vm], out_specs=vm,
        out_shape=jax.ShapeDtypeStruct(v.shape, F32),
        scratch_shapes=[pltpu.VMEM((8, rows, LANES), F32), pltpu.SemaphoreType.DMA((7,)),
                        pltpu.SemaphoreType.DMA((7,))],
    )(v)


def _adam_math(w, m, v, g):
    m = ADAM_B1 * m + (1.0 - ADAM_B1) * g
    v = ADAM_B2 * v + (1.0 - ADAM_B2) * (g * g)
    m_hat = m / (1.0 - ADAM_B1 ** ADAM_STEP)
    v_hat = v / (1.0 - ADAM_B2 ** ADAM_STEP)
    delta = -ADAM_LR * (m_hat / (jnp.sqrt(v_hat) + ADAM_EPS) + ADAM_WD * w)
    return delta, m, v


def adamw(w, m, v, g, *, name, tr=256):
    def body(w_ref, m_ref, v_ref, g_ref, d_ref, nm_ref, nv_ref):
        d, nm, nv = _adam_math(w_ref[...], m_ref[...], v_ref[...], g_ref[...])
        d_ref[...] = d
        nm_ref[...] = nm
        nv_ref[...] = nv

    shape = jax.ShapeDtypeStruct(w.shape, F32)
    if w.ndim == 2:
        return pl.pallas_call(body, name=name, out_shape=[shape] * 3)(w, m, v, g)
    n, R, C = w.shape
    tr = _tile(R, tr)
    spec = pl.BlockSpec((None, tr, C), lambda l, i: (l, i, 0))
    return pl.pallas_call(
        body, name=name, grid=(n, R // tr), in_specs=[spec] * 4, out_specs=[spec] * 3, out_shape=[shape] * 3,
        compiler_params=_cparams("parallel", "parallel"),
    )(w, m, v, g)


def _pack(parts):
    flat = jnp.concatenate([p.reshape(-1).astype(F32) for p in parts])
    size = -(-flat.shape[0] // (8 * LANES)) * (8 * LANES)
    return jnp.pad(flat, (0, size - flat.shape[0])).reshape(size // LANES, LANES)


def _unpack(block, shapes):
    flat = block.reshape(-1)
    out, pos = [], 0
    for shp in shapes:
        size = int(np.prod(shp))
        out.append(flat[pos:pos + size].reshape(shp))
        pos += size
    return out


SMALL = ("norm_mix_g", "norm_mlp_g", "final_norm_g", "hgrn_lb_raw", "hgrn_norm_g", "odd_sinks")
WEIGHTS = ("norm_mix_g", "norm_mlp_g", "final_norm_g", "even_w_in", "even_w_out", "hgrn_lb_raw", "hgrn_norm_g",
           "odd_w_qkv", "odd_b_qkv", "odd_sinks", "odd_w_o", "odd_b_o", "mlp_w1", "mlp_w2")


def kernel(x, norm_mix_g, norm_mlp_g, final_norm_g, even_w_in, even_w_out, hgrn_lb_raw, hgrn_norm_g, odd_w_qkv, odd_b_qkv, odd_sinks, odd_w_o, odd_b_o, mlp_w1, mlp_w2, loss_target, m_norm_mix_g, m_norm_mlp_g, m_final_norm_g, m_even_w_in, m_even_w_out, m_hgrn_lb_raw, m_hgrn_norm_g, m_odd_w_qkv, m_odd_b_qkv, m_odd_sinks, m_odd_w_o, m_odd_b_o, m_mlp_w1, m_mlp_w2, v_norm_mix_g, v_norm_mlp_g, v_final_norm_g, v_even_w_in, v_even_w_out, v_hgrn_lb_raw, v_hgrn_norm_g, v_odd_w_qkv, v_odd_b_qkv, v_odd_sinks, v_odd_w_o, v_odd_b_o, v_mlp_w1, v_mlp_w2):
    args = locals()
    W = {k: args[k] for k in WEIGHTS}
    M = {k: args["m_" + k] for k in WEIGHTS}
    V = {k: args["v_" + k] for k in WEIGHTS}
    _, S, D = x.shape
    depth = norm_mix_g.shape[0]
    HQ = D // C_DIM
    HKV = HQ // C_GROUP
    xi, yi, ci = _mesh_pos()
    shard = 2 * xi + yi

    n_odd, bq = odd_b_qkv.shape
    bo = odd_b_o.shape[1]
    keep = (ci == 0).astype(F32)
    pieces = [lax.dynamic_update_slice(jnp.zeros((n_odd, 4 * bq), F32), odd_b_qkv * keep, (0, shard * bq)),
              lax.dynamic_update_slice(jnp.zeros((n_odd, 4 * bo), F32), odd_b_o * keep, (0, shard * bo))]
    biases = allreduce_small(_pack(pieces), name="gather_biases")
    b_qkv_full, b_o_full = _unpack(biases, [(n_odd, 4 * bq), (n_odd, 4 * bo)])
    P = {k: W[k] for k in SMALL}
    P.update(odd_b_qkv=_pad_heads_cols(b_qkv_full, HQ + 2 * HKV), odd_b_o=b_o_full)

    def layer_keys(layer):
        mixer = ("even_w_in", "even_w_out") if layer % 2 == 0 else ("odd_w_qkv", "odd_w_o")
        return [(k, layer // 2) for k in mixer] + [("mlp_w1", layer), ("mlp_w2", layer)]

    keys = [layer_keys(0)[:1], layer_keys(0)[1:]] + [layer_keys(layer) for layer in range(1, depth)]
    shardings = [[SHARDING[k] for k, _ in ks] for ks in keys]
    started, dep0 = gather_start(
        [[cast_into_full(W[k], i, SHARDING[k], shard, name="cast_" + k) for k, i in ks] for ks in keys],
        shardings, biases, name="gather_start")
    forwarded = {}

    def forward_group(g, after):
        sems, mats = started[g]
        forwarded[g] = gather_forward(mats, shardings[g], sems, after, name="gather_forward")
        return forwarded[g][2]

    def finish_group(g, after):
        sems, mats, _ = forwarded[g]
        mats = gather_finish(mats, shardings[g], sems, after, name="gather_finish")
        Wl = {k: m for (k, _), m in zip(keys[g], mats)}
        if "odd_w_qkv" in Wl:
            Wl["odd_w_qkv"] = _pad_heads_cols(Wl["odd_w_qkv"], HQ + 2 * HKV)
            Wl["odd_w_o"] = _pad_heads_rows(Wl["odd_w_o"], HQ)
        return Wl

    def weights_in(layer, x_in):
        if layer == 0:
            forward_group(0, dep0)
        return finish_group(layer + 1 if layer else 0, x_in)

    def mid(layer, a):
        if layer == 0:
            forward_group(1, a)

    def weights_rest(layer, a):
        return finish_group(1, a) if layer == 0 else {}

    def after_mixer(layer, x1):
        return forward_group(layer + 2, x1) if layer + 1 < depth else None

    reduced = {k: lax.empty(W[k].shape, F32) for k in BIG}
    in_flight = []

    def land(after):
        for kis, (sems, sums, lands, _) in in_flight:
            sums, lands = scatter_wait(sums, lands, sems, after, name="scatter_wait")
            for (k, i), s, g in zip(kis, sums, lands):
                reduced[k] = owner_sum(s, g, shard, ci, reduced[k], i, name="owner_sum_" + k)
        in_flight.clear()

    def grads_ready(kis, B, after):
        land(after)
        bufs = [B[ki].reshape((4, 2, 1) + B[ki].shape[2:]) for ki in kis]
        got = exchange_halves(bufs, name="exchange_halves")
        sums = [pair_sum(b, g, ci, name="pair_sum_" + k) for (k, _), b, g in zip(kis, bufs, got)]
        in_flight.append((kis, scatter_start(sums, name="scatter_start")))
        return in_flight[-1][1][3]

    hooks = dict(weights_in=weights_in, mid=mid, weights_rest=weights_rest, after_mixer=after_mixer,
                 grads_ready=grads_ready)
    loss, dx, G, B = local_step(x[0], loss_target[0], P, hooks)
    land(dx)
    grads = dict(zip(BIG, join_halves([reduced[k] for k in BIG], name="join_halves")))

    small_keys = SMALL + ("odd_b_qkv", "odd_b_o")
    small_shapes = [G[k].shape for k in small_keys]
    small = _unpack(allreduce_small(_pack([G[k] for k in small_keys]), name="reduce_small"), small_shapes)
    grads.update(zip(small_keys, small))
    grads["odd_b_qkv"] = lax.dynamic_slice(grads["odd_b_qkv"], (0, shard * bq), (n_odd, bq))
    grads["odd_b_o"] = lax.dynamic_slice(grads["odd_b_o"], (0, shard * bo), (n_odd, bo))

    deltas, new_m, new_v = {}, {}, {}
    for k in WEIGHTS:
        w2 = (lambda a: a.reshape(1, -1)) if W[k].ndim == 1 else (lambda a: a)
        d, nm, nv = adamw(w2(W[k]), w2(M[k]), w2(V[k]), w2(grads[k]), name="adamw_" + k)
        deltas[k], new_m[k], new_v[k] = (a.reshape(W[k].shape) for a in (d, nm, nv))

    loss = lax.psum(loss[0, 0], ("x", "y", "c"))
    return (loss, dx[None], *[grads[k] for k in WEIGHTS], *[deltas[k] for k in WEIGHTS],
            *[new_m[k] for k in WEIGHTS], *[new_v[k] for k in WEIGHTS])
```

```python
import functools
import math

import jax
import jax.numpy as jnp
import numpy as np
from jax import lax
from jax.experimental import pallas as pl
from jax.experimental.pallas import tpu as pltpu

F32 = jnp.float32
BF16 = jnp.bfloat16

NORM_EPS = 1e-5
ROPE_THETA = 500000.0
ROPE_FRACTION = 4
LANES = 128
BAND = 128
A_DIM = 128
A_BRANCHES = ((128, 1), (512, 4), (2048, 16))
B_DIM = 128
B_CHUNK = 64
B_SUB = 16
A_HEADS_PER_STEP = 2
B_HEADS_PER_STEP = 2
C_DIM = 64
C_GROUP = 8
C_WINDOW = 128

ADAM_LR = 0.001
ADAM_B1 = 0.9
ADAM_B2 = 0.999
ADAM_EPS = 1e-08
ADAM_WD = 0.01
ADAM_STEP = 10

VMEM_LIMIT = 56 * 1024 * 1024


def _cparams(*sem):
    return pltpu.CompilerParams(dimension_semantics=tuple(sem), vmem_limit_bytes=VMEM_LIMIT)


def _tile(n, want):
    if n <= want:
        return n
    t = want - want % LANES
    while n % t:
        t -= LANES
    assert t > 0, (n, want)
    return t


def matmul(a, b, *, trans_a=False, trans_b=False, out_dtypes, a_fn=None, epilogue=None, extras=(), name,
           b_cols=None, into=None, dep=None, tm=1024, tn=1024, tk=2048):
    if trans_a:
        K, M = a.shape
    else:
        M, K = a.shape
    if trans_b:
        N, K2 = b.shape
    else:
        K2, N = b.shape
    assert K == K2, (a.shape, b.shape)
    col0 = 0
    if b_cols is not None:
        assert not trans_b
        col0, N = b_cols[0], b_cols[1] - b_cols[0]
    if into is not None:
        rh, cs = into[0][2:]
        tm, tn = _tile(rh, tm), _tile(cs, tn)
    tm, tn, tk = _tile(M, tm), _tile(N, tn), _tile(K, tk)
    nk = K // tk
    n_extra = len(extras)
    n_out = len(out_dtypes)

    def body(*refs):
        a_ref, b_ref = refs[0], refs[1]
        extra_refs = refs[2:2 + n_extra]
        out_refs = refs[-1 - n_out:-1]
        acc_ref = refs[-1]
        k = pl.program_id(2)
        at = a_ref[...]
        if a_fn is not None:
            at = a_fn(at.astype(F32))
        at = at.astype(BF16)
        bt = b_ref[...].astype(BF16)
        dims = (((0,) if trans_a else (1,), (1,) if trans_b else (0,)), ((), ()))
        prod = lax.dot_general(at, bt, dims, preferred_element_type=F32)

        def finish(acc):
            ex = [r[...] for r in extra_refs]
            outs = epilogue(acc, *ex) if epilogue is not None else (acc,)
            for o_ref, o in zip(out_refs, outs):
                o_ref[...] = o.astype(o_ref.dtype)

        if nk == 1:
            finish(prod)
            return

        @pl.when(k == 0)
        def _():
            acc_ref[...] = prod

        @pl.when((k > 0) & (k < nk - 1))
        def _():
            acc_ref[...] += prod

        @pl.when(k == nk - 1)
        def _():
            finish(acc_ref[...] + prod)

    a_spec = pl.BlockSpec((tk, tm), lambda i, j, k: (k, i)) if trans_a else pl.BlockSpec((tm, tk), lambda i, j, k: (i, k))
    assert col0 % tn == 0, (col0, tn)
    b_spec = (pl.BlockSpec((tn, tk), lambda i, j, k: (j, k)) if trans_b
              else pl.BlockSpec((tk, tn), lambda i, j, k: (k, col0 // tn + j)))
    e_specs = []
    for e in extras:
        if e.shape == (1, N):
            e_specs.append(pl.BlockSpec((1, tn), lambda i, j, k: (0, j)))
        else:
            assert e.shape == (M, N), (e.shape, M, N)
            e_specs.append(pl.BlockSpec((tm, tn), lambda i, j, k: (i, j)))
    args = [a, b, *extras]
    in_specs = [a_spec, b_spec] + e_specs
    if dep is not None:
        args.append(dep)
        in_specs.append(pl.BlockSpec(dep.shape, lambda i, j, k: (0, 0)))
    if into is None:
        out_specs = [pl.BlockSpec((tm, tn), lambda i, j, k: (i, j)) for _ in out_dtypes]
        out_shape = [jax.ShapeDtypeStruct((M, N), dt) for dt in out_dtypes]
    else:
        buf_shape, sharding = into
        assert n_out == 1
        index = grad_buffer_index(buf_shape, (M, N), sharding, tm, tn)
        out_specs = [pl.BlockSpec((None, None, tm, tn), lambda i, j, k: index(i, j))]
        out_shape = [jax.ShapeDtypeStruct(buf_shape, out_dtypes[0])]
    outs = pl.pallas_call(
        body, name=name,
        grid=(M // tm, N // tn, nk),
        in_specs=in_specs, out_specs=out_specs, out_shape=out_shape,
        scratch_shapes=[pltpu.VMEM((tm, tn), F32)],
        compiler_params=_cparams("parallel", "parallel", "arbitrary"),
    )(*args)
    return outs[0] if n_out == 1 else tuple(outs)


def grad_buffer_shape(mat_shape, sharding):
    M, N = mat_shape
    return (4, 2, M // 2, N // 4) if sharding == "cols" else (4, 2, M // 8, N)


def grad_buffer_index(buf_shape, mat_shape, sharding, tm, tn):
    _, _, rh, cs = buf_shape
    M, N = mat_shape
    ib, jb = rh // tm, cs // tn
    if sharding == "cols":
        assert (M, N) == (2 * rh, 4 * cs), (buf_shape, mat_shape)
        return lambda i, j: (j // jb, i // ib, i % ib, j % jb)
    assert sharding == "rows" and (M, N) == (8 * rh, cs), (buf_shape, mat_shape)
    return lambda i, j: (i // (2 * ib), (i // ib) % 2, i % ib, j)


def rmsnorm_fwd(x, g, *, name, dep=None, tr=256):
    S, D = x.shape
    tr = _tile(S, tr)

    def body(x_ref, g_ref, *rest):
        h_ref = rest[-1]
        xv = x_ref[...]
        rstd = lax.rsqrt(jnp.mean(xv * xv, axis=-1, keepdims=True) + NORM_EPS)
        h_ref[...] = (xv * rstd * g_ref[...]).astype(h_ref.dtype)

    args = [x, g.reshape(1, D)]
    in_specs = [pl.BlockSpec((tr, D), lambda i: (i, 0)), pl.BlockSpec((1, D), lambda i: (0, 0))]
    if dep is not None:
        args.append(dep)
        in_specs.append(pl.BlockSpec(dep.shape, lambda i: (0, 0)))
    return pl.pallas_call(
        body, name=name, grid=(S // tr,),
        in_specs=in_specs,
        out_specs=pl.BlockSpec((tr, D), lambda i: (i, 0)),
        out_shape=jax.ShapeDtypeStruct((S, D), BF16),
        compiler_params=_cparams("parallel"),
    )(*args)


def _rms_bwd_rows(xv, gv, dh):
    rstd = lax.rsqrt(jnp.mean(xv * xv, axis=-1, keepdims=True) + NORM_EPS)
    xhat = xv * rstd
    dxhat = dh * gv
    dx = rstd * (dxhat - xhat * jnp.mean(dxhat * xhat, axis=-1, keepdims=True))
    return dx, dh * xhat


def rmsnorm_bwd(x, g, dh, dres, *, name, tr=256):
    S, D = x.shape
    tr = _tile(S, tr)

    def body(x_ref, g_ref, dh_ref, dres_ref, dx_ref, dg_ref):
        @pl.when(pl.program_id(0) == 0)
        def _():
            dg_ref[...] = jnp.zeros_like(dg_ref)

        dx, dgr = _rms_bwd_rows(x_ref[...], g_ref[...], dh_ref[...].astype(F32))
        dx_ref[...] = dres_ref[...] + dx
        dg_ref[...] += jnp.sum(dgr, axis=0, keepdims=True)

    row = pl.BlockSpec((tr, D), lambda i: (i, 0))
    vec = pl.BlockSpec((1, D), lambda i: (0, 0))
    return pl.pallas_call(
        body, name=name, grid=(S // tr,),
        in_specs=[row, vec, row, row],
        out_specs=[row, vec],
        out_shape=[jax.ShapeDtypeStruct((S, D), F32), jax.ShapeDtypeStruct((1, D), F32)],
        compiler_params=_cparams("arbitrary"),
    )(x, g.reshape(1, D), dh, dres)


def final_norm_loss(x, g, target, *, name, tr=256):
    S, D = x.shape
    tr = _tile(S, tr)

    def body(x_ref, g_ref, t_ref, loss_ref, dx_ref, dg_ref):
        @pl.when(pl.program_id(0) == 0)
        def _():
            dg_ref[...] = jnp.zeros_like(dg_ref)
            loss_ref[...] = jnp.zeros_like(loss_ref)

        xv, gv = x_ref[...], g_ref[...]
        rstd = lax.rsqrt(jnp.mean(xv * xv, axis=-1, keepdims=True) + NORM_EPS)
        err = xv * rstd * gv - t_ref[...]
        part = 0.5 * jnp.sum(jnp.mean(err * err, axis=-1, keepdims=True), axis=0, keepdims=True)
        loss_ref[...] += jnp.broadcast_to(part, loss_ref.shape)
        dx, dgr = _rms_bwd_rows(xv, gv, err * (1.0 / D))
        dx_ref[...] = dx
        dg_ref[...] += jnp.sum(dgr, axis=0, keepdims=True)

    row = pl.BlockSpec((tr, D), lambda i: (i, 0))
    vec = pl.BlockSpec((1, D), lambda i: (0, 0))
    return pl.pallas_call(
        body, name=name, grid=(S // tr,),
        in_specs=[row, vec, row],
        out_specs=[pl.BlockSpec((8, LANES), lambda i: (0, 0)), row, vec],
        out_shape=[jax.ShapeDtypeStruct((8, LANES), F32), jax.ShapeDtypeStruct((S, D), F32),
                   jax.ShapeDtypeStruct((1, D), F32)],
        compiler_params=_cparams("arbitrary"),
    )(x, g.reshape(1, D), target)


def colsum(a, *, name, tr=256):
    S, N = a.shape
    tr = _tile(S, tr)

    def body(a_ref, o_ref):
        @pl.when(pl.program_id(0) == 0)
        def _():
            o_ref[...] = jnp.zeros_like(o_ref)

        o_ref[...] += jnp.sum(a_ref[...].astype(F32), axis=0, keepdims=True)

    return pl.pallas_call(
        body, name=name, grid=(S // tr,),
        in_specs=[pl.BlockSpec((tr, N), lambda i: (i, 0))],
        out_specs=pl.BlockSpec((1, N), lambda i: (0, 0)),
        out_shape=jax.ShapeDtypeStruct((1, N), F32),
        compiler_params=_cparams("arbitrary"),
    )(a)


def rope_tables(seq, head_dim):
    rot = head_dim // ROPE_FRACTION
    half = rot // 2
    inv_freq = 1.0 / (ROPE_THETA ** (jnp.arange(0, rot, 2, dtype=F32) / rot))
    ang = jnp.arange(seq, dtype=F32)[:, None] * inv_freq[None, :]
    cos, sin = jnp.cos(ang), jnp.sin(ang)
    ones = jnp.ones((seq, LANES - 2 * half), F32)
    zeros = jnp.zeros((seq, LANES - 2 * half), F32)
    zh = jnp.zeros((seq, half), F32)
    c = jnp.concatenate([cos, cos, ones], axis=1)
    sa = jnp.concatenate([-sin, zh, zeros], axis=1)
    sb = jnp.concatenate([zh, sin, zeros], axis=1)
    return c, sa, sb, half


def _rope(x, c, sa, sb, half):
    return x * c + pltpu.roll(x, LANES - half, 1) * sa + pltpu.roll(x, half, 1) * sb


def _rope_bwd(d, c, sa, sb, half):
    return d * c + pltpu.roll(d * sa, half, 1) + pltpu.roll(d * sb, LANES - half, 1)


def _band_masks(n, max_dist):
    qi = lax.broadcasted_iota(jnp.int32, (BAND, BAND), 0)
    kj = lax.broadcasted_iota(jnp.int32, (BAND, BAND), 1)
    cur = kj <= qi
    prev = ((kj >= qi) if max_dist == BAND else (kj > qi)) & (n > 0)
    return prev, cur


def _dot_nt(a, b):
    return lax.dot_general(a, b, (((1,), (1,)), ((), ())), preferred_element_type=F32)


def _dot_tn(a, b):
    return lax.dot_general(a, b, (((0,), (0,)), ((), ())), preferred_element_type=F32)


def _dot(a, b):
    return jnp.dot(a, b, preferred_element_type=F32)


def band_attention_fwd(src, tabs, *, L, dil, n_heads, group, q_blk, k_blk, v_blk, blk_per_row, max_dist, scale,
                       half, sink=None, normalize, name):
    c_t, sa_t, sb_t = tabs
    nb = L // BAND
    W = dil * n_heads * LANES
    use_sink = sink is not None

    def body(*refs):
        q_ref, k_ref, v_ref, c_ref, sa_ref, sb_ref = refs[:6]
        pos = 6
        if use_sink:
            sink_ref = refs[pos]
            pos += 1
        o_ref, m_ref, l_ref = refs[pos:pos + 3] if not normalize else (refs[pos], refs[pos + 1], None)
        kr_ref, vb_ref = refs[-2], refs[-1]

        def prep(n, carry):
            rows = pl.ds(pl.multiple_of(n * BAND, BAND), BAND)
            kr_ref[rows, :] = _rope(k_ref[rows, :], c_ref[rows, :], sa_ref[rows, :], sb_ref[rows, :], half).astype(BF16)
            vb_ref[rows, :] = v_ref[rows, :].astype(BF16)
            return carry

        lax.fori_loop(0, nb, prep, 0)

        def step(n, carry):
            rows = pl.ds(pl.multiple_of(n * BAND, BAND), BAND)
            prow = pl.ds(pl.multiple_of(jnp.maximum(n - 1, 0) * BAND, BAND), BAND)
            q = _rope(q_ref[rows, :], c_ref[rows, :], sa_ref[rows, :], sb_ref[rows, :], half).astype(BF16)
            mp, mc = _band_masks(n, max_dist)
            sp = jnp.where(mp, _dot_nt(q, kr_ref[prow, :]) * scale, -jnp.inf)
            sc = jnp.where(mc, _dot_nt(q, kr_ref[rows, :]) * scale, -jnp.inf)
            m = jnp.maximum(jnp.max(sp, axis=-1, keepdims=True), jnp.max(sc, axis=-1, keepdims=True))
            if use_sink:
                sk = sink_ref[:, 0:1]
                m = jnp.maximum(m, sk)
            pp = jnp.exp(sp - m)
            pc = jnp.exp(sc - m)
            l = jnp.sum(pp, axis=-1, keepdims=True) + jnp.sum(pc, axis=-1, keepdims=True)
            if use_sink:
                l = l + jnp.exp(sk - m)
            num = _dot(pp.astype(BF16), vb_ref[prow, :]) + _dot(pc.astype(BF16), vb_ref[rows, :])
            if normalize:
                o_ref[rows, :] = num / l
                m_ref[rows, :] = jnp.broadcast_to(m + jnp.log(l), (BAND, LANES))
            else:
                o_ref[rows, :] = num
                m_ref[rows, :] = jnp.broadcast_to(m, (BAND, LANES))
                l_ref[rows, :] = jnp.broadcast_to(l, (BAND, LANES))
            return carry

        lax.fori_loop(0, nb, step, 0)

    def col(off, div):
        return pl.BlockSpec((L, LANES), lambda r, h: (0, r * blk_per_row + off + h // div))

    tab = pl.BlockSpec((L, LANES), lambda r, h: (0, r))
    out = pl.BlockSpec((L, LANES), lambda r, h: (0, r * n_heads + h))
    in_specs = [col(q_blk, 1), col(k_blk, group), col(v_blk, group), tab, tab, tab]
    args = [src, src, src, c_t, sa_t, sb_t]
    if use_sink:
        in_specs.append(pl.BlockSpec((None, 1, LANES), lambda r, h: (h, 0, 0)))
        args.append(sink)
    n_out = 2 if normalize else 3
    return pl.pallas_call(
        body, name=name, grid=(dil, n_heads),
        in_specs=in_specs,
        out_specs=[out] * n_out,
        out_shape=[jax.ShapeDtypeStruct((L, W), F32)] * n_out,
        scratch_shapes=[pltpu.VMEM((L, LANES), BF16), pltpu.VMEM((L, LANES), BF16)],
        compiler_params=_cparams("parallel", "arbitrary"),
    )(*args)


def band_attention_bwd(src, tabs, o, lse, do, *, L, dil, n_heads, group, q_blk, k_blk, v_blk, blk_per_row, max_dist,
                       scale, half, do_bpr, sink=None, acc=None, name):
    c_t, sa_t, sb_t = tabs
    nb = L // BAND
    n_kv = n_heads // group
    use_sink = sink is not None
    use_acc = acc is not None
    assert not (use_acc and group != 1)

    def body(*refs):
        q_ref, k_ref, v_ref, c_ref, sa_ref, sb_ref, o_ref, lse_ref, do_ref = refs[:9]
        pos = 9
        if use_sink:
            sink_ref = refs[pos]
            pos += 1
        if use_acc:
            aq_ref, ak_ref, av_ref = refs[pos:pos + 3]
            pos += 3
        dq_ref, dk_ref, dv_ref = refs[pos:pos + 3]
        pos += 3
        if use_sink:
            dsink_ref = refs[pos]
        kr_ref, vb_ref, dka_ref, dva_ref = refs[-4:]
        h = pl.program_id(1)

        def prep(n, carry):
            rows = pl.ds(pl.multiple_of(n * BAND, BAND), BAND)
            kr_ref[rows, :] = _rope(k_ref[rows, :], c_ref[rows, :], sa_ref[rows, :], sb_ref[rows, :], half).astype(BF16)
            vb_ref[rows, :] = v_ref[rows, :].astype(BF16)
            return carry

        lax.fori_loop(0, nb, prep, 0)

        @pl.when(h % group == 0)
        def _():
            dka_ref[...] = jnp.zeros_like(dka_ref)
            dva_ref[...] = jnp.zeros_like(dva_ref)

        def step(n, dsk):
            rows = pl.ds(pl.multiple_of(n * BAND, BAND), BAND)
            prow = pl.ds(pl.multiple_of(jnp.maximum(n - 1, 0) * BAND, BAND), BAND)
            cv, sav, sbv = c_ref[rows, :], sa_ref[rows, :], sb_ref[rows, :]
            q = _rope(q_ref[rows, :], cv, sav, sbv, half).astype(BF16)
            dov = do_ref[rows, :]
            lsev = lse_ref[rows, 0:1]
            delta = jnp.sum(dov * o_ref[rows, :], axis=-1, keepdims=True)
            dob = dov.astype(BF16)
            mp, mc = _band_masks(n, max_dist)
            kp, kc, vp, vc = kr_ref[prow, :], kr_ref[rows, :], vb_ref[prow, :], vb_ref[rows, :]
            pp = jnp.exp(jnp.where(mp, _dot_nt(q, kp) * scale, -jnp.inf) - lsev)
            pc = jnp.exp(jnp.where(mc, _dot_nt(q, kc) * scale, -jnp.inf) - lsev)
            dsp = (pp * (_dot_nt(dob, vp) - delta) * scale).astype(BF16)
            dsc = (pc * (_dot_nt(dob, vc) - delta) * scale).astype(BF16)
            dq = _rope_bwd(_dot(dsp, kp) + _dot(dsc, kc), cv, sav, sbv, half)
            if use_acc:
                dq = dq + aq_ref[rows, :]
            dq_ref[rows, :] = dq
            dka_ref[prow, :] += _dot_tn(dsp, q)
            dka_ref[rows, :] += _dot_tn(dsc, q)
            dva_ref[prow, :] += _dot_tn(pp.astype(BF16), dob)
            dva_ref[rows, :] += _dot_tn(pc.astype(BF16), dob)
            if use_sink:
                dsk = dsk - jnp.sum(jnp.exp(sink_ref[:, 0:1] - lsev) * delta, axis=0, keepdims=True)
            return dsk

        dsk = lax.fori_loop(0, nb, step, jnp.zeros((1, 1), F32))
        if use_sink:
            dsink_ref[...] = jnp.broadcast_to(dsk, dsink_ref.shape)

        @pl.when(h % group == group - 1)
        def _():
            def fin(n, carry):
                rows = pl.ds(pl.multiple_of(n * BAND, BAND), BAND)
                dk = _rope_bwd(dka_ref[rows, :], c_ref[rows, :], sa_ref[rows, :], sb_ref[rows, :], half)
                dv = dva_ref[rows, :]
                if use_acc:
                    dk = dk + ak_ref[rows, :]
                    dv = dv + av_ref[rows, :]
                dk_ref[rows, :] = dk
                dv_ref[rows, :] = dv
                return carry

            lax.fori_loop(0, nb, fin, 0)

    def col(off, div):
        return pl.BlockSpec((L, LANES), lambda r, h: (0, r * blk_per_row + off + h // div))

    tab = pl.BlockSpec((L, LANES), lambda r, h: (0, r))
    qo = pl.BlockSpec((L, LANES), lambda r, h: (0, r * n_heads + h))
    kvo = pl.BlockSpec((L, LANES), lambda r, h: (0, r * n_kv + h // group))
    dospec = pl.BlockSpec((L, LANES), lambda r, h: (0, r * do_bpr + h))
    in_specs = [col(q_blk, 1), col(k_blk, group), col(v_blk, group), tab, tab, tab, qo, qo, dospec]
    args = [src, src, src, c_t, sa_t, sb_t, o, lse, do]
    if use_sink:
        in_specs.append(pl.BlockSpec((None, 1, LANES), lambda r, h: (h, 0, 0)))
        args.append(sink)
    if use_acc:
        in_specs += [qo, kvo, kvo]
        args += list(acc)
    out_specs = [qo, kvo, kvo]
    out_shape = [jax.ShapeDtypeStruct((L, dil * n_heads * LANES), F32),
                 jax.ShapeDtypeStruct((L, dil * n_kv * LANES), F32),
                 jax.ShapeDtypeStruct((L, dil * n_kv * LANES), F32)]
    if use_sink:
        out_specs.append(pl.BlockSpec((None, 1, LANES), lambda r, h: (h, 0, 0)))
        out_shape.append(jax.ShapeDtypeStruct((n_heads, 1, LANES), F32))
    return pl.pallas_call(
        body, name=name, grid=(dil, n_heads),
        in_specs=in_specs, out_specs=out_specs, out_shape=out_shape,
        scratch_shapes=[pltpu.VMEM((L, LANES), BF16), pltpu.VMEM((L, LANES), BF16),
                        pltpu.VMEM((L, LANES), F32), pltpu.VMEM((L, LANES), F32)],
        compiler_params=_cparams("parallel", "arbitrary"),
    )(*args)


def banded_fwd(src, tabs, *, L, dil, n_heads, group, hb, q_blk, k_blk, v_blk, blk_per_row, max_dist, scale, half,
               sink=None, normalize, name):
    c_t, sa_t, sb_t = tabs
    nb = L // BAND
    kvb = hb // group
    n_hg = n_heads // hb
    W = dil * n_heads * LANES
    use_sink = sink is not None
    assert n_heads % hb == 0 and hb % group == 0
    assert (q_blk % hb, k_blk % kvb, v_blk % kvb) == (0, 0, 0) and (dil == 1 or blk_per_row % hb == 0)

    def body(*refs):
        q_ref, k_ref, v_ref, c_ref, sa_ref, sb_ref = refs[:6]
        pos = 6
        if use_sink:
            sink_ref = refs[pos]
            pos += 1
        o_ref, m_ref = refs[pos], refs[pos + 1]
        l_ref = None if normalize else refs[pos + 2]
        kr_ref, vb_ref = refs[-2], refs[-1]
        n = pl.program_id(2)

        @pl.when(n == 0)
        def _():
            def prep(b, carry):
                rows = pl.ds(pl.multiple_of(b * BAND, BAND), BAND)
                cv, sav, sbv = c_ref[rows, :], sa_ref[rows, :], sb_ref[rows, :]
                for j in range(kvb):
                    cols = slice(j * LANES, (j + 1) * LANES)
                    kr_ref[rows, cols] = _rope(k_ref[rows, cols].astype(F32), cv, sav, sbv, half).astype(BF16)
                    vb_ref[rows, cols] = v_ref[rows, cols].astype(BF16)
                return carry

            lax.fori_loop(0, nb, prep, 0)

        rows = pl.ds(pl.multiple_of(n * BAND, BAND), BAND)
        prow = pl.ds(pl.multiple_of(jnp.maximum(n - 1, 0) * BAND, BAND), BAND)
        cv, sav, sbv = c_ref[rows, :], sa_ref[rows, :], sb_ref[rows, :]
        mp, mc = _band_masks(n, max_dist)
        for i in range(hb):
            qc = slice(i * LANES, (i + 1) * LANES)
            kc = slice((i // group) * LANES, (i // group + 1) * LANES)
            q = _rope(q_ref[:, qc].astype(F32), cv, sav, sbv, half).astype(BF16)
            sp = jnp.where(mp, _dot_nt(q, kr_ref[prow, kc]) * scale, -jnp.inf)
            sc = jnp.where(mc, _dot_nt(q, kr_ref[rows, kc]) * scale, -jnp.inf)
            m = jnp.maximum(jnp.max(sp, axis=-1, keepdims=True), jnp.max(sc, axis=-1, keepdims=True))
            if use_sink:
                sk = sink_ref[i, :, 0:1]
                m = jnp.maximum(m, sk)
            pp = jnp.exp(sp - m)
            pc = jnp.exp(sc - m)
            l = jnp.sum(pp, axis=-1, keepdims=True) + jnp.sum(pc, axis=-1, keepdims=True)
            if use_sink:
                l = l + jnp.exp(sk - m)
            num = _dot(pp.astype(BF16), vb_ref[prow, kc]) + _dot(pc.astype(BF16), vb_ref[rows, kc])
            if normalize:
                o_ref[:, qc] = num / l
                m_ref[:, qc] = jnp.broadcast_to(m + jnp.log(l), (BAND, LANES))
            else:
                o_ref[:, qc] = num
                m_ref[:, qc] = jnp.broadcast_to(m, (BAND, LANES))
                l_ref[:, qc] = jnp.broadcast_to(l, (BAND, LANES))

    qspec = pl.BlockSpec((BAND, hb * LANES), lambda r, g, n: (n, (r * blk_per_row + q_blk) // hb + g))

    def kv(off):
        return pl.BlockSpec((L, kvb * LANES), lambda r, g, n: (0, (r * blk_per_row + off) // kvb + g))

    tab = pl.BlockSpec((L, LANES), lambda r, g, n: (0, r))
    out = pl.BlockSpec((BAND, hb * LANES), lambda r, g, n: (n, r * n_hg + g))
    in_specs = [qspec, kv(k_blk), kv(v_blk), tab, tab, tab]
    args = [src, src, src, c_t, sa_t, sb_t]
    if use_sink:
        in_specs.append(pl.BlockSpec((hb, 1, LANES), lambda r, g, n: (g, 0, 0)))
        args.append(sink)
    n_out = 2 if normalize else 3
    return pl.pallas_call(
        body, name=name, grid=(dil, n_hg, nb),
        in_specs=in_specs,
        out_specs=[out] * n_out,
        out_shape=[jax.ShapeDtypeStruct((L, W), F32)] * n_out,
        scratch_shapes=[pltpu.VMEM((L, kvb * LANES), BF16), pltpu.VMEM((L, kvb * LANES), BF16)],
        compiler_params=_cparams("parallel", "parallel", "arbitrary"),
    )(*args)


def banded_bwd(src, tabs, o, lse, do, *, L, dil, n_heads, group, hb, q_blk, k_blk, v_blk, blk_per_row, max_dist,
               scale, half, do_bpr, sink=None, acc=None, name):
    c_t, sa_t, sb_t = tabs
    nb = L // BAND
    kvb = hb // group
    n_hg = n_heads // hb
    n_kv = n_heads // group
    use_sink = sink is not None
    use_acc = acc is not None
    assert n_heads % hb == 0 and hb % group == 0 and do_bpr % hb == 0
    assert (q_blk % hb, k_blk % kvb, v_blk % kvb) == (0, 0, 0) and (dil == 1 or blk_per_row % hb == 0)

    def body(*refs):
        q_ref, k_ref, v_ref, c_ref, sa_ref, sb_ref, o_ref, lse_ref, do_ref = refs[:9]
        pos = 9
        if use_sink:
            sink_ref = refs[pos]
            pos += 1
        if use_acc:
            aq_ref, ak_ref, av_ref = refs[pos:pos + 3]
            pos += 3
        dq_ref, dk_ref, dv_ref = refs[pos:pos + 3]
        pos += 3
        if use_sink:
            dsink_ref = refs[pos]
        kr_ref, vb_ref, dka_ref, dva_ref = refs[-4:]
        n = pl.program_id(2)

        @pl.when(n == 0)
        def _():
            def prep(b, carry):
                rows = pl.ds(pl.multiple_of(b * BAND, BAND), BAND)
                cv, sav, sbv = c_ref[rows, :], sa_ref[rows, :], sb_ref[rows, :]
                for j in range(kvb):
                    cols = slice(j * LANES, (j + 1) * LANES)
                    kr_ref[rows, cols] = _rope(k_ref[rows, cols].astype(F32), cv, sav, sbv, half).astype(BF16)
                    vb_ref[rows, cols] = v_ref[rows, cols].astype(BF16)
                return carry

            lax.fori_loop(0, nb, prep, 0)
            dka_ref[...] = jnp.zeros_like(dka_ref)
            dva_ref[...] = jnp.zeros_like(dva_ref)
            if use_sink:
                dsink_ref[...] = jnp.zeros_like(dsink_ref)

        rows = pl.ds(pl.multiple_of(n * BAND, BAND), BAND)
        prow = pl.ds(pl.multiple_of(jnp.maximum(n - 1, 0) * BAND, BAND), BAND)
        cv, sav, sbv = c_ref[rows, :], sa_ref[rows, :], sb_ref[rows, :]
        mp, mc = _band_masks(n, max_dist)
        for i in range(hb):
            qc = slice(i * LANES, (i + 1) * LANES)
            kc = slice((i // group) * LANES, (i // group + 1) * LANES)
            q = _rope(q_ref[:, qc].astype(F32), cv, sav, sbv, half).astype(BF16)
            dov = do_ref[:, qc]
            lsev = lse_ref[:, i * LANES:i * LANES + 1]
            delta = jnp.sum(dov * o_ref[:, qc], axis=-1, keepdims=True)
            dob = dov.astype(BF16)
            kp, kcur, vp, vcur = kr_ref[prow, kc], kr_ref[rows, kc], vb_ref[prow, kc], vb_ref[rows, kc]
            pp = jnp.exp(jnp.where(mp, _dot_nt(q, kp) * scale, -jnp.inf) - lsev)
            pc = jnp.exp(jnp.where(mc, _dot_nt(q, kcur) * scale, -jnp.inf) - lsev)
            dsp = (pp * (_dot_nt(dob, vp) - delta) * scale).astype(BF16)
            dsc = (pc * (_dot_nt(dob, vcur) - delta) * scale).astype(BF16)
            dq = _rope_bwd(_dot(dsp, kp) + _dot(dsc, kcur), cv, sav, sbv, half)
            if use_acc:
                dq = dq + aq_ref[:, qc]
            dq_ref[:, qc] = dq
            dka_ref[prow, kc] += _dot_tn(dsp, q)
            dka_ref[rows, kc] += _dot_tn(dsc, q)
            dva_ref[prow, kc] += _dot_tn(pp.astype(BF16), dob)
            dva_ref[rows, kc] += _dot_tn(pc.astype(BF16), dob)
            if use_sink:
                dsk = jnp.sum(jnp.exp(sink_ref[i, :, 0:1] - lsev) * delta, axis=0, keepdims=True)
                dsink_ref[i] -= jnp.broadcast_to(dsk, (1, LANES))

        @pl.when(n == nb - 1)
        def _():
            def fin(b, carry):
                brow = pl.ds(pl.multiple_of(b * BAND, BAND), BAND)
                bc, bsa, bsb = c_ref[brow, :], sa_ref[brow, :], sb_ref[brow, :]
                for j in range(kvb):
                    cols = slice(j * LANES, (j + 1) * LANES)
                    dk = _rope_bwd(dka_ref[brow, cols], bc, bsa, bsb, half)
                    dv = dva_ref[brow, cols]
                    if use_acc:
                        dk = dk + ak_ref[brow, cols]
                        dv = dv + av_ref[brow, cols]
                    dk_ref[brow, cols] = dk
                    dv_ref[brow, cols] = dv
                return carry

            lax.fori_loop(0, nb, fin, 0)

    qspec = pl.BlockSpec((BAND, hb * LANES), lambda r, g, n: (n, (r * blk_per_row + q_blk) // hb + g))

    def kv(off):
        return pl.BlockSpec((L, kvb * LANES), lambda r, g, n: (0, (r * blk_per_row + off) // kvb + g))

    tab = pl.BlockSpec((L, LANES), lambda r, g, n: (0, r))
    qo = pl.BlockSpec((BAND, hb * LANES), lambda r, g, n: (n, r * n_hg + g))
    kvo = pl.BlockSpec((L, kvb * LANES), lambda r, g, n: (0, r * (n_kv // kvb) + g))
    dospec = pl.BlockSpec((BAND, hb * LANES), lambda r, g, n: (n, r * (do_bpr // hb) + g))
    in_specs = [qspec, kv(k_blk), kv(v_blk), tab, tab, tab, qo, qo, dospec]
    args = [src, src, src, c_t, sa_t, sb_t, o, lse, do]
    sink_spec = pl.BlockSpec((hb, 1, LANES), lambda r, g, n: (g, 0, 0))
    if use_sink:
        in_specs.append(sink_spec)
        args.append(sink)
    if use_acc:
        in_specs += [qo, kvo, kvo]
        args += list(acc)
    out_specs = [qo, kvo, kvo]
    out_shape = [jax.ShapeDtypeStruct((L, dil * n_heads * LANES), F32),
                 jax.ShapeDtypeStruct((L, dil * n_kv * LANES), F32),
                 jax.ShapeDtypeStruct((L, dil * n_kv * LANES), F32)]
    if use_sink:
        out_specs.append(sink_spec)
        out_shape.append(jax.ShapeDtypeStruct((n_heads, 1, LANES), F32))
    return pl.pallas_call(
        body, name=name, grid=(dil, n_hg, nb),
        in_specs=in_specs, out_specs=out_specs, out_shape=out_shape,
        scratch_shapes=[pltpu.VMEM((L, kvb * LANES), BF16), pltpu.VMEM((L, kvb * LANES), BF16),
                        pltpu.VMEM((L, kvb * LANES), F32), pltpu.VMEM((L, kvb * LANES), F32)],
        compiler_params=_cparams("parallel", "parallel", "arbitrary"),
    )(*args)


def merge_branches(parts, *, name, tr=256):
    S, W = parts[0][0].shape
    tr = _tile(S, tr)
    nbr = len(parts)

    def body(*refs):
        ins, (o_ref, lse_ref) = refs[:3 * nbr], refs[3 * nbr:]
        nums = [ins[3 * i][...] for i in range(nbr)]
        ms = [ins[3 * i + 1][...] for i in range(nbr)]
        ls = [ins[3 * i + 2][...] for i in range(nbr)]
        mx = functools.reduce(jnp.maximum, ms)
        ws = [jnp.exp(m - mx) for m in ms]
        num = sum(w * n for w, n in zip(ws, nums))
        den = sum(w * l for w, l in zip(ws, ls))
        o_ref[...] = num / den
        lse_ref[...] = mx + jnp.log(den)

    row = pl.BlockSpec((tr, W), lambda i: (i, 0))
    flat = [a for p in parts for a in p]
    return pl.pallas_call(
        body, name=name, grid=(S // tr,),
        in_specs=[row] * len(flat), out_specs=[row, row],
        out_shape=[jax.ShapeDtypeStruct((S, W), F32)] * 2,
        compiler_params=_cparams("parallel"),
    )(*flat)


def _sigmoid(x):
    return 1.0 / (1.0 + jnp.exp(-x))


def _tri(n, lower):
    r = lax.broadcasted_iota(jnp.int32, (n, n), 0)
    c = lax.broadcasted_iota(jnp.int32, (n, n), 1)
    return ((c <= r) if lower else (c >= r)).astype(F32)


def _dot_exact(a, b, trans_a=False, trans_b=False):
    dims = (((0,) if trans_a else (1,), (1,) if trans_b else (0,)), ((), ()))
    return lax.dot_general(a, b, dims, preferred_element_type=F32, precision=lax.Precision.HIGHEST)


def _hgrn_gates(qb, fb, lb):
    sq = _sigmoid(qb)
    q = qb * sq * (B_DIM ** -0.5)
    sf = _sigmoid(fb)
    gate = lb + (1.0 - lb) * sf
    return q, 1.0 - gate, gate, sq, sf


def _hgrn_intra_fwd(q_ref, k_ref, b_ref):
    C, n_sub = B_CHUNK, B_CHUNK // B_SUB
    b_all, k_all = b_ref[...], k_ref[...]
    srow = lax.broadcasted_iota(jnp.int32, (C, LANES), 0)
    lane = lax.broadcasted_iota(jnp.int32, (B_SUB, C), 1)
    trow = lax.broadcasted_iota(jnp.int32, (B_SUB, LANES), 0)
    blocks = []
    for i in range(n_sub):
        r0 = i * B_SUB
        qi, bi = q_ref[pl.ds(r0, B_SUB), :], b_ref[pl.ds(r0, B_SUB), :]
        if i == 0:
            a_i = jnp.zeros((B_SUB, C), F32)
        else:
            ref_b = b_ref[pl.ds(r0 - 1, 1), :]
            qt = (qi * jnp.exp(bi - ref_b)).astype(BF16)
            kt = jnp.where(srow < r0, k_all * jnp.exp(jnp.minimum(ref_b - b_all, 0.0)), 0.0).astype(BF16)
            a_i = _dot_nt(qt, kt)
        for sl in range(B_SUB):
            s = r0 + sl
            e = jnp.where(trow >= sl, jnp.exp(jnp.minimum(bi - b_ref[pl.ds(s, 1), :], 0.0)), 0.0)
            colv = jnp.sum(qi * k_ref[pl.ds(s, 1), :] * e, axis=-1, keepdims=True)
            a_i = a_i + jnp.where(lane == s, colv, 0.0)
        blocks.append(a_i)
    return jnp.concatenate(blocks, axis=0)


def hgrn_fwd(src, lb, norm_g, *, S, n_heads, hb, q_blk, f_blk, i_blk, g_blk, name):
    nc = S // B_CHUNK
    assert n_heads % hb == 0 and (q_blk % hb, f_blk % hb, i_blk % hb, g_blk % hb) == (0, 0, 0, 0)

    def body(qb_ref, fb_ref, ib_ref, gb_ref, lb_ref, ng_ref, out_ref, o_ref, st_ref, state_ref, q_s, k_s, b_s):
        state_ref[...] = jnp.zeros_like(state_ref)
        tril = _tri(B_CHUNK, True)

        def step(c, carry):
            rows = pl.ds(pl.multiple_of(c * B_CHUNK, B_CHUNK), B_CHUNK)
            for j in range(hb):
                cols = slice(j * LANES, (j + 1) * LANES)
                q, k, gate, _, _ = _hgrn_gates(qb_ref[rows, cols], fb_ref[rows, cols], lb_ref[j])
                b = _dot_exact(tril, jnp.log(gate))
                vf = ib_ref[rows, cols]
                v = vf.astype(BF16)
                q_s[j], k_s[j], b_s[j] = q, k, b
                st = state_ref[j]
                st_ref[j, c] = st
                a = _hgrn_intra_fwd(q_s.at[j], k_s.at[j], b_s.at[j])
                o = _dot(a.astype(BF16), v) + _dot_nt((q * jnp.exp(b)).astype(BF16), st.astype(BF16))
                b_last = b_s[j, pl.ds(B_CHUNK - 1, 1), :]
                state_ref[j] = st * jnp.exp(b_last) + _dot_exact(vf, k * jnp.exp(b_last - b), trans_a=True)
                o_ref[rows, cols] = o
                rstd = lax.rsqrt(jnp.mean(o * o, axis=-1, keepdims=True) + NORM_EPS)
                gb = gb_ref[rows, cols]
                out_ref[rows, cols] = o * rstd * ng_ref[...] * (gb * _sigmoid(gb))
            return carry

        lax.fori_loop(0, nc, step, 0)

    def col(off):
        return pl.BlockSpec((S, hb * LANES), lambda g: (0, off // hb + g))

    hv = pl.BlockSpec((hb, 1, LANES), lambda g: (g, 0, 0))
    return pl.pallas_call(
        body, name=name, grid=(n_heads // hb,),
        in_specs=[col(q_blk), col(f_blk), col(i_blk), col(g_blk), hv, pl.BlockSpec((1, LANES), lambda g: (0, 0))],
        out_specs=[col(0), col(0), pl.BlockSpec((hb, nc, LANES, LANES), lambda g: (g, 0, 0, 0))],
        out_shape=[jax.ShapeDtypeStruct((S, n_heads * LANES), F32), jax.ShapeDtypeStruct((S, n_heads * LANES), F32),
                   jax.ShapeDtypeStruct((n_heads, nc, LANES, LANES), F32)],
        scratch_shapes=[pltpu.VMEM((hb, LANES, LANES), F32)] + [pltpu.VMEM((hb, B_CHUNK, LANES), F32)] * 3,
        compiler_params=_cparams("parallel"),
    )(src, src, src, src, lb, norm_g.reshape(1, LANES))


def hgrn_bwd(src, lb, norm_g, o, states, dout, *, S, n_heads, hb, q_blk, f_blk, i_blk, g_blk, dout_blk, name):
    nc = S // B_CHUNK
    C, n_sub = B_CHUNK, B_CHUNK // B_SUB
    assert n_heads % hb == 0 and (q_blk % hb, f_blk % hb, i_blk % hb, g_blk % hb, dout_blk % hb) == (0, 0, 0, 0, 0)

    def body(qb_ref, fb_ref, ib_ref, gb_ref, lb_ref, ng_ref, o_ref, st_ref, dout_ref,
             dqb_ref, dfb_ref, dib_ref, dgb_ref, dlb_ref, dng_ref, dstate_ref, q_sh, k_sh, b_sh, dq_sh, dk_sh):
        @pl.when(pl.program_id(0) == 0)
        def _():
            dng_ref[...] = jnp.zeros_like(dng_ref)

        dstate_ref[...] = jnp.zeros_like(dstate_ref)
        tril = _tri(C, True)
        triu = _tri(C, False)
        ngv = ng_ref[...]
        srow = lax.broadcasted_iota(jnp.int32, (C, LANES), 0)
        lane = lax.broadcasted_iota(jnp.int32, (B_SUB, C), 1)
        trow = lax.broadcasted_iota(jnp.int32, (B_SUB, LANES), 0)
        causal = lax.broadcasted_iota(jnp.int32, (C, C), 1) <= lax.broadcasted_iota(jnp.int32, (C, C), 0)

        def one_head(j, c, rows, carry):
            dlog_carry, dlb_acc, dng_acc = carry
            cols = slice(j * LANES, (j + 1) * LANES)
            q_s, k_s, b_s, dq_s, dk_s = q_sh.at[j], k_sh.at[j], b_sh.at[j], dq_sh.at[j], dk_sh.at[j]
            lbv = lb_ref[j]
            qb, fb, gb = qb_ref[rows, cols], fb_ref[rows, cols], gb_ref[rows, cols]
            q, k, gate, sq, sf = _hgrn_gates(qb, fb, lbv)
            b = _dot_exact(tril, jnp.log(gate))
            vf = ib_ref[rows, cols]
            v = vf.astype(BF16)
            q_s[...], k_s[...], b_s[...] = q, k, b
            st = st_ref[j, c]
            dst = dstate_ref[j]

            ov = o_ref[rows, cols]
            dout = dout_ref[rows, cols]
            rstd = lax.rsqrt(jnp.mean(ov * ov, axis=-1, keepdims=True) + NORM_EPS)
            xhat = ov * rstd
            sg = _sigmoid(gb)
            dy = dout * (gb * sg)
            dgb_ref[rows, cols] = dout * (xhat * ngv) * (sg * (1.0 + gb * (1.0 - sg)))
            dng_acc = dng_acc + jnp.sum(dy * xhat, axis=0, keepdims=True)
            dxhat = dy * ngv
            do = rstd * (dxhat - xhat * jnp.mean(dxhat * xhat, axis=-1, keepdims=True))
            dob = do.astype(BF16)

            eb = jnp.exp(b)
            b_last = b_s[pl.ds(C - 1, 1), :]
            ebl = jnp.exp(b_last - b)
            a = _hgrn_intra_fwd(q_s, k_s, b_s)
            da = jnp.where(causal, _dot_exact(do, vf, trans_b=True), 0.0)
            dv = _dot_tn(a.astype(BF16), dob) + _dot_nt((k * ebl).astype(BF16), dst.astype(BF16))
            dq_s[...] = _dot_exact(do, st) * eb
            dk_s[...] = _dot_exact(vf, dst) * ebl
            dstate_ref[j] = dst * jnp.exp(b_last) + _dot_exact(do, q * eb, trans_a=True)
            b_all, k_all = b, k
            for i in range(n_sub):
                r0 = i * B_SUB
                blk = pl.ds(r0, B_SUB)
                qi, bi = q_s[blk, :], b_s[blk, :]
                da_i = da[r0:r0 + B_SUB, :]
                dq_i = jnp.zeros((B_SUB, LANES), F32)
                if i > 0:
                    ref_b = b_s[pl.ds(r0 - 1, 1), :]
                    eq = jnp.exp(bi - ref_b)
                    ek = jnp.where(srow < r0, jnp.exp(jnp.minimum(ref_b - b_all, 0.0)), 0.0)
                    da_off = jnp.where(lane < r0, da_i, 0.0)
                    dq_i = _dot_exact(da_off, k_all * ek) * eq
                    dk_s[...] += _dot_exact(da_off, qi * eq, trans_a=True) * ek
                for sl in range(B_SUB):
                    s = r0 + sl
                    e = jnp.where(trow >= sl, jnp.exp(jnp.minimum(bi - b_s[pl.ds(s, 1), :], 0.0)), 0.0)
                    dac = jnp.sum(jnp.where(lane == s, da_i, 0.0), axis=-1, keepdims=True)
                    dq_i = dq_i + dac * k_s[pl.ds(s, 1), :] * e
                    dk_s[pl.ds(s, 1), :] += jnp.sum(dac * qi * e, axis=0, keepdims=True)
                dq_s[blk, :] += dq_i
            dq, dk = dq_s[...], dk_s[...]
            db = q * dq - k * dk
            dlog = _dot_exact(triu, db) + dlog_carry
            dlog_carry = dlog_carry + jnp.sum(db, axis=0, keepdims=True)
            dgate = dlog / gate - dk
            dqb_ref[rows, cols] = dq * (B_DIM ** -0.5) * (sq * (1.0 + qb * (1.0 - sq)))
            dfb_ref[rows, cols] = dgate * (1.0 - lbv) * sf * (1.0 - sf)
            dib_ref[rows, cols] = dv
            dlb_acc = dlb_acc + jnp.sum(dgate * (1.0 - sf), axis=0, keepdims=True)
            return dlog_carry, dlb_acc, dng_acc

        def step(ci, carry):
            c = nc - 1 - ci
            rows = pl.ds(pl.multiple_of(c * C, C), C)
            return tuple(one_head(j, c, rows, carry[j]) for j in range(hb))

        z = jnp.zeros((1, LANES), F32)
        final = lax.fori_loop(0, nc, step, ((z, z, z),) * hb)
        for j in range(hb):
            dlb_ref[j] = final[j][1]
            dng_ref[...] += final[j][2]

    def col(off):
        return pl.BlockSpec((S, hb * LANES), lambda g: (0, off // hb + g), pipeline_mode=pl.Buffered(1))

    hv = pl.BlockSpec((hb, 1, LANES), lambda g: (g, 0, 0))
    vec = pl.BlockSpec((1, LANES), lambda g: (0, 0))
    full = jax.ShapeDtypeStruct((S, n_heads * LANES), F32)
    return pl.pallas_call(
        body, name=name, grid=(n_heads // hb,),
        in_specs=[col(q_blk), col(f_blk), col(i_blk), col(g_blk), hv, vec, col(0),
                  pl.BlockSpec((hb, nc, LANES, LANES), lambda g: (g, 0, 0, 0), pipeline_mode=pl.Buffered(1)),
                  col(dout_blk)],
        out_specs=[pl.BlockSpec((S, hb * LANES), lambda g: (0, g))] * 4 + [hv, vec],
        out_shape=[full, full, full, full, jax.ShapeDtypeStruct((n_heads, 1, LANES), F32),
                   jax.ShapeDtypeStruct((1, LANES), F32)],
        scratch_shapes=[pltpu.VMEM((hb, LANES, LANES), F32)] + [pltpu.VMEM((hb, B_CHUNK, LANES), F32)] * 5,
        compiler_params=_cparams("arbitrary"),
    )(src, src, src, src, lb, norm_g.reshape(1, LANES), o, states, dout)


def _pad_heads_cols(w, n_heads):
    lead = w.shape[:-1]
    w = w.reshape(lead + (n_heads, C_DIM))
    w = jnp.pad(w, [(0, 0)] * len(lead) + [(0, 0), (0, LANES - C_DIM)])
    return w.reshape(lead + (n_heads * LANES,))


def _unpad_heads_cols(w, n_heads):
    lead = w.shape[:-1]
    return w.reshape(lead + (n_heads, LANES))[..., :C_DIM].reshape(lead + (n_heads * C_DIM,))


def _pad_heads_rows(w, n_heads):
    w = jnp.pad(w.reshape(n_heads, C_DIM, w.shape[1]), [(0, 0), (0, LANES - C_DIM), (0, 0)])
    return w.reshape(n_heads * LANES, w.shape[2])


def _unpad_heads_rows(w, n_heads):
    return w.reshape(n_heads, LANES, w.shape[1])[:, :C_DIM].reshape(n_heads * C_DIM, w.shape[1])


def _lower_bounds(lb_raw):
    lb_soft = jax.nn.softmax(lb_raw.astype(F32), axis=0)
    return jnp.cumsum(lb_soft, axis=0) - lb_soft[0:1]


def _relu2(u):
    r = jnp.maximum(u, 0.0)
    return r * r


def local_step(x, target, P, hooks):
    S, D = x.shape
    depth = P["norm_mix_g"].shape[0]
    HA = D // 2 // A_DIM
    HB = HA
    HQ = D // C_DIM
    HKV = HQ // C_GROUP
    even_in = 7 * HA * LANES
    tabs_a = rope_tables(S, A_DIM)
    tabs_c = rope_tables(S, C_DIM)
    half_a, half_c = tabs_a[3], tabs_c[3]
    lbounds = _lower_bounds(P["hgrn_lb_raw"])
    a_in = 3 * HA * LANES
    a_kw = dict(n_heads=HA, group=1, hb=A_HEADS_PER_STEP, q_blk=0, k_blk=HA, v_blk=2 * HA, blk_per_row=3 * HA,
                scale=A_DIM ** -0.5, half=half_a)
    b_kw = dict(S=S, n_heads=HB, hb=B_HEADS_PER_STEP, q_blk=0, f_blk=HB, i_blk=2 * HB, g_blk=3 * HB)
    c_kw = dict(L=S, dil=1, n_heads=HQ, group=C_GROUP, hb=C_GROUP, q_blk=0, k_blk=HQ, v_blk=HQ + HKV,
                blk_per_row=HQ + 2 * HKV, max_dist=C_WINDOW - 1, scale=C_DIM ** -0.5, half=half_c)

    def a_tabs(dil):
        return tuple(t.reshape(S // dil, dil * LANES) for t in tabs_a[:3])

    saved = []
    for layer in range(depth):
        Wl = dict(hooks["weights_in"](layer, x))
        sv = {"x0": x, "W": Wl}
        idx = layer // 2
        h = rmsnorm_fwd(x, P["norm_mix_g"][layer], name="norm_mix_fwd")
        sv["h"] = h
        if layer % 2 == 0:
            qkv = matmul(h, Wl["even_w_in"], b_cols=(0, a_in), out_dtypes=(BF16,), name="even_in_fwd_a")
            proj = matmul(h, Wl["even_w_in"], b_cols=(a_in, even_in), out_dtypes=(F32,), name="even_in_fwd_b")
            parts = []
            sv["qkv"] = {dil: qkv.reshape(S // dil, dil * a_in) for _, dil in A_BRANCHES}
            for window, dil in A_BRANCHES:
                L = S // dil
                res = banded_fwd(sv["qkv"][dil], a_tabs(dil), L=L, dil=dil, max_dist=window // dil, normalize=False,
                                 name=f"dilated_fwd_d{dil}", **a_kw)
                parts.append(tuple(r.reshape(S, HA * LANES) for r in res))
            oa, lse = merge_branches(parts, name="dilated_merge")
            hooks["mid"](layer, oa)
            lb_e = lbounds[idx].reshape(HB, 1, LANES)
            ob, o_raw, states = hgrn_fwd(proj, lb_e, P["hgrn_norm_g"][idx], name="hgrn_fwd", **b_kw)
            Wl.update(hooks["weights_rest"](layer, ob))
            mixed = jnp.concatenate([oa, ob], axis=1).astype(BF16)
            sv.update(proj=proj, oa=oa, lse=lse, o_raw=o_raw, states=states, mixed=mixed, lb=lb_e)
            x = matmul(mixed, Wl["even_w_out"], extras=(x,), epilogue=lambda acc, r: (acc + r,),
                       out_dtypes=(F32,), name="even_out_fwd")
        else:
            proj = matmul(h, Wl["odd_w_qkv"], extras=(P["odd_b_qkv"][idx].reshape(1, -1),),
                          epilogue=lambda acc, b: (acc + b,), out_dtypes=(F32,), name="odd_qkv_fwd")
            sink = jnp.broadcast_to(P["odd_sinks"][idx].reshape(HQ, 1, 1), (HQ, 1, LANES))
            o, lse = banded_fwd(proj, tabs_c[:3], sink=sink, normalize=True, name="swa_fwd", **c_kw)
            hooks["mid"](layer, o)
            Wl.update(hooks["weights_rest"](layer, o))
            sv.update(proj=proj, o=o, lse=lse, sink=sink)
            x = matmul(o, Wl["odd_w_o"], extras=(P["odd_b_o"][idx].reshape(1, D), x),
                       epilogue=lambda acc, b, r: (acc + b + r,), out_dtypes=(F32,), name="odd_out_fwd")
        sv["x1"] = x
        h2 = rmsnorm_fwd(x, P["norm_mlp_g"][layer], dep=hooks["after_mixer"](layer, x), name="norm_mlp_fwd")
        u = matmul(h2, Wl["mlp_w1"], out_dtypes=(BF16,), name="mlp_up_fwd")
        x = matmul(u, Wl["mlp_w2"], a_fn=_relu2, extras=(x,), epilogue=lambda acc, r: (acc + r,),
                   out_dtypes=(F32,), name="mlp_down_fwd")
        sv.update(h2=h2, u=u)
        saved.append(sv)

    loss, dx, dgf = final_norm_loss(x, P["final_norm_g"], target, name="final_norm_loss")

    G = {k: [None] * depth for k in ("norm_mix_g", "norm_mlp_g")}
    for k in ("hgrn_lb", "hgrn_norm_g"):
        G[k] = [None] * ((depth + 1) // 2)
    for k in ("odd_w_qkv", "odd_b_qkv", "odd_sinks", "odd_w_o", "odd_b_o"):
        G[k] = [None] * (depth // 2)
    G["final_norm_g"] = dgf.reshape(D)
    B = {}

    def wgrad(a, d, key, idx, name, a_fn=None):
        M, N = a.shape[1], d.shape[1]
        into = (grad_buffer_shape((M, N), SHARDING[key]), SHARDING[key])
        B[key, idx] = matmul(a, d, trans_a=True, a_fn=a_fn, out_dtypes=(BF16,), into=into, name=name)

    dep = None
    for layer in reversed(range(depth)):
        sv = saved[layer]
        Wl = sv["W"]
        idx = layer // 2
        du = matmul(dx, Wl["mlp_w2"], trans_b=True, extras=(sv["u"],), dep=dep,
                    epilogue=lambda acc, u: (acc * (2.0 * jnp.maximum(u.astype(F32), 0.0)),),
                    out_dtypes=(BF16,), name="mlp_down_bwd_x")
        wgrad(sv["u"], dx, "mlp_w2", layer, "mlp_down_bwd_w", a_fn=_relu2)
        wgrad(sv["h2"], du, "mlp_w1", layer, "mlp_up_bwd_w")
        dh2 = matmul(du, Wl["mlp_w1"], trans_b=True, out_dtypes=(F32,), name="mlp_up_bwd_x")
        dx, dg = rmsnorm_bwd(sv["x1"], P["norm_mlp_g"][layer], dh2, dx, name="norm_mlp_bwd")
        G["norm_mlp_g"][layer] = dg.reshape(D)
        dep = hooks["grads_ready"]([("mlp_w2", layer), ("mlp_w1", layer)], B, dx)
        if layer % 2 == 0:
            dmixed = matmul(dx, Wl["even_w_out"], trans_b=True, dep=dep, out_dtypes=(F32,), name="even_out_bwd_x")
            wgrad(sv["mixed"], dx, "even_w_out", idx, "even_out_bwd_w")
            acc = None
            for window, dil in A_BRANCHES:
                L = S // dil
                rs = lambda a: a.reshape(L, -1)
                acc = banded_bwd(sv["qkv"][dil], a_tabs(dil), rs(sv["oa"]), rs(sv["lse"]), rs(dmixed),
                                         L=L, dil=dil, max_dist=window // dil, do_bpr=2 * HA,
                                         acc=None if acc is None else tuple(rs(a) for a in acc),
                                         name=f"dilated_bwd_d{dil}", **a_kw)
                acc = tuple(a.reshape(S, HA * LANES) for a in acc)
            dqb, dfb, dib, dgb, dlb, dng = hgrn_bwd(sv["proj"], sv["lb"], P["hgrn_norm_g"][idx], sv["o_raw"],
                                                    sv["states"], dmixed, dout_blk=HA, name="hgrn_bwd", **b_kw)
            G["hgrn_lb"][idx] = dlb.reshape(HB * LANES)
            G["hgrn_norm_g"][idx] = dng.reshape(LANES)
            dproj = jnp.concatenate(list(acc) + [dqb, dfb, dib, dgb], axis=1).astype(BF16)
            wgrad(sv["h"], dproj, "even_w_in", idx, "even_in_bwd_w")
            dh = matmul(dproj, Wl["even_w_in"], trans_b=True, out_dtypes=(F32,), name="even_in_bwd_x")
        else:
            do = matmul(dx, Wl["odd_w_o"], trans_b=True, dep=dep, out_dtypes=(F32,), name="odd_out_bwd_x")
            G["odd_b_o"][idx] = colsum(dx, name="odd_out_bwd_b").reshape(D)
            G["odd_w_o"][idx] = matmul(sv["o"], dx, trans_a=True, out_dtypes=(BF16,), name="odd_out_bwd_w")
            dq, dk, dv, dsink = banded_bwd(sv["proj"], tabs_c[:3], sv["o"], sv["lse"], do, sink=sv["sink"],
                                                   do_bpr=HQ, name="swa_bwd", **c_kw)
            G["odd_sinks"][idx] = dsink[:, 0, 0]
            dproj = jnp.concatenate([dq, dk, dv], axis=1)
            G["odd_b_qkv"][idx] = colsum(dproj, name="odd_qkv_bwd_b").reshape(-1)
            dproj = dproj.astype(BF16)
            G["odd_w_qkv"][idx] = matmul(sv["h"], dproj, trans_a=True, out_dtypes=(BF16,), name="odd_qkv_bwd_w")
            dh = matmul(dproj, Wl["odd_w_qkv"], trans_b=True, out_dtypes=(F32,), name="odd_qkv_bwd_x")
            HQ2 = HQ + 2 * HKV
            B["odd_w_qkv", idx] = full_to_buffer(_unpad_heads_cols(G["odd_w_qkv"][idx], HQ2), "cols")
            B["odd_w_o", idx] = full_to_buffer(_unpad_heads_rows(G["odd_w_o"][idx], HQ), "rows")
        dx, dg = rmsnorm_bwd(sv["x0"], P["norm_mix_g"][layer], dh, dx, name="norm_mix_bwd")
        G["norm_mix_g"][layer] = dg.reshape(D)
        mixer = ("even_w_out", "even_w_in") if layer % 2 == 0 else ("odd_w_o", "odd_w_qkv")
        dep = hooks["grads_ready"]([(k, idx) for k in mixer], B, dx)

    del G["odd_w_qkv"], G["odd_w_o"]
    grads = {k: jnp.stack(v) if isinstance(v, list) else v for k, v in G.items()}
    _, lb_vjp = jax.vjp(_lower_bounds, P["hgrn_lb_raw"])
    grads["hgrn_lb_raw"] = lb_vjp(grads.pop("hgrn_lb"))[0]
    grads["odd_b_qkv"] = _unpad_heads_cols(grads["odd_b_qkv"], HQ + 2 * HKV)
    return loss, dx, grads, B


def full_to_buffer(full, sharding):
    M, N = full.shape
    if sharding == "cols":
        return jnp.transpose(full.reshape(2, M // 2, 4, N // 4), (2, 0, 1, 3))
    return full.reshape(4, 2, M // 8, N)


MESH = pl.DeviceIdType.MESH
ANY = pl.BlockSpec(memory_space=pl.ANY)
SHARDING = {"even_w_in": "cols", "even_w_out": "rows", "odd_w_qkv": "cols", "odd_w_o": "rows",
            "mlp_w1": "cols", "mlp_w2": "rows"}
BIG = tuple(SHARDING)


def _mesh_pos():
    return lax.axis_index("x"), lax.axis_index("y"), lax.axis_index("c")


def _other_chips(x, y):
    return [(1 - x, y), (x, 1 - y), (1 - x, 1 - y)]


def cast_into_full(w, layer, sharding, s, *, name, tr=256):
    n, R, C = w.shape
    tr = _tile(R, tr)
    nr = R // tr

    def body(s_ref, w_ref, o_ref):
        o_ref[...] = w_ref[...].astype(BF16)

    if sharding == "cols":
        full, out_map = (R, 4 * C), (lambda i, s_ref: (i, s_ref[0]))
    else:
        full, out_map = (4 * R, C), (lambda i, s_ref: (s_ref[0] * nr + i, 0))
    grid_spec = pltpu.PrefetchScalarGridSpec(
        num_scalar_prefetch=1, grid=(nr,),
        in_specs=[pl.BlockSpec((None, tr, C), lambda i, s_ref: (layer, i, 0))],
        out_specs=pl.BlockSpec((tr, C), out_map))
    return pl.pallas_call(
        body, name=name, grid_spec=grid_spec, out_shape=jax.ShapeDtypeStruct(full, BF16),
        compiler_params=_cparams("parallel"),
    )(s.reshape(1).astype(jnp.int32), w)


HBM = pl.BlockSpec(memory_space=pltpu.HBM)
SEM = pl.BlockSpec(memory_space=pltpu.SEMAPHORE)
EFFECT = pltpu.SideEffectType.DATAFLOW_SIDE_EFFECTING


def _weight_window(ref, sharding, s, hf):
    M, N = ref.shape
    if sharding == "cols":
        rh, C = M // 2, N // 4
        return ref.at[pl.ds(hf * rh, rh), pl.ds(s * C, C)]
    rh = M // 8
    return ref.at[pl.ds((2 * s + hf) * rh, rh), :]


def _in_hbm(a):
    return pltpu.with_memory_space_constraint(a, pltpu.HBM)


def gather_start(groups, shardings, after, *, name):
    sizes = [len(g) for g in groups]
    flat = [m for g in groups for m in g]
    flat_sh = [sh for g in shardings for sh in g]
    T, ng = len(flat), len(groups)

    def body(*refs):
        sems = refs[T + 1:T + 1 + 6 * ng]
        thru = refs[T + 1 + 6 * ng:2 * T + 1 + 6 * ng]
        token = refs[2 * T + 1 + 6 * ng]
        x, y, c = _mesh_pos()
        pos = 0
        for g in range(ng):
            for t in range(sizes[g]):
                w = _weight_window(thru[pos], flat_sh[pos], 2 * x + y, c)
                for j, (cx, cy) in enumerate(_other_chips(x, y)):
                    pltpu.make_async_remote_copy(src_ref=w, dst_ref=w, send_sem=sems[6 * g + j],
                                                 recv_sem=sems[6 * g + 3 + j], device_id=(cx, cy, c),
                                                 device_id_type=MESH).start()
                pos += 1
        token[...] = jnp.zeros_like(token)

    outs = pl.pallas_call(
        body, name=name,
        in_specs=[HBM] * T + [ANY],
        out_specs=[SEM] * (6 * ng) + [HBM] * T + [pl.BlockSpec(memory_space=pltpu.VMEM)],
        out_shape=[pltpu.SemaphoreType.DMA(())] * (6 * ng) + [pltpu.HBM(m.shape, m.dtype) for m in flat]
        + [jax.ShapeDtypeStruct((8, LANES), F32)],
        input_output_aliases={t: 6 * ng + t for t in range(T)},
        compiler_params=pltpu.CompilerParams(has_side_effects=EFFECT),
    )(*[_in_hbm(m) for m in flat], after)
    res, pos = [], 6 * ng
    for g in range(ng):
        res.append((list(outs[6 * g:6 * g + 6]), list(outs[pos:pos + sizes[g]])))
        pos += sizes[g]
    return res, outs[-1]


def gather_forward(mats, shardings, sems, after, *, name):
    T = len(mats)

    def body(*refs):
        sems1 = refs[T:T + 6]
        sems2 = refs[T + 7:T + 13]
        thru = refs[T + 13:2 * T + 13]
        token = refs[2 * T + 13]
        x, y, c = _mesh_pos()
        chips = _other_chips(x, y)
        for t in range(T):
            own = _weight_window(thru[t], shardings[t], 2 * x + y, c)
            for j, (cx, cy) in enumerate(chips):
                landed = _weight_window(thru[t], shardings[t], 2 * cx + cy, c)
                first = pltpu.make_async_remote_copy(src_ref=own, dst_ref=landed, send_sem=sems1[j],
                                                     recv_sem=sems1[3 + j], device_id=(cx, cy, c),
                                                     device_id_type=MESH)
                first.wait_recv()
                first.wait_send()
        for t in range(T):
            for j, (cx, cy) in enumerate(chips):
                landed = _weight_window(thru[t], shardings[t], 2 * cx + cy, c)
                pltpu.make_async_remote_copy(src_ref=landed, dst_ref=landed, send_sem=sems2[j],
                                             recv_sem=sems2[3 + j], device_id=(x, y, 1 - c),
                                             device_id_type=MESH).start()
        token[...] = jnp.zeros_like(token)

    outs = pl.pallas_call(
        body, name=name,
        in_specs=[HBM] * T + [SEM] * 6 + [ANY],
        out_specs=[SEM] * 6 + [HBM] * T + [pl.BlockSpec(memory_space=pltpu.VMEM)],
        out_shape=[pltpu.SemaphoreType.DMA(())] * 6 + [pltpu.HBM(m.shape, m.dtype) for m in mats]
        + [jax.ShapeDtypeStruct((8, LANES), F32)],
        input_output_aliases={t: 6 + t for t in range(T)},
        compiler_params=pltpu.CompilerParams(has_side_effects=EFFECT),
    )(*mats, *sems, after)
    return list(outs[:6]), list(outs[6:6 + T]), outs[-1]


def gather_finish(mats, shardings, sems, after, *, name):
    T = len(mats)

    def body(*refs):
        sems2 = refs[T:T + 6]
        thru = refs[T + 7:2 * T + 7]
        x, y, c = _mesh_pos()
        for t in range(T):
            for j, (cx, cy) in enumerate(_other_chips(x, y)):
                sent = _weight_window(thru[t], shardings[t], 2 * cx + cy, c)
                other = _weight_window(thru[t], shardings[t], 2 * cx + cy, 1 - c)
                cp = pltpu.make_async_remote_copy(src_ref=sent, dst_ref=other, send_sem=sems2[j],
                                                  recv_sem=sems2[3 + j], device_id=(x, y, 1 - c),
                                                  device_id_type=MESH)
                cp.wait_recv()
                cp.wait_send()

    outs = pl.pallas_call(
        body, name=name,
        in_specs=[HBM] * T + [SEM] * 6 + [ANY],
        out_specs=[HBM] * T,
        out_shape=[pltpu.HBM(m.shape, m.dtype) for m in mats],
        input_output_aliases={t: t for t in range(T)},
        compiler_params=pltpu.CompilerParams(has_side_effects=EFFECT),
    )(*mats, *sems, after)
    return list(outs)


def exchange_halves(bufs, *, name):
    T = len(bufs)

    def body(*refs):
        in_refs, out_refs = refs[:T], refs[T:2 * T]
        send_sems, recv_sems = refs[2 * T:]
        x, y, c = _mesh_pos()
        cps = [pltpu.make_async_remote_copy(src_ref=in_refs[t].at[:, pl.ds(1 - c, 1)], dst_ref=out_refs[t],
                                            send_sem=send_sems.at[t], recv_sem=recv_sems.at[t],
                                            device_id=(x, y, 1 - c), device_id_type=MESH) for t in range(T)]
        for cp in cps:
            cp.start()
        for cp in cps:
            cp.wait()

    return pl.pallas_call(
        body, name=name, in_specs=[ANY] * T, out_specs=[ANY] * T,
        out_shape=[jax.ShapeDtypeStruct((4, 1) + b.shape[2:], b.dtype) for b in bufs],
        scratch_shapes=[pltpu.SemaphoreType.DMA((T,)), pltpu.SemaphoreType.DMA((T,))],
    )(*bufs)


def pair_sum(buf, got, c, *, name, tr=512):
    _, _, n, rh, C = buf.shape
    tr = _tile(rh, tr)

    def body(c_ref, a_ref, b_ref, o_ref):
        o_ref[...] = (a_ref[...].astype(F32) + b_ref[...].astype(F32)).astype(o_ref.dtype)

    grid_spec = pltpu.PrefetchScalarGridSpec(
        num_scalar_prefetch=1, grid=(4, n, rh // tr),
        in_specs=[pl.BlockSpec((None, None, None, tr, C), lambda s, l, i, c_ref: (s, c_ref[0], l, i, 0)),
                  pl.BlockSpec((None, None, None, tr, C), lambda s, l, i, c_ref: (s, 0, l, i, 0))],
        out_specs=pl.BlockSpec((None, None, tr, C), lambda s, l, i, c_ref: (s, l, i, 0)))
    return pl.pallas_call(
        body, name=name, grid_spec=grid_spec, out_shape=jax.ShapeDtypeStruct((4, n, rh, C), BF16),
        compiler_params=_cparams("parallel", "parallel", "parallel"),
    )(c.reshape(1).astype(jnp.int32), buf, got)


def scatter_to_owners(sums, *, name):
    T = len(sums)

    def body(*refs):
        in_refs, out_refs = refs[:T], refs[T:2 * T]
        send_sems, recv_sems = refs[2 * T:]
        x, y, c = _mesh_pos()
        chips = _other_chips(x, y)
        cps = [pltpu.make_async_remote_copy(src_ref=in_refs[t].at[pl.ds(2 * cx + cy, 1)],
                                            dst_ref=out_refs[t].at[pl.ds(j, 1)],
                                            send_sem=send_sems.at[t, j], recv_sem=recv_sems.at[t, j],
                                            device_id=(cx, cy, c), device_id_type=MESH)
               for t in range(T) for j, (cx, cy) in enumerate(chips)]
        for cp in cps:
            cp.start()
        for cp in cps:
            cp.wait()

    return pl.pallas_call(
        body, name=name, in_specs=[ANY] * T, out_specs=[ANY] * T,
        out_shape=[jax.ShapeDtypeStruct((3,) + s.shape[1:], s.dtype) for s in sums],
        scratch_shapes=[pltpu.SemaphoreType.DMA((T, 3)), pltpu.SemaphoreType.DMA((T, 3))],
    )(*sums)


def exchange_start(bufs, *, name):
    T = len(bufs)
    lands = [_in_hbm(lax.empty((4, 1) + b.shape[2:], b.dtype)) for b in bufs]

    def body(*refs):
        send_sem, recv_sem = refs[2 * T], refs[2 * T + 1]
        src, dst = refs[2 * T + 2:3 * T + 2], refs[3 * T + 2:4 * T + 2]
        token = refs[4 * T + 2]
        x, y, c = _mesh_pos()
        for t in range(T):
            pltpu.make_async_remote_copy(src_ref=src[t].at[:, pl.ds(1 - c, 1)], dst_ref=dst[t], send_sem=send_sem,
                                         recv_sem=recv_sem, device_id=(x, y, 1 - c), device_id_type=MESH).start()
        token[...] = jnp.zeros_like(token)

    outs = pl.pallas_call(
        body, name=name,
        in_specs=[HBM] * (2 * T),
        out_specs=[SEM] * 2 + [HBM] * (2 * T) + [pl.BlockSpec(memory_space=pltpu.VMEM)],
        out_shape=[pltpu.SemaphoreType.DMA(())] * 2 + [pltpu.HBM(a.shape, a.dtype) for a in list(bufs) + lands]
        + [jax.ShapeDtypeStruct((8, LANES), F32)],
        input_output_aliases={t: 2 + t for t in range(2 * T)},
        compiler_params=pltpu.CompilerParams(has_side_effects=EFFECT),
    )(*[_in_hbm(b) for b in bufs], *lands)
    return list(outs[:2]), list(outs[2:2 + T]), list(outs[2 + T:2 + 2 * T]), outs[-1]


def exchange_wait(bufs, lands, sems, after, *, name):
    T = len(bufs)

    def body(*refs):
        send_sem, recv_sem = refs[2 * T], refs[2 * T + 1]
        src, dst = refs[2 * T + 3:3 * T + 3], refs[3 * T + 3:4 * T + 3]
        x, y, c = _mesh_pos()
        for t in range(T):
            cp = pltpu.make_async_remote_copy(src_ref=src[t].at[:, pl.ds(1 - c, 1)], dst_ref=dst[t],
                                              send_sem=send_sem, recv_sem=recv_sem, device_id=(x, y, 1 - c),
                                              device_id_type=MESH)
            cp.wait_recv()
            cp.wait_send()

    outs = pl.pallas_call(
        body, name=name,
        in_specs=[HBM] * (2 * T) + [SEM] * 2 + [ANY],
        out_specs=[HBM] * (2 * T),
        out_shape=[pltpu.HBM(a.shape, a.dtype) for a in list(bufs) + list(lands)],
        input_output_aliases={t: t for t in range(2 * T)},
        compiler_params=pltpu.CompilerParams(has_side_effects=EFFECT),
    )(*bufs, *lands, *sems, after)
    return list(outs[:T]), list(outs[T:])


def scatter_start(sums, *, name):
    T = len(sums)
    lands = [_in_hbm(lax.empty((3,) + s.shape[1:], s.dtype)) for s in sums]

    def body(*refs):
        sems = refs[2 * T:2 * T + 6]
        src, dst = refs[2 * T + 6:3 * T + 6], refs[3 * T + 6:4 * T + 6]
        token = refs[4 * T + 6]
        x, y, c = _mesh_pos()
        for t in range(T):
            for j, (cx, cy) in enumerate(_other_chips(x, y)):
                pltpu.make_async_remote_copy(src_ref=src[t].at[pl.ds(2 * cx + cy, 1)], dst_ref=dst[t].at[pl.ds(j, 1)],
                                             send_sem=sems[j], recv_sem=sems[3 + j], device_id=(cx, cy, c),
                                             device_id_type=MESH).start()
        token[...] = jnp.zeros_like(token)

    outs = pl.pallas_call(
        body, name=name,
        in_specs=[HBM] * (2 * T),
        out_specs=[SEM] * 6 + [HBM] * (2 * T) + [pl.BlockSpec(memory_space=pltpu.VMEM)],
        out_shape=[pltpu.SemaphoreType.DMA(())] * 6 + [pltpu.HBM(a.shape, a.dtype) for a in list(sums) + lands]
        + [jax.ShapeDtypeStruct((8, LANES), F32)],
        input_output_aliases={t: 6 + t for t in range(2 * T)},
        compiler_params=pltpu.CompilerParams(has_side_effects=EFFECT),
    )(*[_in_hbm(s) for s in sums], *lands)
    return list(outs[:6]), list(outs[6:6 + T]), list(outs[6 + T:6 + 2 * T]), outs[-1]


def scatter_wait(sums, lands, sems, after, *, name):
    T = len(sums)

    def body(*refs):
        sem_refs = refs[2 * T:2 * T + 6]
        src, dst = refs[2 * T + 7:3 * T + 7], refs[3 * T + 7:4 * T + 7]
        x, y, c = _mesh_pos()
        for t in range(T):
            for j, (cx, cy) in enumerate(_other_chips(x, y)):
                cp = pltpu.make_async_remote_copy(src_ref=src[t].at[pl.ds(2 * cx + cy, 1)],
                                                  dst_ref=dst[t].at[pl.ds(j, 1)], send_sem=sem_refs[j],
                                                  recv_sem=sem_refs[3 + j], device_id=(cx, cy, c),
                                                  device_id_type=MESH)
                cp.wait_recv()
                cp.wait_send()

    outs = pl.pallas_call(
        body, name=name,
        in_specs=[HBM] * (2 * T) + [SEM] * 6 + [ANY],
        out_specs=[HBM] * (2 * T),
        out_shape=[pltpu.HBM(a.shape, a.dtype) for a in list(sums) + list(lands)],
        input_output_aliases={t: t for t in range(2 * T)},
        compiler_params=pltpu.CompilerParams(has_side_effects=EFFECT),
    )(*sums, *lands, *sems, after)
    return list(outs[:T]), list(outs[T:])


def owner_sum(sums, got, s, c, grad, layer, *, name, tr=512):
    rh, C = sums.shape[2:]
    tr = _tile(rh, tr)
    nr = rh // tr

    def body(sc_ref, a_ref, b0_ref, b1_ref, b2_ref, g_ref, o_ref):
        o_ref[...] = ((a_ref[...].astype(F32) + b0_ref[...].astype(F32)) + b1_ref[...].astype(F32)) \
            + b2_ref[...].astype(F32)

    def got_spec(j):
        return pl.BlockSpec((None, None, tr, C), lambda i, sc_ref: (j, 0, i, 0))

    grid_spec = pltpu.PrefetchScalarGridSpec(
        num_scalar_prefetch=1, grid=(nr,),
        in_specs=[pl.BlockSpec((None, None, tr, C), lambda i, sc_ref: (sc_ref[0], 0, i, 0)),
                  got_spec(0), got_spec(1), got_spec(2), ANY],
        out_specs=pl.BlockSpec((None, tr, C), lambda i, sc_ref: (layer, sc_ref[1] * nr + i, 0)))
    return pl.pallas_call(
        body, name=name, grid_spec=grid_spec, out_shape=jax.ShapeDtypeStruct(grad.shape, F32),
        input_output_aliases={5: 0},
        compiler_params=_cparams("parallel"),
    )(jnp.stack([s, c]).astype(jnp.int32), sums, got, got, got, grad)


def join_halves(grads, *, name):
    T = len(grads)

    def body(*refs):
        out_refs = refs[T:2 * T]
        send_sems, recv_sems = refs[2 * T:]
        x, y, c = _mesh_pos()

        def win(t, hf):
            rh = grads[t].shape[1] // 2
            return out_refs[t].at[:, pl.ds(hf * rh, rh), :]

        def remote(t, w):
            return pltpu.make_async_remote_copy(src_ref=w, dst_ref=w, send_sem=send_sems.at[t],
                                                recv_sem=recv_sems.at[t], device_id=(x, y, 1 - c),
                                                device_id_type=MESH)

        cps = [remote(t, win(t, c)) for t in range(T)]
        for cp in cps:
            cp.start()
        for t in range(T):
            remote(t, win(t, 1 - c)).wait_recv()
        for cp in cps:
            cp.wait_send()

    return pl.pallas_call(
        body, name=name, in_specs=[ANY] * T, out_specs=[ANY] * T,
        out_shape=[jax.ShapeDtypeStruct(g.shape, g.dtype) for g in grads],
        input_output_aliases={t: t for t in range(T)},
        scratch_shapes=[pltpu.SemaphoreType.DMA((T,)), pltpu.SemaphoreType.DMA((T,))],
    )(*grads)


def allreduce_small(v, *, name):
    rows = v.shape[0]

    def body(v_ref, out_ref, buf, send_sems, recv_sems):
        x, y, c = _mesh_pos()
        me = 4 * x + 2 * y + c
        flips = [(dx, dy, dc) for dx in (0, 1) for dy in (0, 1) for dc in (0, 1)][1:]

        def peer(f):
            return tuple(1 - p if d else p for d, p in zip(f, (x, y, c)))

        cps = []
        for k, f in enumerate(flips):
            px, py, pc = peer(f)
            cps.append(pltpu.make_async_remote_copy(src_ref=v_ref, dst_ref=buf.at[me], send_sem=send_sems.at[k],
                                                    recv_sem=recv_sems.at[k], device_id=(px, py, pc),
                                                    device_id_type=MESH))
        for cp in cps:
            cp.start()
        buf[me] = v_ref[...]
        for k, f in enumerate(flips):
            px, py, pc = peer(f)
            slot = buf.at[4 * px + 2 * py + pc]
            pltpu.make_async_remote_copy(src_ref=slot, dst_ref=slot, send_sem=send_sems.at[k],
                                         recv_sem=recv_sems.at[k], device_id=(px, py, pc),
                                         device_id_type=MESH).wait_recv()
        for cp in cps:
            cp.wait_send()
        acc = buf[0]
        for i in range(1, 8):
            acc = acc + buf[i]
        out_ref[...] = acc

    vm = pl.BlockSpec(memory_space=pltpu.VMEM)
    return pl.pallas_call(
        body, name=name, in_specs=[vm], out_specs=vm,
        out_shape=jax.ShapeDtypeStruct(v.shape, F32),
        scratch_shapes=[pltpu.VMEM((8, rows, LANES), F32), pltpu.SemaphoreType.DMA((7,)),
                        pltpu.SemaphoreType.DMA((7,))],
    )(v)


def _adam_math(w, m, v, g):
    m = ADAM_B1 * m + (1.0 - ADAM_B1) * g
    v = ADAM_B2 * v + (1.0 - ADAM_B2) * (g * g)
    m_hat = m / (1.0 - ADAM_B1 ** ADAM_STEP)
    v_hat = v / (1.0 - ADAM_B2 ** ADAM_STEP)
    delta = -ADAM_LR * (m_hat / (jnp.sqrt(v_hat) + ADAM_EPS) + ADAM_WD * w)
    return delta, m, v


def adamw(w, m, v, g, *, name, tr=256):
    def body(w_ref, m_ref, v_ref, g_ref, d_ref, nm_ref, nv_ref):
        d, nm, nv = _adam_math(w_ref[...], m_ref[...], v_ref[...], g_ref[...])
        d_ref[...] = d
        nm_ref[...] = nm
        nv_ref[...] = nv

    shape = jax.ShapeDtypeStruct(w.shape, F32)
    if w.ndim == 2:
        return pl.pallas_call(body, name=name, out_shape=[shape] * 3)(w, m, v, g)
    n, R, C = w.shape
    tr = _tile(R, tr)
    spec = pl.BlockSpec((None, tr, C), lambda l, i: (l, i, 0))
    return pl.pallas_call(
        body, name=name, grid=(n, R // tr), in_specs=[spec] * 4, out_specs=[spec] * 3, out_shape=[shape] * 3,
        compiler_params=_cparams("parallel", "parallel"),
    )(w, m, v, g)


def _pack(parts):
    flat = jnp.concatenate([p.reshape(-1).astype(F32) for p in parts])
    size = -(-flat.shape[0] // (8 * LANES)) * (8 * LANES)
    return jnp.pad(flat, (0, size - flat.shape[0])).reshape(size // LANES, LANES)


def _unpack(block, shapes):
    flat = block.reshape(-1)
    out, pos = [], 0
    for shp in shapes:
        size = int(np.prod(shp))
        out.append(flat[pos:pos + size].reshape(shp))
        pos += size
    return out


SMALL = ("norm_mix_g", "norm_mlp_g", "final_norm_g", "hgrn_lb_raw", "hgrn_norm_g", "odd_sinks")
WEIGHTS = ("norm_mix_g", "norm_mlp_g", "final_norm_g", "even_w_in", "even_w_out", "hgrn_lb_raw", "hgrn_norm_g",
           "odd_w_qkv", "odd_b_qkv", "odd_sinks", "odd_w_o", "odd_b_o", "mlp_w1", "mlp_w2")


def kernel(x, norm_mix_g, norm_mlp_g, final_norm_g, even_w_in, even_w_out, hgrn_lb_raw, hgrn_norm_g, odd_w_qkv, odd_b_qkv, odd_sinks, odd_w_o, odd_b_o, mlp_w1, mlp_w2, loss_target, m_norm_mix_g, m_norm_mlp_g, m_final_norm_g, m_even_w_in, m_even_w_out, m_hgrn_lb_raw, m_hgrn_norm_g, m_odd_w_qkv, m_odd_b_qkv, m_odd_sinks, m_odd_w_o, m_odd_b_o, m_mlp_w1, m_mlp_w2, v_norm_mix_g, v_norm_mlp_g, v_final_norm_g, v_even_w_in, v_even_w_out, v_hgrn_lb_raw, v_hgrn_norm_g, v_odd_w_qkv, v_odd_b_qkv, v_odd_sinks, v_odd_w_o, v_odd_b_o, v_mlp_w1, v_mlp_w2):
    args = locals()
    W = {k: args[k] for k in WEIGHTS}
    M = {k: args["m_" + k] for k in WEIGHTS}
    V = {k: args["v_" + k] for k in WEIGHTS}
    _, S, D = x.shape
    depth = norm_mix_g.shape[0]
    HQ = D // C_DIM
    HKV = HQ // C_GROUP
    xi, yi, ci = _mesh_pos()
    shard = 2 * xi + yi

    n_odd, bq = odd_b_qkv.shape
    bo = odd_b_o.shape[1]
    keep = (ci == 0).astype(F32)
    pieces = [lax.dynamic_update_slice(jnp.zeros((n_odd, 4 * bq), F32), odd_b_qkv * keep, (0, shard * bq)),
              lax.dynamic_update_slice(jnp.zeros((n_odd, 4 * bo), F32), odd_b_o * keep, (0, shard * bo))]
    biases = allreduce_small(_pack(pieces), name="gather_biases")
    b_qkv_full, b_o_full = _unpack(biases, [(n_odd, 4 * bq), (n_odd, 4 * bo)])
    P = {k: W[k] for k in SMALL}
    P.update(odd_b_qkv=_pad_heads_cols(b_qkv_full, HQ + 2 * HKV), odd_b_o=b_o_full)

    def layer_keys(layer):
        mixer = ("even_w_in", "even_w_out") if layer % 2 == 0 else ("odd_w_qkv", "odd_w_o")
        return [(k, layer // 2) for k in mixer] + [("mlp_w1", layer), ("mlp_w2", layer)]

    keys = [layer_keys(0)[:1], layer_keys(0)[1:]] + [layer_keys(layer) for layer in range(1, depth)]
    shardings = [[SHARDING[k] for k, _ in ks] for ks in keys]
    started, dep0 = gather_start(
        [[cast_into_full(W[k], i, SHARDING[k], shard, name="cast_" + k) for k, i in ks] for ks in keys],
        shardings, biases, name="gather_start")
    forwarded = {}

    def forward_group(g, after):
        sems, mats = started[g]
        forwarded[g] = gather_forward(mats, shardings[g], sems, after, name="gather_forward")
        return forwarded[g][2]

    def finish_group(g, after):
        sems, mats, _ = forwarded[g]
        mats = gather_finish(mats, shardings[g], sems, after, name="gather_finish")
        Wl = {k: m for (k, _), m in zip(keys[g], mats)}
        if "odd_w_qkv" in Wl:
            Wl["odd_w_qkv"] = _pad_heads_cols(Wl["odd_w_qkv"], HQ + 2 * HKV)
            Wl["odd_w_o"] = _pad_heads_rows(Wl["odd_w_o"], HQ)
        return Wl

    def weights_in(layer, x_in):
        if layer == 0:
            forward_group(0, dep0)
        return finish_group(layer + 1 if layer else 0, x_in)

    def mid(layer, a):
        if layer == 0:
            forward_group(1, a)

    def weights_rest(layer, a):
        return finish_group(1, a) if layer == 0 else {}

    def after_mixer(layer, x1):
        return forward_group(layer + 2, x1) if layer + 1 < depth else None

    reduced = {k: lax.empty(W[k].shape, F32) for k in BIG}
    exchanging, scattering = [], []

    def advance(after):
        for kis, (sems, sums, lands, _) in scattering:
            sums, lands = scatter_wait(sums, lands, sems, after, name="scatter_wait")
            for (k, i), s, g in zip(kis, sums, lands):
                reduced[k] = owner_sum(s, g, shard, ci, reduced[k], i, name="owner_sum_" + k)
        scattering.clear()
        for kis, (sems, bufs, lands, _) in exchanging:
            bufs, lands = exchange_wait(bufs, lands, sems, after, name="exchange_wait")
            sums = [pair_sum(b, g, ci, name="pair_sum_" + k) for (k, _), b, g in zip(kis, bufs, lands)]
            scattering.append((kis, scatter_start(sums, name="scatter_start")))
        exchanging.clear()
        return scattering[-1][1][3] if scattering else None

    def grads_ready(kis, B, after):
        advance(after)
        bufs = [B[ki].reshape((4, 2, 1) + B[ki].shape[2:]) for ki in kis]
        exchanging.append((kis, exchange_start(bufs, name="exchange_start")))
        return exchanging[-1][1][3]

    hooks = dict(weights_in=weights_in, mid=mid, weights_rest=weights_rest, after_mixer=after_mixer,
                 grads_ready=grads_ready)
    loss, dx, G, B = local_step(x[0], loss_target[0], P, hooks)
    advance(advance(dx))
    grads = dict(zip(BIG, join_halves([reduced[k] for k in BIG], name="join_halves")))

    small_keys = SMALL + ("odd_b_qkv", "odd_b_o")
    small_shapes = [G[k].shape for k in small_keys]
    small = _unpack(allreduce_small(_pack([G[k] for k in small_keys]), name="reduce_small"), small_shapes)
    grads.update(zip(small_keys, small))
    grads["odd_b_qkv"] = lax.dynamic_slice(grads["odd_b_qkv"], (0, shard * bq), (n_odd, bq))
    grads["odd_b_o"] = lax.dynamic_slice(grads["odd_b_o"], (0, shard * bo), (n_odd, bo))

    deltas, new_m, new_v = {}, {}, {}
    for k in WEIGHTS:
        w2 = (lambda a: a.reshape(1, -1)) if W[k].ndim == 1 else (lambda a: a)
        d, nm, nv = adamw(w2(W[k]), w2(M[k]), w2(V[k]), w2(grads[k]), name="adamw_" + k)
        deltas[k], new_m[k], new_v[k] = (a.reshape(W[k].shape) for a in (d, nm, nv))

    loss = lax.psum(loss[0, 0], ("x", "y", "c"))
    return (loss, dx[None], *[grads[k] for k in WEIGHTS], *[deltas[k] for k in WEIGHTS],
            *[new_m[k] for k in WEIGHTS], *[new_v[k] for k in WEIGHTS])
```

```python
import functools
import math

import jax
import jax.numpy as jnp
import numpy as np
from jax import lax
from jax.experimental import pallas as pl
from jax.experimental.pallas import tpu as pltpu

F32 = jnp.float32
BF16 = jnp.bfloat16

NORM_EPS = 1e-5
ROPE_THETA = 500000.0
ROPE_FRACTION = 4
LANES = 128
BAND = 128
A_DIM = 128
A_BRANCHES = ((128, 1), (512, 4), (2048, 16))
B_DIM = 128
B_CHUNK = 64
B_SUB = 16
A_HEADS_PER_STEP = 2
B_HEADS_PER_STEP = 2
MAX_SCATTERS_IN_FLIGHT = 3
C_DIM = 64
C_GROUP = 8
C_WINDOW = 128

ADAM_LR = 0.001
ADAM_B1 = 0.9
ADAM_B2 = 0.999
ADAM_EPS = 1e-08
ADAM_WD = 0.01
ADAM_STEP = 10

VMEM_LIMIT = 56 * 1024 * 1024


def _cparams(*sem):
    return pltpu.CompilerParams(dimension_semantics=tuple(sem), vmem_limit_bytes=VMEM_LIMIT)


def _tile(n, want):
    if n <= want:
        return n
    t = want - want % LANES
    while n % t:
        t -= LANES
    assert t > 0, (n, want)
    return t


def matmul(a, b, *, trans_a=False, trans_b=False, out_dtypes, a_fn=None, epilogue=None, extras=(), name,
           b_cols=None, into=None, dep=None, tm=1024, tn=1024, tk=2048):
    if trans_a:
        K, M = a.shape
    else:
        M, K = a.shape
    if trans_b:
        N, K2 = b.shape
    else:
        K2, N = b.shape
    assert K == K2, (a.shape, b.shape)
    col0 = 0
    if b_cols is not None:
        assert not trans_b
        col0, N = b_cols[0], b_cols[1] - b_cols[0]
    if into is not None:
        rh, cs = into[0][2:]
        tm, tn = _tile(rh, tm), _tile(cs, tn)
    tm, tn, tk = _tile(M, tm), _tile(N, tn), _tile(K, tk)
    nk = K // tk
    n_extra = len(extras)
    n_out = len(out_dtypes)

    def body(*refs):
        a_ref, b_ref = refs[0], refs[1]
        extra_refs = refs[2:2 + n_extra]
        out_refs = refs[-1 - n_out:-1]
        acc_ref = refs[-1]
        k = pl.program_id(2)
        at = a_ref[...]
        if a_fn is not None:
            at = a_fn(at.astype(F32))
        at = at.astype(BF16)
        bt = b_ref[...].astype(BF16)
        dims = (((0,) if trans_a else (1,), (1,) if trans_b else (0,)), ((), ()))
        prod = lax.dot_general(at, bt, dims, preferred_element_type=F32)

        def finish(acc):
            ex = [r[...] for r in extra_refs]
            outs = epilogue(acc, *ex) if epilogue is not None else (acc,)
            for o_ref, o in zip(out_refs, outs):
                o_ref[...] = o.astype(o_ref.dtype)

        if nk == 1:
            finish(prod)
            return

        @pl.when(k == 0)
        def _():
            acc_ref[...] = prod

        @pl.when((k > 0) & (k < nk - 1))
        def _():
            acc_ref[...] += prod

        @pl.when(k == nk - 1)
        def _():
            finish(acc_ref[...] + prod)

    a_spec = pl.BlockSpec((tk, tm), lambda i, j, k: (k, i)) if trans_a else pl.BlockSpec((tm, tk), lambda i, j, k: (i, k))
    assert col0 % tn == 0, (col0, tn)
    b_spec = (pl.BlockSpec((tn, tk), lambda i, j, k: (j, k)) if trans_b
              else pl.BlockSpec((tk, tn), lambda i, j, k: (k, col0 // tn + j)))
    e_specs = []
    for e in extras:
        if e.shape == (1, N):
            e_specs.append(pl.BlockSpec((1, tn), lambda i, j, k: (0, j)))
        else:
            assert e.shape == (M, N), (e.shape, M, N)
            e_specs.append(pl.BlockSpec((tm, tn), lambda i, j, k: (i, j)))
    args = [a, b, *extras]
    in_specs = [a_spec, b_spec] + e_specs
    if dep is not None:
        args.append(dep)
        in_specs.append(pl.BlockSpec(dep.shape, lambda i, j, k: (0, 0)))
    if into is None:
        out_specs = [pl.BlockSpec((tm, tn), lambda i, j, k: (i, j)) for _ in out_dtypes]
        out_shape = [jax.ShapeDtypeStruct((M, N), dt) for dt in out_dtypes]
    else:
        buf_shape, sharding = into
        assert n_out == 1
        index = grad_buffer_index(buf_shape, (M, N), sharding, tm, tn)
        out_specs = [pl.BlockSpec((None, None, tm, tn), lambda i, j, k: index(i, j))]
        out_shape = [jax.ShapeDtypeStruct(buf_shape, out_dtypes[0])]
    outs = pl.pallas_call(
        body, name=name,
        grid=(M // tm, N // tn, nk),
        in_specs=in_specs, out_specs=out_specs, out_shape=out_shape,
        scratch_shapes=[pltpu.VMEM((tm, tn), F32)],
        compiler_params=_cparams("parallel", "parallel", "arbitrary"),
    )(*args)
    return outs[0] if n_out == 1 else tuple(outs)


def grad_buffer_shape(mat_shape, sharding):
    M, N = mat_shape
    return (4, 2, M // 2, N // 4) if sharding == "cols" else (4, 2, M // 8, N)


def grad_buffer_index(buf_shape, mat_shape, sharding, tm, tn):
    _, _, rh, cs = buf_shape
    M, N = mat_shape
    ib, jb = rh // tm, cs // tn
    if sharding == "cols":
        assert (M, N) == (2 * rh, 4 * cs), (buf_shape, mat_shape)
        return lambda i, j: (j // jb, i // ib, i % ib, j % jb)
    assert sharding == "rows" and (M, N) == (8 * rh, cs), (buf_shape, mat_shape)
    return lambda i, j: (i // (2 * ib), (i // ib) % 2, i % ib, j)


def rmsnorm_fwd(x, g, *, name, dep=None, tr=256):
    S, D = x.shape
    tr = _tile(S, tr)

    def body(x_ref, g_ref, *rest):
        h_ref = rest[-1]
        xv = x_ref[...]
        rstd = lax.rsqrt(jnp.mean(xv * xv, axis=-1, keepdims=True) + NORM_EPS)
        h_ref[...] = (xv * rstd * g_ref[...]).astype(h_ref.dtype)

    args = [x, g.reshape(1, D)]
    in_specs = [pl.BlockSpec((tr, D), lambda i: (i, 0)), pl.BlockSpec((1, D), lambda i: (0, 0))]
    if dep is not None:
        args.append(dep)
        in_specs.append(pl.BlockSpec(dep.shape, lambda i: (0, 0)))
    return pl.pallas_call(
        body, name=name, grid=(S // tr,),
        in_specs=in_specs,
        out_specs=pl.BlockSpec((tr, D), lambda i: (i, 0)),
        out_shape=jax.ShapeDtypeStruct((S, D), BF16),
        compiler_params=_cparams("parallel"),
    )(*args)


def _rms_bwd_rows(xv, gv, dh):
    rstd = lax.rsqrt(jnp.mean(xv * xv, axis=-1, keepdims=True) + NORM_EPS)
    xhat = xv * rstd
    dxhat = dh * gv
    dx = rstd * (dxhat - xhat * jnp.mean(dxhat * xhat, axis=-1, keepdims=True))
    return dx, dh * xhat


def rmsnorm_bwd(x, g, dh, dres, *, name, tr=256):
    S, D = x.shape
    tr = _tile(S, tr)

    def body(x_ref, g_ref, dh_ref, dres_ref, dx_ref, dg_ref):
        @pl.when(pl.program_id(0) == 0)
        def _():
            dg_ref[...] = jnp.zeros_like(dg_ref)

        dx, dgr = _rms_bwd_rows(x_ref[...], g_ref[...], dh_ref[...].astype(F32))
        dx_ref[...] = dres_ref[...] + dx
        dg_ref[...] += jnp.sum(dgr, axis=0, keepdims=True)

    row = pl.BlockSpec((tr, D), lambda i: (i, 0))
    vec = pl.BlockSpec((1, D), lambda i: (0, 0))
    return pl.pallas_call(
        body, name=name, grid=(S // tr,),
        in_specs=[row, vec, row, row],
        out_specs=[row, vec],
        out_shape=[jax.ShapeDtypeStruct((S, D), F32), jax.ShapeDtypeStruct((1, D), F32)],
        compiler_params=_cparams("arbitrary"),
    )(x, g.reshape(1, D), dh, dres)


def final_norm_loss(x, g, target, *, name, tr=256):
    S, D = x.shape
    tr = _tile(S, tr)

    def body(x_ref, g_ref, t_ref, loss_ref, dx_ref, dg_ref):
        @pl.when(pl.program_id(0) == 0)
        def _():
            dg_ref[...] = jnp.zeros_like(dg_ref)
            loss_ref[...] = jnp.zeros_like(loss_ref)

        xv, gv = x_ref[...], g_ref[...]
        rstd = lax.rsqrt(jnp.mean(xv * xv, axis=-1, keepdims=True) + NORM_EPS)
        err = xv * rstd * gv - t_ref[...]
        part = 0.5 * jnp.sum(jnp.mean(err * err, axis=-1, keepdims=True), axis=0, keepdims=True)
        loss_ref[...] += jnp.broadcast_to(part, loss_ref.shape)
        dx, dgr = _rms_bwd_rows(xv, gv, err * (1.0 / D))
        dx_ref[...] = dx
        dg_ref[...] += jnp.sum(dgr, axis=0, keepdims=True)

    row = pl.BlockSpec((tr, D), lambda i: (i, 0))
    vec = pl.BlockSpec((1, D), lambda i: (0, 0))
    return pl.pallas_call(
        body, name=name, grid=(S // tr,),
        in_specs=[row, vec, row],
        out_specs=[pl.BlockSpec((8, LANES), lambda i: (0, 0)), row, vec],
        out_shape=[jax.ShapeDtypeStruct((8, LANES), F32), jax.ShapeDtypeStruct((S, D), F32),
                   jax.ShapeDtypeStruct((1, D), F32)],
        compiler_params=_cparams("arbitrary"),
    )(x, g.reshape(1, D), target)


def colsum(a, *, name, tr=256):
    S, N = a.shape
    tr = _tile(S, tr)

    def body(a_ref, o_ref):
        @pl.when(pl.program_id(0) == 0)
        def _():
            o_ref[...] = jnp.zeros_like(o_ref)

        o_ref[...] += jnp.sum(a_ref[...].astype(F32), axis=0, keepdims=True)

    return pl.pallas_call(
        body, name=name, grid=(S // tr,),
        in_specs=[pl.BlockSpec((tr, N), lambda i: (i, 0))],
        out_specs=pl.BlockSpec((1, N), lambda i: (0, 0)),
        out_shape=jax.ShapeDtypeStruct((1, N), F32),
        compiler_params=_cparams("arbitrary"),
    )(a)


def rope_tables(seq, head_dim):
    rot = head_dim // ROPE_FRACTION
    half = rot // 2
    inv_freq = 1.0 / (ROPE_THETA ** (jnp.arange(0, rot, 2, dtype=F32) / rot))
    ang = jnp.arange(seq, dtype=F32)[:, None] * inv_freq[None, :]
    cos, sin = jnp.cos(ang), jnp.sin(ang)
    ones = jnp.ones((seq, LANES - 2 * half), F32)
    zeros = jnp.zeros((seq, LANES - 2 * half), F32)
    zh = jnp.zeros((seq, half), F32)
    c = jnp.concatenate([cos, cos, ones], axis=1)
    sa = jnp.concatenate([-sin, zh, zeros], axis=1)
    sb = jnp.concatenate([zh, sin, zeros], axis=1)
    return c, sa, sb, half


def _rope(x, c, sa, sb, half):
    return x * c + pltpu.roll(x, LANES - half, 1) * sa + pltpu.roll(x, half, 1) * sb


def _rope_bwd(d, c, sa, sb, half):
    return d * c + pltpu.roll(d * sa, half, 1) + pltpu.roll(d * sb, LANES - half, 1)


def _band_masks(n, max_dist):
    qi = lax.broadcasted_iota(jnp.int32, (BAND, BAND), 0)
    kj = lax.broadcasted_iota(jnp.int32, (BAND, BAND), 1)
    cur = kj <= qi
    prev = ((kj >= qi) if max_dist == BAND else (kj > qi)) & (n > 0)
    return prev, cur


def _dot_nt(a, b):
    return lax.dot_general(a, b, (((1,), (1,)), ((), ())), preferred_element_type=F32)


def _dot_tn(a, b):
    return lax.dot_general(a, b, (((0,), (0,)), ((), ())), preferred_element_type=F32)


def _dot(a, b):
    return jnp.dot(a, b, preferred_element_type=F32)


def band_attention_fwd(src, tabs, *, L, dil, n_heads, group, q_blk, k_blk, v_blk, blk_per_row, max_dist, scale,
                       half, sink=None, normalize, name):
    c_t, sa_t, sb_t = tabs
    nb = L // BAND
    W = dil * n_heads * LANES
    use_sink = sink is not None

    def body(*refs):
        q_ref, k_ref, v_ref, c_ref, sa_ref, sb_ref = refs[:6]
        pos = 6
        if use_sink:
            sink_ref = refs[pos]
            pos += 1
        o_ref, m_ref, l_ref = refs[pos:pos + 3] if not normalize else (refs[pos], refs[pos + 1], None)
        kr_ref, vb_ref = refs[-2], refs[-1]

        def prep(n, carry):
            rows = pl.ds(pl.multiple_of(n * BAND, BAND), BAND)
            kr_ref[rows, :] = _rope(k_ref[rows, :], c_ref[rows, :], sa_ref[rows, :], sb_ref[rows, :], half).astype(BF16)
            vb_ref[rows, :] = v_ref[rows, :].astype(BF16)
            return carry

        lax.fori_loop(0, nb, prep, 0)

        def step(n, carry):
            rows = pl.ds(pl.multiple_of(n * BAND, BAND), BAND)
            prow = pl.ds(pl.multiple_of(jnp.maximum(n - 1, 0) * BAND, BAND), BAND)
            q = _rope(q_ref[rows, :], c_ref[rows, :], sa_ref[rows, :], sb_ref[rows, :], half).astype(BF16)
            mp, mc = _band_masks(n, max_dist)
            sp = jnp.where(mp, _dot_nt(q, kr_ref[prow, :]) * scale, -jnp.inf)
            sc = jnp.where(mc, _dot_nt(q, kr_ref[rows, :]) * scale, -jnp.inf)
            m = jnp.maximum(jnp.max(sp, axis=-1, keepdims=True), jnp.max(sc, axis=-1, keepdims=True))
            if use_sink:
                sk = sink_ref[:, 0:1]
                m = jnp.maximum(m, sk)
            pp = jnp.exp(sp - m)
            pc = jnp.exp(sc - m)
            l = jnp.sum(pp, axis=-1, keepdims=True) + jnp.sum(pc, axis=-1, keepdims=True)
            if use_sink:
                l = l + jnp.exp(sk - m)
            num = _dot(pp.astype(BF16), vb_ref[prow, :]) + _dot(pc.astype(BF16), vb_ref[rows, :])
            if normalize:
                o_ref[rows, :] = num / l
                m_ref[rows, :] = jnp.broadcast_to(m + jnp.log(l), (BAND, LANES))
            else:
                o_ref[rows, :] = num
                m_ref[rows, :] = jnp.broadcast_to(m, (BAND, LANES))
                l_ref[rows, :] = jnp.broadcast_to(l, (BAND, LANES))
            return carry

        lax.fori_loop(0, nb, step, 0)

    def col(off, div):
        return pl.BlockSpec((L, LANES), lambda r, h: (0, r * blk_per_row + off + h // div))

    tab = pl.BlockSpec((L, LANES), lambda r, h: (0, r))
    out = pl.BlockSpec((L, LANES), lambda r, h: (0, r * n_heads + h))
    in_specs = [col(q_blk, 1), col(k_blk, group), col(v_blk, group), tab, tab, tab]
    args = [src, src, src, c_t, sa_t, sb_t]
    if use_sink:
        in_specs.append(pl.BlockSpec((None, 1, LANES), lambda r, h: (h, 0, 0)))
        args.append(sink)
    n_out = 2 if normalize else 3
    return pl.pallas_call(
        body, name=name, grid=(dil, n_heads),
        in_specs=in_specs,
        out_specs=[out] * n_out,
        out_shape=[jax.ShapeDtypeStruct((L, W), F32)] * n_out,
        scratch_shapes=[pltpu.VMEM((L, LANES), BF16), pltpu.VMEM((L, LANES), BF16)],
        compiler_params=_cparams("parallel", "arbitrary"),
    )(*args)


def band_attention_bwd(src, tabs, o, lse, do, *, L, dil, n_heads, group, q_blk, k_blk, v_blk, blk_per_row, max_dist,
                       scale, half, do_bpr, sink=None, acc=None, name):
    c_t, sa_t, sb_t = tabs
    nb = L // BAND
    n_kv = n_heads // group
    use_sink = sink is not None
    use_acc = acc is not None
    assert not (use_acc and group != 1)

    def body(*refs):
        q_ref, k_ref, v_ref, c_ref, sa_ref, sb_ref, o_ref, lse_ref, do_ref = refs[:9]
        pos = 9
        if use_sink:
            sink_ref = refs[pos]
            pos += 1
        if use_acc:
            aq_ref, ak_ref, av_ref = refs[pos:pos + 3]
            pos += 3
        dq_ref, dk_ref, dv_ref = refs[pos:pos + 3]
        pos += 3
        if use_sink:
            dsink_ref = refs[pos]
        kr_ref, vb_ref, dka_ref, dva_ref = refs[-4:]
        h = pl.program_id(1)

        def prep(n, carry):
            rows = pl.ds(pl.multiple_of(n * BAND, BAND), BAND)
            kr_ref[rows, :] = _rope(k_ref[rows, :], c_ref[rows, :], sa_ref[rows, :], sb_ref[rows, :], half).astype(BF16)
            vb_ref[rows, :] = v_ref[rows, :].astype(BF16)
            return carry

        lax.fori_loop(0, nb, prep, 0)

        @pl.when(h % group == 0)
        def _():
            dka_ref[...] = jnp.zeros_like(dka_ref)
            dva_ref[...] = jnp.zeros_like(dva_ref)

        def step(n, dsk):
            rows = pl.ds(pl.multiple_of(n * BAND, BAND), BAND)
            prow = pl.ds(pl.multiple_of(jnp.maximum(n - 1, 0) * BAND, BAND), BAND)
            cv, sav, sbv = c_ref[rows, :], sa_ref[rows, :], sb_ref[rows, :]
            q = _rope(q_ref[rows, :], cv, sav, sbv, half).astype(BF16)
            dov = do_ref[rows, :]
            lsev = lse_ref[rows, 0:1]
            delta = jnp.sum(dov * o_ref[rows, :], axis=-1, keepdims=True)
            dob = dov.astype(BF16)
            mp, mc = _band_masks(n, max_dist)
            kp, kc, vp, vc = kr_ref[prow, :], kr_ref[rows, :], vb_ref[prow, :], vb_ref[rows, :]
            pp = jnp.exp(jnp.where(mp, _dot_nt(q, kp) * scale, -jnp.inf) - lsev)
            pc = jnp.exp(jnp.where(mc, _dot_nt(q, kc) * scale, -jnp.inf) - lsev)
            dsp = (pp * (_dot_nt(dob, vp) - delta) * scale).astype(BF16)
            dsc = (pc * (_dot_nt(dob, vc) - delta) * scale).astype(BF16)
            dq = _rope_bwd(_dot(dsp, kp) + _dot(dsc, kc), cv, sav, sbv, half)
            if use_acc:
                dq = dq + aq_ref[rows, :]
            dq_ref[rows, :] = dq
            dka_ref[prow, :] += _dot_tn(dsp, q)
            dka_ref[rows, :] += _dot_tn(dsc, q)
            dva_ref[prow, :] += _dot_tn(pp.astype(BF16), dob)
            dva_ref[rows, :] += _dot_tn(pc.astype(BF16), dob)
            if use_sink:
                dsk = dsk - jnp.sum(jnp.exp(sink_ref[:, 0:1] - lsev) * delta, axis=0, keepdims=True)
            return dsk

        dsk = lax.fori_loop(0, nb, step, jnp.zeros((1, 1), F32))
        if use_sink:
            dsink_ref[...] = jnp.broadcast_to(dsk, dsink_ref.shape)

        @pl.when(h % group == group - 1)
        def _():
            def fin(n, carry):
                rows = pl.ds(pl.multiple_of(n * BAND, BAND), BAND)
                dk = _rope_bwd(dka_ref[rows, :], c_ref[rows, :], sa_ref[rows, :], sb_ref[rows, :], half)
                dv = dva_ref[rows, :]
                if use_acc:
                    dk = dk + ak_ref[rows, :]
                    dv = dv + av_ref[rows, :]
                dk_ref[rows, :] = dk
                dv_ref[rows, :] = dv
                return carry

            lax.fori_loop(0, nb, fin, 0)

    def col(off, div):
        return pl.BlockSpec((L, LANES), lambda r, h: (0, r * blk_per_row + off + h // div))

    tab = pl.BlockSpec((L, LANES), lambda r, h: (0, r))
    qo = pl.BlockSpec((L, LANES), lambda r, h: (0, r * n_heads + h))
    kvo = pl.BlockSpec((L, LANES), lambda r, h: (0, r * n_kv + h // group))
    dospec = pl.BlockSpec((L, LANES), lambda r, h: (0, r * do_bpr + h))
    in_specs = [col(q_blk, 1), col(k_blk, group), col(v_blk, group), tab, tab, tab, qo, qo, dospec]
    args = [src, src, src, c_t, sa_t, sb_t, o, lse, do]
    if use_sink:
        in_specs.append(pl.BlockSpec((None, 1, LANES), lambda r, h: (h, 0, 0)))
        args.append(sink)
    if use_acc:
        in_specs += [qo, kvo, kvo]
        args += list(acc)
    out_specs = [qo, kvo, kvo]
    out_shape = [jax.ShapeDtypeStruct((L, dil * n_heads * LANES), F32),
                 jax.ShapeDtypeStruct((L, dil * n_kv * LANES), F32),
                 jax.ShapeDtypeStruct((L, dil * n_kv * LANES), F32)]
    if use_sink:
        out_specs.append(pl.BlockSpec((None, 1, LANES), lambda r, h: (h, 0, 0)))
        out_shape.append(jax.ShapeDtypeStruct((n_heads, 1, LANES), F32))
    return pl.pallas_call(
        body, name=name, grid=(dil, n_heads),
        in_specs=in_specs, out_specs=out_specs, out_shape=out_shape,
        scratch_shapes=[pltpu.VMEM((L, LANES), BF16), pltpu.VMEM((L, LANES), BF16),
                        pltpu.VMEM((L, LANES), F32), pltpu.VMEM((L, LANES), F32)],
        compiler_params=_cparams("parallel", "arbitrary"),
    )(*args)


def banded_fwd(src, tabs, *, L, dil, n_heads, group, hb, q_blk, k_blk, v_blk, blk_per_row, max_dist, scale, half,
               sink=None, normalize, name):
    c_t, sa_t, sb_t = tabs
    nb = L // BAND
    kvb = hb // group
    n_hg = n_heads // hb
    W = dil * n_heads * LANES
    use_sink = sink is not None
    assert n_heads % hb == 0 and hb % group == 0
    assert (q_blk % hb, k_blk % kvb, v_blk % kvb) == (0, 0, 0) and (dil == 1 or blk_per_row % hb == 0)

    def body(*refs):
        q_ref, k_ref, v_ref, c_ref, sa_ref, sb_ref = refs[:6]
        pos = 6
        if use_sink:
            sink_ref = refs[pos]
            pos += 1
        o_ref, m_ref = refs[pos], refs[pos + 1]
        l_ref = None if normalize else refs[pos + 2]
        kr_ref, vb_ref = refs[-2], refs[-1]
        n = pl.program_id(2)

        @pl.when(n == 0)
        def _():
            def prep(b, carry):
                rows = pl.ds(pl.multiple_of(b * BAND, BAND), BAND)
                cv, sav, sbv = c_ref[rows, :], sa_ref[rows, :], sb_ref[rows, :]
                for j in range(kvb):
                    cols = slice(j * LANES, (j + 1) * LANES)
                    kr_ref[rows, cols] = _rope(k_ref[rows, cols].astype(F32), cv, sav, sbv, half).astype(BF16)
                    vb_ref[rows, cols] = v_ref[rows, cols].astype(BF16)
                return carry

            lax.fori_loop(0, nb, prep, 0)

        rows = pl.ds(pl.multiple_of(n * BAND, BAND), BAND)
        prow = pl.ds(pl.multiple_of(jnp.maximum(n - 1, 0) * BAND, BAND), BAND)
        cv, sav, sbv = c_ref[rows, :], sa_ref[rows, :], sb_ref[rows, :]
        mp, mc = _band_masks(n, max_dist)
        for i in range(hb):
            qc = slice(i * LANES, (i + 1) * LANES)
            kc = slice((i // group) * LANES, (i // group + 1) * LANES)
            q = _rope(q_ref[:, qc].astype(F32), cv, sav, sbv, half).astype(BF16)
            sp = jnp.where(mp, _dot_nt(q, kr_ref[prow, kc]) * scale, -jnp.inf)
            sc = jnp.where(mc, _dot_nt(q, kr_ref[rows, kc]) * scale, -jnp.inf)
            m = jnp.maximum(jnp.max(sp, axis=-1, keepdims=True), jnp.max(sc, axis=-1, keepdims=True))
            if use_sink:
                sk = sink_ref[i, :, 0:1]
                m = jnp.maximum(m, sk)
            pp = jnp.exp(sp - m)
            pc = jnp.exp(sc - m)
            l = jnp.sum(pp, axis=-1, keepdims=True) + jnp.sum(pc, axis=-1, keepdims=True)
            if use_sink:
                l = l + jnp.exp(sk - m)
            num = _dot(pp.astype(BF16), vb_ref[prow, kc]) + _dot(pc.astype(BF16), vb_ref[rows, kc])
            if normalize:
                o_ref[:, qc] = num / l
                m_ref[:, qc] = jnp.broadcast_to(m + jnp.log(l), (BAND, LANES))
            else:
                o_ref[:, qc] = num
                m_ref[:, qc] = jnp.broadcast_to(m, (BAND, LANES))
                l_ref[:, qc] = jnp.broadcast_to(l, (BAND, LANES))

    qspec = pl.BlockSpec((BAND, hb * LANES), lambda r, g, n: (n, (r * blk_per_row + q_blk) // hb + g))

    def kv(off):
        return pl.BlockSpec((L, kvb * LANES), lambda r, g, n: (0, (r * blk_per_row + off) // kvb + g))

    tab = pl.BlockSpec((L, LANES), lambda r, g, n: (0, r))
    out = pl.BlockSpec((BAND, hb * LANES), lambda r, g, n: (n, r * n_hg + g))
    in_specs = [qspec, kv(k_blk), kv(v_blk), tab, tab, tab]
    args = [src, src, src, c_t, sa_t, sb_t]
    if use_sink:
        in_specs.append(pl.BlockSpec((hb, 1, LANES), lambda r, g, n: (g, 0, 0)))
        args.append(sink)
    n_out = 2 if normalize else 3
    return pl.pallas_call(
        body, name=name, grid=(dil, n_hg, nb),
        in_specs=in_specs,
        out_specs=[out] * n_out,
        out_shape=[jax.ShapeDtypeStruct((L, W), F32)] * n_out,
        scratch_shapes=[pltpu.VMEM((L, kvb * LANES), BF16), pltpu.VMEM((L, kvb * LANES), BF16)],
        compiler_params=_cparams("parallel", "parallel", "arbitrary"),
    )(*args)


def banded_bwd(src, tabs, o, lse, do, *, L, dil, n_heads, group, hb, q_blk, k_blk, v_blk, blk_per_row, max_dist,
               scale, half, do_bpr, sink=None, acc=None, name):
    c_t, sa_t, sb_t = tabs
    nb = L // BAND
    kvb = hb // group
    n_hg = n_heads // hb
    n_kv = n_heads // group
    use_sink = sink is not None
    use_acc = acc is not None
    assert n_heads % hb == 0 and hb % group == 0 and do_bpr % hb == 0
    assert (q_blk % hb, k_blk % kvb, v_blk % kvb) == (0, 0, 0) and (dil == 1 or blk_per_row % hb == 0)

    def body(*refs):
        q_ref, k_ref, v_ref, c_ref, sa_ref, sb_ref, o_ref, lse_ref, do_ref = refs[:9]
        pos = 9
        if use_sink:
            sink_ref = refs[pos]
            pos += 1
        if use_acc:
            aq_ref, ak_ref, av_ref = refs[pos:pos + 3]
            pos += 3
        dq_ref, dk_ref, dv_ref = refs[pos:pos + 3]
        pos += 3
        if use_sink:
            dsink_ref = refs[pos]
        kr_ref, vb_ref, dka_ref, dva_ref = refs[-4:]
        n = pl.program_id(2)

        @pl.when(n == 0)
        def _():
            def prep(b, carry):
                rows = pl.ds(pl.multiple_of(b * BAND, BAND), BAND)
                cv, sav, sbv = c_ref[rows, :], sa_ref[rows, :], sb_ref[rows, :]
                for j in range(kvb):
                    cols = slice(j * LANES, (j + 1) * LANES)
                    kr_ref[rows, cols] = _rope(k_ref[rows, cols].astype(F32), cv, sav, sbv, half).astype(BF16)
                    vb_ref[rows, cols] = v_ref[rows, cols].astype(BF16)
                return carry

            lax.fori_loop(0, nb, prep, 0)
            dka_ref[...] = jnp.zeros_like(dka_ref)
            dva_ref[...] = jnp.zeros_like(dva_ref)
            if use_sink:
                dsink_ref[...] = jnp.zeros_like(dsink_ref)

        rows = pl.ds(pl.multiple_of(n * BAND, BAND), BAND)
        prow = pl.ds(pl.multiple_of(jnp.maximum(n - 1, 0) * BAND, BAND), BAND)
        cv, sav, sbv = c_ref[rows, :], sa_ref[rows, :], sb_ref[rows, :]
        mp, mc = _band_masks(n, max_dist)
        for i in range(hb):
            qc = slice(i * LANES, (i + 1) * LANES)
            kc = slice((i // group) * LANES, (i // group + 1) * LANES)
            q = _rope(q_ref[:, qc].astype(F32), cv, sav, sbv, half).astype(BF16)
            dov = do_ref[:, qc]
            lsev = lse_ref[:, i * LANES:i * LANES + 1]
            delta = jnp.sum(dov * o_ref[:, qc], axis=-1, keepdims=True)
            dob = dov.astype(BF16)
            kp, kcur, vp, vcur = kr_ref[prow, kc], kr_ref[rows, kc], vb_ref[prow, kc], vb_ref[rows, kc]
            pp = jnp.exp(jnp.where(mp, _dot_nt(q, kp) * scale, -jnp.inf) - lsev)
            pc = jnp.exp(jnp.where(mc, _dot_nt(q, kcur) * scale, -jnp.inf) - lsev)
            dsp = (pp * (_dot_nt(dob, vp) - delta) * scale).astype(BF16)
            dsc = (pc * (_dot_nt(dob, vcur) - delta) * scale).astype(BF16)
            dq = _rope_bwd(_dot(dsp, kp) + _dot(dsc, kcur), cv, sav, sbv, half)
            if use_acc:
                dq = dq + aq_ref[:, qc]
            dq_ref[:, qc] = dq
            dka_ref[prow, kc] += _dot_tn(dsp, q)
            dka_ref[rows, kc] += _dot_tn(dsc, q)
            dva_ref[prow, kc] += _dot_tn(pp.astype(BF16), dob)
            dva_ref[rows, kc] += _dot_tn(pc.astype(BF16), dob)
            if use_sink:
                dsk = jnp.sum(jnp.exp(sink_ref[i, :, 0:1] - lsev) * delta, axis=0, keepdims=True)
                dsink_ref[i] -= jnp.broadcast_to(dsk, (1, LANES))

        @pl.when(n == nb - 1)
        def _():
            def fin(b, carry):
                brow = pl.ds(pl.multiple_of(b * BAND, BAND), BAND)
                bc, bsa, bsb = c_ref[brow, :], sa_ref[brow, :], sb_ref[brow, :]
                for j in range(kvb):
                    cols = slice(j * LANES, (j + 1) * LANES)
                    dk = _rope_bwd(dka_ref[brow, cols], bc, bsa, bsb, half)
                    dv = dva_ref[brow, cols]
                    if use_acc:
                        dk = dk + ak_ref[brow, cols]
                        dv = dv + av_ref[brow, cols]
                    dk_ref[brow, cols] = dk
                    dv_ref[brow, cols] = dv
                return carry

            lax.fori_loop(0, nb, fin, 0)

    qspec = pl.BlockSpec((BAND, hb * LANES), lambda r, g, n: (n, (r * blk_per_row + q_blk) // hb + g))

    def kv(off):
        return pl.BlockSpec((L, kvb * LANES), lambda r, g, n: (0, (r * blk_per_row + off) // kvb + g))

    tab = pl.BlockSpec((L, LANES), lambda r, g, n: (0, r))
    qo = pl.BlockSpec((BAND, hb * LANES), lambda r, g, n: (n, r * n_hg + g))
    kvo = pl.BlockSpec((L, kvb * LANES), lambda r, g, n: (0, r * (n_kv // kvb) + g))
    dospec = pl.BlockSpec((BAND, hb * LANES), lambda r, g, n: (n, r * (do_bpr // hb) + g))
    in_specs = [qspec, kv(k_blk), kv(v_blk), tab, tab, tab, qo, qo, dospec]
    args = [src, src, src, c_t, sa_t, sb_t, o, lse, do]
    sink_spec = pl.BlockSpec((hb, 1, LANES), lambda r, g, n: (g, 0, 0))
    if use_sink:
        in_specs.append(sink_spec)
        args.append(sink)
    if use_acc:
        in_specs += [qo, kvo, kvo]
        args += list(acc)
    out_specs = [qo, kvo, kvo]
    out_shape = [jax.ShapeDtypeStruct((L, dil * n_heads * LANES), F32),
                 jax.ShapeDtypeStruct((L, dil * n_kv * LANES), F32),
                 jax.ShapeDtypeStruct((L, dil * n_kv * LANES), F32)]
    if use_sink:
        out_specs.append(sink_spec)
        out_shape.append(jax.ShapeDtypeStruct((n_heads, 1, LANES), F32))
    return pl.pallas_call(
        body, name=name, grid=(dil, n_hg, nb),
        in_specs=in_specs, out_specs=out_specs, out_shape=out_shape,
        scratch_shapes=[pltpu.VMEM((L, kvb * LANES), BF16), pltpu.VMEM((L, kvb * LANES), BF16),
                        pltpu.VMEM((L, kvb * LANES), F32), pltpu.VMEM((L, kvb * LANES), F32)],
        compiler_params=_cparams("parallel", "parallel", "arbitrary"),
    )(*args)


def merge_branches(parts, *, name, tr=256):
    S, W = parts[0][0].shape
    tr = _tile(S, tr)
    nbr = len(parts)

    def body(*refs):
        ins, (o_ref, lse_ref) = refs[:3 * nbr], refs[3 * nbr:]
        nums = [ins[3 * i][...] for i in range(nbr)]
        ms = [ins[3 * i + 1][...] for i in range(nbr)]
        ls = [ins[3 * i + 2][...] for i in range(nbr)]
        mx = functools.reduce(jnp.maximum, ms)
        ws = [jnp.exp(m - mx) for m in ms]
        num = sum(w * n for w, n in zip(ws, nums))
        den = sum(w * l for w, l in zip(ws, ls))
        o_ref[...] = num / den
        lse_ref[...] = mx + jnp.log(den)

    row = pl.BlockSpec((tr, W), lambda i: (i, 0))
    flat = [a for p in parts for a in p]
    return pl.pallas_call(
        body, name=name, grid=(S // tr,),
        in_specs=[row] * len(flat), out_specs=[row, row],
        out_shape=[jax.ShapeDtypeStruct((S, W), F32)] * 2,
        compiler_params=_cparams("parallel"),
    )(*flat)


def _sigmoid(x):
    return 1.0 / (1.0 + jnp.exp(-x))


def _tri(n, lower):
    r = lax.broadcasted_iota(jnp.int32, (n, n), 0)
    c = lax.broadcasted_iota(jnp.int32, (n, n), 1)
    return ((c <= r) if lower else (c >= r)).astype(F32)


def _dot_exact(a, b, trans_a=False, trans_b=False):
    dims = (((0,) if trans_a else (1,), (1,) if trans_b else (0,)), ((), ()))
    return lax.dot_general(a, b, dims, preferred_element_type=F32, precision=lax.Precision.HIGHEST)


def _hgrn_gates(qb, fb, lb):
    sq = _sigmoid(qb)
    q = qb * sq * (B_DIM ** -0.5)
    sf = _sigmoid(fb)
    gate = lb + (1.0 - lb) * sf
    return q, 1.0 - gate, gate, sq, sf


def _hgrn_intra_fwd(q_ref, k_ref, b_ref):
    C, n_sub = B_CHUNK, B_CHUNK // B_SUB
    b_all, k_all = b_ref[...], k_ref[...]
    srow = lax.broadcasted_iota(jnp.int32, (C, LANES), 0)
    lane = lax.broadcasted_iota(jnp.int32, (B_SUB, C), 1)
    trow = lax.broadcasted_iota(jnp.int32, (B_SUB, LANES), 0)
    blocks = []
    for i in range(n_sub):
        r0 = i * B_SUB
        qi, bi = q_ref[pl.ds(r0, B_SUB), :], b_ref[pl.ds(r0, B_SUB), :]
        if i == 0:
            a_i = jnp.zeros((B_SUB, C), F32)
        else:
            ref_b = b_ref[pl.ds(r0 - 1, 1), :]
            qt = (qi * jnp.exp(bi - ref_b)).astype(BF16)
            kt = jnp.where(srow < r0, k_all * jnp.exp(jnp.minimum(ref_b - b_all, 0.0)), 0.0).astype(BF16)
            a_i = _dot_nt(qt, kt)
        for sl in range(B_SUB):
            s = r0 + sl
            e = jnp.where(trow >= sl, jnp.exp(jnp.minimum(bi - b_ref[pl.ds(s, 1), :], 0.0)), 0.0)
            colv = jnp.sum(qi * k_ref[pl.ds(s, 1), :] * e, axis=-1, keepdims=True)
            a_i = a_i + jnp.where(lane == s, colv, 0.0)
        blocks.append(a_i)
    return jnp.concatenate(blocks, axis=0)


def hgrn_fwd(src, lb, norm_g, *, S, n_heads, hb, q_blk, f_blk, i_blk, g_blk, name):
    nc = S // B_CHUNK
    assert n_heads % hb == 0 and (q_blk % hb, f_blk % hb, i_blk % hb, g_blk % hb) == (0, 0, 0, 0)

    def body(qb_ref, fb_ref, ib_ref, gb_ref, lb_ref, ng_ref, out_ref, o_ref, st_ref, state_ref, q_s, k_s, b_s):
        state_ref[...] = jnp.zeros_like(state_ref)
        tril = _tri(B_CHUNK, True)

        def step(c, carry):
            rows = pl.ds(pl.multiple_of(c * B_CHUNK, B_CHUNK), B_CHUNK)
            for j in range(hb):
                cols = slice(j * LANES, (j + 1) * LANES)
                q, k, gate, _, _ = _hgrn_gates(qb_ref[rows, cols], fb_ref[rows, cols], lb_ref[j])
                b = _dot_exact(tril, jnp.log(gate))
                vf = ib_ref[rows, cols]
                v = vf.astype(BF16)
                q_s[j], k_s[j], b_s[j] = q, k, b
                st = state_ref[j]
                st_ref[j, c] = st
                a = _hgrn_intra_fwd(q_s.at[j], k_s.at[j], b_s.at[j])
                o = _dot(a.astype(BF16), v) + _dot_nt((q * jnp.exp(b)).astype(BF16), st.astype(BF16))
                b_last = b_s[j, pl.ds(B_CHUNK - 1, 1), :]
                state_ref[j] = st * jnp.exp(b_last) + _dot_exact(vf, k * jnp.exp(b_last - b), trans_a=True)
                o_ref[rows, cols] = o
                rstd = lax.rsqrt(jnp.mean(o * o, axis=-1, keepdims=True) + NORM_EPS)
                gb = gb_ref[rows, cols]
                out_ref[rows, cols] = o * rstd * ng_ref[...] * (gb * _sigmoid(gb))
            return carry

        lax.fori_loop(0, nc, step, 0)

    def col(off):
        return pl.BlockSpec((S, hb * LANES), lambda g: (0, off // hb + g))

    hv = pl.BlockSpec((hb, 1, LANES), lambda g: (g, 0, 0))
    return pl.pallas_call(
        body, name=name, grid=(n_heads // hb,),
        in_specs=[col(q_blk), col(f_blk), col(i_blk), col(g_blk), hv, pl.BlockSpec((1, LANES), lambda g: (0, 0))],
        out_specs=[col(0), col(0), pl.BlockSpec((hb, nc, LANES, LANES), lambda g: (g, 0, 0, 0))],
        out_shape=[jax.ShapeDtypeStruct((S, n_heads * LANES), F32), jax.ShapeDtypeStruct((S, n_heads * LANES), F32),
                   jax.ShapeDtypeStruct((n_heads, nc, LANES, LANES), F32)],
        scratch_shapes=[pltpu.VMEM((hb, LANES, LANES), F32)] + [pltpu.VMEM((hb, B_CHUNK, LANES), F32)] * 3,
        compiler_params=_cparams("parallel"),
    )(src, src, src, src, lb, norm_g.reshape(1, LANES))


def hgrn_bwd(src, lb, norm_g, o, states, dout, *, S, n_heads, hb, q_blk, f_blk, i_blk, g_blk, dout_blk, name):
    nc = S // B_CHUNK
    C, n_sub = B_CHUNK, B_CHUNK // B_SUB
    assert n_heads % hb == 0 and (q_blk % hb, f_blk % hb, i_blk % hb, g_blk % hb, dout_blk % hb) == (0, 0, 0, 0, 0)

    def body(qb_ref, fb_ref, ib_ref, gb_ref, lb_ref, ng_ref, o_ref, st_ref, dout_ref,
             dqb_ref, dfb_ref, dib_ref, dgb_ref, dlb_ref, dng_ref, dstate_ref, q_sh, k_sh, b_sh, dq_sh, dk_sh):
        @pl.when(pl.program_id(0) == 0)
        def _():
            dng_ref[...] = jnp.zeros_like(dng_ref)

        dstate_ref[...] = jnp.zeros_like(dstate_ref)
        tril = _tri(C, True)
        triu = _tri(C, False)
        ngv = ng_ref[...]
        srow = lax.broadcasted_iota(jnp.int32, (C, LANES), 0)
        lane = lax.broadcasted_iota(jnp.int32, (B_SUB, C), 1)
        trow = lax.broadcasted_iota(jnp.int32, (B_SUB, LANES), 0)
        causal = lax.broadcasted_iota(jnp.int32, (C, C), 1) <= lax.broadcasted_iota(jnp.int32, (C, C), 0)

        def one_head(j, c, rows, carry):
            dlog_carry, dlb_acc, dng_acc = carry
            cols = slice(j * LANES, (j + 1) * LANES)
            q_s, k_s, b_s, dq_s, dk_s = q_sh.at[j], k_sh.at[j], b_sh.at[j], dq_sh.at[j], dk_sh.at[j]
            lbv = lb_ref[j]
            qb, fb, gb = qb_ref[rows, cols], fb_ref[rows, cols], gb_ref[rows, cols]
            q, k, gate, sq, sf = _hgrn_gates(qb, fb, lbv)
            b = _dot_exact(tril, jnp.log(gate))
            vf = ib_ref[rows, cols]
            v = vf.astype(BF16)
            q_s[...], k_s[...], b_s[...] = q, k, b
            st = st_ref[j, c]
            dst = dstate_ref[j]

            ov = o_ref[rows, cols]
            dout = dout_ref[rows, cols]
            rstd = lax.rsqrt(jnp.mean(ov * ov, axis=-1, keepdims=True) + NORM_EPS)
            xhat = ov * rstd
            sg = _sigmoid(gb)
            dy = dout * (gb * sg)
            dgb_ref[rows, cols] = dout * (xhat * ngv) * (sg * (1.0 + gb * (1.0 - sg)))
            dng_acc = dng_acc + jnp.sum(dy * xhat, axis=0, keepdims=True)
            dxhat = dy * ngv
            do = rstd * (dxhat - xhat * jnp.mean(dxhat * xhat, axis=-1, keepdims=True))
            dob = do.astype(BF16)

            eb = jnp.exp(b)
            b_last = b_s[pl.ds(C - 1, 1), :]
            ebl = jnp.exp(b_last - b)
            a = _hgrn_intra_fwd(q_s, k_s, b_s)
            da = jnp.where(causal, _dot_exact(do, vf, trans_b=True), 0.0)
            dv = _dot_tn(a.astype(BF16), dob) + _dot_nt((k * ebl).astype(BF16), dst.astype(BF16))
            dq_s[...] = _dot_exact(do, st) * eb
            dk_s[...] = _dot_exact(vf, dst) * ebl
            dstate_ref[j] = dst * jnp.exp(b_last) + _dot_exact(do, q * eb, trans_a=True)
            b_all, k_all = b, k
            for i in range(n_sub):
                r0 = i * B_SUB
                blk = pl.ds(r0, B_SUB)
                qi, bi = q_s[blk, :], b_s[blk, :]
                da_i = da[r0:r0 + B_SUB, :]
                dq_i = jnp.zeros((B_SUB, LANES), F32)
                if i > 0:
                    ref_b = b_s[pl.ds(r0 - 1, 1), :]
                    eq = jnp.exp(bi - ref_b)
                    ek = jnp.where(srow < r0, jnp.exp(jnp.minimum(ref_b - b_all, 0.0)), 0.0)
                    da_off = jnp.where(lane < r0, da_i, 0.0)
                    dq_i = _dot_exact(da_off, k_all * ek) * eq
                    dk_s[...] += _dot_exact(da_off, qi * eq, trans_a=True) * ek
                for sl in range(B_SUB):
                    s = r0 + sl
                    e = jnp.where(trow >= sl, jnp.exp(jnp.minimum(bi - b_s[pl.ds(s, 1), :], 0.0)), 0.0)
                    dac = jnp.sum(jnp.where(lane == s, da_i, 0.0), axis=-1, keepdims=True)
                    dq_i = dq_i + dac * k_s[pl.ds(s, 1), :] * e
                    dk_s[pl.ds(s, 1), :] += jnp.sum(dac * qi * e, axis=0, keepdims=True)
                dq_s[blk, :] += dq_i
            dq, dk = dq_s[...], dk_s[...]
            db = q * dq - k * dk
            dlog = _dot_exact(triu, db) + dlog_carry
            dlog_carry = dlog_carry + jnp.sum(db, axis=0, keepdims=True)
            dgate = dlog / gate - dk
            dqb_ref[rows, cols] = dq * (B_DIM ** -0.5) * (sq * (1.0 + qb * (1.0 - sq)))
            dfb_ref[rows, cols] = dgate * (1.0 - lbv) * sf * (1.0 - sf)
            dib_ref[rows, cols] = dv
            dlb_acc = dlb_acc + jnp.sum(dgate * (1.0 - sf), axis=0, keepdims=True)
            return dlog_carry, dlb_acc, dng_acc

        def step(ci, carry):
            c = nc - 1 - ci
            rows = pl.ds(pl.multiple_of(c * C, C), C)
            return tuple(one_head(j, c, rows, carry[j]) for j in range(hb))

        z = jnp.zeros((1, LANES), F32)
        final = lax.fori_loop(0, nc, step, ((z, z, z),) * hb)
        for j in range(hb):
            dlb_ref[j] = final[j][1]
            dng_ref[...] += final[j][2]

    def col(off):
        return pl.BlockSpec((S, hb * LANES), lambda g: (0, off // hb + g), pipeline_mode=pl.Buffered(1))

    hv = pl.BlockSpec((hb, 1, LANES), lambda g: (g, 0, 0))
    vec = pl.BlockSpec((1, LANES), lambda g: (0, 0))
    full = jax.ShapeDtypeStruct((S, n_heads * LANES), F32)
    return pl.pallas_call(
        body, name=name, grid=(n_heads // hb,),
        in_specs=[col(q_blk), col(f_blk), col(i_blk), col(g_blk), hv, vec, col(0),
                  pl.BlockSpec((hb, nc, LANES, LANES), lambda g: (g, 0, 0, 0), pipeline_mode=pl.Buffered(1)),
                  col(dout_blk)],
        out_specs=[pl.BlockSpec((S, hb * LANES), lambda g: (0, g))] * 4 + [hv, vec],
        out_shape=[full, full, full, full, jax.ShapeDtypeStruct((n_heads, 1, LANES), F32),
                   jax.ShapeDtypeStruct((1, LANES), F32)],
        scratch_shapes=[pltpu.VMEM((hb, LANES, LANES), F32)] + [pltpu.VMEM((hb, B_CHUNK, LANES), F32)] * 5,
        compiler_params=_cparams("arbitrary"),
    )(src, src, src, src, lb, norm_g.reshape(1, LANES), o, states, dout)


def _pad_heads_cols(w, n_heads):
    lead = w.shape[:-1]
    w = w.reshape(lead + (n_heads, C_DIM))
    w = jnp.pad(w, [(0, 0)] * len(lead) + [(0, 0), (0, LANES - C_DIM)])
    return w.reshape(lead + (n_heads * LANES,))


def _unpad_heads_cols(w, n_heads):
    lead = w.shape[:-1]
    return w.reshape(lead + (n_heads, LANES))[..., :C_DIM].reshape(lead + (n_heads * C_DIM,))


def _pad_heads_rows(w, n_heads):
    w = jnp.pad(w.reshape(n_heads, C_DIM, w.shape[1]), [(0, 0), (0, LANES - C_DIM), (0, 0)])
    return w.reshape(n_heads * LANES, w.shape[2])


def _unpad_heads_rows(w, n_heads):
    return w.reshape(n_heads, LANES, w.shape[1])[:, :C_DIM].reshape(n_heads * C_DIM, w.shape[1])


def _lower_bounds(lb_raw):
    lb_soft = jax.nn.softmax(lb_raw.astype(F32), axis=0)
    return jnp.cumsum(lb_soft, axis=0) - lb_soft[0:1]


def _relu2(u):
    r = jnp.maximum(u, 0.0)
    return r * r


def local_step(x, target, P, hooks):
    S, D = x.shape
    depth = P["norm_mix_g"].shape[0]
    HA = D // 2 // A_DIM
    HB = HA
    HQ = D // C_DIM
    HKV = HQ // C_GROUP
    even_in = 7 * HA * LANES
    tabs_a = rope_tables(S, A_DIM)
    tabs_c = rope_tables(S, C_DIM)
    half_a, half_c = tabs_a[3], tabs_c[3]
    lbounds = _lower_bounds(P["hgrn_lb_raw"])
    a_in = 3 * HA * LANES
    a_kw = dict(n_heads=HA, group=1, hb=A_HEADS_PER_STEP, q_blk=0, k_blk=HA, v_blk=2 * HA, blk_per_row=3 * HA,
                scale=A_DIM ** -0.5, half=half_a)
    b_kw = dict(S=S, n_heads=HB, hb=B_HEADS_PER_STEP, q_blk=0, f_blk=HB, i_blk=2 * HB, g_blk=3 * HB)
    c_kw = dict(L=S, dil=1, n_heads=HQ, group=C_GROUP, hb=C_GROUP, q_blk=0, k_blk=HQ, v_blk=HQ + HKV,
                blk_per_row=HQ + 2 * HKV, max_dist=C_WINDOW - 1, scale=C_DIM ** -0.5, half=half_c)

    def a_tabs(dil):
        return tuple(t.reshape(S // dil, dil * LANES) for t in tabs_a[:3])

    saved = []
    for layer in range(depth):
        Wl = dict(hooks["weights_in"](layer, x))
        sv = {"x0": x, "W": Wl}
        idx = layer // 2
        h = rmsnorm_fwd(x, P["norm_mix_g"][layer], name="norm_mix_fwd")
        sv["h"] = h
        if layer % 2 == 0:
            qkv = matmul(h, Wl["even_w_in"], b_cols=(0, a_in), out_dtypes=(BF16,), name="even_in_fwd_a")
            proj = matmul(h, Wl["even_w_in"], b_cols=(a_in, even_in), out_dtypes=(F32,), name="even_in_fwd_b")
            parts = []
            sv["qkv"] = {dil: qkv.reshape(S // dil, dil * a_in) for _, dil in A_BRANCHES}
            for window, dil in A_BRANCHES:
                L = S // dil
                res = banded_fwd(sv["qkv"][dil], a_tabs(dil), L=L, dil=dil, max_dist=window // dil, normalize=False,
                                 name=f"dilated_fwd_d{dil}", **a_kw)
                parts.append(tuple(r.reshape(S, HA * LANES) for r in res))
            oa, lse = merge_branches(parts, name="dilated_merge")
            hooks["mid"](layer, oa)
            lb_e = lbounds[idx].reshape(HB, 1, LANES)
            ob, o_raw, states = hgrn_fwd(proj, lb_e, P["hgrn_norm_g"][idx], name="hgrn_fwd", **b_kw)
            Wl.update(hooks["weights_rest"](layer, ob))
            mixed = jnp.concatenate([oa, ob], axis=1).astype(BF16)
            sv.update(proj=proj, oa=oa, lse=lse, o_raw=o_raw, states=states, mixed=mixed, lb=lb_e)
            x = matmul(mixed, Wl["even_w_out"], extras=(x,), epilogue=lambda acc, r: (acc + r,),
                       out_dtypes=(F32,), name="even_out_fwd")
        else:
            proj = matmul(h, Wl["odd_w_qkv"], extras=(P["odd_b_qkv"][idx].reshape(1, -1),),
                          epilogue=lambda acc, b: (acc + b,), out_dtypes=(F32,), name="odd_qkv_fwd")
            sink = jnp.broadcast_to(P["odd_sinks"][idx].reshape(HQ, 1, 1), (HQ, 1, LANES))
            o, lse = banded_fwd(proj, tabs_c[:3], sink=sink, normalize=True, name="swa_fwd", **c_kw)
            hooks["mid"](layer, o)
            Wl.update(hooks["weights_rest"](layer, o))
            sv.update(proj=proj, o=o, lse=lse, sink=sink)
            x = matmul(o, Wl["odd_w_o"], extras=(P["odd_b_o"][idx].reshape(1, D), x),
                       epilogue=lambda acc, b, r: (acc + b + r,), out_dtypes=(F32,), name="odd_out_fwd")
        sv["x1"] = x
        h2 = rmsnorm_fwd(x, P["norm_mlp_g"][layer], dep=hooks["after_mixer"](layer, x), name="norm_mlp_fwd")
        u = matmul(h2, Wl["mlp_w1"], out_dtypes=(BF16,), name="mlp_up_fwd")
        x = matmul(u, Wl["mlp_w2"], a_fn=_relu2, extras=(x,), epilogue=lambda acc, r: (acc + r,),
                   out_dtypes=(F32,), name="mlp_down_fwd")
        sv.update(h2=h2, u=u)
        saved.append(sv)

    loss, dx, dgf = final_norm_loss(x, P["final_norm_g"], target, name="final_norm_loss")

    G = {k: [None] * depth for k in ("norm_mix_g", "norm_mlp_g")}
    for k in ("hgrn_lb", "hgrn_norm_g"):
        G[k] = [None] * ((depth + 1) // 2)
    for k in ("odd_w_qkv", "odd_b_qkv", "odd_sinks", "odd_w_o", "odd_b_o"):
        G[k] = [None] * (depth // 2)
    G["final_norm_g"] = dgf.reshape(D)
    B = {}

    def wgrad(a, d, key, idx, name, a_fn=None):
        M, N = a.shape[1], d.shape[1]
        into = (grad_buffer_shape((M, N), SHARDING[key]), SHARDING[key])
        B[key, idx] = matmul(a, d, trans_a=True, a_fn=a_fn, out_dtypes=(BF16,), into=into, name=name)

    dep = None
    for layer in reversed(range(depth)):
        sv = saved[layer]
        Wl = sv["W"]
        idx = layer // 2
        du = matmul(dx, Wl["mlp_w2"], trans_b=True, extras=(sv["u"],), dep=dep,
                    epilogue=lambda acc, u: (acc * (2.0 * jnp.maximum(u.astype(F32), 0.0)),),
                    out_dtypes=(BF16,), name="mlp_down_bwd_x")
        wgrad(sv["u"], dx, "mlp_w2", layer, "mlp_down_bwd_w", a_fn=_relu2)
        wgrad(sv["h2"], du, "mlp_w1", layer, "mlp_up_bwd_w")
        dh2 = matmul(du, Wl["mlp_w1"], trans_b=True, out_dtypes=(F32,), name="mlp_up_bwd_x")
        dx, dg = rmsnorm_bwd(sv["x1"], P["norm_mlp_g"][layer], dh2, dx, name="norm_mlp_bwd")
        G["norm_mlp_g"][layer] = dg.reshape(D)
        dep = hooks["grads_ready"]([("mlp_w2", layer), ("mlp_w1", layer)], B, dx)
        if layer % 2 == 0:
            dmixed = matmul(dx, Wl["even_w_out"], trans_b=True, dep=dep, out_dtypes=(F32,), name="even_out_bwd_x")
            wgrad(sv["mixed"], dx, "even_w_out", idx, "even_out_bwd_w")
            acc = None
            for window, dil in A_BRANCHES:
                L = S // dil
                rs = lambda a: a.reshape(L, -1)
                acc = banded_bwd(sv["qkv"][dil], a_tabs(dil), rs(sv["oa"]), rs(sv["lse"]), rs(dmixed),
                                         L=L, dil=dil, max_dist=window // dil, do_bpr=2 * HA,
                                         acc=None if acc is None else tuple(rs(a) for a in acc),
                                         name=f"dilated_bwd_d{dil}", **a_kw)
                acc = tuple(a.reshape(S, HA * LANES) for a in acc)
            dqb, dfb, dib, dgb, dlb, dng = hgrn_bwd(sv["proj"], sv["lb"], P["hgrn_norm_g"][idx], sv["o_raw"],
                                                    sv["states"], dmixed, dout_blk=HA, name="hgrn_bwd", **b_kw)
            G["hgrn_lb"][idx] = dlb.reshape(HB * LANES)
            G["hgrn_norm_g"][idx] = dng.reshape(LANES)
            dproj = jnp.concatenate(list(acc) + [dqb, dfb, dib, dgb], axis=1).astype(BF16)
            wgrad(sv["h"], dproj, "even_w_in", idx, "even_in_bwd_w")
            dh = matmul(dproj, Wl["even_w_in"], trans_b=True, out_dtypes=(F32,), name="even_in_bwd_x")
        else:
            do = matmul(dx, Wl["odd_w_o"], trans_b=True, dep=dep, out_dtypes=(F32,), name="odd_out_bwd_x")
            G["odd_b_o"][idx] = colsum(dx, name="odd_out_bwd_b").reshape(D)
            G["odd_w_o"][idx] = matmul(sv["o"], dx, trans_a=True, out_dtypes=(BF16,), name="odd_out_bwd_w")
            dq, dk, dv, dsink = banded_bwd(sv["proj"], tabs_c[:3], sv["o"], sv["lse"], do, sink=sv["sink"],
                                                   do_bpr=HQ, name="swa_bwd", **c_kw)
            G["odd_sinks"][idx] = dsink[:, 0, 0]
            dproj = jnp.concatenate([dq, dk, dv], axis=1)
            G["odd_b_qkv"][idx] = colsum(dproj, name="odd_qkv_bwd_b").reshape(-1)
            dproj = dproj.astype(BF16)
            G["odd_w_qkv"][idx] = matmul(sv["h"], dproj, trans_a=True, out_dtypes=(BF16,), name="odd_qkv_bwd_w")
            dh = matmul(dproj, Wl["odd_w_qkv"], trans_b=True, out_dtypes=(F32,), name="odd_qkv_bwd_x")
            HQ2 = HQ + 2 * HKV
            B["odd_w_qkv", idx] = full_to_buffer(_unpad_heads_cols(G["odd_w_qkv"][idx], HQ2), "cols")
            B["odd_w_o", idx] = full_to_buffer(_unpad_heads_rows(G["odd_w_o"][idx], HQ), "rows")
        dx, dg = rmsnorm_bwd(sv["x0"], P["norm_mix_g"][layer], dh, dx, name="norm_mix_bwd")
        G["norm_mix_g"][layer] = dg.reshape(D)
        mixer = ("even_w_out", "even_w_in") if layer % 2 == 0 else ("odd_w_o", "odd_w_qkv")
        dep = hooks["grads_ready"]([(k, idx) for k in mixer], B, dx)

    del G["odd_w_qkv"], G["odd_w_o"]
    grads = {k: jnp.stack(v) if isinstance(v, list) else v for k, v in G.items()}
    _, lb_vjp = jax.vjp(_lower_bounds, P["hgrn_lb_raw"])
    grads["hgrn_lb_raw"] = lb_vjp(grads.pop("hgrn_lb"))[0]
    grads["odd_b_qkv"] = _unpad_heads_cols(grads["odd_b_qkv"], HQ + 2 * HKV)
    return loss, dx, grads, B


def full_to_buffer(full, sharding):
    M, N = full.shape
    if sharding == "cols":
        return jnp.transpose(full.reshape(2, M // 2, 4, N // 4), (2, 0, 1, 3))
    return full.reshape(4, 2, M // 8, N)


MESH = pl.DeviceIdType.MESH
ANY = pl.BlockSpec(memory_space=pl.ANY)
SHARDING = {"even_w_in": "cols", "even_w_out": "rows", "odd_w_qkv": "cols", "odd_w_o": "rows",
            "mlp_w1": "cols", "mlp_w2": "rows"}
BIG = tuple(SHARDING)


def _mesh_pos():
    return lax.axis_index("x"), lax.axis_index("y"), lax.axis_index("c")


def _other_chips(x, y):
    return [(1 - x, y), (x, 1 - y), (1 - x, 1 - y)]


def cast_into_full(w, layer, sharding, s, *, name, tr=256):
    n, R, C = w.shape
    tr = _tile(R, tr)
    nr = R // tr

    def body(s_ref, w_ref, o_ref):
        o_ref[...] = w_ref[...].astype(BF16)

    if sharding == "cols":
        full, out_map = (R, 4 * C), (lambda i, s_ref: (i, s_ref[0]))
    else:
        full, out_map = (4 * R, C), (lambda i, s_ref: (s_ref[0] * nr + i, 0))
    grid_spec = pltpu.PrefetchScalarGridSpec(
        num_scalar_prefetch=1, grid=(nr,),
        in_specs=[pl.BlockSpec((None, tr, C), lambda i, s_ref: (layer, i, 0))],
        out_specs=pl.BlockSpec((tr, C), out_map))
    return pl.pallas_call(
        body, name=name, grid_spec=grid_spec, out_shape=jax.ShapeDtypeStruct(full, BF16),
        compiler_params=_cparams("parallel"),
    )(s.reshape(1).astype(jnp.int32), w)


HBM = pl.BlockSpec(memory_space=pltpu.HBM)
SEM = pl.BlockSpec(memory_space=pltpu.SEMAPHORE)
EFFECT = pltpu.SideEffectType.DATAFLOW_SIDE_EFFECTING


def _weight_window(ref, sharding, s, hf):
    M, N = ref.shape
    if sharding == "cols":
        rh, C = M // 2, N // 4
        return ref.at[pl.ds(hf * rh, rh), pl.ds(s * C, C)]
    rh = M // 8
    return ref.at[pl.ds((2 * s + hf) * rh, rh), :]


def _in_hbm(a):
    return pltpu.with_memory_space_constraint(a, pltpu.HBM)


def gather_start(groups, shardings, after, *, name):
    sizes = [len(g) for g in groups]
    flat = [m for g in groups for m in g]
    flat_sh = [sh for g in shardings for sh in g]
    T, ng = len(flat), len(groups)

    def body(*refs):
        sems = refs[T + 1:T + 1 + 6 * ng]
        thru = refs[T + 1 + 6 * ng:2 * T + 1 + 6 * ng]
        token = refs[2 * T + 1 + 6 * ng]
        x, y, c = _mesh_pos()
        pos = 0
        for g in range(ng):
            for t in range(sizes[g]):
                w = _weight_window(thru[pos], flat_sh[pos], 2 * x + y, c)
                for j, (cx, cy) in enumerate(_other_chips(x, y)):
                    pltpu.make_async_remote_copy(src_ref=w, dst_ref=w, send_sem=sems[6 * g + j],
                                                 recv_sem=sems[6 * g + 3 + j], device_id=(cx, cy, c),
                                                 device_id_type=MESH).start()
                pos += 1
        token[...] = jnp.zeros_like(token)

    outs = pl.pallas_call(
        body, name=name,
        in_specs=[HBM] * T + [ANY],
        out_specs=[SEM] * (6 * ng) + [HBM] * T + [pl.BlockSpec(memory_space=pltpu.VMEM)],
        out_shape=[pltpu.SemaphoreType.DMA(())] * (6 * ng) + [pltpu.HBM(m.shape, m.dtype) for m in flat]
        + [jax.ShapeDtypeStruct((8, LANES), F32)],
        input_output_aliases={t: 6 * ng + t for t in range(T)},
        compiler_params=pltpu.CompilerParams(has_side_effects=EFFECT),
    )(*[_in_hbm(m) for m in flat], after)
    res, pos = [], 6 * ng
    for g in range(ng):
        res.append((list(outs[6 * g:6 * g + 6]), list(outs[pos:pos + sizes[g]])))
        pos += sizes[g]
    return res, outs[-1]


def gather_forward(mats, shardings, sems, after, *, name):
    T = len(mats)

    def body(*refs):
        sems1 = refs[T:T + 6]
        sems2 = refs[T + 7:T + 13]
        thru = refs[T + 13:2 * T + 13]
        token = refs[2 * T + 13]
        x, y, c = _mesh_pos()
        chips = _other_chips(x, y)
        for t in range(T):
            own = _weight_window(thru[t], shardings[t], 2 * x + y, c)
            for j, (cx, cy) in enumerate(chips):
                landed = _weight_window(thru[t], shardings[t], 2 * cx + cy, c)
                first = pltpu.make_async_remote_copy(src_ref=own, dst_ref=landed, send_sem=sems1[j],
                                                     recv_sem=sems1[3 + j], device_id=(cx, cy, c),
                                                     device_id_type=MESH)
                first.wait_recv()
                first.wait_send()
        for t in range(T):
            for j, (cx, cy) in enumerate(chips):
                landed = _weight_window(thru[t], shardings[t], 2 * cx + cy, c)
                pltpu.make_async_remote_copy(src_ref=landed, dst_ref=landed, send_sem=sems2[j],
                                             recv_sem=sems2[3 + j], device_id=(x, y, 1 - c),
                                             device_id_type=MESH).start()
        token[...] = jnp.zeros_like(token)

    outs = pl.pallas_call(
        body, name=name,
        in_specs=[HBM] * T + [SEM] * 6 + [ANY],
        out_specs=[SEM] * 6 + [HBM] * T + [pl.BlockSpec(memory_space=pltpu.VMEM)],
        out_shape=[pltpu.SemaphoreType.DMA(())] * 6 + [pltpu.HBM(m.shape, m.dtype) for m in mats]
        + [jax.ShapeDtypeStruct((8, LANES), F32)],
        input_output_aliases={t: 6 + t for t in range(T)},
        compiler_params=pltpu.CompilerParams(has_side_effects=EFFECT),
    )(*mats, *sems, after)
    return list(outs[:6]), list(outs[6:6 + T]), outs[-1]


def gather_finish(mats, shardings, sems, after, *, name):
    T = len(mats)

    def body(*refs):
        sems2 = refs[T:T + 6]
        thru = refs[T + 7:2 * T + 7]
        x, y, c = _mesh_pos()
        for t in range(T):
            for j, (cx, cy) in enumerate(_other_chips(x, y)):
                sent = _weight_window(thru[t], shardings[t], 2 * cx + cy, c)
                other = _weight_window(thru[t], shardings[t], 2 * cx + cy, 1 - c)
                cp = pltpu.make_async_remote_copy(src_ref=sent, dst_ref=other, send_sem=sems2[j],
                                                  recv_sem=sems2[3 + j], device_id=(x, y, 1 - c),
                                                  device_id_type=MESH)
                cp.wait_recv()
                cp.wait_send()

    outs = pl.pallas_call(
        body, name=name,
        in_specs=[HBM] * T + [SEM] * 6 + [ANY],
        out_specs=[HBM] * T,
        out_shape=[pltpu.HBM(m.shape, m.dtype) for m in mats],
        input_output_aliases={t: t for t in range(T)},
        compiler_params=pltpu.CompilerParams(has_side_effects=EFFECT),
    )(*mats, *sems, after)
    return list(outs)


def exchange_halves(bufs, *, name):
    T = len(bufs)

    def body(*refs):
        in_refs, out_refs = refs[:T], refs[T:2 * T]
        send_sems, recv_sems = refs[2 * T:]
        x, y, c = _mesh_pos()
        cps = [pltpu.make_async_remote_copy(src_ref=in_refs[t].at[:, pl.ds(1 - c, 1)], dst_ref=out_refs[t],
                                            send_sem=send_sems.at[t], recv_sem=recv_sems.at[t],
                                            device_id=(x, y, 1 - c), device_id_type=MESH) for t in range(T)]
        for cp in cps:
            cp.start()
        for cp in cps:
            cp.wait()

    return pl.pallas_call(
        body, name=name, in_specs=[ANY] * T, out_specs=[ANY] * T,
        out_shape=[jax.ShapeDtypeStruct((4, 1) + b.shape[2:], b.dtype) for b in bufs],
        scratch_shapes=[pltpu.SemaphoreType.DMA((T,)), pltpu.SemaphoreType.DMA((T,))],
    )(*bufs)


def pair_sum(buf, got, c, *, name, tr=512):
    _, _, n, rh, C = buf.shape
    tr = _tile(rh, tr)

    def body(c_ref, a_ref, b_ref, o_ref):
        o_ref[...] = (a_ref[...].astype(F32) + b_ref[...].astype(F32)).astype(o_ref.dtype)

    grid_spec = pltpu.PrefetchScalarGridSpec(
        num_scalar_prefetch=1, grid=(4, n, rh // tr),
        in_specs=[pl.BlockSpec((None, None, None, tr, C), lambda s, l, i, c_ref: (s, c_ref[0], l, i, 0)),
                  pl.BlockSpec((None, None, None, tr, C), lambda s, l, i, c_ref: (s, 0, l, i, 0))],
        out_specs=pl.BlockSpec((None, None, tr, C), lambda s, l, i, c_ref: (s, l, i, 0)))
    return pl.pallas_call(
        body, name=name, grid_spec=grid_spec, out_shape=jax.ShapeDtypeStruct((4, n, rh, C), BF16),
        compiler_params=_cparams("parallel", "parallel", "parallel"),
    )(c.reshape(1).astype(jnp.int32), buf, got)


def scatter_to_owners(sums, *, name):
    T = len(sums)

    def body(*refs):
        in_refs, out_refs = refs[:T], refs[T:2 * T]
        send_sems, recv_sems = refs[2 * T:]
        x, y, c = _mesh_pos()
        chips = _other_chips(x, y)
        cps = [pltpu.make_async_remote_copy(src_ref=in_refs[t].at[pl.ds(2 * cx + cy, 1)],
                                            dst_ref=out_refs[t].at[pl.ds(j, 1)],
                                            send_sem=send_sems.at[t, j], recv_sem=recv_sems.at[t, j],
                                            device_id=(cx, cy, c), device_id_type=MESH)
               for t in range(T) for j, (cx, cy) in enumerate(chips)]
        for cp in cps:
            cp.start()
        for cp in cps:
            cp.wait()

    return pl.pallas_call(
        body, name=name, in_specs=[ANY] * T, out_specs=[ANY] * T,
        out_shape=[jax.ShapeDtypeStruct((3,) + s.shape[1:], s.dtype) for s in sums],
        scratch_shapes=[pltpu.SemaphoreType.DMA((T, 3)), pltpu.SemaphoreType.DMA((T, 3))],
    )(*sums)


def exchange_start(bufs, *, name):
    T = len(bufs)
    lands = [_in_hbm(lax.empty((4, 1) + b.shape[2:], b.dtype)) for b in bufs]

    def body(*refs):
        send_sem, recv_sem = refs[2 * T], refs[2 * T + 1]
        src, dst = refs[2 * T + 2:3 * T + 2], refs[3 * T + 2:4 * T + 2]
        token = refs[4 * T + 2]
        x, y, c = _mesh_pos()
        for t in range(T):
            pltpu.make_async_remote_copy(src_ref=src[t].at[:, pl.ds(1 - c, 1)], dst_ref=dst[t], send_sem=send_sem,
                                         recv_sem=recv_sem, device_id=(x, y, 1 - c), device_id_type=MESH).start()
        token[...] = jnp.zeros_like(token)

    outs = pl.pallas_call(
        body, name=name,
        in_specs=[HBM] * (2 * T),
        out_specs=[SEM] * 2 + [HBM] * (2 * T) + [pl.BlockSpec(memory_space=pltpu.VMEM)],
        out_shape=[pltpu.SemaphoreType.DMA(())] * 2 + [pltpu.HBM(a.shape, a.dtype) for a in list(bufs) + lands]
        + [jax.ShapeDtypeStruct((8, LANES), F32)],
        input_output_aliases={t: 2 + t for t in range(2 * T)},
        compiler_params=pltpu.CompilerParams(has_side_effects=EFFECT),
    )(*[_in_hbm(b) for b in bufs], *lands)
    return list(outs[:2]), list(outs[2:2 + T]), list(outs[2 + T:2 + 2 * T]), outs[-1]


def exchange_wait(bufs, lands, sems, after, *, name):
    T = len(bufs)

    def body(*refs):
        send_sem, recv_sem = refs[2 * T], refs[2 * T + 1]
        src, dst = refs[2 * T + 3:3 * T + 3], refs[3 * T + 3:4 * T + 3]
        x, y, c = _mesh_pos()
        for t in range(T):
            cp = pltpu.make_async_remote_copy(src_ref=src[t].at[:, pl.ds(1 - c, 1)], dst_ref=dst[t],
                                              send_sem=send_sem, recv_sem=recv_sem, device_id=(x, y, 1 - c),
                                              device_id_type=MESH)
            cp.wait_recv()
            cp.wait_send()

    outs = pl.pallas_call(
        body, name=name,
        in_specs=[HBM] * (2 * T) + [SEM] * 2 + [ANY],
        out_specs=[HBM] * (2 * T),
        out_shape=[pltpu.HBM(a.shape, a.dtype) for a in list(bufs) + list(lands)],
        input_output_aliases={t: t for t in range(2 * T)},
        compiler_params=pltpu.CompilerParams(has_side_effects=EFFECT),
    )(*bufs, *lands, *sems, after)
    return list(outs[:T]), list(outs[T:])


def scatter_start(sums, *, name):
    T = len(sums)
    lands = [_in_hbm(lax.empty((3,) + s.shape[1:], s.dtype)) for s in sums]

    def body(*refs):
        sems = refs[2 * T:2 * T + 6]
        src, dst = refs[2 * T + 6:3 * T + 6], refs[3 * T + 6:4 * T + 6]
        token = refs[4 * T + 6]
        x, y, c = _mesh_pos()
        for t in range(T):
            for j, (cx, cy) in enumerate(_other_chips(x, y)):
                pltpu.make_async_remote_copy(src_ref=src[t].at[pl.ds(2 * cx + cy, 1)], dst_ref=dst[t].at[pl.ds(j, 1)],
                                             send_sem=sems[j], recv_sem=sems[3 + j], device_id=(cx, cy, c),
                                             device_id_type=MESH).start()
        token[...] = jnp.zeros_like(token)

    outs = pl.pallas_call(
        body, name=name,
        in_specs=[HBM] * (2 * T),
        out_specs=[SEM] * 6 + [HBM] * (2 * T) + [pl.BlockSpec(memory_space=pltpu.VMEM)],
        out_shape=[pltpu.SemaphoreType.DMA(())] * 6 + [pltpu.HBM(a.shape, a.dtype) for a in list(sums) + lands]
        + [jax.ShapeDtypeStruct((8, LANES), F32)],
        input_output_aliases={t: 6 + t for t in range(2 * T)},
        compiler_params=pltpu.CompilerParams(has_side_effects=EFFECT),
    )(*[_in_hbm(s) for s in sums], *lands)
    return list(outs[:6]), list(outs[6:6 + T]), list(outs[6 + T:6 + 2 * T]), outs[-1]


def scatter_wait(sums, lands, sems, after, *, name):
    T = len(sums)

    def body(*refs):
        sem_refs = refs[2 * T:2 * T + 6]
        src, dst = refs[2 * T + 7:3 * T + 7], refs[3 * T + 7:4 * T + 7]
        x, y, c = _mesh_pos()
        for t in range(T):
            for j, (cx, cy) in enumerate(_other_chips(x, y)):
                cp = pltpu.make_async_remote_copy(src_ref=src[t].at[pl.ds(2 * cx + cy, 1)],
                                                  dst_ref=dst[t].at[pl.ds(j, 1)], send_sem=sem_refs[j],
                                                  recv_sem=sem_refs[3 + j], device_id=(cx, cy, c),
                                                  device_id_type=MESH)
                cp.wait_recv()
                cp.wait_send()

    outs = pl.pallas_call(
        body, name=name,
        in_specs=[HBM] * (2 * T) + [SEM] * 6 + [ANY],
        out_specs=[HBM] * (2 * T),
        out_shape=[pltpu.HBM(a.shape, a.dtype) for a in list(sums) + list(lands)],
        input_output_aliases={t: t for t in range(2 * T)},
        compiler_params=pltpu.CompilerParams(has_side_effects=EFFECT),
    )(*sums, *lands, *sems, after)
    return list(outs[:T]), list(outs[T:])


def owner_sum(sums, got, s, c, grad, layer, *, name, tr=512):
    rh, C = sums.shape[2:]
    tr = _tile(rh, tr)
    nr = rh // tr

    def body(sc_ref, a_ref, b0_ref, b1_ref, b2_ref, g_ref, o_ref):
        o_ref[...] = ((a_ref[...].astype(F32) + b0_ref[...].astype(F32)) + b1_ref[...].astype(F32)) \
            + b2_ref[...].astype(F32)

    def got_spec(j):
        return pl.BlockSpec((None, None, tr, C), lambda i, sc_ref: (j, 0, i, 0))

    grid_spec = pltpu.PrefetchScalarGridSpec(
        num_scalar_prefetch=1, grid=(nr,),
        in_specs=[pl.BlockSpec((None, None, tr, C), lambda i, sc_ref: (sc_ref[0], 0, i, 0)),
                  got_spec(0), got_spec(1), got_spec(2), ANY],
        out_specs=pl.BlockSpec((None, tr, C), lambda i, sc_ref: (layer, sc_ref[1] * nr + i, 0)))
    return pl.pallas_call(
        body, name=name, grid_spec=grid_spec, out_shape=jax.ShapeDtypeStruct(grad.shape, F32),
        input_output_aliases={5: 0},
        compiler_params=_cparams("parallel"),
    )(jnp.stack([s, c]).astype(jnp.int32), sums, got, got, got, grad)


def join_halves(grads, *, name):
    T = len(grads)

    def body(*refs):
        out_refs = refs[T:2 * T]
        send_sems, recv_sems = refs[2 * T:]
        x, y, c = _mesh_pos()

        def win(t, hf):
            rh = grads[t].shape[1] // 2
            return out_refs[t].at[:, pl.ds(hf * rh, rh), :]

        def remote(t, w):
            return pltpu.make_async_remote_copy(src_ref=w, dst_ref=w, send_sem=send_sems.at[t],
                                                recv_sem=recv_sems.at[t], device_id=(x, y, 1 - c),
                                                device_id_type=MESH)

        cps = [remote(t, win(t, c)) for t in range(T)]
        for cp in cps:
            cp.start()
        for t in range(T):
            remote(t, win(t, 1 - c)).wait_recv()
        for cp in cps:
            cp.wait_send()

    return pl.pallas_call(
        body, name=name, in_specs=[ANY] * T, out_specs=[ANY] * T,
        out_shape=[jax.ShapeDtypeStruct(g.shape, g.dtype) for g in grads],
        input_output_aliases={t: t for t in range(T)},
        scratch_shapes=[pltpu.SemaphoreType.DMA((T,)), pltpu.SemaphoreType.DMA((T,))],
    )(*grads)


def allreduce_small(v, *, name):
    rows = v.shape[0]

    def body(v_ref, out_ref, buf, send_sems, recv_sems):
        x, y, c = _mesh_pos()
        me = 4 * x + 2 * y + c
        flips = [(dx, dy, dc) for dx in (0, 1) for dy in (0, 1) for dc in (0, 1)][1:]

        def peer(f):
            return tuple(1 - p if d else p for d, p in zip(f, (x, y, c)))

        cps = []
        for k, f in enumerate(flips):
            px, py, pc = peer(f)
            cps.append(pltpu.make_async_remote_copy(src_ref=v_ref, dst_ref=buf.at[me], send_sem=send_sems.at[k],
                                                    recv_sem=recv_sems.at[k], device_id=(px, py, pc),
                                                    device_id_type=MESH))
        for cp in cps:
            cp.start()
        buf[me] = v_ref[...]
        for k, f in enumerate(flips):
            px, py, pc = peer(f)
            slot = buf.at[4 * px + 2 * py + pc]
            pltpu.make_async_remote_copy(src_ref=slot, dst_ref=slot, send_sem=send_sems.at[k],
                                         recv_sem=recv_sems.at[k], device_id=(px, py, pc),
                                         device_id_type=MESH).wait_recv()
        for cp in cps:
            cp.wait_send()
        acc = buf[0]
        for i in range(1, 8):
            acc = acc + buf[i]
        out_ref[...] = acc

    vm = pl.BlockSpec(memory_space=pltpu.VMEM)
    return pl.pallas_call(
        body, name=name, in_specs=[vm], out_specs=vm,
        out_shape=jax.ShapeDtypeStruct(v.shape, F32),
        scratch_shapes=[pltpu.VMEM((8, rows, LANES), F32), pltpu.SemaphoreType.DMA((7,)),
                        pltpu.SemaphoreType.DMA((7,))],
    )(v)


def _adam_math(w, m, v, g):
    m = ADAM_B1 * m + (1.0 - ADAM_B1) * g
    v = ADAM_B2 * v + (1.0 - ADAM_B2) * (g * g)
    m_hat = m / (1.0 - ADAM_B1 ** ADAM_STEP)
    v_hat = v / (1.0 - ADAM_B2 ** ADAM_STEP)
    delta = -ADAM_LR * (m_hat / (jnp.sqrt(v_hat) + ADAM_EPS) + ADAM_WD * w)
    return delta, m, v


def adamw(w, m, v, g, *, name, tr=256):
    def body(w_ref, m_ref, v_ref, g_ref, d_ref, nm_ref, nv_ref):
        d, nm, nv = _adam_math(w_ref[...], m_ref[...], v_ref[...], g_ref[...])
        d_ref[...] = d
        nm_ref[...] = nm
        nv_ref[...] = nv

    shape = jax.ShapeDtypeStruct(w.shape, F32)
    if w.ndim == 2:
        return pl.pallas_call(body, name=name, out_shape=[shape] * 3)(w, m, v, g)
    n, R, C = w.shape
    tr = _tile(R, tr)
    spec = pl.BlockSpec((None, tr, C), lambda l, i: (l, i, 0))
    return pl.pallas_call(
        body, name=name, grid=(n, R // tr), in_specs=[spec] * 4, out_specs=[spec] * 3, out_shape=[shape] * 3,
        compiler_params=_cparams("parallel", "parallel"),
    )(w, m, v, g)


def _pack(parts):
    flat = jnp.concatenate([p.reshape(-1).astype(F32) for p in parts])
    size = -(-flat.shape[0] // (8 * LANES)) * (8 * LANES)
    return jnp.pad(flat, (0, size - flat.shape[0])).reshape(size // LANES, LANES)


def _unpack(block, shapes):
    flat = block.reshape(-1)
    out, pos = [], 0
    for shp in shapes:
        size = int(np.prod(shp))
        out.append(flat[pos:pos + size].reshape(shp))
        pos += size
    return out


SMALL = ("norm_mix_g", "norm_mlp_g", "final_norm_g", "hgrn_lb_raw", "hgrn_norm_g", "odd_sinks")
WEIGHTS = ("norm_mix_g", "norm_mlp_g", "final_norm_g", "even_w_in", "even_w_out", "hgrn_lb_raw", "hgrn_norm_g",
           "odd_w_qkv", "odd_b_qkv", "odd_sinks", "odd_w_o", "odd_b_o", "mlp_w1", "mlp_w2")


def kernel(x, norm_mix_g, norm_mlp_g, final_norm_g, even_w_in, even_w_out, hgrn_lb_raw, hgrn_norm_g, odd_w_qkv, odd_b_qkv, odd_sinks, odd_w_o, odd_b_o, mlp_w1, mlp_w2, loss_target, m_norm_mix_g, m_norm_mlp_g, m_final_norm_g, m_even_w_in, m_even_w_out, m_hgrn_lb_raw, m_hgrn_norm_g, m_odd_w_qkv, m_odd_b_qkv, m_odd_sinks, m_odd_w_o, m_odd_b_o, m_mlp_w1, m_mlp_w2, v_norm_mix_g, v_norm_mlp_g, v_final_norm_g, v_even_w_in, v_even_w_out, v_hgrn_lb_raw, v_hgrn_norm_g, v_odd_w_qkv, v_odd_b_qkv, v_odd_sinks, v_odd_w_o, v_odd_b_o, v_mlp_w1, v_mlp_w2):
    args = locals()
    W = {k: args[k] for k in WEIGHTS}
    M = {k: args["m_" + k] for k in WEIGHTS}
    V = {k: args["v_" + k] for k in WEIGHTS}
    _, S, D = x.shape
    depth = norm_mix_g.shape[0]
    HQ = D // C_DIM
    HKV = HQ // C_GROUP
    xi, yi, ci = _mesh_pos()
    shard = 2 * xi + yi

    n_odd, bq = odd_b_qkv.shape
    bo = odd_b_o.shape[1]
    keep = (ci == 0).astype(F32)
    pieces = [lax.dynamic_update_slice(jnp.zeros((n_odd, 4 * bq), F32), odd_b_qkv * keep, (0, shard * bq)),
              lax.dynamic_update_slice(jnp.zeros((n_odd, 4 * bo), F32), odd_b_o * keep, (0, shard * bo))]
    biases = allreduce_small(_pack(pieces), name="gather_biases")
    b_qkv_full, b_o_full = _unpack(biases, [(n_odd, 4 * bq), (n_odd, 4 * bo)])
    P = {k: W[k] for k in SMALL}
    P.update(odd_b_qkv=_pad_heads_cols(b_qkv_full, HQ + 2 * HKV), odd_b_o=b_o_full)

    def layer_keys(layer):
        mixer = ("even_w_in", "even_w_out") if layer % 2 == 0 else ("odd_w_qkv", "odd_w_o")
        return [(k, layer // 2) for k in mixer] + [("mlp_w1", layer), ("mlp_w2", layer)]

    keys = [layer_keys(0)[:1], layer_keys(0)[1:]] + [layer_keys(layer) for layer in range(1, depth)]
    shardings = [[SHARDING[k] for k, _ in ks] for ks in keys]
    started, dep0 = gather_start(
        [[cast_into_full(W[k], i, SHARDING[k], shard, name="cast_" + k) for k, i in ks] for ks in keys],
        shardings, biases, name="gather_start")
    forwarded = {}

    def forward_group(g, after):
        sems, mats = started[g]
        forwarded[g] = gather_forward(mats, shardings[g], sems, after, name="gather_forward")
        return forwarded[g][2]

    def finish_group(g, after):
        sems, mats, _ = forwarded[g]
        mats = gather_finish(mats, shardings[g], sems, after, name="gather_finish")
        Wl = {k: m for (k, _), m in zip(keys[g], mats)}
        if "odd_w_qkv" in Wl:
            Wl["odd_w_qkv"] = _pad_heads_cols(Wl["odd_w_qkv"], HQ + 2 * HKV)
            Wl["odd_w_o"] = _pad_heads_rows(Wl["odd_w_o"], HQ)
        return Wl

    def weights_in(layer, x_in):
        if layer == 0:
            forward_group(0, dep0)
        return finish_group(layer + 1 if layer else 0, x_in)

    def mid(layer, a):
        if layer == 0:
            forward_group(1, a)

    def weights_rest(layer, a):
        return finish_group(1, a) if layer == 0 else {}

    def after_mixer(layer, x1):
        return forward_group(layer + 2, x1) if layer + 1 < depth else None

    reduced = {k: lax.empty(W[k].shape, F32) for k in BIG}
    exchanging, scattering = [], []

    def settle(entry, after):
        kis, (sems, sums, lands, _) = entry
        sums, lands = scatter_wait(sums, lands, sems, after, name="scatter_wait")
        for (k, i), s, g in zip(kis, sums, lands):
            reduced[k] = owner_sum(s, g, shard, ci, reduced[k], i, name="owner_sum_" + k)

    def scatter_exchanged(after):
        for kis, (sems, bufs, lands, _) in exchanging:
            bufs, lands = exchange_wait(bufs, lands, sems, after, name="exchange_wait")
            sums = [pair_sum(b, g, ci, name="pair_sum_" + k) for (k, _), b, g in zip(kis, bufs, lands)]
            scattering.append((kis, scatter_start(sums, name="scatter_start")))
        exchanging.clear()

    def grads_ready(kis, B, after):
        scatter_exchanged(after)
        while len(scattering) > MAX_SCATTERS_IN_FLIGHT:
            settle(scattering.pop(0), after)
        bufs = [B[ki].reshape((4, 2, 1) + B[ki].shape[2:]) for ki in kis]
        exchanging.append((kis, exchange_start(bufs, name="exchange_start")))
        return exchanging[-1][1][3]

    hooks = dict(weights_in=weights_in, mid=mid, weights_rest=weights_rest, after_mixer=after_mixer,
                 grads_ready=grads_ready)
    loss, dx, G, B = local_step(x[0], loss_target[0], P, hooks)
    scatter_exchanged(dx)
    for entry in scattering:
        settle(entry, scattering[-1][1][3])
    grads = dict(zip(BIG, join_halves([reduced[k] for k in BIG], name="join_halves")))

    small_keys = SMALL + ("odd_b_qkv", "odd_b_o")
    small_shapes = [G[k].shape for k in small_keys]
    small = _unpack(allreduce_small(_pack([G[k] for k in small_keys]), name="reduce_small"), small_shapes)
    grads.update(zip(small_keys, small))
    grads["odd_b_qkv"] = lax.dynamic_slice(grads["odd_b_qkv"], (0, shard * bq), (n_odd, bq))
    grads["odd_b_o"] = lax.dynamic_slice(grads["odd_b_o"], (0, shard * bo), (n_odd, bo))

    deltas, new_m, new_v = {}, {}, {}
    for k in WEIGHTS:
        w2 = (lambda a: a.reshape(1, -1)) if W[k].ndim == 1 else (lambda a: a)
        d, nm, nv = adamw(w2(W[k]), w2(M[k]), w2(V[k]), w2(grads[k]), name="adamw_" + k)
        deltas[k], new_m[k], new_v[k] = (a.reshape(W[k].shape) for a in (d, nm, nv))

    loss = lax.psum(loss[0, 0], ("x", "y", "c"))
    return (loss, dx[None], *[grads[k] for k in WEIGHTS], *[deltas[k] for k in WEIGHTS],
            *[new_m[k] for k in WEIGHTS], *[new_v[k] for k in WEIGHTS])
```

```python
import functools
import math

import jax
import jax.numpy as jnp
import numpy as np
from jax import lax
from jax.experimental import pallas as pl
from jax.experimental.pallas import tpu as pltpu

F32 = jnp.float32
BF16 = jnp.bfloat16

NORM_EPS = 1e-5
ROPE_THETA = 500000.0
ROPE_FRACTION = 4
LANES = 128
BAND = 128
A_DIM = 128
A_BRANCHES = ((128, 1), (512, 4), (2048, 16))
B_DIM = 128
B_CHUNK = 64
B_SUB = 16
A_HEADS_PER_STEP = 2
B_HEADS_PER_STEP = 2
MAX_SCATTERS_IN_FLIGHT = 3
C_DIM = 64
C_GROUP = 8
C_WINDOW = 128

ADAM_LR = 0.001
ADAM_B1 = 0.9
ADAM_B2 = 0.999
ADAM_EPS = 1e-08
ADAM_WD = 0.01
ADAM_STEP = 10

VMEM_LIMIT = 56 * 1024 * 1024


def _cparams(*sem):
    return pltpu.CompilerParams(dimension_semantics=tuple(sem), vmem_limit_bytes=VMEM_LIMIT)


def _tile(n, want):
    if n <= want:
        return n
    t = want - want % LANES
    while n % t:
        t -= LANES
    assert t > 0, (n, want)
    return t


def matmul(a, b, *, trans_a=False, trans_b=False, out_dtypes, a_fn=None, epilogue=None, extras=(), name,
           b_cols=None, into=None, dep=None, tm=1024, tn=1024, tk=2048):
    if trans_a:
        K, M = a.shape
    else:
        M, K = a.shape
    if trans_b:
        N, K2 = b.shape
    else:
        K2, N = b.shape
    assert K == K2, (a.shape, b.shape)
    col0 = 0
    if b_cols is not None:
        assert not trans_b
        col0, N = b_cols[0], b_cols[1] - b_cols[0]
    if into is not None:
        rh, cs = into[0][2:]
        tm, tn = _tile(rh, tm), _tile(cs, tn)
    tm, tn, tk = _tile(M, tm), _tile(N, tn), _tile(K, tk)
    nk = K // tk
    n_extra = len(extras)
    n_out = len(out_dtypes)

    def body(*refs):
        a_ref, b_ref = refs[0], refs[1]
        extra_refs = refs[2:2 + n_extra]
        out_refs = refs[-1 - n_out:-1]
        acc_ref = refs[-1]
        k = pl.program_id(2)
        at = a_ref[...]
        if a_fn is not None:
            at = a_fn(at.astype(F32))
        at = at.astype(BF16)
        bt = b_ref[...].astype(BF16)
        dims = (((0,) if trans_a else (1,), (1,) if trans_b else (0,)), ((), ()))
        prod = lax.dot_general(at, bt, dims, preferred_element_type=F32)

        def finish(acc):
            ex = [r[...] for r in extra_refs]
            outs = epilogue(acc, *ex) if epilogue is not None else (acc,)
            for o_ref, o in zip(out_refs, outs):
                o_ref[...] = o.astype(o_ref.dtype)

        if nk == 1:
            finish(prod)
            return

        @pl.when(k == 0)
        def _():
            acc_ref[...] = prod

        @pl.when((k > 0) & (k < nk - 1))
        def _():
            acc_ref[...] += prod

        @pl.when(k == nk - 1)
        def _():
            finish(acc_ref[...] + prod)

    a_spec = pl.BlockSpec((tk, tm), lambda i, j, k: (k, i)) if trans_a else pl.BlockSpec((tm, tk), lambda i, j, k: (i, k))
    assert col0 % tn == 0, (col0, tn)
    b_spec = (pl.BlockSpec((tn, tk), lambda i, j, k: (j, k)) if trans_b
              else pl.BlockSpec((tk, tn), lambda i, j, k: (k, col0 // tn + j)))
    e_specs = []
    for e in extras:
        if e.shape == (1, N):
            e_specs.append(pl.BlockSpec((1, tn), lambda i, j, k: (0, j)))
        else:
            assert e.shape == (M, N), (e.shape, M, N)
            e_specs.append(pl.BlockSpec((tm, tn), lambda i, j, k: (i, j)))
    args = [a, b, *extras]
    in_specs = [a_spec, b_spec] + e_specs
    if dep is not None:
        args.append(dep)
        in_specs.append(pl.BlockSpec(dep.shape, lambda i, j, k: (0, 0)))
    if into is None:
        out_specs = [pl.BlockSpec((tm, tn), lambda i, j, k: (i, j)) for _ in out_dtypes]
        out_shape = [jax.ShapeDtypeStruct((M, N), dt) for dt in out_dtypes]
    else:
        buf_shape, sharding = into
        assert n_out == 1
        index = grad_buffer_index(buf_shape, (M, N), sharding, tm, tn)
        out_specs = [pl.BlockSpec((None, None, tm, tn), lambda i, j, k: index(i, j))]
        out_shape = [jax.ShapeDtypeStruct(buf_shape, out_dtypes[0])]
    outs = pl.pallas_call(
        body, name=name,
        grid=(M // tm, N // tn, nk),
        in_specs=in_specs, out_specs=out_specs, out_shape=out_shape,
        scratch_shapes=[pltpu.VMEM((tm, tn), F32)],
        compiler_params=_cparams("parallel", "parallel", "arbitrary"),
    )(*args)
    return outs[0] if n_out == 1 else tuple(outs)


def grad_buffer_shape(mat_shape, sharding):
    M, N = mat_shape
    return (4, 2, M // 2, N // 4) if sharding == "cols" else (4, 2, M // 8, N)


def grad_buffer_index(buf_shape, mat_shape, sharding, tm, tn):
    _, _, rh, cs = buf_shape
    M, N = mat_shape
    ib, jb = rh // tm, cs // tn
    if sharding == "cols":
        assert (M, N) == (2 * rh, 4 * cs), (buf_shape, mat_shape)
        return lambda i, j: (j // jb, i // ib, i % ib, j % jb)
    assert sharding == "rows" and (M, N) == (8 * rh, cs), (buf_shape, mat_shape)
    return lambda i, j: (i // (2 * ib), (i // ib) % 2, i % ib, j)


def rmsnorm_fwd(x, g, *, name, dep=None, tr=256):
    S, D = x.shape
    tr = _tile(S, tr)

    def body(x_ref, g_ref, *rest):
        h_ref = rest[-1]
        xv = x_ref[...]
        rstd = lax.rsqrt(jnp.mean(xv * xv, axis=-1, keepdims=True) + NORM_EPS)
        h_ref[...] = (xv * rstd * g_ref[...]).astype(h_ref.dtype)

    args = [x, g.reshape(1, D)]
    in_specs = [pl.BlockSpec((tr, D), lambda i: (i, 0)), pl.BlockSpec((1, D), lambda i: (0, 0))]
    if dep is not None:
        args.append(dep)
        in_specs.append(pl.BlockSpec(dep.shape, lambda i: (0, 0)))
    return pl.pallas_call(
        body, name=name, grid=(S // tr,),
        in_specs=in_specs,
        out_specs=pl.BlockSpec((tr, D), lambda i: (i, 0)),
        out_shape=jax.ShapeDtypeStruct((S, D), BF16),
        compiler_params=_cparams("parallel"),
    )(*args)


def _rms_bwd_rows(xv, gv, dh):
    rstd = lax.rsqrt(jnp.mean(xv * xv, axis=-1, keepdims=True) + NORM_EPS)
    xhat = xv * rstd
    dxhat = dh * gv
    dx = rstd * (dxhat - xhat * jnp.mean(dxhat * xhat, axis=-1, keepdims=True))
    return dx, dh * xhat


def rmsnorm_bwd(x, g, dh, dres, *, name, tr=256):
    S, D = x.shape
    tr = _tile(S, tr)

    def body(x_ref, g_ref, dh_ref, dres_ref, dx_ref, dg_ref):
        @pl.when(pl.program_id(0) == 0)
        def _():
            dg_ref[...] = jnp.zeros_like(dg_ref)

        dx, dgr = _rms_bwd_rows(x_ref[...], g_ref[...], dh_ref[...].astype(F32))
        dx_ref[...] = dres_ref[...] + dx
        dg_ref[...] += jnp.sum(dgr, axis=0, keepdims=True)

    row = pl.BlockSpec((tr, D), lambda i: (i, 0))
    vec = pl.BlockSpec((1, D), lambda i: (0, 0))
    return pl.pallas_call(
        body, name=name, grid=(S // tr,),
        in_specs=[row, vec, row, row],
        out_specs=[row, vec],
        out_shape=[jax.ShapeDtypeStruct((S, D), F32), jax.ShapeDtypeStruct((1, D), F32)],
        compiler_params=_cparams("arbitrary"),
    )(x, g.reshape(1, D), dh, dres)


def final_norm_loss(x, g, target, *, name, tr=256):
    S, D = x.shape
    tr = _tile(S, tr)

    def body(x_ref, g_ref, t_ref, loss_ref, dx_ref, dg_ref):
        @pl.when(pl.program_id(0) == 0)
        def _():
            dg_ref[...] = jnp.zeros_like(dg_ref)
            loss_ref[...] = jnp.zeros_like(loss_ref)

        xv, gv = x_ref[...], g_ref[...]
        rstd = lax.rsqrt(jnp.mean(xv * xv, axis=-1, keepdims=True) + NORM_EPS)
        err = xv * rstd * gv - t_ref[...]
        part = 0.5 * jnp.sum(jnp.mean(err * err, axis=-1, keepdims=True), axis=0, keepdims=True)
        loss_ref[...] += jnp.broadcast_to(part, loss_ref.shape)
        dx, dgr = _rms_bwd_rows(xv, gv, err * (1.0 / D))
        dx_ref[...] = dx
        dg_ref[...] += jnp.sum(dgr, axis=0, keepdims=True)

    row = pl.BlockSpec((tr, D), lambda i: (i, 0))
    vec = pl.BlockSpec((1, D), lambda i: (0, 0))
    return pl.pallas_call(
        body, name=name, grid=(S // tr,),
        in_specs=[row, vec, row],
        out_specs=[pl.BlockSpec((8, LANES), lambda i: (0, 0)), row, vec],
        out_shape=[jax.ShapeDtypeStruct((8, LANES), F32), jax.ShapeDtypeStruct((S, D), F32),
                   jax.ShapeDtypeStruct((1, D), F32)],
        compiler_params=_cparams("arbitrary"),
    )(x, g.reshape(1, D), target)


def colsum(a, *, name, tr=256):
    S, N = a.shape
    tr = _tile(S, tr)

    def body(a_ref, o_ref):
        @pl.when(pl.program_id(0) == 0)
        def _():
            o_ref[...] = jnp.zeros_like(o_ref)

        o_ref[...] += jnp.sum(a_ref[...].astype(F32), axis=0, keepdims=True)

    return pl.pallas_call(
        body, name=name, grid=(S // tr,),
        in_specs=[pl.BlockSpec((tr, N), lambda i: (i, 0))],
        out_specs=pl.BlockSpec((1, N), lambda i: (0, 0)),
        out_shape=jax.ShapeDtypeStruct((1, N), F32),
        compiler_params=_cparams("arbitrary"),
    )(a)


def rope_tables(seq, head_dim):
    rot = head_dim // ROPE_FRACTION
    half = rot // 2
    inv_freq = 1.0 / (ROPE_THETA ** (jnp.arange(0, rot, 2, dtype=F32) / rot))
    ang = jnp.arange(seq, dtype=F32)[:, None] * inv_freq[None, :]
    cos, sin = jnp.cos(ang), jnp.sin(ang)
    ones = jnp.ones((seq, LANES - 2 * half), F32)
    zeros = jnp.zeros((seq, LANES - 2 * half), F32)
    zh = jnp.zeros((seq, half), F32)
    c = jnp.concatenate([cos, cos, ones], axis=1)
    sa = jnp.concatenate([-sin, zh, zeros], axis=1)
    sb = jnp.concatenate([zh, sin, zeros], axis=1)
    return c, sa, sb, half


def _rope(x, c, sa, sb, half):
    return x * c + pltpu.roll(x, LANES - half, 1) * sa + pltpu.roll(x, half, 1) * sb


def _rope_bwd(d, c, sa, sb, half):
    return d * c + pltpu.roll(d * sa, half, 1) + pltpu.roll(d * sb, LANES - half, 1)


def _band_masks(n, max_dist):
    qi = lax.broadcasted_iota(jnp.int32, (BAND, BAND), 0)
    kj = lax.broadcasted_iota(jnp.int32, (BAND, BAND), 1)
    cur = kj <= qi
    prev = ((kj >= qi) if max_dist == BAND else (kj > qi)) & (n > 0)
    return prev, cur


def _dot_nt(a, b):
    return lax.dot_general(a, b, (((1,), (1,)), ((), ())), preferred_element_type=F32)


def _dot_tn(a, b):
    return lax.dot_general(a, b, (((0,), (0,)), ((), ())), preferred_element_type=F32)


def _dot(a, b):
    return jnp.dot(a, b, preferred_element_type=F32)


def band_attention_fwd(src, tabs, *, L, dil, n_heads, group, q_blk, k_blk, v_blk, blk_per_row, max_dist, scale,
                       half, sink=None, normalize, name):
    c_t, sa_t, sb_t = tabs
    nb = L // BAND
    W = dil * n_heads * LANES
    use_sink = sink is not None

    def body(*refs):
        q_ref, k_ref, v_ref, c_ref, sa_ref, sb_ref = refs[:6]
        pos = 6
        if use_sink:
            sink_ref = refs[pos]
            pos += 1
        o_ref, m_ref, l_ref = refs[pos:pos + 3] if not normalize else (refs[pos], refs[pos + 1], None)
        kr_ref, vb_ref = refs[-2], refs[-1]

        def prep(n, carry):
            rows = pl.ds(pl.multiple_of(n * BAND, BAND), BAND)
            kr_ref[rows, :] = _rope(k_ref[rows, :], c_ref[rows, :], sa_ref[rows, :], sb_ref[rows, :], half).astype(BF16)
            vb_ref[rows, :] = v_ref[rows, :].astype(BF16)
            return carry

        lax.fori_loop(0, nb, prep, 0)

        def step(n, carry):
            rows = pl.ds(pl.multiple_of(n * BAND, BAND), BAND)
            prow = pl.ds(pl.multiple_of(jnp.maximum(n - 1, 0) * BAND, BAND), BAND)
            q = _rope(q_ref[rows, :], c_ref[rows, :], sa_ref[rows, :], sb_ref[rows, :], half).astype(BF16)
            mp, mc = _band_masks(n, max_dist)
            sp = jnp.where(mp, _dot_nt(q, kr_ref[prow, :]) * scale, -jnp.inf)
            sc = jnp.where(mc, _dot_nt(q, kr_ref[rows, :]) * scale, -jnp.inf)
            m = jnp.maximum(jnp.max(sp, axis=-1, keepdims=True), jnp.max(sc, axis=-1, keepdims=True))
            if use_sink:
                sk = sink_ref[:, 0:1]
                m = jnp.maximum(m, sk)
            pp = jnp.exp(sp - m)
            pc = jnp.exp(sc - m)
            l = jnp.sum(pp, axis=-1, keepdims=True) + jnp.sum(pc, axis=-1, keepdims=True)
            if use_sink:
                l = l + jnp.exp(sk - m)
            num = _dot(pp.astype(BF16), vb_ref[prow, :]) + _dot(pc.astype(BF16), vb_ref[rows, :])
            if normalize:
                o_ref[rows, :] = num / l
                m_ref[rows, :] = jnp.broadcast_to(m + jnp.log(l), (BAND, LANES))
            else:
                o_ref[rows, :] = num
                m_ref[rows, :] = jnp.broadcast_to(m, (BAND, LANES))
                l_ref[rows, :] = jnp.broadcast_to(l, (BAND, LANES))
            return carry

        lax.fori_loop(0, nb, step, 0)

    def col(off, div):
        return pl.BlockSpec((L, LANES), lambda r, h: (0, r * blk_per_row + off + h // div))

    tab = pl.BlockSpec((L, LANES), lambda r, h: (0, r))
    out = pl.BlockSpec((L, LANES), lambda r, h: (0, r * n_heads + h))
    in_specs = [col(q_blk, 1), col(k_blk, group), col(v_blk, group), tab, tab, tab]
    args = [src, src, src, c_t, sa_t, sb_t]
    if use_sink:
        in_specs.append(pl.BlockSpec((None, 1, LANES), lambda r, h: (h, 0, 0)))
        args.append(sink)
    n_out = 2 if normalize else 3
    return pl.pallas_call(
        body, name=name, grid=(dil, n_heads),
        in_specs=in_specs,
        out_specs=[out] * n_out,
        out_shape=[jax.ShapeDtypeStruct((L, W), F32)] * n_out,
        scratch_shapes=[pltpu.VMEM((L, LANES), BF16), pltpu.VMEM((L, LANES), BF16)],
        compiler_params=_cparams("parallel", "arbitrary"),
    )(*args)


def band_attention_bwd(src, tabs, o, lse, do, *, L, dil, n_heads, group, q_blk, k_blk, v_blk, blk_per_row, max_dist,
                       scale, half, do_bpr, sink=None, acc=None, name):
    c_t, sa_t, sb_t = tabs
    nb = L // BAND
    n_kv = n_heads // group
    use_sink = sink is not None
    use_acc = acc is not None
    assert not (use_acc and group != 1)

    def body(*refs):
        q_ref, k_ref, v_ref, c_ref, sa_ref, sb_ref, o_ref, lse_ref, do_ref = refs[:9]
        pos = 9
        if use_sink:
            sink_ref = refs[pos]
            pos += 1
        if use_acc:
            aq_ref, ak_ref, av_ref = refs[pos:pos + 3]
            pos += 3
        dq_ref, dk_ref, dv_ref = refs[pos:pos + 3]
        pos += 3
        if use_sink:
            dsink_ref = refs[pos]
        kr_ref, vb_ref, dka_ref, dva_ref = refs[-4:]
        h = pl.program_id(1)

        def prep(n, carry):
            rows = pl.ds(pl.multiple_of(n * BAND, BAND), BAND)
            kr_ref[rows, :] = _rope(k_ref[rows, :], c_ref[rows, :], sa_ref[rows, :], sb_ref[rows, :], half).astype(BF16)
            vb_ref[rows, :] = v_ref[rows, :].astype(BF16)
            return carry

        lax.fori_loop(0, nb, prep, 0)

        @pl.when(h % group == 0)
        def _():
            dka_ref[...] = jnp.zeros_like(dka_ref)
            dva_ref[...] = jnp.zeros_like(dva_ref)

        def step(n, dsk):
            rows = pl.ds(pl.multiple_of(n * BAND, BAND), BAND)
            prow = pl.ds(pl.multiple_of(jnp.maximum(n - 1, 0) * BAND, BAND), BAND)
            cv, sav, sbv = c_ref[rows, :], sa_ref[rows, :], sb_ref[rows, :]
            q = _rope(q_ref[rows, :], cv, sav, sbv, half).astype(BF16)
            dov = do_ref[rows, :]
            lsev = lse_ref[rows, 0:1]
            delta = jnp.sum(dov * o_ref[rows, :], axis=-1, keepdims=True)
            dob = dov.astype(BF16)
            mp, mc = _band_masks(n, max_dist)
            kp, kc, vp, vc = kr_ref[prow, :], kr_ref[rows, :], vb_ref[prow, :], vb_ref[rows, :]
            pp = jnp.exp(jnp.where(mp, _dot_nt(q, kp) * scale, -jnp.inf) - lsev)
            pc = jnp.exp(jnp.where(mc, _dot_nt(q, kc) * scale, -jnp.inf) - lsev)
            dsp = (pp * (_dot_nt(dob, vp) - delta) * scale).astype(BF16)
            dsc = (pc * (_dot_nt(dob, vc) - delta) * scale).astype(BF16)
            dq = _rope_bwd(_dot(dsp, kp) + _dot(dsc, kc), cv, sav, sbv, half)
            if use_acc:
                dq = dq + aq_ref[rows, :]
            dq_ref[rows, :] = dq
            dka_ref[prow, :] += _dot_tn(dsp, q)
            dka_ref[rows, :] += _dot_tn(dsc, q)
            dva_ref[prow, :] += _dot_tn(pp.astype(BF16), dob)
            dva_ref[rows, :] += _dot_tn(pc.astype(BF16), dob)
            if use_sink:
                dsk = dsk - jnp.sum(jnp.exp(sink_ref[:, 0:1] - lsev) * delta, axis=0, keepdims=True)
            return dsk

        dsk = lax.fori_loop(0, nb, step, jnp.zeros((1, 1), F32))
        if use_sink:
            dsink_ref[...] = jnp.broadcast_to(dsk, dsink_ref.shape)

        @pl.when(h % group == group - 1)
        def _():
            def fin(n, carry):
                rows = pl.ds(pl.multiple_of(n * BAND, BAND), BAND)
                dk = _rope_bwd(dka_ref[rows, :], c_ref[rows, :], sa_ref[rows, :], sb_ref[rows, :], half)
                dv = dva_ref[rows, :]
                if use_acc:
                    dk = dk + ak_ref[rows, :]
                    dv = dv + av_ref[rows, :]
                dk_ref[rows, :] = dk
                dv_ref[rows, :] = dv
                return carry

            lax.fori_loop(0, nb, fin, 0)

    def col(off, div):
        return pl.BlockSpec((L, LANES), lambda r, h: (0, r * blk_per_row + off + h // div))

    tab = pl.BlockSpec((L, LANES), lambda r, h: (0, r))
    qo = pl.BlockSpec((L, LANES), lambda r, h: (0, r * n_heads + h))
    kvo = pl.BlockSpec((L, LANES), lambda r, h: (0, r * n_kv + h // group))
    dospec = pl.BlockSpec((L, LANES), lambda r, h: (0, r * do_bpr + h))
    in_specs = [col(q_blk, 1), col(k_blk, group), col(v_blk, group), tab, tab, tab, qo, qo, dospec]
    args = [src, src, src, c_t, sa_t, sb_t, o, lse, do]
    if use_sink:
        in_specs.append(pl.BlockSpec((None, 1, LANES), lambda r, h: (h, 0, 0)))
        args.append(sink)
    if use_acc:
        in_specs += [qo, kvo, kvo]
        args += list(acc)
    out_specs = [qo, kvo, kvo]
    out_shape = [jax.ShapeDtypeStruct((L, dil * n_heads * LANES), F32),
                 jax.ShapeDtypeStruct((L, dil * n_kv * LANES), F32),
                 jax.ShapeDtypeStruct((L, dil * n_kv * LANES), F32)]
    if use_sink:
        out_specs.append(pl.BlockSpec((None, 1, LANES), lambda r, h: (h, 0, 0)))
        out_shape.append(jax.ShapeDtypeStruct((n_heads, 1, LANES), F32))
    return pl.pallas_call(
        body, name=name, grid=(dil, n_heads),
        in_specs=in_specs, out_specs=out_specs, out_shape=out_shape,
        scratch_shapes=[pltpu.VMEM((L, LANES), BF16), pltpu.VMEM((L, LANES), BF16),
                        pltpu.VMEM((L, LANES), F32), pltpu.VMEM((L, LANES), F32)],
        compiler_params=_cparams("parallel", "arbitrary"),
    )(*args)


def banded_fwd(src, tabs, *, L, dil, n_heads, group, hb, q_blk, k_blk, v_blk, blk_per_row, max_dist, scale, half,
               sink=None, normalize, name):
    c_t, sa_t, sb_t = tabs
    nb = L // BAND
    kvb = hb // group
    n_hg = n_heads // hb
    W = dil * n_heads * LANES
    use_sink = sink is not None
    assert n_heads % hb == 0 and hb % group == 0
    assert (q_blk % hb, k_blk % kvb, v_blk % kvb) == (0, 0, 0) and (dil == 1 or blk_per_row % hb == 0)

    def body(*refs):
        q_ref, k_ref, v_ref, c_ref, sa_ref, sb_ref = refs[:6]
        pos = 6
        if use_sink:
            sink_ref = refs[pos]
            pos += 1
        o_ref, m_ref = refs[pos], refs[pos + 1]
        l_ref = None if normalize else refs[pos + 2]
        kr_ref, vb_ref = refs[-2], refs[-1]
        n = pl.program_id(2)

        @pl.when(n == 0)
        def _():
            def prep(b, carry):
                rows = pl.ds(pl.multiple_of(b * BAND, BAND), BAND)
                cv, sav, sbv = c_ref[rows, :], sa_ref[rows, :], sb_ref[rows, :]
                for j in range(kvb):
                    cols = slice(j * LANES, (j + 1) * LANES)
                    kr_ref[rows, cols] = _rope(k_ref[rows, cols].astype(F32), cv, sav, sbv, half).astype(BF16)
                    vb_ref[rows, cols] = v_ref[rows, cols].astype(BF16)
                return carry

            lax.fori_loop(0, nb, prep, 0)

        rows = pl.ds(pl.multiple_of(n * BAND, BAND), BAND)
        prow = pl.ds(pl.multiple_of(jnp.maximum(n - 1, 0) * BAND, BAND), BAND)
        cv, sav, sbv = c_ref[rows, :], sa_ref[rows, :], sb_ref[rows, :]
        mp, mc = _band_masks(n, max_dist)
        for i in range(hb):
            qc = slice(i * LANES, (i + 1) * LANES)
            kc = slice((i // group) * LANES, (i // group + 1) * LANES)
            q = _rope(q_ref[:, qc].astype(F32), cv, sav, sbv, half).astype(BF16)
            sp = jnp.where(mp, _dot_nt(q, kr_ref[prow, kc]) * scale, -jnp.inf)
            sc = jnp.where(mc, _dot_nt(q, kr_ref[rows, kc]) * scale, -jnp.inf)
            m = jnp.maximum(jnp.max(sp, axis=-1, keepdims=True), jnp.max(sc, axis=-1, keepdims=True))
            if use_sink:
                sk = sink_ref[i, :, 0:1]
                m = jnp.maximum(m, sk)
            pp = jnp.exp(sp - m)
            pc = jnp.exp(sc - m)
            l = jnp.sum(pp, axis=-1, keepdims=True) + jnp.sum(pc, axis=-1, keepdims=True)
            if use_sink:
                l = l + jnp.exp(sk - m)
            num = _dot(pp.astype(BF16), vb_ref[prow, kc]) + _dot(pc.astype(BF16), vb_ref[rows, kc])
            if normalize:
                o_ref[:, qc] = num / l
                m_ref[:, qc] = jnp.broadcast_to(m + jnp.log(l), (BAND, LANES))
            else:
                o_ref[:, qc] = num
                m_ref[:, qc] = jnp.broadcast_to(m, (BAND, LANES))
                l_ref[:, qc] = jnp.broadcast_to(l, (BAND, LANES))

    qspec = pl.BlockSpec((BAND, hb * LANES), lambda r, g, n: (n, (r * blk_per_row + q_blk) // hb + g))

    def kv(off):
        return pl.BlockSpec((L, kvb * LANES), lambda r, g, n: (0, (r * blk_per_row + off) // kvb + g))

    tab = pl.BlockSpec((L, LANES), lambda r, g, n: (0, r))
    out = pl.BlockSpec((BAND, hb * LANES), lambda r, g, n: (n, r * n_hg + g))
    in_specs = [qspec, kv(k_blk), kv(v_blk), tab, tab, tab]
    args = [src, src, src, c_t, sa_t, sb_t]
    if use_sink:
        in_specs.append(pl.BlockSpec((hb, 1, LANES), lambda r, g, n: (g, 0, 0)))
        args.append(sink)
    n_out = 2 if normalize else 3
    return pl.pallas_call(
        body, name=name, grid=(dil, n_hg, nb),
        in_specs=in_specs,
        out_specs=[out] * n_out,
        out_shape=[jax.ShapeDtypeStruct((L, W), F32)] * n_out,
        scratch_shapes=[pltpu.VMEM((L, kvb * LANES), BF16), pltpu.VMEM((L, kvb * LANES), BF16)],
        compiler_params=_cparams("parallel", "parallel", "arbitrary"),
    )(*args)


def banded_bwd(src, tabs, o, lse, do, *, L, dil, n_heads, group, hb, q_blk, k_blk, v_blk, blk_per_row, max_dist,
               scale, half, do_bpr, sink=None, acc=None, name):
    c_t, sa_t, sb_t = tabs
    nb = L // BAND
    kvb = hb // group
    n_hg = n_heads // hb
    n_kv = n_heads // group
    use_sink = sink is not None
    use_acc = acc is not None
    assert n_heads % hb == 0 and hb % group == 0 and do_bpr % hb == 0
    assert (q_blk % hb, k_blk % kvb, v_blk % kvb) == (0, 0, 0) and (dil == 1 or blk_per_row % hb == 0)

    def body(*refs):
        q_ref, k_ref, v_ref, c_ref, sa_ref, sb_ref, o_ref, lse_ref, do_ref = refs[:9]
        pos = 9
        if use_sink:
            sink_ref = refs[pos]
            pos += 1
        if use_acc:
            aq_ref, ak_ref, av_ref = refs[pos:pos + 3]
            pos += 3
        dq_ref, dk_ref, dv_ref = refs[pos:pos + 3]
        pos += 3
        if use_sink:
            dsink_ref = refs[pos]
        kr_ref, vb_ref, dka_ref, dva_ref = refs[-4:]
        n = pl.program_id(2)

        @pl.when(n == 0)
        def _():
            def prep(b, carry):
                rows = pl.ds(pl.multiple_of(b * BAND, BAND), BAND)
                cv, sav, sbv = c_ref[rows, :], sa_ref[rows, :], sb_ref[rows, :]
                for j in range(kvb):
                    cols = slice(j * LANES, (j + 1) * LANES)
                    kr_ref[rows, cols] = _rope(k_ref[rows, cols].astype(F32), cv, sav, sbv, half).astype(BF16)
                    vb_ref[rows, cols] = v_ref[rows, cols].astype(BF16)
                return carry

            lax.fori_loop(0, nb, prep, 0)
            dka_ref[...] = jnp.zeros_like(dka_ref)
            dva_ref[...] = jnp.zeros_like(dva_ref)
            if use_sink:
                dsink_ref[...] = jnp.zeros_like(dsink_ref)

        rows = pl.ds(pl.multiple_of(n * BAND, BAND), BAND)
        prow = pl.ds(pl.multiple_of(jnp.maximum(n - 1, 0) * BAND, BAND), BAND)
        cv, sav, sbv = c_ref[rows, :], sa_ref[rows, :], sb_ref[rows, :]
        mp, mc = _band_masks(n, max_dist)
        for i in range(hb):
            qc = slice(i * LANES, (i + 1) * LANES)
            kc = slice((i // group) * LANES, (i // group + 1) * LANES)
            q = _rope(q_ref[:, qc].astype(F32), cv, sav, sbv, half).astype(BF16)
            dov = do_ref[:, qc]
            lsev = lse_ref[:, i * LANES:i * LANES + 1]
            delta = jnp.sum(dov * o_ref[:, qc], axis=-1, keepdims=True)
            dob = dov.astype(BF16)
            kp, kcur, vp, vcur = kr_ref[prow, kc], kr_ref[rows, kc], vb_ref[prow, kc], vb_ref[rows, kc]
            pp = jnp.exp(jnp.where(mp, _dot_nt(q, kp) * scale, -jnp.inf) - lsev)
            pc = jnp.exp(jnp.where(mc, _dot_nt(q, kcur) * scale, -jnp.inf) - lsev)
            dsp = (pp * (_dot_nt(dob, vp) - delta) * scale).astype(BF16)
            dsc = (pc * (_dot_nt(dob, vcur) - delta) * scale).astype(BF16)
            dq = _rope_bwd(_dot(dsp, kp) + _dot(dsc, kcur), cv, sav, sbv, half)
            if use_acc:
                dq = dq + aq_ref[:, qc]
            dq_ref[:, qc] = dq
            dka_ref[prow, kc] += _dot_tn(dsp, q)
            dka_ref[rows, kc] += _dot_tn(dsc, q)
            dva_ref[prow, kc] += _dot_tn(pp.astype(BF16), dob)
            dva_ref[rows, kc] += _dot_tn(pc.astype(BF16), dob)
            if use_sink:
                dsk = jnp.sum(jnp.exp(sink_ref[i, :, 0:1] - lsev) * delta, axis=0, keepdims=True)
                dsink_ref[i] -= jnp.broadcast_to(dsk, (1, LANES))

        @pl.when(n == nb - 1)
        def _():
            def fin(b, carry):
                brow = pl.ds(pl.multiple_of(b * BAND, BAND), BAND)
                bc, bsa, bsb = c_ref[brow, :], sa_ref[brow, :], sb_ref[brow, :]
                for j in range(kvb):
                    cols = slice(j * LANES, (j + 1) * LANES)
                    dk = _rope_bwd(dka_ref[brow, cols], bc, bsa, bsb, half)
                    dv = dva_ref[brow, cols]
                    if use_acc:
                        dk = dk + ak_ref[brow, cols]
                        dv = dv + av_ref[brow, cols]
                    dk_ref[brow, cols] = dk
                    dv_ref[brow, cols] = dv
                return carry

            lax.fori_loop(0, nb, fin, 0)

    qspec = pl.BlockSpec((BAND, hb * LANES), lambda r, g, n: (n, (r * blk_per_row + q_blk) // hb + g))

    def kv(off):
        return pl.BlockSpec((L, kvb * LANES), lambda r, g, n: (0, (r * blk_per_row + off) // kvb + g))

    tab = pl.BlockSpec((L, LANES), lambda r, g, n: (0, r))
    qo = pl.BlockSpec((BAND, hb * LANES), lambda r, g, n: (n, r * n_hg + g))
    kvo = pl.BlockSpec((L, kvb * LANES), lambda r, g, n: (0, r * (n_kv // kvb) + g))
    dospec = pl.BlockSpec((BAND, hb * LANES), lambda r, g, n: (n, r * (do_bpr // hb) + g))
    in_specs = [qspec, kv(k_blk), kv(v_blk), tab, tab, tab, qo, qo, dospec]
    args = [src, src, src, c_t, sa_t, sb_t, o, lse, do]
    sink_spec = pl.BlockSpec((hb, 1, LANES), lambda r, g, n: (g, 0, 0))
    if use_sink:
        in_specs.append(sink_spec)
        args.append(sink)
    if use_acc:
        in_specs += [qo, kvo, kvo]
        args += list(acc)
    out_specs = [qo, kvo, kvo]
    out_shape = [jax.ShapeDtypeStruct((L, dil * n_heads * LANES), F32),
                 jax.ShapeDtypeStruct((L, dil * n_kv * LANES), F32),
                 jax.ShapeDtypeStruct((L, dil * n_kv * LANES), F32)]
    if use_sink:
        out_specs.append(sink_spec)
        out_shape.append(jax.ShapeDtypeStruct((n_heads, 1, LANES), F32))
    return pl.pallas_call(
        body, name=name, grid=(dil, n_hg, nb),
        in_specs=in_specs, out_specs=out_specs, out_shape=out_shape,
        scratch_shapes=[pltpu.VMEM((L, kvb * LANES), BF16), pltpu.VMEM((L, kvb * LANES), BF16),
                        pltpu.VMEM((L, kvb * LANES), F32), pltpu.VMEM((L, kvb * LANES), F32)],
        compiler_params=_cparams("parallel", "parallel", "arbitrary"),
    )(*args)


def merge_branches(parts, *, name, tr=256):
    S, W = parts[0][0].shape
    tr = _tile(S, tr)
    nbr = len(parts)

    def body(*refs):
        ins, (o_ref, lse_ref) = refs[:3 * nbr], refs[3 * nbr:]
        nums = [ins[3 * i][...] for i in range(nbr)]
        ms = [ins[3 * i + 1][...] for i in range(nbr)]
        ls = [ins[3 * i + 2][...] for i in range(nbr)]
        mx = functools.reduce(jnp.maximum, ms)
        ws = [jnp.exp(m - mx) for m in ms]
        num = sum(w * n for w, n in zip(ws, nums))
        den = sum(w * l for w, l in zip(ws, ls))
        o_ref[...] = num / den
        lse_ref[...] = mx + jnp.log(den)

    row = pl.BlockSpec((tr, W), lambda i: (i, 0))
    flat = [a for p in parts for a in p]
    return pl.pallas_call(
        body, name=name, grid=(S // tr,),
        in_specs=[row] * len(flat), out_specs=[row, row],
        out_shape=[jax.ShapeDtypeStruct((S, W), F32)] * 2,
        compiler_params=_cparams("parallel"),
    )(*flat)


def _sigmoid(x):
    return 1.0 / (1.0 + jnp.exp(-x))


def _tri(n, lower):
    r = lax.broadcasted_iota(jnp.int32, (n, n), 0)
    c = lax.broadcasted_iota(jnp.int32, (n, n), 1)
    return ((c <= r) if lower else (c >= r)).astype(F32)


def _dot_exact(a, b, trans_a=False, trans_b=False):
    dims = (((0,) if trans_a else (1,), (1,) if trans_b else (0,)), ((), ()))
    return lax.dot_general(a, b, dims, preferred_element_type=F32, precision=lax.Precision.HIGHEST)


def _hgrn_gates(qb, fb, lb):
    sq = _sigmoid(qb)
    q = qb * sq * (B_DIM ** -0.5)
    sf = _sigmoid(fb)
    gate = lb + (1.0 - lb) * sf
    return q, 1.0 - gate, gate, sq, sf


def _hgrn_intra_fwd(q_ref, k_ref, b_ref):
    C, n_sub = B_CHUNK, B_CHUNK // B_SUB
    b_all, k_all = b_ref[...], k_ref[...]
    srow = lax.broadcasted_iota(jnp.int32, (C, LANES), 0)
    lane = lax.broadcasted_iota(jnp.int32, (B_SUB, C), 1)
    trow = lax.broadcasted_iota(jnp.int32, (B_SUB, LANES), 0)
    blocks = []
    for i in range(n_sub):
        r0 = i * B_SUB
        qi, bi = q_ref[pl.ds(r0, B_SUB), :], b_ref[pl.ds(r0, B_SUB), :]
        if i == 0:
            a_i = jnp.zeros((B_SUB, C), F32)
        else:
            ref_b = b_ref[pl.ds(r0 - 1, 1), :]
            qt = (qi * jnp.exp(bi - ref_b)).astype(BF16)
            kt = jnp.where(srow < r0, k_all * jnp.exp(jnp.minimum(ref_b - b_all, 0.0)), 0.0).astype(BF16)
            a_i = _dot_nt(qt, kt)
        for sl in range(B_SUB):
            s = r0 + sl
            e = jnp.where(trow >= sl, jnp.exp(jnp.minimum(bi - b_ref[pl.ds(s, 1), :], 0.0)), 0.0)
            colv = jnp.sum(qi * k_ref[pl.ds(s, 1), :] * e, axis=-1, keepdims=True)
            a_i = a_i + jnp.where(lane == s, colv, 0.0)
        blocks.append(a_i)
    return jnp.concatenate(blocks, axis=0)


def hgrn_fwd(src, lb, norm_g, *, S, n_heads, hb, q_blk, f_blk, i_blk, g_blk, name):
    nc = S // B_CHUNK
    assert n_heads % hb == 0 and (q_blk % hb, f_blk % hb, i_blk % hb, g_blk % hb) == (0, 0, 0, 0)

    def body(qb_ref, fb_ref, ib_ref, gb_ref, lb_ref, ng_ref, out_ref, o_ref, st_ref, state_ref, q_s, k_s, b_s):
        state_ref[...] = jnp.zeros_like(state_ref)
        tril = _tri(B_CHUNK, True)

        def step(c, carry):
            rows = pl.ds(pl.multiple_of(c * B_CHUNK, B_CHUNK), B_CHUNK)
            for j in range(hb):
                cols = slice(j * LANES, (j + 1) * LANES)
                q, k, gate, _, _ = _hgrn_gates(qb_ref[rows, cols], fb_ref[rows, cols], lb_ref[j])
                b = _dot_exact(tril, jnp.log(gate))
                vf = ib_ref[rows, cols]
                v = vf.astype(BF16)
                q_s[j], k_s[j], b_s[j] = q, k, b
                st = state_ref[j]
                st_ref[j, c] = st
                a = _hgrn_intra_fwd(q_s.at[j], k_s.at[j], b_s.at[j])
                o = _dot(a.astype(BF16), v) + _dot_nt((q * jnp.exp(b)).astype(BF16), st.astype(BF16))
                b_last = b_s[j, pl.ds(B_CHUNK - 1, 1), :]
                state_ref[j] = st * jnp.exp(b_last) + _dot_exact(vf, k * jnp.exp(b_last - b), trans_a=True)
                o_ref[rows, cols] = o
                rstd = lax.rsqrt(jnp.mean(o * o, axis=-1, keepdims=True) + NORM_EPS)
                gb = gb_ref[rows, cols]
                out_ref[rows, cols] = o * rstd * ng_ref[...] * (gb * _sigmoid(gb))
            return carry

        lax.fori_loop(0, nc, step, 0)

    def col(off):
        return pl.BlockSpec((S, hb * LANES), lambda g: (0, off // hb + g))

    hv = pl.BlockSpec((hb, 1, LANES), lambda g: (g, 0, 0))
    return pl.pallas_call(
        body, name=name, grid=(n_heads // hb,),
        in_specs=[col(q_blk), col(f_blk), col(i_blk), col(g_blk), hv, pl.BlockSpec((1, LANES), lambda g: (0, 0))],
        out_specs=[col(0), col(0), pl.BlockSpec((hb, nc, LANES, LANES), lambda g: (g, 0, 0, 0))],
        out_shape=[jax.ShapeDtypeStruct((S, n_heads * LANES), F32), jax.ShapeDtypeStruct((S, n_heads * LANES), F32),
                   jax.ShapeDtypeStruct((n_heads, nc, LANES, LANES), F32)],
        scratch_shapes=[pltpu.VMEM((hb, LANES, LANES), F32)] + [pltpu.VMEM((hb, B_CHUNK, LANES), F32)] * 3,
        compiler_params=_cparams("parallel"),
    )(src, src, src, src, lb, norm_g.reshape(1, LANES))


def hgrn_bwd(src, lb, norm_g, o, states, dout, *, S, n_heads, hb, q_blk, f_blk, i_blk, g_blk, dout_blk, name):
    nc = S // B_CHUNK
    C, n_sub = B_CHUNK, B_CHUNK // B_SUB
    assert n_heads % hb == 0 and (q_blk % hb, f_blk % hb, i_blk % hb, g_blk % hb, dout_blk % hb) == (0, 0, 0, 0, 0)

    def body(qb_ref, fb_ref, ib_ref, gb_ref, lb_ref, ng_ref, o_ref, st_ref, dout_ref,
             dqb_ref, dfb_ref, dib_ref, dgb_ref, dlb_ref, dng_ref, dstate_ref, q_sh, k_sh, b_sh, dq_sh, dk_sh):
        @pl.when(pl.program_id(0) == 0)
        def _():
            dng_ref[...] = jnp.zeros_like(dng_ref)

        dstate_ref[...] = jnp.zeros_like(dstate_ref)
        tril = _tri(C, True)
        triu = _tri(C, False)
        ngv = ng_ref[...]
        srow = lax.broadcasted_iota(jnp.int32, (C, LANES), 0)
        lane = lax.broadcasted_iota(jnp.int32, (B_SUB, C), 1)
        trow = lax.broadcasted_iota(jnp.int32, (B_SUB, LANES), 0)
        causal = lax.broadcasted_iota(jnp.int32, (C, C), 1) <= lax.broadcasted_iota(jnp.int32, (C, C), 0)

        def one_head(j, c, rows, carry):
            dlog_carry, dlb_acc, dng_acc = carry
            cols = slice(j * LANES, (j + 1) * LANES)
            q_s, k_s, b_s, dq_s, dk_s = q_sh.at[j], k_sh.at[j], b_sh.at[j], dq_sh.at[j], dk_sh.at[j]
            lbv = lb_ref[j]
            qb, fb, gb = qb_ref[rows, cols], fb_ref[rows, cols], gb_ref[rows, cols]
            q, k, gate, sq, sf = _hgrn_gates(qb, fb, lbv)
            b = _dot_exact(tril, jnp.log(gate))
            vf = ib_ref[rows, cols]
            v = vf.astype(BF16)
            q_s[...], k_s[...], b_s[...] = q, k, b
            st = st_ref[j, c]
            dst = dstate_ref[j]

            ov = o_ref[rows, cols]
            dout = dout_ref[rows, cols]
            rstd = lax.rsqrt(jnp.mean(ov * ov, axis=-1, keepdims=True) + NORM_EPS)
            xhat = ov * rstd
            sg = _sigmoid(gb)
            dy = dout * (gb * sg)
            dgb_ref[rows, cols] = dout * (xhat * ngv) * (sg * (1.0 + gb * (1.0 - sg)))
            dng_acc = dng_acc + jnp.sum(dy * xhat, axis=0, keepdims=True)
            dxhat = dy * ngv
            do = rstd * (dxhat - xhat * jnp.mean(dxhat * xhat, axis=-1, keepdims=True))
            dob = do.astype(BF16)

            eb = jnp.exp(b)
            b_last = b_s[pl.ds(C - 1, 1), :]
            ebl = jnp.exp(b_last - b)
            a = _hgrn_intra_fwd(q_s, k_s, b_s)
            da = jnp.where(causal, _dot_exact(do, vf, trans_b=True), 0.0)
            dv = _dot_tn(a.astype(BF16), dob) + _dot_nt((k * ebl).astype(BF16), dst.astype(BF16))
            dq_s[...] = _dot_exact(do, st) * eb
            dk_s[...] = _dot_exact(vf, dst) * ebl
            dstate_ref[j] = dst * jnp.exp(b_last) + _dot_exact(do, q * eb, trans_a=True)
            b_all, k_all = b, k
            for i in range(n_sub):
                r0 = i * B_SUB
                blk = pl.ds(r0, B_SUB)
                qi, bi = q_s[blk, :], b_s[blk, :]
                da_i = da[r0:r0 + B_SUB, :]
                dq_i = jnp.zeros((B_SUB, LANES), F32)
                if i > 0:
                    ref_b = b_s[pl.ds(r0 - 1, 1), :]
                    eq = jnp.exp(bi - ref_b)
                    ek = jnp.where(srow < r0, jnp.exp(jnp.minimum(ref_b - b_all, 0.0)), 0.0)
                    da_off = jnp.where(lane < r0, da_i, 0.0)
                    dq_i = _dot_exact(da_off, k_all * ek) * eq
                    dk_s[...] += _dot_exact(da_off, qi * eq, trans_a=True) * ek
                for sl in range(B_SUB):
                    s = r0 + sl
                    e = jnp.where(trow >= sl, jnp.exp(jnp.minimum(bi - b_s[pl.ds(s, 1), :], 0.0)), 0.0)
                    dac = jnp.sum(jnp.where(lane == s, da_i, 0.0), axis=-1, keepdims=True)
                    dq_i = dq_i + dac * k_s[pl.ds(s, 1), :] * e
                    dk_s[pl.ds(s, 1), :] += jnp.sum(dac * qi * e, axis=0, keepdims=True)
                dq_s[blk, :] += dq_i
            dq, dk = dq_s[...], dk_s[...]
            db = q * dq - k * dk
            dlog = _dot_exact(triu, db) + dlog_carry
            dlog_carry = dlog_carry + jnp.sum(db, axis=0, keepdims=True)
            dgate = dlog / gate - dk
            dqb_ref[rows, cols] = dq * (B_DIM ** -0.5) * (sq * (1.0 + qb * (1.0 - sq)))
            dfb_ref[rows, cols] = dgate * (1.0 - lbv) * sf * (1.0 - sf)
            dib_ref[rows, cols] = dv
            dlb_acc = dlb_acc + jnp.sum(dgate * (1.0 - sf), axis=0, keepdims=True)
            return dlog_carry, dlb_acc, dng_acc

        def step(ci, carry):
            c = nc - 1 - ci
            rows = pl.ds(pl.multiple_of(c * C, C), C)
            return tuple(one_head(j, c, rows, carry[j]) for j in range(hb))

        z = jnp.zeros((1, LANES), F32)
        final = lax.fori_loop(0, nc, step, ((z, z, z),) * hb)
        for j in range(hb):
            dlb_ref[j] = final[j][1]
            dng_ref[...] += final[j][2]

    def col(off):
        return pl.BlockSpec((S, hb * LANES), lambda g: (0, off // hb + g), pipeline_mode=pl.Buffered(1))

    hv = pl.BlockSpec((hb, 1, LANES), lambda g: (g, 0, 0))
    vec = pl.BlockSpec((1, LANES), lambda g: (0, 0))
    full = jax.ShapeDtypeStruct((S, n_heads * LANES), F32)
    return pl.pallas_call(
        body, name=name, grid=(n_heads // hb,),
        in_specs=[col(q_blk), col(f_blk), col(i_blk), col(g_blk), hv, vec, col(0),
                  pl.BlockSpec((hb, nc, LANES, LANES), lambda g: (g, 0, 0, 0), pipeline_mode=pl.Buffered(1)),
                  col(dout_blk)],
        out_specs=[pl.BlockSpec((S, hb * LANES), lambda g: (0, g))] * 4 + [hv, vec],
        out_shape=[full, full, full, full, jax.ShapeDtypeStruct((n_heads, 1, LANES), F32),
                   jax.ShapeDtypeStruct((1, LANES), F32)],
        scratch_shapes=[pltpu.VMEM((hb, LANES, LANES), F32)] + [pltpu.VMEM((hb, B_CHUNK, LANES), F32)] * 5,
        compiler_params=_cparams("arbitrary"),
    )(src, src, src, src, lb, norm_g.reshape(1, LANES), o, states, dout)


def _pad_heads_cols(w, n_heads):
    lead = w.shape[:-1]
    w = w.reshape(lead + (n_heads, C_DIM))
    w = jnp.pad(w, [(0, 0)] * len(lead) + [(0, 0), (0, LANES - C_DIM)])
    return w.reshape(lead + (n_heads * LANES,))


def _unpad_heads_cols(w, n_heads):
    lead = w.shape[:-1]
    return w.reshape(lead + (n_heads, LANES))[..., :C_DIM].reshape(lead + (n_heads * C_DIM,))


def _pad_heads_rows(w, n_heads):
    w = jnp.pad(w.reshape(n_heads, C_DIM, w.shape[1]), [(0, 0), (0, LANES - C_DIM), (0, 0)])
    return w.reshape(n_heads * LANES, w.shape[2])


def _unpad_heads_rows(w, n_heads):
    return w.reshape(n_heads, LANES, w.shape[1])[:, :C_DIM].reshape(n_heads * C_DIM, w.shape[1])


def _lower_bounds(lb_raw):
    lb_soft = jax.nn.softmax(lb_raw.astype(F32), axis=0)
    return jnp.cumsum(lb_soft, axis=0) - lb_soft[0:1]


def _relu2(u):
    r = jnp.maximum(u, 0.0)
    return r * r


def local_step(x, target, P, hooks):
    S, D = x.shape
    depth = P["norm_mix_g"].shape[0]
    HA = D // 2 // A_DIM
    HB = HA
    HQ = D // C_DIM
    HKV = HQ // C_GROUP
    even_in = 7 * HA * LANES
    tabs_a = rope_tables(S, A_DIM)
    tabs_c = rope_tables(S, C_DIM)
    half_a, half_c = tabs_a[3], tabs_c[3]
    lbounds = _lower_bounds(P["hgrn_lb_raw"])
    a_in = 3 * HA * LANES
    a_kw = dict(n_heads=HA, group=1, hb=A_HEADS_PER_STEP, q_blk=0, k_blk=HA, v_blk=2 * HA, blk_per_row=3 * HA,
                scale=A_DIM ** -0.5, half=half_a)
    b_kw = dict(S=S, n_heads=HB, hb=B_HEADS_PER_STEP, q_blk=0, f_blk=HB, i_blk=2 * HB, g_blk=3 * HB)
    c_kw = dict(L=S, dil=1, n_heads=HQ, group=C_GROUP, hb=C_GROUP, q_blk=0, k_blk=HQ, v_blk=HQ + HKV,
                blk_per_row=HQ + 2 * HKV, max_dist=C_WINDOW - 1, scale=C_DIM ** -0.5, half=half_c)

    def a_tabs(dil):
        return tuple(t.reshape(S // dil, dil * LANES) for t in tabs_a[:3])

    saved = []
    for layer in range(depth):
        Wl = dict(hooks["weights_in"](layer, x))
        sv = {"x0": x, "W": Wl}
        idx = layer // 2
        h = rmsnorm_fwd(x, P["norm_mix_g"][layer], dep=Wl.pop("_dep", None), name="norm_mix_fwd")
        sv["h"] = h
        if layer % 2 == 0:
            qkv = matmul(h, Wl["even_w_in"], b_cols=(0, a_in), out_dtypes=(BF16,), name="even_in_fwd_a")
            proj = matmul(h, Wl["even_w_in"], b_cols=(a_in, even_in), out_dtypes=(F32,), name="even_in_fwd_b")
            parts = []
            sv["qkv"] = {dil: qkv.reshape(S // dil, dil * a_in) for _, dil in A_BRANCHES}
            for window, dil in A_BRANCHES:
                L = S // dil
                res = banded_fwd(sv["qkv"][dil], a_tabs(dil), L=L, dil=dil, max_dist=window // dil, normalize=False,
                                 name=f"dilated_fwd_d{dil}", **a_kw)
                parts.append(tuple(r.reshape(S, HA * LANES) for r in res))
            oa, lse = merge_branches(parts, name="dilated_merge")
            hooks["mid"](layer, oa)
            lb_e = lbounds[idx].reshape(HB, 1, LANES)
            ob, o_raw, states = hgrn_fwd(proj, lb_e, P["hgrn_norm_g"][idx], name="hgrn_fwd", **b_kw)
            Wl.update(hooks["weights_rest"](layer, ob))
            mixed = jnp.concatenate([oa, ob], axis=1).astype(BF16)
            sv.update(proj=proj, oa=oa, lse=lse, o_raw=o_raw, states=states, mixed=mixed, lb=lb_e)
            x = matmul(mixed, Wl["even_w_out"], extras=(x,), epilogue=lambda acc, r: (acc + r,),
                       dep=Wl.pop("_dep", None), out_dtypes=(F32,), name="even_out_fwd")
        else:
            proj = matmul(h, Wl["odd_w_qkv"], extras=(P["odd_b_qkv"][idx].reshape(1, -1),),
                          epilogue=lambda acc, b: (acc + b,), out_dtypes=(F32,), name="odd_qkv_fwd")
            sink = jnp.broadcast_to(P["odd_sinks"][idx].reshape(HQ, 1, 1), (HQ, 1, LANES))
            o, lse = banded_fwd(proj, tabs_c[:3], sink=sink, normalize=True, name="swa_fwd", **c_kw)
            hooks["mid"](layer, o)
            Wl.update(hooks["weights_rest"](layer, o))
            sv.update(proj=proj, o=o, lse=lse, sink=sink)
            x = matmul(o, Wl["odd_w_o"], extras=(P["odd_b_o"][idx].reshape(1, D), x),
                       epilogue=lambda acc, b, r: (acc + b + r,), out_dtypes=(F32,), name="odd_out_fwd")
        sv["x1"] = x
        h2 = rmsnorm_fwd(x, P["norm_mlp_g"][layer], dep=hooks["after_mixer"](layer, x), name="norm_mlp_fwd")
        u = matmul(h2, Wl["mlp_w1"], out_dtypes=(BF16,), name="mlp_up_fwd")
        x = matmul(u, Wl["mlp_w2"], a_fn=_relu2, extras=(x,), epilogue=lambda acc, r: (acc + r,),
                   out_dtypes=(F32,), name="mlp_down_fwd")
        sv.update(h2=h2, u=u)
        saved.append(sv)

    loss, dx, dgf = final_norm_loss(x, P["final_norm_g"], target, name="final_norm_loss")

    G = {k: [None] * depth for k in ("norm_mix_g", "norm_mlp_g")}
    for k in ("hgrn_lb", "hgrn_norm_g"):
        G[k] = [None] * ((depth + 1) // 2)
    for k in ("odd_w_qkv", "odd_b_qkv", "odd_sinks", "odd_w_o", "odd_b_o"):
        G[k] = [None] * (depth // 2)
    G["final_norm_g"] = dgf.reshape(D)
    B = {}

    def wgrad(a, d, key, idx, name, a_fn=None):
        M, N = a.shape[1], d.shape[1]
        into = (grad_buffer_shape((M, N), SHARDING[key]), SHARDING[key])
        B[key, idx] = matmul(a, d, trans_a=True, a_fn=a_fn, out_dtypes=(BF16,), into=into, name=name)

    dep = None
    for layer in reversed(range(depth)):
        sv = saved[layer]
        Wl = sv["W"]
        idx = layer // 2
        du = matmul(dx, Wl["mlp_w2"], trans_b=True, extras=(sv["u"],), dep=dep,
                    epilogue=lambda acc, u: (acc * (2.0 * jnp.maximum(u.astype(F32), 0.0)),),
                    out_dtypes=(BF16,), name="mlp_down_bwd_x")
        wgrad(sv["u"], dx, "mlp_w2", layer, "mlp_down_bwd_w", a_fn=_relu2)
        wgrad(sv["h2"], du, "mlp_w1", layer, "mlp_up_bwd_w")
        dh2 = matmul(du, Wl["mlp_w1"], trans_b=True, out_dtypes=(F32,), name="mlp_up_bwd_x")
        dx, dg = rmsnorm_bwd(sv["x1"], P["norm_mlp_g"][layer], dh2, dx, name="norm_mlp_bwd")
        G["norm_mlp_g"][layer] = dg.reshape(D)
        dep = hooks["grads_ready"]([("mlp_w2", layer), ("mlp_w1", layer)], B, dx)
        if layer % 2 == 0:
            dmixed = matmul(dx, Wl["even_w_out"], trans_b=True, dep=dep, out_dtypes=(F32,), name="even_out_bwd_x")
            wgrad(sv["mixed"], dx, "even_w_out", idx, "even_out_bwd_w")
            acc = None
            for window, dil in A_BRANCHES:
                L = S // dil
                rs = lambda a: a.reshape(L, -1)
                acc = banded_bwd(sv["qkv"][dil], a_tabs(dil), rs(sv["oa"]), rs(sv["lse"]), rs(dmixed),
                                         L=L, dil=dil, max_dist=window // dil, do_bpr=2 * HA,
                                         acc=None if acc is None else tuple(rs(a) for a in acc),
                                         name=f"dilated_bwd_d{dil}", **a_kw)
                acc = tuple(a.reshape(S, HA * LANES) for a in acc)
            dqb, dfb, dib, dgb, dlb, dng = hgrn_bwd(sv["proj"], sv["lb"], P["hgrn_norm_g"][idx], sv["o_raw"],
                                                    sv["states"], dmixed, dout_blk=HA, name="hgrn_bwd", **b_kw)
            G["hgrn_lb"][idx] = dlb.reshape(HB * LANES)
            G["hgrn_norm_g"][idx] = dng.reshape(LANES)
            dproj = jnp.concatenate(list(acc) + [dqb, dfb, dib, dgb], axis=1).astype(BF16)
            wgrad(sv["h"], dproj, "even_w_in", idx, "even_in_bwd_w")
            dh = matmul(dproj, Wl["even_w_in"], trans_b=True, out_dtypes=(F32,), name="even_in_bwd_x")
        else:
            do = matmul(dx, Wl["odd_w_o"], trans_b=True, dep=dep, out_dtypes=(F32,), name="odd_out_bwd_x")
            G["odd_b_o"][idx] = colsum(dx, name="odd_out_bwd_b").reshape(D)
            G["odd_w_o"][idx] = matmul(sv["o"], dx, trans_a=True, out_dtypes=(BF16,), name="odd_out_bwd_w")
            dq, dk, dv, dsink = banded_bwd(sv["proj"], tabs_c[:3], sv["o"], sv["lse"], do, sink=sv["sink"],
                                                   do_bpr=HQ, name="swa_bwd", **c_kw)
            G["odd_sinks"][idx] = dsink[:, 0, 0]
            dproj = jnp.concatenate([dq, dk, dv], axis=1)
            G["odd_b_qkv"][idx] = colsum(dproj, name="odd_qkv_bwd_b").reshape(-1)
            dproj = dproj.astype(BF16)
            G["odd_w_qkv"][idx] = matmul(sv["h"], dproj, trans_a=True, out_dtypes=(BF16,), name="odd_qkv_bwd_w")
            dh = matmul(dproj, Wl["odd_w_qkv"], trans_b=True, out_dtypes=(F32,), name="odd_qkv_bwd_x")
            HQ2 = HQ + 2 * HKV
            B["odd_w_qkv", idx] = full_to_buffer(_unpad_heads_cols(G["odd_w_qkv"][idx], HQ2), "cols")
            B["odd_w_o", idx] = full_to_buffer(_unpad_heads_rows(G["odd_w_o"][idx], HQ), "rows")
        dx, dg = rmsnorm_bwd(sv["x0"], P["norm_mix_g"][layer], dh, dx, name="norm_mix_bwd")
        G["norm_mix_g"][layer] = dg.reshape(D)
        mixer = ("even_w_out", "even_w_in") if layer % 2 == 0 else ("odd_w_o", "odd_w_qkv")
        dep = hooks["grads_ready"]([(k, idx) for k in mixer], B, dx)

    del G["odd_w_qkv"], G["odd_w_o"]
    grads = {k: jnp.stack(v) if isinstance(v, list) else v for k, v in G.items()}
    _, lb_vjp = jax.vjp(_lower_bounds, P["hgrn_lb_raw"])
    grads["hgrn_lb_raw"] = lb_vjp(grads.pop("hgrn_lb"))[0]
    grads["odd_b_qkv"] = _unpad_heads_cols(grads["odd_b_qkv"], HQ + 2 * HKV)
    return loss, dx, grads, B


def full_to_buffer(full, sharding):
    M, N = full.shape
    if sharding == "cols":
        return jnp.transpose(full.reshape(2, M // 2, 4, N // 4), (2, 0, 1, 3))
    return full.reshape(4, 2, M // 8, N)


MESH = pl.DeviceIdType.MESH
ANY = pl.BlockSpec(memory_space=pl.ANY)
SHARDING = {"even_w_in": "cols", "even_w_out": "rows", "odd_w_qkv": "cols", "odd_w_o": "rows",
            "mlp_w1": "cols", "mlp_w2": "rows"}
BIG = tuple(SHARDING)


def _mesh_pos():
    return lax.axis_index("x"), lax.axis_index("y"), lax.axis_index("c")


def _other_chips(x, y):
    return [(1 - x, y), (x, 1 - y), (1 - x, 1 - y)]


def cast_into_full(w, layer, sharding, s, *, name, tr=256):
    n, R, C = w.shape
    tr = _tile(R, tr)
    nr = R // tr

    def body(s_ref, w_ref, o_ref):
        o_ref[...] = w_ref[...].astype(BF16)

    if sharding == "cols":
        full, out_map = (R, 4 * C), (lambda i, s_ref: (i, s_ref[0]))
    else:
        full, out_map = (4 * R, C), (lambda i, s_ref: (s_ref[0] * nr + i, 0))
    grid_spec = pltpu.PrefetchScalarGridSpec(
        num_scalar_prefetch=1, grid=(nr,),
        in_specs=[pl.BlockSpec((None, tr, C), lambda i, s_ref: (layer, i, 0))],
        out_specs=pl.BlockSpec((tr, C), out_map))
    return pl.pallas_call(
        body, name=name, grid_spec=grid_spec, out_shape=jax.ShapeDtypeStruct(full, BF16),
        compiler_params=_cparams("parallel"),
    )(s.reshape(1).astype(jnp.int32), w)


HBM = pl.BlockSpec(memory_space=pltpu.HBM)
SEM = pl.BlockSpec(memory_space=pltpu.SEMAPHORE)
EFFECT = pltpu.SideEffectType.DATAFLOW_SIDE_EFFECTING


def _weight_window(ref, sharding, s, hf):
    M, N = ref.shape
    if sharding == "cols":
        rh, C = M // 2, N // 4
        return ref.at[pl.ds(hf * rh, rh), pl.ds(s * C, C)]
    rh = M // 8
    return ref.at[pl.ds((2 * s + hf) * rh, rh), :]


def _in_hbm(a):
    return pltpu.with_memory_space_constraint(a, pltpu.HBM)


def gather_start(groups, shardings, after, *, name):
    sizes = [len(g) for g in groups]
    flat = [m for g in groups for m in g]
    flat_sh = [sh for g in shardings for sh in g]
    T, ng = len(flat), len(groups)

    def body(*refs):
        sems = refs[T + 1:T + 1 + 6 * ng]
        thru = refs[T + 1 + 6 * ng:2 * T + 1 + 6 * ng]
        token = refs[2 * T + 1 + 6 * ng]
        x, y, c = _mesh_pos()
        pos = 0
        for g in range(ng):
            for t in range(sizes[g]):
                w = _weight_window(thru[pos], flat_sh[pos], 2 * x + y, c)
                for j, (cx, cy) in enumerate(_other_chips(x, y)):
                    pltpu.make_async_remote_copy(src_ref=w, dst_ref=w, send_sem=sems[6 * g + j],
                                                 recv_sem=sems[6 * g + 3 + j], device_id=(cx, cy, c),
                                                 device_id_type=MESH).start()
                pos += 1
        token[...] = jnp.zeros_like(token)

    outs = pl.pallas_call(
        body, name=name,
        in_specs=[HBM] * T + [ANY],
        out_specs=[SEM] * (6 * ng) + [HBM] * T + [pl.BlockSpec(memory_space=pltpu.VMEM)],
        out_shape=[pltpu.SemaphoreType.DMA(())] * (6 * ng) + [pltpu.HBM(m.shape, m.dtype) for m in flat]
        + [jax.ShapeDtypeStruct((8, LANES), F32)],
        input_output_aliases={t: 6 * ng + t for t in range(T)},
        compiler_params=pltpu.CompilerParams(has_side_effects=EFFECT),
    )(*[_in_hbm(m) for m in flat], after)
    res, pos = [], 6 * ng
    for g in range(ng):
        res.append((list(outs[6 * g:6 * g + 6]), list(outs[pos:pos + sizes[g]])))
        pos += sizes[g]
    return res, outs[-1]


def gather_forward(mats, shardings, sems, after, *, name):
    T = len(mats)

    def body(*refs):
        sems1 = refs[T:T + 6]
        sems2 = refs[T + 7:T + 13]
        thru = refs[T + 13:2 * T + 13]
        token = refs[2 * T + 13]
        x, y, c = _mesh_pos()
        chips = _other_chips(x, y)
        for t in range(T):
            own = _weight_window(thru[t], shardings[t], 2 * x + y, c)
            for j, (cx, cy) in enumerate(chips):
                landed = _weight_window(thru[t], shardings[t], 2 * cx + cy, c)
                first = pltpu.make_async_remote_copy(src_ref=own, dst_ref=landed, send_sem=sems1[j],
                                                     recv_sem=sems1[3 + j], device_id=(cx, cy, c),
                                                     device_id_type=MESH)
                first.wait_recv()
                first.wait_send()
        for t in range(T):
            for j, (cx, cy) in enumerate(chips):
                landed = _weight_window(thru[t], shardings[t], 2 * cx + cy, c)
                pltpu.make_async_remote_copy(src_ref=landed, dst_ref=landed, send_sem=sems2[j],
                                             recv_sem=sems2[3 + j], device_id=(x, y, 1 - c),
                                             device_id_type=MESH).start()
        token[...] = jnp.zeros_like(token)

    outs = pl.pallas_call(
        body, name=name,
        in_specs=[HBM] * T + [SEM] * 6 + [ANY],
        out_specs=[SEM] * 6 + [HBM] * T + [pl.BlockSpec(memory_space=pltpu.VMEM)],
        out_shape=[pltpu.SemaphoreType.DMA(())] * 6 + [pltpu.HBM(m.shape, m.dtype) for m in mats]
        + [jax.ShapeDtypeStruct((8, LANES), F32)],
        input_output_aliases={t: 6 + t for t in range(T)},
        compiler_params=pltpu.CompilerParams(has_side_effects=EFFECT),
    )(*mats, *sems, after)
    return list(outs[:6]), list(outs[6:6 + T]), outs[-1]


def gather_finish(mats, shardings, sems, after, *, name):
    T = len(mats)

    def body(*refs):
        sems2 = refs[T:T + 6]
        thru = refs[T + 7:2 * T + 7]
        x, y, c = _mesh_pos()
        for t in range(T):
            for j, (cx, cy) in enumerate(_other_chips(x, y)):
                sent = _weight_window(thru[t], shardings[t], 2 * cx + cy, c)
                other = _weight_window(thru[t], shardings[t], 2 * cx + cy, 1 - c)
                cp = pltpu.make_async_remote_copy(src_ref=sent, dst_ref=other, send_sem=sems2[j],
                                                  recv_sem=sems2[3 + j], device_id=(x, y, 1 - c),
                                                  device_id_type=MESH)
                cp.wait_recv()
                cp.wait_send()

    outs = pl.pallas_call(
        body, name=name,
        in_specs=[HBM] * T + [SEM] * 6 + [ANY],
        out_specs=[HBM] * T,
        out_shape=[pltpu.HBM(m.shape, m.dtype) for m in mats],
        input_output_aliases={t: t for t in range(T)},
        compiler_params=pltpu.CompilerParams(has_side_effects=EFFECT),
    )(*mats, *sems, after)
    return list(outs)


def exchange_halves(bufs, *, name):
    T = len(bufs)

    def body(*refs):
        in_refs, out_refs = refs[:T], refs[T:2 * T]
        send_sems, recv_sems = refs[2 * T:]
        x, y, c = _mesh_pos()
        cps = [pltpu.make_async_remote_copy(src_ref=in_refs[t].at[:, pl.ds(1 - c, 1)], dst_ref=out_refs[t],
                                            send_sem=send_sems.at[t], recv_sem=recv_sems.at[t],
                                            device_id=(x, y, 1 - c), device_id_type=MESH) for t in range(T)]
        for cp in cps:
            cp.start()
        for cp in cps:
            cp.wait()

    return pl.pallas_call(
        body, name=name, in_specs=[ANY] * T, out_specs=[ANY] * T,
        out_shape=[jax.ShapeDtypeStruct((4, 1) + b.shape[2:], b.dtype) for b in bufs],
        scratch_shapes=[pltpu.SemaphoreType.DMA((T,)), pltpu.SemaphoreType.DMA((T,))],
    )(*bufs)


def pair_sum(buf, got, c, *, name, tr=512):
    _, _, n, rh, C = buf.shape
    tr = _tile(rh, tr)

    def body(c_ref, a_ref, b_ref, o_ref):
        o_ref[...] = (a_ref[...].astype(F32) + b_ref[...].astype(F32)).astype(o_ref.dtype)

    grid_spec = pltpu.PrefetchScalarGridSpec(
        num_scalar_prefetch=1, grid=(4, n, rh // tr),
        in_specs=[pl.BlockSpec((None, None, None, tr, C), lambda s, l, i, c_ref: (s, c_ref[0], l, i, 0)),
                  pl.BlockSpec((None, None, None, tr, C), lambda s, l, i, c_ref: (s, 0, l, i, 0))],
        out_specs=pl.BlockSpec((None, None, tr, C), lambda s, l, i, c_ref: (s, l, i, 0)))
    return pl.pallas_call(
        body, name=name, grid_spec=grid_spec, out_shape=jax.ShapeDtypeStruct((4, n, rh, C), BF16),
        compiler_params=_cparams("parallel", "parallel", "parallel"),
    )(c.reshape(1).astype(jnp.int32), buf, got)


def scatter_to_owners(sums, *, name):
    T = len(sums)

    def body(*refs):
        in_refs, out_refs = refs[:T], refs[T:2 * T]
        send_sems, recv_sems = refs[2 * T:]
        x, y, c = _mesh_pos()
        chips = _other_chips(x, y)
        cps = [pltpu.make_async_remote_copy(src_ref=in_refs[t].at[pl.ds(2 * cx + cy, 1)],
                                            dst_ref=out_refs[t].at[pl.ds(j, 1)],
                                            send_sem=send_sems.at[t, j], recv_sem=recv_sems.at[t, j],
                                            device_id=(cx, cy, c), device_id_type=MESH)
               for t in range(T) for j, (cx, cy) in enumerate(chips)]
        for cp in cps:
            cp.start()
        for cp in cps:
            cp.wait()

    return pl.pallas_call(
        body, name=name, in_specs=[ANY] * T, out_specs=[ANY] * T,
        out_shape=[jax.ShapeDtypeStruct((3,) + s.shape[1:], s.dtype) for s in sums],
        scratch_shapes=[pltpu.SemaphoreType.DMA((T, 3)), pltpu.SemaphoreType.DMA((T, 3))],
    )(*sums)


def exchange_start(bufs, after, *, name):
    T = len(bufs)
    lands = [_in_hbm(lax.empty((4, 1) + b.shape[2:], b.dtype)) for b in bufs]

    def body(*refs):
        send_sem, recv_sem = refs[2 * T + 1], refs[2 * T + 2]
        src, dst = refs[2 * T + 3:3 * T + 3], refs[3 * T + 3:4 * T + 3]
        token = refs[4 * T + 3]
        x, y, c = _mesh_pos()
        for t in range(T):
            pltpu.make_async_remote_copy(src_ref=src[t].at[:, pl.ds(1 - c, 1)], dst_ref=dst[t], send_sem=send_sem,
                                         recv_sem=recv_sem, device_id=(x, y, 1 - c), device_id_type=MESH).start()
        token[...] = jnp.zeros_like(token)

    outs = pl.pallas_call(
        body, name=name,
        in_specs=[HBM] * (2 * T) + [ANY],
        out_specs=[SEM] * 2 + [HBM] * (2 * T) + [pl.BlockSpec(memory_space=pltpu.VMEM)],
        out_shape=[pltpu.SemaphoreType.DMA(())] * 2 + [pltpu.HBM(a.shape, a.dtype) for a in list(bufs) + lands]
        + [jax.ShapeDtypeStruct((8, LANES), F32)],
        input_output_aliases={t: 2 + t for t in range(2 * T)},
        compiler_params=pltpu.CompilerParams(has_side_effects=EFFECT),
    )(*[_in_hbm(b) for b in bufs], *lands, after)
    return list(outs[:2]), list(outs[2:2 + T]), list(outs[2 + T:2 + 2 * T]), outs[-1]


def exchange_wait(bufs, lands, sems, after, *, name):
    T = len(bufs)

    def body(*refs):
        send_sem, recv_sem = refs[2 * T], refs[2 * T + 1]
        src, dst = refs[2 * T + 3:3 * T + 3], refs[3 * T + 3:4 * T + 3]
        x, y, c = _mesh_pos()
        for t in range(T):
            cp = pltpu.make_async_remote_copy(src_ref=src[t].at[:, pl.ds(1 - c, 1)], dst_ref=dst[t],
                                              send_sem=send_sem, recv_sem=recv_sem, device_id=(x, y, 1 - c),
                                              device_id_type=MESH)
            cp.wait_recv()
            cp.wait_send()

    outs = pl.pallas_call(
        body, name=name,
        in_specs=[HBM] * (2 * T) + [SEM] * 2 + [ANY],
        out_specs=[HBM] * (2 * T),
        out_shape=[pltpu.HBM(a.shape, a.dtype) for a in list(bufs) + list(lands)],
        input_output_aliases={t: t for t in range(2 * T)},
        compiler_params=pltpu.CompilerParams(has_side_effects=EFFECT),
    )(*bufs, *lands, *sems, after)
    return list(outs[:T]), list(outs[T:])


def scatter_start(sums, *, name):
    T = len(sums)
    lands = [_in_hbm(lax.empty((3,) + s.shape[1:], s.dtype)) for s in sums]

    def body(*refs):
        sems = refs[2 * T:2 * T + 6]
        src, dst = refs[2 * T + 6:3 * T + 6], refs[3 * T + 6:4 * T + 6]
        token = refs[4 * T + 6]
        x, y, c = _mesh_pos()
        for t in range(T):
            for j, (cx, cy) in enumerate(_other_chips(x, y)):
                pltpu.make_async_remote_copy(src_ref=src[t].at[pl.ds(2 * cx + cy, 1)], dst_ref=dst[t].at[pl.ds(j, 1)],
                                             send_sem=sems[j], recv_sem=sems[3 + j], device_id=(cx, cy, c),
                                             device_id_type=MESH).start()
        token[...] = jnp.zeros_like(token)

    outs = pl.pallas_call(
        body, name=name,
        in_specs=[HBM] * (2 * T),
        out_specs=[SEM] * 6 + [HBM] * (2 * T) + [pl.BlockSpec(memory_space=pltpu.VMEM)],
        out_shape=[pltpu.SemaphoreType.DMA(())] * 6 + [pltpu.HBM(a.shape, a.dtype) for a in list(sums) + lands]
        + [jax.ShapeDtypeStruct((8, LANES), F32)],
        input_output_aliases={t: 6 + t for t in range(2 * T)},
        compiler_params=pltpu.CompilerParams(has_side_effects=EFFECT),
    )(*[_in_hbm(s) for s in sums], *lands)
    return list(outs[:6]), list(outs[6:6 + T]), list(outs[6 + T:6 + 2 * T]), outs[-1]


def scatter_wait(sums, lands, sems, after, *, name):
    T = len(sums)

    def body(*refs):
        sem_refs = refs[2 * T:2 * T + 6]
        src, dst = refs[2 * T + 7:3 * T + 7], refs[3 * T + 7:4 * T + 7]
        x, y, c = _mesh_pos()
        for t in range(T):
            for j, (cx, cy) in enumerate(_other_chips(x, y)):
                cp = pltpu.make_async_remote_copy(src_ref=src[t].at[pl.ds(2 * cx + cy, 1)],
                                                  dst_ref=dst[t].at[pl.ds(j, 1)], send_sem=sem_refs[j],
                                                  recv_sem=sem_refs[3 + j], device_id=(cx, cy, c),
                                                  device_id_type=MESH)
                cp.wait_recv()
                cp.wait_send()

    outs = pl.pallas_call(
        body, name=name,
        in_specs=[HBM] * (2 * T) + [SEM] * 6 + [ANY],
        out_specs=[HBM] * (2 * T),
        out_shape=[pltpu.HBM(a.shape, a.dtype) for a in list(sums) + list(lands)],
        input_output_aliases={t: t for t in range(2 * T)},
        compiler_params=pltpu.CompilerParams(has_side_effects=EFFECT),
    )(*sums, *lands, *sems, after)
    return list(outs[:T]), list(outs[T:])


def owner_sum(sums, got, s, c, grad, layer, *, name, tr=512):
    rh, C = sums.shape[2:]
    tr = _tile(rh, tr)
    nr = rh // tr

    def body(sc_ref, a_ref, b0_ref, b1_ref, b2_ref, g_ref, o_ref):
        o_ref[...] = ((a_ref[...].astype(F32) + b0_ref[...].astype(F32)) + b1_ref[...].astype(F32)) \
            + b2_ref[...].astype(F32)

    def got_spec(j):
        return pl.BlockSpec((None, None, tr, C), lambda i, sc_ref: (j, 0, i, 0))

    grid_spec = pltpu.PrefetchScalarGridSpec(
        num_scalar_prefetch=1, grid=(nr,),
        in_specs=[pl.BlockSpec((None, None, tr, C), lambda i, sc_ref: (sc_ref[0], 0, i, 0)),
                  got_spec(0), got_spec(1), got_spec(2), ANY],
        out_specs=pl.BlockSpec((None, tr, C), lambda i, sc_ref: (layer, sc_ref[1] * nr + i, 0)))
    return pl.pallas_call(
        body, name=name, grid_spec=grid_spec, out_shape=jax.ShapeDtypeStruct(grad.shape, F32),
        input_output_aliases={5: 0},
        compiler_params=_cparams("parallel"),
    )(jnp.stack([s, c]).astype(jnp.int32), sums, got, got, got, grad)


def join_halves(grads, *, name):
    T = len(grads)

    def body(*refs):
        out_refs = refs[T:2 * T]
        send_sems, recv_sems = refs[2 * T:]
        x, y, c = _mesh_pos()

        def win(t, hf):
            rh = grads[t].shape[1] // 2
            return out_refs[t].at[:, pl.ds(hf * rh, rh), :]

        def remote(t, w):
            return pltpu.make_async_remote_copy(src_ref=w, dst_ref=w, send_sem=send_sems.at[t],
                                                recv_sem=recv_sems.at[t], device_id=(x, y, 1 - c),
                                                device_id_type=MESH)

        cps = [remote(t, win(t, c)) for t in range(T)]
        for cp in cps:
            cp.start()
        for t in range(T):
            remote(t, win(t, 1 - c)).wait_recv()
        for cp in cps:
            cp.wait_send()

    return pl.pallas_call(
        body, name=name, in_specs=[ANY] * T, out_specs=[ANY] * T,
        out_shape=[jax.ShapeDtypeStruct(g.shape, g.dtype) for g in grads],
        input_output_aliases={t: t for t in range(T)},
        scratch_shapes=[pltpu.SemaphoreType.DMA((T,)), pltpu.SemaphoreType.DMA((T,))],
    )(*grads)


def allreduce_small(v, *, name):
    rows = v.shape[0]

    def body(v_ref, out_ref, buf, send_sems, recv_sems):
        x, y, c = _mesh_pos()
        me = 4 * x + 2 * y + c
        flips = [(dx, dy, dc) for dx in (0, 1) for dy in (0, 1) for dc in (0, 1)][1:]

        def peer(f):
            return tuple(1 - p if d else p for d, p in zip(f, (x, y, c)))

        cps = []
        for k, f in enumerate(flips):
            px, py, pc = peer(f)
            cps.append(pltpu.make_async_remote_copy(src_ref=v_ref, dst_ref=buf.at[me], send_sem=send_sems.at[k],
                                                    recv_sem=recv_sems.at[k], device_id=(px, py, pc),
                                                    device_id_type=MESH))
        for cp in cps:
            cp.start()
        buf[me] = v_ref[...]
        for k, f in enumerate(flips):
            px, py, pc = peer(f)
            slot = buf.at[4 * px + 2 * py + pc]
            pltpu.make_async_remote_copy(src_ref=slot, dst_ref=slot, send_sem=send_sems.at[k],
                                         recv_sem=recv_sems.at[k], device_id=(px, py, pc),
                                         device_id_type=MESH).wait_recv()
        for cp in cps:
            cp.wait_send()
        acc = buf[0]
        for i in range(1, 8):
            acc = acc + buf[i]
        out_ref[...] = acc

    vm = pl.BlockSpec(memory_space=pltpu.VMEM)
    return pl.pallas_call(
        body, name=name, in_specs=[vm], out_specs=vm,
        out_shape=jax.ShapeDtypeStruct(v.shape, F32),
        scratch_shapes=[pltpu.VMEM((8, rows, LANES), F32), pltpu.SemaphoreType.DMA((7,)),
                        pltpu.SemaphoreType.DMA((7,))],
    )(v)


def _adam_math(w, m, v, g):
    m = ADAM_B1 * m + (1.0 - ADAM_B1) * g
    v = ADAM_B2 * v + (1.0 - ADAM_B2) * (g * g)
    m_hat = m / (1.0 - ADAM_B1 ** ADAM_STEP)
    v_hat = v / (1.0 - ADAM_B2 ** ADAM_STEP)
    delta = -ADAM_LR * (m_hat / (jnp.sqrt(v_hat) + ADAM_EPS) + ADAM_WD * w)
    return delta, m, v


def adamw(w, m, v, g, *, name, tr=256):
    def body(w_ref, m_ref, v_ref, g_ref, d_ref, nm_ref, nv_ref):
        d, nm, nv = _adam_math(w_ref[...], m_ref[...], v_ref[...], g_ref[...])
        d_ref[...] = d
        nm_ref[...] = nm
        nv_ref[...] = nv

    shape = jax.ShapeDtypeStruct(w.shape, F32)
    if w.ndim == 2:
        return pl.pallas_call(body, name=name, out_shape=[shape] * 3)(w, m, v, g)
    n, R, C = w.shape
    tr = _tile(R, tr)
    spec = pl.BlockSpec((None, tr, C), lambda l, i: (l, i, 0))
    return pl.pallas_call(
        body, name=name, grid=(n, R // tr), in_specs=[spec] * 4, out_specs=[spec] * 3, out_shape=[shape] * 3,
        compiler_params=_cparams("parallel", "parallel"),
    )(w, m, v, g)


def _pack(parts):
    flat = jnp.concatenate([p.reshape(-1).astype(F32) for p in parts])
    size = -(-flat.shape[0] // (8 * LANES)) * (8 * LANES)
    return jnp.pad(flat, (0, size - flat.shape[0])).reshape(size // LANES, LANES)


def _unpack(block, shapes):
    flat = block.reshape(-1)
    out, pos = [], 0
    for shp in shapes:
        size = int(np.prod(shp))
        out.append(flat[pos:pos + size].reshape(shp))
        pos += size
    return out


SMALL = ("norm_mix_g", "norm_mlp_g", "final_norm_g", "hgrn_lb_raw", "hgrn_norm_g", "odd_sinks")
WEIGHTS = ("norm_mix_g", "norm_mlp_g", "final_norm_g", "even_w_in", "even_w_out", "hgrn_lb_raw", "hgrn_norm_g",
           "odd_w_qkv", "odd_b_qkv", "odd_sinks", "odd_w_o", "odd_b_o", "mlp_w1", "mlp_w2")


def kernel(x, norm_mix_g, norm_mlp_g, final_norm_g, even_w_in, even_w_out, hgrn_lb_raw, hgrn_norm_g, odd_w_qkv, odd_b_qkv, odd_sinks, odd_w_o, odd_b_o, mlp_w1, mlp_w2, loss_target, m_norm_mix_g, m_norm_mlp_g, m_final_norm_g, m_even_w_in, m_even_w_out, m_hgrn_lb_raw, m_hgrn_norm_g, m_odd_w_qkv, m_odd_b_qkv, m_odd_sinks, m_odd_w_o, m_odd_b_o, m_mlp_w1, m_mlp_w2, v_norm_mix_g, v_norm_mlp_g, v_final_norm_g, v_even_w_in, v_even_w_out, v_hgrn_lb_raw, v_hgrn_norm_g, v_odd_w_qkv, v_odd_b_qkv, v_odd_sinks, v_odd_w_o, v_odd_b_o, v_mlp_w1, v_mlp_w2):
    args = locals()
    W = {k: args[k] for k in WEIGHTS}
    M = {k: args["m_" + k] for k in WEIGHTS}
    V = {k: args["v_" + k] for k in WEIGHTS}
    _, S, D = x.shape
    depth = norm_mix_g.shape[0]
    HQ = D // C_DIM
    HKV = HQ // C_GROUP
    xi, yi, ci = _mesh_pos()
    shard = 2 * xi + yi

    n_odd, bq = odd_b_qkv.shape
    bo = odd_b_o.shape[1]
    keep = (ci == 0).astype(F32)
    pieces = [lax.dynamic_update_slice(jnp.zeros((n_odd, 4 * bq), F32), odd_b_qkv * keep, (0, shard * bq)),
              lax.dynamic_update_slice(jnp.zeros((n_odd, 4 * bo), F32), odd_b_o * keep, (0, shard * bo))]
    biases = allreduce_small(_pack(pieces), name="gather_biases")
    b_qkv_full, b_o_full = _unpack(biases, [(n_odd, 4 * bq), (n_odd, 4 * bo)])
    P = {k: W[k] for k in SMALL}
    P.update(odd_b_qkv=_pad_heads_cols(b_qkv_full, HQ + 2 * HKV), odd_b_o=b_o_full)

    def layer_keys(layer):
        mixer = ("even_w_in", "even_w_out") if layer % 2 == 0 else ("odd_w_qkv", "odd_w_o")
        return [(k, layer // 2) for k in mixer] + [("mlp_w1", layer), ("mlp_w2", layer)]

    keys = [layer_keys(0)[:1], layer_keys(0)[1:]] + [layer_keys(layer) for layer in range(1, depth)]
    shardings = [[SHARDING[k] for k, _ in ks] for ks in keys]
    started, forwarded = {}, {}

    def start_groups(gs, after):
        mats = [[cast_into_full(W[k], i, SHARDING[k], shard, name="cast_" + k) for k, i in keys[g]] for g in gs]
        res, token = gather_start(mats, [shardings[g] for g in gs], after, name="gather_start")
        started.update(zip(gs, res))
        return token

    dep0 = start_groups([0, 1], biases)

    def forward_group(g, after):
        sems, mats = started[g]
        forwarded[g] = gather_forward(mats, shardings[g], sems, after, name="gather_forward")
        return forwarded[g][2]

    def finish_group(g, after):
        sems, mats, _ = forwarded[g]
        mats = gather_finish(mats, shardings[g], sems, after, name="gather_finish")
        Wl = {k: m for (k, _), m in zip(keys[g], mats)}
        if "odd_w_qkv" in Wl:
            Wl["odd_w_qkv"] = _pad_heads_cols(Wl["odd_w_qkv"], HQ + 2 * HKV)
            Wl["odd_w_o"] = _pad_heads_rows(Wl["odd_w_o"], HQ)
        return Wl

    def weights_in(layer, x_in):
        if layer == 0:
            forward_group(0, dep0)
        Wl = finish_group(layer + 1 if layer else 0, x_in)
        if 0 < layer < depth - 1:
            Wl["_dep"] = start_groups([layer + 2], x_in)
        return Wl

    def mid(layer, a):
        if layer == 0:
            token = forward_group(1, a)
            started["next"] = start_groups([2], token) if depth > 1 else token

    def weights_rest(layer, a):
        return dict(finish_group(1, a), _dep=started.pop("next")) if layer == 0 else {}

    def after_mixer(layer, x1):
        return forward_group(layer + 2, x1) if layer + 1 < depth else None

    reduced = {k: lax.empty(W[k].shape, F32) for k in BIG}
    exchanging, scattering = [], []

    def settle(entry, after):
        kis, (sems, sums, lands, _) = entry
        sums, lands = scatter_wait(sums, lands, sems, after, name="scatter_wait")
        for (k, i), s, g in zip(kis, sums, lands):
            reduced[k] = owner_sum(s, g, shard, ci, reduced[k], i, name="owner_sum_" + k)

    def scatter_exchanged(after):
        for kis, (sems, bufs, lands, _) in exchanging:
            bufs, lands = exchange_wait(bufs, lands, sems, after, name="exchange_wait")
            sums = [pair_sum(b, g, ci, name="pair_sum_" + k) for (k, _), b, g in zip(kis, bufs, lands)]
            scattering.append((kis, scatter_start(sums, name="scatter_start")))
        exchanging.clear()

    def grads_ready(kis, B, after):
        scatter_exchanged(after)
        while len(scattering) > MAX_SCATTERS_IN_FLIGHT:
            settle(scattering.pop(0), after)
        bufs = [B[ki].reshape((4, 2, 1) + B[ki].shape[2:]) for ki in kis]
        order = scattering[-1][1][3] if scattering else after
        exchanging.append((kis, exchange_start(bufs, order, name="exchange_start")))
        return exchanging[-1][1][3]

    hooks = dict(weights_in=weights_in, mid=mid, weights_rest=weights_rest, after_mixer=after_mixer,
                 grads_ready=grads_ready)
    loss, dx, G, B = local_step(x[0], loss_target[0], P, hooks)
    scatter_exchanged(dx)
    for entry in scattering:
        settle(entry, scattering[-1][1][3])
    grads = dict(zip(BIG, join_halves([reduced[k] for k in BIG], name="join_halves")))

    small_keys = SMALL + ("odd_b_qkv", "odd_b_o")
    small_shapes = [G[k].shape for k in small_keys]
    small = _unpack(allreduce_small(_pack([G[k] for k in small_keys]), name="reduce_small"), small_shapes)
    grads.update(zip(small_keys, small))
    grads["odd_b_qkv"] = lax.dynamic_slice(grads["odd_b_qkv"], (0, shard * bq), (n_odd, bq))
    grads["odd_b_o"] = lax.dynamic_slice(grads["odd_b_o"], (0, shard * bo), (n_odd, bo))

    deltas, new_m, new_v = {}, {}, {}
    for k in WEIGHTS:
        w2 = (lambda a: a.reshape(1, -1)) if W[k].ndim == 1 else (lambda a: a)
        d, nm, nv = adamw(w2(W[k]), w2(M[k]), w2(V[k]), w2(grads[k]), name="adamw_" + k)
        deltas[k], new_m[k], new_v[k] = (a.reshape(W[k].shape) for a in (d, nm, nv))

    loss = lax.psum(loss[0, 0], ("x", "y", "c"))
    return (loss, dx[None], *[grads[k] for k in WEIGHTS], *[deltas[k] for k in WEIGHTS],
            *[new_m[k] for k in WEIGHTS], *[new_v[k] for k in WEIGHTS])
```

```python
import functools
import math

import jax
import jax.numpy as jnp
import numpy as np
from jax import lax
from jax.experimental import pallas as pl
from jax.experimental.pallas import tpu as pltpu

F32 = jnp.float32
BF16 = jnp.bfloat16

NORM_EPS = 1e-5
ROPE_THETA = 500000.0
ROPE_FRACTION = 4
LANES = 128
BAND = 128
A_DIM = 128
A_BRANCHES = ((128, 1), (512, 4), (2048, 16))
B_DIM = 128
B_CHUNK = 64
B_SUB = 16
A_HEADS_PER_STEP = 2
B_HEADS_PER_STEP = 2
MAX_SCATTERS_IN_FLIGHT = 3
C_DIM = 64
C_GROUP = 8
C_WINDOW = 128

ADAM_LR = 0.001
ADAM_B1 = 0.9
ADAM_B2 = 0.999
ADAM_EPS = 1e-08
ADAM_WD = 0.01
ADAM_STEP = 10

VMEM_LIMIT = 56 * 1024 * 1024


def _cparams(*sem):
    return pltpu.CompilerParams(dimension_semantics=tuple(sem), vmem_limit_bytes=VMEM_LIMIT)


def _tile(n, want):
    if n <= want:
        return n
    t = want - want % LANES
    while n % t:
        t -= LANES
    assert t > 0, (n, want)
    return t


def matmul(a, b, *, trans_a=False, trans_b=False, out_dtypes, a_fn=None, epilogue=None, extras=(), name,
           b_cols=None, into=None, dep=None, tm=1024, tn=1024, tk=2048):
    if trans_a:
        K, M = a.shape
    else:
        M, K = a.shape
    if trans_b:
        N, K2 = b.shape
    else:
        K2, N = b.shape
    assert K == K2, (a.shape, b.shape)
    col0 = 0
    if b_cols is not None:
        assert not trans_b
        col0, N = b_cols[0], b_cols[1] - b_cols[0]
    if into is not None:
        rh, cs = into[0][2:]
        tm, tn = _tile(rh, tm), _tile(cs, tn)
    tm, tn, tk = _tile(M, tm), _tile(N, tn), _tile(K, tk)
    nk = K // tk
    n_extra = len(extras)
    n_out = len(out_dtypes)

    def body(*refs):
        a_ref, b_ref = refs[0], refs[1]
        extra_refs = refs[2:2 + n_extra]
        out_refs = refs[-1 - n_out:-1]
        acc_ref = refs[-1]
        k = pl.program_id(2)
        at = a_ref[...]
        if a_fn is not None:
            at = a_fn(at.astype(F32))
        at = at.astype(BF16)
        bt = b_ref[...].astype(BF16)
        dims = (((0,) if trans_a else (1,), (1,) if trans_b else (0,)), ((), ()))
        prod = lax.dot_general(at, bt, dims, preferred_element_type=F32)

        def finish(acc):
            ex = [r[...] for r in extra_refs]
            outs = epilogue(acc, *ex) if epilogue is not None else (acc,)
            for o_ref, o in zip(out_refs, outs):
                o_ref[...] = o.astype(o_ref.dtype)

        if nk == 1:
            finish(prod)
            return

        @pl.when(k == 0)
        def _():
            acc_ref[...] = prod

        @pl.when((k > 0) & (k < nk - 1))
        def _():
            acc_ref[...] += prod

        @pl.when(k == nk - 1)
        def _():
            finish(acc_ref[...] + prod)

    a_spec = pl.BlockSpec((tk, tm), lambda i, j, k: (k, i)) if trans_a else pl.BlockSpec((tm, tk), lambda i, j, k: (i, k))
    assert col0 % tn == 0, (col0, tn)
    b_spec = (pl.BlockSpec((tn, tk), lambda i, j, k: (j, k)) if trans_b
              else pl.BlockSpec((tk, tn), lambda i, j, k: (k, col0 // tn + j)))
    e_specs = []
    for e in extras:
        if e.shape == (1, N):
            e_specs.append(pl.BlockSpec((1, tn), lambda i, j, k: (0, j)))
        else:
            assert e.shape == (M, N), (e.shape, M, N)
            e_specs.append(pl.BlockSpec((tm, tn), lambda i, j, k: (i, j)))
    args = [a, b, *extras]
    in_specs = [a_spec, b_spec] + e_specs
    if dep is not None:
        args.append(dep)
        in_specs.append(pl.BlockSpec(dep.shape, lambda i, j, k: (0, 0)))
    if into is None:
        out_specs = [pl.BlockSpec((tm, tn), lambda i, j, k: (i, j)) for _ in out_dtypes]
        out_shape = [jax.ShapeDtypeStruct((M, N), dt) for dt in out_dtypes]
    else:
        buf_shape, sharding = into
        assert n_out == 1
        index = grad_buffer_index(buf_shape, (M, N), sharding, tm, tn)
        out_specs = [pl.BlockSpec((None, None, tm, tn), lambda i, j, k: index(i, j))]
        out_shape = [jax.ShapeDtypeStruct(buf_shape, out_dtypes[0])]
    outs = pl.pallas_call(
        body, name=name,
        grid=(M // tm, N // tn, nk),
        in_specs=in_specs, out_specs=out_specs, out_shape=out_shape,
        scratch_shapes=[pltpu.VMEM((tm, tn), F32)],
        compiler_params=_cparams("parallel", "parallel", "arbitrary"),
    )(*args)
    return outs[0] if n_out == 1 else tuple(outs)


def grad_buffer_shape(mat_shape, sharding):
    M, N = mat_shape
    return (4, 2, M // 2, N // 4) if sharding == "cols" else (4, 2, M // 8, N)


def grad_buffer_index(buf_shape, mat_shape, sharding, tm, tn):
    _, _, rh, cs = buf_shape
    M, N = mat_shape
    ib, jb = rh // tm, cs // tn
    if sharding == "cols":
        assert (M, N) == (2 * rh, 4 * cs), (buf_shape, mat_shape)
        return lambda i, j: (j // jb, i // ib, i % ib, j % jb)
    assert sharding == "rows" and (M, N) == (8 * rh, cs), (buf_shape, mat_shape)
    return lambda i, j: (i // (2 * ib), (i // ib) % 2, i % ib, j)


def rmsnorm_fwd(x, g, *, name, dep=None, tr=256):
    S, D = x.shape
    tr = _tile(S, tr)

    def body(x_ref, g_ref, *rest):
        h_ref = rest[-1]
        xv = x_ref[...]
        rstd = lax.rsqrt(jnp.mean(xv * xv, axis=-1, keepdims=True) + NORM_EPS)
        h_ref[...] = (xv * rstd * g_ref[...]).astype(h_ref.dtype)

    args = [x, g.reshape(1, D)]
    in_specs = [pl.BlockSpec((tr, D), lambda i: (i, 0)), pl.BlockSpec((1, D), lambda i: (0, 0))]
    if dep is not None:
        args.append(dep)
        in_specs.append(pl.BlockSpec(dep.shape, lambda i: (0, 0)))
    return pl.pallas_call(
        body, name=name, grid=(S // tr,),
        in_specs=in_specs,
        out_specs=pl.BlockSpec((tr, D), lambda i: (i, 0)),
        out_shape=jax.ShapeDtypeStruct((S, D), BF16),
        compiler_params=_cparams("parallel"),
    )(*args)


def _rms_bwd_rows(xv, gv, dh):
    rstd = lax.rsqrt(jnp.mean(xv * xv, axis=-1, keepdims=True) + NORM_EPS)
    xhat = xv * rstd
    dxhat = dh * gv
    dx = rstd * (dxhat - xhat * jnp.mean(dxhat * xhat, axis=-1, keepdims=True))
    return dx, dh * xhat


def rmsnorm_bwd(x, g, dh, dres, *, name, tr=256):
    S, D = x.shape
    tr = _tile(S, tr)

    def body(x_ref, g_ref, dh_ref, dres_ref, dx_ref, dg_ref):
        @pl.when(pl.program_id(0) == 0)
        def _():
            dg_ref[...] = jnp.zeros_like(dg_ref)

        dx, dgr = _rms_bwd_rows(x_ref[...], g_ref[...], dh_ref[...].astype(F32))
        dx_ref[...] = dres_ref[...] + dx
        dg_ref[...] += jnp.sum(dgr, axis=0, keepdims=True)

    row = pl.BlockSpec((tr, D), lambda i: (i, 0))
    vec = pl.BlockSpec((1, D), lambda i: (0, 0))
    return pl.pallas_call(
        body, name=name, grid=(S // tr,),
        in_specs=[row, vec, row, row],
        out_specs=[row, vec],
        out_shape=[jax.ShapeDtypeStruct((S, D), F32), jax.ShapeDtypeStruct((1, D), F32)],
        compiler_params=_cparams("arbitrary"),
    )(x, g.reshape(1, D), dh, dres)


def final_norm_loss(x, g, target, *, name, tr=256):
    S, D = x.shape
    tr = _tile(S, tr)

    def body(x_ref, g_ref, t_ref, loss_ref, dx_ref, dg_ref):
        @pl.when(pl.program_id(0) == 0)
        def _():
            dg_ref[...] = jnp.zeros_like(dg_ref)
            loss_ref[...] = jnp.zeros_like(loss_ref)

        xv, gv = x_ref[...], g_ref[...]
        rstd = lax.rsqrt(jnp.mean(xv * xv, axis=-1, keepdims=True) + NORM_EPS)
        err = xv * rstd * gv - t_ref[...]
        part = 0.5 * jnp.sum(jnp.mean(err * err, axis=-1, keepdims=True), axis=0, keepdims=True)
        loss_ref[...] += jnp.broadcast_to(part, loss_ref.shape)
        dx, dgr = _rms_bwd_rows(xv, gv, err * (1.0 / D))
        dx_ref[...] = dx
        dg_ref[...] += jnp.sum(dgr, axis=0, keepdims=True)

    row = pl.BlockSpec((tr, D), lambda i: (i, 0))
    vec = pl.BlockSpec((1, D), lambda i: (0, 0))
    return pl.pallas_call(
        body, name=name, grid=(S // tr,),
        in_specs=[row, vec, row],
        out_specs=[pl.BlockSpec((8, LANES), lambda i: (0, 0)), row, vec],
        out_shape=[jax.ShapeDtypeStruct((8, LANES), F32), jax.ShapeDtypeStruct((S, D), F32),
                   jax.ShapeDtypeStruct((1, D), F32)],
        compiler_params=_cparams("arbitrary"),
    )(x, g.reshape(1, D), target)


def colsum(a, *, name, tr=256):
    S, N = a.shape
    tr = _tile(S, tr)

    def body(a_ref, o_ref):
        @pl.when(pl.program_id(0) == 0)
        def _():
            o_ref[...] = jnp.zeros_like(o_ref)

        o_ref[...] += jnp.sum(a_ref[...].astype(F32), axis=0, keepdims=True)

    return pl.pallas_call(
        body, name=name, grid=(S // tr,),
        in_specs=[pl.BlockSpec((tr, N), lambda i: (i, 0))],
        out_specs=pl.BlockSpec((1, N), lambda i: (0, 0)),
        out_shape=jax.ShapeDtypeStruct((1, N), F32),
        compiler_params=_cparams("arbitrary"),
    )(a)


def rope_tables(seq, head_dim):
    rot = head_dim // ROPE_FRACTION
    half = rot // 2
    inv_freq = 1.0 / (ROPE_THETA ** (jnp.arange(0, rot, 2, dtype=F32) / rot))
    ang = jnp.arange(seq, dtype=F32)[:, None] * inv_freq[None, :]
    cos, sin = jnp.cos(ang), jnp.sin(ang)
    ones = jnp.ones((seq, LANES - 2 * half), F32)
    zeros = jnp.zeros((seq, LANES - 2 * half), F32)
    zh = jnp.zeros((seq, half), F32)
    c = jnp.concatenate([cos, cos, ones], axis=1)
    sa = jnp.concatenate([-sin, zh, zeros], axis=1)
    sb = jnp.concatenate([zh, sin, zeros], axis=1)
    return c, sa, sb, half


def _rope(x, c, sa, sb, half):
    return x * c + pltpu.roll(x, LANES - half, 1) * sa + pltpu.roll(x, half, 1) * sb


def _rope_bwd(d, c, sa, sb, half):
    return d * c + pltpu.roll(d * sa, half, 1) + pltpu.roll(d * sb, LANES - half, 1)


def _band_masks(n, max_dist):
    qi = lax.broadcasted_iota(jnp.int32, (BAND, BAND), 0)
    kj = lax.broadcasted_iota(jnp.int32, (BAND, BAND), 1)
    cur = kj <= qi
    prev = ((kj >= qi) if max_dist == BAND else (kj > qi)) & (n > 0)
    return prev, cur


def _dot_nt(a, b):
    return lax.dot_general(a, b, (((1,), (1,)), ((), ())), preferred_element_type=F32)


def _dot_tn(a, b):
    return lax.dot_general(a, b, (((0,), (0,)), ((), ())), preferred_element_type=F32)


def _dot(a, b):
    return jnp.dot(a, b, preferred_element_type=F32)


def band_attention_fwd(src, tabs, *, L, dil, n_heads, group, q_blk, k_blk, v_blk, blk_per_row, max_dist, scale,
                       half, sink=None, normalize, name):
    c_t, sa_t, sb_t = tabs
    nb = L // BAND
    W = dil * n_heads * LANES
    use_sink = sink is not None

    def body(*refs):
        q_ref, k_ref, v_ref, c_ref, sa_ref, sb_ref = refs[:6]
        pos = 6
        if use_sink:
            sink_ref = refs[pos]
            pos += 1
        o_ref, m_ref, l_ref = refs[pos:pos + 3] if not normalize else (refs[pos], refs[pos + 1], None)
        kr_ref, vb_ref = refs[-2], refs[-1]

        def prep(n, carry):
            rows = pl.ds(pl.multiple_of(n * BAND, BAND), BAND)
            kr_ref[rows, :] = _rope(k_ref[rows, :], c_ref[rows, :], sa_ref[rows, :], sb_ref[rows, :], half).astype(BF16)
            vb_ref[rows, :] = v_ref[rows, :].astype(BF16)
            return carry

        lax.fori_loop(0, nb, prep, 0)

        def step(n, carry):
            rows = pl.ds(pl.multiple_of(n * BAND, BAND), BAND)
            prow = pl.ds(pl.multiple_of(jnp.maximum(n - 1, 0) * BAND, BAND), BAND)
            q = _rope(q_ref[rows, :], c_ref[rows, :], sa_ref[rows, :], sb_ref[rows, :], half).astype(BF16)
            mp, mc = _band_masks(n, max_dist)
            sp = jnp.where(mp, _dot_nt(q, kr_ref[prow, :]) * scale, -jnp.inf)
            sc = jnp.where(mc, _dot_nt(q, kr_ref[rows, :]) * scale, -jnp.inf)
            m = jnp.maximum(jnp.max(sp, axis=-1, keepdims=True), jnp.max(sc, axis=-1, keepdims=True))
            if use_sink:
                sk = sink_ref[:, 0:1]
                m = jnp.maximum(m, sk)
            pp = jnp.exp(sp - m)
            pc = jnp.exp(sc - m)
            l = jnp.sum(pp, axis=-1, keepdims=True) + jnp.sum(pc, axis=-1, keepdims=True)
            if use_sink:
                l = l + jnp.exp(sk - m)
            num = _dot(pp.astype(BF16), vb_ref[prow, :]) + _dot(pc.astype(BF16), vb_ref[rows, :])
            if normalize:
                o_ref[rows, :] = num / l
                m_ref[rows, :] = jnp.broadcast_to(m + jnp.log(l), (BAND, LANES))
            else:
                o_ref[rows, :] = num
                m_ref[rows, :] = jnp.broadcast_to(m, (BAND, LANES))
                l_ref[rows, :] = jnp.broadcast_to(l, (BAND, LANES))
            return carry

        lax.fori_loop(0, nb, step, 0)

    def col(off, div):
        return pl.BlockSpec((L, LANES), lambda r, h: (0, r * blk_per_row + off + h // div))

    tab = pl.BlockSpec((L, LANES), lambda r, h: (0, r))
    out = pl.BlockSpec((L, LANES), lambda r, h: (0, r * n_heads + h))
    in_specs = [col(q_blk, 1), col(k_blk, group), col(v_blk, group), tab, tab, tab]
    args = [src, src, src, c_t, sa_t, sb_t]
    if use_sink:
        in_specs.append(pl.BlockSpec((None, 1, LANES), lambda r, h: (h, 0, 0)))
        args.append(sink)
    n_out = 2 if normalize else 3
    return pl.pallas_call(
        body, name=name, grid=(dil, n_heads),
        in_specs=in_specs,
        out_specs=[out] * n_out,
        out_shape=[jax.ShapeDtypeStruct((L, W), F32)] * n_out,
        scratch_shapes=[pltpu.VMEM((L, LANES), BF16), pltpu.VMEM((L, LANES), BF16)],
        compiler_params=_cparams("parallel", "arbitrary"),
    )(*args)


def band_attention_bwd(src, tabs, o, lse, do, *, L, dil, n_heads, group, q_blk, k_blk, v_blk, blk_per_row, max_dist,
                       scale, half, do_bpr, sink=None, acc=None, name):
    c_t, sa_t, sb_t = tabs
    nb = L // BAND
    n_kv = n_heads // group
    use_sink = sink is not None
    use_acc = acc is not None
    assert not (use_acc and group != 1)

    def body(*refs):
        q_ref, k_ref, v_ref, c_ref, sa_ref, sb_ref, o_ref, lse_ref, do_ref = refs[:9]
        pos = 9
        if use_sink:
            sink_ref = refs[pos]
            pos += 1
        if use_acc:
            aq_ref, ak_ref, av_ref = refs[pos:pos + 3]
            pos += 3
        dq_ref, dk_ref, dv_ref = refs[pos:pos + 3]
        pos += 3
        if use_sink:
            dsink_ref = refs[pos]
        kr_ref, vb_ref, dka_ref, dva_ref = refs[-4:]
        h = pl.program_id(1)

        def prep(n, carry):
            rows = pl.ds(pl.multiple_of(n * BAND, BAND), BAND)
            kr_ref[rows, :] = _rope(k_ref[rows, :], c_ref[rows, :], sa_ref[rows, :], sb_ref[rows, :], half).astype(BF16)
            vb_ref[rows, :] = v_ref[rows, :].astype(BF16)
            return carry

        lax.fori_loop(0, nb, prep, 0)

        @pl.when(h % group == 0)
        def _():
            dka_ref[...] = jnp.zeros_like(dka_ref)
            dva_ref[...] = jnp.zeros_like(dva_ref)

        def step(n, dsk):
            rows = pl.ds(pl.multiple_of(n * BAND, BAND), BAND)
            prow = pl.ds(pl.multiple_of(jnp.maximum(n - 1, 0) * BAND, BAND), BAND)
            cv, sav, sbv = c_ref[rows, :], sa_ref[rows, :], sb_ref[rows, :]
            q = _rope(q_ref[rows, :], cv, sav, sbv, half).astype(BF16)
            dov = do_ref[rows, :]
            lsev = lse_ref[rows, 0:1]
            delta = jnp.sum(dov * o_ref[rows, :], axis=-1, keepdims=True)
            dob = dov.astype(BF16)
            mp, mc = _band_masks(n, max_dist)
            kp, kc, vp, vc = kr_ref[prow, :], kr_ref[rows, :], vb_ref[prow, :], vb_ref[rows, :]
            pp = jnp.exp(jnp.where(mp, _dot_nt(q, kp) * scale, -jnp.inf) - lsev)
            pc = jnp.exp(jnp.where(mc, _dot_nt(q, kc) * scale, -jnp.inf) - lsev)
            dsp = (pp * (_dot_nt(dob, vp) - delta) * scale).astype(BF16)
            dsc = (pc * (_dot_nt(dob, vc) - delta) * scale).astype(BF16)
            dq = _rope_bwd(_dot(dsp, kp) + _dot(dsc, kc), cv, sav, sbv, half)
            if use_acc:
                dq = dq + aq_ref[rows, :]
            dq_ref[rows, :] = dq
            dka_ref[prow, :] += _dot_tn(dsp, q)
            dka_ref[rows, :] += _dot_tn(dsc, q)
            dva_ref[prow, :] += _dot_tn(pp.astype(BF16), dob)
            dva_ref[rows, :] += _dot_tn(pc.astype(BF16), dob)
            if use_sink:
                dsk = dsk - jnp.sum(jnp.exp(sink_ref[:, 0:1] - lsev) * delta, axis=0, keepdims=True)
            return dsk

        dsk = lax.fori_loop(0, nb, step, jnp.zeros((1, 1), F32))
        if use_sink:
            dsink_ref[...] = jnp.broadcast_to(dsk, dsink_ref.shape)

        @pl.when(h % group == group - 1)
        def _():
            def fin(n, carry):
                rows = pl.ds(pl.multiple_of(n * BAND, BAND), BAND)
                dk = _rope_bwd(dka_ref[rows, :], c_ref[rows, :], sa_ref[rows, :], sb_ref[rows, :], half)
                dv = dva_ref[rows, :]
                if use_acc:
                    dk = dk + ak_ref[rows, :]
                    dv = dv + av_ref[rows, :]
                dk_ref[rows, :] = dk
                dv_ref[rows, :] = dv
                return carry

            lax.fori_loop(0, nb, fin, 0)

    def col(off, div):
        return pl.BlockSpec((L, LANES), lambda r, h: (0, r * blk_per_row + off + h // div))

    tab = pl.BlockSpec((L, LANES), lambda r, h: (0, r))
    qo = pl.BlockSpec((L, LANES), lambda r, h: (0, r * n_heads + h))
    kvo = pl.BlockSpec((L, LANES), lambda r, h: (0, r * n_kv + h // group))
    dospec = pl.BlockSpec((L, LANES), lambda r, h: (0, r * do_bpr + h))
    in_specs = [col(q_blk, 1), col(k_blk, group), col(v_blk, group), tab, tab, tab, qo, qo, dospec]
    args = [src, src, src, c_t, sa_t, sb_t, o, lse, do]
    if use_sink:
        in_specs.append(pl.BlockSpec((None, 1, LANES), lambda r, h: (h, 0, 0)))
        args.append(sink)
    if use_acc:
        in_specs += [qo, kvo, kvo]
        args += list(acc)
    out_specs = [qo, kvo, kvo]
    out_shape = [jax.ShapeDtypeStruct((L, dil * n_heads * LANES), F32),
                 jax.ShapeDtypeStruct((L, dil * n_kv * LANES), F32),
                 jax.ShapeDtypeStruct((L, dil * n_kv * LANES), F32)]
    if use_sink:
        out_specs.append(pl.BlockSpec((None, 1, LANES), lambda r, h: (h, 0, 0)))
        out_shape.append(jax.ShapeDtypeStruct((n_heads, 1, LANES), F32))
    return pl.pallas_call(
        body, name=name, grid=(dil, n_heads),
        in_specs=in_specs, out_specs=out_specs, out_shape=out_shape,
        scratch_shapes=[pltpu.VMEM((L, LANES), BF16), pltpu.VMEM((L, LANES), BF16),
                        pltpu.VMEM((L, LANES), F32), pltpu.VMEM((L, LANES), F32)],
        compiler_params=_cparams("parallel", "arbitrary"),
    )(*args)


def banded_fwd(src, tabs, *, L, dil, n_heads, group, hb, q_blk, k_blk, v_blk, blk_per_row, max_dist, scale, half,
               sink=None, normalize, name):
    c_t, sa_t, sb_t = tabs
    nb = L // BAND
    kvb = hb // group
    n_hg = n_heads // hb
    W = dil * n_heads * LANES
    use_sink = sink is not None
    assert n_heads % hb == 0 and hb % group == 0
    assert (q_blk % hb, k_blk % kvb, v_blk % kvb) == (0, 0, 0) and (dil == 1 or blk_per_row % hb == 0)

    def body(*refs):
        q_ref, k_ref, v_ref, c_ref, sa_ref, sb_ref = refs[:6]
        pos = 6
        if use_sink:
            sink_ref = refs[pos]
            pos += 1
        o_ref, m_ref = refs[pos], refs[pos + 1]
        l_ref = None if normalize else refs[pos + 2]
        kr_ref, vb_ref = refs[-2], refs[-1]
        n = pl.program_id(2)

        @pl.when(n == 0)
        def _():
            def prep(b, carry):
                rows = pl.ds(pl.multiple_of(b * BAND, BAND), BAND)
                cv, sav, sbv = c_ref[rows, :], sa_ref[rows, :], sb_ref[rows, :]
                for j in range(kvb):
                    cols = slice(j * LANES, (j + 1) * LANES)
                    kr_ref[rows, cols] = _rope(k_ref[rows, cols].astype(F32), cv, sav, sbv, half).astype(BF16)
                    vb_ref[rows, cols] = v_ref[rows, cols].astype(BF16)
                return carry

            lax.fori_loop(0, nb, prep, 0)

        rows = pl.ds(pl.multiple_of(n * BAND, BAND), BAND)
        prow = pl.ds(pl.multiple_of(jnp.maximum(n - 1, 0) * BAND, BAND), BAND)
        cv, sav, sbv = c_ref[rows, :], sa_ref[rows, :], sb_ref[rows, :]
        mp, mc = _band_masks(n, max_dist)
        for i in range(hb):
            qc = slice(i * LANES, (i + 1) * LANES)
            kc = slice((i // group) * LANES, (i // group + 1) * LANES)
            q = _rope(q_ref[:, qc].astype(F32), cv, sav, sbv, half).astype(BF16)
            sp = jnp.where(mp, _dot_nt(q, kr_ref[prow, kc]) * scale, -jnp.inf)
            sc = jnp.where(mc, _dot_nt(q, kr_ref[rows, kc]) * scale, -jnp.inf)
            m = jnp.maximum(jnp.max(sp, axis=-1, keepdims=True), jnp.max(sc, axis=-1, keepdims=True))
            if use_sink:
                sk = sink_ref[i, :, 0:1]
                m = jnp.maximum(m, sk)
            pp = jnp.exp(sp - m)
            pc = jnp.exp(sc - m)
            l = jnp.sum(pp, axis=-1, keepdims=True) + jnp.sum(pc, axis=-1, keepdims=True)
            if use_sink:
                l = l + jnp.exp(sk - m)
            num = _dot(pp.astype(BF16), vb_ref[prow, kc]) + _dot(pc.astype(BF16), vb_ref[rows, kc])
            if normalize:
                o_ref[:, qc] = num / l
                m_ref[:, qc] = jnp.broadcast_to(m + jnp.log(l), (BAND, LANES))
            else:
                o_ref[:, qc] = num
                m_ref[:, qc] = jnp.broadcast_to(m, (BAND, LANES))
                l_ref[:, qc] = jnp.broadcast_to(l, (BAND, LANES))

    qspec = pl.BlockSpec((BAND, hb * LANES), lambda r, g, n: (n, (r * blk_per_row + q_blk) // hb + g))

    def kv(off):
        return pl.BlockSpec((L, kvb * LANES), lambda r, g, n: (0, (r * blk_per_row + off) // kvb + g))

    tab = pl.BlockSpec((L, LANES), lambda r, g, n: (0, r))
    out = pl.BlockSpec((BAND, hb * LANES), lambda r, g, n: (n, r * n_hg + g))
    in_specs = [qspec, kv(k_blk), kv(v_blk), tab, tab, tab]
    args = [src, src, src, c_t, sa_t, sb_t]
    if use_sink:
        in_specs.append(pl.BlockSpec((hb, 1, LANES), lambda r, g, n: (g, 0, 0)))
        args.append(sink)
    n_out = 2 if normalize else 3
    return pl.pallas_call(
        body, name=name, grid=(dil, n_hg, nb),
        in_specs=in_specs,
        out_specs=[out] * n_out,
        out_shape=[jax.ShapeDtypeStruct((L, W), F32)] * n_out,
        scratch_shapes=[pltpu.VMEM((L, kvb * LANES), BF16), pltpu.VMEM((L, kvb * LANES), BF16)],
        compiler_params=_cparams("parallel", "parallel", "arbitrary"),
    )(*args)


def banded_bwd(src, tabs, o, lse, do, *, L, dil, n_heads, group, hb, q_blk, k_blk, v_blk, blk_per_row, max_dist,
               scale, half, do_bpr, sink=None, acc=None, dep=None, name):
    c_t, sa_t, sb_t = tabs
    nb = L // BAND
    kvb = hb // group
    n_hg = n_heads // hb
    n_kv = n_heads // group
    use_sink = sink is not None
    use_acc = acc is not None
    assert n_heads % hb == 0 and hb % group == 0 and do_bpr % hb == 0
    assert (q_blk % hb, k_blk % kvb, v_blk % kvb) == (0, 0, 0) and (dil == 1 or blk_per_row % hb == 0)

    def body(*refs):
        q_ref, k_ref, v_ref, c_ref, sa_ref, sb_ref, o_ref, lse_ref, do_ref = refs[:9]
        pos = 9
        if use_sink:
            sink_ref = refs[pos]
            pos += 1
        if use_acc:
            aq_ref, ak_ref, av_ref = refs[pos:pos + 3]
            pos += 3
        if dep is not None:
            pos += 1
        dq_ref, dk_ref, dv_ref = refs[pos:pos + 3]
        pos += 3
        if use_sink:
            dsink_ref = refs[pos]
        kr_ref, vb_ref, dka_ref, dva_ref = refs[-4:]
        n = pl.program_id(2)

        @pl.when(n == 0)
        def _():
            def prep(b, carry):
                rows = pl.ds(pl.multiple_of(b * BAND, BAND), BAND)
                cv, sav, sbv = c_ref[rows, :], sa_ref[rows, :], sb_ref[rows, :]
                for j in range(kvb):
                    cols = slice(j * LANES, (j + 1) * LANES)
                    kr_ref[rows, cols] = _rope(k_ref[rows, cols].astype(F32), cv, sav, sbv, half).astype(BF16)
                    vb_ref[rows, cols] = v_ref[rows, cols].astype(BF16)
                return carry

            lax.fori_loop(0, nb, prep, 0)
            dka_ref[...] = jnp.zeros_like(dka_ref)
            dva_ref[...] = jnp.zeros_like(dva_ref)
            if use_sink:
                dsink_ref[...] = jnp.zeros_like(dsink_ref)

        rows = pl.ds(pl.multiple_of(n * BAND, BAND), BAND)
        prow = pl.ds(pl.multiple_of(jnp.maximum(n - 1, 0) * BAND, BAND), BAND)
        cv, sav, sbv = c_ref[rows, :], sa_ref[rows, :], sb_ref[rows, :]
        mp, mc = _band_masks(n, max_dist)
        for i in range(hb):
            qc = slice(i * LANES, (i + 1) * LANES)
            kc = slice((i // group) * LANES, (i // group + 1) * LANES)
            q = _rope(q_ref[:, qc].astype(F32), cv, sav, sbv, half).astype(BF16)
            dov = do_ref[:, qc]
            lsev = lse_ref[:, i * LANES:i * LANES + 1]
            delta = jnp.sum(dov * o_ref[:, qc], axis=-1, keepdims=True)
            dob = dov.astype(BF16)
            kp, kcur, vp, vcur = kr_ref[prow, kc], kr_ref[rows, kc], vb_ref[prow, kc], vb_ref[rows, kc]
            pp = jnp.exp(jnp.where(mp, _dot_nt(q, kp) * scale, -jnp.inf) - lsev)
            pc = jnp.exp(jnp.where(mc, _dot_nt(q, kcur) * scale, -jnp.inf) - lsev)
            dsp = (pp * (_dot_nt(dob, vp) - delta) * scale).astype(BF16)
            dsc = (pc * (_dot_nt(dob, vcur) - delta) * scale).astype(BF16)
            dq = _rope_bwd(_dot(dsp, kp) + _dot(dsc, kcur), cv, sav, sbv, half)
            if use_acc:
                dq = dq + aq_ref[:, qc]
            dq_ref[:, qc] = dq
            dka_ref[prow, kc] += _dot_tn(dsp, q)
            dka_ref[rows, kc] += _dot_tn(dsc, q)
            dva_ref[prow, kc] += _dot_tn(pp.astype(BF16), dob)
            dva_ref[rows, kc] += _dot_tn(pc.astype(BF16), dob)
            if use_sink:
                dsk = jnp.sum(jnp.exp(sink_ref[i, :, 0:1] - lsev) * delta, axis=0, keepdims=True)
                dsink_ref[i] -= jnp.broadcast_to(dsk, (1, LANES))

        @pl.when(n == nb - 1)
        def _():
            def fin(b, carry):
                brow = pl.ds(pl.multiple_of(b * BAND, BAND), BAND)
                bc, bsa, bsb = c_ref[brow, :], sa_ref[brow, :], sb_ref[brow, :]
                for j in range(kvb):
                    cols = slice(j * LANES, (j + 1) * LANES)
                    dk = _rope_bwd(dka_ref[brow, cols], bc, bsa, bsb, half)
                    dv = dva_ref[brow, cols]
                    if use_acc:
                        dk = dk + ak_ref[brow, cols]
                        dv = dv + av_ref[brow, cols]
                    dk_ref[brow, cols] = dk
                    dv_ref[brow, cols] = dv
                return carry

            lax.fori_loop(0, nb, fin, 0)

    qspec = pl.BlockSpec((BAND, hb * LANES), lambda r, g, n: (n, (r * blk_per_row + q_blk) // hb + g))

    def kv(off):
        return pl.BlockSpec((L, kvb * LANES), lambda r, g, n: (0, (r * blk_per_row + off) // kvb + g))

    tab = pl.BlockSpec((L, LANES), lambda r, g, n: (0, r))
    qo = pl.BlockSpec((BAND, hb * LANES), lambda r, g, n: (n, r * n_hg + g))
    kvo = pl.BlockSpec((L, kvb * LANES), lambda r, g, n: (0, r * (n_kv // kvb) + g))
    dospec = pl.BlockSpec((BAND, hb * LANES), lambda r, g, n: (n, r * (do_bpr // hb) + g))
    in_specs = [qspec, kv(k_blk), kv(v_blk), tab, tab, tab, qo, qo, dospec]
    args = [src, src, src, c_t, sa_t, sb_t, o, lse, do]
    sink_spec = pl.BlockSpec((hb, 1, LANES), lambda r, g, n: (g, 0, 0))
    if use_sink:
        in_specs.append(sink_spec)
        args.append(sink)
    if use_acc:
        in_specs += [qo, kvo, kvo]
        args += list(acc)
    if dep is not None:
        in_specs.append(pl.BlockSpec(dep.shape, lambda r, g, n: (0, 0)))
        args.append(dep)
    out_specs = [qo, kvo, kvo]
    out_shape = [jax.ShapeDtypeStruct((L, dil * n_heads * LANES), F32),
                 jax.ShapeDtypeStruct((L, dil * n_kv * LANES), F32),
                 jax.ShapeDtypeStruct((L, dil * n_kv * LANES), F32)]
    if use_sink:
        out_specs.append(sink_spec)
        out_shape.append(jax.ShapeDtypeStruct((n_heads, 1, LANES), F32))
    return pl.pallas_call(
        body, name=name, grid=(dil, n_hg, nb),
        in_specs=in_specs, out_specs=out_specs, out_shape=out_shape,
        scratch_shapes=[pltpu.VMEM((L, kvb * LANES), BF16), pltpu.VMEM((L, kvb * LANES), BF16),
                        pltpu.VMEM((L, kvb * LANES), F32), pltpu.VMEM((L, kvb * LANES), F32)],
        compiler_params=_cparams("parallel", "parallel", "arbitrary"),
    )(*args)


def merge_branches(parts, *, name, tr=256):
    S, W = parts[0][0].shape
    tr = _tile(S, tr)
    nbr = len(parts)

    def body(*refs):
        ins, (o_ref, lse_ref) = refs[:3 * nbr], refs[3 * nbr:]
        nums = [ins[3 * i][...] for i in range(nbr)]
        ms = [ins[3 * i + 1][...] for i in range(nbr)]
        ls = [ins[3 * i + 2][...] for i in range(nbr)]
        mx = functools.reduce(jnp.maximum, ms)
        ws = [jnp.exp(m - mx) for m in ms]
        num = sum(w * n for w, n in zip(ws, nums))
        den = sum(w * l for w, l in zip(ws, ls))
        o_ref[...] = num / den
        lse_ref[...] = mx + jnp.log(den)

    row = pl.BlockSpec((tr, W), lambda i: (i, 0))
    flat = [a for p in parts for a in p]
    return pl.pallas_call(
        body, name=name, grid=(S // tr,),
        in_specs=[row] * len(flat), out_specs=[row, row],
        out_shape=[jax.ShapeDtypeStruct((S, W), F32)] * 2,
        compiler_params=_cparams("parallel"),
    )(*flat)


def _sigmoid(x):
    return 1.0 / (1.0 + jnp.exp(-x))


def _tri(n, lower):
    r = lax.broadcasted_iota(jnp.int32, (n, n), 0)
    c = lax.broadcasted_iota(jnp.int32, (n, n), 1)
    return ((c <= r) if lower else (c >= r)).astype(F32)


def _dot_exact(a, b, trans_a=False, trans_b=False):
    dims = (((0,) if trans_a else (1,), (1,) if trans_b else (0,)), ((), ()))
    return lax.dot_general(a, b, dims, preferred_element_type=F32, precision=lax.Precision.HIGHEST)


def _hgrn_gates(qb, fb, lb):
    sq = _sigmoid(qb)
    q = qb * sq * (B_DIM ** -0.5)
    sf = _sigmoid(fb)
    gate = lb + (1.0 - lb) * sf
    return q, 1.0 - gate, gate, sq, sf


def _hgrn_intra_fwd(q_ref, k_ref, b_ref):
    C, n_sub = B_CHUNK, B_CHUNK // B_SUB
    b_all, k_all = b_ref[...], k_ref[...]
    srow = lax.broadcasted_iota(jnp.int32, (C, LANES), 0)
    lane = lax.broadcasted_iota(jnp.int32, (B_SUB, C), 1)
    trow = lax.broadcasted_iota(jnp.int32, (B_SUB, LANES), 0)
    blocks = []
    for i in range(n_sub):
        r0 = i * B_SUB
        qi, bi = q_ref[pl.ds(r0, B_SUB), :], b_ref[pl.ds(r0, B_SUB), :]
        if i == 0:
            a_i = jnp.zeros((B_SUB, C), F32)
        else:
            ref_b = b_ref[pl.ds(r0 - 1, 1), :]
            qt = (qi * jnp.exp(bi - ref_b)).astype(BF16)
            kt = jnp.where(srow < r0, k_all * jnp.exp(jnp.minimum(ref_b - b_all, 0.0)), 0.0).astype(BF16)
            a_i = _dot_nt(qt, kt)
        for sl in range(B_SUB):
            s = r0 + sl
            e = jnp.where(trow >= sl, jnp.exp(jnp.minimum(bi - b_ref[pl.ds(s, 1), :], 0.0)), 0.0)
            colv = jnp.sum(qi * k_ref[pl.ds(s, 1), :] * e, axis=-1, keepdims=True)
            a_i = a_i + jnp.where(lane == s, colv, 0.0)
        blocks.append(a_i)
    return jnp.concatenate(blocks, axis=0)


def hgrn_fwd(src, lb, norm_g, *, S, n_heads, hb, q_blk, f_blk, i_blk, g_blk, name):
    nc = S // B_CHUNK
    assert n_heads % hb == 0 and (q_blk % hb, f_blk % hb, i_blk % hb, g_blk % hb) == (0, 0, 0, 0)

    def body(qb_ref, fb_ref, ib_ref, gb_ref, lb_ref, ng_ref, out_ref, o_ref, st_ref, state_ref, q_s, k_s, b_s):
        state_ref[...] = jnp.zeros_like(state_ref)
        tril = _tri(B_CHUNK, True)

        def step(c, carry):
            rows = pl.ds(pl.multiple_of(c * B_CHUNK, B_CHUNK), B_CHUNK)
            for j in range(hb):
                cols = slice(j * LANES, (j + 1) * LANES)
                q, k, gate, _, _ = _hgrn_gates(qb_ref[rows, cols], fb_ref[rows, cols], lb_ref[j])
                b = _dot_exact(tril, jnp.log(gate))
                vf = ib_ref[rows, cols]
                v = vf.astype(BF16)
                q_s[j], k_s[j], b_s[j] = q, k, b
                st = state_ref[j]
                st_ref[j, c] = st
                a = _hgrn_intra_fwd(q_s.at[j], k_s.at[j], b_s.at[j])
                o = _dot(a.astype(BF16), v) + _dot_nt((q * jnp.exp(b)).astype(BF16), st.astype(BF16))
                b_last = b_s[j, pl.ds(B_CHUNK - 1, 1), :]
                state_ref[j] = st * jnp.exp(b_last) + _dot_exact(vf, k * jnp.exp(b_last - b), trans_a=True)
                o_ref[rows, cols] = o
                rstd = lax.rsqrt(jnp.mean(o * o, axis=-1, keepdims=True) + NORM_EPS)
                gb = gb_ref[rows, cols]
                out_ref[rows, cols] = o * rstd * ng_ref[...] * (gb * _sigmoid(gb))
            return carry

        lax.fori_loop(0, nc, step, 0)

    def col(off):
        return pl.BlockSpec((S, hb * LANES), lambda g: (0, off // hb + g))

    hv = pl.BlockSpec((hb, 1, LANES), lambda g: (g, 0, 0))
    return pl.pallas_call(
        body, name=name, grid=(n_heads // hb,),
        in_specs=[col(q_blk), col(f_blk), col(i_blk), col(g_blk), hv, pl.BlockSpec((1, LANES), lambda g: (0, 0))],
        out_specs=[col(0), col(0), pl.BlockSpec((hb, nc, LANES, LANES), lambda g: (g, 0, 0, 0))],
        out_shape=[jax.ShapeDtypeStruct((S, n_heads * LANES), F32), jax.ShapeDtypeStruct((S, n_heads * LANES), F32),
                   jax.ShapeDtypeStruct((n_heads, nc, LANES, LANES), F32)],
        scratch_shapes=[pltpu.VMEM((hb, LANES, LANES), F32)] + [pltpu.VMEM((hb, B_CHUNK, LANES), F32)] * 3,
        compiler_params=_cparams("parallel"),
    )(src, src, src, src, lb, norm_g.reshape(1, LANES))


def hgrn_bwd(src, lb, norm_g, o, states, dout, *, S, n_heads, hb, q_blk, f_blk, i_blk, g_blk, dout_blk, name):
    nc = S // B_CHUNK
    C, n_sub = B_CHUNK, B_CHUNK // B_SUB
    assert n_heads % hb == 0 and (q_blk % hb, f_blk % hb, i_blk % hb, g_blk % hb, dout_blk % hb) == (0, 0, 0, 0, 0)

    def body(qb_ref, fb_ref, ib_ref, gb_ref, lb_ref, ng_ref, o_ref, st_ref, dout_ref,
             dqb_ref, dfb_ref, dib_ref, dgb_ref, dlb_ref, dng_ref, dstate_ref, q_sh, k_sh, b_sh, dq_sh, dk_sh):
        @pl.when(pl.program_id(0) == 0)
        def _():
            dng_ref[...] = jnp.zeros_like(dng_ref)

        dstate_ref[...] = jnp.zeros_like(dstate_ref)
        tril = _tri(C, True)
        triu = _tri(C, False)
        ngv = ng_ref[...]
        srow = lax.broadcasted_iota(jnp.int32, (C, LANES), 0)
        lane = lax.broadcasted_iota(jnp.int32, (B_SUB, C), 1)
        trow = lax.broadcasted_iota(jnp.int32, (B_SUB, LANES), 0)
        causal = lax.broadcasted_iota(jnp.int32, (C, C), 1) <= lax.broadcasted_iota(jnp.int32, (C, C), 0)

        def one_head(j, c, rows, carry):
            dlog_carry, dlb_acc, dng_acc = carry
            cols = slice(j * LANES, (j + 1) * LANES)
            q_s, k_s, b_s, dq_s, dk_s = q_sh.at[j], k_sh.at[j], b_sh.at[j], dq_sh.at[j], dk_sh.at[j]
            lbv = lb_ref[j]
            qb, fb, gb = qb_ref[rows, cols], fb_ref[rows, cols], gb_ref[rows, cols]
            q, k, gate, sq, sf = _hgrn_gates(qb, fb, lbv)
            b = _dot_exact(tril, jnp.log(gate))
            vf = ib_ref[rows, cols]
            v = vf.astype(BF16)
            q_s[...], k_s[...], b_s[...] = q, k, b
            st = st_ref[j, c]
            dst = dstate_ref[j]

            ov = o_ref[rows, cols]
            dout = dout_ref[rows, cols]
            rstd = lax.rsqrt(jnp.mean(ov * ov, axis=-1, keepdims=True) + NORM_EPS)
            xhat = ov * rstd
            sg = _sigmoid(gb)
            dy = dout * (gb * sg)
            dgb_ref[rows, cols] = dout * (xhat * ngv) * (sg * (1.0 + gb * (1.0 - sg)))
            dng_acc = dng_acc + jnp.sum(dy * xhat, axis=0, keepdims=True)
            dxhat = dy * ngv
            do = rstd * (dxhat - xhat * jnp.mean(dxhat * xhat, axis=-1, keepdims=True))
            dob = do.astype(BF16)

            eb = jnp.exp(b)
            b_last = b_s[pl.ds(C - 1, 1), :]
            ebl = jnp.exp(b_last - b)
            a = _hgrn_intra_fwd(q_s, k_s, b_s)
            da = jnp.where(causal, _dot_exact(do, vf, trans_b=True), 0.0)
            dv = _dot_tn(a.astype(BF16), dob) + _dot_nt((k * ebl).astype(BF16), dst.astype(BF16))
            dq_s[...] = _dot_exact(do, st) * eb
            dk_s[...] = _dot_exact(vf, dst) * ebl
            dstate_ref[j] = dst * jnp.exp(b_last) + _dot_exact(do, q * eb, trans_a=True)
            b_all, k_all = b, k
            for i in range(n_sub):
                r0 = i * B_SUB
                blk = pl.ds(r0, B_SUB)
                qi, bi = q_s[blk, :], b_s[blk, :]
                da_i = da[r0:r0 + B_SUB, :]
                dq_i = jnp.zeros((B_SUB, LANES), F32)
                if i > 0:
                    ref_b = b_s[pl.ds(r0 - 1, 1), :]
                    eq = jnp.exp(bi - ref_b)
                    ek = jnp.where(srow < r0, jnp.exp(jnp.minimum(ref_b - b_all, 0.0)), 0.0)
                    da_off = jnp.where(lane < r0, da_i, 0.0)
                    dq_i = _dot_exact(da_off, k_all * ek) * eq
                    dk_s[...] += _dot_exact(da_off, qi * eq, trans_a=True) * ek
                for sl in range(B_SUB):
                    s = r0 + sl
                    e = jnp.where(trow >= sl, jnp.exp(jnp.minimum(bi - b_s[pl.ds(s, 1), :], 0.0)), 0.0)
                    dac = jnp.sum(jnp.where(lane == s, da_i, 0.0), axis=-1, keepdims=True)
                    dq_i = dq_i + dac * k_s[pl.ds(s, 1), :] * e
                    dk_s[pl.ds(s, 1), :] += jnp.sum(dac * qi * e, axis=0, keepdims=True)
                dq_s[blk, :] += dq_i
            dq, dk = dq_s[...], dk_s[...]
            db = q * dq - k * dk
            dlog = _dot_exact(triu, db) + dlog_carry
            dlog_carry = dlog_carry + jnp.sum(db, axis=0, keepdims=True)
            dgate = dlog / gate - dk
            dqb_ref[rows, cols] = dq * (B_DIM ** -0.5) * (sq * (1.0 + qb * (1.0 - sq)))
            dfb_ref[rows, cols] = dgate * (1.0 - lbv) * sf * (1.0 - sf)
            dib_ref[rows, cols] = dv
            dlb_acc = dlb_acc + jnp.sum(dgate * (1.0 - sf), axis=0, keepdims=True)
            return dlog_carry, dlb_acc, dng_acc

        def step(ci, carry):
            c = nc - 1 - ci
            rows = pl.ds(pl.multiple_of(c * C, C), C)
            return tuple(one_head(j, c, rows, carry[j]) for j in range(hb))

        z = jnp.zeros((1, LANES), F32)
        final = lax.fori_loop(0, nc, step, ((z, z, z),) * hb)
        for j in range(hb):
            dlb_ref[j] = final[j][1]
            dng_ref[...] += final[j][2]

    def col(off):
        return pl.BlockSpec((S, hb * LANES), lambda g: (0, off // hb + g), pipeline_mode=pl.Buffered(1))

    hv = pl.BlockSpec((hb, 1, LANES), lambda g: (g, 0, 0))
    vec = pl.BlockSpec((1, LANES), lambda g: (0, 0))
    full = jax.ShapeDtypeStruct((S, n_heads * LANES), F32)
    return pl.pallas_call(
        body, name=name, grid=(n_heads // hb,),
        in_specs=[col(q_blk), col(f_blk), col(i_blk), col(g_blk), hv, vec, col(0),
                  pl.BlockSpec((hb, nc, LANES, LANES), lambda g: (g, 0, 0, 0), pipeline_mode=pl.Buffered(1)),
                  col(dout_blk)],
        out_specs=[pl.BlockSpec((S, hb * LANES), lambda g: (0, g))] * 4 + [hv, vec],
        out_shape=[full, full, full, full, jax.ShapeDtypeStruct((n_heads, 1, LANES), F32),
                   jax.ShapeDtypeStruct((1, LANES), F32)],
        scratch_shapes=[pltpu.VMEM((hb, LANES, LANES), F32)] + [pltpu.VMEM((hb, B_CHUNK, LANES), F32)] * 5,
        compiler_params=_cparams("arbitrary"),
    )(src, src, src, src, lb, norm_g.reshape(1, LANES), o, states, dout)


def _pad_heads_cols(w, n_heads):
    lead = w.shape[:-1]
    w = w.reshape(lead + (n_heads, C_DIM))
    w = jnp.pad(w, [(0, 0)] * len(lead) + [(0, 0), (0, LANES - C_DIM)])
    return w.reshape(lead + (n_heads * LANES,))


def _unpad_heads_cols(w, n_heads):
    lead = w.shape[:-1]
    return w.reshape(lead + (n_heads, LANES))[..., :C_DIM].reshape(lead + (n_heads * C_DIM,))


def _pad_heads_rows(w, n_heads):
    w = jnp.pad(w.reshape(n_heads, C_DIM, w.shape[1]), [(0, 0), (0, LANES - C_DIM), (0, 0)])
    return w.reshape(n_heads * LANES, w.shape[2])


def _unpad_heads_rows(w, n_heads):
    return w.reshape(n_heads, LANES, w.shape[1])[:, :C_DIM].reshape(n_heads * C_DIM, w.shape[1])


def _lower_bounds(lb_raw):
    lb_soft = jax.nn.softmax(lb_raw.astype(F32), axis=0)
    return jnp.cumsum(lb_soft, axis=0) - lb_soft[0:1]


def _relu2(u):
    r = jnp.maximum(u, 0.0)
    return r * r


def local_step(x, target, P, hooks):
    S, D = x.shape
    depth = P["norm_mix_g"].shape[0]
    HA = D // 2 // A_DIM
    HB = HA
    HQ = D // C_DIM
    HKV = HQ // C_GROUP
    even_in = 7 * HA * LANES
    tabs_a = rope_tables(S, A_DIM)
    tabs_c = rope_tables(S, C_DIM)
    half_a, half_c = tabs_a[3], tabs_c[3]
    lbounds = _lower_bounds(P["hgrn_lb_raw"])
    a_in = 3 * HA * LANES
    a_kw = dict(n_heads=HA, group=1, hb=A_HEADS_PER_STEP, q_blk=0, k_blk=HA, v_blk=2 * HA, blk_per_row=3 * HA,
                scale=A_DIM ** -0.5, half=half_a)
    b_kw = dict(S=S, n_heads=HB, hb=B_HEADS_PER_STEP, q_blk=0, f_blk=HB, i_blk=2 * HB, g_blk=3 * HB)
    c_kw = dict(L=S, dil=1, n_heads=HQ, group=C_GROUP, hb=C_GROUP, q_blk=0, k_blk=HQ, v_blk=HQ + HKV,
                blk_per_row=HQ + 2 * HKV, max_dist=C_WINDOW - 1, scale=C_DIM ** -0.5, half=half_c)

    def a_tabs(dil):
        return tuple(t.reshape(S // dil, dil * LANES) for t in tabs_a[:3])

    saved = []
    for layer in range(depth):
        Wl = dict(hooks["weights_in"](layer, x))
        sv = {"x0": x, "W": Wl}
        idx = layer // 2
        h = rmsnorm_fwd(x, P["norm_mix_g"][layer], dep=Wl.pop("_dep", None), name="norm_mix_fwd")
        sv["h"] = h
        if layer % 2 == 0:
            qkv = matmul(h, Wl["even_w_in"], b_cols=(0, a_in), out_dtypes=(BF16,), name="even_in_fwd_a")
            proj = matmul(h, Wl["even_w_in"], b_cols=(a_in, even_in), out_dtypes=(F32,), name="even_in_fwd_b")
            parts = []
            sv["qkv"] = {dil: qkv.reshape(S // dil, dil * a_in) for _, dil in A_BRANCHES}
            for window, dil in A_BRANCHES:
                L = S // dil
                res = banded_fwd(sv["qkv"][dil], a_tabs(dil), L=L, dil=dil, max_dist=window // dil, normalize=False,
                                 name=f"dilated_fwd_d{dil}", **a_kw)
                parts.append(tuple(r.reshape(S, HA * LANES) for r in res))
            oa, lse = merge_branches(parts, name="dilated_merge")
            hooks["mid"](layer, oa)
            lb_e = lbounds[idx].reshape(HB, 1, LANES)
            ob, o_raw, states = hgrn_fwd(proj, lb_e, P["hgrn_norm_g"][idx], name="hgrn_fwd", **b_kw)
            Wl.update(hooks["weights_rest"](layer, ob))
            mixed = jnp.concatenate([oa, ob], axis=1).astype(BF16)
            sv.update(proj=proj, oa=oa, lse=lse, o_raw=o_raw, states=states, mixed=mixed, lb=lb_e)
            x = matmul(mixed, Wl["even_w_out"], extras=(x,), epilogue=lambda acc, r: (acc + r,),
                       dep=Wl.pop("_dep", None), out_dtypes=(F32,), name="even_out_fwd")
        else:
            proj = matmul(h, Wl["odd_w_qkv"], extras=(P["odd_b_qkv"][idx].reshape(1, -1),),
                          epilogue=lambda acc, b: (acc + b,), out_dtypes=(F32,), name="odd_qkv_fwd")
            sink = jnp.broadcast_to(P["odd_sinks"][idx].reshape(HQ, 1, 1), (HQ, 1, LANES))
            o, lse = banded_fwd(proj, tabs_c[:3], sink=sink, normalize=True, name="swa_fwd", **c_kw)
            hooks["mid"](layer, o)
            Wl.update(hooks["weights_rest"](layer, o))
            sv.update(proj=proj, o=o, lse=lse, sink=sink)
            x = matmul(o, Wl["odd_w_o"], extras=(P["odd_b_o"][idx].reshape(1, D), x),
                       epilogue=lambda acc, b, r: (acc + b + r,), out_dtypes=(F32,), name="odd_out_fwd")
        sv["x1"] = x
        h2 = rmsnorm_fwd(x, P["norm_mlp_g"][layer], dep=hooks["after_mixer"](layer, x), name="norm_mlp_fwd")
        u = matmul(h2, Wl["mlp_w1"], out_dtypes=(BF16,), name="mlp_up_fwd")
        x = matmul(u, Wl["mlp_w2"], a_fn=_relu2, extras=(x,), epilogue=lambda acc, r: (acc + r,),
                   out_dtypes=(F32,), name="mlp_down_fwd")
        sv.update(h2=h2, u=u)
        saved.append(sv)

    loss, dx, dgf = final_norm_loss(x, P["final_norm_g"], target, name="final_norm_loss")

    G = {k: [None] * depth for k in ("norm_mix_g", "norm_mlp_g")}
    for k in ("hgrn_lb", "hgrn_norm_g"):
        G[k] = [None] * ((depth + 1) // 2)
    for k in ("odd_w_qkv", "odd_b_qkv", "odd_sinks", "odd_w_o", "odd_b_o"):
        G[k] = [None] * (depth // 2)
    G["final_norm_g"] = dgf.reshape(D)
    B = {}

    def wgrad(a, d, key, idx, name, a_fn=None):
        M, N = a.shape[1], d.shape[1]
        into = (grad_buffer_shape((M, N), SHARDING[key]), SHARDING[key])
        B[key, idx] = matmul(a, d, trans_a=True, a_fn=a_fn, out_dtypes=(BF16,), into=into, name=name)

    dep = None
    for layer in reversed(range(depth)):
        sv = saved[layer]
        Wl = sv["W"]
        idx = layer // 2
        du = matmul(dx, Wl["mlp_w2"], trans_b=True, extras=(sv["u"],), dep=dep,
                    epilogue=lambda acc, u: (acc * (2.0 * jnp.maximum(u.astype(F32), 0.0)),),
                    out_dtypes=(BF16,), name="mlp_down_bwd_x")
        wgrad(sv["u"], dx, "mlp_w2", layer, "mlp_down_bwd_w", a_fn=_relu2)
        wgrad(sv["h2"], du, "mlp_w1", layer, "mlp_up_bwd_w")
        dh2 = matmul(du, Wl["mlp_w1"], trans_b=True, out_dtypes=(F32,), name="mlp_up_bwd_x")
        dx, dg = rmsnorm_bwd(sv["x1"], P["norm_mlp_g"][layer], dh2, dx, name="norm_mlp_bwd")
        G["norm_mlp_g"][layer] = dg.reshape(D)
        dep = hooks["grads_ready"]([("mlp_w2", layer), ("mlp_w1", layer)], B, dx)
        if layer % 2 == 0:
            dmixed = matmul(dx, Wl["even_w_out"], trans_b=True, dep=dep, out_dtypes=(F32,), name="even_out_bwd_x")
            wgrad(sv["mixed"], dx, "even_w_out", idx, "even_out_bwd_w")
            dep = hooks["bwd_mid"](dmixed)
            acc = None
            for window, dil in A_BRANCHES:
                L = S // dil
                rs = lambda a: a.reshape(L, -1)
                acc = banded_bwd(sv["qkv"][dil], a_tabs(dil), rs(sv["oa"]), rs(sv["lse"]), rs(dmixed),
                                 L=L, dil=dil, max_dist=window // dil, do_bpr=2 * HA,
                                 acc=None if acc is None else tuple(rs(a) for a in acc),
                                 dep=dep if acc is None else None, name=f"dilated_bwd_d{dil}", **a_kw)
                acc = tuple(a.reshape(S, HA * LANES) for a in acc)
            dqb, dfb, dib, dgb, dlb, dng = hgrn_bwd(sv["proj"], sv["lb"], P["hgrn_norm_g"][idx], sv["o_raw"],
                                                    sv["states"], dmixed, dout_blk=HA, name="hgrn_bwd", **b_kw)
            G["hgrn_lb"][idx] = dlb.reshape(HB * LANES)
            G["hgrn_norm_g"][idx] = dng.reshape(LANES)
            dproj = jnp.concatenate(list(acc) + [dqb, dfb, dib, dgb], axis=1).astype(BF16)
            wgrad(sv["h"], dproj, "even_w_in", idx, "even_in_bwd_w")
            dh = matmul(dproj, Wl["even_w_in"], trans_b=True, out_dtypes=(F32,), name="even_in_bwd_x")
        else:
            do = matmul(dx, Wl["odd_w_o"], trans_b=True, dep=dep, out_dtypes=(F32,), name="odd_out_bwd_x")
            G["odd_b_o"][idx] = colsum(dx, name="odd_out_bwd_b").reshape(D)
            G["odd_w_o"][idx] = matmul(sv["o"], dx, trans_a=True, out_dtypes=(BF16,), name="odd_out_bwd_w")
            dq, dk, dv, dsink = banded_bwd(sv["proj"], tabs_c[:3], sv["o"], sv["lse"], do, sink=sv["sink"],
                                           do_bpr=HQ, dep=hooks["bwd_mid"](do), name="swa_bwd", **c_kw)
            G["odd_sinks"][idx] = dsink[:, 0, 0]
            dproj = jnp.concatenate([dq, dk, dv], axis=1)
            G["odd_b_qkv"][idx] = colsum(dproj, name="odd_qkv_bwd_b").reshape(-1)
            dproj = dproj.astype(BF16)
            G["odd_w_qkv"][idx] = matmul(sv["h"], dproj, trans_a=True, out_dtypes=(BF16,), name="odd_qkv_bwd_w")
            dh = matmul(dproj, Wl["odd_w_qkv"], trans_b=True, out_dtypes=(F32,), name="odd_qkv_bwd_x")
            HQ2 = HQ + 2 * HKV
            B["odd_w_qkv", idx] = full_to_buffer(_unpad_heads_cols(G["odd_w_qkv"][idx], HQ2), "cols")
            B["odd_w_o", idx] = full_to_buffer(_unpad_heads_rows(G["odd_w_o"][idx], HQ), "rows")
        dx, dg = rmsnorm_bwd(sv["x0"], P["norm_mix_g"][layer], dh, dx, name="norm_mix_bwd")
        G["norm_mix_g"][layer] = dg.reshape(D)
        mixer = ("even_w_out", "even_w_in") if layer % 2 == 0 else ("odd_w_o", "odd_w_qkv")
        dep = hooks["grads_ready"]([(k, idx) for k in mixer], B, dx)

    del G["odd_w_qkv"], G["odd_w_o"]
    grads = {k: jnp.stack(v) if isinstance(v, list) else v for k, v in G.items()}
    _, lb_vjp = jax.vjp(_lower_bounds, P["hgrn_lb_raw"])
    grads["hgrn_lb_raw"] = lb_vjp(grads.pop("hgrn_lb"))[0]
    grads["odd_b_qkv"] = _unpad_heads_cols(grads["odd_b_qkv"], HQ + 2 * HKV)
    return loss, dx, grads, B


def full_to_buffer(full, sharding):
    M, N = full.shape
    if sharding == "cols":
        return jnp.transpose(full.reshape(2, M // 2, 4, N // 4), (2, 0, 1, 3))
    return full.reshape(4, 2, M // 8, N)


MESH = pl.DeviceIdType.MESH
ANY = pl.BlockSpec(memory_space=pl.ANY)
SHARDING = {"even_w_in": "cols", "even_w_out": "rows", "odd_w_qkv": "cols", "odd_w_o": "rows",
            "mlp_w1": "cols", "mlp_w2": "rows"}
BIG = tuple(SHARDING)


def _mesh_pos():
    return lax.axis_index("x"), lax.axis_index("y"), lax.axis_index("c")


def _other_chips(x, y):
    return [(1 - x, y), (x, 1 - y), (1 - x, 1 - y)]


def cast_into_full(w, layer, sharding, s, *, name, tr=256):
    n, R, C = w.shape
    tr = _tile(R, tr)
    nr = R // tr

    def body(s_ref, w_ref, o_ref):
        o_ref[...] = w_ref[...].astype(BF16)

    if sharding == "cols":
        full, out_map = (R, 4 * C), (lambda i, s_ref: (i, s_ref[0]))
    else:
        full, out_map = (4 * R, C), (lambda i, s_ref: (s_ref[0] * nr + i, 0))
    grid_spec = pltpu.PrefetchScalarGridSpec(
        num_scalar_prefetch=1, grid=(nr,),
        in_specs=[pl.BlockSpec((None, tr, C), lambda i, s_ref: (layer, i, 0))],
        out_specs=pl.BlockSpec((tr, C), out_map))
    return pl.pallas_call(
        body, name=name, grid_spec=grid_spec, out_shape=jax.ShapeDtypeStruct(full, BF16),
        compiler_params=_cparams("parallel"),
    )(s.reshape(1).astype(jnp.int32), w)


HBM = pl.BlockSpec(memory_space=pltpu.HBM)
SEM = pl.BlockSpec(memory_space=pltpu.SEMAPHORE)
EFFECT = pltpu.SideEffectType.DATAFLOW_SIDE_EFFECTING


def _weight_window(ref, sharding, s, hf):
    M, N = ref.shape
    if sharding == "cols":
        rh, C = M // 2, N // 4
        return ref.at[pl.ds(hf * rh, rh), pl.ds(s * C, C)]
    rh = M // 8
    return ref.at[pl.ds((2 * s + hf) * rh, rh), :]


def _in_hbm(a):
    return pltpu.with_memory_space_constraint(a, pltpu.HBM)


def gather_start(groups, shardings, after, *, name):
    sizes = [len(g) for g in groups]
    flat = [m for g in groups for m in g]
    flat_sh = [sh for g in shardings for sh in g]
    T, ng = len(flat), len(groups)

    def body(*refs):
        sems = refs[T + 1:T + 1 + 6 * ng]
        thru = refs[T + 1 + 6 * ng:2 * T + 1 + 6 * ng]
        token = refs[2 * T + 1 + 6 * ng]
        x, y, c = _mesh_pos()
        pos = 0
        for g in range(ng):
            for t in range(sizes[g]):
                w = _weight_window(thru[pos], flat_sh[pos], 2 * x + y, c)
                for j, (cx, cy) in enumerate(_other_chips(x, y)):
                    pltpu.make_async_remote_copy(src_ref=w, dst_ref=w, send_sem=sems[6 * g + j],
                                                 recv_sem=sems[6 * g + 3 + j], device_id=(cx, cy, c),
                                                 device_id_type=MESH).start()
                pos += 1
        token[...] = jnp.zeros_like(token)

    outs = pl.pallas_call(
        body, name=name,
        in_specs=[HBM] * T + [ANY],
        out_specs=[SEM] * (6 * ng) + [HBM] * T + [pl.BlockSpec(memory_space=pltpu.VMEM)],
        out_shape=[pltpu.SemaphoreType.DMA(())] * (6 * ng) + [pltpu.HBM(m.shape, m.dtype) for m in flat]
        + [jax.ShapeDtypeStruct((8, LANES), F32)],
        input_output_aliases={t: 6 * ng + t for t in range(T)},
        compiler_params=pltpu.CompilerParams(has_side_effects=EFFECT),
    )(*[_in_hbm(m) for m in flat], after)
    res, pos = [], 6 * ng
    for g in range(ng):
        res.append((list(outs[6 * g:6 * g + 6]), list(outs[pos:pos + sizes[g]])))
        pos += sizes[g]
    return res, outs[-1]


def gather_forward(mats, shardings, sems, after, *, name):
    T = len(mats)

    def body(*refs):
        sems1 = refs[T:T + 6]
        sems2 = refs[T + 7:T + 13]
        thru = refs[T + 13:2 * T + 13]
        token = refs[2 * T + 13]
        x, y, c = _mesh_pos()
        chips = _other_chips(x, y)
        for t in range(T):
            own = _weight_window(thru[t], shardings[t], 2 * x + y, c)
            for j, (cx, cy) in enumerate(chips):
                landed = _weight_window(thru[t], shardings[t], 2 * cx + cy, c)
                first = pltpu.make_async_remote_copy(src_ref=own, dst_ref=landed, send_sem=sems1[j],
                                                     recv_sem=sems1[3 + j], device_id=(cx, cy, c),
                                                     device_id_type=MESH)
                first.wait_recv()
                first.wait_send()
        for t in range(T):
            for j, (cx, cy) in enumerate(chips):
                landed = _weight_window(thru[t], shardings[t], 2 * cx + cy, c)
                pltpu.make_async_remote_copy(src_ref=landed, dst_ref=landed, send_sem=sems2[j],
                                             recv_sem=sems2[3 + j], device_id=(x, y, 1 - c),
                                             device_id_type=MESH).start()
        token[...] = jnp.zeros_like(token)

    outs = pl.pallas_call(
        body, name=name,
        in_specs=[HBM] * T + [SEM] * 6 + [ANY],
        out_specs=[SEM] * 6 + [HBM] * T + [pl.BlockSpec(memory_space=pltpu.VMEM)],
        out_shape=[pltpu.SemaphoreType.DMA(())] * 6 + [pltpu.HBM(m.shape, m.dtype) for m in mats]
        + [jax.ShapeDtypeStruct((8, LANES), F32)],
        input_output_aliases={t: 6 + t for t in range(T)},
        compiler_params=pltpu.CompilerParams(has_side_effects=EFFECT),
    )(*mats, *sems, after)
    return list(outs[:6]), list(outs[6:6 + T]), outs[-1]


def gather_finish(mats, shardings, sems, after, *, name):
    T = len(mats)

    def body(*refs):
        sems2 = refs[T:T + 6]
        thru = refs[T + 7:2 * T + 7]
        x, y, c = _mesh_pos()
        for t in range(T):
            for j, (cx, cy) in enumerate(_other_chips(x, y)):
                sent = _weight_window(thru[t], shardings[t], 2 * cx + cy, c)
                other = _weight_window(thru[t], shardings[t], 2 * cx + cy, 1 - c)
                cp = pltpu.make_async_remote_copy(src_ref=sent, dst_ref=other, send_sem=sems2[j],
                                                  recv_sem=sems2[3 + j], device_id=(x, y, 1 - c),
                                                  device_id_type=MESH)
                cp.wait_recv()
                cp.wait_send()

    outs = pl.pallas_call(
        body, name=name,
        in_specs=[HBM] * T + [SEM] * 6 + [ANY],
        out_specs=[HBM] * T,
        out_shape=[pltpu.HBM(m.shape, m.dtype) for m in mats],
        input_output_aliases={t: t for t in range(T)},
        compiler_params=pltpu.CompilerParams(has_side_effects=EFFECT),
    )(*mats, *sems, after)
    return list(outs)


def exchange_halves(bufs, *, name):
    T = len(bufs)

    def body(*refs):
        in_refs, out_refs = refs[:T], refs[T:2 * T]
        send_sems, recv_sems = refs[2 * T:]
        x, y, c = _mesh_pos()
        cps = [pltpu.make_async_remote_copy(src_ref=in_refs[t].at[:, pl.ds(1 - c, 1)], dst_ref=out_refs[t],
                                            send_sem=send_sems.at[t], recv_sem=recv_sems.at[t],
                                            device_id=(x, y, 1 - c), device_id_type=MESH) for t in range(T)]
        for cp in cps:
            cp.start()
        for cp in cps:
            cp.wait()

    return pl.pallas_call(
        body, name=name, in_specs=[ANY] * T, out_specs=[ANY] * T,
        out_shape=[jax.ShapeDtypeStruct((4, 1) + b.shape[2:], b.dtype) for b in bufs],
        scratch_shapes=[pltpu.SemaphoreType.DMA((T,)), pltpu.SemaphoreType.DMA((T,))],
    )(*bufs)


def pair_sum(buf, got, c, *, name, tr=512):
    _, _, n, rh, C = buf.shape
    tr = _tile(rh, tr)

    def body(c_ref, a_ref, b_ref, o_ref):
        o_ref[...] = (a_ref[...].astype(F32) + b_ref[...].astype(F32)).astype(o_ref.dtype)

    grid_spec = pltpu.PrefetchScalarGridSpec(
        num_scalar_prefetch=1, grid=(4, n, rh // tr),
        in_specs=[pl.BlockSpec((None, None, None, tr, C), lambda s, l, i, c_ref: (s, c_ref[0], l, i, 0)),
                  pl.BlockSpec((None, None, None, tr, C), lambda s, l, i, c_ref: (s, 0, l, i, 0))],
        out_specs=pl.BlockSpec((None, None, tr, C), lambda s, l, i, c_ref: (s, l, i, 0)))
    return pl.pallas_call(
        body, name=name, grid_spec=grid_spec, out_shape=jax.ShapeDtypeStruct((4, n, rh, C), BF16),
        compiler_params=_cparams("parallel", "parallel", "parallel"),
    )(c.reshape(1).astype(jnp.int32), buf, got)


def scatter_to_owners(sums, *, name):
    T = len(sums)

    def body(*refs):
        in_refs, out_refs = refs[:T], refs[T:2 * T]
        send_sems, recv_sems = refs[2 * T:]
        x, y, c = _mesh_pos()
        chips = _other_chips(x, y)
        cps = [pltpu.make_async_remote_copy(src_ref=in_refs[t].at[pl.ds(2 * cx + cy, 1)],
                                            dst_ref=out_refs[t].at[pl.ds(j, 1)],
                                            send_sem=send_sems.at[t, j], recv_sem=recv_sems.at[t, j],
                                            device_id=(cx, cy, c), device_id_type=MESH)
               for t in range(T) for j, (cx, cy) in enumerate(chips)]
        for cp in cps:
            cp.start()
        for cp in cps:
            cp.wait()

    return pl.pallas_call(
        body, name=name, in_specs=[ANY] * T, out_specs=[ANY] * T,
        out_shape=[jax.ShapeDtypeStruct((3,) + s.shape[1:], s.dtype) for s in sums],
        scratch_shapes=[pltpu.SemaphoreType.DMA((T, 3)), pltpu.SemaphoreType.DMA((T, 3))],
    )(*sums)


def exchange_start(bufs, after, *, name):
    T = len(bufs)
    lands = [_in_hbm(lax.empty((4, 1) + b.shape[2:], b.dtype)) for b in bufs]

    def body(*refs):
        send_sem, recv_sem = refs[2 * T + 1], refs[2 * T + 2]
        src, dst = refs[2 * T + 3:3 * T + 3], refs[3 * T + 3:4 * T + 3]
        token = refs[4 * T + 3]
        x, y, c = _mesh_pos()
        for t in range(T):
            pltpu.make_async_remote_copy(src_ref=src[t].at[:, pl.ds(1 - c, 1)], dst_ref=dst[t], send_sem=send_sem,
                                         recv_sem=recv_sem, device_id=(x, y, 1 - c), device_id_type=MESH).start()
        token[...] = jnp.zeros_like(token)

    outs = pl.pallas_call(
        body, name=name,
        in_specs=[HBM] * (2 * T) + [ANY],
        out_specs=[SEM] * 2 + [HBM] * (2 * T) + [pl.BlockSpec(memory_space=pltpu.VMEM)],
        out_shape=[pltpu.SemaphoreType.DMA(())] * 2 + [pltpu.HBM(a.shape, a.dtype) for a in list(bufs) + lands]
        + [jax.ShapeDtypeStruct((8, LANES), F32)],
        input_output_aliases={t: 2 + t for t in range(2 * T)},
        compiler_params=pltpu.CompilerParams(has_side_effects=EFFECT),
    )(*[_in_hbm(b) for b in bufs], *lands, after)
    return list(outs[:2]), list(outs[2:2 + T]), list(outs[2 + T:2 + 2 * T]), outs[-1]


def exchange_wait(bufs, lands, sems, after, *, name):
    T = len(bufs)

    def body(*refs):
        send_sem, recv_sem = refs[2 * T], refs[2 * T + 1]
        src, dst = refs[2 * T + 3:3 * T + 3], refs[3 * T + 3:4 * T + 3]
        x, y, c = _mesh_pos()
        for t in range(T):
            cp = pltpu.make_async_remote_copy(src_ref=src[t].at[:, pl.ds(1 - c, 1)], dst_ref=dst[t],
                                              send_sem=send_sem, recv_sem=recv_sem, device_id=(x, y, 1 - c),
                                              device_id_type=MESH)
            cp.wait_recv()
            cp.wait_send()

    outs = pl.pallas_call(
        body, name=name,
        in_specs=[HBM] * (2 * T) + [SEM] * 2 + [ANY],
        out_specs=[HBM] * (2 * T),
        out_shape=[pltpu.HBM(a.shape, a.dtype) for a in list(bufs) + list(lands)],
        input_output_aliases={t: t for t in range(2 * T)},
        compiler_params=pltpu.CompilerParams(has_side_effects=EFFECT),
    )(*bufs, *lands, *sems, after)
    return list(outs[:T]), list(outs[T:])


def scatter_start(sums, *, name):
    T = len(sums)
    lands = [_in_hbm(lax.empty((3,) + s.shape[1:], s.dtype)) for s in sums]

    def body(*refs):
        sems = refs[2 * T:2 * T + 6]
        src, dst = refs[2 * T + 6:3 * T + 6], refs[3 * T + 6:4 * T + 6]
        token = refs[4 * T + 6]
        x, y, c = _mesh_pos()
        for t in range(T):
            for j, (cx, cy) in enumerate(_other_chips(x, y)):
                pltpu.make_async_remote_copy(src_ref=src[t].at[pl.ds(2 * cx + cy, 1)], dst_ref=dst[t].at[pl.ds(j, 1)],
                                             send_sem=sems[j], recv_sem=sems[3 + j], device_id=(cx, cy, c),
                                             device_id_type=MESH).start()
        token[...] = jnp.zeros_like(token)

    outs = pl.pallas_call(
        body, name=name,
        in_specs=[HBM] * (2 * T),
        out_specs=[SEM] * 6 + [HBM] * (2 * T) + [pl.BlockSpec(memory_space=pltpu.VMEM)],
        out_shape=[pltpu.SemaphoreType.DMA(())] * 6 + [pltpu.HBM(a.shape, a.dtype) for a in list(sums) + lands]
        + [jax.ShapeDtypeStruct((8, LANES), F32)],
        input_output_aliases={t: 6 + t for t in range(2 * T)},
        compiler_params=pltpu.CompilerParams(has_side_effects=EFFECT),
    )(*[_in_hbm(s) for s in sums], *lands)
    return list(outs[:6]), list(outs[6:6 + T]), list(outs[6 + T:6 + 2 * T]), outs[-1]


def scatter_wait(sums, lands, sems, after, *, name):
    T = len(sums)

    def body(*refs):
        sem_refs = refs[2 * T:2 * T + 6]
        src, dst = refs[2 * T + 7:3 * T + 7], refs[3 * T + 7:4 * T + 7]
        x, y, c = _mesh_pos()
        for t in range(T):
            for j, (cx, cy) in enumerate(_other_chips(x, y)):
                cp = pltpu.make_async_remote_copy(src_ref=src[t].at[pl.ds(2 * cx + cy, 1)],
                                                  dst_ref=dst[t].at[pl.ds(j, 1)], send_sem=sem_refs[j],
                                                  recv_sem=sem_refs[3 + j], device_id=(cx, cy, c),
                                                  device_id_type=MESH)
                cp.wait_recv()
                cp.wait_send()

    outs = pl.pallas_call(
        body, name=name,
        in_specs=[HBM] * (2 * T) + [SEM] * 6 + [ANY],
        out_specs=[HBM] * (2 * T),
        out_shape=[pltpu.HBM(a.shape, a.dtype) for a in list(sums) + list(lands)],
        input_output_aliases={t: t for t in range(2 * T)},
        compiler_params=pltpu.CompilerParams(has_side_effects=EFFECT),
    )(*sums, *lands, *sems, after)
    return list(outs[:T]), list(outs[T:])


def owner_sum(sums, got, s, c, grad, layer, *, name, tr=512):
    rh, C = sums.shape[2:]
    tr = _tile(rh, tr)
    nr = rh // tr

    def body(sc_ref, a_ref, b0_ref, b1_ref, b2_ref, g_ref, o_ref):
        o_ref[...] = ((a_ref[...].astype(F32) + b0_ref[...].astype(F32)) + b1_ref[...].astype(F32)) \
            + b2_ref[...].astype(F32)

    def got_spec(j):
        return pl.BlockSpec((None, None, tr, C), lambda i, sc_ref: (j, 0, i, 0))

    grid_spec = pltpu.PrefetchScalarGridSpec(
        num_scalar_prefetch=1, grid=(nr,),
        in_specs=[pl.BlockSpec((None, None, tr, C), lambda i, sc_ref: (sc_ref[0], 0, i, 0)),
                  got_spec(0), got_spec(1), got_spec(2), ANY],
        out_specs=pl.BlockSpec((None, tr, C), lambda i, sc_ref: (layer, sc_ref[1] * nr + i, 0)))
    return pl.pallas_call(
        body, name=name, grid_spec=grid_spec, out_shape=jax.ShapeDtypeStruct(grad.shape, F32),
        input_output_aliases={5: 0},
        compiler_params=_cparams("parallel"),
    )(jnp.stack([s, c]).astype(jnp.int32), sums, got, got, got, grad)


def join_halves(grads, *, name):
    T = len(grads)

    def body(*refs):
        out_refs = refs[T:2 * T]
        send_sems, recv_sems = refs[2 * T:]
        x, y, c = _mesh_pos()

        def win(t, hf):
            rh = grads[t].shape[1] // 2
            return out_refs[t].at[:, pl.ds(hf * rh, rh), :]

        def remote(t, w):
            return pltpu.make_async_remote_copy(src_ref=w, dst_ref=w, send_sem=send_sems.at[t],
                                                recv_sem=recv_sems.at[t], device_id=(x, y, 1 - c),
                                                device_id_type=MESH)

        cps = [remote(t, win(t, c)) for t in range(T)]
        for cp in cps:
            cp.start()
        for t in range(T):
            remote(t, win(t, 1 - c)).wait_recv()
        for cp in cps:
            cp.wait_send()

    return pl.pallas_call(
        body, name=name, in_specs=[ANY] * T, out_specs=[ANY] * T,
        out_shape=[jax.ShapeDtypeStruct(g.shape, g.dtype) for g in grads],
        input_output_aliases={t: t for t in range(T)},
        scratch_shapes=[pltpu.SemaphoreType.DMA((T,)), pltpu.SemaphoreType.DMA((T,))],
    )(*grads)


def allreduce_small(v, *, name):
    rows = v.shape[0]

    def body(v_ref, out_ref, buf, send_sems, recv_sems):
        x, y, c = _mesh_pos()
        me = 4 * x + 2 * y + c
        flips = [(dx, dy, dc) for dx in (0, 1) for dy in (0, 1) for dc in (0, 1)][1:]

        def peer(f):
            return tuple(1 - p if d else p for d, p in zip(f, (x, y, c)))

        cps = []
        for k, f in enumerate(flips):
            px, py, pc = peer(f)
            cps.append(pltpu.make_async_remote_copy(src_ref=v_ref, dst_ref=buf.at[me], send_sem=send_sems.at[k],
                                                    recv_sem=recv_sems.at[k], device_id=(px, py, pc),
                                                    device_id_type=MESH))
        for cp in cps:
            cp.start()
        buf[me] = v_ref[...]
        for k, f in enumerate(flips):
            px, py, pc = peer(f)
            slot = buf.at[4 * px + 2 * py + pc]
            pltpu.make_async_remote_copy(src_ref=slot, dst_ref=slot, send_sem=send_sems.at[k],
                                         recv_sem=recv_sems.at[k], device_id=(px, py, pc),
                                         device_id_type=MESH).wait_recv()
        for cp in cps:
            cp.wait_send()
        acc = buf[0]
        for i in range(1, 8):
            acc = acc + buf[i]
        out_ref[...] = acc

    vm = pl.BlockSpec(memory_space=pltpu.VMEM)
    return pl.pallas_call(
        body, name=name, in_specs=[vm], out_specs=vm,
        out_shape=jax.ShapeDtypeStruct(v.shape, F32),
        scratch_shapes=[pltpu.VMEM((8, rows, LANES), F32), pltpu.SemaphoreType.DMA((7,)),
                        pltpu.SemaphoreType.DMA((7,))],
    )(v)


def _adam_math(w, m, v, g):
    m = ADAM_B1 * m + (1.0 - ADAM_B1) * g
    v = ADAM_B2 * v + (1.0 - ADAM_B2) * (g * g)
    m_hat = m / (1.0 - ADAM_B1 ** ADAM_STEP)
    v_hat = v / (1.0 - ADAM_B2 ** ADAM_STEP)
    delta = -ADAM_LR * (m_hat / (jnp.sqrt(v_hat) + ADAM_EPS) + ADAM_WD * w)
    return delta, m, v


def adamw(w, m, v, g, *, name, tr=256):
    def body(w_ref, m_ref, v_ref, g_ref, d_ref, nm_ref, nv_ref):
        d, nm, nv = _adam_math(w_ref[...], m_ref[...], v_ref[...], g_ref[...])
        d_ref[...] = d
        nm_ref[...] = nm
        nv_ref[...] = nv

    shape = jax.ShapeDtypeStruct(w.shape, F32)
    if w.ndim == 2:
        return pl.pallas_call(body, name=name, out_shape=[shape] * 3)(w, m, v, g)
    n, R, C = w.shape
    tr = _tile(R, tr)
    spec = pl.BlockSpec((None, tr, C), lambda l, i: (l, i, 0))
    return pl.pallas_call(
        body, name=name, grid=(n, R // tr), in_specs=[spec] * 4, out_specs=[spec] * 3, out_shape=[shape] * 3,
        compiler_params=_cparams("parallel", "parallel"),
    )(w, m, v, g)


def _pack(parts):
    flat = jnp.concatenate([p.reshape(-1).astype(F32) for p in parts])
    size = -(-flat.shape[0] // (8 * LANES)) * (8 * LANES)
    return jnp.pad(flat, (0, size - flat.shape[0])).reshape(size // LANES, LANES)


def _unpack(block, shapes):
    flat = block.reshape(-1)
    out, pos = [], 0
    for shp in shapes:
        size = int(np.prod(shp))
        out.append(flat[pos:pos + size].reshape(shp))
        pos += size
    return out


SMALL = ("norm_mix_g", "norm_mlp_g", "final_norm_g", "hgrn_lb_raw", "hgrn_norm_g", "odd_sinks")
WEIGHTS = ("norm_mix_g", "norm_mlp_g", "final_norm_g", "even_w_in", "even_w_out", "hgrn_lb_raw", "hgrn_norm_g",
           "odd_w_qkv", "odd_b_qkv", "odd_sinks", "odd_w_o", "odd_b_o", "mlp_w1", "mlp_w2")


def kernel(x, norm_mix_g, norm_mlp_g, final_norm_g, even_w_in, even_w_out, hgrn_lb_raw, hgrn_norm_g, odd_w_qkv, odd_b_qkv, odd_sinks, odd_w_o, odd_b_o, mlp_w1, mlp_w2, loss_target, m_norm_mix_g, m_norm_mlp_g, m_final_norm_g, m_even_w_in, m_even_w_out, m_hgrn_lb_raw, m_hgrn_norm_g, m_odd_w_qkv, m_odd_b_qkv, m_odd_sinks, m_odd_w_o, m_odd_b_o, m_mlp_w1, m_mlp_w2, v_norm_mix_g, v_norm_mlp_g, v_final_norm_g, v_even_w_in, v_even_w_out, v_hgrn_lb_raw, v_hgrn_norm_g, v_odd_w_qkv, v_odd_b_qkv, v_odd_sinks, v_odd_w_o, v_odd_b_o, v_mlp_w1, v_mlp_w2):
    args = locals()
    W = {k: args[k] for k in WEIGHTS}
    M = {k: args["m_" + k] for k in WEIGHTS}
    V = {k: args["v_" + k] for k in WEIGHTS}
    _, S, D = x.shape
    depth = norm_mix_g.shape[0]
    HQ = D // C_DIM
    HKV = HQ // C_GROUP
    xi, yi, ci = _mesh_pos()
    shard = 2 * xi + yi

    n_odd, bq = odd_b_qkv.shape
    bo = odd_b_o.shape[1]
    keep = (ci == 0).astype(F32)
    pieces = [lax.dynamic_update_slice(jnp.zeros((n_odd, 4 * bq), F32), odd_b_qkv * keep, (0, shard * bq)),
              lax.dynamic_update_slice(jnp.zeros((n_odd, 4 * bo), F32), odd_b_o * keep, (0, shard * bo))]
    biases = allreduce_small(_pack(pieces), name="gather_biases")
    b_qkv_full, b_o_full = _unpack(biases, [(n_odd, 4 * bq), (n_odd, 4 * bo)])
    P = {k: W[k] for k in SMALL}
    P.update(odd_b_qkv=_pad_heads_cols(b_qkv_full, HQ + 2 * HKV), odd_b_o=b_o_full)

    def layer_keys(layer):
        mixer = ("even_w_in", "even_w_out") if layer % 2 == 0 else ("odd_w_qkv", "odd_w_o")
        return [(k, layer // 2) for k in mixer] + [("mlp_w1", layer), ("mlp_w2", layer)]

    keys = [layer_keys(0)[:1], layer_keys(0)[1:]] + [layer_keys(layer) for layer in range(1, depth)]
    shardings = [[SHARDING[k] for k, _ in ks] for ks in keys]
    started, forwarded = {}, {}

    def start_groups(gs, after):
        mats = [[cast_into_full(W[k], i, SHARDING[k], shard, name="cast_" + k) for k, i in keys[g]] for g in gs]
        res, token = gather_start(mats, [shardings[g] for g in gs], after, name="gather_start")
        started.update(zip(gs, res))
        return token

    dep0 = start_groups(list(range(len(keys))), biases)

    def forward_group(g, after):
        sems, mats = started[g]
        forwarded[g] = gather_forward(mats, shardings[g], sems, after, name="gather_forward")
        return forwarded[g][2]

    def finish_group(g, after):
        sems, mats, _ = forwarded[g]
        mats = gather_finish(mats, shardings[g], sems, after, name="gather_finish")
        Wl = {k: m for (k, _), m in zip(keys[g], mats)}
        if "odd_w_qkv" in Wl:
            Wl["odd_w_qkv"] = _pad_heads_cols(Wl["odd_w_qkv"], HQ + 2 * HKV)
            Wl["odd_w_o"] = _pad_heads_rows(Wl["odd_w_o"], HQ)
        return Wl

    def weights_in(layer, x_in):
        if layer == 0:
            forward_group(0, dep0)
        return finish_group(layer + 1 if layer else 0, x_in)

    def mid(layer, a):
        if layer == 0:
            forward_group(1, a)

    def weights_rest(layer, a):
        return finish_group(1, a) if layer == 0 else {}

    def after_mixer(layer, x1):
        return forward_group(layer + 2, x1) if layer + 1 < depth else None

    reduced = {k: lax.empty(W[k].shape, F32) for k in BIG}
    exchanging, scattering = [], []

    def settle(entry, after):
        kis, (sems, sums, lands, _) = entry
        sums, lands = scatter_wait(sums, lands, sems, after, name="scatter_wait")
        for (k, i), s, g in zip(kis, sums, lands):
            reduced[k] = owner_sum(s, g, shard, ci, reduced[k], i, name="owner_sum_" + k)

    def scatter_exchanged(after):
        for kis, (sems, bufs, lands, _) in exchanging:
            bufs, lands = exchange_wait(bufs, lands, sems, after, name="exchange_wait")
            sums = [pair_sum(b, g, ci, name="pair_sum_" + k) for (k, _), b, g in zip(kis, bufs, lands)]
            scattering.append((kis, scatter_start(sums, name="scatter_start")))
        exchanging.clear()

    def grads_ready(kis, B, after):
        scatter_exchanged(after)
        while len(scattering) > MAX_SCATTERS_IN_FLIGHT:
            settle(scattering.pop(0), after)
        bufs = [B[ki].reshape((4, 2, 1) + B[ki].shape[2:]) for ki in kis]
        order = scattering[-1][1][3] if scattering else after
        exchanging.append((kis, exchange_start(bufs, order, name="exchange_start")))
        return exchanging[-1][1][3]

    def bwd_mid(after):
        scatter_exchanged(after)
        return scattering[-1][1][3] if scattering else None

    hooks = dict(weights_in=weights_in, mid=mid, weights_rest=weights_rest, after_mixer=after_mixer,
                 grads_ready=grads_ready, bwd_mid=bwd_mid)
    loss, dx, G, B = local_step(x[0], loss_target[0], P, hooks)
    scatter_exchanged(dx)
    for entry in scattering:
        settle(entry, scattering[-1][1][3])
    grads = dict(zip(BIG, join_halves([reduced[k] for k in BIG], name="join_halves")))

    small_keys = SMALL + ("odd_b_qkv", "odd_b_o")
    small_shapes = [G[k].shape for k in small_keys]
    small = _unpack(allreduce_small(_pack([G[k] for k in small_keys]), name="reduce_small"), small_shapes)
    grads.update(zip(small_keys, small))
    grads["odd_b_qkv"] = lax.dynamic_slice(grads["odd_b_qkv"], (0, shard * bq), (n_odd, bq))
    grads["odd_b_o"] = lax.dynamic_slice(grads["odd_b_o"], (0, shard * bo), (n_odd, bo))

    deltas, new_m, new_v = {}, {}, {}
    for k in WEIGHTS:
        w2 = (lambda a: a.reshape(1, -1)) if W[k].ndim == 1 else (lambda a: a)
        d, nm, nv = adamw(w2(W[k]), w2(M[k]), w2(V[k]), w2(grads[k]), name="adamw_" + k)
        deltas[k], new_m[k], new_v[k] = (a.reshape(W[k].shape) for a in (d, nm, nv))

    loss = lax.psum(loss[0, 0], ("x", "y", "c"))
    return (loss, dx[None], *[grads[k] for k in WEIGHTS], *[deltas[k] for k in WEIGHTS],
            *[new_m[k] for k in WEIGHTS], *[new_v[k] for k in WEIGHTS])
```

```python
import functools

import jax
import jax.numpy as jnp
import numpy as np
from jax import lax
from jax.experimental import pallas as pl
from jax.experimental.pallas import tpu as pltpu

F32 = jnp.float32
BF16 = jnp.bfloat16

NORM_EPS = 1e-5
ROPE_THETA = 500000.0
ROPE_FRACTION = 4
LANES = 128
BAND = 128
A_DIM = 128
A_BRANCHES = ((128, 1), (512, 4), (2048, 16))
B_DIM = 128
B_CHUNK = 64
B_SUB = 16
A_HEADS_PER_STEP = 2
B_HEADS_PER_STEP = 2
MAX_SCATTERS_IN_FLIGHT = 3
C_DIM = 64
C_GROUP = 8
C_WINDOW = 128

ADAM_LR = 0.001
ADAM_B1 = 0.9
ADAM_B2 = 0.999
ADAM_EPS = 1e-08
ADAM_WD = 0.01
ADAM_STEP = 10

VMEM_LIMIT = 56 * 1024 * 1024


def _cparams(*sem):
    return pltpu.CompilerParams(dimension_semantics=tuple(sem), vmem_limit_bytes=VMEM_LIMIT)


def _tile(n, want):
    if n <= want:
        return n
    t = want - want % LANES
    while n % t:
        t -= LANES
    assert t > 0, (n, want)
    return t


def matmul(a, b, *, trans_a=False, trans_b=False, out_dtypes, a_fn=None, epilogue=None, extras=(), name,
           b_cols=None, into=None, dep=None, tm=1024, tn=1024, tk=2048):
    if trans_a:
        K, M = a.shape
    else:
        M, K = a.shape
    if trans_b:
        N, K2 = b.shape
    else:
        K2, N = b.shape
    assert K == K2, (a.shape, b.shape)
    col0 = 0
    if b_cols is not None:
        assert not trans_b
        col0, N = b_cols[0], b_cols[1] - b_cols[0]
    if into is not None:
        rh, cs = into[0][2:]
        tm, tn = _tile(rh, tm), _tile(cs, tn)
    tm, tn, tk = _tile(M, tm), _tile(N, tn), _tile(K, tk)
    nk = K // tk
    n_extra = len(extras)
    n_out = len(out_dtypes)

    def body(*refs):
        a_ref, b_ref = refs[0], refs[1]
        extra_refs = refs[2:2 + n_extra]
        out_refs = refs[-1 - n_out:-1]
        acc_ref = refs[-1]
        k = pl.program_id(2)
        at = a_ref[...]
        if a_fn is not None:
            at = a_fn(at.astype(F32))
        at = at.astype(BF16)
        bt = b_ref[...].astype(BF16)
        dims = (((0,) if trans_a else (1,), (1,) if trans_b else (0,)), ((), ()))
        prod = lax.dot_general(at, bt, dims, preferred_element_type=F32)

        def finish(acc):
            ex = [r[...] for r in extra_refs]
            outs = epilogue(acc, *ex) if epilogue is not None else (acc,)
            for o_ref, o in zip(out_refs, outs):
                o_ref[...] = o.astype(o_ref.dtype)

        if nk == 1:
            finish(prod)
            return

        @pl.when(k == 0)
        def _():
            acc_ref[...] = prod

        @pl.when((k > 0) & (k < nk - 1))
        def _():
            acc_ref[...] += prod

        @pl.when(k == nk - 1)
        def _():
            finish(acc_ref[...] + prod)

    a_spec = pl.BlockSpec((tk, tm), lambda i, j, k: (k, i)) if trans_a else pl.BlockSpec((tm, tk), lambda i, j, k: (i, k))
    assert col0 % tn == 0, (col0, tn)
    b_spec = (pl.BlockSpec((tn, tk), lambda i, j, k: (j, k)) if trans_b
              else pl.BlockSpec((tk, tn), lambda i, j, k: (k, col0 // tn + j)))
    e_specs = []
    for e in extras:
        if e.shape == (1, N):
            e_specs.append(pl.BlockSpec((1, tn), lambda i, j, k: (0, j)))
        else:
            assert e.shape == (M, N), (e.shape, M, N)
            e_specs.append(pl.BlockSpec((tm, tn), lambda i, j, k: (i, j)))
    args = [a, b, *extras]
    in_specs = [a_spec, b_spec] + e_specs
    if dep is not None:
        args.append(dep)
        in_specs.append(pl.BlockSpec(dep.shape, lambda i, j, k: (0, 0)))
    if into is None:
        out_specs = [pl.BlockSpec((tm, tn), lambda i, j, k: (i, j)) for _ in out_dtypes]
        out_shape = [jax.ShapeDtypeStruct((M, N), dt) for dt in out_dtypes]
    else:
        buf_shape, sharding = into
        assert n_out == 1
        index = grad_buffer_index(buf_shape, (M, N), sharding, tm, tn)
        out_specs = [pl.BlockSpec((None, None, tm, tn), lambda i, j, k: index(i, j))]
        out_shape = [jax.ShapeDtypeStruct(buf_shape, out_dtypes[0])]
    outs = pl.pallas_call(
        body, name=name,
        grid=(M // tm, N // tn, nk),
        in_specs=in_specs, out_specs=out_specs, out_shape=out_shape,
        scratch_shapes=[pltpu.VMEM((tm, tn), F32)],
        compiler_params=_cparams("parallel", "parallel", "arbitrary"),
    )(*args)
    return outs[0] if n_out == 1 else tuple(outs)


def grad_buffer_shape(mat_shape, sharding):
    M, N = mat_shape
    return (4, 2, M // 2, N // 4) if sharding == "cols" else (4, 2, M // 8, N)


def grad_buffer_index(buf_shape, mat_shape, sharding, tm, tn):
    _, _, rh, cs = buf_shape
    M, N = mat_shape
    ib, jb = rh // tm, cs // tn
    if sharding == "cols":
        assert (M, N) == (2 * rh, 4 * cs), (buf_shape, mat_shape)
        return lambda i, j: (j // jb, i // ib, i % ib, j % jb)
    assert sharding == "rows" and (M, N) == (8 * rh, cs), (buf_shape, mat_shape)
    return lambda i, j: (i // (2 * ib), (i // ib) % 2, i % ib, j)


def rmsnorm_fwd(x, g, *, name, dep=None, tr=256):
    S, D = x.shape
    tr = _tile(S, tr)

    def body(x_ref, g_ref, *rest):
        h_ref = rest[-1]
        xv = x_ref[...]
        rstd = lax.rsqrt(jnp.mean(xv * xv, axis=-1, keepdims=True) + NORM_EPS)
        h_ref[...] = (xv * rstd * g_ref[...]).astype(h_ref.dtype)

    args = [x, g.reshape(1, D)]
    in_specs = [pl.BlockSpec((tr, D), lambda i: (i, 0)), pl.BlockSpec((1, D), lambda i: (0, 0))]
    if dep is not None:
        args.append(dep)
        in_specs.append(pl.BlockSpec(dep.shape, lambda i: (0, 0)))
    return pl.pallas_call(
        body, name=name, grid=(S // tr,),
        in_specs=in_specs,
        out_specs=pl.BlockSpec((tr, D), lambda i: (i, 0)),
        out_shape=jax.ShapeDtypeStruct((S, D), BF16),
        compiler_params=_cparams("parallel"),
    )(*args)


def _rms_bwd_rows(xv, gv, dh):
    rstd = lax.rsqrt(jnp.mean(xv * xv, axis=-1, keepdims=True) + NORM_EPS)
    xhat = xv * rstd
    dxhat = dh * gv
    dx = rstd * (dxhat - xhat * jnp.mean(dxhat * xhat, axis=-1, keepdims=True))
    return dx, dh * xhat


def rmsnorm_bwd(x, g, dh, dres, *, name, tr=256):
    S, D = x.shape
    tr = _tile(S, tr)

    def body(x_ref, g_ref, dh_ref, dres_ref, dx_ref, dg_ref):
        @pl.when(pl.program_id(0) == 0)
        def _():
            dg_ref[...] = jnp.zeros_like(dg_ref)

        dx, dgr = _rms_bwd_rows(x_ref[...], g_ref[...], dh_ref[...].astype(F32))
        dx_ref[...] = dres_ref[...] + dx
        dg_ref[...] += jnp.sum(dgr, axis=0, keepdims=True)

    row = pl.BlockSpec((tr, D), lambda i: (i, 0))
    vec = pl.BlockSpec((1, D), lambda i: (0, 0))
    return pl.pallas_call(
        body, name=name, grid=(S // tr,),
        in_specs=[row, vec, row, row],
        out_specs=[row, vec],
        out_shape=[jax.ShapeDtypeStruct((S, D), F32), jax.ShapeDtypeStruct((1, D), F32)],
        compiler_params=_cparams("arbitrary"),
    )(x, g.reshape(1, D), dh, dres)


def final_norm_loss(x, g, target, *, name, tr=256):
    S, D = x.shape
    tr = _tile(S, tr)

    def body(x_ref, g_ref, t_ref, loss_ref, dx_ref, dg_ref):
        @pl.when(pl.program_id(0) == 0)
        def _():
            dg_ref[...] = jnp.zeros_like(dg_ref)
            loss_ref[...] = jnp.zeros_like(loss_ref)

        xv, gv = x_ref[...], g_ref[...]
        rstd = lax.rsqrt(jnp.mean(xv * xv, axis=-1, keepdims=True) + NORM_EPS)
        err = xv * rstd * gv - t_ref[...]
        part = 0.5 * jnp.sum(jnp.mean(err * err, axis=-1, keepdims=True), axis=0, keepdims=True)
        loss_ref[...] += jnp.broadcast_to(part, loss_ref.shape)
        dx, dgr = _rms_bwd_rows(xv, gv, err * (1.0 / D))
        dx_ref[...] = dx
        dg_ref[...] += jnp.sum(dgr, axis=0, keepdims=True)

    row = pl.BlockSpec((tr, D), lambda i: (i, 0))
    vec = pl.BlockSpec((1, D), lambda i: (0, 0))
    return pl.pallas_call(
        body, name=name, grid=(S // tr,),
        in_specs=[row, vec, row],
        out_specs=[pl.BlockSpec((8, LANES), lambda i: (0, 0)), row, vec],
        out_shape=[jax.ShapeDtypeStruct((8, LANES), F32), jax.ShapeDtypeStruct((S, D), F32),
                   jax.ShapeDtypeStruct((1, D), F32)],
        compiler_params=_cparams("arbitrary"),
    )(x, g.reshape(1, D), target)


def colsum(a, *, name, tr=256):
    S, N = a.shape
    tr = _tile(S, tr)

    def body(a_ref, o_ref):
        @pl.when(pl.program_id(0) == 0)
        def _():
            o_ref[...] = jnp.zeros_like(o_ref)

        o_ref[...] += jnp.sum(a_ref[...].astype(F32), axis=0, keepdims=True)

    return pl.pallas_call(
        body, name=name, grid=(S // tr,),
        in_specs=[pl.BlockSpec((tr, N), lambda i: (i, 0))],
        out_specs=pl.BlockSpec((1, N), lambda i: (0, 0)),
        out_shape=jax.ShapeDtypeStruct((1, N), F32),
        compiler_params=_cparams("arbitrary"),
    )(a)


def rope_tables(seq, head_dim):
    rot = head_dim // ROPE_FRACTION
    half = rot // 2
    inv_freq = 1.0 / (ROPE_THETA ** (jnp.arange(0, rot, 2, dtype=F32) / rot))
    ang = jnp.arange(seq, dtype=F32)[:, None] * inv_freq[None, :]
    cos, sin = jnp.cos(ang), jnp.sin(ang)
    ones = jnp.ones((seq, LANES - 2 * half), F32)
    zeros = jnp.zeros((seq, LANES - 2 * half), F32)
    zh = jnp.zeros((seq, half), F32)
    c = jnp.concatenate([cos, cos, ones], axis=1)
    sa = jnp.concatenate([-sin, zh, zeros], axis=1)
    sb = jnp.concatenate([zh, sin, zeros], axis=1)
    return c, sa, sb, half


def _rope(x, c, sa, sb, half):
    return x * c + pltpu.roll(x, LANES - half, 1) * sa + pltpu.roll(x, half, 1) * sb


def _rope_bwd(d, c, sa, sb, half):
    return d * c + pltpu.roll(d * sa, half, 1) + pltpu.roll(d * sb, LANES - half, 1)


def _band_masks(n, max_dist):
    qi = lax.broadcasted_iota(jnp.int32, (BAND, BAND), 0)
    kj = lax.broadcasted_iota(jnp.int32, (BAND, BAND), 1)
    cur = kj <= qi
    prev = ((kj >= qi) if max_dist == BAND else (kj > qi)) & (n > 0)
    return prev, cur


def _dot_nt(a, b):
    return lax.dot_general(a, b, (((1,), (1,)), ((), ())), preferred_element_type=F32)


def _dot_tn(a, b):
    return lax.dot_general(a, b, (((0,), (0,)), ((), ())), preferred_element_type=F32)


def _dot(a, b):
    return jnp.dot(a, b, preferred_element_type=F32)


def banded_fwd(src, tabs, *, L, dil, n_heads, group, hb, q_blk, k_blk, v_blk, blk_per_row, max_dist, scale, half,
               sink=None, normalize, name):
    c_t, sa_t, sb_t = tabs
    nb = L // BAND
    kvb = hb // group
    n_hg = n_heads // hb
    W = dil * n_heads * LANES
    use_sink = sink is not None
    assert n_heads % hb == 0 and hb % group == 0
    assert (q_blk % hb, k_blk % kvb, v_blk % kvb) == (0, 0, 0) and (dil == 1 or blk_per_row % hb == 0)

    def body(*refs):
        q_ref, k_ref, v_ref, c_ref, sa_ref, sb_ref = refs[:6]
        pos = 6
        if use_sink:
            sink_ref = refs[pos]
            pos += 1
        o_ref, m_ref = refs[pos], refs[pos + 1]
        l_ref = None if normalize else refs[pos + 2]
        kr_ref, vb_ref = refs[-2], refs[-1]
        n = pl.program_id(2)

        @pl.when(n == 0)
        def _():
            def prep(b, carry):
                rows = pl.ds(pl.multiple_of(b * BAND, BAND), BAND)
                cv, sav, sbv = c_ref[rows, :], sa_ref[rows, :], sb_ref[rows, :]
                for j in range(kvb):
                    cols = slice(j * LANES, (j + 1) * LANES)
                    kr_ref[rows, cols] = _rope(k_ref[rows, cols].astype(F32), cv, sav, sbv, half).astype(BF16)
                    vb_ref[rows, cols] = v_ref[rows, cols].astype(BF16)
                return carry

            lax.fori_loop(0, nb, prep, 0)

        rows = pl.ds(pl.multiple_of(n * BAND, BAND), BAND)
        prow = pl.ds(pl.multiple_of(jnp.maximum(n - 1, 0) * BAND, BAND), BAND)
        cv, sav, sbv = c_ref[rows, :], sa_ref[rows, :], sb_ref[rows, :]
        mp, mc = _band_masks(n, max_dist)
        for i in range(hb):
            qc = slice(i * LANES, (i + 1) * LANES)
            kc = slice((i // group) * LANES, (i // group + 1) * LANES)
            q = _rope(q_ref[:, qc].astype(F32), cv, sav, sbv, half).astype(BF16)
            sp = jnp.where(mp, _dot_nt(q, kr_ref[prow, kc]) * scale, -jnp.inf)
            sc = jnp.where(mc, _dot_nt(q, kr_ref[rows, kc]) * scale, -jnp.inf)
            m = jnp.maximum(jnp.max(sp, axis=-1, keepdims=True), jnp.max(sc, axis=-1, keepdims=True))
            if use_sink:
                sk = sink_ref[i, :, 0:1]
                m = jnp.maximum(m, sk)
            pp = jnp.exp(sp - m)
            pc = jnp.exp(sc - m)
            l = jnp.sum(pp, axis=-1, keepdims=True) + jnp.sum(pc, axis=-1, keepdims=True)
            if use_sink:
                l = l + jnp.exp(sk - m)
            num = _dot(pp.astype(BF16), vb_ref[prow, kc]) + _dot(pc.astype(BF16), vb_ref[rows, kc])
            if normalize:
                o_ref[:, qc] = num / l
                m_ref[:, qc] = jnp.broadcast_to(m + jnp.log(l), (BAND, LANES))
            else:
                o_ref[:, qc] = num
                m_ref[:, qc] = jnp.broadcast_to(m, (BAND, LANES))
                l_ref[:, qc] = jnp.broadcast_to(l, (BAND, LANES))

    qspec = pl.BlockSpec((BAND, hb * LANES), lambda r, g, n: (n, (r * blk_per_row + q_blk) // hb + g))

    def kv(off):
        return pl.BlockSpec((L, kvb * LANES), lambda r, g, n: (0, (r * blk_per_row + off) // kvb + g))

    tab = pl.BlockSpec((L, LANES), lambda r, g, n: (0, r))
    out = pl.BlockSpec((BAND, hb * LANES), lambda r, g, n: (n, r * n_hg + g))
    in_specs = [qspec, kv(k_blk), kv(v_blk), tab, tab, tab]
    args = [src, src, src, c_t, sa_t, sb_t]
    if use_sink:
        in_specs.append(pl.BlockSpec((hb, 1, LANES), lambda r, g, n: (g, 0, 0)))
        args.append(sink)
    n_out = 2 if normalize else 3
    return pl.pallas_call(
        body, name=name, grid=(dil, n_hg, nb),
        in_specs=in_specs,
        out_specs=[out] * n_out,
        out_shape=[jax.ShapeDtypeStruct((L, W), F32)] * n_out,
        scratch_shapes=[pltpu.VMEM((L, kvb * LANES), BF16), pltpu.VMEM((L, kvb * LANES), BF16)],
        compiler_params=_cparams("parallel", "parallel", "arbitrary"),
    )(*args)


def banded_bwd(src, tabs, o, lse, do, *, L, dil, n_heads, group, hb, q_blk, k_blk, v_blk, blk_per_row, max_dist,
               scale, half, do_bpr, sink=None, acc=None, dep=None, name):
    c_t, sa_t, sb_t = tabs
    nb = L // BAND
    kvb = hb // group
    n_hg = n_heads // hb
    n_kv = n_heads // group
    use_sink = sink is not None
    use_acc = acc is not None
    assert n_heads % hb == 0 and hb % group == 0 and do_bpr % hb == 0
    assert (q_blk % hb, k_blk % kvb, v_blk % kvb) == (0, 0, 0) and (dil == 1 or blk_per_row % hb == 0)

    def body(*refs):
        q_ref, k_ref, v_ref, c_ref, sa_ref, sb_ref, o_ref, lse_ref, do_ref = refs[:9]
        pos = 9
        if use_sink:
            sink_ref = refs[pos]
            pos += 1
        if use_acc:
            aq_ref, ak_ref, av_ref = refs[pos:pos + 3]
            pos += 3
        if dep is not None:
            pos += 1
        dq_ref, dk_ref, dv_ref = refs[pos:pos + 3]
        pos += 3
        if use_sink:
            dsink_ref = refs[pos]
        kr_ref, vb_ref, dka_ref, dva_ref = refs[-4:]
        n = pl.program_id(2)

        @pl.when(n == 0)
        def _():
            def prep(b, carry):
                rows = pl.ds(pl.multiple_of(b * BAND, BAND), BAND)
                cv, sav, sbv = c_ref[rows, :], sa_ref[rows, :], sb_ref[rows, :]
                for j in range(kvb):
                    cols = slice(j * LANES, (j + 1) * LANES)
                    kr_ref[rows, cols] = _rope(k_ref[rows, cols].astype(F32), cv, sav, sbv, half).astype(BF16)
                    vb_ref[rows, cols] = v_ref[rows, cols].astype(BF16)
                return carry

            lax.fori_loop(0, nb, prep, 0)
            dka_ref[...] = jnp.zeros_like(dka_ref)
            dva_ref[...] = jnp.zeros_like(dva_ref)
            if use_sink:
                dsink_ref[...] = jnp.zeros_like(dsink_ref)

        rows = pl.ds(pl.multiple_of(n * BAND, BAND), BAND)
        prow = pl.ds(pl.multiple_of(jnp.maximum(n - 1, 0) * BAND, BAND), BAND)
        cv, sav, sbv = c_ref[rows, :], sa_ref[rows, :], sb_ref[rows, :]
        mp, mc = _band_masks(n, max_dist)
        for i in range(hb):
            qc = slice(i * LANES, (i + 1) * LANES)
            kc = slice((i // group) * LANES, (i // group + 1) * LANES)
            q = _rope(q_ref[:, qc].astype(F32), cv, sav, sbv, half).astype(BF16)
            dov = do_ref[:, qc]
            lsev = lse_ref[:, i * LANES:i * LANES + 1]
            delta = jnp.sum(dov * o_ref[:, qc], axis=-1, keepdims=True)
            dob = dov.astype(BF16)
            kp, kcur, vp, vcur = kr_ref[prow, kc], kr_ref[rows, kc], vb_ref[prow, kc], vb_ref[rows, kc]
            pp = jnp.exp(jnp.where(mp, _dot_nt(q, kp) * scale, -jnp.inf) - lsev)
            pc = jnp.exp(jnp.where(mc, _dot_nt(q, kcur) * scale, -jnp.inf) - lsev)
            dsp = (pp * (_dot_nt(dob, vp) - delta) * scale).astype(BF16)
            dsc = (pc * (_dot_nt(dob, vcur) - delta) * scale).astype(BF16)
            dq = _rope_bwd(_dot(dsp, kp) + _dot(dsc, kcur), cv, sav, sbv, half)
            if use_acc:
                dq = dq + aq_ref[:, qc]
            dq_ref[:, qc] = dq
            dka_ref[prow, kc] += _dot_tn(dsp, q)
            dka_ref[rows, kc] += _dot_tn(dsc, q)
            dva_ref[prow, kc] += _dot_tn(pp.astype(BF16), dob)
            dva_ref[rows, kc] += _dot_tn(pc.astype(BF16), dob)
            if use_sink:
                dsk = jnp.sum(jnp.exp(sink_ref[i, :, 0:1] - lsev) * delta, axis=0, keepdims=True)
                dsink_ref[i] -= jnp.broadcast_to(dsk, (1, LANES))

        @pl.when(n == nb - 1)
        def _():
            def fin(b, carry):
                brow = pl.ds(pl.multiple_of(b * BAND, BAND), BAND)
                bc, bsa, bsb = c_ref[brow, :], sa_ref[brow, :], sb_ref[brow, :]
                for j in range(kvb):
                    cols = slice(j * LANES, (j + 1) * LANES)
                    dk = _rope_bwd(dka_ref[brow, cols], bc, bsa, bsb, half)
                    dv = dva_ref[brow, cols]
                    if use_acc:
                        dk = dk + ak_ref[brow, cols]
                        dv = dv + av_ref[brow, cols]
                    dk_ref[brow, cols] = dk
                    dv_ref[brow, cols] = dv
                return carry

            lax.fori_loop(0, nb, fin, 0)

    qspec = pl.BlockSpec((BAND, hb * LANES), lambda r, g, n: (n, (r * blk_per_row + q_blk) // hb + g))

    def kv(off):
        return pl.BlockSpec((L, kvb * LANES), lambda r, g, n: (0, (r * blk_per_row + off) // kvb + g))

    tab = pl.BlockSpec((L, LANES), lambda r, g, n: (0, r))
    qo = pl.BlockSpec((BAND, hb * LANES), lambda r, g, n: (n, r * n_hg + g))
    kvo = pl.BlockSpec((L, kvb * LANES), lambda r, g, n: (0, r * (n_kv // kvb) + g))
    dospec = pl.BlockSpec((BAND, hb * LANES), lambda r, g, n: (n, r * (do_bpr // hb) + g))
    in_specs = [qspec, kv(k_blk), kv(v_blk), tab, tab, tab, qo, qo, dospec]
    args = [src, src, src, c_t, sa_t, sb_t, o, lse, do]
    sink_spec = pl.BlockSpec((hb, 1, LANES), lambda r, g, n: (g, 0, 0))
    if use_sink:
        in_specs.append(sink_spec)
        args.append(sink)
    if use_acc:
        in_specs += [qo, kvo, kvo]
        args += list(acc)
    if dep is not None:
        in_specs.append(pl.BlockSpec(dep.shape, lambda r, g, n: (0, 0)))
        args.append(dep)
    out_specs = [qo, kvo, kvo]
    out_shape = [jax.ShapeDtypeStruct((L, dil * n_heads * LANES), F32),
                 jax.ShapeDtypeStruct((L, dil * n_kv * LANES), F32),
                 jax.ShapeDtypeStruct((L, dil * n_kv * LANES), F32)]
    if use_sink:
        out_specs.append(sink_spec)
        out_shape.append(jax.ShapeDtypeStruct((n_heads, 1, LANES), F32))
    return pl.pallas_call(
        body, name=name, grid=(dil, n_hg, nb),
        in_specs=in_specs, out_specs=out_specs, out_shape=out_shape,
        scratch_shapes=[pltpu.VMEM((L, kvb * LANES), BF16), pltpu.VMEM((L, kvb * LANES), BF16),
                        pltpu.VMEM((L, kvb * LANES), F32), pltpu.VMEM((L, kvb * LANES), F32)],
        compiler_params=_cparams("parallel", "parallel", "arbitrary"),
    )(*args)


def merge_branches(parts, *, name, tr=256):
    S, W = parts[0][0].shape
    tr = _tile(S, tr)
    nbr = len(parts)

    def body(*refs):
        ins, (o_ref, lse_ref) = refs[:3 * nbr], refs[3 * nbr:]
        nums = [ins[3 * i][...] for i in range(nbr)]
        ms = [ins[3 * i + 1][...] for i in range(nbr)]
        ls = [ins[3 * i + 2][...] for i in range(nbr)]
        mx = functools.reduce(jnp.maximum, ms)
        ws = [jnp.exp(m - mx) for m in ms]
        num = sum(w * n for w, n in zip(ws, nums))
        den = sum(w * l for w, l in zip(ws, ls))
        o_ref[...] = num / den
        lse_ref[...] = mx + jnp.log(den)

    row = pl.BlockSpec((tr, W), lambda i: (i, 0))
    flat = [a for p in parts for a in p]
    return pl.pallas_call(
        body, name=name, grid=(S // tr,),
        in_specs=[row] * len(flat), out_specs=[row, row],
        out_shape=[jax.ShapeDtypeStruct((S, W), F32)] * 2,
        compiler_params=_cparams("parallel"),
    )(*flat)


def _sigmoid(x):
    return 1.0 / (1.0 + jnp.exp(-x))


def _tri(n, lower):
    r = lax.broadcasted_iota(jnp.int32, (n, n), 0)
    c = lax.broadcasted_iota(jnp.int32, (n, n), 1)
    return ((c <= r) if lower else (c >= r)).astype(F32)


def _dot_exact(a, b, trans_a=False, trans_b=False):
    dims = (((0,) if trans_a else (1,), (1,) if trans_b else (0,)), ((), ()))
    return lax.dot_general(a, b, dims, preferred_element_type=F32, precision=lax.Precision.HIGHEST)


def _hgrn_gates(qb, fb, lb):
    sq = _sigmoid(qb)
    q = qb * sq * (B_DIM ** -0.5)
    sf = _sigmoid(fb)
    gate = lb + (1.0 - lb) * sf
    return q, 1.0 - gate, gate, sq, sf


def _hgrn_intra_fwd(q_ref, k_ref, b_ref):
    C, n_sub = B_CHUNK, B_CHUNK // B_SUB
    b_all, k_all = b_ref[...], k_ref[...]
    srow = lax.broadcasted_iota(jnp.int32, (C, LANES), 0)
    lane = lax.broadcasted_iota(jnp.int32, (B_SUB, C), 1)
    trow = lax.broadcasted_iota(jnp.int32, (B_SUB, LANES), 0)
    blocks = []
    for i in range(n_sub):
        r0 = i * B_SUB
        qi, bi = q_ref[pl.ds(r0, B_SUB), :], b_ref[pl.ds(r0, B_SUB), :]
        if i == 0:
            a_i = jnp.zeros((B_SUB, C), F32)
        else:
            ref_b = b_ref[pl.ds(r0 - 1, 1), :]
            qt = (qi * jnp.exp(bi - ref_b)).astype(BF16)
            kt = jnp.where(srow < r0, k_all * jnp.exp(jnp.minimum(ref_b - b_all, 0.0)), 0.0).astype(BF16)
            a_i = _dot_nt(qt, kt)
        for sl in range(B_SUB):
            s = r0 + sl
            e = jnp.where(trow >= sl, jnp.exp(jnp.minimum(bi - b_ref[pl.ds(s, 1), :], 0.0)), 0.0)
            colv = jnp.sum(qi * k_ref[pl.ds(s, 1), :] * e, axis=-1, keepdims=True)
            a_i = a_i + jnp.where(lane == s, colv, 0.0)
        blocks.append(a_i)
    return jnp.concatenate(blocks, axis=0)


def hgrn_fwd(src, lb, norm_g, *, S, n_heads, hb, q_blk, f_blk, i_blk, g_blk, name):
    nc = S // B_CHUNK
    assert n_heads % hb == 0 and (q_blk % hb, f_blk % hb, i_blk % hb, g_blk % hb) == (0, 0, 0, 0)

    def body(qb_ref, fb_ref, ib_ref, gb_ref, lb_ref, ng_ref, out_ref, o_ref, st_ref, state_ref, q_s, k_s, b_s):
        state_ref[...] = jnp.zeros_like(state_ref)
        tril = _tri(B_CHUNK, True)

        def step(c, carry):
            rows = pl.ds(pl.multiple_of(c * B_CHUNK, B_CHUNK), B_CHUNK)
            for j in range(hb):
                cols = slice(j * LANES, (j + 1) * LANES)
                q, k, gate, _, _ = _hgrn_gates(qb_ref[rows, cols], fb_ref[rows, cols], lb_ref[j])
                b = _dot_exact(tril, jnp.log(gate))
                vf = ib_ref[rows, cols]
                v = vf.astype(BF16)
                q_s[j], k_s[j], b_s[j] = q, k, b
                st = state_ref[j]
                st_ref[j, c] = st
                a = _hgrn_intra_fwd(q_s.at[j], k_s.at[j], b_s.at[j])
                o = _dot(a.astype(BF16), v) + _dot_nt((q * jnp.exp(b)).astype(BF16), st.astype(BF16))
                b_last = b_s[j, pl.ds(B_CHUNK - 1, 1), :]
                state_ref[j] = st * jnp.exp(b_last) + _dot_exact(vf, k * jnp.exp(b_last - b), trans_a=True)
                o_ref[rows, cols] = o
                rstd = lax.rsqrt(jnp.mean(o * o, axis=-1, keepdims=True) + NORM_EPS)
                gb = gb_ref[rows, cols]
                out_ref[rows, cols] = o * rstd * ng_ref[...] * (gb * _sigmoid(gb))
            return carry

        lax.fori_loop(0, nc, step, 0)

    def col(off):
        return pl.BlockSpec((S, hb * LANES), lambda g: (0, off // hb + g))

    hv = pl.BlockSpec((hb, 1, LANES), lambda g: (g, 0, 0))
    return pl.pallas_call(
        body, name=name, grid=(n_heads // hb,),
        in_specs=[col(q_blk), col(f_blk), col(i_blk), col(g_blk), hv, pl.BlockSpec((1, LANES), lambda g: (0, 0))],
        out_specs=[col(0), col(0), pl.BlockSpec((hb, nc, LANES, LANES), lambda g: (g, 0, 0, 0))],
        out_shape=[jax.ShapeDtypeStruct((S, n_heads * LANES), F32), jax.ShapeDtypeStruct((S, n_heads * LANES), F32),
                   jax.ShapeDtypeStruct((n_heads, nc, LANES, LANES), F32)],
        scratch_shapes=[pltpu.VMEM((hb, LANES, LANES), F32)] + [pltpu.VMEM((hb, B_CHUNK, LANES), F32)] * 3,
        compiler_params=_cparams("parallel"),
    )(src, src, src, src, lb, norm_g.reshape(1, LANES))


def hgrn_bwd(src, lb, norm_g, o, states, dout, *, S, n_heads, hb, q_blk, f_blk, i_blk, g_blk, dout_blk, name):
    nc = S // B_CHUNK
    C, n_sub = B_CHUNK, B_CHUNK // B_SUB
    assert n_heads % hb == 0 and (q_blk % hb, f_blk % hb, i_blk % hb, g_blk % hb, dout_blk % hb) == (0, 0, 0, 0, 0)

    def body(qb_ref, fb_ref, ib_ref, gb_ref, lb_ref, ng_ref, o_ref, st_ref, dout_ref,
             dqb_ref, dfb_ref, dib_ref, dgb_ref, dlb_ref, dng_ref, dstate_ref, q_sh, k_sh, b_sh, dq_sh, dk_sh):
        @pl.when(pl.program_id(0) == 0)
        def _():
            dng_ref[...] = jnp.zeros_like(dng_ref)

        dstate_ref[...] = jnp.zeros_like(dstate_ref)
        tril = _tri(C, True)
        triu = _tri(C, False)
        ngv = ng_ref[...]
        srow = lax.broadcasted_iota(jnp.int32, (C, LANES), 0)
        lane = lax.broadcasted_iota(jnp.int32, (B_SUB, C), 1)
        trow = lax.broadcasted_iota(jnp.int32, (B_SUB, LANES), 0)
        causal = lax.broadcasted_iota(jnp.int32, (C, C), 1) <= lax.broadcasted_iota(jnp.int32, (C, C), 0)

        def one_head(j, c, rows, carry):
            dlog_carry, dlb_acc, dng_acc = carry
            cols = slice(j * LANES, (j + 1) * LANES)
            q_s, k_s, b_s, dq_s, dk_s = q_sh.at[j], k_sh.at[j], b_sh.at[j], dq_sh.at[j], dk_sh.at[j]
            lbv = lb_ref[j]
            qb, fb, gb = qb_ref[rows, cols], fb_ref[rows, cols], gb_ref[rows, cols]
            q, k, gate, sq, sf = _hgrn_gates(qb, fb, lbv)
            b = _dot_exact(tril, jnp.log(gate))
            vf = ib_ref[rows, cols]
            v = vf.astype(BF16)
            q_s[...], k_s[...], b_s[...] = q, k, b
            st = st_ref[j, c]
            dst = dstate_ref[j]

            ov = o_ref[rows, cols]
            dout = dout_ref[rows, cols]
            rstd = lax.rsqrt(jnp.mean(ov * ov, axis=-1, keepdims=True) + NORM_EPS)
            xhat = ov * rstd
            sg = _sigmoid(gb)
            dy = dout * (gb * sg)
            dgb_ref[rows, cols] = dout * (xhat * ngv) * (sg * (1.0 + gb * (1.0 - sg)))
            dng_acc = dng_acc + jnp.sum(dy * xhat, axis=0, keepdims=True)
            dxhat = dy * ngv
            do = rstd * (dxhat - xhat * jnp.mean(dxhat * xhat, axis=-1, keepdims=True))
            dob = do.astype(BF16)

            eb = jnp.exp(b)
            b_last = b_s[pl.ds(C - 1, 1), :]
            ebl = jnp.exp(b_last - b)
            a = _hgrn_intra_fwd(q_s, k_s, b_s)
            da = jnp.where(causal, _dot_exact(do, vf, trans_b=True), 0.0)
            dv = _dot_tn(a.astype(BF16), dob) + _dot_nt((k * ebl).astype(BF16), dst.astype(BF16))
            dq_s[...] = _dot_exact(do, st) * eb
            dk_s[...] = _dot_exact(vf, dst) * ebl
            dstate_ref[j] = dst * jnp.exp(b_last) + _dot_exact(do, q * eb, trans_a=True)
            b_all, k_all = b, k
            for i in range(n_sub):
                r0 = i * B_SUB
                blk = pl.ds(r0, B_SUB)
                qi, bi = q_s[blk, :], b_s[blk, :]
                da_i = da[r0:r0 + B_SUB, :]
                dq_i = jnp.zeros((B_SUB, LANES), F32)
                if i > 0:
                    ref_b = b_s[pl.ds(r0 - 1, 1), :]
                    eq = jnp.exp(bi - ref_b)
                    ek = jnp.where(srow < r0, jnp.exp(jnp.minimum(ref_b - b_all, 0.0)), 0.0)
                    da_off = jnp.where(lane < r0, da_i, 0.0)
                    dq_i = _dot_exact(da_off, k_all * ek) * eq
                    dk_s[...] += _dot_exact(da_off, qi * eq, trans_a=True) * ek
                for sl in range(B_SUB):
                    s = r0 + sl
                    e = jnp.where(trow >= sl, jnp.exp(jnp.minimum(bi - b_s[pl.ds(s, 1), :], 0.0)), 0.0)
                    dac = jnp.sum(jnp.where(lane == s, da_i, 0.0), axis=-1, keepdims=True)
                    dq_i = dq_i + dac * k_s[pl.ds(s, 1), :] * e
                    dk_s[pl.ds(s, 1), :] += jnp.sum(dac * qi * e, axis=0, keepdims=True)
                dq_s[blk, :] += dq_i
            dq, dk = dq_s[...], dk_s[...]
            db = q * dq - k * dk
            dlog = _dot_exact(triu, db) + dlog_carry
            dlog_carry = dlog_carry + jnp.sum(db, axis=0, keepdims=True)
            dgate = dlog / gate - dk
            dqb_ref[rows, cols] = dq * (B_DIM ** -0.5) * (sq * (1.0 + qb * (1.0 - sq)))
            dfb_ref[rows, cols] = dgate * (1.0 - lbv) * sf * (1.0 - sf)
            dib_ref[rows, cols] = dv
            dlb_acc = dlb_acc + jnp.sum(dgate * (1.0 - sf), axis=0, keepdims=True)
            return dlog_carry, dlb_acc, dng_acc

        def step(ci, carry):
            c = nc - 1 - ci
            rows = pl.ds(pl.multiple_of(c * C, C), C)
            return tuple(one_head(j, c, rows, carry[j]) for j in range(hb))

        z = jnp.zeros((1, LANES), F32)
        final = lax.fori_loop(0, nc, step, ((z, z, z),) * hb)
        for j in range(hb):
            dlb_ref[j] = final[j][1]
            dng_ref[...] += final[j][2]

    def col(off):
        return pl.BlockSpec((S, hb * LANES), lambda g: (0, off // hb + g), pipeline_mode=pl.Buffered(1))

    hv = pl.BlockSpec((hb, 1, LANES), lambda g: (g, 0, 0))
    vec = pl.BlockSpec((1, LANES), lambda g: (0, 0))
    full = jax.ShapeDtypeStruct((S, n_heads * LANES), F32)
    return pl.pallas_call(
        body, name=name, grid=(n_heads // hb,),
        in_specs=[col(q_blk), col(f_blk), col(i_blk), col(g_blk), hv, vec, col(0),
                  pl.BlockSpec((hb, nc, LANES, LANES), lambda g: (g, 0, 0, 0), pipeline_mode=pl.Buffered(1)),
                  col(dout_blk)],
        out_specs=[pl.BlockSpec((S, hb * LANES), lambda g: (0, g))] * 4 + [hv, vec],
        out_shape=[full, full, full, full, jax.ShapeDtypeStruct((n_heads, 1, LANES), F32),
                   jax.ShapeDtypeStruct((1, LANES), F32)],
        scratch_shapes=[pltpu.VMEM((hb, LANES, LANES), F32)] + [pltpu.VMEM((hb, B_CHUNK, LANES), F32)] * 5,
        compiler_params=_cparams("arbitrary"),
    )(src, src, src, src, lb, norm_g.reshape(1, LANES), o, states, dout)


def _pad_heads_cols(w, n_heads):
    lead = w.shape[:-1]
    w = w.reshape(lead + (n_heads, C_DIM))
    w = jnp.pad(w, [(0, 0)] * len(lead) + [(0, 0), (0, LANES - C_DIM)])
    return w.reshape(lead + (n_heads * LANES,))


def _unpad_heads_cols(w, n_heads):
    lead = w.shape[:-1]
    return w.reshape(lead + (n_heads, LANES))[..., :C_DIM].reshape(lead + (n_heads * C_DIM,))


def _pad_heads_rows(w, n_heads):
    w = jnp.pad(w.reshape(n_heads, C_DIM, w.shape[1]), [(0, 0), (0, LANES - C_DIM), (0, 0)])
    return w.reshape(n_heads * LANES, w.shape[2])


def _unpad_heads_rows(w, n_heads):
    return w.reshape(n_heads, LANES, w.shape[1])[:, :C_DIM].reshape(n_heads * C_DIM, w.shape[1])


def _lower_bounds(lb_raw):
    lb_soft = jax.nn.softmax(lb_raw.astype(F32), axis=0)
    return jnp.cumsum(lb_soft, axis=0) - lb_soft[0:1]


def _relu2(u):
    r = jnp.maximum(u, 0.0)
    return r * r


def local_step(x, target, P, hooks):
    S, D = x.shape
    depth = P["norm_mix_g"].shape[0]
    HA = D // 2 // A_DIM
    HB = HA
    HQ = D // C_DIM
    HKV = HQ // C_GROUP
    even_in = 7 * HA * LANES
    tabs_a = rope_tables(S, A_DIM)
    tabs_c = rope_tables(S, C_DIM)
    half_a, half_c = tabs_a[3], tabs_c[3]
    lbounds = _lower_bounds(P["hgrn_lb_raw"])
    a_in = 3 * HA * LANES
    a_kw = dict(n_heads=HA, group=1, hb=A_HEADS_PER_STEP, q_blk=0, k_blk=HA, v_blk=2 * HA, blk_per_row=3 * HA,
                scale=A_DIM ** -0.5, half=half_a)
    b_kw = dict(S=S, n_heads=HB, hb=B_HEADS_PER_STEP, q_blk=0, f_blk=HB, i_blk=2 * HB, g_blk=3 * HB)
    c_kw = dict(L=S, dil=1, n_heads=HQ, group=C_GROUP, hb=C_GROUP, q_blk=0, k_blk=HQ, v_blk=HQ + HKV,
                blk_per_row=HQ + 2 * HKV, max_dist=C_WINDOW - 1, scale=C_DIM ** -0.5, half=half_c)

    def a_tabs(dil):
        return tuple(t.reshape(S // dil, dil * LANES) for t in tabs_a[:3])

    saved = []
    for layer in range(depth):
        Wl = dict(hooks["weights_in"](layer, x))
        sv = {"x0": x, "W": Wl}
        idx = layer // 2
        h = rmsnorm_fwd(x, P["norm_mix_g"][layer], dep=Wl.pop("_dep", None), name="norm_mix_fwd")
        sv["h"] = h
        if layer % 2 == 0:
            qkv = matmul(h, Wl["even_w_in"], b_cols=(0, a_in), out_dtypes=(BF16,), name="even_in_fwd_a")
            proj = matmul(h, Wl["even_w_in"], b_cols=(a_in, even_in), out_dtypes=(F32,), name="even_in_fwd_b")
            parts = []
            sv["qkv"] = {dil: qkv.reshape(S // dil, dil * a_in) for _, dil in A_BRANCHES}
            for window, dil in A_BRANCHES:
                L = S // dil
                res = banded_fwd(sv["qkv"][dil], a_tabs(dil), L=L, dil=dil, max_dist=window // dil, normalize=False,
                                 name=f"dilated_fwd_d{dil}", **a_kw)
                parts.append(tuple(r.reshape(S, HA * LANES) for r in res))
            oa, lse = merge_branches(parts, name="dilated_merge")
            hooks["mid"](layer, oa)
            lb_e = lbounds[idx].reshape(HB, 1, LANES)
            ob, o_raw, states = hgrn_fwd(proj, lb_e, P["hgrn_norm_g"][idx], name="hgrn_fwd", **b_kw)
            Wl.update(hooks["weights_rest"](layer, ob))
            mixed = jnp.concatenate([oa, ob], axis=1).astype(BF16)
            sv.update(proj=proj, oa=oa, lse=lse, o_raw=o_raw, states=states, mixed=mixed, lb=lb_e)
            x = matmul(mixed, Wl["even_w_out"], extras=(x,), epilogue=lambda acc, r: (acc + r,),
                       dep=Wl.pop("_dep", None), out_dtypes=(F32,), name="even_out_fwd")
        else:
            proj = matmul(h, Wl["odd_w_qkv"], extras=(P["odd_b_qkv"][idx].reshape(1, -1),),
                          epilogue=lambda acc, b: (acc + b,), out_dtypes=(F32,), name="odd_qkv_fwd")
            sink = jnp.broadcast_to(P["odd_sinks"][idx].reshape(HQ, 1, 1), (HQ, 1, LANES))
            o, lse = banded_fwd(proj, tabs_c[:3], sink=sink, normalize=True, name="swa_fwd", **c_kw)
            hooks["mid"](layer, o)
            Wl.update(hooks["weights_rest"](layer, o))
            sv.update(proj=proj, o=o, lse=lse, sink=sink)
            x = matmul(o, Wl["odd_w_o"], extras=(P["odd_b_o"][idx].reshape(1, D), x),
                       epilogue=lambda acc, b, r: (acc + b + r,), out_dtypes=(F32,), name="odd_out_fwd")
        sv["x1"] = x
        h2 = rmsnorm_fwd(x, P["norm_mlp_g"][layer], dep=hooks["after_mixer"](layer, x), name="norm_mlp_fwd")
        u = matmul(h2, Wl["mlp_w1"], out_dtypes=(BF16,), name="mlp_up_fwd")
        x = matmul(u, Wl["mlp_w2"], a_fn=_relu2, extras=(x,), epilogue=lambda acc, r: (acc + r,),
                   out_dtypes=(F32,), name="mlp_down_fwd")
        sv.update(h2=h2, u=u)
        saved.append(sv)

    loss, dx, dgf = final_norm_loss(x, P["final_norm_g"], target, name="final_norm_loss")

    G = {k: [None] * depth for k in ("norm_mix_g", "norm_mlp_g")}
    for k in ("hgrn_lb", "hgrn_norm_g"):
        G[k] = [None] * ((depth + 1) // 2)
    for k in ("odd_w_qkv", "odd_b_qkv", "odd_sinks", "odd_w_o", "odd_b_o"):
        G[k] = [None] * (depth // 2)
    G["final_norm_g"] = dgf.reshape(D)
    B = {}

    def wgrad(a, d, key, idx, name, a_fn=None):
        M, N = a.shape[1], d.shape[1]
        into = (grad_buffer_shape((M, N), SHARDING[key]), SHARDING[key])
        B[key, idx] = matmul(a, d, trans_a=True, a_fn=a_fn, out_dtypes=(BF16,), into=into, name=name)

    dep = None
    for layer in reversed(range(depth)):
        sv = saved[layer]
        Wl = sv["W"]
        idx = layer // 2
        du = matmul(dx, Wl["mlp_w2"], trans_b=True, extras=(sv["u"],), dep=dep,
                    epilogue=lambda acc, u: (acc * (2.0 * jnp.maximum(u.astype(F32), 0.0)),),
                    out_dtypes=(BF16,), name="mlp_down_bwd_x")
        wgrad(sv["u"], dx, "mlp_w2", layer, "mlp_down_bwd_w", a_fn=_relu2)
        wgrad(sv["h2"], du, "mlp_w1", layer, "mlp_up_bwd_w")
        dh2 = matmul(du, Wl["mlp_w1"], trans_b=True, out_dtypes=(F32,), name="mlp_up_bwd_x")
        dx, dg = rmsnorm_bwd(sv["x1"], P["norm_mlp_g"][layer], dh2, dx, name="norm_mlp_bwd")
        G["norm_mlp_g"][layer] = dg.reshape(D)
        dep = hooks["grads_ready"]([("mlp_w2", layer), ("mlp_w1", layer)], B, dx)
        if layer % 2 == 0:
            dmixed = matmul(dx, Wl["even_w_out"], trans_b=True, dep=dep, out_dtypes=(F32,), name="even_out_bwd_x")
            wgrad(sv["mixed"], dx, "even_w_out", idx, "even_out_bwd_w")
            dep = hooks["bwd_mid"](dmixed)
            acc = None
            for window, dil in A_BRANCHES:
                L = S // dil
                rs = lambda a: a.reshape(L, -1)
                acc = banded_bwd(sv["qkv"][dil], a_tabs(dil), rs(sv["oa"]), rs(sv["lse"]), rs(dmixed),
                                 L=L, dil=dil, max_dist=window // dil, do_bpr=2 * HA,
                                 acc=None if acc is None else tuple(rs(a) for a in acc),
                                 dep=dep if acc is None else None, name=f"dilated_bwd_d{dil}", **a_kw)
                acc = tuple(a.reshape(S, HA * LANES) for a in acc)
            dqb, dfb, dib, dgb, dlb, dng = hgrn_bwd(sv["proj"], sv["lb"], P["hgrn_norm_g"][idx], sv["o_raw"],
                                                    sv["states"], dmixed, dout_blk=HA, name="hgrn_bwd", **b_kw)
            G["hgrn_lb"][idx] = dlb.reshape(HB * LANES)
            G["hgrn_norm_g"][idx] = dng.reshape(LANES)
            dproj = jnp.concatenate(list(acc) + [dqb, dfb, dib, dgb], axis=1).astype(BF16)
            wgrad(sv["h"], dproj, "even_w_in", idx, "even_in_bwd_w")
            dh = matmul(dproj, Wl["even_w_in"], trans_b=True, out_dtypes=(F32,), name="even_in_bwd_x")
        else:
            do = matmul(dx, Wl["odd_w_o"], trans_b=True, dep=dep, out_dtypes=(F32,), name="odd_out_bwd_x")
            G["odd_b_o"][idx] = colsum(dx, name="odd_out_bwd_b").reshape(D)
            G["odd_w_o"][idx] = matmul(sv["o"], dx, trans_a=True, out_dtypes=(BF16,), name="odd_out_bwd_w")
            dq, dk, dv, dsink = banded_bwd(sv["proj"], tabs_c[:3], sv["o"], sv["lse"], do, sink=sv["sink"],
                                           do_bpr=HQ, dep=hooks["bwd_mid"](do), name="swa_bwd", **c_kw)
            G["odd_sinks"][idx] = dsink[:, 0, 0]
            dproj = jnp.concatenate([dq, dk, dv], axis=1)
            G["odd_b_qkv"][idx] = colsum(dproj, name="odd_qkv_bwd_b").reshape(-1)
            dproj = dproj.astype(BF16)
            G["odd_w_qkv"][idx] = matmul(sv["h"], dproj, trans_a=True, out_dtypes=(BF16,), name="odd_qkv_bwd_w")
            dh = matmul(dproj, Wl["odd_w_qkv"], trans_b=True, out_dtypes=(F32,), name="odd_qkv_bwd_x")
            HQ2 = HQ + 2 * HKV
            B["odd_w_qkv", idx] = full_to_buffer(_unpad_heads_cols(G["odd_w_qkv"][idx], HQ2), "cols")
            B["odd_w_o", idx] = full_to_buffer(_unpad_heads_rows(G["odd_w_o"][idx], HQ), "rows")
        dx, dg = rmsnorm_bwd(sv["x0"], P["norm_mix_g"][layer], dh, dx, name="norm_mix_bwd")
        G["norm_mix_g"][layer] = dg.reshape(D)
        mixer = ("even_w_out", "even_w_in") if layer % 2 == 0 else ("odd_w_o", "odd_w_qkv")
        dep = hooks["grads_ready"]([(k, idx) for k in mixer], B, dx)

    del G["odd_w_qkv"], G["odd_w_o"]
    grads = {k: jnp.stack(v) if isinstance(v, list) else v for k, v in G.items()}
    _, lb_vjp = jax.vjp(_lower_bounds, P["hgrn_lb_raw"])
    grads["hgrn_lb_raw"] = lb_vjp(grads.pop("hgrn_lb"))[0]
    grads["odd_b_qkv"] = _unpad_heads_cols(grads["odd_b_qkv"], HQ + 2 * HKV)
    return loss, dx, grads, B


def full_to_buffer(full, sharding):
    M, N = full.shape
    if sharding == "cols":
        return jnp.transpose(full.reshape(2, M // 2, 4, N // 4), (2, 0, 1, 3))
    return full.reshape(4, 2, M // 8, N)


MESH = pl.DeviceIdType.MESH
ANY = pl.BlockSpec(memory_space=pl.ANY)
SHARDING = {"even_w_in": "cols", "even_w_out": "rows", "odd_w_qkv": "cols", "odd_w_o": "rows",
            "mlp_w1": "cols", "mlp_w2": "rows"}
BIG = tuple(SHARDING)


def _mesh_pos():
    return lax.axis_index("x"), lax.axis_index("y"), lax.axis_index("c")


def _other_chips(x, y):
    return [(1 - x, y), (x, 1 - y), (1 - x, 1 - y)]


def cast_into_full(w, layer, sharding, s, *, name, tr=256):
    n, R, C = w.shape
    tr = _tile(R, tr)
    nr = R // tr

    def body(s_ref, w_ref, o_ref):
        o_ref[...] = w_ref[...].astype(BF16)

    if sharding == "cols":
        full, out_map = (R, 4 * C), (lambda i, s_ref: (i, s_ref[0]))
    else:
        full, out_map = (4 * R, C), (lambda i, s_ref: (s_ref[0] * nr + i, 0))
    grid_spec = pltpu.PrefetchScalarGridSpec(
        num_scalar_prefetch=1, grid=(nr,),
        in_specs=[pl.BlockSpec((None, tr, C), lambda i, s_ref: (layer, i, 0))],
        out_specs=pl.BlockSpec((tr, C), out_map))
    return pl.pallas_call(
        body, name=name, grid_spec=grid_spec, out_shape=jax.ShapeDtypeStruct(full, BF16),
        compiler_params=_cparams("parallel"),
    )(s.reshape(1).astype(jnp.int32), w)


HBM = pl.BlockSpec(memory_space=pltpu.HBM)
SEM = pl.BlockSpec(memory_space=pltpu.SEMAPHORE)
EFFECT = pltpu.SideEffectType.DATAFLOW_SIDE_EFFECTING


def _weight_window(ref, sharding, s, hf):
    M, N = ref.shape
    if sharding == "cols":
        rh, C = M // 2, N // 4
        return ref.at[pl.ds(hf * rh, rh), pl.ds(s * C, C)]
    rh = M // 8
    return ref.at[pl.ds((2 * s + hf) * rh, rh), :]


def _in_hbm(a):
    return pltpu.with_memory_space_constraint(a, pltpu.HBM)


def gather_start(groups, shardings, after, *, name):
    sizes = [len(g) for g in groups]
    flat = [m for g in groups for m in g]
    flat_sh = [sh for g in shardings for sh in g]
    T, ng = len(flat), len(groups)

    def body(*refs):
        sems = refs[T + 1:T + 1 + 6 * ng]
        thru = refs[T + 1 + 6 * ng:2 * T + 1 + 6 * ng]
        token = refs[2 * T + 1 + 6 * ng]
        x, y, c = _mesh_pos()
        pos = 0
        for g in range(ng):
            for t in range(sizes[g]):
                w = _weight_window(thru[pos], flat_sh[pos], 2 * x + y, c)
                for j, (cx, cy) in enumerate(_other_chips(x, y)):
                    pltpu.make_async_remote_copy(src_ref=w, dst_ref=w, send_sem=sems[6 * g + j],
                                                 recv_sem=sems[6 * g + 3 + j], device_id=(cx, cy, c),
                                                 device_id_type=MESH).start()
                pos += 1
        token[...] = jnp.zeros_like(token)

    outs = pl.pallas_call(
        body, name=name,
        in_specs=[HBM] * T + [ANY],
        out_specs=[SEM] * (6 * ng) + [HBM] * T + [pl.BlockSpec(memory_space=pltpu.VMEM)],
        out_shape=[pltpu.SemaphoreType.DMA(())] * (6 * ng) + [pltpu.HBM(m.shape, m.dtype) for m in flat]
        + [jax.ShapeDtypeStruct((8, LANES), F32)],
        input_output_aliases={t: 6 * ng + t for t in range(T)},
        compiler_params=pltpu.CompilerParams(has_side_effects=EFFECT),
    )(*[_in_hbm(m) for m in flat], after)
    res, pos = [], 6 * ng
    for g in range(ng):
        res.append((list(outs[6 * g:6 * g + 6]), list(outs[pos:pos + sizes[g]])))
        pos += sizes[g]
    return res, outs[-1]


def gather_forward(mats, shardings, sems, after, *, name):
    T = len(mats)

    def body(*refs):
        sems1 = refs[T:T + 6]
        sems2 = refs[T + 7:T + 13]
        thru = refs[T + 13:2 * T + 13]
        token = refs[2 * T + 13]
        x, y, c = _mesh_pos()
        chips = _other_chips(x, y)
        for t in range(T):
            own = _weight_window(thru[t], shardings[t], 2 * x + y, c)
            for j, (cx, cy) in enumerate(chips):
                landed = _weight_window(thru[t], shardings[t], 2 * cx + cy, c)
                first = pltpu.make_async_remote_copy(src_ref=own, dst_ref=landed, send_sem=sems1[j],
                                                     recv_sem=sems1[3 + j], device_id=(cx, cy, c),
                                                     device_id_type=MESH)
                first.wait_recv()
                first.wait_send()
        for t in range(T):
            for j, (cx, cy) in enumerate(chips):
                landed = _weight_window(thru[t], shardings[t], 2 * cx + cy, c)
                pltpu.make_async_remote_copy(src_ref=landed, dst_ref=landed, send_sem=sems2[j],
                                             recv_sem=sems2[3 + j], device_id=(x, y, 1 - c),
                                             device_id_type=MESH).start()
        token[...] = jnp.zeros_like(token)

    outs = pl.pallas_call(
        body, name=name,
        in_specs=[HBM] * T + [SEM] * 6 + [ANY],
        out_specs=[SEM] * 6 + [HBM] * T + [pl.BlockSpec(memory_space=pltpu.VMEM)],
        out_shape=[pltpu.SemaphoreType.DMA(())] * 6 + [pltpu.HBM(m.shape, m.dtype) for m in mats]
        + [jax.ShapeDtypeStruct((8, LANES), F32)],
        input_output_aliases={t: 6 + t for t in range(T)},
        compiler_params=pltpu.CompilerParams(has_side_effects=EFFECT),
    )(*mats, *sems, after)
    return list(outs[:6]), list(outs[6:6 + T]), outs[-1]


def gather_finish(mats, shardings, sems, after, *, name):
    T = len(mats)

    def body(*refs):
        sems2 = refs[T:T + 6]
        thru = refs[T + 7:2 * T + 7]
        x, y, c = _mesh_pos()
        for t in range(T):
            for j, (cx, cy) in enumerate(_other_chips(x, y)):
                sent = _weight_window(thru[t], shardings[t], 2 * cx + cy, c)
                other = _weight_window(thru[t], shardings[t], 2 * cx + cy, 1 - c)
                cp = pltpu.make_async_remote_copy(src_ref=sent, dst_ref=other, send_sem=sems2[j],
                                                  recv_sem=sems2[3 + j], device_id=(x, y, 1 - c),
                                                  device_id_type=MESH)
                cp.wait_recv()
                cp.wait_send()

    outs = pl.pallas_call(
        body, name=name,
        in_specs=[HBM] * T + [SEM] * 6 + [ANY],
        out_specs=[HBM] * T,
        out_shape=[pltpu.HBM(m.shape, m.dtype) for m in mats],
        input_output_aliases={t: t for t in range(T)},
        compiler_params=pltpu.CompilerParams(has_side_effects=EFFECT),
    )(*mats, *sems, after)
    return list(outs)


def pair_sum(buf, got, c, *, name, tr=512):
    _, _, n, rh, C = buf.shape
    tr = _tile(rh, tr)

    def body(c_ref, a_ref, b_ref, o_ref):
        o_ref[...] = (a_ref[...].astype(F32) + b_ref[...].astype(F32)).astype(o_ref.dtype)

    grid_spec = pltpu.PrefetchScalarGridSpec(
        num_scalar_prefetch=1, grid=(4, n, rh // tr),
        in_specs=[pl.BlockSpec((None, None, None, tr, C), lambda s, l, i, c_ref: (s, c_ref[0], l, i, 0)),
                  pl.BlockSpec((None, None, None, tr, C), lambda s, l, i, c_ref: (s, 0, l, i, 0))],
        out_specs=pl.BlockSpec((None, None, tr, C), lambda s, l, i, c_ref: (s, l, i, 0)))
    return pl.pallas_call(
        body, name=name, grid_spec=grid_spec, out_shape=jax.ShapeDtypeStruct((4, n, rh, C), BF16),
        compiler_params=_cparams("parallel", "parallel", "parallel"),
    )(c.reshape(1).astype(jnp.int32), buf, got)


def exchange_start(bufs, after, *, name):
    T = len(bufs)
    lands = [_in_hbm(lax.empty((4, 1) + b.shape[2:], b.dtype)) for b in bufs]

    def body(*refs):
        send_sem, recv_sem = refs[2 * T + 1], refs[2 * T + 2]
        src, dst = refs[2 * T + 3:3 * T + 3], refs[3 * T + 3:4 * T + 3]
        token = refs[4 * T + 3]
        x, y, c = _mesh_pos()
        for t in range(T):
            pltpu.make_async_remote_copy(src_ref=src[t].at[:, pl.ds(1 - c, 1)], dst_ref=dst[t], send_sem=send_sem,
                                         recv_sem=recv_sem, device_id=(x, y, 1 - c), device_id_type=MESH).start()
        token[...] = jnp.zeros_like(token)

    outs = pl.pallas_call(
        body, name=name,
        in_specs=[HBM] * (2 * T) + [ANY],
        out_specs=[SEM] * 2 + [HBM] * (2 * T) + [pl.BlockSpec(memory_space=pltpu.VMEM)],
        out_shape=[pltpu.SemaphoreType.DMA(())] * 2 + [pltpu.HBM(a.shape, a.dtype) for a in list(bufs) + lands]
        + [jax.ShapeDtypeStruct((8, LANES), F32)],
        input_output_aliases={t: 2 + t for t in range(2 * T)},
        compiler_params=pltpu.CompilerParams(has_side_effects=EFFECT),
    )(*[_in_hbm(b) for b in bufs], *lands, after)
    return list(outs[:2]), list(outs[2:2 + T]), list(outs[2 + T:2 + 2 * T]), outs[-1]


def exchange_wait(bufs, lands, sems, after, *, name):
    T = len(bufs)

    def body(*refs):
        send_sem, recv_sem = refs[2 * T], refs[2 * T + 1]
        src, dst = refs[2 * T + 3:3 * T + 3], refs[3 * T + 3:4 * T + 3]
        x, y, c = _mesh_pos()
        for t in range(T):
            cp = pltpu.make_async_remote_copy(src_ref=src[t].at[:, pl.ds(1 - c, 1)], dst_ref=dst[t],
                                              send_sem=send_sem, recv_sem=recv_sem, device_id=(x, y, 1 - c),
                                              device_id_type=MESH)
            cp.wait_recv()
            cp.wait_send()

    outs = pl.pallas_call(
        body, name=name,
        in_specs=[HBM] * (2 * T) + [SEM] * 2 + [ANY],
        out_specs=[HBM] * (2 * T),
        out_shape=[pltpu.HBM(a.shape, a.dtype) for a in list(bufs) + list(lands)],
        input_output_aliases={t: t for t in range(2 * T)},
        compiler_params=pltpu.CompilerParams(has_side_effects=EFFECT),
    )(*bufs, *lands, *sems, after)
    return list(outs[:T]), list(outs[T:])


def scatter_start(sums, *, name):
    T = len(sums)
    lands = [_in_hbm(lax.empty((3,) + s.shape[1:], s.dtype)) for s in sums]

    def body(*refs):
        sems = refs[2 * T:2 * T + 6]
        src, dst = refs[2 * T + 6:3 * T + 6], refs[3 * T + 6:4 * T + 6]
        token = refs[4 * T + 6]
        x, y, c = _mesh_pos()
        for t in range(T):
            for j, (cx, cy) in enumerate(_other_chips(x, y)):
                pltpu.make_async_remote_copy(src_ref=src[t].at[pl.ds(2 * cx + cy, 1)], dst_ref=dst[t].at[pl.ds(j, 1)],
                                             send_sem=sems[j], recv_sem=sems[3 + j], device_id=(cx, cy, c),
                                             device_id_type=MESH).start()
        token[...] = jnp.zeros_like(token)

    outs = pl.pallas_call(
        body, name=name,
        in_specs=[HBM] * (2 * T),
        out_specs=[SEM] * 6 + [HBM] * (2 * T) + [pl.BlockSpec(memory_space=pltpu.VMEM)],
        out_shape=[pltpu.SemaphoreType.DMA(())] * 6 + [pltpu.HBM(a.shape, a.dtype) for a in list(sums) + lands]
        + [jax.ShapeDtypeStruct((8, LANES), F32)],
        input_output_aliases={t: 6 + t for t in range(2 * T)},
        compiler_params=pltpu.CompilerParams(has_side_effects=EFFECT),
    )(*[_in_hbm(s) for s in sums], *lands)
    return list(outs[:6]), list(outs[6:6 + T]), list(outs[6 + T:6 + 2 * T]), outs[-1]


def scatter_wait(sums, lands, sems, after, *, name):
    T = len(sums)

    def body(*refs):
        sem_refs = refs[2 * T:2 * T + 6]
        src, dst = refs[2 * T + 7:3 * T + 7], refs[3 * T + 7:4 * T + 7]
        x, y, c = _mesh_pos()
        for t in range(T):
            for j, (cx, cy) in enumerate(_other_chips(x, y)):
                cp = pltpu.make_async_remote_copy(src_ref=src[t].at[pl.ds(2 * cx + cy, 1)],
                                                  dst_ref=dst[t].at[pl.ds(j, 1)], send_sem=sem_refs[j],
                                                  recv_sem=sem_refs[3 + j], device_id=(cx, cy, c),
                                                  device_id_type=MESH)
                cp.wait_recv()
                cp.wait_send()

    outs = pl.pallas_call(
        body, name=name,
        in_specs=[HBM] * (2 * T) + [SEM] * 6 + [ANY],
        out_specs=[HBM] * (2 * T),
        out_shape=[pltpu.HBM(a.shape, a.dtype) for a in list(sums) + list(lands)],
        input_output_aliases={t: t for t in range(2 * T)},
        compiler_params=pltpu.CompilerParams(has_side_effects=EFFECT),
    )(*sums, *lands, *sems, after)
    return list(outs[:T]), list(outs[T:])


def owner_sum(sums, got, s, c, grad, layer, *, name, tr=512):
    rh, C = sums.shape[2:]
    tr = _tile(rh, tr)
    nr = rh // tr

    def body(sc_ref, a_ref, b0_ref, b1_ref, b2_ref, g_ref, o_ref):
        o_ref[...] = ((a_ref[...].astype(F32) + b0_ref[...].astype(F32)) + b1_ref[...].astype(F32)) \
            + b2_ref[...].astype(F32)

    def got_spec(j):
        return pl.BlockSpec((None, None, tr, C), lambda i, sc_ref: (j, 0, i, 0))

    grid_spec = pltpu.PrefetchScalarGridSpec(
        num_scalar_prefetch=1, grid=(nr,),
        in_specs=[pl.BlockSpec((None, None, tr, C), lambda i, sc_ref: (sc_ref[0], 0, i, 0)),
                  got_spec(0), got_spec(1), got_spec(2), ANY],
        out_specs=pl.BlockSpec((None, tr, C), lambda i, sc_ref: (layer, sc_ref[1] * nr + i, 0)))
    return pl.pallas_call(
        body, name=name, grid_spec=grid_spec, out_shape=jax.ShapeDtypeStruct(grad.shape, F32),
        input_output_aliases={5: 0},
        compiler_params=_cparams("parallel"),
    )(jnp.stack([s, c]).astype(jnp.int32), sums, got, got, got, grad)


def join_halves(grads, *, name):
    T = len(grads)

    def body(*refs):
        out_refs = refs[T:2 * T]
        send_sems, recv_sems = refs[2 * T:]
        x, y, c = _mesh_pos()

        def win(t, hf):
            rh = grads[t].shape[1] // 2
            return out_refs[t].at[:, pl.ds(hf * rh, rh), :]

        def remote(t, w):
            return pltpu.make_async_remote_copy(src_ref=w, dst_ref=w, send_sem=send_sems.at[t],
                                                recv_sem=recv_sems.at[t], device_id=(x, y, 1 - c),
                                                device_id_type=MESH)

        cps = [remote(t, win(t, c)) for t in range(T)]
        for cp in cps:
            cp.start()
        for t in range(T):
            remote(t, win(t, 1 - c)).wait_recv()
        for cp in cps:
            cp.wait_send()

    return pl.pallas_call(
        body, name=name, in_specs=[ANY] * T, out_specs=[ANY] * T,
        out_shape=[jax.ShapeDtypeStruct(g.shape, g.dtype) for g in grads],
        input_output_aliases={t: t for t in range(T)},
        scratch_shapes=[pltpu.SemaphoreType.DMA((T,)), pltpu.SemaphoreType.DMA((T,))],
    )(*grads)


def allreduce_small(v, *, name):
    rows = v.shape[0]

    def body(v_ref, out_ref, buf, send_sems, recv_sems):
        x, y, c = _mesh_pos()
        me = 4 * x + 2 * y + c
        flips = [(dx, dy, dc) for dx in (0, 1) for dy in (0, 1) for dc in (0, 1)][1:]

        def peer(f):
            return tuple(1 - p if d else p for d, p in zip(f, (x, y, c)))

        cps = []
        for k, f in enumerate(flips):
            px, py, pc = peer(f)
            cps.append(pltpu.make_async_remote_copy(src_ref=v_ref, dst_ref=buf.at[me], send_sem=send_sems.at[k],
                                                    recv_sem=recv_sems.at[k], device_id=(px, py, pc),
                                                    device_id_type=MESH))
        for cp in cps:
            cp.start()
        buf[me] = v_ref[...]
        for k, f in enumerate(flips):
            px, py, pc = peer(f)
            slot = buf.at[4 * px + 2 * py + pc]
            pltpu.make_async_remote_copy(src_ref=slot, dst_ref=slot, send_sem=send_sems.at[k],
                                         recv_sem=recv_sems.at[k], device_id=(px, py, pc),
                                         device_id_type=MESH).wait_recv()
        for cp in cps:
            cp.wait_send()
        acc = buf[0]
        for i in range(1, 8):
            acc = acc + buf[i]
        out_ref[...] = acc

    vm = pl.BlockSpec(memory_space=pltpu.VMEM)
    return pl.pallas_call(
        body, name=name, in_specs=[vm], out_specs=vm,
        out_shape=jax.ShapeDtypeStruct(v.shape, F32),
        scratch_shapes=[pltpu.VMEM((8, rows, LANES), F32), pltpu.SemaphoreType.DMA((7,)),
                        pltpu.SemaphoreType.DMA((7,))],
    )(v)


def _adam_math(w, m, v, g):
    m = ADAM_B1 * m + (1.0 - ADAM_B1) * g
    v = ADAM_B2 * v + (1.0 - ADAM_B2) * (g * g)
    m_hat = m / (1.0 - ADAM_B1 ** ADAM_STEP)
    v_hat = v / (1.0 - ADAM_B2 ** ADAM_STEP)
    delta = -ADAM_LR * (m_hat / (jnp.sqrt(v_hat) + ADAM_EPS) + ADAM_WD * w)
    return delta, m, v


def adamw(w, m, v, g, *, name, tr=256):
    def body(w_ref, m_ref, v_ref, g_ref, d_ref, nm_ref, nv_ref):
        d, nm, nv = _adam_math(w_ref[...], m_ref[...], v_ref[...], g_ref[...])
        d_ref[...] = d
        nm_ref[...] = nm
        nv_ref[...] = nv

    shape = jax.ShapeDtypeStruct(w.shape, F32)
    if w.ndim == 2:
        return pl.pallas_call(body, name=name, out_shape=[shape] * 3)(w, m, v, g)
    n, R, C = w.shape
    tr = _tile(R, tr)
    spec = pl.BlockSpec((None, tr, C), lambda l, i: (l, i, 0))
    return pl.pallas_call(
        body, name=name, grid=(n, R // tr), in_specs=[spec] * 4, out_specs=[spec] * 3, out_shape=[shape] * 3,
        compiler_params=_cparams("parallel", "parallel"),
    )(w, m, v, g)


def _pack(parts):
    flat = jnp.concatenate([p.reshape(-1).astype(F32) for p in parts])
    size = -(-flat.shape[0] // (8 * LANES)) * (8 * LANES)
    return jnp.pad(flat, (0, size - flat.shape[0])).reshape(size // LANES, LANES)


def _unpack(block, shapes):
    flat = block.reshape(-1)
    out, pos = [], 0
    for shp in shapes:
        size = int(np.prod(shp))
        out.append(flat[pos:pos + size].reshape(shp))
        pos += size
    return out


SMALL = ("norm_mix_g", "norm_mlp_g", "final_norm_g", "hgrn_lb_raw", "hgrn_norm_g", "odd_sinks")
WEIGHTS = ("norm_mix_g", "norm_mlp_g", "final_norm_g", "even_w_in", "even_w_out", "hgrn_lb_raw", "hgrn_norm_g",
           "odd_w_qkv", "odd_b_qkv", "odd_sinks", "odd_w_o", "odd_b_o", "mlp_w1", "mlp_w2")


def kernel(x, norm_mix_g, norm_mlp_g, final_norm_g, even_w_in, even_w_out, hgrn_lb_raw, hgrn_norm_g, odd_w_qkv, odd_b_qkv, odd_sinks, odd_w_o, odd_b_o, mlp_w1, mlp_w2, loss_target, m_norm_mix_g, m_norm_mlp_g, m_final_norm_g, m_even_w_in, m_even_w_out, m_hgrn_lb_raw, m_hgrn_norm_g, m_odd_w_qkv, m_odd_b_qkv, m_odd_sinks, m_odd_w_o, m_odd_b_o, m_mlp_w1, m_mlp_w2, v_norm_mix_g, v_norm_mlp_g, v_final_norm_g, v_even_w_in, v_even_w_out, v_hgrn_lb_raw, v_hgrn_norm_g, v_odd_w_qkv, v_odd_b_qkv, v_odd_sinks, v_odd_w_o, v_odd_b_o, v_mlp_w1, v_mlp_w2):
    args = locals()
    W = {k: args[k] for k in WEIGHTS}
    M = {k: args["m_" + k] for k in WEIGHTS}
    V = {k: args["v_" + k] for k in WEIGHTS}
    _, S, D = x.shape
    depth = norm_mix_g.shape[0]
    HQ = D // C_DIM
    HKV = HQ // C_GROUP
    xi, yi, ci = _mesh_pos()
    shard = 2 * xi + yi

    n_odd, bq = odd_b_qkv.shape
    bo = odd_b_o.shape[1]
    keep = (ci == 0).astype(F32)
    pieces = [lax.dynamic_update_slice(jnp.zeros((n_odd, 4 * bq), F32), odd_b_qkv * keep, (0, shard * bq)),
              lax.dynamic_update_slice(jnp.zeros((n_odd, 4 * bo), F32), odd_b_o * keep, (0, shard * bo))]
    biases = allreduce_small(_pack(pieces), name="gather_biases")
    b_qkv_full, b_o_full = _unpack(biases, [(n_odd, 4 * bq), (n_odd, 4 * bo)])
    P = {k: W[k] for k in SMALL}
    P.update(odd_b_qkv=_pad_heads_cols(b_qkv_full, HQ + 2 * HKV), odd_b_o=b_o_full)

    def layer_keys(layer):
        mixer = ("even_w_in", "even_w_out") if layer % 2 == 0 else ("odd_w_qkv", "odd_w_o")
        return [(k, layer // 2) for k in mixer] + [("mlp_w1", layer), ("mlp_w2", layer)]

    keys = [layer_keys(0)[:1], layer_keys(0)[1:]] + [layer_keys(layer) for layer in range(1, depth)]
    shardings = [[SHARDING[k] for k, _ in ks] for ks in keys]
    started, forwarded = {}, {}

    def start_groups(gs, after):
        mats = [[cast_into_full(W[k], i, SHARDING[k], shard, name="cast_" + k) for k, i in keys[g]] for g in gs]
        res, token = gather_start(mats, [shardings[g] for g in gs], after, name="gather_start")
        started.update(zip(gs, res))
        return token

    dep0 = start_groups(list(range(len(keys))), biases)

    def forward_group(g, after):
        sems, mats = started[g]
        forwarded[g] = gather_forward(mats, shardings[g], sems, after, name="gather_forward")
        return forwarded[g][2]

    def finish_group(g, after):
        sems, mats, _ = forwarded[g]
        mats = gather_finish(mats, shardings[g], sems, after, name="gather_finish")
        Wl = {k: m for (k, _), m in zip(keys[g], mats)}
        if "odd_w_qkv" in Wl:
            Wl["odd_w_qkv"] = _pad_heads_cols(Wl["odd_w_qkv"], HQ + 2 * HKV)
            Wl["odd_w_o"] = _pad_heads_rows(Wl["odd_w_o"], HQ)
        return Wl

    def weights_in(layer, x_in):
        if layer == 0:
            forward_group(0, dep0)
        return finish_group(layer + 1 if layer else 0, x_in)

    def mid(layer, a):
        if layer == 0:
            forward_group(1, a)

    def weights_rest(layer, a):
        return finish_group(1, a) if layer == 0 else {}

    def after_mixer(layer, x1):
        return forward_group(layer + 2, x1) if layer + 1 < depth else None

    reduced = {k: lax.empty(W[k].shape, F32) for k in BIG}
    exchanging, scattering = [], []

    def settle(entry, after):
        kis, (sems, sums, lands, _) = entry
        sums, lands = scatter_wait(sums, lands, sems, after, name="scatter_wait")
        for (k, i), s, g in zip(kis, sums, lands):
            reduced[k] = owner_sum(s, g, shard, ci, reduced[k], i, name="owner_sum_" + k)

    def scatter_exchanged(after):
        for kis, (sems, bufs, lands, _) in exchanging:
            bufs, lands = exchange_wait(bufs, lands, sems, after, name="exchange_wait")
            sums = [pair_sum(b, g, ci, name="pair_sum_" + k) for (k, _), b, g in zip(kis, bufs, lands)]
            scattering.append((kis, scatter_start(sums, name="scatter_start")))
        exchanging.clear()

    def grads_ready(kis, B, after):
        scatter_exchanged(after)
        while len(scattering) > MAX_SCATTERS_IN_FLIGHT:
            settle(scattering.pop(0), after)
        bufs = [B[ki].reshape((4, 2, 1) + B[ki].shape[2:]) for ki in kis]
        order = scattering[-1][1][3] if scattering else after
        exchanging.append((kis, exchange_start(bufs, order, name="exchange_start")))
        return exchanging[-1][1][3]

    def bwd_mid(after):
        scatter_exchanged(after)
        return scattering[-1][1][3] if scattering else None

    hooks = dict(weights_in=weights_in, mid=mid, weights_rest=weights_rest, after_mixer=after_mixer,
                 grads_ready=grads_ready, bwd_mid=bwd_mid)
    loss, dx, G, B = local_step(x[0], loss_target[0], P, hooks)
    scatter_exchanged(dx)
    last = scattering.pop()
    for entry in scattering:
        settle(entry, last[1][3])

    small_keys = SMALL + ("odd_b_qkv", "odd_b_o")
    small_shapes = [G[k].shape for k in small_keys]
    small = _unpack(allreduce_small(_pack([G[k] for k in small_keys]), name="reduce_small"), small_shapes)
    grads = dict(zip(small_keys, small))
    grads["odd_b_qkv"] = lax.dynamic_slice(grads["odd_b_qkv"], (0, shard * bq), (n_odd, bq))
    grads["odd_b_o"] = lax.dynamic_slice(grads["odd_b_o"], (0, shard * bo), (n_odd, bo))

    deltas, new_m, new_v = {}, {}, {}

    def update(names):
        for k in names:
            w2 = (lambda a: a.reshape(1, -1)) if W[k].ndim == 1 else (lambda a: a)
            d, nm, nv = adamw(w2(W[k]), w2(M[k]), w2(V[k]), w2(grads[k]), name="adamw_" + k)
            deltas[k], new_m[k], new_v[k] = (a.reshape(W[k].shape) for a in (d, nm, nv))

    late = [k for k in BIG if k in {name for name, _ in last[0]}]
    early = [k for k in BIG if k not in late]
    grads.update(zip(early, join_halves([reduced[k] for k in early], name="join_halves_early")))
    update([k for k in WEIGHTS if k not in late])
    settle(last, deltas[early[-1]])
    grads.update(zip(late, join_halves([reduced[k] for k in late], name="join_halves_late")))
    update(late)

    loss = lax.psum(loss[0, 0], ("x", "y", "c"))
    return (loss, dx[None], *[grads[k] for k in WEIGHTS], *[deltas[k] for k in WEIGHTS],
            *[new_m[k] for k in WEIGHTS], *[new_v[k] for k in WEIGHTS])
```

```python
import functools

import jax
import jax.numpy as jnp
import numpy as np
from jax import lax
from jax.experimental import pallas as pl
from jax.experimental.pallas import tpu as pltpu

F32 = jnp.float32
BF16 = jnp.bfloat16

NORM_EPS = 1e-5
ROPE_THETA = 500000.0
ROPE_FRACTION = 4
LANES = 128
BAND = 128
A_DIM = 128
A_BRANCHES = ((128, 1), (512, 4), (2048, 16))
B_DIM = 128
B_CHUNK = 64
B_SUB = 16
A_HEADS_PER_STEP = 2
B_HEADS_PER_STEP = 2
MAX_SCATTERS_IN_FLIGHT = 3
C_DIM = 64
C_GROUP = 8
C_WINDOW = 128

ADAM_LR = 0.001
ADAM_B1 = 0.9
ADAM_B2 = 0.999
ADAM_EPS = 1e-08
ADAM_WD = 0.01
ADAM_STEP = 10

VMEM_LIMIT = 56 * 1024 * 1024


def _cparams(*sem):
    return pltpu.CompilerParams(dimension_semantics=tuple(sem), vmem_limit_bytes=VMEM_LIMIT)


def _tile(n, want):
    if n <= want:
        return n
    t = want - want % LANES
    while n % t:
        t -= LANES
    assert t > 0, (n, want)
    return t


def matmul(a, b, *, trans_a=False, trans_b=False, out_dtypes, a_fn=None, epilogue=None, extras=(), name,
           b_cols=None, into=None, dep=None, tm=1024, tn=1024, tk=2048):
    if trans_a:
        K, M = a.shape
    else:
        M, K = a.shape
    if trans_b:
        N, K2 = b.shape
    else:
        K2, N = b.shape
    assert K == K2, (a.shape, b.shape)
    col0 = 0
    if b_cols is not None:
        assert not trans_b
        col0, N = b_cols[0], b_cols[1] - b_cols[0]
    if into is not None:
        rh, cs = into[0][2:]
        tm, tn = _tile(rh, tm), _tile(cs, tn)
    tm, tn, tk = _tile(M, tm), _tile(N, tn), _tile(K, tk)
    nk = K // tk
    n_extra = len(extras)
    n_out = len(out_dtypes)

    def body(*refs):
        a_ref, b_ref = refs[0], refs[1]
        extra_refs = refs[2:2 + n_extra]
        out_refs = refs[-1 - n_out:-1]
        acc_ref = refs[-1]
        k = pl.program_id(2)
        at = a_ref[...]
        if a_fn is not None:
            at = a_fn(at.astype(F32))
        at = at.astype(BF16)
        bt = b_ref[...].astype(BF16)
        dims = (((0,) if trans_a else (1,), (1,) if trans_b else (0,)), ((), ()))
        prod = lax.dot_general(at, bt, dims, preferred_element_type=F32)

        def finish(acc):
            ex = [r[...] for r in extra_refs]
            outs = epilogue(acc, *ex) if epilogue is not None else (acc,)
            for o_ref, o in zip(out_refs, outs):
                o_ref[...] = o.astype(o_ref.dtype)

        if nk == 1:
            finish(prod)
            return

        @pl.when(k == 0)
        def _():
            acc_ref[...] = prod

        @pl.when((k > 0) & (k < nk - 1))
        def _():
            acc_ref[...] += prod

        @pl.when(k == nk - 1)
        def _():
            finish(acc_ref[...] + prod)

    a_spec = pl.BlockSpec((tk, tm), lambda i, j, k: (k, i)) if trans_a else pl.BlockSpec((tm, tk), lambda i, j, k: (i, k))
    assert col0 % tn == 0, (col0, tn)
    b_spec = (pl.BlockSpec((tn, tk), lambda i, j, k: (j, k)) if trans_b
              else pl.BlockSpec((tk, tn), lambda i, j, k: (k, col0 // tn + j)))
    e_specs = []
    for e in extras:
        if e.shape == (1, N):
            e_specs.append(pl.BlockSpec((1, tn), lambda i, j, k: (0, j)))
        else:
            assert e.shape == (M, N), (e.shape, M, N)
            e_specs.append(pl.BlockSpec((tm, tn), lambda i, j, k: (i, j)))
    args = [a, b, *extras]
    in_specs = [a_spec, b_spec] + e_specs
    if dep is not None:
        args.append(dep)
        in_specs.append(pl.BlockSpec(dep.shape, lambda i, j, k: (0, 0)))
    if into is None:
        out_specs = [pl.BlockSpec((tm, tn), lambda i, j, k: (i, j)) for _ in out_dtypes]
        out_shape = [jax.ShapeDtypeStruct((M, N), dt) for dt in out_dtypes]
    else:
        buf_shape, sharding = into
        assert n_out == 1
        index = grad_buffer_index(buf_shape, (M, N), sharding, tm, tn)
        out_specs = [pl.BlockSpec((None, None, tm, tn), lambda i, j, k: index(i, j))]
        out_shape = [jax.ShapeDtypeStruct(buf_shape, out_dtypes[0])]
    outs = pl.pallas_call(
        body, name=name,
        grid=(M // tm, N // tn, nk),
        in_specs=in_specs, out_specs=out_specs, out_shape=out_shape,
        scratch_shapes=[pltpu.VMEM((tm, tn), F32)],
        compiler_params=_cparams("parallel", "parallel", "arbitrary"),
    )(*args)
    return outs[0] if n_out == 1 else tuple(outs)


def grad_buffer_shape(mat_shape, sharding):
    M, N = mat_shape
    return (4, 2, M // 2, N // 4) if sharding == "cols" else (4, 2, M // 8, N)


def grad_buffer_index(buf_shape, mat_shape, sharding, tm, tn):
    _, _, rh, cs = buf_shape
    M, N = mat_shape
    ib, jb = rh // tm, cs // tn
    if sharding == "cols":
        assert (M, N) == (2 * rh, 4 * cs), (buf_shape, mat_shape)
        return lambda i, j: (j // jb, i // ib, i % ib, j % jb)
    assert sharding == "rows" and (M, N) == (8 * rh, cs), (buf_shape, mat_shape)
    return lambda i, j: (i // (2 * ib), (i // ib) % 2, i % ib, j)


def rmsnorm_fwd(x, g, *, name, dep=None, tr=256):
    S, D = x.shape
    tr = _tile(S, tr)

    def body(x_ref, g_ref, *rest):
        h_ref = rest[-1]
        xv = x_ref[...]
        rstd = lax.rsqrt(jnp.mean(xv * xv, axis=-1, keepdims=True) + NORM_EPS)
        h_ref[...] = (xv * rstd * g_ref[...]).astype(h_ref.dtype)

    args = [x, g.reshape(1, D)]
    in_specs = [pl.BlockSpec((tr, D), lambda i: (i, 0)), pl.BlockSpec((1, D), lambda i: (0, 0))]
    if dep is not None:
        args.append(dep)
        in_specs.append(pl.BlockSpec(dep.shape, lambda i: (0, 0)))
    return pl.pallas_call(
        body, name=name, grid=(S // tr,),
        in_specs=in_specs,
        out_specs=pl.BlockSpec((tr, D), lambda i: (i, 0)),
        out_shape=jax.ShapeDtypeStruct((S, D), BF16),
        compiler_params=_cparams("parallel"),
    )(*args)


def _rms_bwd_rows(xv, gv, dh):
    rstd = lax.rsqrt(jnp.mean(xv * xv, axis=-1, keepdims=True) + NORM_EPS)
    xhat = xv * rstd
    dxhat = dh * gv
    dx = rstd * (dxhat - xhat * jnp.mean(dxhat * xhat, axis=-1, keepdims=True))
    return dx, dh * xhat


def rmsnorm_bwd(x, g, dh, dres, *, name, tr=256):
    S, D = x.shape
    tr = _tile(S, tr)

    def body(x_ref, g_ref, dh_ref, dres_ref, dx_ref, dg_ref):
        @pl.when(pl.program_id(0) == 0)
        def _():
            dg_ref[...] = jnp.zeros_like(dg_ref)

        dx, dgr = _rms_bwd_rows(x_ref[...], g_ref[...], dh_ref[...].astype(F32))
        dx_ref[...] = dres_ref[...] + dx
        dg_ref[...] += jnp.sum(dgr, axis=0, keepdims=True)

    row = pl.BlockSpec((tr, D), lambda i: (i, 0))
    vec = pl.BlockSpec((1, D), lambda i: (0, 0))
    return pl.pallas_call(
        body, name=name, grid=(S // tr,),
        in_specs=[row, vec, row, row],
        out_specs=[row, vec],
        out_shape=[jax.ShapeDtypeStruct((S, D), F32), jax.ShapeDtypeStruct((1, D), F32)],
        compiler_params=_cparams("arbitrary"),
    )(x, g.reshape(1, D), dh, dres)


def final_norm_loss(x, g, target, *, name, tr=256):
    S, D = x.shape
    tr = _tile(S, tr)

    def body(x_ref, g_ref, t_ref, loss_ref, dx_ref, dg_ref):
        @pl.when(pl.program_id(0) == 0)
        def _():
            dg_ref[...] = jnp.zeros_like(dg_ref)
            loss_ref[...] = jnp.zeros_like(loss_ref)

        xv, gv = x_ref[...], g_ref[...]
        rstd = lax.rsqrt(jnp.mean(xv * xv, axis=-1, keepdims=True) + NORM_EPS)
        err = xv * rstd * gv - t_ref[...]
        part = 0.5 * jnp.sum(jnp.mean(err * err, axis=-1, keepdims=True), axis=0, keepdims=True)
        loss_ref[...] += jnp.broadcast_to(part, loss_ref.shape)
        dx, dgr = _rms_bwd_rows(xv, gv, err * (1.0 / D))
        dx_ref[...] = dx
        dg_ref[...] += jnp.sum(dgr, axis=0, keepdims=True)

    row = pl.BlockSpec((tr, D), lambda i: (i, 0))
    vec = pl.BlockSpec((1, D), lambda i: (0, 0))
    return pl.pallas_call(
        body, name=name, grid=(S // tr,),
        in_specs=[row, vec, row],
        out_specs=[pl.BlockSpec((8, LANES), lambda i: (0, 0)), row, vec],
        out_shape=[jax.ShapeDtypeStruct((8, LANES), F32), jax.ShapeDtypeStruct((S, D), F32),
                   jax.ShapeDtypeStruct((1, D), F32)],
        compiler_params=_cparams("arbitrary"),
    )(x, g.reshape(1, D), target)


def colsum(a, *, name, tr=256):
    S, N = a.shape
    tr = _tile(S, tr)

    def body(a_ref, o_ref):
        @pl.when(pl.program_id(0) == 0)
        def _():
            o_ref[...] = jnp.zeros_like(o_ref)

        o_ref[...] += jnp.sum(a_ref[...].astype(F32), axis=0, keepdims=True)

    return pl.pallas_call(
        body, name=name, grid=(S // tr,),
        in_specs=[pl.BlockSpec((tr, N), lambda i: (i, 0))],
        out_specs=pl.BlockSpec((1, N), lambda i: (0, 0)),
        out_shape=jax.ShapeDtypeStruct((1, N), F32),
        compiler_params=_cparams("arbitrary"),
    )(a)


def rope_tables(seq, head_dim):
    rot = head_dim // ROPE_FRACTION
    half = rot // 2
    inv_freq = 1.0 / (ROPE_THETA ** (jnp.arange(0, rot, 2, dtype=F32) / rot))
    ang = jnp.arange(seq, dtype=F32)[:, None] * inv_freq[None, :]
    cos, sin = jnp.cos(ang), jnp.sin(ang)
    ones = jnp.ones((seq, LANES - 2 * half), F32)
    zeros = jnp.zeros((seq, LANES - 2 * half), F32)
    zh = jnp.zeros((seq, half), F32)
    c = jnp.concatenate([cos, cos, ones], axis=1)
    sa = jnp.concatenate([-sin, zh, zeros], axis=1)
    sb = jnp.concatenate([zh, sin, zeros], axis=1)
    return c, sa, sb, half


def _rope(x, c, sa, sb, half):
    return x * c + pltpu.roll(x, LANES - half, 1) * sa + pltpu.roll(x, half, 1) * sb


def _rope_bwd(d, c, sa, sb, half):
    return d * c + pltpu.roll(d * sa, half, 1) + pltpu.roll(d * sb, LANES - half, 1)


def _band_masks(n, max_dist):
    qi = lax.broadcasted_iota(jnp.int32, (BAND, BAND), 0)
    kj = lax.broadcasted_iota(jnp.int32, (BAND, BAND), 1)
    cur = kj <= qi
    prev = ((kj >= qi) if max_dist == BAND else (kj > qi)) & (n > 0)
    return prev, cur


def _dot_nt(a, b):
    return lax.dot_general(a, b, (((1,), (1,)), ((), ())), preferred_element_type=F32)


def _dot_tn(a, b):
    return lax.dot_general(a, b, (((0,), (0,)), ((), ())), preferred_element_type=F32)


def _dot(a, b):
    return jnp.dot(a, b, preferred_element_type=F32)


def banded_fwd(src, tabs, *, L, dil, n_heads, group, hb, q_blk, k_blk, v_blk, blk_per_row, max_dist, scale, half,
               sink=None, normalize, name):
    c_t, sa_t, sb_t = tabs
    nb = L // BAND
    kvb = hb // group
    n_hg = n_heads // hb
    W = dil * n_heads * LANES
    use_sink = sink is not None
    assert n_heads % hb == 0 and hb % group == 0
    assert (q_blk % hb, k_blk % kvb, v_blk % kvb) == (0, 0, 0) and (dil == 1 or blk_per_row % hb == 0)

    def body(*refs):
        q_ref, k_ref, v_ref, c_ref, sa_ref, sb_ref = refs[:6]
        pos = 6
        if use_sink:
            sink_ref = refs[pos]
            pos += 1
        o_ref, m_ref = refs[pos], refs[pos + 1]
        l_ref = None if normalize else refs[pos + 2]
        kr_ref, vb_ref = refs[-2], refs[-1]
        n = pl.program_id(2)

        @pl.when(n == 0)
        def _():
            def prep(b, carry):
                rows = pl.ds(pl.multiple_of(b * BAND, BAND), BAND)
                cv, sav, sbv = c_ref[rows, :], sa_ref[rows, :], sb_ref[rows, :]
                for j in range(kvb):
                    cols = slice(j * LANES, (j + 1) * LANES)
                    kr_ref[rows, cols] = _rope(k_ref[rows, cols].astype(F32), cv, sav, sbv, half).astype(BF16)
                    vb_ref[rows, cols] = v_ref[rows, cols].astype(BF16)
                return carry

            lax.fori_loop(0, nb, prep, 0)

        rows = pl.ds(pl.multiple_of(n * BAND, BAND), BAND)
        prow = pl.ds(pl.multiple_of(jnp.maximum(n - 1, 0) * BAND, BAND), BAND)
        cv, sav, sbv = c_ref[rows, :], sa_ref[rows, :], sb_ref[rows, :]
        mp, mc = _band_masks(n, max_dist)
        for i in range(hb):
            qc = slice(i * LANES, (i + 1) * LANES)
            kc = slice((i // group) * LANES, (i // group + 1) * LANES)
            q = _rope(q_ref[:, qc].astype(F32), cv, sav, sbv, half).astype(BF16)
            sp = jnp.where(mp, _dot_nt(q, kr_ref[prow, kc]) * scale, -jnp.inf)
            sc = jnp.where(mc, _dot_nt(q, kr_ref[rows, kc]) * scale, -jnp.inf)
            m = jnp.maximum(jnp.max(sp, axis=-1, keepdims=True), jnp.max(sc, axis=-1, keepdims=True))
            if use_sink:
                sk = sink_ref[i, :, 0:1]
                m = jnp.maximum(m, sk)
            pp = jnp.exp(sp - m)
            pc = jnp.exp(sc - m)
            l = jnp.sum(pp, axis=-1, keepdims=True) + jnp.sum(pc, axis=-1, keepdims=True)
            if use_sink:
                l = l + jnp.exp(sk - m)
            num = _dot(pp.astype(BF16), vb_ref[prow, kc]) + _dot(pc.astype(BF16), vb_ref[rows, kc])
            if normalize:
                o_ref[:, qc] = num / l
                m_ref[:, qc] = jnp.broadcast_to(m + jnp.log(l), (BAND, LANES))
            else:
                o_ref[:, qc] = num
                m_ref[:, qc] = jnp.broadcast_to(m, (BAND, LANES))
                l_ref[:, qc] = jnp.broadcast_to(l, (BAND, LANES))

    qspec = pl.BlockSpec((BAND, hb * LANES), lambda r, g, n: (n, (r * blk_per_row + q_blk) // hb + g))

    def kv(off):
        return pl.BlockSpec((L, kvb * LANES), lambda r, g, n: (0, (r * blk_per_row + off) // kvb + g))

    tab = pl.BlockSpec((L, LANES), lambda r, g, n: (0, r))
    out = pl.BlockSpec((BAND, hb * LANES), lambda r, g, n: (n, r * n_hg + g))
    in_specs = [qspec, kv(k_blk), kv(v_blk), tab, tab, tab]
    args = [src, src, src, c_t, sa_t, sb_t]
    if use_sink:
        in_specs.append(pl.BlockSpec((hb, 1, LANES), lambda r, g, n: (g, 0, 0)))
        args.append(sink)
    n_out = 2 if normalize else 3
    return pl.pallas_call(
        body, name=name, grid=(dil, n_hg, nb),
        in_specs=in_specs,
        out_specs=[out] * n_out,
        out_shape=[jax.ShapeDtypeStruct((L, W), F32)] * n_out,
        scratch_shapes=[pltpu.VMEM((L, kvb * LANES), BF16), pltpu.VMEM((L, kvb * LANES), BF16)],
        compiler_params=_cparams("parallel", "parallel", "arbitrary"),
    )(*args)


def banded_bwd(src, tabs, o, lse, do, *, L, dil, n_heads, group, hb, q_blk, k_blk, v_blk, blk_per_row, max_dist,
               scale, half, do_bpr, sink=None, acc=None, dep=None, name):
    c_t, sa_t, sb_t = tabs
    nb = L // BAND
    kvb = hb // group
    n_hg = n_heads // hb
    n_kv = n_heads // group
    use_sink = sink is not None
    use_acc = acc is not None
    assert n_heads % hb == 0 and hb % group == 0 and do_bpr % hb == 0
    assert (q_blk % hb, k_blk % kvb, v_blk % kvb) == (0, 0, 0) and (dil == 1 or blk_per_row % hb == 0)

    def body(*refs):
        q_ref, k_ref, v_ref, c_ref, sa_ref, sb_ref, o_ref, lse_ref, do_ref = refs[:9]
        pos = 9
        if use_sink:
            sink_ref = refs[pos]
            pos += 1
        if use_acc:
            aq_ref, ak_ref, av_ref = refs[pos:pos + 3]
            pos += 3
        if dep is not None:
            pos += 1
        dq_ref, dk_ref, dv_ref = refs[pos:pos + 3]
        pos += 3
        if use_sink:
            dsink_ref = refs[pos]
        kr_ref, vb_ref, dka_ref, dva_ref = refs[-4:]
        n = pl.program_id(2)

        @pl.when(n == 0)
        def _():
            def prep(b, carry):
                rows = pl.ds(pl.multiple_of(b * BAND, BAND), BAND)
                cv, sav, sbv = c_ref[rows, :], sa_ref[rows, :], sb_ref[rows, :]
                for j in range(kvb):
                    cols = slice(j * LANES, (j + 1) * LANES)
                    kr_ref[rows, cols] = _rope(k_ref[rows, cols].astype(F32), cv, sav, sbv, half).astype(BF16)
                    vb_ref[rows, cols] = v_ref[rows, cols].astype(BF16)
                return carry

            lax.fori_loop(0, nb, prep, 0)
            dka_ref[...] = jnp.zeros_like(dka_ref)
            dva_ref[...] = jnp.zeros_like(dva_ref)
            if use_sink:
                dsink_ref[...] = jnp.zeros_like(dsink_ref)

        rows = pl.ds(pl.multiple_of(n * BAND, BAND), BAND)
        prow = pl.ds(pl.multiple_of(jnp.maximum(n - 1, 0) * BAND, BAND), BAND)
        cv, sav, sbv = c_ref[rows, :], sa_ref[rows, :], sb_ref[rows, :]
        mp, mc = _band_masks(n, max_dist)
        for i in range(hb):
            qc = slice(i * LANES, (i + 1) * LANES)
            kc = slice((i // group) * LANES, (i // group + 1) * LANES)
            q = _rope(q_ref[:, qc].astype(F32), cv, sav, sbv, half).astype(BF16)
            dov = do_ref[:, qc]
            lsev = lse_ref[:, i * LANES:i * LANES + 1]
            delta = jnp.sum(dov * o_ref[:, qc], axis=-1, keepdims=True)
            dob = dov.astype(BF16)
            kp, kcur, vp, vcur = kr_ref[prow, kc], kr_ref[rows, kc], vb_ref[prow, kc], vb_ref[rows, kc]
            pp = jnp.exp(jnp.where(mp, _dot_nt(q, kp) * scale, -jnp.inf) - lsev)
            pc = jnp.exp(jnp.where(mc, _dot_nt(q, kcur) * scale, -jnp.inf) - lsev)
            dsp = (pp * (_dot_nt(dob, vp) - delta) * scale).astype(BF16)
            dsc = (pc * (_dot_nt(dob, vcur) - delta) * scale).astype(BF16)
            dq = _rope_bwd(_dot(dsp, kp) + _dot(dsc, kcur), cv, sav, sbv, half)
            if use_acc:
                dq = dq + aq_ref[:, qc]
            dq_ref[:, qc] = dq
            dka_ref[prow, kc] += _dot_tn(dsp, q)
            dka_ref[rows, kc] += _dot_tn(dsc, q)
            dva_ref[prow, kc] += _dot_tn(pp.astype(BF16), dob)
            dva_ref[rows, kc] += _dot_tn(pc.astype(BF16), dob)
            if use_sink:
                dsk = jnp.sum(jnp.exp(sink_ref[i, :, 0:1] - lsev) * delta, axis=0, keepdims=True)
                dsink_ref[i] -= jnp.broadcast_to(dsk, (1, LANES))

        @pl.when(n == nb - 1)
        def _():
            def fin(b, carry):
                brow = pl.ds(pl.multiple_of(b * BAND, BAND), BAND)
                bc, bsa, bsb = c_ref[brow, :], sa_ref[brow, :], sb_ref[brow, :]
                for j in range(kvb):
                    cols = slice(j * LANES, (j + 1) * LANES)
                    dk = _rope_bwd(dka_ref[brow, cols], bc, bsa, bsb, half)
                    dv = dva_ref[brow, cols]
                    if use_acc:
                        dk = dk + ak_ref[brow, cols]
                        dv = dv + av_ref[brow, cols]
                    dk_ref[brow, cols] = dk
                    dv_ref[brow, cols] = dv
                return carry

            lax.fori_loop(0, nb, fin, 0)

    qspec = pl.BlockSpec((BAND, hb * LANES), lambda r, g, n: (n, (r * blk_per_row + q_blk) // hb + g))

    def kv(off):
        return pl.BlockSpec((L, kvb * LANES), lambda r, g, n: (0, (r * blk_per_row + off) // kvb + g))

    tab = pl.BlockSpec((L, LANES), lambda r, g, n: (0, r))
    qo = pl.BlockSpec((BAND, hb * LANES), lambda r, g, n: (n, r * n_hg + g))
    kvo = pl.BlockSpec((L, kvb * LANES), lambda r, g, n: (0, r * (n_kv // kvb) + g))
    dospec = pl.BlockSpec((BAND, hb * LANES), lambda r, g, n: (n, r * (do_bpr // hb) + g))
    in_specs = [qspec, kv(k_blk), kv(v_blk), tab, tab, tab, qo, qo, dospec]
    args = [src, src, src, c_t, sa_t, sb_t, o, lse, do]
    sink_spec = pl.BlockSpec((hb, 1, LANES), lambda r, g, n: (g, 0, 0))
    if use_sink:
        in_specs.append(sink_spec)
        args.append(sink)
    if use_acc:
        in_specs += [qo, kvo, kvo]
        args += list(acc)
    if dep is not None:
        in_specs.append(pl.BlockSpec(dep.shape, lambda r, g, n: (0, 0)))
        args.append(dep)
    out_specs = [qo, kvo, kvo]
    out_shape = [jax.ShapeDtypeStruct((L, dil * n_heads * LANES), F32),
                 jax.ShapeDtypeStruct((L, dil * n_kv * LANES), F32),
                 jax.ShapeDtypeStruct((L, dil * n_kv * LANES), F32)]
    if use_sink:
        out_specs.append(sink_spec)
        out_shape.append(jax.ShapeDtypeStruct((n_heads, 1, LANES), F32))
    return pl.pallas_call(
        body, name=name, grid=(dil, n_hg, nb),
        in_specs=in_specs, out_specs=out_specs, out_shape=out_shape,
        scratch_shapes=[pltpu.VMEM((L, kvb * LANES), BF16), pltpu.VMEM((L, kvb * LANES), BF16),
                        pltpu.VMEM((L, kvb * LANES), F32), pltpu.VMEM((L, kvb * LANES), F32)],
        compiler_params=_cparams("parallel", "parallel", "arbitrary"),
    )(*args)


def merge_branches(parts, *, name, tr=256):
    S, W = parts[0][0].shape
    tr = _tile(S, tr)
    nbr = len(parts)

    def body(*refs):
        ins, (o_ref, lse_ref) = refs[:3 * nbr], refs[3 * nbr:]
        nums = [ins[3 * i][...] for i in range(nbr)]
        ms = [ins[3 * i + 1][...] for i in range(nbr)]
        ls = [ins[3 * i + 2][...] for i in range(nbr)]
        mx = functools.reduce(jnp.maximum, ms)
        ws = [jnp.exp(m - mx) for m in ms]
        num = sum(w * n for w, n in zip(ws, nums))
        den = sum(w * l for w, l in zip(ws, ls))
        o_ref[...] = num / den
        lse_ref[...] = mx + jnp.log(den)

    row = pl.BlockSpec((tr, W), lambda i: (i, 0))
    flat = [a for p in parts for a in p]
    return pl.pallas_call(
        body, name=name, grid=(S // tr,),
        in_specs=[row] * len(flat), out_specs=[row, row],
        out_shape=[jax.ShapeDtypeStruct((S, W), F32)] * 2,
        compiler_params=_cparams("parallel"),
    )(*flat)


def _sigmoid(x):
    return 1.0 / (1.0 + jnp.exp(-x))


def _tri(n, lower):
    r = lax.broadcasted_iota(jnp.int32, (n, n), 0)
    c = lax.broadcasted_iota(jnp.int32, (n, n), 1)
    return ((c <= r) if lower else (c >= r)).astype(F32)


def _dot_exact(a, b, trans_a=False, trans_b=False):
    dims = (((0,) if trans_a else (1,), (1,) if trans_b else (0,)), ((), ()))
    return lax.dot_general(a, b, dims, preferred_element_type=F32, precision=lax.Precision.HIGHEST)


def _hgrn_gates(qb, fb, lb):
    sq = _sigmoid(qb)
    q = qb * sq * (B_DIM ** -0.5)
    sf = _sigmoid(fb)
    gate = lb + (1.0 - lb) * sf
    return q, 1.0 - gate, gate, sq, sf


def _hgrn_intra_fwd(q_ref, k_ref, b_ref):
    C, n_sub = B_CHUNK, B_CHUNK // B_SUB
    b_all, k_all = b_ref[...], k_ref[...]
    srow = lax.broadcasted_iota(jnp.int32, (C, LANES), 0)
    lane = lax.broadcasted_iota(jnp.int32, (B_SUB, C), 1)
    trow = lax.broadcasted_iota(jnp.int32, (B_SUB, LANES), 0)
    blocks = []
    for i in range(n_sub):
        r0 = i * B_SUB
        qi, bi = q_ref[pl.ds(r0, B_SUB), :], b_ref[pl.ds(r0, B_SUB), :]
        if i == 0:
            a_i = jnp.zeros((B_SUB, C), F32)
        else:
            ref_b = b_ref[pl.ds(r0 - 1, 1), :]
            qt = (qi * jnp.exp(bi - ref_b)).astype(BF16)
            kt = jnp.where(srow < r0, k_all * jnp.exp(jnp.minimum(ref_b - b_all, 0.0)), 0.0).astype(BF16)
            a_i = _dot_nt(qt, kt)
        for sl in range(B_SUB):
            s = r0 + sl
            e = jnp.where(trow >= sl, jnp.exp(jnp.minimum(bi - b_ref[pl.ds(s, 1), :], 0.0)), 0.0)
            colv = jnp.sum(qi * k_ref[pl.ds(s, 1), :] * e, axis=-1, keepdims=True)
            a_i = a_i + jnp.where(lane == s, colv, 0.0)
        blocks.append(a_i)
    return jnp.concatenate(blocks, axis=0)


def hgrn_fwd(src, lb, norm_g, *, S, n_heads, hb, q_blk, f_blk, i_blk, g_blk, name):
    nc = S // B_CHUNK
    assert n_heads % hb == 0 and (q_blk % hb, f_blk % hb, i_blk % hb, g_blk % hb) == (0, 0, 0, 0)

    def body(qb_ref, fb_ref, ib_ref, gb_ref, lb_ref, ng_ref, out_ref, o_ref, st_ref, state_ref, q_s, k_s, b_s):
        state_ref[...] = jnp.zeros_like(state_ref)
        tril = _tri(B_CHUNK, True)

        def step(c, carry):
            rows = pl.ds(pl.multiple_of(c * B_CHUNK, B_CHUNK), B_CHUNK)
            for j in range(hb):
                cols = slice(j * LANES, (j + 1) * LANES)
                q, k, gate, _, _ = _hgrn_gates(qb_ref[rows, cols], fb_ref[rows, cols], lb_ref[j])
                b = _dot_exact(tril, jnp.log(gate))
                vf = ib_ref[rows, cols]
                v = vf.astype(BF16)
                q_s[j], k_s[j], b_s[j] = q, k, b
                st = state_ref[j]
                st_ref[j, c] = st
                a = _hgrn_intra_fwd(q_s.at[j], k_s.at[j], b_s.at[j])
                o = _dot(a.astype(BF16), v) + _dot_nt((q * jnp.exp(b)).astype(BF16), st.astype(BF16))
                b_last = b_s[j, pl.ds(B_CHUNK - 1, 1), :]
                state_ref[j] = st * jnp.exp(b_last) + _dot_exact(vf, k * jnp.exp(b_last - b), trans_a=True)
                o_ref[rows, cols] = o
                rstd = lax.rsqrt(jnp.mean(o * o, axis=-1, keepdims=True) + NORM_EPS)
                gb = gb_ref[rows, cols]
                out_ref[rows, cols] = o * rstd * ng_ref[...] * (gb * _sigmoid(gb))
            return carry

        lax.fori_loop(0, nc, step, 0)

    def col(off):
        return pl.BlockSpec((S, hb * LANES), lambda g: (0, off // hb + g))

    hv = pl.BlockSpec((hb, 1, LANES), lambda g: (g, 0, 0))
    return pl.pallas_call(
        body, name=name, grid=(n_heads // hb,),
        in_specs=[col(q_blk), col(f_blk), col(i_blk), col(g_blk), hv, pl.BlockSpec((1, LANES), lambda g: (0, 0))],
        out_specs=[col(0), col(0), pl.BlockSpec((hb, nc, LANES, LANES), lambda g: (g, 0, 0, 0))],
        out_shape=[jax.ShapeDtypeStruct((S, n_heads * LANES), F32), jax.ShapeDtypeStruct((S, n_heads * LANES), F32),
                   jax.ShapeDtypeStruct((n_heads, nc, LANES, LANES), F32)],
        scratch_shapes=[pltpu.VMEM((hb, LANES, LANES), F32)] + [pltpu.VMEM((hb, B_CHUNK, LANES), F32)] * 3,
        compiler_params=_cparams("parallel"),
    )(src, src, src, src, lb, norm_g.reshape(1, LANES))


def hgrn_bwd(src, lb, norm_g, o, states, dout, *, S, n_heads, hb, q_blk, f_blk, i_blk, g_blk, dout_blk, name):
    nc = S // B_CHUNK
    C, n_sub = B_CHUNK, B_CHUNK // B_SUB
    assert n_heads % hb == 0 and (q_blk % hb, f_blk % hb, i_blk % hb, g_blk % hb, dout_blk % hb) == (0, 0, 0, 0, 0)

    def body(qb_ref, fb_ref, ib_ref, gb_ref, lb_ref, ng_ref, o_ref, st_ref, dout_ref,
             dqb_ref, dfb_ref, dib_ref, dgb_ref, dlb_ref, dng_ref, dstate_ref, q_sh, k_sh, b_sh, dq_sh, dk_sh):
        @pl.when(pl.program_id(0) == 0)
        def _():
            dng_ref[...] = jnp.zeros_like(dng_ref)

        dstate_ref[...] = jnp.zeros_like(dstate_ref)
        tril = _tri(C, True)
        triu = _tri(C, False)
        ngv = ng_ref[...]
        srow = lax.broadcasted_iota(jnp.int32, (C, LANES), 0)
        lane = lax.broadcasted_iota(jnp.int32, (B_SUB, C), 1)
        trow = lax.broadcasted_iota(jnp.int32, (B_SUB, LANES), 0)
        causal = lax.broadcasted_iota(jnp.int32, (C, C), 1) <= lax.broadcasted_iota(jnp.int32, (C, C), 0)

        def one_head(j, c, rows, carry):
            dlog_carry, dlb_acc, dng_acc = carry
            cols = slice(j * LANES, (j + 1) * LANES)
            q_s, k_s, b_s, dq_s, dk_s = q_sh.at[j], k_sh.at[j], b_sh.at[j], dq_sh.at[j], dk_sh.at[j]
            lbv = lb_ref[j]
            qb, fb, gb = qb_ref[rows, cols], fb_ref[rows, cols], gb_ref[rows, cols]
            q, k, gate, sq, sf = _hgrn_gates(qb, fb, lbv)
            b = _dot_exact(tril, jnp.log(gate))
            vf = ib_ref[rows, cols]
            v = vf.astype(BF16)
            q_s[...], k_s[...], b_s[...] = q, k, b
            st = st_ref[j, c]
            dst = dstate_ref[j]

            ov = o_ref[rows, cols]
            dout = dout_ref[rows, cols]
            rstd = lax.rsqrt(jnp.mean(ov * ov, axis=-1, keepdims=True) + NORM_EPS)
            xhat = ov * rstd
            sg = _sigmoid(gb)
            dy = dout * (gb * sg)
            dgb_ref[rows, cols] = dout * (xhat * ngv) * (sg * (1.0 + gb * (1.0 - sg)))
            dng_acc = dng_acc + jnp.sum(dy * xhat, axis=0, keepdims=True)
            dxhat = dy * ngv
            do = rstd * (dxhat - xhat * jnp.mean(dxhat * xhat, axis=-1, keepdims=True))
            dob = do.astype(BF16)

            eb = jnp.exp(b)
            b_last = b_s[pl.ds(C - 1, 1), :]
            ebl = jnp.exp(b_last - b)
            a = _hgrn_intra_fwd(q_s, k_s, b_s)
            da = jnp.where(causal, _dot_exact(do, vf, trans_b=True), 0.0)
            dv = _dot_tn(a.astype(BF16), dob) + _dot_nt((k * ebl).astype(BF16), dst.astype(BF16))
            dq_s[...] = _dot_exact(do, st) * eb
            dk_s[...] = _dot_exact(vf, dst) * ebl
            dstate_ref[j] = dst * jnp.exp(b_last) + _dot_exact(do, q * eb, trans_a=True)
            b_all, k_all = b, k
            for i in range(n_sub):
                r0 = i * B_SUB
                blk = pl.ds(r0, B_SUB)
                qi, bi = q_s[blk, :], b_s[blk, :]
                da_i = da[r0:r0 + B_SUB, :]
                dq_i = jnp.zeros((B_SUB, LANES), F32)
                if i > 0:
                    ref_b = b_s[pl.ds(r0 - 1, 1), :]
                    eq = jnp.exp(bi - ref_b)
                    ek = jnp.where(srow < r0, jnp.exp(jnp.minimum(ref_b - b_all, 0.0)), 0.0)
                    da_off = jnp.where(lane < r0, da_i, 0.0)
                    dq_i = _dot_exact(da_off, k_all * ek) * eq
                    dk_s[...] += _dot_exact(da_off, qi * eq, trans_a=True) * ek
                for sl in range(B_SUB):
                    s = r0 + sl
                    e = jnp.where(trow >= sl, jnp.exp(jnp.minimum(bi - b_s[pl.ds(s, 1), :], 0.0)), 0.0)
                    dac = jnp.sum(jnp.where(lane == s, da_i, 0.0), axis=-1, keepdims=True)
                    dq_i = dq_i + dac * k_s[pl.ds(s, 1), :] * e
                    dk_s[pl.ds(s, 1), :] += jnp.sum(dac * qi * e, axis=0, keepdims=True)
                dq_s[blk, :] += dq_i
            dq, dk = dq_s[...], dk_s[...]
            db = q * dq - k * dk
            dlog = _dot_exact(triu, db) + dlog_carry
            dlog_carry = dlog_carry + jnp.sum(db, axis=0, keepdims=True)
            dgate = dlog / gate - dk
            dqb_ref[rows, cols] = dq * (B_DIM ** -0.5) * (sq * (1.0 + qb * (1.0 - sq)))
            dfb_ref[rows, cols] = dgate * (1.0 - lbv) * sf * (1.0 - sf)
            dib_ref[rows, cols] = dv
            dlb_acc = dlb_acc + jnp.sum(dgate * (1.0 - sf), axis=0, keepdims=True)
            return dlog_carry, dlb_acc, dng_acc

        def step(ci, carry):
            c = nc - 1 - ci
            rows = pl.ds(pl.multiple_of(c * C, C), C)
            return tuple(one_head(j, c, rows, carry[j]) for j in range(hb))

        z = jnp.zeros((1, LANES), F32)
        final = lax.fori_loop(0, nc, step, ((z, z, z),) * hb)
        for j in range(hb):
            dlb_ref[j] = final[j][1]
            dng_ref[...] += final[j][2]

    def col(off):
        return pl.BlockSpec((S, hb * LANES), lambda g: (0, off // hb + g), pipeline_mode=pl.Buffered(1))

    hv = pl.BlockSpec((hb, 1, LANES), lambda g: (g, 0, 0))
    vec = pl.BlockSpec((1, LANES), lambda g: (0, 0))
    full = jax.ShapeDtypeStruct((S, n_heads * LANES), F32)
    return pl.pallas_call(
        body, name=name, grid=(n_heads // hb,),
        in_specs=[col(q_blk), col(f_blk), col(i_blk), col(g_blk), hv, vec, col(0),
                  pl.BlockSpec((hb, nc, LANES, LANES), lambda g: (g, 0, 0, 0), pipeline_mode=pl.Buffered(1)),
                  col(dout_blk)],
        out_specs=[pl.BlockSpec((S, hb * LANES), lambda g: (0, g))] * 4 + [hv, vec],
        out_shape=[full, full, full, full, jax.ShapeDtypeStruct((n_heads, 1, LANES), F32),
                   jax.ShapeDtypeStruct((1, LANES), F32)],
        scratch_shapes=[pltpu.VMEM((hb, LANES, LANES), F32)] + [pltpu.VMEM((hb, B_CHUNK, LANES), F32)] * 5,
        compiler_params=_cparams("arbitrary"),
    )(src, src, src, src, lb, norm_g.reshape(1, LANES), o, states, dout)


def _pad_heads_cols(w, n_heads):
    lead = w.shape[:-1]
    w = w.reshape(lead + (n_heads, C_DIM))
    w = jnp.pad(w, [(0, 0)] * len(lead) + [(0, 0), (0, LANES - C_DIM)])
    return w.reshape(lead + (n_heads * LANES,))


def _unpad_heads_cols(w, n_heads):
    lead = w.shape[:-1]
    return w.reshape(lead + (n_heads, LANES))[..., :C_DIM].reshape(lead + (n_heads * C_DIM,))


def _pad_heads_rows(w, n_heads):
    w = jnp.pad(w.reshape(n_heads, C_DIM, w.shape[1]), [(0, 0), (0, LANES - C_DIM), (0, 0)])
    return w.reshape(n_heads * LANES, w.shape[2])


def _unpad_heads_rows(w, n_heads):
    return w.reshape(n_heads, LANES, w.shape[1])[:, :C_DIM].reshape(n_heads * C_DIM, w.shape[1])


def _lower_bounds(lb_raw):
    lb_soft = jax.nn.softmax(lb_raw.astype(F32), axis=0)
    return jnp.cumsum(lb_soft, axis=0) - lb_soft[0:1]


def _relu2(u):
    r = jnp.maximum(u, 0.0)
    return r * r


def local_step(x, target, P, hooks):
    S, D = x.shape
    depth = P["norm_mix_g"].shape[0]
    HA = D // 2 // A_DIM
    HB = HA
    HQ = D // C_DIM
    HKV = HQ // C_GROUP
    even_in = 7 * HA * LANES
    tabs_a = rope_tables(S, A_DIM)
    tabs_c = rope_tables(S, C_DIM)
    half_a, half_c = tabs_a[3], tabs_c[3]
    lbounds = _lower_bounds(P["hgrn_lb_raw"])
    a_in = 3 * HA * LANES
    a_kw = dict(n_heads=HA, group=1, hb=A_HEADS_PER_STEP, q_blk=0, k_blk=HA, v_blk=2 * HA, blk_per_row=3 * HA,
                scale=A_DIM ** -0.5, half=half_a)
    b_kw = dict(S=S, n_heads=HB, hb=B_HEADS_PER_STEP, q_blk=0, f_blk=HB, i_blk=2 * HB, g_blk=3 * HB)
    c_kw = dict(L=S, dil=1, n_heads=HQ, group=C_GROUP, hb=C_GROUP, q_blk=0, k_blk=HQ, v_blk=HQ + HKV,
                blk_per_row=HQ + 2 * HKV, max_dist=C_WINDOW - 1, scale=C_DIM ** -0.5, half=half_c)

    def a_tabs(dil):
        return tuple(t.reshape(S // dil, dil * LANES) for t in tabs_a[:3])

    saved = []
    for layer in range(depth):
        Wl = dict(hooks["weights_in"](layer, x))
        sv = {"x0": x, "W": Wl}
        idx = layer // 2
        h = rmsnorm_fwd(x, P["norm_mix_g"][layer], dep=Wl.pop("_dep", None), name="norm_mix_fwd")
        sv["h"] = h
        if layer % 2 == 0:
            qkv = matmul(h, Wl["even_w_in"], b_cols=(0, a_in), out_dtypes=(BF16,), name="even_in_fwd_a")
            proj = matmul(h, Wl["even_w_in"], b_cols=(a_in, even_in), out_dtypes=(F32,), name="even_in_fwd_b")
            parts = []
            sv["qkv"] = {dil: qkv.reshape(S // dil, dil * a_in) for _, dil in A_BRANCHES}
            for window, dil in A_BRANCHES:
                L = S // dil
                res = banded_fwd(sv["qkv"][dil], a_tabs(dil), L=L, dil=dil, max_dist=window // dil, normalize=False,
                                 name=f"dilated_fwd_d{dil}", **a_kw)
                parts.append(tuple(r.reshape(S, HA * LANES) for r in res))
            oa, lse = merge_branches(parts, name="dilated_merge")
            hooks["mid"](layer, oa)
            lb_e = lbounds[idx].reshape(HB, 1, LANES)
            ob, o_raw, states = hgrn_fwd(proj, lb_e, P["hgrn_norm_g"][idx], name="hgrn_fwd", **b_kw)
            Wl.update(hooks["weights_rest"](layer, ob))
            mixed = jnp.concatenate([oa, ob], axis=1).astype(BF16)
            sv.update(proj=proj, oa=oa, lse=lse, o_raw=o_raw, states=states, mixed=mixed, lb=lb_e)
            x = matmul(mixed, Wl["even_w_out"], extras=(x,), epilogue=lambda acc, r: (acc + r,),
                       dep=Wl.pop("_dep", None), out_dtypes=(F32,), name="even_out_fwd")
        else:
            proj = matmul(h, Wl["odd_w_qkv"], extras=(P["odd_b_qkv"][idx].reshape(1, -1),),
                          epilogue=lambda acc, b: (acc + b,), out_dtypes=(F32,), name="odd_qkv_fwd")
            sink = jnp.broadcast_to(P["odd_sinks"][idx].reshape(HQ, 1, 1), (HQ, 1, LANES))
            o, lse = banded_fwd(proj, tabs_c[:3], sink=sink, normalize=True, name="swa_fwd", **c_kw)
            hooks["mid"](layer, o)
            Wl.update(hooks["weights_rest"](layer, o))
            sv.update(proj=proj, o=o, lse=lse, sink=sink)
            x = matmul(o, Wl["odd_w_o"], extras=(P["odd_b_o"][idx].reshape(1, D), x),
                       epilogue=lambda acc, b, r: (acc + b + r,), out_dtypes=(F32,), name="odd_out_fwd")
        sv["x1"] = x
        h2 = rmsnorm_fwd(x, P["norm_mlp_g"][layer], dep=hooks["after_mixer"](layer, x), name="norm_mlp_fwd")
        u = matmul(h2, Wl["mlp_w1"], out_dtypes=(BF16,), name="mlp_up_fwd")
        x = matmul(u, Wl["mlp_w2"], a_fn=_relu2, extras=(x,), epilogue=lambda acc, r: (acc + r,),
                   out_dtypes=(F32,), name="mlp_down_fwd")
        sv.update(h2=h2, u=u)
        saved.append(sv)

    loss, dx, dgf = final_norm_loss(x, P["final_norm_g"], target, name="final_norm_loss")

    G = {k: [None] * depth for k in ("norm_mix_g", "norm_mlp_g")}
    for k in ("hgrn_lb", "hgrn_norm_g"):
        G[k] = [None] * ((depth + 1) // 2)
    for k in ("odd_w_qkv", "odd_b_qkv", "odd_sinks", "odd_w_o", "odd_b_o"):
        G[k] = [None] * (depth // 2)
    G["final_norm_g"] = dgf.reshape(D)
    B = {}

    def wgrad(a, d, key, idx, name, a_fn=None):
        M, N = a.shape[1], d.shape[1]
        into = (grad_buffer_shape((M, N), SHARDING[key]), SHARDING[key])
        B[key, idx] = matmul(a, d, trans_a=True, a_fn=a_fn, out_dtypes=(BF16,), into=into, name=name)

    dep = None
    for layer in reversed(range(depth)):
        sv = saved[layer]
        Wl = sv["W"]
        idx = layer // 2
        du = matmul(dx, Wl["mlp_w2"], trans_b=True, extras=(sv["u"],), dep=dep,
                    epilogue=lambda acc, u: (acc * (2.0 * jnp.maximum(u.astype(F32), 0.0)),),
                    out_dtypes=(BF16,), name="mlp_down_bwd_x")
        wgrad(sv["u"], dx, "mlp_w2", layer, "mlp_down_bwd_w", a_fn=_relu2)
        wgrad(sv["h2"], du, "mlp_w1", layer, "mlp_up_bwd_w")
        dh2 = matmul(du, Wl["mlp_w1"], trans_b=True, out_dtypes=(F32,), name="mlp_up_bwd_x")
        dx, dg = rmsnorm_bwd(sv["x1"], P["norm_mlp_g"][layer], dh2, dx, name="norm_mlp_bwd")
        G["norm_mlp_g"][layer] = dg.reshape(D)
        dep = hooks["grads_ready"]([("mlp_w2", layer), ("mlp_w1", layer)], B, dx)
        if layer % 2 == 0:
            dmixed = matmul(dx, Wl["even_w_out"], trans_b=True, dep=dep, out_dtypes=(F32,), name="even_out_bwd_x")
            wgrad(sv["mixed"], dx, "even_w_out", idx, "even_out_bwd_w")
            dep = hooks["bwd_mid"](dmixed)
            acc = None
            for window, dil in A_BRANCHES:
                L = S // dil
                rs = lambda a: a.reshape(L, -1)
                acc = banded_bwd(sv["qkv"][dil], a_tabs(dil), rs(sv["oa"]), rs(sv["lse"]), rs(dmixed),
                                 L=L, dil=dil, max_dist=window // dil, do_bpr=2 * HA,
                                 acc=None if acc is None else tuple(rs(a) for a in acc),
                                 dep=dep if acc is None else None, name=f"dilated_bwd_d{dil}", **a_kw)
                acc = tuple(a.reshape(S, HA * LANES) for a in acc)
            dqb, dfb, dib, dgb, dlb, dng = hgrn_bwd(sv["proj"], sv["lb"], P["hgrn_norm_g"][idx], sv["o_raw"],
                                                    sv["states"], dmixed, dout_blk=HA, name="hgrn_bwd", **b_kw)
            G["hgrn_lb"][idx] = dlb.reshape(HB * LANES)
            G["hgrn_norm_g"][idx] = dng.reshape(LANES)
            dproj = jnp.concatenate(list(acc) + [dqb, dfb, dib, dgb], axis=1).astype(BF16)
            wgrad(sv["h"], dproj, "even_w_in", idx, "even_in_bwd_w")
            dh = matmul(dproj, Wl["even_w_in"], trans_b=True, out_dtypes=(F32,), name="even_in_bwd_x")
        else:
            do = matmul(dx, Wl["odd_w_o"], trans_b=True, dep=dep, out_dtypes=(F32,), name="odd_out_bwd_x")
            G["odd_b_o"][idx] = colsum(dx, name="odd_out_bwd_b").reshape(D)
            G["odd_w_o"][idx] = matmul(sv["o"], dx, trans_a=True, out_dtypes=(BF16,), name="odd_out_bwd_w")
            dq, dk, dv, dsink = banded_bwd(sv["proj"], tabs_c[:3], sv["o"], sv["lse"], do, sink=sv["sink"],
                                           do_bpr=HQ, dep=hooks["bwd_mid"](do), name="swa_bwd", **c_kw)
            G["odd_sinks"][idx] = dsink[:, 0, 0]
            dproj = jnp.concatenate([dq, dk, dv], axis=1)
            G["odd_b_qkv"][idx] = colsum(dproj, name="odd_qkv_bwd_b").reshape(-1)
            dproj = dproj.astype(BF16)
            G["odd_w_qkv"][idx] = matmul(sv["h"], dproj, trans_a=True, out_dtypes=(BF16,), name="odd_qkv_bwd_w")
            dh = matmul(dproj, Wl["odd_w_qkv"], trans_b=True, out_dtypes=(F32,), name="odd_qkv_bwd_x")
            HQ2 = HQ + 2 * HKV
            B["odd_w_qkv", idx] = full_to_buffer(_unpad_heads_cols(G["odd_w_qkv"][idx], HQ2), "cols")
            B["odd_w_o", idx] = full_to_buffer(_unpad_heads_rows(G["odd_w_o"][idx], HQ), "rows")
        dx, dg = rmsnorm_bwd(sv["x0"], P["norm_mix_g"][layer], dh, dx, name="norm_mix_bwd")
        G["norm_mix_g"][layer] = dg.reshape(D)
        mixer = ("even_w_out", "even_w_in") if layer % 2 == 0 else ("odd_w_o", "odd_w_qkv")
        dep = hooks["grads_ready"]([(k, idx) for k in mixer], B, dx)

    del G["odd_w_qkv"], G["odd_w_o"]
    grads = {k: jnp.stack(v) if isinstance(v, list) else v for k, v in G.items()}
    _, lb_vjp = jax.vjp(_lower_bounds, P["hgrn_lb_raw"])
    grads["hgrn_lb_raw"] = lb_vjp(grads.pop("hgrn_lb"))[0]
    grads["odd_b_qkv"] = _unpad_heads_cols(grads["odd_b_qkv"], HQ + 2 * HKV)
    return loss, dx, grads, B


def full_to_buffer(full, sharding):
    M, N = full.shape
    if sharding == "cols":
        return jnp.transpose(full.reshape(2, M // 2, 4, N // 4), (2, 0, 1, 3))
    return full.reshape(4, 2, M // 8, N)


MESH = pl.DeviceIdType.MESH
ANY = pl.BlockSpec(memory_space=pl.ANY)
SHARDING = {"even_w_in": "cols", "even_w_out": "rows", "odd_w_qkv": "cols", "odd_w_o": "rows",
            "mlp_w1": "cols", "mlp_w2": "rows"}
BIG = tuple(SHARDING)


def _mesh_pos():
    return lax.axis_index("x"), lax.axis_index("y"), lax.axis_index("c")


def _other_chips(x, y):
    return [(1 - x, y), (x, 1 - y), (1 - x, 1 - y)]


def cast_into_full(w, layer, sharding, s, *, name, tr=256):
    n, R, C = w.shape
    tr = _tile(R, tr)
    nr = R // tr

    def body(s_ref, w_ref, o_ref):
        o_ref[...] = w_ref[...].astype(BF16)

    if sharding == "cols":
        full, out_map = (R, 4 * C), (lambda i, s_ref: (i, s_ref[0]))
    else:
        full, out_map = (4 * R, C), (lambda i, s_ref: (s_ref[0] * nr + i, 0))
    grid_spec = pltpu.PrefetchScalarGridSpec(
        num_scalar_prefetch=1, grid=(nr,),
        in_specs=[pl.BlockSpec((None, tr, C), lambda i, s_ref: (layer, i, 0))],
        out_specs=pl.BlockSpec((tr, C), out_map))
    return pl.pallas_call(
        body, name=name, grid_spec=grid_spec, out_shape=jax.ShapeDtypeStruct(full, BF16),
        compiler_params=_cparams("parallel"),
    )(s.reshape(1).astype(jnp.int32), w)


HBM = pl.BlockSpec(memory_space=pltpu.HBM)
SEM = pl.BlockSpec(memory_space=pltpu.SEMAPHORE)
EFFECT = pltpu.SideEffectType.DATAFLOW_SIDE_EFFECTING


def _weight_window(ref, sharding, s, hf):
    M, N = ref.shape
    if sharding == "cols":
        rh, C = M // 2, N // 4
        return ref.at[pl.ds(hf * rh, rh), pl.ds(s * C, C)]
    rh = M // 8
    return ref.at[pl.ds((2 * s + hf) * rh, rh), :]


def _in_hbm(a):
    return pltpu.with_memory_space_constraint(a, pltpu.HBM)


def gather_start(groups, shardings, after, *, name):
    sizes = [len(g) for g in groups]
    flat = [m for g in groups for m in g]
    flat_sh = [sh for g in shardings for sh in g]
    T, ng = len(flat), len(groups)

    def body(*refs):
        sems = refs[T + 1:T + 1 + 6 * ng]
        thru = refs[T + 1 + 6 * ng:2 * T + 1 + 6 * ng]
        token = refs[2 * T + 1 + 6 * ng]
        x, y, c = _mesh_pos()
        pos = 0
        for g in range(ng):
            for t in range(sizes[g]):
                w = _weight_window(thru[pos], flat_sh[pos], 2 * x + y, c)
                for j, (cx, cy) in enumerate(_other_chips(x, y)):
                    pltpu.make_async_remote_copy(src_ref=w, dst_ref=w, send_sem=sems[6 * g + j],
                                                 recv_sem=sems[6 * g + 3 + j], device_id=(cx, cy, c),
                                                 device_id_type=MESH).start()
                pos += 1
        token[...] = jnp.zeros_like(token)

    outs = pl.pallas_call(
        body, name=name,
        in_specs=[HBM] * T + [ANY],
        out_specs=[SEM] * (6 * ng) + [HBM] * T + [pl.BlockSpec(memory_space=pltpu.VMEM)],
        out_shape=[pltpu.SemaphoreType.DMA(())] * (6 * ng) + [pltpu.HBM(m.shape, m.dtype) for m in flat]
        + [jax.ShapeDtypeStruct((8, LANES), F32)],
        input_output_aliases={t: 6 * ng + t for t in range(T)},
        compiler_params=pltpu.CompilerParams(has_side_effects=EFFECT),
    )(*[_in_hbm(m) for m in flat], after)
    res, pos = [], 6 * ng
    for g in range(ng):
        res.append((list(outs[6 * g:6 * g + 6]), list(outs[pos:pos + sizes[g]])))
        pos += sizes[g]
    return res, outs[-1]


def gather_forward(mats, shardings, sems, after, *, name):
    T = len(mats)

    def body(*refs):
        sems1 = refs[T:T + 6]
        sems2 = refs[T + 7:T + 13]
        thru = refs[T + 13:2 * T + 13]
        token = refs[2 * T + 13]
        x, y, c = _mesh_pos()
        chips = _other_chips(x, y)
        for t in range(T):
            own = _weight_window(thru[t], shardings[t], 2 * x + y, c)
            for j, (cx, cy) in enumerate(chips):
                landed = _weight_window(thru[t], shardings[t], 2 * cx + cy, c)
                first = pltpu.make_async_remote_copy(src_ref=own, dst_ref=landed, send_sem=sems1[j],
                                                     recv_sem=sems1[3 + j], device_id=(cx, cy, c),
                                                     device_id_type=MESH)
                first.wait_recv()
                first.wait_send()
        for t in range(T):
            for j, (cx, cy) in enumerate(chips):
                landed = _weight_window(thru[t], shardings[t], 2 * cx + cy, c)
                pltpu.make_async_remote_copy(src_ref=landed, dst_ref=landed, send_sem=sems2[j],
                                             recv_sem=sems2[3 + j], device_id=(x, y, 1 - c),
                                             device_id_type=MESH).start()
        token[...] = jnp.zeros_like(token)

    outs = pl.pallas_call(
        body, name=name,
        in_specs=[HBM] * T + [SEM] * 6 + [ANY],
        out_specs=[SEM] * 6 + [HBM] * T + [pl.BlockSpec(memory_space=pltpu.VMEM)],
        out_shape=[pltpu.SemaphoreType.DMA(())] * 6 + [pltpu.HBM(m.shape, m.dtype) for m in mats]
        + [jax.ShapeDtypeStruct((8, LANES), F32)],
        input_output_aliases={t: 6 + t for t in range(T)},
        compiler_params=pltpu.CompilerParams(has_side_effects=EFFECT),
    )(*mats, *sems, after)
    return list(outs[:6]), list(outs[6:6 + T]), outs[-1]


def gather_finish(mats, shardings, sems, after, *, name):
    T = len(mats)

    def body(*refs):
        sems2 = refs[T:T + 6]
        thru = refs[T + 7:2 * T + 7]
        x, y, c = _mesh_pos()
        for t in range(T):
            for j, (cx, cy) in enumerate(_other_chips(x, y)):
                sent = _weight_window(thru[t], shardings[t], 2 * cx + cy, c)
                other = _weight_window(thru[t], shardings[t], 2 * cx + cy, 1 - c)
                cp = pltpu.make_async_remote_copy(src_ref=sent, dst_ref=other, send_sem=sems2[j],
                                                  recv_sem=sems2[3 + j], device_id=(x, y, 1 - c),
                                                  device_id_type=MESH)
                cp.wait_recv()
                cp.wait_send()

    outs = pl.pallas_call(
        body, name=name,
        in_specs=[HBM] * T + [SEM] * 6 + [ANY],
        out_specs=[HBM] * T,
        out_shape=[pltpu.HBM(m.shape, m.dtype) for m in mats],
        input_output_aliases={t: t for t in range(T)},
        compiler_params=pltpu.CompilerParams(has_side_effects=EFFECT),
    )(*mats, *sems, after)
    return list(outs)


def pair_sum(buf, got, c, *, name, tr=512):
    _, _, n, rh, C = buf.shape
    tr = _tile(rh, tr)

    def body(c_ref, a_ref, b_ref, o_ref):
        o_ref[...] = (a_ref[...].astype(F32) + b_ref[...].astype(F32)).astype(o_ref.dtype)

    grid_spec = pltpu.PrefetchScalarGridSpec(
        num_scalar_prefetch=1, grid=(4, n, rh // tr),
        in_specs=[pl.BlockSpec((None, None, None, tr, C), lambda s, l, i, c_ref: (s, c_ref[0], l, i, 0)),
                  pl.BlockSpec((None, None, None, tr, C), lambda s, l, i, c_ref: (s, 0, l, i, 0))],
        out_specs=pl.BlockSpec((None, None, tr, C), lambda s, l, i, c_ref: (s, l, i, 0)))
    return pl.pallas_call(
        body, name=name, grid_spec=grid_spec, out_shape=jax.ShapeDtypeStruct((4, n, rh, C), BF16),
        compiler_params=_cparams("parallel", "parallel", "parallel"),
    )(c.reshape(1).astype(jnp.int32), buf, got)


def exchange_start(bufs, after, *, name):
    T = len(bufs)
    lands = [_in_hbm(lax.empty((4, 1) + b.shape[2:], b.dtype)) for b in bufs]

    def body(*refs):
        send_sem, recv_sem = refs[2 * T + 1], refs[2 * T + 2]
        src, dst = refs[2 * T + 3:3 * T + 3], refs[3 * T + 3:4 * T + 3]
        token = refs[4 * T + 3]
        x, y, c = _mesh_pos()
        for t in range(T):
            pltpu.make_async_remote_copy(src_ref=src[t].at[:, pl.ds(1 - c, 1)], dst_ref=dst[t], send_sem=send_sem,
                                         recv_sem=recv_sem, device_id=(x, y, 1 - c), device_id_type=MESH).start()
        token[...] = jnp.zeros_like(token)

    outs = pl.pallas_call(
        body, name=name,
        in_specs=[HBM] * (2 * T) + [ANY],
        out_specs=[SEM] * 2 + [HBM] * (2 * T) + [pl.BlockSpec(memory_space=pltpu.VMEM)],
        out_shape=[pltpu.SemaphoreType.DMA(())] * 2 + [pltpu.HBM(a.shape, a.dtype) for a in list(bufs) + lands]
        + [jax.ShapeDtypeStruct((8, LANES), F32)],
        input_output_aliases={t: 2 + t for t in range(2 * T)},
        compiler_params=pltpu.CompilerParams(has_side_effects=EFFECT),
    )(*[_in_hbm(b) for b in bufs], *lands, after)
    return list(outs[:2]), list(outs[2:2 + T]), list(outs[2 + T:2 + 2 * T]), outs[-1]


def exchange_wait(bufs, lands, sems, after, *, name):
    T = len(bufs)

    def body(*refs):
        send_sem, recv_sem = refs[2 * T], refs[2 * T + 1]
        src, dst = refs[2 * T + 3:3 * T + 3], refs[3 * T + 3:4 * T + 3]
        x, y, c = _mesh_pos()
        for t in range(T):
            cp = pltpu.make_async_remote_copy(src_ref=src[t].at[:, pl.ds(1 - c, 1)], dst_ref=dst[t],
                                              send_sem=send_sem, recv_sem=recv_sem, device_id=(x, y, 1 - c),
                                              device_id_type=MESH)
            cp.wait_recv()
            cp.wait_send()

    outs = pl.pallas_call(
        body, name=name,
        in_specs=[HBM] * (2 * T) + [SEM] * 2 + [ANY],
        out_specs=[HBM] * (2 * T),
        out_shape=[pltpu.HBM(a.shape, a.dtype) for a in list(bufs) + list(lands)],
        input_output_aliases={t: t for t in range(2 * T)},
        compiler_params=pltpu.CompilerParams(has_side_effects=EFFECT),
    )(*bufs, *lands, *sems, after)
    return list(outs[:T]), list(outs[T:])


def scatter_start(sums, *, name):
    T = len(sums)
    lands = [_in_hbm(lax.empty((3,) + s.shape[1:], s.dtype)) for s in sums]

    def body(*refs):
        sems = refs[2 * T:2 * T + 6]
        src, dst = refs[2 * T + 6:3 * T + 6], refs[3 * T + 6:4 * T + 6]
        token = refs[4 * T + 6]
        x, y, c = _mesh_pos()
        for t in range(T):
            for j, (cx, cy) in enumerate(_other_chips(x, y)):
                pltpu.make_async_remote_copy(src_ref=src[t].at[pl.ds(2 * cx + cy, 1)], dst_ref=dst[t].at[pl.ds(j, 1)],
                                             send_sem=sems[j], recv_sem=sems[3 + j], device_id=(cx, cy, c),
                                             device_id_type=MESH).start()
        token[...] = jnp.zeros_like(token)

    outs = pl.pallas_call(
        body, name=name,
        in_specs=[HBM] * (2 * T),
        out_specs=[SEM] * 6 + [HBM] * (2 * T) + [pl.BlockSpec(memory_space=pltpu.VMEM)],
        out_shape=[pltpu.SemaphoreType.DMA(())] * 6 + [pltpu.HBM(a.shape, a.dtype) for a in list(sums) + lands]
        + [jax.ShapeDtypeStruct((8, LANES), F32)],
        input_output_aliases={t: 6 + t for t in range(2 * T)},
        compiler_params=pltpu.CompilerParams(has_side_effects=EFFECT),
    )(*[_in_hbm(s) for s in sums], *lands)
    return list(outs[:6]), list(outs[6:6 + T]), list(outs[6 + T:6 + 2 * T]), outs[-1]


def scatter_wait(sums, lands, sems, after, *, name):
    T = len(sums)

    def body(*refs):
        sem_refs = refs[2 * T:2 * T + 6]
        src, dst = refs[2 * T + 7:3 * T + 7], refs[3 * T + 7:4 * T + 7]
        x, y, c = _mesh_pos()
        for t in range(T):
            for j, (cx, cy) in enumerate(_other_chips(x, y)):
                cp = pltpu.make_async_remote_copy(src_ref=src[t].at[pl.ds(2 * cx + cy, 1)],
                                                  dst_ref=dst[t].at[pl.ds(j, 1)], send_sem=sem_refs[j],
                                                  recv_sem=sem_refs[3 + j], device_id=(cx, cy, c),
                                                  device_id_type=MESH)
                cp.wait_recv()
                cp.wait_send()

    outs = pl.pallas_call(
        body, name=name,
        in_specs=[HBM] * (2 * T) + [SEM] * 6 + [ANY],
        out_specs=[HBM] * (2 * T),
        out_shape=[pltpu.HBM(a.shape, a.dtype) for a in list(sums) + list(lands)],
        input_output_aliases={t: t for t in range(2 * T)},
        compiler_params=pltpu.CompilerParams(has_side_effects=EFFECT),
    )(*sums, *lands, *sems, after)
    return list(outs[:T]), list(outs[T:])


def owner_sum(sums, got, s, c, grad, layer, *, name, tr=512):
    rh, C = sums.shape[2:]
    tr = _tile(rh, tr)
    nr = rh // tr

    def body(sc_ref, a_ref, b0_ref, b1_ref, b2_ref, g_ref, o_ref):
        o_ref[...] = ((a_ref[...].astype(F32) + b0_ref[...].astype(F32)) + b1_ref[...].astype(F32)) \
            + b2_ref[...].astype(F32)

    def got_spec(j):
        return pl.BlockSpec((None, None, tr, C), lambda i, sc_ref: (j, 0, i, 0))

    grid_spec = pltpu.PrefetchScalarGridSpec(
        num_scalar_prefetch=1, grid=(nr,),
        in_specs=[pl.BlockSpec((None, None, tr, C), lambda i, sc_ref: (sc_ref[0], 0, i, 0)),
                  got_spec(0), got_spec(1), got_spec(2), ANY],
        out_specs=pl.BlockSpec((None, tr, C), lambda i, sc_ref: (layer, sc_ref[1] * nr + i, 0)))
    return pl.pallas_call(
        body, name=name, grid_spec=grid_spec, out_shape=jax.ShapeDtypeStruct(grad.shape, F32),
        input_output_aliases={5: 0},
        compiler_params=_cparams("parallel"),
    )(jnp.stack([s, c]).astype(jnp.int32), sums, got, got, got, grad)


def join_halves(grads, *, name):
    T = len(grads)

    def body(*refs):
        out_refs = refs[T:2 * T]
        send_sems, recv_sems = refs[2 * T:]
        x, y, c = _mesh_pos()

        def win(t, hf):
            rh = grads[t].shape[1] // 2
            return out_refs[t].at[:, pl.ds(hf * rh, rh), :]

        def remote(t, w):
            return pltpu.make_async_remote_copy(src_ref=w, dst_ref=w, send_sem=send_sems.at[t],
                                                recv_sem=recv_sems.at[t], device_id=(x, y, 1 - c),
                                                device_id_type=MESH)

        cps = [remote(t, win(t, c)) for t in range(T)]
        for cp in cps:
            cp.start()
        for t in range(T):
            remote(t, win(t, 1 - c)).wait_recv()
        for cp in cps:
            cp.wait_send()

    return pl.pallas_call(
        body, name=name, in_specs=[ANY] * T, out_specs=[ANY] * T,
        out_shape=[jax.ShapeDtypeStruct(g.shape, g.dtype) for g in grads],
        input_output_aliases={t: t for t in range(T)},
        scratch_shapes=[pltpu.SemaphoreType.DMA((T,)), pltpu.SemaphoreType.DMA((T,))],
    )(*grads)


def allreduce_small(v, *, name):
    rows = v.shape[0]

    def body(v_ref, out_ref, buf, send_sems, recv_sems):
        x, y, c = _mesh_pos()
        me = 4 * x + 2 * y + c
        flips = [(dx, dy, dc) for dx in (0, 1) for dy in (0, 1) for dc in (0, 1)][1:]

        def peer(f):
            return tuple(1 - p if d else p for d, p in zip(f, (x, y, c)))

        cps = []
        for k, f in enumerate(flips):
            px, py, pc = peer(f)
            cps.append(pltpu.make_async_remote_copy(src_ref=v_ref, dst_ref=buf.at[me], send_sem=send_sems.at[k],
                                                    recv_sem=recv_sems.at[k], device_id=(px, py, pc),
                                                    device_id_type=MESH))
        for cp in cps:
            cp.start()
        buf[me] = v_ref[...]
        for k, f in enumerate(flips):
            px, py, pc = peer(f)
            slot = buf.at[4 * px + 2 * py + pc]
            pltpu.make_async_remote_copy(src_ref=slot, dst_ref=slot, send_sem=send_sems.at[k],
                                         recv_sem=recv_sems.at[k], device_id=(px, py, pc),
                                         device_id_type=MESH).wait_recv()
        for cp in cps:
            cp.wait_send()
        acc = buf[0]
        for i in range(1, 8):
            acc = acc + buf[i]
        out_ref[...] = acc

    vm = pl.BlockSpec(memory_space=pltpu.VMEM)
    return pl.pallas_call(
        body, name=name, in_specs=[vm], out_specs=vm,
        out_shape=jax.ShapeDtypeStruct(v.shape, F32),
        scratch_shapes=[pltpu.VMEM((8, rows, LANES), F32), pltpu.SemaphoreType.DMA((7,)),
                        pltpu.SemaphoreType.DMA((7,))],
    )(v)


def _adam_math(w, m, v, g):
    m = ADAM_B1 * m + (1.0 - ADAM_B1) * g
    v = ADAM_B2 * v + (1.0 - ADAM_B2) * (g * g)
    m_hat = m / (1.0 - ADAM_B1 ** ADAM_STEP)
    v_hat = v / (1.0 - ADAM_B2 ** ADAM_STEP)
    delta = -ADAM_LR * (m_hat / (jnp.sqrt(v_hat) + ADAM_EPS) + ADAM_WD * w)
    return delta, m, v


def adamw(w, m, v, g, *, name, tr=256):
    def body(w_ref, m_ref, v_ref, g_ref, d_ref, nm_ref, nv_ref, go_ref):
        gv = g_ref[...]
        d, nm, nv = _adam_math(w_ref[...], m_ref[...], v_ref[...], gv)
        d_ref[...] = d
        nm_ref[...] = nm
        nv_ref[...] = nv
        go_ref[...] = gv

    shape = jax.ShapeDtypeStruct(w.shape, F32)
    if w.ndim == 2:
        return pl.pallas_call(body, name=name, out_shape=[shape] * 4)(w, m, v, g)
    n, R, C = w.shape
    tr = _tile(R, tr)
    spec = pl.BlockSpec((None, tr, C), lambda l, i: (l, i, 0))
    return pl.pallas_call(
        body, name=name, grid=(n, R // tr), in_specs=[spec] * 4, out_specs=[spec] * 4, out_shape=[shape] * 4,
        compiler_params=_cparams("parallel", "parallel"),
    )(w, m, v, g)


def _pack(parts):
    flat = jnp.concatenate([p.reshape(-1).astype(F32) for p in parts])
    size = -(-flat.shape[0] // (8 * LANES)) * (8 * LANES)
    return jnp.pad(flat, (0, size - flat.shape[0])).reshape(size // LANES, LANES)


def _unpack(block, shapes):
    flat = block.reshape(-1)
    out, pos = [], 0
    for shp in shapes:
        size = int(np.prod(shp))
        out.append(flat[pos:pos + size].reshape(shp))
        pos += size
    return out


SMALL = ("norm_mix_g", "norm_mlp_g", "final_norm_g", "hgrn_lb_raw", "hgrn_norm_g", "odd_sinks")
WEIGHTS = ("norm_mix_g", "norm_mlp_g", "final_norm_g", "even_w_in", "even_w_out", "hgrn_lb_raw", "hgrn_norm_g",
           "odd_w_qkv", "odd_b_qkv", "odd_sinks", "odd_w_o", "odd_b_o", "mlp_w1", "mlp_w2")


def kernel(x, norm_mix_g, norm_mlp_g, final_norm_g, even_w_in, even_w_out, hgrn_lb_raw, hgrn_norm_g, odd_w_qkv, odd_b_qkv, odd_sinks, odd_w_o, odd_b_o, mlp_w1, mlp_w2, loss_target, m_norm_mix_g, m_norm_mlp_g, m_final_norm_g, m_even_w_in, m_even_w_out, m_hgrn_lb_raw, m_hgrn_norm_g, m_odd_w_qkv, m_odd_b_qkv, m_odd_sinks, m_odd_w_o, m_odd_b_o, m_mlp_w1, m_mlp_w2, v_norm_mix_g, v_norm_mlp_g, v_final_norm_g, v_even_w_in, v_even_w_out, v_hgrn_lb_raw, v_hgrn_norm_g, v_odd_w_qkv, v_odd_b_qkv, v_odd_sinks, v_odd_w_o, v_odd_b_o, v_mlp_w1, v_mlp_w2):
    args = locals()
    W = {k: args[k] for k in WEIGHTS}
    M = {k: args["m_" + k] for k in WEIGHTS}
    V = {k: args["v_" + k] for k in WEIGHTS}
    _, S, D = x.shape
    depth = norm_mix_g.shape[0]
    HQ = D // C_DIM
    HKV = HQ // C_GROUP
    xi, yi, ci = _mesh_pos()
    shard = 2 * xi + yi

    n_odd, bq = odd_b_qkv.shape
    bo = odd_b_o.shape[1]
    keep = (ci == 0).astype(F32)
    pieces = [lax.dynamic_update_slice(jnp.zeros((n_odd, 4 * bq), F32), odd_b_qkv * keep, (0, shard * bq)),
              lax.dynamic_update_slice(jnp.zeros((n_odd, 4 * bo), F32), odd_b_o * keep, (0, shard * bo))]
    biases = allreduce_small(_pack(pieces), name="gather_biases")
    b_qkv_full, b_o_full = _unpack(biases, [(n_odd, 4 * bq), (n_odd, 4 * bo)])
    P = {k: W[k] for k in SMALL}
    P.update(odd_b_qkv=_pad_heads_cols(b_qkv_full, HQ + 2 * HKV), odd_b_o=b_o_full)

    def layer_keys(layer):
        mixer = ("even_w_in", "even_w_out") if layer % 2 == 0 else ("odd_w_qkv", "odd_w_o")
        return [(k, layer // 2) for k in mixer] + [("mlp_w1", layer), ("mlp_w2", layer)]

    keys = [layer_keys(0)[:1], layer_keys(0)[1:]] + [layer_keys(layer) for layer in range(1, depth)]
    shardings = [[SHARDING[k] for k, _ in ks] for ks in keys]
    started, forwarded = {}, {}

    def start_groups(gs, after):
        mats = [[cast_into_full(W[k], i, SHARDING[k], shard, name="cast_" + k) for k, i in keys[g]] for g in gs]
        res, token = gather_start(mats, [shardings[g] for g in gs], after, name="gather_start")
        started.update(zip(gs, res))
        return token

    dep0 = start_groups(list(range(len(keys))), biases)

    def forward_group(g, after):
        sems, mats = started[g]
        forwarded[g] = gather_forward(mats, shardings[g], sems, after, name="gather_forward")
        return forwarded[g][2]

    def finish_group(g, after):
        sems, mats, _ = forwarded[g]
        mats = gather_finish(mats, shardings[g], sems, after, name="gather_finish")
        Wl = {k: m for (k, _), m in zip(keys[g], mats)}
        if "odd_w_qkv" in Wl:
            Wl["odd_w_qkv"] = _pad_heads_cols(Wl["odd_w_qkv"], HQ + 2 * HKV)
            Wl["odd_w_o"] = _pad_heads_rows(Wl["odd_w_o"], HQ)
        return Wl

    def weights_in(layer, x_in):
        if layer == 0:
            forward_group(0, dep0)
        return finish_group(layer + 1 if layer else 0, x_in)

    def mid(layer, a):
        if layer == 0:
            forward_group(1, a)

    def weights_rest(layer, a):
        return finish_group(1, a) if layer == 0 else {}

    def after_mixer(layer, x1):
        return forward_group(layer + 2, x1) if layer + 1 < depth else None

    reduced = {k: lax.empty(W[k].shape, F32) for k in BIG}
    exchanging, scattering = [], []

    def settle(entry, after):
        kis, (sems, sums, lands, _) = entry
        sums, lands = scatter_wait(sums, lands, sems, after, name="scatter_wait")
        for (k, i), s, g in zip(kis, sums, lands):
            reduced[k] = owner_sum(s, g, shard, ci, reduced[k], i, name="owner_sum_" + k)

    def scatter_exchanged(after):
        for kis, (sems, bufs, lands, _) in exchanging:
            bufs, lands = exchange_wait(bufs, lands, sems, after, name="exchange_wait")
            sums = [pair_sum(b, g, ci, name="pair_sum_" + k) for (k, _), b, g in zip(kis, bufs, lands)]
            scattering.append((kis, scatter_start(sums, name="scatter_start")))
        exchanging.clear()

    def grads_ready(kis, B, after):
        scatter_exchanged(after)
        while len(scattering) > MAX_SCATTERS_IN_FLIGHT:
            settle(scattering.pop(0), after)
        bufs = [B[ki].reshape((4, 2, 1) + B[ki].shape[2:]) for ki in kis]
        order = scattering[-1][1][3] if scattering else after
        exchanging.append((kis, exchange_start(bufs, order, name="exchange_start")))
        return exchanging[-1][1][3]

    def bwd_mid(after):
        scatter_exchanged(after)
        return scattering[-1][1][3] if scattering else None

    hooks = dict(weights_in=weights_in, mid=mid, weights_rest=weights_rest, after_mixer=after_mixer,
                 grads_ready=grads_ready, bwd_mid=bwd_mid)
    loss, dx, G, B = local_step(x[0], loss_target[0], P, hooks)
    scatter_exchanged(dx)
    last = scattering.pop()
    for entry in scattering:
        settle(entry, last[1][3])

    small_keys = SMALL + ("odd_b_qkv", "odd_b_o")
    small_shapes = [G[k].shape for k in small_keys]
    small = _unpack(allreduce_small(_pack([G[k] for k in small_keys]), name="reduce_small"), small_shapes)
    grads = dict(zip(small_keys, small))
    grads["odd_b_qkv"] = lax.dynamic_slice(grads["odd_b_qkv"], (0, shard * bq), (n_odd, bq))
    grads["odd_b_o"] = lax.dynamic_slice(grads["odd_b_o"], (0, shard * bo), (n_odd, bo))

    deltas, new_m, new_v = {}, {}, {}

    def update(names):
        for k in names:
            w2 = (lambda a: a.reshape(1, -1)) if W[k].ndim == 1 else (lambda a: a)
            d, nm, nv, g = adamw(w2(W[k]), w2(M[k]), w2(V[k]), w2(grads[k]), name="adamw_" + k)
            deltas[k], new_m[k], new_v[k], grads[k] = (a.reshape(W[k].shape) for a in (d, nm, nv, g))

    late = [k for k in BIG if k in {name for name, _ in last[0]}]
    early = [k for k in BIG if k not in late]
    grads.update(zip(early, join_halves([reduced[k] for k in early], name="join_halves_early")))
    update([k for k in WEIGHTS if k not in late])
    settle(last, deltas[early[-1]])
    grads.update(zip(late, join_halves([reduced[k] for k in late], name="join_halves_late")))
    update(late)

    loss = lax.psum(loss[0, 0], ("x", "y", "c"))
    return (loss, dx[None], *[grads[k] for k in WEIGHTS], *[deltas[k] for k in WEIGHTS],
            *[new_m[k] for k in WEIGHTS], *[new_v[k] for k in WEIGHTS])
```
